```python
import jax
import jax.numpy as jnp
from jax import lax
import numpy as np

D_MODEL = 1024
BATCH = 8
SEQ = 8192
DEPTH = 2

CTX_LEN = 256
GRID_W = 64
N_BRANCH = 3
BRANCH_W = 512
MLA_HEADS = 8
MLA_NOPE = 64
MLA_ROPE = 32
MLA_QK = MLA_NOPE + MLA_ROPE
MLA_V = BRANCH_W // MLA_HEADS
MLA_Q_LORA = 256
MLA_KV_LORA = 128
GLA_HEADS = 4
GLA_DK = (D_MODEL // 2) // GLA_HEADS
GLA_DV = BRANCH_W // GLA_HEADS
GLA_GATE_RANK = 16
GLA_GATE_NORMALIZER = 16.0
RET_HEADS = 4
RET_DK = BRANCH_W // RET_HEADS
RET_DV = BRANCH_W // RET_HEADS
D_FF = 2816
CONV_W = 3
CHUNK = 64
Q_BLOCK = 128
ROPE_THETA = 10000.0
RET_THETA = 10000.0
EPS = 1e-6
F32 = jnp.float32

IN_LAYOUT = (
    ('mla_q', MLA_Q_LORA),
    ('mla_kv', MLA_KV_LORA),
    ('mla_kr', MLA_ROPE),
    ('gla_q', GLA_HEADS * GLA_DK),
    ('gla_k', GLA_HEADS * GLA_DK),
    ('gla_v', GLA_HEADS * GLA_DV),
    ('gla_g', GLA_HEADS * GLA_DV),
    ('gla_rf', GLA_GATE_RANK),
    ('gla_rb', GLA_GATE_RANK),
    ('ret_q', RET_HEADS * RET_DK),
    ('ret_k', RET_HEADS * RET_DK),
    ('ret_v', RET_HEADS * RET_DV),
    ('ret_g', RET_HEADS * RET_DV),
    ('gate_mla', D_MODEL),
    ('gate_gla', D_MODEL),
    ('gate_ret', D_MODEL),
)
N_IN = sum(width for _, width in IN_LAYOUT)
CTX_SIDE = ('mla_kv', 'mla_kr', 'gla_k', 'gla_v', 'gla_rf', 'gla_rb', 'ret_k', 'ret_v')
QUERY_SIDE = ('mla_q', 'gla_q', 'gla_g', 'ret_q', 'ret_g', 'gate_mla', 'gate_gla', 'gate_ret')

kernel_name = 'hybrid_mla_gla_retention_dit_block'


def rms_norm(x, w=None):
    x32 = x.astype(F32)
    y = x32 * lax.rsqrt(jnp.mean(x32 * x32, axis=-1, keepdims=True) + EPS)
    if w is not None:
        y = y * w.astype(F32)
    return y.astype(x.dtype)


def modulate(x, shift, scale):
    return x * (1 + scale) + shift


def split_heads(t, n_heads):
    b, s, _ = t.shape
    return t.reshape(b, s, n_heads, -1).transpose(0, 2, 1, 3)


def merge_heads(t):
    b, h, s, d = t.shape
    return t.transpose(0, 2, 1, 3).reshape(b, s, h * d)


def in_proj(a, w, names):
    out, start = {}, 0
    for name, width in IN_LAYOUT:
        if name in names:
            out[name] = a @ w[:, start:start + width]
        start += width
    return out


def rope_tables(pos, dim, theta):
    inv = theta ** (-jnp.arange(dim // 2, dtype=F32) * 2.0 / dim)
    ang = pos.astype(F32)[:, None] * inv[None, :]
    return jnp.cos(ang), jnp.sin(ang)


def retention_tables(pos):
    inv = 1.0 / (RET_THETA ** jnp.linspace(0.0, 1.0, RET_DK // 2, dtype=F32))
    ang = pos.astype(F32)[:, None] * inv[None, :]
    return jnp.cos(ang), jnp.sin(ang)


def rotate_half(x, cos, sin):
    n = x.shape[-1] // 2
    x1, x2 = x[..., :n], x[..., n:]
    return jnp.concatenate([x1 * cos - x2 * sin, x1 * sin + x2 * cos], axis=-1).astype(x.dtype)


def axial_rope(x, tabs):
    cos_r, sin_r, cos_c, sin_c = tabs
    half = x.shape[-1] // 2
    return jnp.concatenate([rotate_half(x[..., :half], cos_r, sin_r),
                            rotate_half(x[..., half:], cos_c, sin_c)], axis=-1)


def mla_rope(t, tabs):
    if tabs is None:
        return t
    return jnp.concatenate([t[..., :MLA_NOPE], axial_rope(t[..., MLA_NOPE:], tabs)], axis=-1)


def mla_queries(cq, q_norm_a, w_qb, q_norm, tabs):
    q = split_heads(rms_norm(cq, q_norm_a) @ w_qb, MLA_HEADS)
    return mla_rope(rms_norm(q, q_norm), tabs)


def mla_keys_values(ckv, kr, kv_norm_a, w_kvb, k_norm, tabs):
    kv = split_heads(rms_norm(ckv, kv_norm_a) @ w_kvb, MLA_HEADS)
    b, h, t, _ = kv.shape
    k_rope = jnp.broadcast_to(kr[:, None], (b, h, t, MLA_ROPE))
    k = rms_norm(jnp.concatenate([kv[..., :MLA_NOPE], k_rope], axis=-1), k_norm)
    return mla_rope(k, tabs), kv[..., MLA_NOPE:]


def attend(q, k, v):
    s = jnp.einsum('bhqd,bhkd->bhqk', q, k, preferred_element_type=F32) * (MLA_QK ** -0.5)
    p = jax.nn.softmax(s, axis=-1).astype(v.dtype)
    return jnp.einsum('bhqk,bhkd->bhqd', p, v)


def blocked_attend(q, k, v):
    b, h, s, d = q.shape
    qb = jnp.moveaxis(q.reshape(b, h, s // Q_BLOCK, Q_BLOCK, d), 2, 0)
    ob = lax.map(lambda qi: attend(qi, k, v), qb)
    return jnp.moveaxis(ob, 0, 2).reshape(b, h, s, v.shape[-1])


def chunk_mask(inclusive):
    idx = jnp.arange(CHUNK)
    return idx[:, None] >= idx[None, :] if inclusive else idx[:, None] > idx[None, :]


def gla_chunk_scan(q, k, v, log_a, s0, inclusive):
    b_, h, t, dk = k.shape
    dv = v.shape[-1]
    n = t // CHUNK
    kc = k.reshape(b_, h, n, CHUNK, dk).astype(F32)
    vc = v.reshape(b_, h, n, CHUNK, dv).astype(F32)
    cum = jnp.cumsum(log_a.reshape(b_, h, n, CHUNK, dk).astype(F32), axis=3)
    cum_last = cum[:, :, :, -1]
    inc = jnp.einsum('bhnjd,bhnjv->bhndv', kc * jnp.exp(cum_last[:, :, :, None] - cum), vc)

    def step(s, xs):
        decay, u = xs
        return decay[..., None] * s + u, s

    s_final, s_start = lax.scan(step, s0, (jnp.moveaxis(jnp.exp(cum_last), 2, 0), jnp.moveaxis(inc, 2, 0)))
    if q is None:
        return None, s_final
    s_start = jnp.moveaxis(s_start, 0, 2)
    q_dec = q.reshape(b_, h, n, CHUNK, dk).astype(F32) * jnp.exp(cum)
    att = jnp.einsum('bhnid,bhnjd->bhnij', q_dec, kc * jnp.exp(-cum))
    att = jnp.where(chunk_mask(inclusive), att, 0.0)
    o = jnp.einsum('bhnij,bhnjv->bhniv', att, vc) + jnp.einsum('bhnid,bhndv->bhniv', q_dec, s_start)
    return o.reshape(b_, h, t, dv).astype(v.dtype), s_final


def ret_chunk_scan(q, k, v, log_g, s0, inclusive):
    b_, h, t, dk = k.shape
    dv = v.shape[-1]
    n = t // CHUNK
    idx = jnp.arange(CHUNK, dtype=F32)
    lg = log_g.astype(F32)
    kc = k.reshape(b_, h, n, CHUNK, dk).astype(F32)
    vc = v.reshape(b_, h, n, CHUNK, dv).astype(F32)
    zeta = jnp.exp((CHUNK - 1 - idx)[None, :] * lg[:, None])
    inc = jnp.einsum('bhnjd,bhnjv->bhndv', kc * zeta[None, :, None, :, None], vc)
    g_chunk = jnp.exp(CHUNK * lg)[None, :, None, None]

    def step(s, u):
        return g_chunk * s + u, s

    s_final, s_start = lax.scan(step, s0, jnp.moveaxis(inc, 2, 0))
    if q is None:
        return None, s_final
    s_start = jnp.moveaxis(s_start, 0, 2)
    mask = chunk_mask(inclusive)
    rel = jnp.where(mask, idx[:, None] - idx[None, :], 0.0)
    dmat = jnp.where(mask[None], jnp.exp(rel[None] * lg[:, None, None]), 0.0)
    xi = jnp.exp((idx + 1.0)[None, :] * lg[:, None])
    qc = q.reshape(b_, h, n, CHUNK, dk).astype(F32)
    att = jnp.einsum('bhnid,bhnjd->bhnij', qc, kc) * dmat[None, :, None]
    o = (jnp.einsum('bhnij,bhnjv->bhniv', att, vc)
         + jnp.einsum('bhnid,bhndv->bhniv', qc, s_start) * xi[None, :, None, :, None])
    return o.reshape(b_, h, t, dv).astype(v.dtype), s_final


def scan_both_directions(chunk_fn, q, k, v, dec_f, dec_b, s_f, s_b, per_token_decay):
    rev = lambda t: None if t is None else jnp.flip(t, axis=2)
    o_f, s_f = chunk_fn(q, k, v, dec_f, s_f, True)
    o_b, s_b = chunk_fn(rev(q), rev(k), rev(v), rev(dec_b) if per_token_decay else dec_b, s_b, False)
    o = None if q is None else o_f + rev(o_b)
    return o, s_f, s_b


def gla_log_decay(r, w2, b):
    return jax.nn.log_sigmoid((r @ w2 + b).astype(F32)) / GLA_GATE_NORMALIZER


def gated_head_norm(o, g, w):
    return merge_heads(rms_norm(o, w)).astype(g.dtype) * jax.nn.silu(g)


def gated_merge(ys, gs, b_gate, w_branch, w_out):
    out = None
    for n in range(N_BRANCH):
        term = jax.nn.sigmoid(gs[n] + b_gate[n]) * (ys[n] @ w_branch[n])
        out = term if out is None else out + term
    return out @ w_out


def token_mixers(a, ac, need_ctx, lat_tabs, ret_lat_tabs, ret_ctx_tabs, w_in, b_gate,
                 mla_q_norm_a, mla_w_qb, mla_kv_norm_a, mla_w_kvb, mla_q_norm, mla_k_norm,
                 gla_w_gk2, gla_b_gk, gla_o_norm, ret_decay, w_branch, w_out):
    bsz = a.shape[0]
    p = in_proj(a, w_in, CTX_SIDE + QUERY_SIDE)
    pc = in_proj(ac, w_in, CTX_SIDE + QUERY_SIDE if need_ctx else CTX_SIDE)

    k_c, v_c = mla_keys_values(pc['mla_kv'], pc['mla_kr'], mla_kv_norm_a, mla_w_kvb, mla_k_norm, None)
    k_l, v_l = mla_keys_values(p['mla_kv'], p['mla_kr'], mla_kv_norm_a, mla_w_kvb, mla_k_norm, lat_tabs)
    q_l = mla_queries(p['mla_q'], mla_q_norm_a, mla_w_qb, mla_q_norm, lat_tabs)
    y_mla = merge_heads(blocked_attend(q_l, jnp.concatenate([k_c, k_l], axis=2),
                                       jnp.concatenate([v_c, v_l], axis=2)))

    def gla_inputs(z, with_q):
        q = split_heads(z['gla_q'], GLA_HEADS) * (GLA_DK ** -0.5) if with_q else None
        k = split_heads(z['gla_k'], GLA_HEADS)
        v = split_heads(z['gla_v'], GLA_HEADS)
        la_f = split_heads(gla_log_decay(z['gla_rf'], gla_w_gk2[0], gla_b_gk[0]), GLA_HEADS)
        la_b = split_heads(gla_log_decay(z['gla_rb'], gla_w_gk2[1], gla_b_gk[1]), GLA_HEADS)
        return q, k, v, la_f, la_b

    zg = jnp.zeros((bsz, GLA_HEADS, GLA_DK, GLA_DV), F32)
    o_gc, sg_f, sg_b = scan_both_directions(gla_chunk_scan, *gla_inputs(pc, need_ctx), zg, zg, True)
    o_gl, _, _ = scan_both_directions(gla_chunk_scan, *gla_inputs(p, True), sg_f, sg_b, True)
    y_gla = gated_head_norm(o_gl, p['gla_g'], gla_o_norm)

    log_g = -jnp.exp(ret_decay.astype(F32))

    def ret_inputs(z, tabs, with_q):
        q = rotate_half(split_heads(z['ret_q'], RET_HEADS), *tabs) if with_q else None
        k = rotate_half(split_heads(z['ret_k'], RET_HEADS), *tabs) * (RET_DK ** -0.5)
        v = split_heads(z['ret_v'], RET_HEADS)
        return q, k, v

    zr = jnp.zeros((bsz, RET_HEADS, RET_DK, RET_DV), F32)
    o_rc, sr_f, sr_b = scan_both_directions(ret_chunk_scan, *ret_inputs(pc, ret_ctx_tabs, need_ctx),
                                            log_g[0], log_g[1], zr, zr, False)
    o_rl, _, _ = scan_both_directions(ret_chunk_scan, *ret_inputs(p, ret_lat_tabs, True),
                                      log_g[0], log_g[1], sr_f, sr_b, False)
    y_ret = gated_head_norm(o_rl, p['ret_g'], None)

    y = gated_merge((y_mla, y_gla, y_ret), (p['gate_mla'], p['gate_gla'], p['gate_ret']),
                    b_gate, w_branch, w_out)
    if not need_ctx:
        return y, None
    q_c = mla_queries(pc['mla_q'], mla_q_norm_a, mla_w_qb, mla_q_norm, None)
    y_c = gated_merge((merge_heads(attend(q_c, k_c, v_c)),
                       gated_head_norm(o_gc, pc['gla_g'], gla_o_norm),
                       gated_head_norm(o_rc, pc['ret_g'], None)),
                      (pc['gate_mla'], pc['gate_gla'], pc['gate_ret']), b_gate, w_branch, w_out)
    return y, y_c


def conv_ffn(a, w_in, w_dw, b_dw, w_out):
    gate = a @ w_in[:, :D_FF]
    up = a @ w_in[:, D_FF:]
    gate = lax.conv_general_dilated(gate, w_dw[:, None, :], window_strides=(1,),
                                    padding=[(CONV_W // 2, CONV_W // 2)],
                                    dimension_numbers=('NWC', 'WIO', 'NWC'),
                                    feature_group_count=D_FF) + b_dw
    return (jax.nn.gelu(gate) * up) @ w_out


def _fwd_setup_inputs(seed: int = 0) -> dict:
    key = jax.random.key(seed)
    ks = iter(jax.random.split(key, 32))

    def nrm(shape, scale):
        return scale * jax.random.normal(next(ks), shape, F32)

    def gain(shape):
        return 1.0 + nrm(shape, 0.02)

    L, D = DEPTH, D_MODEL
    ret_base = jnp.log(-jnp.log1p(-(2.0 ** (-5.0 - jnp.arange(RET_HEADS, dtype=F32)))))
    return {
        'x': nrm((BATCH, SEQ, D), 1.0),
        'c': nrm((BATCH, D), 1.0),
        'ctx': nrm((BATCH, CTX_LEN, D), 1.0),
        'c_ctx': nrm((D,), 1.0),
        'w_ada': nrm((L, D, 6 * D), 0.5 * D ** -0.5),
        'b_ada': nrm((L, 6 * D), 0.02),
        'norm1_w': gain((L, D)),
        'norm2_w': gain((L, D)),
        'w_in': nrm((L, D, N_IN), D ** -0.5),
        'b_gate': nrm((L, N_BRANCH, D), 0.02),
        'mla_q_norm_a': gain((L, MLA_Q_LORA)),
        'mla_w_qb': nrm((L, MLA_Q_LORA, MLA_HEADS * MLA_QK), MLA_Q_LORA ** -0.5),
        'mla_kv_norm_a': gain((L, MLA_KV_LORA)),
        'mla_w_kvb': nrm((L, MLA_KV_LORA, MLA_HEADS * (MLA_NOPE + MLA_V)), MLA_KV_LORA ** -0.5),
        'mla_q_norm': gain((L, MLA_QK)),
        'mla_k_norm': gain((L, MLA_QK)),
        'gla_w_gk2': nrm((L, 2, GLA_GATE_RANK, GLA_HEADS * GLA_DK), GLA_GATE_RANK ** -0.5),
        'gla_b_gk': nrm((L, 2, GLA_HEADS * GLA_DK), 0.1),
        'gla_o_norm': gain((L, GLA_DV)),
        'ret_decay': ret_base + nrm((L, 2, RET_HEADS), 0.01),
        'w_branch': nrm((L, N_BRANCH, BRANCH_W, D), BRANCH_W ** -0.5),
        'w_out': nrm((L, D, D), D ** -0.5),
        'w_ffn_in': nrm((L, D, 2 * D_FF), D ** -0.5),
        'w_dw': nrm((L, CONV_W, D_FF), CONV_W ** -0.5),
        'b_dw': nrm((L, D_FF), 0.02),
        'w_ffn_out': nrm((L, D_FF, D), D_FF ** -0.5),
    }


def _fwd_reference(x, c, ctx, c_ctx, w_ada, b_ada, norm1_w, norm2_w, w_in, b_gate,
              mla_q_norm_a, mla_w_qb, mla_kv_norm_a, mla_w_kvb, mla_q_norm, mla_k_norm,
              gla_w_gk2, gla_b_gk, gla_o_norm, ret_decay, w_branch, w_out,
              w_ffn_in, w_dw, b_dw, w_ffn_out):
    seq = x.shape[1]
    ctx_len = ctx.shape[1]
    rows = seq // GRID_W
    row_pos = jnp.repeat(jnp.arange(rows), GRID_W)
    col_pos = jnp.tile(jnp.arange(GRID_W), rows)
    cos_r, sin_r = rope_tables(row_pos, MLA_ROPE // 2, ROPE_THETA)
    cos_c, sin_c = rope_tables(col_pos, MLA_ROPE // 2, ROPE_THETA)
    lat_tabs = (cos_r, sin_r, cos_c, sin_c)
    ret_ctx_tabs = retention_tables(jnp.arange(ctx_len))
    ret_lat_tabs = retention_tables(ctx_len + jnp.arange(seq))
    cond = jax.nn.silu(c)
    cond_c = jax.nn.silu(c_ctx)
    h, hc = x, ctx
    for l in range(DEPTH):
        need_ctx = l < DEPTH - 1
        mod = jnp.split((cond @ w_ada[l] + b_ada[l])[:, None, :], 6, axis=-1)
        mod_c = jnp.split(cond_c @ w_ada[l] + b_ada[l], 6, axis=-1)
        a = modulate(rms_norm(h, norm1_w[l]), mod[0], mod[1])
        ac = modulate(rms_norm(hc, norm1_w[l]), mod_c[0], mod_c[1])
        y, y_c = token_mixers(a, ac, need_ctx, lat_tabs, ret_lat_tabs, ret_ctx_tabs, w_in[l], b_gate[l],
                              mla_q_norm_a[l], mla_w_qb[l], mla_kv_norm_a[l], mla_w_kvb[l],
                              mla_q_norm[l], mla_k_norm[l], gla_w_gk2[l], gla_b_gk[l], gla_o_norm[l],
                              ret_decay[l], w_branch[l], w_out[l])
        h = h + mod[2] * y
        h = h + mod[5] * conv_ffn(modulate(rms_norm(h, norm2_w[l]), mod[3], mod[4]),
                                  w_ffn_in[l], w_dw[l], b_dw[l], w_ffn_out[l])
        if need_ctx:
            hc = hc + mod_c[2] * y_c
            hc = hc + mod_c[5] * conv_ffn(modulate(rms_norm(hc, norm2_w[l]), mod_c[3], mod_c[4]),
                                          w_ffn_in[l], w_dw[l], b_dw[l], w_ffn_out[l])
    return h


import jax as _jax
import jax.numpy as _jnp

TWIN_FORMAT = 'train_step'
FWD_PARAMS = ['x', 'c', 'ctx', 'c_ctx', 'w_ada', 'b_ada', 'norm1_w', 'norm2_w', 'w_in', 'b_gate', 'mla_q_norm_a', 'mla_w_qb', 'mla_kv_norm_a', 'mla_w_kvb', 'mla_q_norm', 'mla_k_norm', 'gla_w_gk2', 'gla_b_gk', 'gla_o_norm', 'ret_decay', 'w_branch', 'w_out', 'w_ffn_in', 'w_dw', 'b_dw', 'w_ffn_out']
TWIN_WEIGHTS = ['c_ctx', 'w_ada', 'b_ada', 'norm1_w', 'norm2_w', 'w_in', 'b_gate', 'mla_q_norm_a', 'mla_w_qb', 'mla_kv_norm_a', 'mla_w_kvb', 'mla_q_norm', 'mla_k_norm', 'gla_w_gk2', 'gla_b_gk', 'gla_o_norm', 'ret_decay', 'w_branch', 'w_out', 'w_ffn_in', 'w_dw', 'b_dw', 'w_ffn_out']
TWIN_DIFF_INPUT = 'x'
TWIN_INPUTS = ['x', 'c', 'ctx', 'c_ctx', 'w_ada', 'b_ada', 'norm1_w', 'norm2_w', 'w_in', 'b_gate', 'mla_q_norm_a', 'mla_w_qb', 'mla_kv_norm_a', 'mla_w_kvb', 'mla_q_norm', 'mla_k_norm', 'gla_w_gk2', 'gla_b_gk', 'gla_o_norm', 'ret_decay', 'w_branch', 'w_out', 'w_ffn_in', 'w_dw', 'b_dw', 'w_ffn_out', 'loss_target', 'm_c_ctx', 'm_w_ada', 'm_b_ada', 'm_norm1_w', 'm_norm2_w', 'm_w_in', 'm_b_gate', 'm_mla_q_norm_a', 'm_mla_w_qb', 'm_mla_kv_norm_a', 'm_mla_w_kvb', 'm_mla_q_norm', 'm_mla_k_norm', 'm_gla_w_gk2', 'm_gla_b_gk', 'm_gla_o_norm', 'm_ret_decay', 'm_w_branch', 'm_w_out', 'm_w_ffn_in', 'm_w_dw', 'm_b_dw', 'm_w_ffn_out', 'v_c_ctx', 'v_w_ada', 'v_b_ada', 'v_norm1_w', 'v_norm2_w', 'v_w_in', 'v_b_gate', 'v_mla_q_norm_a', 'v_mla_w_qb', 'v_mla_kv_norm_a', 'v_mla_w_kvb', 'v_mla_q_norm', 'v_mla_k_norm', 'v_gla_w_gk2', 'v_gla_b_gk', 'v_gla_o_norm', 'v_ret_decay', 'v_w_branch', 'v_w_out', 'v_w_ffn_in', 'v_w_dw', 'v_b_dw', 'v_w_ffn_out']
TWIN_OUTPUTS = ['loss', 'grad_x', 'grad_c_ctx', 'grad_w_ada', 'grad_b_ada', 'grad_norm1_w', 'grad_norm2_w', 'grad_w_in', 'grad_b_gate', 'grad_mla_q_norm_a', 'grad_mla_w_qb', 'grad_mla_kv_norm_a', 'grad_mla_w_kvb', 'grad_mla_q_norm', 'grad_mla_k_norm', 'grad_gla_w_gk2', 'grad_gla_b_gk', 'grad_gla_o_norm', 'grad_ret_decay', 'grad_w_branch', 'grad_w_out', 'grad_w_ffn_in', 'grad_w_dw', 'grad_b_dw', 'grad_w_ffn_out', 'delta_c_ctx', 'delta_w_ada', 'delta_b_ada', 'delta_norm1_w', 'delta_norm2_w', 'delta_w_in', 'delta_b_gate', 'delta_mla_q_norm_a', 'delta_mla_w_qb', 'delta_mla_kv_norm_a', 'delta_mla_w_kvb', 'delta_mla_q_norm', 'delta_mla_k_norm', 'delta_gla_w_gk2', 'delta_gla_b_gk', 'delta_gla_o_norm', 'delta_ret_decay', 'delta_w_branch', 'delta_w_out', 'delta_w_ffn_in', 'delta_w_dw', 'delta_b_dw', 'delta_w_ffn_out', 'new_m_c_ctx', 'new_m_w_ada', 'new_m_b_ada', 'new_m_norm1_w', 'new_m_norm2_w', 'new_m_w_in', 'new_m_b_gate', 'new_m_mla_q_norm_a', 'new_m_mla_w_qb', 'new_m_mla_kv_norm_a', 'new_m_mla_w_kvb', 'new_m_mla_q_norm', 'new_m_mla_k_norm', 'new_m_gla_w_gk2', 'new_m_gla_b_gk', 'new_m_gla_o_norm', 'new_m_ret_decay', 'new_m_w_branch', 'new_m_w_out', 'new_m_w_ffn_in', 'new_m_w_dw', 'new_m_b_dw', 'new_m_w_ffn_out', 'new_v_c_ctx', 'new_v_w_ada', 'new_v_b_ada', 'new_v_norm1_w', 'new_v_norm2_w', 'new_v_w_in', 'new_v_b_gate', 'new_v_mla_q_norm_a', 'new_v_mla_w_qb', 'new_v_mla_kv_norm_a', 'new_v_mla_w_kvb', 'new_v_mla_q_norm', 'new_v_mla_k_norm', 'new_v_gla_w_gk2', 'new_v_gla_b_gk', 'new_v_gla_o_norm', 'new_v_ret_decay', 'new_v_w_branch', 'new_v_w_out', 'new_v_w_ffn_in', 'new_v_w_dw', 'new_v_b_dw', 'new_v_w_ffn_out']
TWIN_LEAF_KINDS = {'loss': 'loss', 'grad_x': 'grad_x', 'grad_c_ctx': 'grad_w', 'grad_w_ada': 'grad_w', 'grad_b_ada': 'grad_w', 'grad_norm1_w': 'grad_w', 'grad_norm2_w': 'grad_w', 'grad_w_in': 'grad_w', 'grad_b_gate': 'grad_w', 'grad_mla_q_norm_a': 'grad_w', 'grad_mla_w_qb': 'grad_w', 'grad_mla_kv_norm_a': 'grad_w', 'grad_mla_w_kvb': 'grad_w', 'grad_mla_q_norm': 'grad_w', 'grad_mla_k_norm': 'grad_w', 'grad_gla_w_gk2': 'grad_w', 'grad_gla_b_gk': 'grad_w', 'grad_gla_o_norm': 'grad_w', 'grad_ret_decay': 'grad_w', 'grad_w_branch': 'grad_w', 'grad_w_out': 'grad_w', 'grad_w_ffn_in': 'grad_w', 'grad_w_dw': 'grad_w', 'grad_b_dw': 'grad_w', 'grad_w_ffn_out': 'grad_w', 'delta_c_ctx': 'delta_w', 'delta_w_ada': 'delta_w', 'delta_b_ada': 'delta_w', 'delta_norm1_w': 'delta_w', 'delta_norm2_w': 'delta_w', 'delta_w_in': 'delta_w', 'delta_b_gate': 'delta_w', 'delta_mla_q_norm_a': 'delta_w', 'delta_mla_w_qb': 'delta_w', 'delta_mla_kv_norm_a': 'delta_w', 'delta_mla_w_kvb': 'delta_w', 'delta_mla_q_norm': 'delta_w', 'delta_mla_k_norm': 'delta_w', 'delta_gla_w_gk2': 'delta_w', 'delta_gla_b_gk': 'delta_w', 'delta_gla_o_norm': 'delta_w', 'delta_ret_decay': 'delta_w', 'delta_w_branch': 'delta_w', 'delta_w_out': 'delta_w', 'delta_w_ffn_in': 'delta_w', 'delta_w_dw': 'delta_w', 'delta_b_dw': 'delta_w', 'delta_w_ffn_out': 'delta_w', 'new_m_c_ctx': 'new_m', 'new_m_w_ada': 'new_m', 'new_m_b_ada': 'new_m', 'new_m_norm1_w': 'new_m', 'new_m_norm2_w': 'new_m', 'new_m_w_in': 'new_m', 'new_m_b_gate': 'new_m', 'new_m_mla_q_norm_a': 'new_m', 'new_m_mla_w_qb': 'new_m', 'new_m_mla_kv_norm_a': 'new_m', 'new_m_mla_w_kvb': 'new_m', 'new_m_mla_q_norm': 'new_m', 'new_m_mla_k_norm': 'new_m', 'new_m_gla_w_gk2': 'new_m', 'new_m_gla_b_gk': 'new_m', 'new_m_gla_o_norm': 'new_m', 'new_m_ret_decay': 'new_m', 'new_m_w_branch': 'new_m', 'new_m_w_out': 'new_m', 'new_m_w_ffn_in': 'new_m', 'new_m_w_dw': 'new_m', 'new_m_b_dw': 'new_m', 'new_m_w_ffn_out': 'new_m', 'new_v_c_ctx': 'new_v', 'new_v_w_ada': 'new_v', 'new_v_b_ada': 'new_v', 'new_v_norm1_w': 'new_v', 'new_v_norm2_w': 'new_v', 'new_v_w_in': 'new_v', 'new_v_b_gate': 'new_v', 'new_v_mla_q_norm_a': 'new_v', 'new_v_mla_w_qb': 'new_v', 'new_v_mla_kv_norm_a': 'new_v', 'new_v_mla_w_kvb': 'new_v', 'new_v_mla_q_norm': 'new_v', 'new_v_mla_k_norm': 'new_v', 'new_v_gla_w_gk2': 'new_v', 'new_v_gla_b_gk': 'new_v', 'new_v_gla_o_norm': 'new_v', 'new_v_ret_decay': 'new_v', 'new_v_w_branch': 'new_v', 'new_v_w_out': 'new_v', 'new_v_w_ffn_in': 'new_v', 'new_v_w_dw': 'new_v', 'new_v_b_dw': 'new_v', 'new_v_w_ffn_out': 'new_v'}


def _forward(args):
    return _fwd_reference(*[args[k] for k in FWD_PARAMS])


def _output_shape():
    def fwd():
        inp = _fwd_setup_inputs(0)
        return _fwd_reference(*[inp[k] for k in FWD_PARAMS])
    out = _jax.eval_shape(fwd)
    return out.shape, out.dtype

N_MICROBATCH = 1
ADAM_LR = 0.001
ADAM_B1 = 0.9
ADAM_B2 = 0.999
ADAM_EPS = 1e-08
ADAM_WD = 0.01
ADAM_STEP = 10
PER_EXAMPLE_BATCH_AXIS = {'x': 0, 'c': 0, 'ctx': 0, 'loss_target': 0}
SHARED_INPUTS = []
_WEIGHT_DTYPES = {'c_ctx': _jnp.float32, 'w_ada': _jnp.float32, 'b_ada': _jnp.float32, 'norm1_w': _jnp.float32, 'norm2_w': _jnp.float32, 'w_in': _jnp.float32, 'b_gate': _jnp.float32, 'mla_q_norm_a': _jnp.float32, 'mla_w_qb': _jnp.float32, 'mla_kv_norm_a': _jnp.float32, 'mla_w_kvb': _jnp.float32, 'mla_q_norm': _jnp.float32, 'mla_k_norm': _jnp.float32, 'gla_w_gk2': _jnp.float32, 'gla_b_gk': _jnp.float32, 'gla_o_norm': _jnp.float32, 'ret_decay': _jnp.float32, 'w_branch': _jnp.float32, 'w_out': _jnp.float32, 'w_ffn_in': _jnp.float32, 'w_dw': _jnp.float32, 'b_dw': _jnp.float32, 'w_ffn_out': _jnp.float32}
MOMENT_SCALE = {'c_ctx': 5.205665e-02, 'w_ada': 1.464071e+00, 'b_ada': 3.983660e+00, 'norm1_w': 1.615487e+00, 'norm2_w': 7.483640e+00, 'w_in': 7.947803e-02, 'b_gate': 2.013054e-01, 'mla_q_norm_a': 1.993333e-02, 'mla_w_qb': 1.137425e-02, 'mla_kv_norm_a': 1.024352e+00, 'mla_w_kvb': 1.061072e-01, 'mla_q_norm': 7.184980e-02, 'mla_k_norm': 7.210478e-02, 'gla_w_gk2': 8.936340e-03, 'gla_b_gk': 2.345496e-02, 'gla_o_norm': 6.088245e+00, 'ret_decay': 6.443602e-01, 'w_branch': 6.124153e-02, 'w_out': 1.031840e-01, 'w_ffn_in': 1.365099e-01, 'w_dw': 8.580770e-01, 'b_dw': 1.009862e+00, 'w_ffn_out': 1.451239e-01}


def _to_microbatches(a, axis):
    t = _jnp.moveaxis(a, axis, 0)
    t = t.reshape((N_MICROBATCH, t.shape[0] // N_MICROBATCH) + t.shape[1:])
    return _jnp.moveaxis(t, 1, axis + 1)


def setup_inputs(seed: int = 0) -> dict:
    inp = _fwd_setup_inputs(seed)
    key = _jax.random.fold_in(_jax.random.key(seed), 7919)
    shape, _ = _output_shape()
    out = dict(inp)
    out["loss_target"] = _jax.random.normal(_jax.random.fold_in(key, 0), shape, _jnp.float32)
    for i, name in enumerate(TWIN_WEIGHTS):
        w = inp[name].astype(_jnp.float32)
        if MOMENT_SCALE is None:
            s = _jnp.sqrt(_jnp.mean(_jnp.square(w)) + 1e-30)
        else:
            s = MOMENT_SCALE[name]
        km, kv = _jax.random.split(_jax.random.fold_in(key, i + 1))
        out[name] = w
        out["m_" + name] = s * _jax.random.normal(km, w.shape, _jnp.float32)
        out["v_" + name] = (s * s) * _jax.random.uniform(kv, w.shape, _jnp.float32, 0.5, 1.5)
    if N_MICROBATCH > 1:
        for name, axis in PER_EXAMPLE_BATCH_AXIS.items():
            out[name] = _to_microbatches(out[name], axis)
    return {'x': out['x'], 'c': out['c'], 'ctx': out['ctx'], 'c_ctx': out['c_ctx'], 'w_ada': out['w_ada'], 'b_ada': out['b_ada'], 'norm1_w': out['norm1_w'], 'norm2_w': out['norm2_w'], 'w_in': out['w_in'], 'b_gate': out['b_gate'], 'mla_q_norm_a': out['mla_q_norm_a'], 'mla_w_qb': out['mla_w_qb'], 'mla_kv_norm_a': out['mla_kv_norm_a'], 'mla_w_kvb': out['mla_w_kvb'], 'mla_q_norm': out['mla_q_norm'], 'mla_k_norm': out['mla_k_norm'], 'gla_w_gk2': out['gla_w_gk2'], 'gla_b_gk': out['gla_b_gk'], 'gla_o_norm': out['gla_o_norm'], 'ret_decay': out['ret_decay'], 'w_branch': out['w_branch'], 'w_out': out['w_out'], 'w_ffn_in': out['w_ffn_in'], 'w_dw': out['w_dw'], 'b_dw': out['b_dw'], 'w_ffn_out': out['w_ffn_out'], 'loss_target': out['loss_target'], 'm_c_ctx': out['m_c_ctx'], 'm_w_ada': out['m_w_ada'], 'm_b_ada': out['m_b_ada'], 'm_norm1_w': out['m_norm1_w'], 'm_norm2_w': out['m_norm2_w'], 'm_w_in': out['m_w_in'], 'm_b_gate': out['m_b_gate'], 'm_mla_q_norm_a': out['m_mla_q_norm_a'], 'm_mla_w_qb': out['m_mla_w_qb'], 'm_mla_kv_norm_a': out['m_mla_kv_norm_a'], 'm_mla_w_kvb': out['m_mla_w_kvb'], 'm_mla_q_norm': out['m_mla_q_norm'], 'm_mla_k_norm': out['m_mla_k_norm'], 'm_gla_w_gk2': out['m_gla_w_gk2'], 'm_gla_b_gk': out['m_gla_b_gk'], 'm_gla_o_norm': out['m_gla_o_norm'], 'm_ret_decay': out['m_ret_decay'], 'm_w_branch': out['m_w_branch'], 'm_w_out': out['m_w_out'], 'm_w_ffn_in': out['m_w_ffn_in'], 'm_w_dw': out['m_w_dw'], 'm_b_dw': out['m_b_dw'], 'm_w_ffn_out': out['m_w_ffn_out'], 'v_c_ctx': out['v_c_ctx'], 'v_w_ada': out['v_w_ada'], 'v_b_ada': out['v_b_ada'], 'v_norm1_w': out['v_norm1_w'], 'v_norm2_w': out['v_norm2_w'], 'v_w_in': out['v_w_in'], 'v_b_gate': out['v_b_gate'], 'v_mla_q_norm_a': out['v_mla_q_norm_a'], 'v_mla_w_qb': out['v_mla_w_qb'], 'v_mla_kv_norm_a': out['v_mla_kv_norm_a'], 'v_mla_w_kvb': out['v_mla_w_kvb'], 'v_mla_q_norm': out['v_mla_q_norm'], 'v_mla_k_norm': out['v_mla_k_norm'], 'v_gla_w_gk2': out['v_gla_w_gk2'], 'v_gla_b_gk': out['v_gla_b_gk'], 'v_gla_o_norm': out['v_gla_o_norm'], 'v_ret_decay': out['v_ret_decay'], 'v_w_branch': out['v_w_branch'], 'v_w_out': out['v_w_out'], 'v_w_ffn_in': out['v_w_ffn_in'], 'v_w_dw': out['v_w_dw'], 'v_b_dw': out['v_b_dw'], 'v_w_ffn_out': out['v_w_ffn_out']}


def _loss(weights, diff, rest, loss_target):
    with _jax.named_scope("forward"):
        args = {**rest, TWIN_DIFF_INPUT: diff, **{k: w.astype(_WEIGHT_DTYPES[k]) for k, w in weights.items()}}
        y = _forward(args)
    with _jax.named_scope("loss_head"):
        err = _jnp.square(y.astype(_jnp.float32) - loss_target)
        return 0.5 * _jnp.sum(_jnp.mean(err, axis=-1)) if err.ndim else 0.5 * err


def _adamw(w, g, m, v):
    m = ADAM_B1 * m + (1.0 - ADAM_B1) * g
    v = ADAM_B2 * v + (1.0 - ADAM_B2) * _jnp.square(g)
    m_hat = m / (1.0 - ADAM_B1 ** ADAM_STEP)
    v_hat = v / (1.0 - ADAM_B2 ** ADAM_STEP)
    delta = -ADAM_LR * (m_hat / (_jnp.sqrt(v_hat) + ADAM_EPS) + ADAM_WD * w)
    return delta, m, v


def reference(x, c, ctx, c_ctx, w_ada, b_ada, norm1_w, norm2_w, w_in, b_gate, mla_q_norm_a, mla_w_qb, mla_kv_norm_a, mla_w_kvb, mla_q_norm, mla_k_norm, gla_w_gk2, gla_b_gk, gla_o_norm, ret_decay, w_branch, w_out, w_ffn_in, w_dw, b_dw, w_ffn_out, loss_target, m_c_ctx, m_w_ada, m_b_ada, m_norm1_w, m_norm2_w, m_w_in, m_b_gate, m_mla_q_norm_a, m_mla_w_qb, m_mla_kv_norm_a, m_mla_w_kvb, m_mla_q_norm, m_mla_k_norm, m_gla_w_gk2, m_gla_b_gk, m_gla_o_norm, m_ret_decay, m_w_branch, m_w_out, m_w_ffn_in, m_w_dw, m_b_dw, m_w_ffn_out, v_c_ctx, v_w_ada, v_b_ada, v_norm1_w, v_norm2_w, v_w_in, v_b_gate, v_mla_q_norm_a, v_mla_w_qb, v_mla_kv_norm_a, v_mla_w_kvb, v_mla_q_norm, v_mla_k_norm, v_gla_w_gk2, v_gla_b_gk, v_gla_o_norm, v_ret_decay, v_w_branch, v_w_out, v_w_ffn_in, v_w_dw, v_b_dw, v_w_ffn_out):
    given = dict(x=x, c=c, ctx=ctx, c_ctx=c_ctx, w_ada=w_ada, b_ada=b_ada, norm1_w=norm1_w, norm2_w=norm2_w, w_in=w_in, b_gate=b_gate, mla_q_norm_a=mla_q_norm_a, mla_w_qb=mla_w_qb, mla_kv_norm_a=mla_kv_norm_a, mla_w_kvb=mla_w_kvb, mla_q_norm=mla_q_norm, mla_k_norm=mla_k_norm, gla_w_gk2=gla_w_gk2, gla_b_gk=gla_b_gk, gla_o_norm=gla_o_norm, ret_decay=ret_decay, w_branch=w_branch, w_out=w_out, w_ffn_in=w_ffn_in, w_dw=w_dw, b_dw=b_dw, w_ffn_out=w_ffn_out, loss_target=loss_target, m_c_ctx=m_c_ctx, m_w_ada=m_w_ada, m_b_ada=m_b_ada, m_norm1_w=m_norm1_w, m_norm2_w=m_norm2_w, m_w_in=m_w_in, m_b_gate=m_b_gate, m_mla_q_norm_a=m_mla_q_norm_a, m_mla_w_qb=m_mla_w_qb, m_mla_kv_norm_a=m_mla_kv_norm_a, m_mla_w_kvb=m_mla_w_kvb, m_mla_q_norm=m_mla_q_norm, m_mla_k_norm=m_mla_k_norm, m_gla_w_gk2=m_gla_w_gk2, m_gla_b_gk=m_gla_b_gk, m_gla_o_norm=m_gla_o_norm, m_ret_decay=m_ret_decay, m_w_branch=m_w_branch, m_w_out=m_w_out, m_w_ffn_in=m_w_ffn_in, m_w_dw=m_w_dw, m_b_dw=m_b_dw, m_w_ffn_out=m_w_ffn_out, v_c_ctx=v_c_ctx, v_w_ada=v_w_ada, v_b_ada=v_b_ada, v_norm1_w=v_norm1_w, v_norm2_w=v_norm2_w, v_w_in=v_w_in, v_b_gate=v_b_gate, v_mla_q_norm_a=v_mla_q_norm_a, v_mla_w_qb=v_mla_w_qb, v_mla_kv_norm_a=v_mla_kv_norm_a, v_mla_w_kvb=v_mla_w_kvb, v_mla_q_norm=v_mla_q_norm, v_mla_k_norm=v_mla_k_norm, v_gla_w_gk2=v_gla_w_gk2, v_gla_b_gk=v_gla_b_gk, v_gla_o_norm=v_gla_o_norm, v_ret_decay=v_ret_decay, v_w_branch=v_w_branch, v_w_out=v_w_out, v_w_ffn_in=v_w_ffn_in, v_w_dw=v_w_dw, v_b_dw=v_b_dw, v_w_ffn_out=v_w_ffn_out)
    weights = {n: given[n] for n in TWIN_WEIGHTS}
    shared = {n: given[n] for n in SHARED_INPUTS}
    per_example = {n: given[n] for n in ['x', 'c', 'ctx']}
    grad_fn = _jax.value_and_grad(_loss, argnums=(0, 1))

    def one_microbatch(ex, loss_target):
        ex = dict(ex)
        diff = ex.pop(TWIN_DIFF_INPUT)
        return grad_fn(weights, diff, {**shared, **ex}, loss_target)

    if N_MICROBATCH == 1:
        loss, (grad_w, grad_x) = one_microbatch(per_example, given["loss_target"])
    else:
        def body(carry, xs):
            loss_sum, grad_sum = carry
            l_k, (gw_k, gx_k) = one_microbatch(xs[0], xs[1])
            with _jax.named_scope("update"):
                return (loss_sum + l_k, _jax.tree.map(_jnp.add, grad_sum, gw_k)), gx_k

        init = (_jnp.zeros((), _jnp.float32), _jax.tree.map(_jnp.zeros_like, weights))
        (loss, grad_w), grad_x = _jax.lax.scan(body, init, (per_example, given["loss_target"]))
    with _jax.named_scope("update"):
        delta_w, new_m, new_v = {}, {}, {}
        for n in TWIN_WEIGHTS:
            delta_w[n], new_m[n], new_v[n] = _adamw(weights[n], grad_w[n], given["m_" + n], given["v_" + n])
    return (loss, grad_x, *[grad_w[n] for n in TWIN_WEIGHTS], *[delta_w[n] for n in TWIN_WEIGHTS],
            *[new_m[n] for n in TWIN_WEIGHTS], *[new_v[n] for n in TWIN_WEIGHTS])
```

```python
import functools
import math

import jax
import jax.numpy as jnp
import numpy as np
from jax import lax
from jax.experimental import pallas as pl
from jax.experimental.pallas import tpu as pltpu

F32 = jnp.float32
BF16 = jnp.bfloat16

N_DEV = 8
D = 1024
DEPTH = 2
GRID_W = 64
MLA_HEADS = 8
MLA_NOPE = 64
MLA_ROPE = 32
MLA_QK = 96
MLA_V = 64
MLA_Q_LORA = 256
MLA_KV_LORA = 128
GLA_HEADS = 4
GLA_DK = 128
GLA_RANK = 16
GLA_NORMALIZER = 16.0
RET_HEADS = 4
RET_DK = 128
BRANCH_W = 512
D_FF = 2816
CHUNK = 64
ROPE_THETA = 10000.0
RET_THETA = 10000.0
EPS = 1e-6
HEAD_PAD = 128
N_IN_PAD = 8192

ADAM_LR = 0.001
ADAM_B1 = 0.9
ADAM_B2 = 0.999
ADAM_EPS = 1e-08
ADAM_WD = 0.01
ADAM_STEP = 10

ROW_TILE = 256
SCAN_CHUNKS = ROW_TILE // CHUNK
VMEM_LIMIT_BYTES = 56 * 1024 * 1024
MESH = pl.DeviceIdType.MESH


def _cparams(n_axes):
    return pltpu.CompilerParams(dimension_semantics=("arbitrary",) * n_axes, vmem_limit_bytes=VMEM_LIMIT_BYTES)


def _pick(dim, cands):
    for cand in cands:
        if dim % cand == 0:
            return cand
    return dim


_DOT_DIMS = {"nn": (((1,), (0,)), ((), ())), "nt": (((1,), (1,)), ((), ())), "tn": (((0,), (0,)), ((), ()))}


def _dg(a, b, mode):
    return lax.dot_general(a.astype(BF16), b.astype(BF16), _DOT_DIMS[mode], preferred_element_type=F32)


def _bdot(a, b, mode):
    @jax.custom_vjp
    def f(a, b):
        return _dg(a, b, mode)

    def fwd(a, b):
        return _dg(a, b, mode), (a, b)

    def bwd(res, g):
        a, b = res
        if mode == "nn":
            return _dg(g, b, "nt").astype(a.dtype), _dg(a, g, "tn").astype(b.dtype)
        if mode == "nt":
            return _dg(g, b, "nn").astype(a.dtype), _dg(g, a, "tn").astype(b.dtype)
        return _dg(b, g, "nt").astype(a.dtype), _dg(a, g, "nn").astype(b.dtype)

    f.defvjp(fwd, bwd)
    return f(a, b)


def _roll(x, shift, axis):
    n = x.shape[axis]
    shift = shift % n

    @jax.custom_vjp
    def f(x):
        return pltpu.roll(x, shift, axis)

    def fwd(x):
        return pltpu.roll(x, shift, axis), None

    def bwd(_, g):
        return (pltpu.roll(g, (n - shift) % n, axis),)

    f.defvjp(fwd, bwd)
    return f(x)


def _tri_cumsum(x, forward):
    def mm(lower, v):
        rows = lax.broadcasted_iota(jnp.int32, (CHUNK, CHUNK), 0)
        cols = lax.broadcasted_iota(jnp.int32, (CHUNK, CHUNK), 1)
        m = ((rows >= cols) if lower else (rows <= cols)).astype(F32)
        return jnp.dot(m, v, precision=lax.Precision.HIGHEST, preferred_element_type=F32)

    @jax.custom_vjp
    def f(x):
        return mm(forward, x)

    def fwd(x):
        return mm(forward, x), None

    def bwd(_, g):
        return (mm(not forward, g),)

    f.defvjp(fwd, bwd)
    return f(x)


@jax.custom_jvp
def _log_sigmoid(x):
    return jnp.minimum(x, 0.0) - jnp.log(1.0 + jnp.exp(-jnp.abs(x)))


@_log_sigmoid.defjvp
def _log_sigmoid_jvp(primals, tangents):
    (x,), (t,) = primals, tangents
    return _log_sigmoid(x), t * jax.nn.sigmoid(-x)


def _rms(x, n, w=None):
    y = x * lax.rsqrt(jnp.sum(x * x, axis=-1, keepdims=True) * (1.0 / n) + EPS)
    return y if w is None else y * w


def _silu(x):
    return x * jax.nn.sigmoid(x)


def _gelu_tanh(x):
    return 0.5 * x * (1.0 + jnp.tanh(math.sqrt(2.0 / math.pi) * (x + 0.044715 * (x * x * x))))


def _mm(a, b, mode, name):
    if mode == "nn":
        (m, k), (_, n) = a.shape, b.shape
    elif mode == "nt":
        (m, k), (n, _) = a.shape, b.shape
    else:
        (k, m), (_, n) = a.shape, b.shape
    tm = _pick(m, (768, 512, 256, 128))
    tn = _pick(n, (1024, 512, 256, 128))
    tk = _pick(k, (1024, 768, 512, 256, 128))
    nk = k // tk
    if mode == "nn":
        a_spec = pl.BlockSpec((tm, tk), lambda i, j, kk: (i, kk))
        b_spec = pl.BlockSpec((tk, tn), lambda i, j, kk: (kk, j))
    elif mode == "nt":
        a_spec = pl.BlockSpec((tm, tk), lambda i, j, kk: (i, kk))
        b_spec = pl.BlockSpec((tn, tk), lambda i, j, kk: (j, kk))
    else:
        a_spec = pl.BlockSpec((tk, tm), lambda i, j, kk: (kk, i))
        b_spec = pl.BlockSpec((tk, tn), lambda i, j, kk: (kk, j))

    def body(a_ref, b_ref, o_ref, acc_ref):
        kk = pl.program_id(2)

        @pl.when(kk == 0)
        def _():
            acc_ref[...] = jnp.zeros_like(acc_ref)

        acc_ref[...] += _dg(a_ref[...], b_ref[...], mode)

        @pl.when(kk == nk - 1)
        def _():
            o_ref[...] = acc_ref[...]

    return pl.pallas_call(
        body, name=name, grid=(m // tm, n // tn, nk),
        in_specs=[a_spec, b_spec], out_specs=pl.BlockSpec((tm, tn), lambda i, j, kk: (i, j)),
        out_shape=jax.ShapeDtypeStruct((m, n), F32),
        scratch_shapes=[pltpu.VMEM((tm, tn), F32)],
        compiler_params=_cparams(3),
    )(a, b)


def linear(x, w, name):
    @jax.custom_vjp
    def op(x, w):
        return _mm(x, w.astype(BF16), "nn", name + "_f")

    def fwd(x, w):
        wb = w.astype(BF16)
        return _mm(x, wb, "nn", name + "_f"), (x, wb)

    def bwd(res, g):
        x, wb = res
        return _mm(g, wb, "nt", name + "_dx"), _mm(x, g, "tn", name + "_dw")

    op.defvjp(fwd, bwd)
    return op(x, w)


def rowwise(name, fn, rows, segs, params, out_widths, nct, diff_rows=None):
    n_row, n_seg, n_par, n_out = len(rows), len(segs), len(params), len(out_widths)
    diff_rows = [True] * n_row if diff_rows is None else list(diff_rows)
    r_total = rows[0].shape[0]
    n_tiles = r_total // ROW_TILE

    def seg_of(i):
        return jnp.where(i < nct, 0, 1)

    def row_spec(width):
        return pl.BlockSpec((ROW_TILE, width), lambda i: (i, 0))

    def seg_spec(shape):
        nd = len(shape)
        return pl.BlockSpec((1,) + tuple(shape[1:]), lambda i: (seg_of(i),) + (0,) * (nd - 1))

    def par_spec(shape):
        nd = len(shape)
        return pl.BlockSpec(tuple(shape), lambda i: (0,) * nd)

    in_specs = ([row_spec(r.shape[1]) for r in rows] + [seg_spec(s.shape) for s in segs]
                + [par_spec(p.shape) for p in params])

    def load(refs):
        vals = [r[...].astype(F32) for r in refs[:n_row]]
        vals += [r[0].astype(F32) for r in refs[n_row:n_row + n_seg]]
        vals += [r[...].astype(F32) for r in refs[n_row + n_seg:n_row + n_seg + n_par]]
        return vals

    def fwd_call(arrs):
        def body(*refs):
            outs = fn(*load(refs))
            for o_ref, val in zip(refs[n_row + n_seg + n_par:], outs):
                o_ref[...] = val

        return pl.pallas_call(
            body, name=name + "_f", grid=(n_tiles,), in_specs=in_specs,
            out_specs=[row_spec(w) for w in out_widths],
            out_shape=[jax.ShapeDtypeStruct((r_total, w), F32) for w in out_widths],
            compiler_params=_cparams(1),
        )(*arrs)

    d_idx = [k for k in range(n_row) if diff_rows[k]]

    def bwd_call(arrs, douts):
        n_in = n_row + n_seg + n_par

        def body(*refs):
            i = pl.program_id(0)
            vals = load(refs[:n_in])
            gs = [r[...] for r in refs[n_in:n_in + n_out]]
            out_refs = refs[n_in + n_out:]
            diff_pos = d_idx + list(range(n_row, n_in))

            def f(*dv):
                full = list(vals)
                for pos, v in zip(diff_pos, dv):
                    full[pos] = v
                return tuple(fn(*full))

            _, vjp = jax.vjp(f, *[vals[p] for p in diff_pos])
            grads = vjp(tuple(gs))
            nd = len(d_idx)
            for o_ref, g in zip(out_refs[:nd], grads[:nd]):
                o_ref[...] = g
            first_seg = jnp.logical_or(i == 0, i == nct)
            for o_ref, g in zip(out_refs[nd:nd + n_seg], grads[nd:nd + n_seg]):
                @pl.when(first_seg)
                def _(o_ref=o_ref, g=g):
                    o_ref[0] = g

                @pl.when(jnp.logical_not(first_seg))
                def _(o_ref=o_ref, g=g):
                    o_ref[0] += g
            for o_ref, g in zip(out_refs[nd + n_seg:], grads[nd + n_seg:]):
                @pl.when(i == 0)
                def _(o_ref=o_ref, g=g):
                    o_ref[...] = g

                @pl.when(i != 0)
                def _(o_ref=o_ref, g=g):
                    o_ref[...] += g

        out_specs = ([row_spec(rows[k].shape[1]) for k in d_idx] + [seg_spec(s.shape) for s in segs]
                     + [par_spec(p.shape) for p in params])
        out_shape = ([jax.ShapeDtypeStruct(rows[k].shape, F32) for k in d_idx]
                     + [jax.ShapeDtypeStruct(s.shape, F32) for s in segs]
                     + [jax.ShapeDtypeStruct(p.shape, F32) for p in params])
        return pl.pallas_call(
            body, name=name + "_b", grid=(n_tiles,),
            in_specs=in_specs + [row_spec(w) for w in out_widths],
            out_specs=out_specs, out_shape=out_shape, compiler_params=_cparams(1),
        )(*arrs, *douts)

    @jax.custom_vjp
    def op(*arrs):
        return tuple(fwd_call(arrs))

    def op_fwd(*arrs):
        return tuple(fwd_call(arrs)), arrs

    def op_bwd(arrs, douts):
        grads = list(bwd_call(arrs, douts))
        nd = len(d_idx)
        row_grads = [jnp.zeros_like(arrs[k]) for k in range(n_row)]
        for k, g in zip(d_idx, grads[:nd]):
            row_grads[k] = g
        return tuple(row_grads + grads[nd:])

    op.defvjp(op_fwd, op_bwd)
    return op(*rows, *segs, *params)


ATT_SCALE = MLA_QK ** -0.5


def _attn_fwd_call(q, k, v, tc, name):
    r_total = q.shape[0]
    nq = r_total // ROW_TILE
    nctq = tc // ROW_TILE

    def body(q_ref, k_ref, v_ref, o_ref, lse_ref):
        i = pl.program_id(1)
        qv = q_ref[...]

        def run(nk):
            s = lax.dot_general(qv, k_ref[0:nk, :], _DOT_DIMS["nt"], preferred_element_type=F32) * ATT_SCALE
            m = jnp.max(s, axis=-1, keepdims=True)
            p = jnp.exp(s - m)
            l = jnp.sum(p, axis=-1, keepdims=True)
            o = lax.dot_general(p.astype(BF16), v_ref[0:nk, :], _DOT_DIMS["nn"], preferred_element_type=F32)
            o_ref[...] = o / l
            lse_ref[...] = jnp.broadcast_to(m + jnp.log(l), (ROW_TILE, HEAD_PAD))

        @pl.when(i < nctq)
        def _():
            run(tc)

        @pl.when(i >= nctq)
        def _():
            run(r_total)

    q_spec = pl.BlockSpec((ROW_TILE, HEAD_PAD), lambda h, i: (i, h))
    kv_spec = pl.BlockSpec((r_total, HEAD_PAD), lambda h, i: (0, h))
    return pl.pallas_call(
        body, name=name, grid=(MLA_HEADS, nq), in_specs=[q_spec, kv_spec, kv_spec],
        out_specs=[q_spec, q_spec],
        out_shape=[jax.ShapeDtypeStruct(q.shape, F32), jax.ShapeDtypeStruct(q.shape, F32)],
        compiler_params=_cparams(2),
    )(q, k, v)


def _attn_bwd_call(q, k, v, o, lse, do, tc, name):
    r_total = q.shape[0]
    nq = r_total // ROW_TILE
    nctq = tc // ROW_TILE
    kc = _pick(r_total, (768, 512, 256))

    def body(q_ref, k_ref, v_ref, o_ref, lse_ref, do_ref, dq_ref, dk_ref, dv_ref):
        i = pl.program_id(1)

        @pl.when(i == 0)
        def _():
            dk_ref[...] = jnp.zeros_like(dk_ref)
            dv_ref[...] = jnp.zeros_like(dv_ref)

        qv = q_ref[...]
        dov = do_ref[...]
        dob = dov.astype(BF16)
        lse = lse_ref[:, 0:1]
        delta = jnp.sum(dov * o_ref[...], axis=-1, keepdims=True)

        def chunk(start, size, dq):
            kk = k_ref[pl.ds(start, size), :]
            vv = v_ref[pl.ds(start, size), :]
            s = lax.dot_general(qv, kk, _DOT_DIMS["nt"], preferred_element_type=F32) * ATT_SCALE
            p = jnp.exp(s - lse)
            dp = lax.dot_general(dob, vv, _DOT_DIMS["nt"], preferred_element_type=F32)
            ds = (p * (dp - delta) * ATT_SCALE).astype(BF16)
            dk_ref[pl.ds(start, size), :] += lax.dot_general(ds, qv, _DOT_DIMS["tn"], preferred_element_type=F32)
            dv_ref[pl.ds(start, size), :] += lax.dot_general(p.astype(BF16), dob, _DOT_DIMS["tn"],
                                                             preferred_element_type=F32)
            return dq + lax.dot_general(ds, kk, _DOT_DIMS["nn"], preferred_element_type=F32)

        zero = jnp.zeros((ROW_TILE, HEAD_PAD), F32)

        @pl.when(i < nctq)
        def _():
            dq = zero
            for c in range(tc // ROW_TILE):
                dq = chunk(c * ROW_TILE, ROW_TILE, dq)
            dq_ref[...] = dq

        @pl.when(i >= nctq)
        def _():
            dq_ref[...] = lax.fori_loop(0, r_total // kc, lambda c, dq: chunk(pl.multiple_of(c * kc, kc), kc, dq), zero)

    q_spec = pl.BlockSpec((ROW_TILE, HEAD_PAD), lambda h, i: (i, h))
    kv_spec = pl.BlockSpec((r_total, HEAD_PAD), lambda h, i: (0, h))
    return pl.pallas_call(
        body, name=name, grid=(MLA_HEADS, nq),
        in_specs=[q_spec, kv_spec, kv_spec, q_spec, q_spec, q_spec],
        out_specs=[q_spec, kv_spec, kv_spec],
        out_shape=[jax.ShapeDtypeStruct(q.shape, F32)] * 3,
        compiler_params=_cparams(2),
    )(q, k, v, o, lse, do)


def attention(q, k, v, tc, name):
    @jax.custom_vjp
    def op(q, k, v):
        return _attn_fwd_call(q.astype(BF16), k.astype(BF16), v.astype(BF16), tc, name + "_f")[0]

    def fwd(q, k, v):
        qb, kb, vb = q.astype(BF16), k.astype(BF16), v.astype(BF16)
        o, lse = _attn_fwd_call(qb, kb, vb, tc, name + "_f")
        return o, (qb, kb, vb, o, lse)

    def bwd(res, do):
        qb, kb, vb, o, lse = res
        return tuple(_attn_bwd_call(qb, kb, vb, o, lse, do, tc, name + "_b"))

    op.defvjp(fwd, bwd)
    return op(q, k, v)


def _chunk_masks(forward):
    rows = lax.broadcasted_iota(jnp.int32, (CHUNK, CHUNK), 0)
    cols = lax.broadcasted_iota(jnp.int32, (CHUNK, CHUNK), 1)
    return (rows >= cols) if forward else (rows < cols)


def _gla_chunk(forward, q, k, v, la, st0):
    cum = _tri_cumsum(la, forward)
    tot = jnp.sum(la, axis=0, keepdims=True)
    k_end = k * jnp.exp(tot - cum)
    st1 = st0 * jnp.exp(tot) + _bdot(v, k_end, "tn")
    q_dec = q * jnp.exp(cum)
    att = _bdot(q_dec, k * jnp.exp(-cum), "nt")
    att = jnp.where(_chunk_masks(forward), att, 0.0)
    o = _bdot(att, v, "nn") + _bdot(q_dec, st0, "nt")
    return o, st1


def _ret_chunk(forward, q, k, v, rd, st0):
    lg = -jnp.exp(rd[0:1, 0:1])
    rows = lax.broadcasted_iota(jnp.int32, (CHUNK, CHUNK), 0).astype(F32)
    cols = lax.broadcasted_iota(jnp.int32, (CHUNK, CHUNK), 1).astype(F32)
    pos = lax.broadcasted_iota(jnp.int32, (CHUNK, 1), 0).astype(F32)
    if forward:
        to_end, from_start, rel = CHUNK - 1.0 - pos, pos + 1.0, rows - cols
    else:
        to_end, from_start, rel = pos, CHUNK - pos, cols - rows
    mask = _chunk_masks(forward)
    dmat = jnp.where(mask, jnp.exp(jnp.where(mask, rel, 0.0) * lg), 0.0)
    st1 = st0 * jnp.exp(CHUNK * lg) + _bdot(v, k * jnp.exp(to_end * lg), "tn")
    att = _bdot(q, k, "nt") * dmat
    o = _bdot(att, v, "nn") + _bdot(q, st0, "nt") * jnp.exp(from_start * lg)
    return o, st1


def _scan_block_fn(chunk_fn, forward):
    order = range(SCAN_CHUNKS) if forward else range(SCAN_CHUNKS - 1, -1, -1)

    def block(q, k, v, aux, st0):
        outs = [None] * SCAN_CHUNKS
        st = st0
        for c in order:
            sl = slice(c * CHUNK, (c + 1) * CHUNK)
            a = aux[sl] if aux.shape[0] == ROW_TILE else aux
            outs[c], st = chunk_fn(forward, q[sl], k[sl], v[sl], a, st)
        return jnp.concatenate(outs, axis=0), st

    return block


def scan(kind, forward, q, k, v, aux, tc, name):
    heads = q.shape[1] // HEAD_PAD
    r_total = q.shape[0]
    nblk = r_total // ROW_TILE
    nctb = tc // ROW_TILE
    block_fn = _scan_block_fn(_gla_chunk if kind == "gla" else _ret_chunk, forward)
    per_row_aux = kind == "gla"

    def blk(g):
        if forward:
            return g
        return jnp.where(g < nctb, nctb - 1 - g, nblk - 1 - (g - nctb))

    def specs(step_to_g):
        row = pl.BlockSpec((ROW_TILE, HEAD_PAD), lambda h, s: (blk(step_to_g(s)), h))
        aux_spec = row if per_row_aux else pl.BlockSpec((1, 8, HEAD_PAD), lambda h, s: (h, 0, 0))
        st = pl.BlockSpec((1, 1, HEAD_PAD, HEAD_PAD), lambda h, s: (h, step_to_g(s), 0, 0))
        return row, aux_spec, st

    def fwd_call(q, k, v, aux):
        row, aux_spec, st_spec = specs(lambda s: s)

        def body(q_ref, k_ref, v_ref, a_ref, o_ref, st0_ref, st_ref):
            @pl.when(pl.program_id(1) == 0)
            def _():
                st_ref[...] = jnp.zeros_like(st_ref)

            st0 = st_ref[...]
            st0_ref[0, 0] = st0
            a = a_ref[...] if per_row_aux else a_ref[0]
            o, st1 = block_fn(q_ref[...], k_ref[...], v_ref[...], a, st0)
            o_ref[...] = o
            st_ref[...] = st1

        return pl.pallas_call(
            body, name=name + "_f", grid=(heads, nblk), in_specs=[row, row, row, aux_spec],
            out_specs=[row, st_spec],
            out_shape=[jax.ShapeDtypeStruct(q.shape, F32),
                       jax.ShapeDtypeStruct((heads, nblk, HEAD_PAD, HEAD_PAD), F32)],
            scratch_shapes=[pltpu.VMEM((HEAD_PAD, HEAD_PAD), F32)],
            compiler_params=_cparams(2),
        )(q, k, v, aux)

    def bwd_call(q, k, v, aux, st0s, do):
        row, aux_spec, st_spec = specs(lambda s: nblk - 1 - s)

        def body(q_ref, k_ref, v_ref, a_ref, st0_ref, do_ref, dq_ref, dk_ref, dv_ref, da_ref, dst_ref):
            s = pl.program_id(1)

            @pl.when(s == 0)
            def _():
                dst_ref[...] = jnp.zeros_like(dst_ref)

            a = a_ref[...] if per_row_aux else a_ref[0]
            _, vjp = jax.vjp(block_fn, q_ref[...], k_ref[...], v_ref[...], a, st0_ref[0, 0])
            dq, dk, dv, da, dst0 = vjp((do_ref[...], dst_ref[...]))
            dq_ref[...] = dq
            dk_ref[...] = dk
            dv_ref[...] = dv
            dst_ref[...] = dst0
            if per_row_aux:
                da_ref[...] = da
            else:
                @pl.when(s == 0)
                def _():
                    da_ref[0] = da

                @pl.when(s != 0)
                def _():
                    da_ref[0] += da

        return pl.pallas_call(
            body, name=name + "_b", grid=(heads, nblk),
            in_specs=[row, row, row, aux_spec, st_spec, row],
            out_specs=[row, row, row, aux_spec],
            out_shape=[jax.ShapeDtypeStruct(q.shape, F32)] * 3 + [jax.ShapeDtypeStruct(aux.shape, F32)],
            scratch_shapes=[pltpu.VMEM((HEAD_PAD, HEAD_PAD), F32)],
            compiler_params=_cparams(2),
        )(q, k, v, aux, st0s, do)

    @jax.custom_vjp
    def op(q, k, v, aux):
        return fwd_call(q, k, v, aux)[0]

    def fwd(q, k, v, aux):
        o, st0s = fwd_call(q, k, v, aux)
        return o, (q, k, v, aux, st0s)

    def bwd(res, do):
        return tuple(bwd_call(*res, do))

    op.defvjp(fwd, bwd)
    return op(q, k, v, aux)


HALO = 8


def _neighbours(main, prev8, next8, i, nct, n_tiles):
    has_prev = jnp.logical_and(i != 0, i != nct).astype(F32)
    has_next = jnp.logical_and(i != nct - 1, i != n_tiles - 1).astype(F32)
    row = lax.broadcasted_iota(jnp.int32, main.shape, 0)
    down = jnp.where(row == 0, prev8[HALO - 1:HALO] * has_prev, pltpu.roll(main, 1, 0))
    up = jnp.where(row == ROW_TILE - 1, next8[0:1] * has_next, pltpu.roll(main, ROW_TILE - 1, 0))
    return down, up


def dwconv(x, w8, b, tc, name):
    r_total, width = x.shape
    n_tiles = r_total // ROW_TILE
    nct = tc // ROW_TILE
    per = ROW_TILE // HALO
    main_spec = pl.BlockSpec((ROW_TILE, width), lambda i: (i, 0))
    prev_spec = pl.BlockSpec((HALO, width), lambda i: (jnp.maximum(i * per - 1, 0), 0))
    next_spec = pl.BlockSpec((HALO, width), lambda i: (jnp.minimum((i + 1) * per, r_total // HALO - 1), 0))
    w_spec = pl.BlockSpec((8, width), lambda i: (0, 0))
    b_spec = pl.BlockSpec((1, width), lambda i: (0, 0))

    def fwd_call(x, w8, b):
        def body(x_ref, p_ref, n_ref, w_ref, b_ref, o_ref):
            xv = x_ref[...]
            down, up = _neighbours(xv, p_ref[...], n_ref[...], pl.program_id(0), nct, n_tiles)
            o_ref[...] = w_ref[0:1] * down + w_ref[1:2] * xv + w_ref[2:3] * up + b_ref[...]

        return pl.pallas_call(
            body, name=name + "_f", grid=(n_tiles,), in_specs=[main_spec, prev_spec, next_spec, w_spec, b_spec],
            out_specs=main_spec, out_shape=jax.ShapeDtypeStruct(x.shape, F32), compiler_params=_cparams(1),
        )(x, x, x, w8, b)

    def bwd_call(x, w8, g):
        def body(x_ref, xp_ref, xn_ref, g_ref, gp_ref, gn_ref, w_ref, dx_ref, dw_ref, db_ref):
            i = pl.program_id(0)
            xv, gv = x_ref[...], g_ref[...]
            x_down, x_up = _neighbours(xv, xp_ref[...], xn_ref[...], i, nct, n_tiles)
            g_down, g_up = _neighbours(gv, gp_ref[...], gn_ref[...], i, nct, n_tiles)
            dx_ref[...] = w_ref[0:1] * g_up + w_ref[1:2] * gv + w_ref[2:3] * g_down
            dw = jnp.concatenate([jnp.sum(gv * x_down, axis=0, keepdims=True),
                                  jnp.sum(gv * xv, axis=0, keepdims=True),
                                  jnp.sum(gv * x_up, axis=0, keepdims=True),
                                  jnp.zeros((5, width), F32)], axis=0)
            db = jnp.sum(gv, axis=0, keepdims=True)

            @pl.when(i == 0)
            def _():
                dw_ref[...] = dw
                db_ref[...] = db

            @pl.when(i != 0)
            def _():
                dw_ref[...] += dw
                db_ref[...] += db

        return pl.pallas_call(
            body, name=name + "_b", grid=(n_tiles,),
            in_specs=[main_spec, prev_spec, next_spec, main_spec, prev_spec, next_spec, w_spec],
            out_specs=[main_spec, w_spec, b_spec],
            out_shape=[jax.ShapeDtypeStruct(x.shape, F32), jax.ShapeDtypeStruct((8, width), F32),
                       jax.ShapeDtypeStruct((1, width), F32)],
            compiler_params=_cparams(1),
        )(x, x, x, g, g, g, w8)

    @jax.custom_vjp
    def op(x, w8, b):
        return fwd_call(x, w8, b)

    def fwd(x, w8, b):
        return fwd_call(x, w8, b), (x, w8)

    def bwd(res, g):
        return tuple(bwd_call(*res, g))

    op.defvjp(fwd, bwd)
    return op(x, w8, b)


def loss_head(h, target, tc, name):
    r_total, width = h.shape
    n_tiles = r_total // ROW_TILE
    nct = tc // ROW_TILE

    def call(h, target):
        def body(h_ref, t_ref, dh_ref, loss_ref, acc_ref):
            i = pl.program_id(0)

            @pl.when(i == 0)
            def _():
                acc_ref[...] = jnp.zeros_like(acc_ref)

            @pl.when(i < nct)
            def _():
                dh_ref[...] = jnp.zeros_like(dh_ref)

            @pl.when(i >= nct)
            def _():
                err = h_ref[...] - t_ref[...]
                dh_ref[...] = err * (1.0 / width)
                acc_ref[...] += jnp.sum((err * err).reshape(ROW_TILE // 8, 8, width), axis=0)

            @pl.when(i == n_tiles - 1)
            def _():
                loss_ref[...] = jnp.sum(acc_ref[...]).reshape(1, 1) * (0.5 / width)

        row = pl.BlockSpec((ROW_TILE, width), lambda i: (i, 0))
        return pl.pallas_call(
            body, name=name, grid=(n_tiles,),
            in_specs=[row, pl.BlockSpec((ROW_TILE, width), lambda i: (jnp.maximum(i - nct, 0), 0))],
            out_specs=[row, pl.BlockSpec((1, 1), lambda i: (0, 0))],
            out_shape=[jax.ShapeDtypeStruct(h.shape, F32), jax.ShapeDtypeStruct((1, 1), F32)],
            scratch_shapes=[pltpu.VMEM((8, width), F32)], compiler_params=_cparams(1),
        )(h, target)

    @jax.custom_vjp
    def op(h, target):
        return call(h, target)[1][0, 0]

    def fwd(h, target):
        dh, loss = call(h, target)
        return loss[0, 0], (dh, target)

    def bwd(res, g):
        dh, target = res
        return dh * g, jnp.zeros_like(target)

    op.defvjp(fwd, bwd)
    return op(h, target)


PACK_W = 1024
PACK_TILE = 128


def slab_sum(slabs, name):
    n_slab, n, _ = slabs.shape

    def body(s_ref, o_ref):
        acc = s_ref[0]
        for j in range(1, n_slab):
            acc = acc + s_ref[j]
        o_ref[...] = acc

    return pl.pallas_call(
        body, name=name, grid=(n // PACK_TILE,),
        in_specs=[pl.BlockSpec((n_slab, PACK_TILE, PACK_W), lambda i: (0, i, 0))],
        out_specs=pl.BlockSpec((PACK_TILE, PACK_W), lambda i: (i, 0)),
        out_shape=jax.ShapeDtypeStruct((n, PACK_W), F32), compiler_params=_cparams(1),
    )(slabs)


def adamw(g_slabs, w, m, v, name):
    n_slab, n, _ = g_slabs.shape

    def body(g_ref, w_ref, m_ref, v_ref, go_ref, d_ref, mo_ref, vo_ref):
        g = g_ref[0]
        for j in range(1, n_slab):
            g = g + g_ref[j]
        m_new = ADAM_B1 * m_ref[...] + (1.0 - ADAM_B1) * g
        v_new = ADAM_B2 * v_ref[...] + (1.0 - ADAM_B2) * (g * g)
        m_hat = m_new / (1.0 - ADAM_B1 ** ADAM_STEP)
        v_hat = v_new / (1.0 - ADAM_B2 ** ADAM_STEP)
        go_ref[...] = g
        d_ref[...] = -ADAM_LR * (m_hat / (jnp.sqrt(v_hat) + ADAM_EPS) + ADAM_WD * w_ref[...])
        mo_ref[...] = m_new
        vo_ref[...] = v_new

    flat = pl.BlockSpec((PACK_TILE, PACK_W), lambda i: (i, 0))
    return pl.pallas_call(
        body, name=name, grid=(n // PACK_TILE,),
        in_specs=[pl.BlockSpec((n_slab, PACK_TILE, PACK_W), lambda i: (0, i, 0)), flat, flat, flat],
        out_specs=[flat] * 4, out_shape=[jax.ShapeDtypeStruct((n, PACK_W), F32)] * 4, compiler_params=_cparams(1),
    )(g_slabs, w, m, v)


def all_gather(x, name):
    m_per, n = x.shape

    def body(x_ref, out_ref, send_sems, recv_sems, local_sem):
        px, py, pc = lax.axis_index("x"), lax.axis_index("y"), lax.axis_index("c")
        me, sibling = (px, py, pc), (px, py, 1 - pc)
        chips = [(1 - px, py), (px, 1 - py), (1 - px, 1 - py)]

        def rows(bx, by, bc):
            return out_ref.at[pl.ds((4 * bx + 2 * by + bc) * m_per, m_per), :]

        def copy(k, block, to, src=None):
            return pltpu.make_async_remote_copy(
                src_ref=rows(*block) if src is None else src, dst_ref=rows(*block),
                send_sem=send_sems.at[k], recv_sem=recv_sems.at[k], device_id=to, device_id_type=MESH)

        mine = pltpu.make_async_copy(x_ref, rows(*me), local_sem)
        mine.start()
        first = [copy(0, me, sibling, src=x_ref)]
        first += [copy(1 + j, me, (*chip, pc), src=x_ref) for j, chip in enumerate(chips)]
        for cp in first:
            cp.start()
        passed = [copy(4 + j, (*chip, pc), sibling) for j, chip in enumerate(chips)]
        for j, chip in enumerate(chips):
            copy(1 + j, (*chip, pc), me).wait_recv()
            passed[j].start()
        copy(0, sibling, me).wait_recv()
        for j, chip in enumerate(chips):
            copy(4 + j, (*chip, 1 - pc), me).wait_recv()
        for cp in first + passed:
            cp.wait_send()
        mine.wait()

    return pl.pallas_call(
        body, name=name, out_shape=jax.ShapeDtypeStruct((N_DEV * m_per, n), x.dtype),
        in_specs=[pl.BlockSpec(memory_space=pl.ANY)], out_specs=pl.BlockSpec(memory_space=pl.ANY),
        scratch_shapes=[pltpu.SemaphoreType.DMA((7,)), pltpu.SemaphoreType.DMA((7,)), pltpu.SemaphoreType.DMA],
    )(x)


def all_to_all(x, name):
    _, n, cols = x.shape

    def body(x_ref, out_ref, send_sems, recv_sems, local_sem):
        px, py, pc = lax.axis_index("x"), lax.axis_index("y"), lax.axis_index("c")
        mine_idx = 4 * px + 2 * py + pc
        flips = [(fx, fy, fc) for fx in (0, 1) for fy in (0, 1) for fc in (0, 1)][1:]
        local = pltpu.make_async_copy(x_ref.at[mine_idx], out_ref.at[mine_idx], local_sem)
        local.start()
        copies = []
        for k, (fx, fy, fc) in enumerate(flips):
            qx, qy, qc = px ^ fx, py ^ fy, pc ^ fc
            peer_idx = 4 * qx + 2 * qy + qc
            copies.append((
                pltpu.make_async_remote_copy(
                    src_ref=x_ref.at[peer_idx], dst_ref=out_ref.at[mine_idx], send_sem=send_sems.at[k],
                    recv_sem=recv_sems.at[k], device_id=(qx, qy, qc), device_id_type=MESH),
                pltpu.make_async_remote_copy(
                    src_ref=x_ref.at[peer_idx], dst_ref=out_ref.at[peer_idx], send_sem=send_sems.at[k],
                    recv_sem=recv_sems.at[k], device_id=(qx, qy, qc), device_id_type=MESH)))
        for send, _ in copies:
            send.start()
        for _, landing in copies:
            landing.wait_recv()
        for send, _ in copies:
            send.wait_send()
        local.wait()

    return pl.pallas_call(
        body, name=name, out_shape=jax.ShapeDtypeStruct(x.shape, x.dtype),
        in_specs=[pl.BlockSpec(memory_space=pl.ANY)], out_specs=pl.BlockSpec(memory_space=pl.ANY),
        scratch_shapes=[pltpu.SemaphoreType.DMA((7,)), pltpu.SemaphoreType.DMA((7,)), pltpu.SemaphoreType.DMA],
    )(x)


IN_OFFSETS = {}
_off = 0
for _name, _width in (("mla_q", 256), ("mla_kv", 128), ("mla_kr", 32), ("gla_q", 512), ("gla_k", 512), ("gla_v", 512),
                      ("gla_g", 512), ("gla_rf", 16), ("gla_rb", 16), ("ret_q", 512), ("ret_k", 512), ("ret_v", 512),
                      ("ret_g", 512), ("gate_mla", 1024), ("gate_gla", 1024), ("gate_ret", 1024)):
    IN_OFFSETS[_name] = (_off, _off + _width)
    _off += _width
N_IN = _off

P_GLA, P_RET, P_GATE, P_MLAQ, P_MLAKV, P_MLAKR, P_RANK, P_END = 0, 2048, 4096, 7168, 7424, 7552, 7680, 7808


def _pad_in_proj(w):
    def cols(a, b):
        return w[:, IN_OFFSETS[a][0]:IN_OFFSETS[b][1]]

    def z(n):
        return jnp.zeros((w.shape[0], n), w.dtype)

    return jnp.concatenate([cols("gla_q", "gla_g"), cols("ret_q", "ret_g"), cols("gate_mla", "gate_ret"),
                            cols("mla_q", "mla_kv"), z(MLA_NOPE), cols("mla_kr", "mla_kr"),
                            z(HEAD_PAD - MLA_QK), cols("gla_rf", "gla_rb"), z(HEAD_PAD - 2 * GLA_RANK),
                            z(N_IN_PAD - P_END)], axis=1)


def _pad_last(a, n):
    return jnp.pad(a, [(0, 0)] * (a.ndim - 1) + [(0, n - a.shape[-1])])


def _position_tables(tc, t):
    pos = jnp.arange(t)
    inv = ROPE_THETA ** (-jnp.arange(MLA_ROPE // 4, dtype=F32) * 2.0 / (MLA_ROPE // 2))
    ang_r = (pos // GRID_W).astype(F32)[:, None] * inv[None, :]
    ang_c = (pos % GRID_W).astype(F32)[:, None] * inv[None, :]
    z8, z32, z64 = jnp.zeros((t, 8), F32), jnp.zeros((t, 32), F32), jnp.zeros((t, 64), F32)
    lat_c = jnp.concatenate([jnp.ones((t, 64), F32), jnp.cos(ang_r), jnp.cos(ang_r), jnp.cos(ang_c), jnp.cos(ang_c),
                             z32], axis=1)
    lat_sn = jnp.concatenate([z64, -jnp.sin(ang_r), z8, -jnp.sin(ang_c), z8, z32], axis=1)
    lat_sp = jnp.concatenate([z64, z8, jnp.sin(ang_r), z8, jnp.sin(ang_c), z32], axis=1)
    ctx_c = jnp.concatenate([jnp.ones((tc, MLA_QK), F32), jnp.zeros((tc, HEAD_PAD - MLA_QK), F32)], axis=1)
    ctx_z = jnp.zeros((tc, HEAD_PAD), F32)
    rinv = 1.0 / (RET_THETA ** jnp.linspace(0.0, 1.0, RET_DK // 2, dtype=F32))
    rang = jnp.arange(tc + t).astype(F32)[:, None] * rinv[None, :]
    return dict(c=jnp.concatenate([ctx_c, lat_c]), sn=jnp.concatenate([ctx_z, lat_sn]),
                sp=jnp.concatenate([ctx_z, lat_sp]),
                rc=jnp.concatenate([jnp.cos(rang), jnp.cos(rang)], axis=1),
                rs=jnp.concatenate([-jnp.sin(rang), jnp.sin(rang)], axis=1))


def _heads(x):
    return [x[:, h * HEAD_PAD:(h + 1) * HEAD_PAD] for h in range(x.shape[1] // HEAD_PAD)]


def _mla_rope(x, c, sn, sp):
    return x * c + _roll(x, HEAD_PAD - 8, 1) * sn + _roll(x, 8, 1) * sp


def _norm_mod_fn(shift_row, scale_row):
    def fn(h, mod, w):
        return (_rms(h, D, w) * (1.0 + mod[scale_row:scale_row + 1]) + mod[shift_row:shift_row + 1],)
    return fn


def _resid_fn(gate_row):
    def fn(h, y, mod):
        return (h + mod[gate_row:gate_row + 1] * y,)
    return fn


def _q_fn(cq, c, sn, sp, norm_a, w_qb, q_norm):
    qf = _bdot(_rms(cq, MLA_Q_LORA, norm_a), w_qb, "nn")
    return (jnp.concatenate([_mla_rope(_rms(qh, MLA_QK, q_norm), c, sn, sp) for qh in _heads(qf)], axis=1),)


def _kv_fn(ckv, kr, c, sn, sp, norm_a, w_k, w_v, k_norm):
    x = _rms(ckv, MLA_KV_LORA, norm_a)
    kf = _bdot(x, w_k, "nn")
    k = jnp.concatenate([_mla_rope(_rms(kh + kr, MLA_QK, k_norm), c, sn, sp) for kh in _heads(kf)], axis=1)
    return k, _bdot(x, w_v, "nn")


def _decay_fn(ranks, w2, b):
    la = _log_sigmoid(_bdot(ranks, w2, "nn") + b) * (1.0 / GLA_NORMALIZER)
    return la[:, :GLA_HEADS * GLA_DK], la[:, GLA_HEADS * GLA_DK:]


def _gla_q_fn(q):
    return (q * (GLA_DK ** -0.5),)


def _ret_rot_fn(q, k, rc, rs):
    def rot(x, scale):
        return jnp.concatenate([(xh * rc + _roll(xh, RET_DK // 2, 1) * rs) * scale for xh in _heads(x)], axis=1)
    return rot(q, 1.0), rot(k, RET_DK ** -0.5)


def _gla_out_fn(o_f, o_b, g, w):
    y = jnp.concatenate([_rms(oh, HEAD_PAD, w) for oh in _heads(o_f + o_b)], axis=1)
    return (y * _silu(g),)


def _ret_out_fn(o_f, o_b, g):
    y = jnp.concatenate([_rms(oh, HEAD_PAD) for oh in _heads(o_f + o_b)], axis=1)
    return (y * _silu(g),)


def _merge_fn(z0, z1, z2, g0, g1, g2, bg):
    return (jax.nn.sigmoid(g0 + bg[0:1]) * z0 + jax.nn.sigmoid(g1 + bg[1:2]) * z1 + jax.nn.sigmoid(g2 + bg[2:3]) * z2,)


def _ffn_act_fn(c, up):
    return (_gelu_tanh(c) * up,)


def _layer(l, h, mod, w, tabs, tc):
    nct = tc // ROW_TILE
    tag = f"_l{l}"
    row = lambda a: a[l][None]
    a = rowwise("norm1" + tag, _norm_mod_fn(0, 1), [h], [mod], [row(w["norm1_w"])], [D], nct)[0]
    p = linear(a, _pad_in_proj(w["w_in"][l]), "in_proj" + tag)
    piece = lambda start, width: p[:, start:start + width]

    w_qb = _pad_last(w["mla_w_qb"][l].reshape(MLA_Q_LORA, MLA_HEADS, MLA_QK), HEAD_PAD).reshape(MLA_Q_LORA, -1)
    w_kvb = w["mla_w_kvb"][l].reshape(MLA_KV_LORA, MLA_HEADS, MLA_NOPE + MLA_V)
    w_k = _pad_last(w_kvb[:, :, :MLA_NOPE], HEAD_PAD).reshape(MLA_KV_LORA, -1)
    w_v = _pad_last(w_kvb[:, :, MLA_NOPE:], HEAD_PAD).reshape(MLA_KV_LORA, -1)
    rope = [tabs["c"], tabs["sn"], tabs["sp"]]
    q = rowwise("mla_q" + tag, _q_fn, [piece(P_MLAQ, MLA_Q_LORA)] + rope, [],
                [row(w["mla_q_norm_a"]), w_qb, _pad_last(row(w["mla_q_norm"]), HEAD_PAD)],
                [MLA_HEADS * HEAD_PAD], nct, diff_rows=[True, False, False, False])[0]
    k, v = rowwise("mla_kv" + tag, _kv_fn, [piece(P_MLAKV, MLA_KV_LORA), piece(P_MLAKR, HEAD_PAD)] + rope, [],
                   [row(w["mla_kv_norm_a"]), w_k, w_v, _pad_last(row(w["mla_k_norm"]), HEAD_PAD)],
                   [MLA_HEADS * HEAD_PAD] * 2, nct, diff_rows=[True, True, False, False, False])
    y_mla = attention(q, k, v, tc, "attn" + tag)
    wb_mla = _pad_last(w["w_branch"][l, 0].reshape(MLA_HEADS, MLA_V, D).transpose(0, 2, 1), HEAD_PAD)
    wb_mla = wb_mla.transpose(0, 2, 1).reshape(MLA_HEADS * HEAD_PAD, D)

    w2 = jnp.zeros((HEAD_PAD, 2 * GLA_HEADS * GLA_DK), F32)
    w2 = w2.at[:GLA_RANK, :GLA_HEADS * GLA_DK].set(w["gla_w_gk2"][l, 0])
    w2 = w2.at[GLA_RANK:2 * GLA_RANK, GLA_HEADS * GLA_DK:].set(w["gla_w_gk2"][l, 1])
    la_f, la_b = rowwise("gla_decay" + tag, _decay_fn, [piece(P_RANK, HEAD_PAD)], [],
                         [w2, w["gla_b_gk"][l].reshape(1, -1)], [GLA_HEADS * GLA_DK] * 2, nct)
    gq = rowwise("gla_q" + tag, _gla_q_fn, [piece(P_GLA, 512)], [], [], [512], nct)[0]
    gk, gv, gg = piece(P_GLA + 512, 512), piece(P_GLA + 1024, 512), piece(P_GLA + 1536, 512)
    o_f = scan("gla", True, gq, gk, gv, la_f, tc, "gla_fw" + tag)
    o_b = scan("gla", False, gq, gk, gv, la_b, tc, "gla_bw" + tag)
    y_gla = rowwise("gla_out" + tag, _gla_out_fn, [o_f, o_b, gg], [], [row(w["gla_o_norm"])], [512], nct)[0]

    rq, rk = rowwise("ret_rot" + tag, _ret_rot_fn, [piece(P_RET, 512), piece(P_RET + 512, 512), tabs["rc"], tabs["rs"]],
                     [], [], [512, 512], nct, diff_rows=[True, True, False, False])
    rv, rg = piece(P_RET + 1024, 512), piece(P_RET + 1536, 512)
    rd = jnp.broadcast_to(w["ret_decay"][l][:, :, None, None], (2, RET_HEADS, 8, HEAD_PAD))
    r_f = scan("ret", True, rq, rk, rv, rd[0], tc, "ret_fw" + tag)
    r_b = scan("ret", False, rq, rk, rv, rd[1], tc, "ret_bw" + tag)
    y_ret = rowwise("ret_out" + tag, _ret_out_fn, [r_f, r_b, rg], [], [], [512], nct)[0]

    z = [linear(y_mla, wb_mla, "branch_mla" + tag), linear(y_gla, w["w_branch"][l, 1], "branch_gla" + tag),
         linear(y_ret, w["w_branch"][l, 2], "branch_ret" + tag)]
    gates = [piece(P_GATE + n * D, D) for n in range(3)]
    merged = rowwise("merge" + tag, _merge_fn, z + gates, [], [_pad_rows(w["b_gate"][l], 8)], [D], nct)[0]
    y = linear(merged, w["w_out"][l], "w_out" + tag)
    h = rowwise("resid1" + tag, _resid_fn(2), [h, y], [mod], [], [D], nct)[0]

    a2 = rowwise("norm2" + tag, _norm_mod_fn(3, 4), [h], [mod], [row(w["norm2_w"])], [D], nct)[0]
    gate = linear(a2, w["w_ffn_in"][l][:, :D_FF], "ffn_gate" + tag)
    up = linear(a2, w["w_ffn_in"][l][:, D_FF:], "ffn_up" + tag)
    conv = dwconv(gate, _pad_rows(w["w_dw"][l], 8), row(w["b_dw"]), tc, "dwconv" + tag)
    u = rowwise("ffn_act" + tag, _ffn_act_fn, [conv, up], [], [], [D_FF], nct)[0]
    f = linear(u, w["w_ffn_out"][l], "ffn_out" + tag)
    return rowwise("resid2" + tag, _resid_fn(5), [h, f], [mod], [], [D], nct)[0]


def _pad_rows(a, n):
    return jnp.pad(a, [(0, n - a.shape[0])] + [(0, 0)] * (a.ndim - 1))


def local_loss(w, mod, x, ctx, target):
    tc, t = ctx.shape[0], x.shape[0]
    tabs = _position_tables(tc, t)
    h = jnp.concatenate([ctx, x], axis=0)
    for l in range(DEPTH):
        h = _layer(l, h, mod[l], w, tabs, tc)
    return loss_head(h, target, tc, "loss_head")


ADA_ROWS = 16


def ada_forward(cond_in, w_ada, b_loc):
    cols = w_ada.shape[2]

    def body(x_ref, w_ref, b_ref, o_ref):
        s = _silu(x_ref[...])
        for l in range(DEPTH):
            o_ref[l] = _dg(s, w_ref[l], "nn") + b_ref[l]

    return pl.pallas_call(
        body, name="ada_forward", out_shape=jax.ShapeDtypeStruct((DEPTH, ADA_ROWS, cols), F32),
        compiler_params=pltpu.CompilerParams(vmem_limit_bytes=VMEM_LIMIT_BYTES),
    )(cond_in, w_ada, b_loc)


def ada_backward(cond_in, g_loc, dmod_own, w_ada):
    cols = w_ada.shape[2]

    def body(x_ref, g_ref, own_ref, w_ref, gw_ref, dc_ref, gb_ref):
        x = x_ref[...]
        s = _silu(x)
        dcond = jnp.zeros((8, D), F32)
        for l in range(DEPTH):
            g_ctx = jnp.sum(g_ref[2 * l], axis=0, keepdims=True)
            g_rows = jnp.concatenate([g_ref[2 * l + 1], jnp.broadcast_to(g_ctx, (8, cols))], axis=0)
            keep = lax.broadcasted_iota(jnp.int32, (ADA_ROWS, cols), 0) <= N_DEV
            gw_ref[l] = _dg(s, jnp.where(keep, g_rows, 0.0), "tn")
            dcond = dcond + _dg(jnp.broadcast_to(g_ctx, (8, cols)), w_ref[l], "nt")
            gb_ref[l:l + 1, :] = own_ref[2 * l:2 * l + 1, :] + own_ref[2 * l + 1:2 * l + 2, :]
        xc = x[N_DEV:N_DEV + 1]
        sig = jax.nn.sigmoid(xc)
        dc_ref[...] = dcond[0:1] * (sig * (1.0 + xc * (1.0 - sig)))

    return pl.pallas_call(
        body, name="ada_backward",
        out_shape=[jax.ShapeDtypeStruct(w_ada.shape, F32), jax.ShapeDtypeStruct((1, D), F32),
                   jax.ShapeDtypeStruct((DEPTH, 6 * D), F32)],
        compiler_params=pltpu.CompilerParams(vmem_limit_bytes=VMEM_LIMIT_BYTES),
    )(cond_in, g_loc, dmod_own, w_ada)


WEIGHTS = ["c_ctx", "w_ada", "b_ada", "norm1_w", "norm2_w", "w_in", "b_gate", "mla_q_norm_a", "mla_w_qb",
           "mla_kv_norm_a", "mla_w_kvb", "mla_q_norm", "mla_k_norm", "gla_w_gk2", "gla_b_gk", "gla_o_norm",
           "ret_decay", "w_branch", "w_out", "w_ffn_in", "w_dw", "b_dw", "w_ffn_out"]
INPUTS = ["x", "c", "ctx"] + WEIGHTS + ["loss_target"] + ["m_" + n for n in WEIGHTS] + ["v_" + n for n in WEIGHTS]
BIG = {"w_in": 2, "mla_w_qb": 2, "mla_w_kvb": 2, "w_branch": 3, "w_out": 1, "w_ffn_in": 2, "w_ffn_out": 1}
SMALL_SHARDED = {"b_gate": 2, "gla_w_gk2": 3, "gla_b_gk": 2, "w_dw": 2}
SMALL = ["c_ctx", "b_ada", "norm1_w", "norm2_w", "b_gate", "mla_q_norm_a", "mla_kv_norm_a", "mla_q_norm", "mla_k_norm",
         "gla_w_gk2", "gla_b_gk", "gla_o_norm", "ret_decay", "w_dw", "b_dw"]


def _pack(arrays, rows, dtype):
    flat = jnp.concatenate([a.reshape(-1).astype(dtype) for a in arrays])
    return jnp.pad(flat, (0, rows * PACK_W - flat.shape[0])).reshape(rows, PACK_W)


def _pack_rows(sizes, multiple):
    return -(-sum(sizes) // (PACK_W * multiple)) * multiple


def _unpack(flat, shapes):
    out, off = [], 0
    for shape in shapes:
        size = math.prod(shape)
        out.append(flat[off:off + size].reshape(shape))
        off += size
    return out


def _join_shards(stacked, axis):
    moved = jnp.moveaxis(stacked, 0, axis)
    shape = list(moved.shape)
    return moved.reshape(shape[:axis] + [shape[axis] * shape[axis + 1]] + shape[axis + 2:])


def _split_shards(full, axis):
    shape = list(full.shape)
    split = full.reshape(shape[:axis] + [N_DEV, shape[axis] // N_DEV] + shape[axis + 1:])
    return jnp.moveaxis(split, axis, 0)


def _gather_shards(local, axes, dtype, rows_multiple, name):
    names = list(axes)
    shapes = [local[n].shape for n in names]
    rows = _pack_rows([math.prod(s) for s in shapes], rows_multiple)
    gathered = all_gather(_pack([local[n] for n in names], rows, dtype), name).reshape(N_DEV, rows * PACK_W)
    full = {}
    off = 0
    for n, shape in zip(names, shapes):
        size = math.prod(shape)
        stacked = gathered[:, off:off + size].reshape((N_DEV,) + tuple(shape))
        full[n] = _join_shards(stacked, axes[n]).astype(F32)
        off += size
    return full


def kernel(*args):
    a = dict(zip(INPUTS, args))
    me = 4 * lax.axis_index("x") + 2 * lax.axis_index("y") + lax.axis_index("c")
    cols = a["w_ada"].shape[2]

    c_all = all_gather(jnp.pad(a["c"], ((0, 7), (0, 0))), "gather_c").reshape(N_DEV, 8, D)[:, 0]
    cond_in = jnp.concatenate([c_all, a["c_ctx"][None], jnp.zeros((ADA_ROWS - N_DEV - 1, D), F32)], axis=0)
    b_loc = lax.dynamic_slice_in_dim(a["b_ada"], me * cols, cols, axis=1)[:, None, :]
    mod_loc = ada_forward(cond_in, a["w_ada"], b_loc)
    mod_all = all_gather(mod_loc.reshape(DEPTH * ADA_ROWS, cols), "gather_mod")
    mod_all = mod_all.reshape(N_DEV, DEPTH, ADA_ROWS, cols).transpose(1, 2, 0, 3).reshape(DEPTH, ADA_ROWS, 6, D)
    mod_me = lax.dynamic_index_in_dim(mod_all, me, axis=1, keepdims=False)
    mod = jnp.pad(jnp.stack([mod_all[:, N_DEV], mod_me], axis=1), ((0, 0), (0, 0), (0, 2), (0, 0)))

    w = _gather_shards(a, BIG, BF16, 16, "gather_weights")
    w.update(_gather_shards(a, SMALL_SHARDED, F32, 8, "gather_small"))
    for n in SMALL:
        if n not in SMALL_SHARDED and n not in ("c_ctx", "b_ada"):
            w[n] = a[n]

    loss, (gw, gmod, gx) = jax.value_and_grad(local_loss, argnums=(0, 1, 2))(
        w, mod, a["x"][0], a["ctx"][0], a["loss_target"][0])
    loss = lax.psum(loss, ("x", "y", "c"))

    dmod_own = gmod[:, :, :6].reshape(2 * DEPTH, 6 * D)
    g_all = all_gather(jnp.pad(dmod_own, ((0, 8 - 2 * DEPTH), (0, 0))), "gather_dmod").reshape(N_DEV, 8, 6 * D)
    g_loc = lax.dynamic_slice_in_dim(g_all[:, :2 * DEPTH], me * cols, cols, axis=2).transpose(1, 0, 2)
    g_w_ada, g_c_ctx, g_b_ada = ada_backward(cond_in, g_loc, dmod_own, a["w_ada"])

    small_part = dict(gw, c_ctx=g_c_ctx, b_ada=g_b_ada)
    small_shapes = [a[n].shape if n not in SMALL_SHARDED else gw[n].shape for n in SMALL]
    rows = _pack_rows([math.prod(s) for s in small_shapes], PACK_TILE)
    parts = all_gather(_pack([small_part[n] for n in SMALL], rows, F32), "gather_small_grads")
    small_sum = _unpack(slab_sum(parts.reshape(N_DEV, rows, PACK_W), "sum_small_grads").reshape(-1), small_shapes)
    g_small = {}
    for n, g in zip(SMALL, small_sum):
        if n in SMALL_SHARDED:
            ax = SMALL_SHARDED[n]
            g = lax.dynamic_slice_in_dim(g, me * a[n].shape[ax], a[n].shape[ax], axis=ax)
        g_small[n] = g

    big_rows = _pack_rows([a[n].size for n in BIG], 16)
    slabs = jnp.concatenate([_split_shards(gw[n], ax).reshape(N_DEV, -1) for n, ax in BIG.items()], axis=1)
    slabs = jnp.pad(slabs, ((0, 0), (0, big_rows * PACK_W - slabs.shape[1]))).reshape(N_DEV, big_rows, PACK_W)
    landed = all_to_all(slabs, "scatter_grads")

    def update(names, g_slabs, rows, label):
        packs = [_pack([a[pre + n] for n in names], rows, F32) for pre in ("", "m_", "v_")]
        outs = adamw(g_slabs, *packs, label)
        return [dict(zip(names, _unpack(o.reshape(-1), [a[n].shape for n in names]))) for o in outs]

    res_big = update(list(BIG), landed, big_rows, "adamw_big")
    ada_rows = a["w_ada"].size // PACK_W
    res_ada = update(["w_ada"], g_w_ada.reshape(1, ada_rows, PACK_W), ada_rows, "adamw_ada")
    small_rows = _pack_rows([a[n].size for n in SMALL], PACK_TILE)
    res_small = update(SMALL, _pack([g_small[n] for n in SMALL], small_rows, F32)[None], small_rows, "adamw_small")

    outs = [loss, gx[None]]
    for k in range(4):
        merged = {**res_big[k], **res_ada[k], **res_small[k]}
        outs += [merged[n] for n in WEIGHTS]
    return tuple(outs)
```

```python
import functools
import math

import jax
import jax.numpy as jnp
import numpy as np
from jax import lax
from jax.experimental import pallas as pl
from jax.experimental.pallas import tpu as pltpu

F32 = jnp.float32
BF16 = jnp.bfloat16

N_DEV = 8
D = 1024
DEPTH = 2
GRID_W = 64
MLA_HEADS = 8
MLA_NOPE = 64
MLA_ROPE = 32
MLA_QK = 96
MLA_V = 64
MLA_Q_LORA = 256
MLA_KV_LORA = 128
GLA_HEADS = 4
GLA_DK = 128
GLA_RANK = 16
GLA_NORMALIZER = 16.0
RET_HEADS = 4
RET_DK = 128
BRANCH_W = 512
D_FF = 2816
CHUNK = 64
ROPE_THETA = 10000.0
RET_THETA = 10000.0
EPS = 1e-6
HEAD_PAD = 128
N_IN_PAD = 8192

ADAM_LR = 0.001
ADAM_B1 = 0.9
ADAM_B2 = 0.999
ADAM_EPS = 1e-08
ADAM_WD = 0.01
ADAM_STEP = 10

ROW_TILE = 256
SCAN_CHUNKS = ROW_TILE // CHUNK
VMEM_LIMIT_BYTES = 56 * 1024 * 1024
MESH = pl.DeviceIdType.MESH


def _cparams(n_axes):
    return pltpu.CompilerParams(dimension_semantics=("arbitrary",) * n_axes, vmem_limit_bytes=VMEM_LIMIT_BYTES)


def _pick(dim, cands):
    for cand in cands:
        if dim % cand == 0:
            return cand
    return dim


_DOT_DIMS = {"nn": (((1,), (0,)), ((), ())), "nt": (((1,), (1,)), ((), ())), "tn": (((0,), (0,)), ((), ()))}


def _dg(a, b, mode):
    return lax.dot_general(a.astype(BF16), b.astype(BF16), _DOT_DIMS[mode], preferred_element_type=F32)


def _bdot(a, b, mode):
    @jax.custom_vjp
    def f(a, b):
        return _dg(a, b, mode)

    def fwd(a, b):
        return _dg(a, b, mode), (a, b)

    def bwd(res, g):
        a, b = res
        if mode == "nn":
            return _dg(g, b, "nt").astype(a.dtype), _dg(a, g, "tn").astype(b.dtype)
        if mode == "nt":
            return _dg(g, b, "nn").astype(a.dtype), _dg(g, a, "tn").astype(b.dtype)
        return _dg(b, g, "nt").astype(a.dtype), _dg(a, g, "nn").astype(b.dtype)

    f.defvjp(fwd, bwd)
    return f(a, b)


def _roll(x, shift, axis):
    n = x.shape[axis]
    shift = shift % n

    @jax.custom_vjp
    def f(x):
        return pltpu.roll(x, shift, axis)

    def fwd(x):
        return pltpu.roll(x, shift, axis), None

    def bwd(_, g):
        return (pltpu.roll(g, (n - shift) % n, axis),)

    f.defvjp(fwd, bwd)
    return f(x)


def _tri_cumsum(x, forward):
    def mm(lower, v):
        rows = lax.broadcasted_iota(jnp.int32, (CHUNK, CHUNK), 0)
        cols = lax.broadcasted_iota(jnp.int32, (CHUNK, CHUNK), 1)
        m = ((rows >= cols) if lower else (rows <= cols)).astype(F32)
        return jnp.dot(m, v, precision=lax.Precision.HIGHEST, preferred_element_type=F32)

    @jax.custom_vjp
    def f(x):
        return mm(forward, x)

    def fwd(x):
        return mm(forward, x), None

    def bwd(_, g):
        return (mm(not forward, g),)

    f.defvjp(fwd, bwd)
    return f(x)


@jax.custom_jvp
def _log_sigmoid(x):
    return jnp.minimum(x, 0.0) - jnp.log(1.0 + jnp.exp(-jnp.abs(x)))


@_log_sigmoid.defjvp
def _log_sigmoid_jvp(primals, tangents):
    (x,), (t,) = primals, tangents
    return _log_sigmoid(x), t * jax.nn.sigmoid(-x)


def _rms(x, n, w=None):
    y = x * lax.rsqrt(jnp.sum(x * x, axis=-1, keepdims=True) * (1.0 / n) + EPS)
    return y if w is None else y * w


def _silu(x):
    return x * jax.nn.sigmoid(x)


def _gelu_tanh(x):
    return 0.5 * x * (1.0 + jnp.tanh(math.sqrt(2.0 / math.pi) * (x + 0.044715 * (x * x * x))))


def _mm(a, b, mode, name):
    if mode == "nn":
        (m, k), (_, n) = a.shape, b.shape
    elif mode == "nt":
        (m, k), (n, _) = a.shape, b.shape
    else:
        (k, m), (_, n) = a.shape, b.shape
    tm = _pick(m, (1024, 768, 1408, 512, 256, 128))
    tn = _pick(n, (1024, 1408, 512, 256, 128))
    tk = _pick(k, (1024, 768, 1408, 512, 256, 128))
    nk = k // tk
    if mode == "nn":
        a_spec = pl.BlockSpec((tm, tk), lambda i, j, kk: (i, kk))
        b_spec = pl.BlockSpec((tk, tn), lambda i, j, kk: (kk, j))
    elif mode == "nt":
        a_spec = pl.BlockSpec((tm, tk), lambda i, j, kk: (i, kk))
        b_spec = pl.BlockSpec((tn, tk), lambda i, j, kk: (j, kk))
    else:
        a_spec = pl.BlockSpec((tk, tm), lambda i, j, kk: (kk, i))
        b_spec = pl.BlockSpec((tk, tn), lambda i, j, kk: (kk, j))

    def body(a_ref, b_ref, o_ref, acc_ref):
        kk = pl.program_id(2)

        @pl.when(kk == 0)
        def _():
            acc_ref[...] = jnp.zeros_like(acc_ref)

        acc_ref[...] += _dg(a_ref[...], b_ref[...], mode)

        @pl.when(kk == nk - 1)
        def _():
            o_ref[...] = acc_ref[...]

    return pl.pallas_call(
        body, name=name, grid=(m // tm, n // tn, nk),
        in_specs=[a_spec, b_spec], out_specs=pl.BlockSpec((tm, tn), lambda i, j, kk: (i, j)),
        out_shape=jax.ShapeDtypeStruct((m, n), F32),
        scratch_shapes=[pltpu.VMEM((tm, tn), F32)],
        compiler_params=_cparams(3),
    )(a, b)


def linear(x, w, name):
    @jax.custom_vjp
    def op(x, w):
        return _mm(x, w.astype(BF16), "nn", name + "_f")

    def fwd(x, w):
        wb = w.astype(BF16)
        return _mm(x, wb, "nn", name + "_f"), (x, wb)

    def bwd(res, g):
        x, wb = res
        return _mm(g, wb, "nt", name + "_dx"), _mm(x, g, "tn", name + "_dw")

    op.defvjp(fwd, bwd)
    return op(x, w)


def rowwise(name, fn, rows, segs, params, out_widths, nct, diff_rows=None):
    n_row, n_seg, n_par, n_out = len(rows), len(segs), len(params), len(out_widths)
    diff_rows = [True] * n_row if diff_rows is None else list(diff_rows)
    r_total = rows[0].shape[0]
    n_tiles = r_total // ROW_TILE

    def seg_of(i):
        return jnp.where(i < nct, 0, 1)

    def row_spec(width):
        return pl.BlockSpec((ROW_TILE, width), lambda i: (i, 0))

    def seg_spec(shape):
        nd = len(shape)
        return pl.BlockSpec((1,) + tuple(shape[1:]), lambda i: (seg_of(i),) + (0,) * (nd - 1))

    def par_spec(shape):
        nd = len(shape)
        return pl.BlockSpec(tuple(shape), lambda i: (0,) * nd)

    in_specs = ([row_spec(r.shape[1]) for r in rows] + [seg_spec(s.shape) for s in segs]
                + [par_spec(p.shape) for p in params])

    def load(refs):
        vals = [r[...].astype(F32) for r in refs[:n_row]]
        vals += [r[0].astype(F32) for r in refs[n_row:n_row + n_seg]]
        vals += [r[...].astype(F32) for r in refs[n_row + n_seg:n_row + n_seg + n_par]]
        return vals

    def fwd_call(arrs):
        def body(*refs):
            outs = fn(*load(refs))
            for o_ref, val in zip(refs[n_row + n_seg + n_par:], outs):
                o_ref[...] = val

        return pl.pallas_call(
            body, name=name + "_f", grid=(n_tiles,), in_specs=in_specs,
            out_specs=[row_spec(w) for w in out_widths],
            out_shape=[jax.ShapeDtypeStruct((r_total, w), F32) for w in out_widths],
            compiler_params=_cparams(1),
        )(*arrs)

    d_idx = [k for k in range(n_row) if diff_rows[k]]

    def bwd_call(arrs, douts):
        n_in = n_row + n_seg + n_par

        def body(*refs):
            i = pl.program_id(0)
            vals = load(refs[:n_in])
            gs = [r[...] for r in refs[n_in:n_in + n_out]]
            out_refs = refs[n_in + n_out:]
            diff_pos = d_idx + list(range(n_row, n_in))

            def f(*dv):
                full = list(vals)
                for pos, v in zip(diff_pos, dv):
                    full[pos] = v
                return tuple(fn(*full))

            _, vjp = jax.vjp(f, *[vals[p] for p in diff_pos])
            grads = vjp(tuple(gs))
            nd = len(d_idx)
            for o_ref, g in zip(out_refs[:nd], grads[:nd]):
                o_ref[...] = g
            first_seg = jnp.logical_or(i == 0, i == nct)
            for o_ref, g in zip(out_refs[nd:nd + n_seg], grads[nd:nd + n_seg]):
                @pl.when(first_seg)
                def _(o_ref=o_ref, g=g):
                    o_ref[0] = g

                @pl.when(jnp.logical_not(first_seg))
                def _(o_ref=o_ref, g=g):
                    o_ref[0] += g
            for o_ref, g in zip(out_refs[nd + n_seg:], grads[nd + n_seg:]):
                @pl.when(i == 0)
                def _(o_ref=o_ref, g=g):
                    o_ref[...] = g

                @pl.when(i != 0)
                def _(o_ref=o_ref, g=g):
                    o_ref[...] += g

        out_specs = ([row_spec(rows[k].shape[1]) for k in d_idx] + [seg_spec(s.shape) for s in segs]
                     + [par_spec(p.shape) for p in params])
        out_shape = ([jax.ShapeDtypeStruct(rows[k].shape, F32) for k in d_idx]
                     + [jax.ShapeDtypeStruct(s.shape, F32) for s in segs]
                     + [jax.ShapeDtypeStruct(p.shape, F32) for p in params])
        return pl.pallas_call(
            body, name=name + "_b", grid=(n_tiles,),
            in_specs=in_specs + [row_spec(w) for w in out_widths],
            out_specs=out_specs, out_shape=out_shape, compiler_params=_cparams(1),
        )(*arrs, *douts)

    @jax.custom_vjp
    def op(*arrs):
        return tuple(fwd_call(arrs))

    def op_fwd(*arrs):
        return tuple(fwd_call(arrs)), arrs

    def op_bwd(arrs, douts):
        grads = list(bwd_call(arrs, douts))
        nd = len(d_idx)
        row_grads = [jnp.zeros_like(arrs[k]) for k in range(n_row)]
        for k, g in zip(d_idx, grads[:nd]):
            row_grads[k] = g
        return tuple(row_grads + grads[nd:])

    op.defvjp(op_fwd, op_bwd)
    return op(*rows, *segs, *params)


ATT_SCALE = MLA_QK ** -0.5


def _attn_fwd_call(q, k, v, tc, name):
    r_total = q.shape[0]
    nq = r_total // ROW_TILE
    nctq = tc // ROW_TILE

    def body(q_ref, k_ref, v_ref, o_ref, lse_ref):
        i = pl.program_id(1)
        qv = q_ref[...]

        def run(nk):
            s = lax.dot_general(qv, k_ref[0:nk, :], _DOT_DIMS["nt"], preferred_element_type=F32) * ATT_SCALE
            m = jnp.max(s, axis=-1, keepdims=True)
            p = jnp.exp(s - m)
            l = jnp.sum(p, axis=-1, keepdims=True)
            o = lax.dot_general(p.astype(BF16), v_ref[0:nk, :], _DOT_DIMS["nn"], preferred_element_type=F32)
            o_ref[...] = o / l
            lse_ref[...] = jnp.broadcast_to(m + jnp.log(l), (ROW_TILE, HEAD_PAD))

        @pl.when(i < nctq)
        def _():
            run(tc)

        @pl.when(i >= nctq)
        def _():
            run(r_total)

    q_spec = pl.BlockSpec((ROW_TILE, HEAD_PAD), lambda h, i: (i, h))
    kv_spec = pl.BlockSpec((r_total, HEAD_PAD), lambda h, i: (0, h))
    return pl.pallas_call(
        body, name=name, grid=(MLA_HEADS, nq), in_specs=[q_spec, kv_spec, kv_spec],
        out_specs=[q_spec, q_spec],
        out_shape=[jax.ShapeDtypeStruct(q.shape, F32), jax.ShapeDtypeStruct(q.shape, F32)],
        compiler_params=_cparams(2),
    )(q, k, v)


def _attn_bwd_call(q, k, v, o, lse, do, tc, name):
    r_total = q.shape[0]
    nq = r_total // ROW_TILE
    nctq = tc // ROW_TILE
    kc = _pick(r_total, (768, 512, 256))

    def body(q_ref, k_ref, v_ref, o_ref, lse_ref, do_ref, dq_ref, dk_ref, dv_ref):
        i = pl.program_id(1)

        @pl.when(i == 0)
        def _():
            dk_ref[...] = jnp.zeros_like(dk_ref)
            dv_ref[...] = jnp.zeros_like(dv_ref)

        qv = q_ref[...]
        dov = do_ref[...]
        dob = dov.astype(BF16)
        lse = lse_ref[:, 0:1]
        delta = jnp.sum(dov * o_ref[...], axis=-1, keepdims=True)

        def chunk(start, size, dq):
            kk = k_ref[pl.ds(start, size), :]
            vv = v_ref[pl.ds(start, size), :]
            s = lax.dot_general(qv, kk, _DOT_DIMS["nt"], preferred_element_type=F32) * ATT_SCALE
            p = jnp.exp(s - lse)
            dp = lax.dot_general(dob, vv, _DOT_DIMS["nt"], preferred_element_type=F32)
            ds = (p * (dp - delta) * ATT_SCALE).astype(BF16)
            dk_ref[pl.ds(start, size), :] += lax.dot_general(ds, qv, _DOT_DIMS["tn"], preferred_element_type=F32)
            dv_ref[pl.ds(start, size), :] += lax.dot_general(p.astype(BF16), dob, _DOT_DIMS["tn"],
                                                             preferred_element_type=F32)
            return dq + lax.dot_general(ds, kk, _DOT_DIMS["nn"], preferred_element_type=F32)

        zero = jnp.zeros((ROW_TILE, HEAD_PAD), F32)

        @pl.when(i < nctq)
        def _():
            dq = zero
            for c in range(tc // ROW_TILE):
                dq = chunk(c * ROW_TILE, ROW_TILE, dq)
            dq_ref[...] = dq

        @pl.when(i >= nctq)
        def _():
            dq_ref[...] = lax.fori_loop(0, r_total // kc, lambda c, dq: chunk(pl.multiple_of(c * kc, kc), kc, dq), zero)

    q_spec = pl.BlockSpec((ROW_TILE, HEAD_PAD), lambda h, i: (i, h))
    kv_spec = pl.BlockSpec((r_total, HEAD_PAD), lambda h, i: (0, h))
    return pl.pallas_call(
        body, name=name, grid=(MLA_HEADS, nq),
        in_specs=[q_spec, kv_spec, kv_spec, q_spec, q_spec, q_spec],
        out_specs=[q_spec, kv_spec, kv_spec],
        out_shape=[jax.ShapeDtypeStruct(q.shape, F32)] * 3,
        compiler_params=_cparams(2),
    )(q, k, v, o, lse, do)


def attention(q, k, v, tc, name):
    @jax.custom_vjp
    def op(q, k, v):
        return _attn_fwd_call(q.astype(BF16), k.astype(BF16), v.astype(BF16), tc, name + "_f")[0]

    def fwd(q, k, v):
        qb, kb, vb = q.astype(BF16), k.astype(BF16), v.astype(BF16)
        o, lse = _attn_fwd_call(qb, kb, vb, tc, name + "_f")
        return o, (qb, kb, vb, o, lse)

    def bwd(res, do):
        qb, kb, vb, o, lse = res
        return tuple(_attn_bwd_call(qb, kb, vb, o, lse, do, tc, name + "_b"))

    op.defvjp(fwd, bwd)
    return op(q, k, v)


def _chunk_masks(forward):
    rows = lax.broadcasted_iota(jnp.int32, (CHUNK, CHUNK), 0)
    cols = lax.broadcasted_iota(jnp.int32, (CHUNK, CHUNK), 1)
    return (rows >= cols) if forward else (rows < cols)


def _gla_chunk(forward, q, k, v, la, st0):
    cum = _tri_cumsum(la, forward)
    tot = jnp.sum(la, axis=0, keepdims=True)
    k_end = k * jnp.exp(tot - cum)
    st1 = st0 * jnp.exp(tot) + _bdot(v, k_end, "tn")
    q_dec = q * jnp.exp(cum)
    att = _bdot(q_dec, k * jnp.exp(-cum), "nt")
    att = jnp.where(_chunk_masks(forward), att, 0.0)
    o = _bdot(att, v, "nn") + _bdot(q_dec, st0, "nt")
    return o, st1


def _ret_chunk(forward, q, k, v, rd, st0):
    lg = -jnp.exp(rd[0:1, 0:1])
    rows = lax.broadcasted_iota(jnp.int32, (CHUNK, CHUNK), 0).astype(F32)
    cols = lax.broadcasted_iota(jnp.int32, (CHUNK, CHUNK), 1).astype(F32)
    pos = lax.broadcasted_iota(jnp.int32, (CHUNK, 1), 0).astype(F32)
    if forward:
        to_end, from_start, rel = CHUNK - 1.0 - pos, pos + 1.0, rows - cols
    else:
        to_end, from_start, rel = pos, CHUNK - pos, cols - rows
    mask = _chunk_masks(forward)
    dmat = jnp.where(mask, jnp.exp(jnp.where(mask, rel, 0.0) * lg), 0.0)
    st1 = st0 * jnp.exp(CHUNK * lg) + _bdot(v, k * jnp.exp(to_end * lg), "tn")
    att = _bdot(q, k, "nt") * dmat
    o = _bdot(att, v, "nn") + _bdot(q, st0, "nt") * jnp.exp(from_start * lg)
    return o, st1


def _scan_block_fn(chunk_fn, forward):
    order = range(SCAN_CHUNKS) if forward else range(SCAN_CHUNKS - 1, -1, -1)

    def block(q, k, v, aux, st0):
        outs = [None] * SCAN_CHUNKS
        st = st0
        for c in order:
            sl = slice(c * CHUNK, (c + 1) * CHUNK)
            a = aux[sl] if aux.shape[0] == ROW_TILE else aux
            outs[c], st = chunk_fn(forward, q[sl], k[sl], v[sl], a, st)
        return jnp.concatenate(outs, axis=0), st

    return block


def scan(kind, forward, q, k, v, aux, tc, name):
    heads = q.shape[1] // HEAD_PAD
    r_total = q.shape[0]
    nblk = r_total // ROW_TILE
    nctb = tc // ROW_TILE
    block_fn = _scan_block_fn(_gla_chunk if kind == "gla" else _ret_chunk, forward)
    per_row_aux = kind == "gla"

    def blk(g):
        if forward:
            return g
        return jnp.where(g < nctb, nctb - 1 - g, nblk - 1 - (g - nctb))

    def specs(step_to_g):
        row = pl.BlockSpec((ROW_TILE, HEAD_PAD), lambda h, s: (blk(step_to_g(s)), h))
        aux_spec = row if per_row_aux else pl.BlockSpec((1, 8, HEAD_PAD), lambda h, s: (h, 0, 0))
        st = pl.BlockSpec((1, 1, HEAD_PAD, HEAD_PAD), lambda h, s: (h, step_to_g(s), 0, 0))
        return row, aux_spec, st

    def fwd_call(q, k, v, aux):
        row, aux_spec, st_spec = specs(lambda s: s)

        def body(q_ref, k_ref, v_ref, a_ref, o_ref, st0_ref, st_ref):
            @pl.when(pl.program_id(1) == 0)
            def _():
                st_ref[...] = jnp.zeros_like(st_ref)

            st0 = st_ref[...]
            st0_ref[0, 0] = st0
            a = a_ref[...] if per_row_aux else a_ref[0]
            o, st1 = block_fn(q_ref[...], k_ref[...], v_ref[...], a, st0)
            o_ref[...] = o
            st_ref[...] = st1

        return pl.pallas_call(
            body, name=name + "_f", grid=(heads, nblk), in_specs=[row, row, row, aux_spec],
            out_specs=[row, st_spec],
            out_shape=[jax.ShapeDtypeStruct(q.shape, F32),
                       jax.ShapeDtypeStruct((heads, nblk, HEAD_PAD, HEAD_PAD), F32)],
            scratch_shapes=[pltpu.VMEM((HEAD_PAD, HEAD_PAD), F32)],
            compiler_params=_cparams(2),
        )(q, k, v, aux)

    def bwd_call(q, k, v, aux, st0s, do):
        row, aux_spec, st_spec = specs(lambda s: nblk - 1 - s)

        def body(q_ref, k_ref, v_ref, a_ref, st0_ref, do_ref, dq_ref, dk_ref, dv_ref, da_ref, dst_ref):
            s = pl.program_id(1)

            @pl.when(s == 0)
            def _():
                dst_ref[...] = jnp.zeros_like(dst_ref)

            a = a_ref[...] if per_row_aux else a_ref[0]
            _, vjp = jax.vjp(block_fn, q_ref[...], k_ref[...], v_ref[...], a, st0_ref[0, 0])
            dq, dk, dv, da, dst0 = vjp((do_ref[...], dst_ref[...]))
            dq_ref[...] = dq
            dk_ref[...] = dk
            dv_ref[...] = dv
            dst_ref[...] = dst0
            if per_row_aux:
                da_ref[...] = da
            else:
                @pl.when(s == 0)
                def _():
                    da_ref[0] = da

                @pl.when(s != 0)
                def _():
                    da_ref[0] += da

        return pl.pallas_call(
            body, name=name + "_b", grid=(heads, nblk),
            in_specs=[row, row, row, aux_spec, st_spec, row],
            out_specs=[row, row, row, aux_spec],
            out_shape=[jax.ShapeDtypeStruct(q.shape, F32)] * 3 + [jax.ShapeDtypeStruct(aux.shape, F32)],
            scratch_shapes=[pltpu.VMEM((HEAD_PAD, HEAD_PAD), F32)],
            compiler_params=_cparams(2),
        )(q, k, v, aux, st0s, do)

    @jax.custom_vjp
    def op(q, k, v, aux):
        return fwd_call(q, k, v, aux)[0]

    def fwd(q, k, v, aux):
        o, st0s = fwd_call(q, k, v, aux)
        return o, (q, k, v, aux, st0s)

    def bwd(res, do):
        return tuple(bwd_call(*res, do))

    op.defvjp(fwd, bwd)
    return op(q, k, v, aux)


HALO = 8


def _neighbours(main, prev8, next8, i, nct, n_tiles):
    has_prev = jnp.logical_and(i != 0, i != nct).astype(F32)
    has_next = jnp.logical_and(i != nct - 1, i != n_tiles - 1).astype(F32)
    row = lax.broadcasted_iota(jnp.int32, main.shape, 0)
    down = jnp.where(row == 0, prev8[HALO - 1:HALO] * has_prev, pltpu.roll(main, 1, 0))
    up = jnp.where(row == ROW_TILE - 1, next8[0:1] * has_next, pltpu.roll(main, ROW_TILE - 1, 0))
    return down, up


def dwconv(x, w8, b, tc, name):
    r_total, width = x.shape
    n_tiles = r_total // ROW_TILE
    nct = tc // ROW_TILE
    per = ROW_TILE // HALO
    main_spec = pl.BlockSpec((ROW_TILE, width), lambda i: (i, 0))
    prev_spec = pl.BlockSpec((HALO, width), lambda i: (jnp.maximum(i * per - 1, 0), 0))
    next_spec = pl.BlockSpec((HALO, width), lambda i: (jnp.minimum((i + 1) * per, r_total // HALO - 1), 0))
    w_spec = pl.BlockSpec((8, width), lambda i: (0, 0))
    b_spec = pl.BlockSpec((1, width), lambda i: (0, 0))

    def fwd_call(x, w8, b):
        def body(x_ref, p_ref, n_ref, w_ref, b_ref, o_ref):
            xv = x_ref[...]
            down, up = _neighbours(xv, p_ref[...], n_ref[...], pl.program_id(0), nct, n_tiles)
            o_ref[...] = w_ref[0:1] * down + w_ref[1:2] * xv + w_ref[2:3] * up + b_ref[...]

        return pl.pallas_call(
            body, name=name + "_f", grid=(n_tiles,), in_specs=[main_spec, prev_spec, next_spec, w_spec, b_spec],
            out_specs=main_spec, out_shape=jax.ShapeDtypeStruct(x.shape, F32), compiler_params=_cparams(1),
        )(x, x, x, w8, b)

    def bwd_call(x, w8, g):
        def body(x_ref, xp_ref, xn_ref, g_ref, gp_ref, gn_ref, w_ref, dx_ref, dw_ref, db_ref):
            i = pl.program_id(0)
            xv, gv = x_ref[...], g_ref[...]
            x_down, x_up = _neighbours(xv, xp_ref[...], xn_ref[...], i, nct, n_tiles)
            g_down, g_up = _neighbours(gv, gp_ref[...], gn_ref[...], i, nct, n_tiles)
            dx_ref[...] = w_ref[0:1] * g_up + w_ref[1:2] * gv + w_ref[2:3] * g_down
            dw = jnp.concatenate([jnp.sum(gv * x_down, axis=0, keepdims=True),
                                  jnp.sum(gv * xv, axis=0, keepdims=True),
                                  jnp.sum(gv * x_up, axis=0, keepdims=True),
                                  jnp.zeros((5, width), F32)], axis=0)
            db = jnp.sum(gv, axis=0, keepdims=True)

            @pl.when(i == 0)
            def _():
                dw_ref[...] = dw
                db_ref[...] = db

            @pl.when(i != 0)
            def _():
                dw_ref[...] += dw
                db_ref[...] += db

        return pl.pallas_call(
            body, name=name + "_b", grid=(n_tiles,),
            in_specs=[main_spec, prev_spec, next_spec, main_spec, prev_spec, next_spec, w_spec],
            out_specs=[main_spec, w_spec, b_spec],
            out_shape=[jax.ShapeDtypeStruct(x.shape, F32), jax.ShapeDtypeStruct((8, width), F32),
                       jax.ShapeDtypeStruct((1, width), F32)],
            compiler_params=_cparams(1),
        )(x, x, x, g, g, g, w8)

    @jax.custom_vjp
    def op(x, w8, b):
        return fwd_call(x, w8, b)

    def fwd(x, w8, b):
        return fwd_call(x, w8, b), (x, w8)

    def bwd(res, g):
        return tuple(bwd_call(*res, g))

    op.defvjp(fwd, bwd)
    return op(x, w8, b)


def loss_head(h, target, tc, name):
    r_total, width = h.shape
    n_tiles = r_total // ROW_TILE
    nct = tc // ROW_TILE

    def call(h, target):
        def body(h_ref, t_ref, dh_ref, loss_ref, acc_ref):
            i = pl.program_id(0)

            @pl.when(i == 0)
            def _():
                acc_ref[...] = jnp.zeros_like(acc_ref)

            @pl.when(i < nct)
            def _():
                dh_ref[...] = jnp.zeros_like(dh_ref)

            @pl.when(i >= nct)
            def _():
                err = h_ref[...] - t_ref[...]
                dh_ref[...] = err * (1.0 / width)
                acc_ref[...] += jnp.sum((err * err).reshape(ROW_TILE // 8, 8, width), axis=0)

            @pl.when(i == n_tiles - 1)
            def _():
                loss_ref[...] = jnp.sum(acc_ref[...]).reshape(1, 1) * (0.5 / width)

        row = pl.BlockSpec((ROW_TILE, width), lambda i: (i, 0))
        return pl.pallas_call(
            body, name=name, grid=(n_tiles,),
            in_specs=[row, pl.BlockSpec((ROW_TILE, width), lambda i: (jnp.maximum(i - nct, 0), 0))],
            out_specs=[row, pl.BlockSpec((1, 1), lambda i: (0, 0))],
            out_shape=[jax.ShapeDtypeStruct(h.shape, F32), jax.ShapeDtypeStruct((1, 1), F32)],
            scratch_shapes=[pltpu.VMEM((8, width), F32)], compiler_params=_cparams(1),
        )(h, target)

    @jax.custom_vjp
    def op(h, target):
        return call(h, target)[1][0, 0]

    def fwd(h, target):
        dh, loss = call(h, target)
        return loss[0, 0], (dh, target)

    def bwd(res, g):
        dh, target = res
        return dh * g, jnp.zeros_like(target)

    op.defvjp(fwd, bwd)
    return op(h, target)


PACK_W = 1024
PACK_TILE = 128


def slab_sum(slabs, name):
    n_slab, n, _ = slabs.shape

    def body(s_ref, o_ref):
        acc = s_ref[0]
        for j in range(1, n_slab):
            acc = acc + s_ref[j]
        o_ref[...] = acc

    return pl.pallas_call(
        body, name=name, grid=(n // PACK_TILE,),
        in_specs=[pl.BlockSpec((n_slab, PACK_TILE, PACK_W), lambda i: (0, i, 0))],
        out_specs=pl.BlockSpec((PACK_TILE, PACK_W), lambda i: (i, 0)),
        out_shape=jax.ShapeDtypeStruct((n, PACK_W), F32), compiler_params=_cparams(1),
    )(slabs)


def adamw(g_slabs, w, m, v, name):
    n_slab, n, _ = g_slabs.shape

    def body(g_ref, w_ref, m_ref, v_ref, go_ref, d_ref, mo_ref, vo_ref):
        g = g_ref[0].astype(F32)
        for j in range(1, n_slab):
            g = g + g_ref[j].astype(F32)
        m_new = ADAM_B1 * m_ref[...] + (1.0 - ADAM_B1) * g
        v_new = ADAM_B2 * v_ref[...] + (1.0 - ADAM_B2) * (g * g)
        m_hat = m_new / (1.0 - ADAM_B1 ** ADAM_STEP)
        v_hat = v_new / (1.0 - ADAM_B2 ** ADAM_STEP)
        go_ref[...] = g
        d_ref[...] = -ADAM_LR * (m_hat / (jnp.sqrt(v_hat) + ADAM_EPS) + ADAM_WD * w_ref[...])
        mo_ref[...] = m_new
        vo_ref[...] = v_new

    flat = pl.BlockSpec((PACK_TILE, PACK_W), lambda i: (i, 0))
    return pl.pallas_call(
        body, name=name, grid=(n // PACK_TILE,),
        in_specs=[pl.BlockSpec((n_slab, PACK_TILE, PACK_W), lambda i: (0, i, 0)), flat, flat, flat],
        out_specs=[flat] * 4, out_shape=[jax.ShapeDtypeStruct((n, PACK_W), F32)] * 4, compiler_params=_cparams(1),
    )(g_slabs, w, m, v)


def all_gather(x, name):
    m_per, n = x.shape

    def body(x_ref, out_ref, send_sems, recv_sems, local_sem):
        px, py, pc = lax.axis_index("x"), lax.axis_index("y"), lax.axis_index("c")
        me, sibling = (px, py, pc), (px, py, 1 - pc)
        chips = [(1 - px, py), (px, 1 - py), (1 - px, 1 - py)]

        def rows(bx, by, bc):
            return out_ref.at[pl.ds((4 * bx + 2 * by + bc) * m_per, m_per), :]

        def copy(k, block, to, src=None):
            return pltpu.make_async_remote_copy(
                src_ref=rows(*block) if src is None else src, dst_ref=rows(*block),
                send_sem=send_sems.at[k], recv_sem=recv_sems.at[k], device_id=to, device_id_type=MESH)

        mine = pltpu.make_async_copy(x_ref, rows(*me), local_sem)
        mine.start()
        first = [copy(0, me, sibling, src=x_ref)]
        first += [copy(1 + j, me, (*chip, pc), src=x_ref) for j, chip in enumerate(chips)]
        for cp in first:
            cp.start()
        passed = [copy(4 + j, (*chip, pc), sibling) for j, chip in enumerate(chips)]
        for j, chip in enumerate(chips):
            copy(1 + j, (*chip, pc), me).wait_recv()
            passed[j].start()
        copy(0, sibling, me).wait_recv()
        for j, chip in enumerate(chips):
            copy(4 + j, (*chip, 1 - pc), me).wait_recv()
        for cp in first + passed:
            cp.wait_send()
        mine.wait()

    return pl.pallas_call(
        body, name=name, out_shape=jax.ShapeDtypeStruct((N_DEV * m_per, n), x.dtype),
        in_specs=[pl.BlockSpec(memory_space=pl.ANY)], out_specs=pl.BlockSpec(memory_space=pl.ANY),
        scratch_shapes=[pltpu.SemaphoreType.DMA((7,)), pltpu.SemaphoreType.DMA((7,)), pltpu.SemaphoreType.DMA],
    )(x)


def all_to_all(x, name):
    _, n, cols = x.shape

    def body(x_ref, out_ref, send_sems, recv_sems, local_sem):
        px, py, pc = lax.axis_index("x"), lax.axis_index("y"), lax.axis_index("c")
        mine_idx = 4 * px + 2 * py + pc
        flips = [(fx, fy, fc) for fx in (0, 1) for fy in (0, 1) for fc in (0, 1)][1:]
        local = pltpu.make_async_copy(x_ref.at[mine_idx], out_ref.at[mine_idx], local_sem)
        local.start()
        copies = []
        for k, (fx, fy, fc) in enumerate(flips):
            qx, qy, qc = px ^ fx, py ^ fy, pc ^ fc
            peer_idx = 4 * qx + 2 * qy + qc
            copies.append((
                pltpu.make_async_remote_copy(
                    src_ref=x_ref.at[peer_idx], dst_ref=out_ref.at[mine_idx], send_sem=send_sems.at[k],
                    recv_sem=recv_sems.at[k], device_id=(qx, qy, qc), device_id_type=MESH),
                pltpu.make_async_remote_copy(
                    src_ref=x_ref.at[peer_idx], dst_ref=out_ref.at[peer_idx], send_sem=send_sems.at[k],
                    recv_sem=recv_sems.at[k], device_id=(qx, qy, qc), device_id_type=MESH)))
        for send, _ in copies:
            send.start()
        for _, landing in copies:
            landing.wait_recv()
        for send, _ in copies:
            send.wait_send()
        local.wait()

    return pl.pallas_call(
        body, name=name, out_shape=jax.ShapeDtypeStruct(x.shape, x.dtype),
        in_specs=[pl.BlockSpec(memory_space=pl.ANY)], out_specs=pl.BlockSpec(memory_space=pl.ANY),
        scratch_shapes=[pltpu.SemaphoreType.DMA((7,)), pltpu.SemaphoreType.DMA((7,)), pltpu.SemaphoreType.DMA],
    )(x)


IN_OFFSETS = {}
_off = 0
for _name, _width in (("mla_q", 256), ("mla_kv", 128), ("mla_kr", 32), ("gla_q", 512), ("gla_k", 512), ("gla_v", 512),
                      ("gla_g", 512), ("gla_rf", 16), ("gla_rb", 16), ("ret_q", 512), ("ret_k", 512), ("ret_v", 512),
                      ("ret_g", 512), ("gate_mla", 1024), ("gate_gla", 1024), ("gate_ret", 1024)):
    IN_OFFSETS[_name] = (_off, _off + _width)
    _off += _width
N_IN = _off

P_GLA, P_RET, P_GATE, P_MLAQ, P_MLAKV, P_MLAKR, P_RANK, P_END = 0, 2048, 4096, 7168, 7424, 7552, 7680, 7808


def _pad_in_proj(w):
    def cols(a, b):
        return w[:, IN_OFFSETS[a][0]:IN_OFFSETS[b][1]]

    def z(n):
        return jnp.zeros((w.shape[0], n), w.dtype)

    return jnp.concatenate([cols("gla_q", "gla_g"), cols("ret_q", "ret_g"), cols("gate_mla", "gate_ret"),
                            cols("mla_q", "mla_kv"), z(MLA_NOPE), cols("mla_kr", "mla_kr"),
                            z(HEAD_PAD - MLA_QK), cols("gla_rf", "gla_rb"), z(HEAD_PAD - 2 * GLA_RANK),
                            z(N_IN_PAD - P_END)], axis=1)


def _pad_last(a, n):
    return jnp.pad(a, [(0, 0)] * (a.ndim - 1) + [(0, n - a.shape[-1])])


def _position_tables(tc, t):
    pos = jnp.arange(t)
    inv = ROPE_THETA ** (-jnp.arange(MLA_ROPE // 4, dtype=F32) * 2.0 / (MLA_ROPE // 2))
    ang_r = (pos // GRID_W).astype(F32)[:, None] * inv[None, :]
    ang_c = (pos % GRID_W).astype(F32)[:, None] * inv[None, :]
    z8, z32, z64 = jnp.zeros((t, 8), F32), jnp.zeros((t, 32), F32), jnp.zeros((t, 64), F32)
    lat_c = jnp.concatenate([jnp.ones((t, 64), F32), jnp.cos(ang_r), jnp.cos(ang_r), jnp.cos(ang_c), jnp.cos(ang_c),
                             z32], axis=1)
    lat_sn = jnp.concatenate([z64, -jnp.sin(ang_r), z8, -jnp.sin(ang_c), z8, z32], axis=1)
    lat_sp = jnp.concatenate([z64, z8, jnp.sin(ang_r), z8, jnp.sin(ang_c), z32], axis=1)
    ctx_c = jnp.concatenate([jnp.ones((tc, MLA_QK), F32), jnp.zeros((tc, HEAD_PAD - MLA_QK), F32)], axis=1)
    ctx_z = jnp.zeros((tc, HEAD_PAD), F32)
    rinv = 1.0 / (RET_THETA ** jnp.linspace(0.0, 1.0, RET_DK // 2, dtype=F32))
    rang = jnp.arange(tc + t).astype(F32)[:, None] * rinv[None, :]
    return dict(c=jnp.concatenate([ctx_c, lat_c]), sn=jnp.concatenate([ctx_z, lat_sn]),
                sp=jnp.concatenate([ctx_z, lat_sp]),
                rc=jnp.concatenate([jnp.cos(rang), jnp.cos(rang)], axis=1),
                rs=jnp.concatenate([-jnp.sin(rang), jnp.sin(rang)], axis=1))


def _heads(x):
    return [x[:, h * HEAD_PAD:(h + 1) * HEAD_PAD] for h in range(x.shape[1] // HEAD_PAD)]


def _mla_rope(x, c, sn, sp):
    return x * c + _roll(x, HEAD_PAD - 8, 1) * sn + _roll(x, 8, 1) * sp


def _norm_mod_fn(shift_row, scale_row):
    def fn(h, mod, w):
        return (_rms(h, D, w) * (1.0 + mod[scale_row:scale_row + 1]) + mod[shift_row:shift_row + 1],)
    return fn


def _resid_fn(gate_row):
    def fn(h, y, mod):
        return (h + mod[gate_row:gate_row + 1] * y,)
    return fn


def _q_fn(cq, c, sn, sp, norm_a, w_qb, q_norm):
    qf = _bdot(_rms(cq, MLA_Q_LORA, norm_a), w_qb, "nn")
    return (jnp.concatenate([_mla_rope(_rms(qh, MLA_QK, q_norm), c, sn, sp) for qh in _heads(qf)], axis=1),)


def _kv_fn(ckv, kr, c, sn, sp, norm_a, w_k, w_v, k_norm):
    x = _rms(ckv, MLA_KV_LORA, norm_a)
    kf = _bdot(x, w_k, "nn")
    k = jnp.concatenate([_mla_rope(_rms(kh + kr, MLA_QK, k_norm), c, sn, sp) for kh in _heads(kf)], axis=1)
    return k, _bdot(x, w_v, "nn")


def _decay_fn(ranks, w2, b):
    la = _log_sigmoid(_bdot(ranks, w2, "nn") + b) * (1.0 / GLA_NORMALIZER)
    return la[:, :GLA_HEADS * GLA_DK], la[:, GLA_HEADS * GLA_DK:]


def _gla_q_fn(q):
    return (q * (GLA_DK ** -0.5),)


def _ret_rot_fn(q, k, rc, rs):
    def rot(x, scale):
        return jnp.concatenate([(xh * rc + _roll(xh, RET_DK // 2, 1) * rs) * scale for xh in _heads(x)], axis=1)
    return rot(q, 1.0), rot(k, RET_DK ** -0.5)


def _gla_out_fn(o_f, o_b, g, w):
    y = jnp.concatenate([_rms(oh, HEAD_PAD, w) for oh in _heads(o_f + o_b)], axis=1)
    return (y * _silu(g),)


def _ret_out_fn(o_f, o_b, g):
    y = jnp.concatenate([_rms(oh, HEAD_PAD) for oh in _heads(o_f + o_b)], axis=1)
    return (y * _silu(g),)


def _merge_fn(z0, z1, z2, g0, g1, g2, bg):
    return (jax.nn.sigmoid(g0 + bg[0:1]) * z0 + jax.nn.sigmoid(g1 + bg[1:2]) * z1 + jax.nn.sigmoid(g2 + bg[2:3]) * z2,)


def _ffn_act_fn(c, up):
    return (_gelu_tanh(c) * up,)


def _layer(l, h, mod, w, tabs, tc):
    nct = tc // ROW_TILE
    tag = f"_l{l}"
    row = lambda a: a[l][None]
    a = rowwise("norm1" + tag, _norm_mod_fn(0, 1), [h], [mod], [row(w["norm1_w"])], [D], nct)[0]
    p = linear(a, _pad_in_proj(w["w_in"][l]), "in_proj" + tag)
    piece = lambda start, width: p[:, start:start + width]

    w_qb = _pad_last(w["mla_w_qb"][l].reshape(MLA_Q_LORA, MLA_HEADS, MLA_QK), HEAD_PAD).reshape(MLA_Q_LORA, -1)
    w_kvb = w["mla_w_kvb"][l].reshape(MLA_KV_LORA, MLA_HEADS, MLA_NOPE + MLA_V)
    w_k = _pad_last(w_kvb[:, :, :MLA_NOPE], HEAD_PAD).reshape(MLA_KV_LORA, -1)
    w_v = _pad_last(w_kvb[:, :, MLA_NOPE:], HEAD_PAD).reshape(MLA_KV_LORA, -1)
    rope = [tabs["c"], tabs["sn"], tabs["sp"]]
    q = rowwise("mla_q" + tag, _q_fn, [piece(P_MLAQ, MLA_Q_LORA)] + rope, [],
                [row(w["mla_q_norm_a"]), w_qb, _pad_last(row(w["mla_q_norm"]), HEAD_PAD)],
                [MLA_HEADS * HEAD_PAD], nct, diff_rows=[True, False, False, False])[0]
    k, v = rowwise("mla_kv" + tag, _kv_fn, [piece(P_MLAKV, MLA_KV_LORA), piece(P_MLAKR, HEAD_PAD)] + rope, [],
                   [row(w["mla_kv_norm_a"]), w_k, w_v, _pad_last(row(w["mla_k_norm"]), HEAD_PAD)],
                   [MLA_HEADS * HEAD_PAD] * 2, nct, diff_rows=[True, True, False, False, False])
    y_mla = attention(q, k, v, tc, "attn" + tag)
    wb_mla = _pad_last(w["w_branch"][l, 0].reshape(MLA_HEADS, MLA_V, D).transpose(0, 2, 1), HEAD_PAD)
    wb_mla = wb_mla.transpose(0, 2, 1).reshape(MLA_HEADS * HEAD_PAD, D)

    w2 = jnp.zeros((HEAD_PAD, 2 * GLA_HEADS * GLA_DK), F32)
    w2 = w2.at[:GLA_RANK, :GLA_HEADS * GLA_DK].set(w["gla_w_gk2"][l, 0])
    w2 = w2.at[GLA_RANK:2 * GLA_RANK, GLA_HEADS * GLA_DK:].set(w["gla_w_gk2"][l, 1])
    la_f, la_b = rowwise("gla_decay" + tag, _decay_fn, [piece(P_RANK, HEAD_PAD)], [],
                         [w2, w["gla_b_gk"][l].reshape(1, -1)], [GLA_HEADS * GLA_DK] * 2, nct)
    gq = rowwise("gla_q" + tag, _gla_q_fn, [piece(P_GLA, 512)], [], [], [512], nct)[0]
    gk, gv, gg = piece(P_GLA + 512, 512), piece(P_GLA + 1024, 512), piece(P_GLA + 1536, 512)
    o_f = scan("gla", True, gq, gk, gv, la_f, tc, "gla_fw" + tag)
    o_b = scan("gla", False, gq, gk, gv, la_b, tc, "gla_bw" + tag)
    y_gla = rowwise("gla_out" + tag, _gla_out_fn, [o_f, o_b, gg], [], [row(w["gla_o_norm"])], [512], nct)[0]

    rq, rk = rowwise("ret_rot" + tag, _ret_rot_fn, [piece(P_RET, 512), piece(P_RET + 512, 512), tabs["rc"], tabs["rs"]],
                     [], [], [512, 512], nct, diff_rows=[True, True, False, False])
    rv, rg = piece(P_RET + 1024, 512), piece(P_RET + 1536, 512)
    rd = jnp.broadcast_to(w["ret_decay"][l][:, :, None, None], (2, RET_HEADS, 8, HEAD_PAD))
    r_f = scan("ret", True, rq, rk, rv, rd[0], tc, "ret_fw" + tag)
    r_b = scan("ret", False, rq, rk, rv, rd[1], tc, "ret_bw" + tag)
    y_ret = rowwise("ret_out" + tag, _ret_out_fn, [r_f, r_b, rg], [], [], [512], nct)[0]

    z = [linear(y_mla, wb_mla, "branch_mla" + tag), linear(y_gla, w["w_branch"][l, 1], "branch_gla" + tag),
         linear(y_ret, w["w_branch"][l, 2], "branch_ret" + tag)]
    gates = [piece(P_GATE + n * D, D) for n in range(3)]
    merged = rowwise("merge" + tag, _merge_fn, z + gates, [], [_pad_rows(w["b_gate"][l], 8)], [D], nct)[0]
    y = linear(merged, w["w_out"][l], "w_out" + tag)
    h = rowwise("resid1" + tag, _resid_fn(2), [h, y], [mod], [], [D], nct)[0]

    a2 = rowwise("norm2" + tag, _norm_mod_fn(3, 4), [h], [mod], [row(w["norm2_w"])], [D], nct)[0]
    gate = linear(a2, w["w_ffn_in"][l][:, :D_FF], "ffn_gate" + tag)
    up = linear(a2, w["w_ffn_in"][l][:, D_FF:], "ffn_up" + tag)
    conv = dwconv(gate, _pad_rows(w["w_dw"][l], 8), row(w["b_dw"]), tc, "dwconv" + tag)
    u = rowwise("ffn_act" + tag, _ffn_act_fn, [conv, up], [], [], [D_FF], nct)[0]
    f = linear(u, w["w_ffn_out"][l], "ffn_out" + tag)
    return rowwise("resid2" + tag, _resid_fn(5), [h, f], [mod], [], [D], nct)[0]


def _pad_rows(a, n):
    return jnp.pad(a, [(0, n - a.shape[0])] + [(0, 0)] * (a.ndim - 1))


def local_loss(w, mod, x, ctx, target):
    tc, t = ctx.shape[0], x.shape[0]
    tabs = _position_tables(tc, t)
    h = jnp.concatenate([ctx, x], axis=0)
    for l in range(DEPTH):
        h = _layer(l, h, mod[l], w, tabs, tc)
    return loss_head(h, target, tc, "loss_head")


ADA_ROWS = 16


def ada_forward(cond_in, w_ada, b_loc):
    cols = w_ada.shape[2]

    def body(x_ref, w_ref, b_ref, o_ref):
        s = _silu(x_ref[...])
        for l in range(DEPTH):
            o_ref[l] = _dg(s, w_ref[l], "nn") + b_ref[l]

    return pl.pallas_call(
        body, name="ada_forward", out_shape=jax.ShapeDtypeStruct((DEPTH, ADA_ROWS, cols), F32),
        compiler_params=pltpu.CompilerParams(vmem_limit_bytes=VMEM_LIMIT_BYTES),
    )(cond_in, w_ada, b_loc)


def ada_backward(cond_in, g_loc, dmod_own, w_ada):
    cols = w_ada.shape[2]

    def body(x_ref, g_ref, own_ref, w_ref, gw_ref, dc_ref, gb_ref):
        x = x_ref[...]
        s = _silu(x)
        dcond = jnp.zeros((8, D), F32)
        for l in range(DEPTH):
            g_ctx = jnp.sum(g_ref[2 * l], axis=0, keepdims=True)
            g_rows = jnp.concatenate([g_ref[2 * l + 1], jnp.broadcast_to(g_ctx, (8, cols))], axis=0)
            keep = lax.broadcasted_iota(jnp.int32, (ADA_ROWS, cols), 0) <= N_DEV
            gw_ref[l] = _dg(s, jnp.where(keep, g_rows, 0.0), "tn")
            dcond = dcond + _dg(jnp.broadcast_to(g_ctx, (8, cols)), w_ref[l], "nt")
            gb_ref[l:l + 1, :] = own_ref[2 * l:2 * l + 1, :] + own_ref[2 * l + 1:2 * l + 2, :]
        xc = x[N_DEV:N_DEV + 1]
        sig = jax.nn.sigmoid(xc)
        dc_ref[...] = dcond[0:1] * (sig * (1.0 + xc * (1.0 - sig)))

    return pl.pallas_call(
        body, name="ada_backward",
        out_shape=[jax.ShapeDtypeStruct(w_ada.shape, F32), jax.ShapeDtypeStruct((1, D), F32),
                   jax.ShapeDtypeStruct((DEPTH, 6 * D), F32)],
        compiler_params=pltpu.CompilerParams(vmem_limit_bytes=VMEM_LIMIT_BYTES),
    )(cond_in, g_loc, dmod_own, w_ada)


WEIGHTS = ["c_ctx", "w_ada", "b_ada", "norm1_w", "norm2_w", "w_in", "b_gate", "mla_q_norm_a", "mla_w_qb",
           "mla_kv_norm_a", "mla_w_kvb", "mla_q_norm", "mla_k_norm", "gla_w_gk2", "gla_b_gk", "gla_o_norm",
           "ret_decay", "w_branch", "w_out", "w_ffn_in", "w_dw", "b_dw", "w_ffn_out"]
INPUTS = ["x", "c", "ctx"] + WEIGHTS + ["loss_target"] + ["m_" + n for n in WEIGHTS] + ["v_" + n for n in WEIGHTS]
BIG = {"w_in": 2, "mla_w_qb": 2, "mla_w_kvb": 2, "w_branch": 3, "w_out": 1, "w_ffn_in": 2, "w_ffn_out": 1}
SMALL_SHARDED = {"b_gate": 2, "gla_w_gk2": 3, "gla_b_gk": 2, "w_dw": 2}
SMALL = ["c_ctx", "b_ada", "norm1_w", "norm2_w", "b_gate", "mla_q_norm_a", "mla_kv_norm_a", "mla_q_norm", "mla_k_norm",
         "gla_w_gk2", "gla_b_gk", "gla_o_norm", "ret_decay", "w_dw", "b_dw"]


def _entry_rows(size, align):
    return -(-size // (PACK_W * align)) * align


def _pack(arrays, rows, dtype, align, lead=0):
    parts = []
    for a in arrays:
        head = a.shape[:lead]
        size = math.prod(a.shape[lead:])
        r = _entry_rows(size, align)
        if r * PACK_W == size:
            parts.append(a.astype(dtype).reshape(head + (r, PACK_W)))
        else:
            flat = jnp.pad(a.astype(dtype).reshape(head + (size,)), [(0, 0)] * lead + [(0, r * PACK_W - size)])
            parts.append(flat.reshape(head + (r, PACK_W)))
    used = sum(p.shape[lead] for p in parts)
    if rows > used:
        parts.append(jnp.zeros(parts[0].shape[:lead] + (rows - used, PACK_W), dtype))
    return jnp.concatenate(parts, axis=lead)


def _pack_rows(shapes, align, multiple):
    used = sum(_entry_rows(math.prod(s), align) for s in shapes)
    return -(-used // multiple) * multiple


def _unpack(pack, shapes, align):
    head = pack.shape[:-2]
    out, off = [], 0
    for shape in shapes:
        size = math.prod(shape)
        r = _entry_rows(size, align)
        block = lax.slice_in_dim(pack, off, off + r, axis=len(head))
        if r * PACK_W != size:
            block = block.reshape(head + (r * PACK_W,))[..., :size]
        out.append(block.reshape(head + tuple(shape)))
        off += r
    return out


def _join_shards(stacked, axis):
    moved = jnp.moveaxis(stacked, 0, axis)
    shape = list(moved.shape)
    return moved.reshape(shape[:axis] + [shape[axis] * shape[axis + 1]] + shape[axis + 2:])


def _split_shards(full, axis):
    shape = list(full.shape)
    split = full.reshape(shape[:axis] + [N_DEV, shape[axis] // N_DEV] + shape[axis + 1:])
    return jnp.moveaxis(split, axis, 0)


def _gather_shards(local, axes, dtype, rows_multiple, name):
    names = list(axes)
    shapes = [local[n].shape for n in names]
    rows = _pack_rows(shapes, rows_multiple, rows_multiple)
    gathered = all_gather(_pack([local[n] for n in names], rows, dtype, rows_multiple), name)
    stacked = _unpack(gathered.reshape(N_DEV, rows, PACK_W), shapes, rows_multiple)
    return {n: _join_shards(s, axes[n]).astype(F32) for n, s in zip(names, stacked)}


def kernel(*args):
    a = dict(zip(INPUTS, args))
    me = 4 * lax.axis_index("x") + 2 * lax.axis_index("y") + lax.axis_index("c")
    cols = a["w_ada"].shape[2]

    c_all = all_gather(jnp.pad(a["c"], ((0, 7), (0, 0))), "gather_c").reshape(N_DEV, 8, D)[:, 0]
    cond_in = jnp.concatenate([c_all, a["c_ctx"][None], jnp.zeros((ADA_ROWS - N_DEV - 1, D), F32)], axis=0)
    b_loc = lax.dynamic_slice_in_dim(a["b_ada"], me * cols, cols, axis=1)[:, None, :]
    mod_loc = ada_forward(cond_in, a["w_ada"], b_loc)
    mod_all = all_gather(mod_loc.reshape(DEPTH * ADA_ROWS, cols), "gather_mod")
    mod_all = mod_all.reshape(N_DEV, DEPTH, ADA_ROWS, cols).transpose(1, 2, 0, 3).reshape(DEPTH, ADA_ROWS, 6, D)
    mod_me = lax.dynamic_index_in_dim(mod_all, me, axis=1, keepdims=False)
    mod = jnp.pad(jnp.stack([mod_all[:, N_DEV], mod_me], axis=1), ((0, 0), (0, 0), (0, 2), (0, 0)))

    w = _gather_shards(a, BIG, BF16, 16, "gather_weights")
    w.update(_gather_shards(a, SMALL_SHARDED, F32, 8, "gather_small"))
    for n in SMALL:
        if n not in SMALL_SHARDED and n not in ("c_ctx", "b_ada"):
            w[n] = a[n]

    loss, (gw, gmod, gx) = jax.value_and_grad(local_loss, argnums=(0, 1, 2))(
        w, mod, a["x"][0], a["ctx"][0], a["loss_target"][0])
    loss = lax.psum(loss, ("x", "y", "c"))

    dmod_own = gmod[:, :, :6].reshape(2 * DEPTH, 6 * D)
    g_all = all_gather(jnp.pad(dmod_own, ((0, 8 - 2 * DEPTH), (0, 0))), "gather_dmod").reshape(N_DEV, 8, 6 * D)
    g_loc = lax.dynamic_slice_in_dim(g_all[:, :2 * DEPTH], me * cols, cols, axis=2).transpose(1, 0, 2)
    g_w_ada, g_c_ctx, g_b_ada = ada_backward(cond_in, g_loc, dmod_own, a["w_ada"])

    small_part = dict(gw, c_ctx=g_c_ctx, b_ada=g_b_ada)
    small_shapes = [a[n].shape if n not in SMALL_SHARDED else gw[n].shape for n in SMALL]
    rows = _pack_rows(small_shapes, 8, PACK_TILE)
    parts = all_gather(_pack([small_part[n] for n in SMALL], rows, F32, 8), "gather_small_grads")
    small_sum = _unpack(slab_sum(parts.reshape(N_DEV, rows, PACK_W), "sum_small_grads"), small_shapes, 8)
    g_small = {}
    for n, g in zip(SMALL, small_sum):
        if n in SMALL_SHARDED:
            ax = SMALL_SHARDED[n]
            g = lax.dynamic_slice_in_dim(g, me * a[n].shape[ax], a[n].shape[ax], axis=ax)
        g_small[n] = g

    big_rows = _pack_rows([a[n].shape for n in BIG], 16, PACK_TILE)
    slabs = _pack([_split_shards(gw[n], ax) for n, ax in BIG.items()], big_rows, BF16, 16, lead=1)
    landed = all_to_all(slabs, "scatter_grads")

    def update(names, g_slabs, rows, align, label):
        shapes = [a[n].shape for n in names]
        packs = [_pack([a[pre + n] for n in names], rows, F32, align) for pre in ("", "m_", "v_")]
        outs = adamw(g_slabs, *packs, label)
        return [dict(zip(names, _unpack(o, shapes, align))) for o in outs]

    res_big = update(list(BIG), landed, big_rows, 16, "adamw_big")
    ada_rows = _pack_rows([a["w_ada"].shape], 8, PACK_TILE)
    res_ada = update(["w_ada"], _pack([g_w_ada], ada_rows, F32, 8)[None], ada_rows, 8, "adamw_ada")
    small_rows = _pack_rows([a[n].shape for n in SMALL], 8, PACK_TILE)
    res_small = update(SMALL, _pack([g_small[n] for n in SMALL], small_rows, F32, 8)[None], small_rows, 8,
                       "adamw_small")

    outs = [loss, gx[None]]
    for k in range(4):
        merged = {**res_big[k], **res_ada[k], **res_small[k]}
        outs += [merged[n] for n in WEIGHTS]
    return tuple(outs)
```

```python
import functools
import math

import jax
import jax.numpy as jnp
import numpy as np
from jax import lax
from jax.experimental import pallas as pl
from jax.experimental.pallas import tpu as pltpu

F32 = jnp.float32
BF16 = jnp.bfloat16

N_DEV = 8
D = 1024
DEPTH = 2
GRID_W = 64
MLA_HEADS = 8
MLA_NOPE = 64
MLA_ROPE = 32
MLA_QK = 96
MLA_V = 64
MLA_Q_LORA = 256
MLA_KV_LORA = 128
GLA_HEADS = 4
GLA_DK = 128
GLA_RANK = 16
GLA_NORMALIZER = 16.0
RET_HEADS = 4
RET_DK = 128
BRANCH_W = 512
D_FF = 2816
CHUNK = 64
ROPE_THETA = 10000.0
RET_THETA = 10000.0
EPS = 1e-6
HEAD_PAD = 128
N_IN_PAD = 8192

ADAM_LR = 0.001
ADAM_B1 = 0.9
ADAM_B2 = 0.999
ADAM_EPS = 1e-08
ADAM_WD = 0.01
ADAM_STEP = 10

ROW_TILE = 256
SCAN_CHUNKS = ROW_TILE // CHUNK
VMEM_LIMIT_BYTES = 56 * 1024 * 1024
MESH = pl.DeviceIdType.MESH


def _cparams(n_axes):
    return pltpu.CompilerParams(dimension_semantics=("arbitrary",) * n_axes, vmem_limit_bytes=VMEM_LIMIT_BYTES)


def _pick(dim, cands):
    for cand in cands:
        if dim % cand == 0:
            return cand
    return dim


_DOT_DIMS = {"nn": (((1,), (0,)), ((), ())), "nt": (((1,), (1,)), ((), ())), "tn": (((0,), (0,)), ((), ()))}


def _dg(a, b, mode):
    return lax.dot_general(a.astype(BF16), b.astype(BF16), _DOT_DIMS[mode], preferred_element_type=F32)


def _bdot(a, b, mode):
    @jax.custom_vjp
    def f(a, b):
        return _dg(a, b, mode)

    def fwd(a, b):
        return _dg(a, b, mode), (a, b)

    def bwd(res, g):
        a, b = res
        if mode == "nn":
            return _dg(g, b, "nt").astype(a.dtype), _dg(a, g, "tn").astype(b.dtype)
        if mode == "nt":
            return _dg(g, b, "nn").astype(a.dtype), _dg(g, a, "tn").astype(b.dtype)
        return _dg(b, g, "nt").astype(a.dtype), _dg(a, g, "nn").astype(b.dtype)

    f.defvjp(fwd, bwd)
    return f(a, b)


def _roll(x, shift, axis):
    n = x.shape[axis]
    shift = shift % n

    @jax.custom_vjp
    def f(x):
        return pltpu.roll(x, shift, axis)

    def fwd(x):
        return pltpu.roll(x, shift, axis), None

    def bwd(_, g):
        return (pltpu.roll(g, (n - shift) % n, axis),)

    f.defvjp(fwd, bwd)
    return f(x)


def _tri_cumsum(x, forward):
    def mm(lower, v):
        rows = lax.broadcasted_iota(jnp.int32, (CHUNK, CHUNK), 0)
        cols = lax.broadcasted_iota(jnp.int32, (CHUNK, CHUNK), 1)
        m = ((rows >= cols) if lower else (rows <= cols)).astype(F32)
        return jnp.dot(m, v, precision=lax.Precision.HIGHEST, preferred_element_type=F32)

    @jax.custom_vjp
    def f(x):
        return mm(forward, x)

    def fwd(x):
        return mm(forward, x), None

    def bwd(_, g):
        return (mm(not forward, g),)

    f.defvjp(fwd, bwd)
    return f(x)


@jax.custom_jvp
def _log_sigmoid(x):
    return jnp.minimum(x, 0.0) - jnp.log(1.0 + jnp.exp(-jnp.abs(x)))


@_log_sigmoid.defjvp
def _log_sigmoid_jvp(primals, tangents):
    (x,), (t,) = primals, tangents
    return _log_sigmoid(x), t * jax.nn.sigmoid(-x)


def _rms(x, n, w=None):
    y = x * lax.rsqrt(jnp.sum(x * x, axis=-1, keepdims=True) * (1.0 / n) + EPS)
    return y if w is None else y * w


def _silu(x):
    return x * jax.nn.sigmoid(x)


def _gelu_tanh(x):
    return 0.5 * x * (1.0 + jnp.tanh(math.sqrt(2.0 / math.pi) * (x + 0.044715 * (x * x * x))))


def _mm(a, b, mode, name):
    if mode == "nn":
        (m, k), (_, n) = a.shape, b.shape
    elif mode == "nt":
        (m, k), (n, _) = a.shape, b.shape
    else:
        (k, m), (_, n) = a.shape, b.shape
    tm = _pick(m, (1024, 768, 1408, 512, 256, 128))
    tn = _pick(n, (1024, 1408, 512, 256, 128))
    tk = _pick(k, (1024, 768, 1408, 512, 256, 128))
    nk = k // tk
    if mode == "nn":
        a_spec = pl.BlockSpec((tm, tk), lambda i, j, kk: (i, kk))
        b_spec = pl.BlockSpec((tk, tn), lambda i, j, kk: (kk, j))
    elif mode == "nt":
        a_spec = pl.BlockSpec((tm, tk), lambda i, j, kk: (i, kk))
        b_spec = pl.BlockSpec((tn, tk), lambda i, j, kk: (j, kk))
    else:
        a_spec = pl.BlockSpec((tk, tm), lambda i, j, kk: (kk, i))
        b_spec = pl.BlockSpec((tk, tn), lambda i, j, kk: (kk, j))

    def body(a_ref, b_ref, o_ref, acc_ref):
        kk = pl.program_id(2)

        @pl.when(kk == 0)
        def _():
            acc_ref[...] = jnp.zeros_like(acc_ref)

        acc_ref[...] += _dg(a_ref[...], b_ref[...], mode)

        @pl.when(kk == nk - 1)
        def _():
            o_ref[...] = acc_ref[...]

    return pl.pallas_call(
        body, name=name, grid=(m // tm, n // tn, nk),
        in_specs=[a_spec, b_spec], out_specs=pl.BlockSpec((tm, tn), lambda i, j, kk: (i, j)),
        out_shape=jax.ShapeDtypeStruct((m, n), F32),
        scratch_shapes=[pltpu.VMEM((tm, tn), F32)],
        compiler_params=_cparams(3),
    )(a, b)


def linear(x, w, name):
    @jax.custom_vjp
    def op(x, w):
        return _mm(x, w.astype(BF16), "nn", name + "_f")

    def fwd(x, w):
        wb = w.astype(BF16)
        return _mm(x, wb, "nn", name + "_f"), (x, wb)

    def bwd(res, g):
        x, wb = res
        return _mm(g, wb, "nt", name + "_dx"), _mm(x, g, "tn", name + "_dw")

    op.defvjp(fwd, bwd)
    return op(x, w)


def rowwise(name, fn, rows, segs, params, out_widths, nct, diff_rows=None):
    n_row, n_seg, n_par, n_out = len(rows), len(segs), len(params), len(out_widths)
    diff_rows = [True] * n_row if diff_rows is None else list(diff_rows)
    r_total = rows[0].shape[0]
    n_tiles = r_total // ROW_TILE

    def seg_of(i):
        return jnp.where(i < nct, 0, 1)

    def row_spec(width):
        return pl.BlockSpec((ROW_TILE, width), lambda i: (i, 0))

    def seg_spec(shape):
        nd = len(shape)
        return pl.BlockSpec((1,) + tuple(shape[1:]), lambda i: (seg_of(i),) + (0,) * (nd - 1))

    def par_spec(shape):
        nd = len(shape)
        return pl.BlockSpec(tuple(shape), lambda i: (0,) * nd)

    in_specs = ([row_spec(r.shape[1]) for r in rows] + [seg_spec(s.shape) for s in segs]
                + [par_spec(p.shape) for p in params])

    def load(refs):
        vals = [r[...].astype(F32) for r in refs[:n_row]]
        vals += [r[0].astype(F32) for r in refs[n_row:n_row + n_seg]]
        vals += [r[...].astype(F32) for r in refs[n_row + n_seg:n_row + n_seg + n_par]]
        return vals

    def fwd_call(arrs):
        def body(*refs):
            outs = fn(*load(refs))
            for o_ref, val in zip(refs[n_row + n_seg + n_par:], outs):
                o_ref[...] = val

        return pl.pallas_call(
            body, name=name + "_f", grid=(n_tiles,), in_specs=in_specs,
            out_specs=[row_spec(w) for w in out_widths],
            out_shape=[jax.ShapeDtypeStruct((r_total, w), F32) for w in out_widths],
            compiler_params=_cparams(1),
        )(*arrs)

    d_idx = [k for k in range(n_row) if diff_rows[k]]

    def bwd_call(arrs, douts):
        n_in = n_row + n_seg + n_par

        def body(*refs):
            i = pl.program_id(0)
            vals = load(refs[:n_in])
            gs = [r[...] for r in refs[n_in:n_in + n_out]]
            out_refs = refs[n_in + n_out:]
            diff_pos = d_idx + list(range(n_row, n_in))

            def f(*dv):
                full = list(vals)
                for pos, v in zip(diff_pos, dv):
                    full[pos] = v
                return tuple(fn(*full))

            _, vjp = jax.vjp(f, *[vals[p] for p in diff_pos])
            grads = vjp(tuple(gs))
            nd = len(d_idx)
            for o_ref, g in zip(out_refs[:nd], grads[:nd]):
                o_ref[...] = g
            first_seg = jnp.logical_or(i == 0, i == nct)
            for o_ref, g in zip(out_refs[nd:nd + n_seg], grads[nd:nd + n_seg]):
                @pl.when(first_seg)
                def _(o_ref=o_ref, g=g):
                    o_ref[0] = g

                @pl.when(jnp.logical_not(first_seg))
                def _(o_ref=o_ref, g=g):
                    o_ref[0] += g
            for o_ref, g in zip(out_refs[nd + n_seg:], grads[nd + n_seg:]):
                @pl.when(i == 0)
                def _(o_ref=o_ref, g=g):
                    o_ref[...] = g

                @pl.when(i != 0)
                def _(o_ref=o_ref, g=g):
                    o_ref[...] += g

        out_specs = ([row_spec(rows[k].shape[1]) for k in d_idx] + [seg_spec(s.shape) for s in segs]
                     + [par_spec(p.shape) for p in params])
        out_shape = ([jax.ShapeDtypeStruct(rows[k].shape, F32) for k in d_idx]
                     + [jax.ShapeDtypeStruct(s.shape, F32) for s in segs]
                     + [jax.ShapeDtypeStruct(p.shape, F32) for p in params])
        return pl.pallas_call(
            body, name=name + "_b", grid=(n_tiles,),
            in_specs=in_specs + [row_spec(w) for w in out_widths],
            out_specs=out_specs, out_shape=out_shape, compiler_params=_cparams(1),
        )(*arrs, *douts)

    @jax.custom_vjp
    def op(*arrs):
        return tuple(fwd_call(arrs))

    def op_fwd(*arrs):
        return tuple(fwd_call(arrs)), arrs

    def op_bwd(arrs, douts):
        grads = list(bwd_call(arrs, douts))
        nd = len(d_idx)
        row_grads = [jnp.zeros_like(arrs[k]) for k in range(n_row)]
        for k, g in zip(d_idx, grads[:nd]):
            row_grads[k] = g
        return tuple(row_grads + grads[nd:])

    op.defvjp(op_fwd, op_bwd)
    return op(*rows, *segs, *params)


ATT_SCALE = MLA_QK ** -0.5
LOG2E = math.log2(math.e)
ATT_KEY_CHUNKS = (1408, 768, 512, 256)


ATT_LATENT_TILE = 512


def _query_rows_spec(row0, tq):
    return pl.BlockSpec((pl.Element(tq), pl.Element(HEAD_PAD)),
                        lambda h, i: (pl.multiple_of(row0 + i * tq, ROW_TILE), pl.multiple_of(h * HEAD_PAD, HEAD_PAD)))


def _key_chunks(nk):
    kc = _pick(nk, ATT_KEY_CHUNKS)
    return [(c * kc, kc) for c in range(nk // kc)]


def _attn_fwd_call(q, k, v, row0, n_rows, tq, nk, name):
    def body(q_ref, k_ref, v_ref, o_ref, lse_ref):
        qv = q_ref[...]
        m = jnp.full((tq, 1), -jnp.inf, F32)
        l = jnp.zeros((tq, 1), F32)
        acc = jnp.zeros((tq, HEAD_PAD), F32)
        for start, size in _key_chunks(nk):
            s = lax.dot_general(qv, k_ref[start:start + size, :], _DOT_DIMS["nt"], preferred_element_type=F32)
            m_new = jnp.maximum(m, jnp.max(s, axis=-1, keepdims=True))
            alpha = jnp.exp2(m - m_new)
            p = jnp.exp2(s - m_new)
            l = alpha * l + jnp.sum(p, axis=-1, keepdims=True)
            acc = alpha * acc + lax.dot_general(p.astype(BF16), v_ref[start:start + size, :], _DOT_DIMS["nn"],
                                                preferred_element_type=F32)
            m = m_new
        o_ref[...] = acc / l
        lse_ref[...] = jnp.broadcast_to(m + jnp.log2(l), (tq, HEAD_PAD))

    out_spec = pl.BlockSpec((tq, HEAD_PAD), lambda h, i: (i, h))
    kv_spec = pl.BlockSpec((nk, HEAD_PAD), lambda h, i: (0, h))
    out = jax.ShapeDtypeStruct((n_rows, q.shape[1]), F32)
    return pl.pallas_call(
        body, name=name, grid=(MLA_HEADS, n_rows // tq), in_specs=[_query_rows_spec(row0, tq), kv_spec, kv_spec],
        out_specs=[out_spec, out_spec], out_shape=[out, out], compiler_params=_cparams(2),
    )(q, k, v)


def _attn_bwd_call(q, k, v, o, lse, do, row0, n_rows, tq, nk, name):
    nq = n_rows // tq

    def body(q_ref, k_ref, v_ref, o_ref, lse_ref, do_ref, dq_ref, dk_ref, dv_ref):
        i = pl.program_id(1)

        @pl.when(i == 0)
        def _():
            dk_ref[...] = jnp.zeros_like(dk_ref)
            dv_ref[...] = jnp.zeros_like(dv_ref)

        qv = q_ref[...]
        dov = do_ref[...]
        dob = dov.astype(BF16)
        lse = lse_ref[:, 0:1]
        delta = jnp.sum(dov * o_ref[...], axis=-1, keepdims=True)
        dq = jnp.zeros((tq, HEAD_PAD), F32)
        for start, size in _key_chunks(nk):
            kk = k_ref[start:start + size, :]
            vv = v_ref[start:start + size, :]
            s = lax.dot_general(qv, kk, _DOT_DIMS["nt"], preferred_element_type=F32)
            p = jnp.exp2(s - lse)
            dp = lax.dot_general(dob, vv, _DOT_DIMS["nt"], preferred_element_type=F32)
            g = (p * (dp - delta)).astype(BF16)
            dk_ref[start:start + size, :] += lax.dot_general(g, qv, _DOT_DIMS["tn"], preferred_element_type=F32)
            dv_ref[start:start + size, :] += lax.dot_general(p.astype(BF16), dob, _DOT_DIMS["tn"],
                                                             preferred_element_type=F32)
            dq = dq + lax.dot_general(g, kk, _DOT_DIMS["nn"], preferred_element_type=F32)
        dq_ref[...] = dq * ATT_SCALE

        @pl.when(i == nq - 1)
        def _():
            dk_ref[...] = dk_ref[...] * (1.0 / LOG2E)

    own_spec = pl.BlockSpec((tq, HEAD_PAD), lambda h, i: (i, h))
    kv_spec = pl.BlockSpec((nk, HEAD_PAD), lambda h, i: (0, h))
    rows_spec = _query_rows_spec(row0, tq)
    return pl.pallas_call(
        body, name=name, grid=(MLA_HEADS, nq),
        in_specs=[rows_spec, kv_spec, kv_spec, own_spec, own_spec, rows_spec],
        out_specs=[own_spec, kv_spec, kv_spec],
        out_shape=[jax.ShapeDtypeStruct((n_rows, q.shape[1]), F32), jax.ShapeDtypeStruct((nk, q.shape[1]), F32),
                   jax.ShapeDtypeStruct((nk, q.shape[1]), F32)],
        compiler_params=_cparams(2),
    )(q, k, v, o, lse, do)


def attention(q, k, v, tc, name):
    r_total = q.shape[0]
    tq_lat = _pick(r_total - tc, (ATT_LATENT_TILE, ROW_TILE))
    ranges = [(0, tc, ROW_TILE, tc, "_ctx"), (tc, r_total - tc, tq_lat, r_total, "_lat")]

    def operands(q, k, v):
        return (q * (ATT_SCALE * LOG2E)).astype(BF16), k.astype(BF16), v.astype(BF16)

    def forward(qb, kb, vb):
        return [_attn_fwd_call(qb, kb, vb, row0, n_rows, tq, nk, name + tag + "_f")
                for row0, n_rows, tq, nk, tag in ranges]

    @jax.custom_vjp
    def op(q, k, v):
        return jnp.concatenate([o for o, _ in forward(*operands(q, k, v))], axis=0)

    def fwd(q, k, v):
        qb, kb, vb = operands(q, k, v)
        parts = forward(qb, kb, vb)
        return jnp.concatenate([o for o, _ in parts], axis=0), (qb, kb, vb, parts)

    def bwd(res, do):
        qb, kb, vb, parts = res
        (dq_c, dk_c, dv_c), (dq_l, dk_l, dv_l) = [
            _attn_bwd_call(qb, kb, vb, o, lse, do, row0, n_rows, tq, nk, name + tag + "_b")
            for (o, lse), (row0, n_rows, tq, nk, tag) in zip(parts, ranges)]
        grow = lambda part: jnp.pad(part, ((0, r_total - tc), (0, 0)))
        return jnp.concatenate([dq_c, dq_l], axis=0), dk_l + grow(dk_c), dv_l + grow(dv_c)

    op.defvjp(fwd, bwd)
    return op(q, k, v)


def _chunk_masks(forward):
    rows = lax.broadcasted_iota(jnp.int32, (CHUNK, CHUNK), 0)
    cols = lax.broadcasted_iota(jnp.int32, (CHUNK, CHUNK), 1)
    return (rows >= cols) if forward else (rows < cols)


def _gla_chunk(forward, q, k, v, la, st0):
    cum = _tri_cumsum(la, forward)
    tot = jnp.sum(la, axis=0, keepdims=True)
    k_end = k * jnp.exp(tot - cum)
    st1 = st0 * jnp.exp(tot) + _bdot(v, k_end, "tn")
    q_dec = q * (jnp.exp(cum) * (GLA_DK ** -0.5))
    att = _bdot(q_dec, k * jnp.exp(-cum), "nt")
    att = jnp.where(_chunk_masks(forward), att, 0.0)
    o = _bdot(att, v, "nn") + _bdot(q_dec, st0, "nt")
    return o, st1


def _ret_chunk(forward, q, k, v, rd, st0):
    lg = -jnp.exp(rd[0:1, 0:1])
    rows = lax.broadcasted_iota(jnp.int32, (CHUNK, CHUNK), 0).astype(F32)
    cols = lax.broadcasted_iota(jnp.int32, (CHUNK, CHUNK), 1).astype(F32)
    pos = lax.broadcasted_iota(jnp.int32, (CHUNK, 1), 0).astype(F32)
    if forward:
        to_end, from_start, rel = CHUNK - 1.0 - pos, pos + 1.0, rows - cols
    else:
        to_end, from_start, rel = pos, CHUNK - pos, cols - rows
    mask = _chunk_masks(forward)
    dmat = jnp.where(mask, jnp.exp(jnp.where(mask, rel, 0.0) * lg), 0.0)
    st1 = st0 * jnp.exp(CHUNK * lg) + _bdot(v, k * jnp.exp(to_end * lg), "tn")
    att = _bdot(q, k, "nt") * dmat
    o = _bdot(att, v, "nn") + _bdot(q, st0, "nt") * jnp.exp(from_start * lg)
    return o, st1


def _scan_block_fn(chunk_fn, forward):
    order = range(SCAN_CHUNKS) if forward else range(SCAN_CHUNKS - 1, -1, -1)

    def block(q, k, v, aux, st0):
        outs = [None] * SCAN_CHUNKS
        st = st0
        for c in order:
            sl = slice(c * CHUNK, (c + 1) * CHUNK)
            a = aux[sl] if aux.shape[0] == ROW_TILE else aux
            outs[c], st = chunk_fn(forward, q[sl], k[sl], v[sl], a, st)
        return jnp.concatenate(outs, axis=0), st

    return block


def scan(kind, forward, q, k, v, aux, tc, name):
    heads = q.shape[1] // HEAD_PAD
    r_total = q.shape[0]
    nblk = r_total // ROW_TILE
    nctb = tc // ROW_TILE
    block_fn = _scan_block_fn(_gla_chunk if kind == "gla" else _ret_chunk, forward)
    per_row_aux = kind == "gla"

    def blk(g):
        if forward:
            return g
        return jnp.where(g < nctb, nctb - 1 - g, nblk - 1 - (g - nctb))

    def specs(step_to_g):
        row = pl.BlockSpec((ROW_TILE, heads * HEAD_PAD), lambda s: (blk(step_to_g(s)), 0))
        aux_spec = row if per_row_aux else pl.BlockSpec((heads, 8, HEAD_PAD), lambda s: (0, 0, 0))
        st = pl.BlockSpec((1, heads, HEAD_PAD, HEAD_PAD), lambda s: (step_to_g(s), 0, 0, 0))
        return row, aux_spec, st

    def head_cols(h):
        return slice(h * HEAD_PAD, (h + 1) * HEAD_PAD)

    def fwd_call(q, k, v, aux):
        row, aux_spec, st_spec = specs(lambda s: s)

        def body(q_ref, k_ref, v_ref, a_ref, o_ref, st0_ref, st_ref):
            @pl.when(pl.program_id(0) == 0)
            def _():
                st_ref[...] = jnp.zeros_like(st_ref)

            qv, kv, vv = q_ref[...], k_ref[...], v_ref[...]
            outs = []
            for h in range(heads):
                st0 = st_ref[h]
                st0_ref[0, h] = st0
                a = a_ref[:, head_cols(h)] if per_row_aux else a_ref[h]
                o, st1 = block_fn(qv[:, head_cols(h)], kv[:, head_cols(h)], vv[:, head_cols(h)], a, st0)
                outs.append(o)
                st_ref[h] = st1
            o_ref[...] = jnp.concatenate(outs, axis=1)

        return pl.pallas_call(
            body, name=name + "_f", grid=(nblk,), in_specs=[row, row, row, aux_spec],
            out_specs=[row, st_spec],
            out_shape=[jax.ShapeDtypeStruct(q.shape, F32),
                       jax.ShapeDtypeStruct((nblk, heads, HEAD_PAD, HEAD_PAD), F32)],
            scratch_shapes=[pltpu.VMEM((heads, HEAD_PAD, HEAD_PAD), F32)],
            compiler_params=_cparams(1),
        )(q, k, v, aux)

    def bwd_call(q, k, v, aux, st0s, do):
        row, aux_spec, st_spec = specs(lambda s: nblk - 1 - s)

        def body(q_ref, k_ref, v_ref, a_ref, st0_ref, do_ref, dq_ref, dk_ref, dv_ref, da_ref, dst_ref):
            s = pl.program_id(0)

            @pl.when(s == 0)
            def _():
                dst_ref[...] = jnp.zeros_like(dst_ref)

            qv, kv, vv, dov = q_ref[...], k_ref[...], v_ref[...], do_ref[...]
            grads = []
            for h in range(heads):
                a = a_ref[:, head_cols(h)] if per_row_aux else a_ref[h]
                _, vjp = jax.vjp(block_fn, qv[:, head_cols(h)], kv[:, head_cols(h)], vv[:, head_cols(h)], a,
                                 st0_ref[0, h])
                dq, dk, dv, da, dst0 = vjp((dov[:, head_cols(h)], dst_ref[h]))
                dst_ref[h] = dst0
                grads.append((dq, dk, dv, da))
            dq_ref[...] = jnp.concatenate([g[0] for g in grads], axis=1)
            dk_ref[...] = jnp.concatenate([g[1] for g in grads], axis=1)
            dv_ref[...] = jnp.concatenate([g[2] for g in grads], axis=1)
            if per_row_aux:
                da_ref[...] = jnp.concatenate([g[3] for g in grads], axis=1)
            else:
                da = jnp.stack([g[3] for g in grads], axis=0)

                @pl.when(s == 0)
                def _():
                    da_ref[...] = da

                @pl.when(s != 0)
                def _():
                    da_ref[...] += da

        return pl.pallas_call(
            body, name=name + "_b", grid=(nblk,),
            in_specs=[row, row, row, aux_spec, st_spec, row],
            out_specs=[row, row, row, aux_spec],
            out_shape=[jax.ShapeDtypeStruct(q.shape, F32)] * 3 + [jax.ShapeDtypeStruct(aux.shape, F32)],
            scratch_shapes=[pltpu.VMEM((heads, HEAD_PAD, HEAD_PAD), F32)],
            compiler_params=_cparams(1),
        )(q, k, v, aux, st0s, do)

    @jax.custom_vjp
    def op(q, k, v, aux):
        return fwd_call(q, k, v, aux)[0]

    def fwd(q, k, v, aux):
        o, st0s = fwd_call(q, k, v, aux)
        return o, (q, k, v, aux, st0s)

    def bwd(res, do):
        return tuple(bwd_call(*res, do))

    op.defvjp(fwd, bwd)
    return op(q, k, v, aux)


HALO = 8


def _neighbours(main, prev8, next8, i, nct, n_tiles):
    has_prev = jnp.logical_and(i != 0, i != nct).astype(F32)
    has_next = jnp.logical_and(i != nct - 1, i != n_tiles - 1).astype(F32)
    row = lax.broadcasted_iota(jnp.int32, main.shape, 0)
    down = jnp.where(row == 0, prev8[HALO - 1:HALO] * has_prev, pltpu.roll(main, 1, 0))
    up = jnp.where(row == ROW_TILE - 1, next8[0:1] * has_next, pltpu.roll(main, ROW_TILE - 1, 0))
    return down, up


def dwconv(x, w8, b, tc, name):
    r_total, width = x.shape
    n_tiles = r_total // ROW_TILE
    nct = tc // ROW_TILE
    per = ROW_TILE // HALO
    main_spec = pl.BlockSpec((ROW_TILE, width), lambda i: (i, 0))
    prev_spec = pl.BlockSpec((HALO, width), lambda i: (jnp.maximum(i * per - 1, 0), 0))
    next_spec = pl.BlockSpec((HALO, width), lambda i: (jnp.minimum((i + 1) * per, r_total // HALO - 1), 0))
    w_spec = pl.BlockSpec((8, width), lambda i: (0, 0))
    b_spec = pl.BlockSpec((1, width), lambda i: (0, 0))

    def fwd_call(x, w8, b):
        def body(x_ref, p_ref, n_ref, w_ref, b_ref, o_ref):
            xv = x_ref[...]
            down, up = _neighbours(xv, p_ref[...], n_ref[...], pl.program_id(0), nct, n_tiles)
            o_ref[...] = w_ref[0:1] * down + w_ref[1:2] * xv + w_ref[2:3] * up + b_ref[...]

        return pl.pallas_call(
            body, name=name + "_f", grid=(n_tiles,), in_specs=[main_spec, prev_spec, next_spec, w_spec, b_spec],
            out_specs=main_spec, out_shape=jax.ShapeDtypeStruct(x.shape, F32), compiler_params=_cparams(1),
        )(x, x, x, w8, b)

    def bwd_call(x, w8, g):
        def body(x_ref, xp_ref, xn_ref, g_ref, gp_ref, gn_ref, w_ref, dx_ref, dw_ref, db_ref):
            i = pl.program_id(0)
            xv, gv = x_ref[...], g_ref[...]
            x_down, x_up = _neighbours(xv, xp_ref[...], xn_ref[...], i, nct, n_tiles)
            g_down, g_up = _neighbours(gv, gp_ref[...], gn_ref[...], i, nct, n_tiles)
            dx_ref[...] = w_ref[0:1] * g_up + w_ref[1:2] * gv + w_ref[2:3] * g_down
            dw = jnp.concatenate([jnp.sum(gv * x_down, axis=0, keepdims=True),
                                  jnp.sum(gv * xv, axis=0, keepdims=True),
                                  jnp.sum(gv * x_up, axis=0, keepdims=True),
                                  jnp.zeros((5, width), F32)], axis=0)
            db = jnp.sum(gv, axis=0, keepdims=True)

            @pl.when(i == 0)
            def _():
                dw_ref[...] = dw
                db_ref[...] = db

            @pl.when(i != 0)
            def _():
                dw_ref[...] += dw
                db_ref[...] += db

        return pl.pallas_call(
            body, name=name + "_b", grid=(n_tiles,),
            in_specs=[main_spec, prev_spec, next_spec, main_spec, prev_spec, next_spec, w_spec],
            out_specs=[main_spec, w_spec, b_spec],
            out_shape=[jax.ShapeDtypeStruct(x.shape, F32), jax.ShapeDtypeStruct((8, width), F32),
                       jax.ShapeDtypeStruct((1, width), F32)],
            compiler_params=_cparams(1),
        )(x, x, x, g, g, g, w8)

    @jax.custom_vjp
    def op(x, w8, b):
        return fwd_call(x, w8, b)

    def fwd(x, w8, b):
        return fwd_call(x, w8, b), (x, w8)

    def bwd(res, g):
        return tuple(bwd_call(*res, g))

    op.defvjp(fwd, bwd)
    return op(x, w8, b)


def loss_head(h, target, tc, name):
    r_total, width = h.shape
    n_tiles = r_total // ROW_TILE
    nct = tc // ROW_TILE

    def call(h, target):
        def body(h_ref, t_ref, dh_ref, loss_ref, acc_ref):
            i = pl.program_id(0)

            @pl.when(i == 0)
            def _():
                acc_ref[...] = jnp.zeros_like(acc_ref)

            @pl.when(i < nct)
            def _():
                dh_ref[...] = jnp.zeros_like(dh_ref)

            @pl.when(i >= nct)
            def _():
                err = h_ref[...] - t_ref[...]
                dh_ref[...] = err * (1.0 / width)
                acc_ref[...] += jnp.sum((err * err).reshape(ROW_TILE // 8, 8, width), axis=0)

            @pl.when(i == n_tiles - 1)
            def _():
                loss_ref[...] = jnp.sum(acc_ref[...]).reshape(1, 1) * (0.5 / width)

        row = pl.BlockSpec((ROW_TILE, width), lambda i: (i, 0))
        return pl.pallas_call(
            body, name=name, grid=(n_tiles,),
            in_specs=[row, pl.BlockSpec((ROW_TILE, width), lambda i: (jnp.maximum(i - nct, 0), 0))],
            out_specs=[row, pl.BlockSpec((1, 1), lambda i: (0, 0))],
            out_shape=[jax.ShapeDtypeStruct(h.shape, F32), jax.ShapeDtypeStruct((1, 1), F32)],
            scratch_shapes=[pltpu.VMEM((8, width), F32)], compiler_params=_cparams(1),
        )(h, target)

    @jax.custom_vjp
    def op(h, target):
        return call(h, target)[1][0, 0]

    def fwd(h, target):
        dh, loss = call(h, target)
        return loss[0, 0], (dh, target)

    def bwd(res, g):
        dh, target = res
        return dh * g, jnp.zeros_like(target)

    op.defvjp(fwd, bwd)
    return op(h, target)


PACK_W = 1024
PACK_TILE = 128


def slab_sum(slabs, name):
    n_slab, n, _ = slabs.shape

    def body(s_ref, o_ref):
        acc = s_ref[0]
        for j in range(1, n_slab):
            acc = acc + s_ref[j]
        o_ref[...] = acc

    return pl.pallas_call(
        body, name=name, grid=(n // PACK_TILE,),
        in_specs=[pl.BlockSpec((n_slab, PACK_TILE, PACK_W), lambda i: (0, i, 0))],
        out_specs=pl.BlockSpec((PACK_TILE, PACK_W), lambda i: (i, 0)),
        out_shape=jax.ShapeDtypeStruct((n, PACK_W), F32), compiler_params=_cparams(1),
    )(slabs)


def adamw(g_slabs, w, m, v, name):
    n_slab, n, _ = g_slabs.shape

    def body(g_ref, w_ref, m_ref, v_ref, go_ref, d_ref, mo_ref, vo_ref):
        g = g_ref[0].astype(F32)
        for j in range(1, n_slab):
            g = g + g_ref[j].astype(F32)
        m_new = ADAM_B1 * m_ref[...] + (1.0 - ADAM_B1) * g
        v_new = ADAM_B2 * v_ref[...] + (1.0 - ADAM_B2) * (g * g)
        m_hat = m_new / (1.0 - ADAM_B1 ** ADAM_STEP)
        v_hat = v_new / (1.0 - ADAM_B2 ** ADAM_STEP)
        go_ref[...] = g
        d_ref[...] = -ADAM_LR * (m_hat / (jnp.sqrt(v_hat) + ADAM_EPS) + ADAM_WD * w_ref[...])
        mo_ref[...] = m_new
        vo_ref[...] = v_new

    flat = pl.BlockSpec((PACK_TILE, PACK_W), lambda i: (i, 0))
    return pl.pallas_call(
        body, name=name, grid=(n // PACK_TILE,),
        in_specs=[pl.BlockSpec((n_slab, PACK_TILE, PACK_W), lambda i: (0, i, 0)), flat, flat, flat],
        out_specs=[flat] * 4, out_shape=[jax.ShapeDtypeStruct((n, PACK_W), F32)] * 4, compiler_params=_cparams(1),
    )(g_slabs, w, m, v)


def all_gather(x, name):
    m_per, n = x.shape

    def body(x_ref, out_ref, send_sems, recv_sems, local_sem):
        px, py, pc = lax.axis_index("x"), lax.axis_index("y"), lax.axis_index("c")
        me, sibling = (px, py, pc), (px, py, 1 - pc)
        chips = [(1 - px, py), (px, 1 - py), (1 - px, 1 - py)]

        def rows(bx, by, bc):
            return out_ref.at[pl.ds((4 * bx + 2 * by + bc) * m_per, m_per), :]

        def copy(k, block, to, src=None):
            return pltpu.make_async_remote_copy(
                src_ref=rows(*block) if src is None else src, dst_ref=rows(*block),
                send_sem=send_sems.at[k], recv_sem=recv_sems.at[k], device_id=to, device_id_type=MESH)

        mine = pltpu.make_async_copy(x_ref, rows(*me), local_sem)
        mine.start()
        first = [copy(0, me, sibling, src=x_ref)]
        first += [copy(1 + j, me, (*chip, pc), src=x_ref) for j, chip in enumerate(chips)]
        for cp in first:
            cp.start()
        passed = [copy(4 + j, (*chip, pc), sibling) for j, chip in enumerate(chips)]
        for j, chip in enumerate(chips):
            copy(1 + j, (*chip, pc), me).wait_recv()
            passed[j].start()
        copy(0, sibling, me).wait_recv()
        for j, chip in enumerate(chips):
            copy(4 + j, (*chip, 1 - pc), me).wait_recv()
        for cp in first + passed:
            cp.wait_send()
        mine.wait()

    return pl.pallas_call(
        body, name=name, out_shape=jax.ShapeDtypeStruct((N_DEV * m_per, n), x.dtype),
        in_specs=[pl.BlockSpec(memory_space=pl.ANY)], out_specs=pl.BlockSpec(memory_space=pl.ANY),
        scratch_shapes=[pltpu.SemaphoreType.DMA((7,)), pltpu.SemaphoreType.DMA((7,)), pltpu.SemaphoreType.DMA],
    )(x)


def all_to_all(x, name):
    _, n, cols = x.shape

    def body(x_ref, out_ref, send_sems, recv_sems, local_sem):
        px, py, pc = lax.axis_index("x"), lax.axis_index("y"), lax.axis_index("c")
        mine_idx = 4 * px + 2 * py + pc
        flips = [(fx, fy, fc) for fx in (0, 1) for fy in (0, 1) for fc in (0, 1)][1:]
        local = pltpu.make_async_copy(x_ref.at[mine_idx], out_ref.at[mine_idx], local_sem)
        local.start()
        copies = []
        for k, (fx, fy, fc) in enumerate(flips):
            qx, qy, qc = px ^ fx, py ^ fy, pc ^ fc
            peer_idx = 4 * qx + 2 * qy + qc
            copies.append((
                pltpu.make_async_remote_copy(
                    src_ref=x_ref.at[peer_idx], dst_ref=out_ref.at[mine_idx], send_sem=send_sems.at[k],
                    recv_sem=recv_sems.at[k], device_id=(qx, qy, qc), device_id_type=MESH),
                pltpu.make_async_remote_copy(
                    src_ref=x_ref.at[peer_idx], dst_ref=out_ref.at[peer_idx], send_sem=send_sems.at[k],
                    recv_sem=recv_sems.at[k], device_id=(qx, qy, qc), device_id_type=MESH)))
        for send, _ in copies:
            send.start()
        for _, landing in copies:
            landing.wait_recv()
        for send, _ in copies:
            send.wait_send()
        local.wait()

    return pl.pallas_call(
        body, name=name, out_shape=jax.ShapeDtypeStruct(x.shape, x.dtype),
        in_specs=[pl.BlockSpec(memory_space=pl.ANY)], out_specs=pl.BlockSpec(memory_space=pl.ANY),
        scratch_shapes=[pltpu.SemaphoreType.DMA((7,)), pltpu.SemaphoreType.DMA((7,)), pltpu.SemaphoreType.DMA],
    )(x)


IN_OFFSETS = {}
_off = 0
for _name, _width in (("mla_q", 256), ("mla_kv", 128), ("mla_kr", 32), ("gla_q", 512), ("gla_k", 512), ("gla_v", 512),
                      ("gla_g", 512), ("gla_rf", 16), ("gla_rb", 16), ("ret_q", 512), ("ret_k", 512), ("ret_v", 512),
                      ("ret_g", 512), ("gate_mla", 1024), ("gate_gla", 1024), ("gate_ret", 1024)):
    IN_OFFSETS[_name] = (_off, _off + _width)
    _off += _width
N_IN = _off

P_GLA, P_RET, P_GATE, P_MLAQ, P_MLAKV, P_MLAKR, P_RANK, P_END = 0, 2048, 4096, 7168, 7424, 7552, 7680, 7808


def _pad_in_proj(w):
    def cols(a, b):
        return w[:, IN_OFFSETS[a][0]:IN_OFFSETS[b][1]]

    def z(n):
        return jnp.zeros((w.shape[0], n), w.dtype)

    return jnp.concatenate([cols("gla_q", "gla_g"), cols("ret_q", "ret_g"), cols("gate_mla", "gate_ret"),
                            cols("mla_q", "mla_kv"), z(MLA_NOPE), cols("mla_kr", "mla_kr"),
                            z(HEAD_PAD - MLA_QK), cols("gla_rf", "gla_rb"), z(HEAD_PAD - 2 * GLA_RANK),
                            z(N_IN_PAD - P_END)], axis=1)


def _pad_last(a, n):
    return jnp.pad(a, [(0, 0)] * (a.ndim - 1) + [(0, n - a.shape[-1])])


def _position_tables(tc, t):
    pos = jnp.arange(t)
    inv = ROPE_THETA ** (-jnp.arange(MLA_ROPE // 4, dtype=F32) * 2.0 / (MLA_ROPE // 2))
    ang_r = (pos // GRID_W).astype(F32)[:, None] * inv[None, :]
    ang_c = (pos % GRID_W).astype(F32)[:, None] * inv[None, :]
    z8, z32, z64 = jnp.zeros((t, 8), F32), jnp.zeros((t, 32), F32), jnp.zeros((t, 64), F32)
    lat_c = jnp.concatenate([jnp.ones((t, 64), F32), jnp.cos(ang_r), jnp.cos(ang_r), jnp.cos(ang_c), jnp.cos(ang_c),
                             z32], axis=1)
    lat_sn = jnp.concatenate([z64, -jnp.sin(ang_r), z8, -jnp.sin(ang_c), z8, z32], axis=1)
    lat_sp = jnp.concatenate([z64, z8, jnp.sin(ang_r), z8, jnp.sin(ang_c), z32], axis=1)
    ctx_c = jnp.concatenate([jnp.ones((tc, MLA_QK), F32), jnp.zeros((tc, HEAD_PAD - MLA_QK), F32)], axis=1)
    ctx_z = jnp.zeros((tc, HEAD_PAD), F32)
    rinv = 1.0 / (RET_THETA ** jnp.linspace(0.0, 1.0, RET_DK // 2, dtype=F32))
    rang = jnp.arange(tc + t).astype(F32)[:, None] * rinv[None, :]
    return dict(c=jnp.concatenate([ctx_c, lat_c]), sn=jnp.concatenate([ctx_z, lat_sn]),
                sp=jnp.concatenate([ctx_z, lat_sp]),
                rc=jnp.concatenate([jnp.cos(rang), jnp.cos(rang)], axis=1),
                rs=jnp.concatenate([-jnp.sin(rang), jnp.sin(rang)], axis=1))


def _heads(x):
    return [x[:, h * HEAD_PAD:(h + 1) * HEAD_PAD] for h in range(x.shape[1] // HEAD_PAD)]


def _mla_rope(x, c, sn, sp):
    return x * c + _roll(x, HEAD_PAD - 8, 1) * sn + _roll(x, 8, 1) * sp


def _norm_mod_fn(shift_row, scale_row):
    def fn(h, mod, w):
        return (_rms(h, D, w) * (1.0 + mod[scale_row:scale_row + 1]) + mod[shift_row:shift_row + 1],)
    return fn


def _resid_fn(gate_row):
    def fn(h, y, mod):
        return (h + mod[gate_row:gate_row + 1] * y,)
    return fn


def _q_fn(cq, c, sn, sp, norm_a, w_qb, q_norm):
    qf = _bdot(_rms(cq, MLA_Q_LORA, norm_a), w_qb, "nn")
    return (jnp.concatenate([_mla_rope(_rms(qh, MLA_QK, q_norm), c, sn, sp) for qh in _heads(qf)], axis=1),)


def _kv_fn(ckv, kr, c, sn, sp, norm_a, w_k, w_v, k_norm):
    x = _rms(ckv, MLA_KV_LORA, norm_a)
    kf = _bdot(x, w_k, "nn")
    k = jnp.concatenate([_mla_rope(_rms(kh + kr, MLA_QK, k_norm), c, sn, sp) for kh in _heads(kf)], axis=1)
    return k, _bdot(x, w_v, "nn")


def _decay_fn(ranks, w2, b):
    la = _log_sigmoid(_bdot(ranks, w2, "nn") + b) * (1.0 / GLA_NORMALIZER)
    return la[:, :GLA_HEADS * GLA_DK], la[:, GLA_HEADS * GLA_DK:]


def _ret_rot_fn(q, k, rc, rs):
    def rot(x, scale):
        return jnp.concatenate([(xh * rc + _roll(xh, RET_DK // 2, 1) * rs) * scale for xh in _heads(x)], axis=1)
    return rot(q, 1.0), rot(k, RET_DK ** -0.5)


def _gla_out_fn(o_f, o_b, g, w):
    y = jnp.concatenate([_rms(oh, HEAD_PAD, w) for oh in _heads(o_f + o_b)], axis=1)
    return (y * _silu(g),)


def _ret_out_fn(o_f, o_b, g):
    y = jnp.concatenate([_rms(oh, HEAD_PAD) for oh in _heads(o_f + o_b)], axis=1)
    return (y * _silu(g),)


def _merge_fn(z0, z1, z2, g0, g1, g2, bg):
    return (jax.nn.sigmoid(g0 + bg[0:1]) * z0 + jax.nn.sigmoid(g1 + bg[1:2]) * z1 + jax.nn.sigmoid(g2 + bg[2:3]) * z2,)


def _ffn_act_fn(c, up):
    return (_gelu_tanh(c) * up,)


def _layer(l, h, mod, w, tabs, tc):
    nct = tc // ROW_TILE
    tag = f"_l{l}"
    row = lambda a: a[l][None]
    a = rowwise("norm1" + tag, _norm_mod_fn(0, 1), [h], [mod], [row(w["norm1_w"])], [D], nct)[0]
    p = linear(a, _pad_in_proj(w["w_in"][l]), "in_proj" + tag)
    piece = lambda start, width: p[:, start:start + width]

    w_qb = _pad_last(w["mla_w_qb"][l].reshape(MLA_Q_LORA, MLA_HEADS, MLA_QK), HEAD_PAD).reshape(MLA_Q_LORA, -1)
    w_kvb = w["mla_w_kvb"][l].reshape(MLA_KV_LORA, MLA_HEADS, MLA_NOPE + MLA_V)
    w_k = _pad_last(w_kvb[:, :, :MLA_NOPE], HEAD_PAD).reshape(MLA_KV_LORA, -1)
    w_v = _pad_last(w_kvb[:, :, MLA_NOPE:], HEAD_PAD).reshape(MLA_KV_LORA, -1)
    rope = [tabs["c"], tabs["sn"], tabs["sp"]]
    q = rowwise("mla_q" + tag, _q_fn, [piece(P_MLAQ, MLA_Q_LORA)] + rope, [],
                [row(w["mla_q_norm_a"]), w_qb, _pad_last(row(w["mla_q_norm"]), HEAD_PAD)],
                [MLA_HEADS * HEAD_PAD], nct, diff_rows=[True, False, False, False])[0]
    k, v = rowwise("mla_kv" + tag, _kv_fn, [piece(P_MLAKV, MLA_KV_LORA), piece(P_MLAKR, HEAD_PAD)] + rope, [],
                   [row(w["mla_kv_norm_a"]), w_k, w_v, _pad_last(row(w["mla_k_norm"]), HEAD_PAD)],
                   [MLA_HEADS * HEAD_PAD] * 2, nct, diff_rows=[True, True, False, False, False])
    y_mla = attention(q, k, v, tc, "attn" + tag)
    wb_mla = _pad_last(w["w_branch"][l, 0].reshape(MLA_HEADS, MLA_V, D).transpose(0, 2, 1), HEAD_PAD)
    wb_mla = wb_mla.transpose(0, 2, 1).reshape(MLA_HEADS * HEAD_PAD, D)

    w2 = jnp.zeros((HEAD_PAD, 2 * GLA_HEADS * GLA_DK), F32)
    w2 = w2.at[:GLA_RANK, :GLA_HEADS * GLA_DK].set(w["gla_w_gk2"][l, 0])
    w2 = w2.at[GLA_RANK:2 * GLA_RANK, GLA_HEADS * GLA_DK:].set(w["gla_w_gk2"][l, 1])
    la_f, la_b = rowwise("gla_decay" + tag, _decay_fn, [piece(P_RANK, HEAD_PAD)], [],
                         [w2, w["gla_b_gk"][l].reshape(1, -1)], [GLA_HEADS * GLA_DK] * 2, nct)
    gq, gk, gv, gg = [piece(P_GLA + n * 512, 512) for n in range(4)]
    o_f = scan("gla", True, gq, gk, gv, la_f, tc, "gla_fw" + tag)
    o_b = scan("gla", False, gq, gk, gv, la_b, tc, "gla_bw" + tag)
    y_gla = rowwise("gla_out" + tag, _gla_out_fn, [o_f, o_b, gg], [], [row(w["gla_o_norm"])], [512], nct)[0]

    rq, rk = rowwise("ret_rot" + tag, _ret_rot_fn, [piece(P_RET, 512), piece(P_RET + 512, 512), tabs["rc"], tabs["rs"]],
                     [], [], [512, 512], nct, diff_rows=[True, True, False, False])
    rv, rg = piece(P_RET + 1024, 512), piece(P_RET + 1536, 512)
    rd = jnp.broadcast_to(w["ret_decay"][l][:, :, None, None], (2, RET_HEADS, 8, HEAD_PAD))
    r_f = scan("ret", True, rq, rk, rv, rd[0], tc, "ret_fw" + tag)
    r_b = scan("ret", False, rq, rk, rv, rd[1], tc, "ret_bw" + tag)
    y_ret = rowwise("ret_out" + tag, _ret_out_fn, [r_f, r_b, rg], [], [], [512], nct)[0]

    z = [linear(y_mla, wb_mla, "branch_mla" + tag), linear(y_gla, w["w_branch"][l, 1], "branch_gla" + tag),
         linear(y_ret, w["w_branch"][l, 2], "branch_ret" + tag)]
    gates = [piece(P_GATE + n * D, D) for n in range(3)]
    merged = rowwise("merge" + tag, _merge_fn, z + gates, [], [_pad_rows(w["b_gate"][l], 8)], [D], nct)[0]
    y = linear(merged, w["w_out"][l], "w_out" + tag)
    h = rowwise("resid1" + tag, _resid_fn(2), [h, y], [mod], [], [D], nct)[0]

    a2 = rowwise("norm2" + tag, _norm_mod_fn(3, 4), [h], [mod], [row(w["norm2_w"])], [D], nct)[0]
    gate = linear(a2, w["w_ffn_in"][l][:, :D_FF], "ffn_gate" + tag)
    up = linear(a2, w["w_ffn_in"][l][:, D_FF:], "ffn_up" + tag)
    conv = dwconv(gate, _pad_rows(w["w_dw"][l], 8), row(w["b_dw"]), tc, "dwconv" + tag)
    u = rowwise("ffn_act" + tag, _ffn_act_fn, [conv, up], [], [], [D_FF], nct)[0]
    f = linear(u, w["w_ffn_out"][l], "ffn_out" + tag)
    return rowwise("resid2" + tag, _resid_fn(5), [h, f], [mod], [], [D], nct)[0]


def _pad_rows(a, n):
    return jnp.pad(a, [(0, n - a.shape[0])] + [(0, 0)] * (a.ndim - 1))


def local_loss(w, mod, x, ctx, target):
    tc, t = ctx.shape[0], x.shape[0]
    tabs = _position_tables(tc, t)
    h = jnp.concatenate([ctx, x], axis=0)
    for l in range(DEPTH):
        h = _layer(l, h, mod[l], w, tabs, tc)
    return loss_head(h, target, tc, "loss_head")


ADA_ROWS = 16


def ada_forward(cond_in, w_ada, b_loc):
    cols = w_ada.shape[2]

    def body(x_ref, w_ref, b_ref, o_ref):
        s = _silu(x_ref[...])
        for l in range(DEPTH):
            o_ref[l] = _dg(s, w_ref[l], "nn") + b_ref[l]

    return pl.pallas_call(
        body, name="ada_forward", out_shape=jax.ShapeDtypeStruct((DEPTH, ADA_ROWS, cols), F32),
        compiler_params=pltpu.CompilerParams(vmem_limit_bytes=VMEM_LIMIT_BYTES),
    )(cond_in, w_ada, b_loc)


def ada_backward(cond_in, g_loc, dmod_own, w_ada):
    cols = w_ada.shape[2]

    def body(x_ref, g_ref, own_ref, w_ref, gw_ref, dc_ref, gb_ref):
        x = x_ref[...]
        s = _silu(x)
        dcond = jnp.zeros((8, D), F32)
        for l in range(DEPTH):
            g_ctx = jnp.sum(g_ref[2 * l], axis=0, keepdims=True)
            g_rows = jnp.concatenate([g_ref[2 * l + 1], jnp.broadcast_to(g_ctx, (8, cols))], axis=0)
            keep = lax.broadcasted_iota(jnp.int32, (ADA_ROWS, cols), 0) <= N_DEV
            gw_ref[l] = _dg(s, jnp.where(keep, g_rows, 0.0), "tn")
            dcond = dcond + _dg(jnp.broadcast_to(g_ctx, (8, cols)), w_ref[l], "nt")
            gb_ref[l:l + 1, :] = own_ref[2 * l:2 * l + 1, :] + own_ref[2 * l + 1:2 * l + 2, :]
        xc = x[N_DEV:N_DEV + 1]
        sig = jax.nn.sigmoid(xc)
        dc_ref[...] = dcond[0:1] * (sig * (1.0 + xc * (1.0 - sig)))

    return pl.pallas_call(
        body, name="ada_backward",
        out_shape=[jax.ShapeDtypeStruct(w_ada.shape, F32), jax.ShapeDtypeStruct((1, D), F32),
                   jax.ShapeDtypeStruct((DEPTH, 6 * D), F32)],
        compiler_params=pltpu.CompilerParams(vmem_limit_bytes=VMEM_LIMIT_BYTES),
    )(cond_in, g_loc, dmod_own, w_ada)


WEIGHTS = ["c_ctx", "w_ada", "b_ada", "norm1_w", "norm2_w", "w_in", "b_gate", "mla_q_norm_a", "mla_w_qb",
           "mla_kv_norm_a", "mla_w_kvb", "mla_q_norm", "mla_k_norm", "gla_w_gk2", "gla_b_gk", "gla_o_norm",
           "ret_decay", "w_branch", "w_out", "w_ffn_in", "w_dw", "b_dw", "w_ffn_out"]
INPUTS = ["x", "c", "ctx"] + WEIGHTS + ["loss_target"] + ["m_" + n for n in WEIGHTS] + ["v_" + n for n in WEIGHTS]
BIG = {"w_in": 2, "mla_w_qb": 2, "mla_w_kvb": 2, "w_branch": 3, "w_out": 1, "w_ffn_in": 2, "w_ffn_out": 1}
SMALL_SHARDED = {"b_gate": 2, "gla_w_gk2": 3, "gla_b_gk": 2, "w_dw": 2}
SMALL = ["c_ctx", "b_ada", "norm1_w", "norm2_w", "b_gate", "mla_q_norm_a", "mla_kv_norm_a", "mla_q_norm", "mla_k_norm",
         "gla_w_gk2", "gla_b_gk", "gla_o_norm", "ret_decay", "w_dw", "b_dw"]


def _entry_rows(size, align):
    return -(-size // (PACK_W * align)) * align


def _pack(arrays, rows, dtype, align, lead=0):
    parts = []
    for a in arrays:
        head = a.shape[:lead]
        size = math.prod(a.shape[lead:])
        r = _entry_rows(size, align)
        if r * PACK_W == size:
            parts.append(a.astype(dtype).reshape(head + (r, PACK_W)))
        else:
            flat = jnp.pad(a.astype(dtype).reshape(head + (size,)), [(0, 0)] * lead + [(0, r * PACK_W - size)])
            parts.append(flat.reshape(head + (r, PACK_W)))
    used = sum(p.shape[lead] for p in parts)
    if rows > used:
        parts.append(jnp.zeros(parts[0].shape[:lead] + (rows - used, PACK_W), dtype))
    return jnp.concatenate(parts, axis=lead)


def _pack_rows(shapes, align, multiple):
    used = sum(_entry_rows(math.prod(s), align) for s in shapes)
    return -(-used // multiple) * multiple


def _unpack(pack, shapes, align):
    head = pack.shape[:-2]
    out, off = [], 0
    for shape in shapes:
        size = math.prod(shape)
        r = _entry_rows(size, align)
        block = lax.slice_in_dim(pack, off, off + r, axis=len(head))
        if r * PACK_W != size:
            block = block.reshape(head + (r * PACK_W,))[..., :size]
        out.append(block.reshape(head + tuple(shape)))
        off += r
    return out


def _join_shards(stacked, axis):
    moved = jnp.moveaxis(stacked, 0, axis)
    shape = list(moved.shape)
    return moved.reshape(shape[:axis] + [shape[axis] * shape[axis + 1]] + shape[axis + 2:])


def _split_shards(full, axis):
    shape = list(full.shape)
    split = full.reshape(shape[:axis] + [N_DEV, shape[axis] // N_DEV] + shape[axis + 1:])
    return jnp.moveaxis(split, axis, 0)


def _gather_shards(local, axes, dtype, rows_multiple, name):
    names = list(axes)
    shapes = [local[n].shape for n in names]
    rows = _pack_rows(shapes, rows_multiple, rows_multiple)
    gathered = all_gather(_pack([local[n] for n in names], rows, dtype, rows_multiple), name)
    stacked = _unpack(gathered.reshape(N_DEV, rows, PACK_W), shapes, rows_multiple)
    return {n: _join_shards(s, axes[n]).astype(F32) for n, s in zip(names, stacked)}


def kernel(*args):
    a = dict(zip(INPUTS, args))
    me = 4 * lax.axis_index("x") + 2 * lax.axis_index("y") + lax.axis_index("c")
    cols = a["w_ada"].shape[2]

    c_all = all_gather(jnp.pad(a["c"], ((0, 7), (0, 0))), "gather_c").reshape(N_DEV, 8, D)[:, 0]
    cond_in = jnp.concatenate([c_all, a["c_ctx"][None], jnp.zeros((ADA_ROWS - N_DEV - 1, D), F32)], axis=0)
    b_loc = lax.dynamic_slice_in_dim(a["b_ada"], me * cols, cols, axis=1)[:, None, :]
    mod_loc = ada_forward(cond_in, a["w_ada"], b_loc)
    mod_all = all_gather(mod_loc.reshape(DEPTH * ADA_ROWS, cols), "gather_mod")
    mod_all = mod_all.reshape(N_DEV, DEPTH, ADA_ROWS, cols).transpose(1, 2, 0, 3).reshape(DEPTH, ADA_ROWS, 6, D)
    mod_me = lax.dynamic_index_in_dim(mod_all, me, axis=1, keepdims=False)
    mod = jnp.pad(jnp.stack([mod_all[:, N_DEV], mod_me], axis=1), ((0, 0), (0, 0), (0, 2), (0, 0)))

    w = _gather_shards(a, BIG, BF16, 16, "gather_weights")
    w.update(_gather_shards(a, SMALL_SHARDED, F32, 8, "gather_small"))
    for n in SMALL:
        if n not in SMALL_SHARDED and n not in ("c_ctx", "b_ada"):
            w[n] = a[n]

    loss, (gw, gmod, gx) = jax.value_and_grad(local_loss, argnums=(0, 1, 2))(
        w, mod, a["x"][0], a["ctx"][0], a["loss_target"][0])
    loss = lax.psum(loss, ("x", "y", "c"))

    dmod_own = gmod[:, :, :6].reshape(2 * DEPTH, 6 * D)
    g_all = all_gather(jnp.pad(dmod_own, ((0, 8 - 2 * DEPTH), (0, 0))), "gather_dmod").reshape(N_DEV, 8, 6 * D)
    g_loc = lax.dynamic_slice_in_dim(g_all[:, :2 * DEPTH], me * cols, cols, axis=2).transpose(1, 0, 2)
    g_w_ada, g_c_ctx, g_b_ada = ada_backward(cond_in, g_loc, dmod_own, a["w_ada"])

    small_part = dict(gw, c_ctx=g_c_ctx, b_ada=g_b_ada)
    small_shapes = [a[n].shape if n not in SMALL_SHARDED else gw[n].shape for n in SMALL]
    rows = _pack_rows(small_shapes, 8, PACK_TILE)
    parts = all_gather(_pack([small_part[n] for n in SMALL], rows, F32, 8), "gather_small_grads")
    small_sum = _unpack(slab_sum(parts.reshape(N_DEV, rows, PACK_W), "sum_small_grads"), small_shapes, 8)
    g_small = {}
    for n, g in zip(SMALL, small_sum):
        if n in SMALL_SHARDED:
            ax = SMALL_SHARDED[n]
            g = lax.dynamic_slice_in_dim(g, me * a[n].shape[ax], a[n].shape[ax], axis=ax)
        g_small[n] = g

    big_rows = _pack_rows([a[n].shape for n in BIG], 16, PACK_TILE)
    slabs = _pack([_split_shards(gw[n], ax) for n, ax in BIG.items()], big_rows, BF16, 16, lead=1)
    landed = all_to_all(slabs, "scatter_grads")

    def update(names, g_slabs, rows, align, label):
        shapes = [a[n].shape for n in names]
        packs = [_pack([a[pre + n] for n in names], rows, F32, align) for pre in ("", "m_", "v_")]
        outs = adamw(g_slabs, *packs, label)
        return [dict(zip(names, _unpack(o, shapes, align))) for o in outs]

    res_big = update(list(BIG), landed, big_rows, 16, "adamw_big")
    ada_rows = _pack_rows([a["w_ada"].shape], 8, PACK_TILE)
    res_ada = update(["w_ada"], _pack([g_w_ada], ada_rows, F32, 8)[None], ada_rows, 8, "adamw_ada")
    small_rows = _pack_rows([a[n].shape for n in SMALL], 8, PACK_TILE)
    res_small = update(SMALL, _pack([g_small[n] for n in SMALL], small_rows, F32, 8)[None], small_rows, 8,
                       "adamw_small")

    outs = [loss, gx[None]]
    for k in range(4):
        merged = {**res_big[k], **res_ada[k], **res_small[k]}
        outs += [merged[n] for n in WEIGHTS]
    return tuple(outs)
```

```python
import functools
import math

import jax
import jax.numpy as jnp
import numpy as np
from jax import lax
from jax.experimental import pallas as pl
from jax.experimental.pallas import tpu as pltpu

F32 = jnp.float32
BF16 = jnp.bfloat16

N_DEV = 8
D = 1024
DEPTH = 2
GRID_W = 64
MLA_HEADS = 8
MLA_NOPE = 64
MLA_ROPE = 32
MLA_QK = 96
MLA_V = 64
MLA_Q_LORA = 256
MLA_KV_LORA = 128
GLA_HEADS = 4
GLA_DK = 128
GLA_RANK = 16
GLA_NORMALIZER = 16.0
RET_HEADS = 4
RET_DK = 128
BRANCH_W = 512
D_FF = 2816
CHUNK = 64
ROPE_THETA = 10000.0
RET_THETA = 10000.0
EPS = 1e-6
HEAD_PAD = 128
N_IN_PAD = 8192

ADAM_LR = 0.001
ADAM_B1 = 0.9
ADAM_B2 = 0.999
ADAM_EPS = 1e-08
ADAM_WD = 0.01
ADAM_STEP = 10

ROW_TILE = 256
SCAN_CHUNKS = ROW_TILE // CHUNK
VMEM_LIMIT_BYTES = 56 * 1024 * 1024
MESH = pl.DeviceIdType.MESH


def _cparams(n_axes):
    return pltpu.CompilerParams(dimension_semantics=("arbitrary",) * n_axes, vmem_limit_bytes=VMEM_LIMIT_BYTES)


def _pick(dim, cands):
    for cand in cands:
        if dim % cand == 0:
            return cand
    return dim


_DOT_DIMS = {"nn": (((1,), (0,)), ((), ())), "nt": (((1,), (1,)), ((), ())), "tn": (((0,), (0,)), ((), ()))}


def _dg(a, b, mode):
    return lax.dot_general(a.astype(BF16), b.astype(BF16), _DOT_DIMS[mode], preferred_element_type=F32)


def _bdot(a, b, mode):
    @jax.custom_vjp
    def f(a, b):
        return _dg(a, b, mode)

    def fwd(a, b):
        return _dg(a, b, mode), (a, b)

    def bwd(res, g):
        a, b = res
        if mode == "nn":
            return _dg(g, b, "nt").astype(a.dtype), _dg(a, g, "tn").astype(b.dtype)
        if mode == "nt":
            return _dg(g, b, "nn").astype(a.dtype), _dg(g, a, "tn").astype(b.dtype)
        return _dg(b, g, "nt").astype(a.dtype), _dg(a, g, "nn").astype(b.dtype)

    f.defvjp(fwd, bwd)
    return f(a, b)


def _roll(x, shift, axis):
    n = x.shape[axis]
    shift = shift % n

    @jax.custom_vjp
    def f(x):
        return pltpu.roll(x, shift, axis)

    def fwd(x):
        return pltpu.roll(x, shift, axis), None

    def bwd(_, g):
        return (pltpu.roll(g, (n - shift) % n, axis),)

    f.defvjp(fwd, bwd)
    return f(x)


@jax.custom_jvp
def _log_sigmoid(x):
    return jnp.minimum(x, 0.0) - jnp.log(1.0 + jnp.exp(-jnp.abs(x)))


@_log_sigmoid.defjvp
def _log_sigmoid_jvp(primals, tangents):
    (x,), (t,) = primals, tangents
    return _log_sigmoid(x), t * jax.nn.sigmoid(-x)


def _rms(x, n, w=None):
    y = x * lax.rsqrt(jnp.sum(x * x, axis=-1, keepdims=True) * (1.0 / n) + EPS)
    return y if w is None else y * w


def _silu(x):
    return x * jax.nn.sigmoid(x)


def _gelu_tanh(x):
    return 0.5 * x * (1.0 + jnp.tanh(math.sqrt(2.0 / math.pi) * (x + 0.044715 * (x * x * x))))


def _mm(a, b, mode, name):
    if mode == "nn":
        (m, k), (_, n) = a.shape, b.shape
    elif mode == "nt":
        (m, k), (n, _) = a.shape, b.shape
    else:
        (k, m), (_, n) = a.shape, b.shape
    tm = _pick(m, (1024, 768, 1408, 512, 256, 128))
    tn = _pick(n, (1024, 1408, 512, 256, 128))
    tk = _pick(k, (1024, 768, 1408, 512, 256, 128))
    nk = k // tk
    if mode == "nn":
        a_spec = pl.BlockSpec((tm, tk), lambda i, j, kk: (i, kk))
        b_spec = pl.BlockSpec((tk, tn), lambda i, j, kk: (kk, j))
    elif mode == "nt":
        a_spec = pl.BlockSpec((tm, tk), lambda i, j, kk: (i, kk))
        b_spec = pl.BlockSpec((tn, tk), lambda i, j, kk: (j, kk))
    else:
        a_spec = pl.BlockSpec((tk, tm), lambda i, j, kk: (kk, i))
        b_spec = pl.BlockSpec((tk, tn), lambda i, j, kk: (kk, j))

    def body(a_ref, b_ref, o_ref, acc_ref):
        kk = pl.program_id(2)

        @pl.when(kk == 0)
        def _():
            acc_ref[...] = jnp.zeros_like(acc_ref)

        acc_ref[...] += _dg(a_ref[...], b_ref[...], mode)

        @pl.when(kk == nk - 1)
        def _():
            o_ref[...] = acc_ref[...]

    return pl.pallas_call(
        body, name=name, grid=(m // tm, n // tn, nk),
        in_specs=[a_spec, b_spec], out_specs=pl.BlockSpec((tm, tn), lambda i, j, kk: (i, j)),
        out_shape=jax.ShapeDtypeStruct((m, n), F32),
        scratch_shapes=[pltpu.VMEM((tm, tn), F32)],
        compiler_params=_cparams(3),
    )(a, b)


def linear(x, w, name):
    @jax.custom_vjp
    def op(x, w):
        return _mm(x, w.astype(BF16), "nn", name + "_f")

    def fwd(x, w):
        wb = w.astype(BF16)
        return _mm(x, wb, "nn", name + "_f"), (x, wb)

    def bwd(res, g):
        x, wb = res
        return _mm(g, wb, "nt", name + "_dx"), _mm(x, g, "tn", name + "_dw")

    op.defvjp(fwd, bwd)
    return op(x, w)


def rowwise(name, fn, rows, segs, params, out_widths, nct, diff_rows=None):
    n_row, n_seg, n_par, n_out = len(rows), len(segs), len(params), len(out_widths)
    diff_rows = [True] * n_row if diff_rows is None else list(diff_rows)
    r_total = rows[0].shape[0]
    n_tiles = r_total // ROW_TILE

    def seg_of(i):
        return jnp.where(i < nct, 0, 1)

    def row_spec(width):
        return pl.BlockSpec((ROW_TILE, width), lambda i: (i, 0))

    def seg_spec(shape):
        nd = len(shape)
        return pl.BlockSpec((1,) + tuple(shape[1:]), lambda i: (seg_of(i),) + (0,) * (nd - 1))

    def par_spec(shape):
        nd = len(shape)
        return pl.BlockSpec(tuple(shape), lambda i: (0,) * nd)

    in_specs = ([row_spec(r.shape[1]) for r in rows] + [seg_spec(s.shape) for s in segs]
                + [par_spec(p.shape) for p in params])

    def load(refs):
        vals = [r[...].astype(F32) for r in refs[:n_row]]
        vals += [r[0].astype(F32) for r in refs[n_row:n_row + n_seg]]
        vals += [r[...].astype(F32) for r in refs[n_row + n_seg:n_row + n_seg + n_par]]
        return vals

    def fwd_call(arrs):
        def body(*refs):
            outs = fn(*load(refs))
            for o_ref, val in zip(refs[n_row + n_seg + n_par:], outs):
                o_ref[...] = val

        return pl.pallas_call(
            body, name=name + "_f", grid=(n_tiles,), in_specs=in_specs,
            out_specs=[row_spec(w) for w in out_widths],
            out_shape=[jax.ShapeDtypeStruct((r_total, w), F32) for w in out_widths],
            compiler_params=_cparams(1),
        )(*arrs)

    d_idx = [k for k in range(n_row) if diff_rows[k]]

    def bwd_call(arrs, douts):
        n_in = n_row + n_seg + n_par

        def body(*refs):
            i = pl.program_id(0)
            vals = load(refs[:n_in])
            gs = [r[...] for r in refs[n_in:n_in + n_out]]
            out_refs = refs[n_in + n_out:]
            diff_pos = d_idx + list(range(n_row, n_in))

            def f(*dv):
                full = list(vals)
                for pos, v in zip(diff_pos, dv):
                    full[pos] = v
                return tuple(fn(*full))

            _, vjp = jax.vjp(f, *[vals[p] for p in diff_pos])
            grads = vjp(tuple(gs))
            nd = len(d_idx)
            for o_ref, g in zip(out_refs[:nd], grads[:nd]):
                o_ref[...] = g
            first_seg = jnp.logical_or(i == 0, i == nct)
            for o_ref, g in zip(out_refs[nd:nd + n_seg], grads[nd:nd + n_seg]):
                @pl.when(first_seg)
                def _(o_ref=o_ref, g=g):
                    o_ref[0] = g

                @pl.when(jnp.logical_not(first_seg))
                def _(o_ref=o_ref, g=g):
                    o_ref[0] += g
            for o_ref, g in zip(out_refs[nd + n_seg:], grads[nd + n_seg:]):
                @pl.when(i == 0)
                def _(o_ref=o_ref, g=g):
                    o_ref[...] = g

                @pl.when(i != 0)
                def _(o_ref=o_ref, g=g):
                    o_ref[...] += g

        out_specs = ([row_spec(rows[k].shape[1]) for k in d_idx] + [seg_spec(s.shape) for s in segs]
                     + [par_spec(p.shape) for p in params])
        out_shape = ([jax.ShapeDtypeStruct(rows[k].shape, F32) for k in d_idx]
                     + [jax.ShapeDtypeStruct(s.shape, F32) for s in segs]
                     + [jax.ShapeDtypeStruct(p.shape, F32) for p in params])
        return pl.pallas_call(
            body, name=name + "_b", grid=(n_tiles,),
            in_specs=in_specs + [row_spec(w) for w in out_widths],
            out_specs=out_specs, out_shape=out_shape, compiler_params=_cparams(1),
        )(*arrs, *douts)

    @jax.custom_vjp
    def op(*arrs):
        return tuple(fwd_call(arrs))

    def op_fwd(*arrs):
        return tuple(fwd_call(arrs)), arrs

    def op_bwd(arrs, douts):
        grads = list(bwd_call(arrs, douts))
        nd = len(d_idx)
        row_grads = [jnp.zeros_like(arrs[k]) for k in range(n_row)]
        for k, g in zip(d_idx, grads[:nd]):
            row_grads[k] = g
        return tuple(row_grads + grads[nd:])

    op.defvjp(op_fwd, op_bwd)
    return op(*rows, *segs, *params)


ATT_SCALE = MLA_QK ** -0.5
LOG2E = math.log2(math.e)
ATT_KEY_CHUNKS = (768, 512, 256)


ATT_LATENT_TILE = 1024


def _query_rows_spec(row0, tq):
    return pl.BlockSpec((pl.Element(tq), pl.Element(HEAD_PAD)),
                        lambda h, i: (pl.multiple_of(row0 + i * tq, ROW_TILE), pl.multiple_of(h * HEAD_PAD, HEAD_PAD)))


def _key_chunks(nk):
    kc = _pick(nk, ATT_KEY_CHUNKS)
    return [(c * kc, kc) for c in range(nk // kc)]


def _attn_fwd_call(q, k, v, row0, n_rows, tq, nk, name):
    def body(q_ref, k_ref, v_ref, o_ref, lse_ref):
        qv = q_ref[...]
        m = jnp.full((tq, 1), -jnp.inf, F32)
        l = jnp.zeros((tq, 1), F32)
        acc = jnp.zeros((tq, HEAD_PAD), F32)
        for start, size in _key_chunks(nk):
            s = lax.dot_general(qv, k_ref[start:start + size, :], _DOT_DIMS["nt"], preferred_element_type=F32)
            m_new = jnp.maximum(m, jnp.max(s, axis=-1, keepdims=True))
            alpha = jnp.exp2(m - m_new)
            p = jnp.exp2(s - m_new)
            l = alpha * l + jnp.sum(p, axis=-1, keepdims=True)
            acc = alpha * acc + lax.dot_general(p.astype(BF16), v_ref[start:start + size, :], _DOT_DIMS["nn"],
                                                preferred_element_type=F32)
            m = m_new
        o_ref[...] = acc / l
        lse_ref[...] = jnp.broadcast_to(m + jnp.log2(l), (tq, HEAD_PAD))

    out_spec = pl.BlockSpec((tq, HEAD_PAD), lambda h, i: (i, h))
    kv_spec = pl.BlockSpec((nk, HEAD_PAD), lambda h, i: (0, h))
    out = jax.ShapeDtypeStruct((n_rows, q.shape[1]), F32)
    return pl.pallas_call(
        body, name=name, grid=(MLA_HEADS, n_rows // tq), in_specs=[_query_rows_spec(row0, tq), kv_spec, kv_spec],
        out_specs=[out_spec, out_spec], out_shape=[out, out], compiler_params=_cparams(2),
    )(q, k, v)


def _attn_bwd_call(q, k, v, o, lse, do, row0, n_rows, tq, nk, name):
    nq = n_rows // tq

    def body(q_ref, k_ref, v_ref, o_ref, lse_ref, do_ref, dq_ref, dk_ref, dv_ref):
        i = pl.program_id(1)

        @pl.when(i == 0)
        def _():
            dk_ref[...] = jnp.zeros_like(dk_ref)
            dv_ref[...] = jnp.zeros_like(dv_ref)

        qv = q_ref[...]
        dov = do_ref[...]
        dob = dov.astype(BF16)
        lse = lse_ref[:, 0:1]
        delta = jnp.sum(dov * o_ref[...], axis=-1, keepdims=True)
        dq = jnp.zeros((tq, HEAD_PAD), F32)
        for start, size in _key_chunks(nk):
            kk = k_ref[start:start + size, :]
            vv = v_ref[start:start + size, :]
            s = lax.dot_general(qv, kk, _DOT_DIMS["nt"], preferred_element_type=F32)
            p = jnp.exp2(s - lse)
            dp = lax.dot_general(dob, vv, _DOT_DIMS["nt"], preferred_element_type=F32)
            g = (p * (dp - delta)).astype(BF16)
            dk_ref[start:start + size, :] += lax.dot_general(g, qv, _DOT_DIMS["tn"], preferred_element_type=F32)
            dv_ref[start:start + size, :] += lax.dot_general(p.astype(BF16), dob, _DOT_DIMS["tn"],
                                                             preferred_element_type=F32)
            dq = dq + lax.dot_general(g, kk, _DOT_DIMS["nn"], preferred_element_type=F32)
        dq_ref[...] = dq * ATT_SCALE

        @pl.when(i == nq - 1)
        def _():
            dk_ref[...] = dk_ref[...] * (1.0 / LOG2E)

    own_spec = pl.BlockSpec((tq, HEAD_PAD), lambda h, i: (i, h))
    kv_spec = pl.BlockSpec((nk, HEAD_PAD), lambda h, i: (0, h))
    rows_spec = _query_rows_spec(row0, tq)
    return pl.pallas_call(
        body, name=name, grid=(MLA_HEADS, nq),
        in_specs=[rows_spec, kv_spec, kv_spec, own_spec, own_spec, rows_spec],
        out_specs=[own_spec, kv_spec, kv_spec],
        out_shape=[jax.ShapeDtypeStruct((n_rows, q.shape[1]), F32), jax.ShapeDtypeStruct((nk, q.shape[1]), F32),
                   jax.ShapeDtypeStruct((nk, q.shape[1]), F32)],
        compiler_params=_cparams(2),
    )(q, k, v, o, lse, do)


def attention(q, k, v, tc, name):
    r_total = q.shape[0]
    tq_lat = _pick(r_total - tc, (ATT_LATENT_TILE, ROW_TILE))
    ranges = [(0, tc, ROW_TILE, tc, "_ctx"), (tc, r_total - tc, tq_lat, r_total, "_lat")]

    def operands(q, k, v):
        return (q * (ATT_SCALE * LOG2E)).astype(BF16), k.astype(BF16), v.astype(BF16)

    def forward(qb, kb, vb):
        return [_attn_fwd_call(qb, kb, vb, row0, n_rows, tq, nk, name + tag + "_f")
                for row0, n_rows, tq, nk, tag in ranges]

    @jax.custom_vjp
    def op(q, k, v):
        return jnp.concatenate([o for o, _ in forward(*operands(q, k, v))], axis=0)

    def fwd(q, k, v):
        qb, kb, vb = operands(q, k, v)
        parts = forward(qb, kb, vb)
        return jnp.concatenate([o for o, _ in parts], axis=0), (qb, kb, vb, parts)

    def bwd(res, do):
        qb, kb, vb, parts = res
        (dq_c, dk_c, dv_c), (dq_l, dk_l, dv_l) = [
            _attn_bwd_call(qb, kb, vb, o, lse, do, row0, n_rows, tq, nk, name + tag + "_b")
            for (o, lse), (row0, n_rows, tq, nk, tag) in zip(parts, ranges)]
        grow = lambda part: jnp.pad(part, ((0, r_total - tc), (0, 0)))
        return jnp.concatenate([dq_c, dq_l], axis=0), dk_l + grow(dk_c), dv_l + grow(dv_c)

    op.defvjp(fwd, bwd)
    return op(q, k, v)


CHUNK_SHIFT = CHUNK.bit_length() - 1


def _block_pairs():
    rows = lax.broadcasted_iota(jnp.int32, (ROW_TILE, ROW_TILE), 0)
    cols = lax.broadcasted_iota(jnp.int32, (ROW_TILE, ROW_TILE), 1)
    same = lax.shift_right_logical(rows, CHUNK_SHIFT) == lax.shift_right_logical(cols, CHUNK_SHIFT)
    return rows, cols, same


def _block_mask(kind):
    rows, cols, same = _block_pairs()
    order = {"lower_incl": rows >= cols, "upper_incl": rows <= cols, "lower_strict": rows > cols,
             "upper_strict": rows < cols}[kind]
    return jnp.logical_and(same, order)


def _row_chunk():
    return lax.shift_right_logical(lax.broadcasted_iota(jnp.int32, (ROW_TILE, 1), 0), CHUNK_SHIFT)


def _dot01(kind, x):
    m = _block_mask(kind).astype(BF16)
    hi = x.astype(BF16)
    rest = x - hi.astype(F32)
    mid = rest.astype(BF16)
    lo = (rest - mid.astype(F32)).astype(BF16)
    terms = jnp.concatenate([hi, mid, lo], axis=1)
    out = lax.dot_general(m, terms, _DOT_DIMS["nn"], preferred_element_type=F32)
    n = x.shape[1]
    return out[:, :n] + out[:, n:2 * n] + out[:, 2 * n:]


def _chunk_sums(x, forward):
    kinds = ("lower_incl", "upper_strict") if forward else ("upper_incl", "lower_strict")
    transposed = ("upper_incl", "lower_strict") if forward else ("lower_incl", "upper_strict")

    @jax.custom_vjp
    def f(x):
        return _dot01(kinds[0], x), _dot01(kinds[1], x)

    def fwd(x):
        return (_dot01(kinds[0], x), _dot01(kinds[1], x)), None

    def bwd(_, g):
        return (_dot01(transposed[0], g[0]) + _dot01(transposed[1], g[1]),)

    f.defvjp(fwd, bwd)
    return f(x)


def _scan_order(forward):
    if forward:
        return list(range(SCAN_CHUNKS)), lambda c: c * CHUNK + CHUNK - 1
    return list(range(SCAN_CHUNKS - 1, -1, -1)), lambda c: c * CHUNK


def _carry_states(forward, st0, inc_all, decay_of):
    order, _ = _scan_order(forward)
    entering = [None] * SCAN_CHUNKS
    st = st0
    for c in order:
        entering[c] = st
        st = st * decay_of(c) + inc_all[:, c * HEAD_PAD:(c + 1) * HEAD_PAD]
    return jnp.concatenate(entering, axis=0), st


def _per_chunk_lanes(x):
    chunk = _row_chunk()
    return jnp.concatenate([jnp.where(chunk == c, x, 0.0) for c in range(SCAN_CHUNKS)], axis=1)


def _own_chunk_lanes(x4):
    chunk = _row_chunk()
    n = x4.shape[1] // SCAN_CHUNKS
    out = jnp.where(chunk == 0, x4[:, :n], 0.0)
    for c in range(1, SCAN_CHUNKS):
        out = out + jnp.where(chunk == c, x4[:, c * n:(c + 1) * n], 0.0)
    return out


def _gla_block(forward, q, k, v, la, st0):
    cum, after = _chunk_sums(la, forward)
    _, last_row = _scan_order(forward)
    q_dec = q * (jnp.exp(cum) * (GLA_DK ** -0.5))
    att = _bdot(q_dec, k * jnp.exp(-cum), "nt")
    att = jnp.where(_block_mask("lower_incl" if forward else "upper_strict"), att, 0.0)
    inc_all = _bdot(v, _per_chunk_lanes(k * jnp.exp(after)), "tn")
    entering, st1 = _carry_states(forward, st0, inc_all,
                                  lambda c: jnp.exp(cum[last_row(c):last_row(c) + 1, :]))
    o = _bdot(att, v, "nn") + _own_chunk_lanes(_bdot(q_dec, entering, "nt"))
    return o, st1


def _ret_block(forward, q, k, v, rd, st0):
    lg = -jnp.exp(rd[0:1, 0:1])
    rows, cols, _ = _block_pairs()
    pos = jnp.bitwise_and(lax.broadcasted_iota(jnp.int32, (ROW_TILE, 1), 0), CHUNK - 1).astype(F32)
    if forward:
        to_end, from_start, rel = CHUNK - 1.0 - pos, pos + 1.0, (rows - cols).astype(F32)
    else:
        to_end, from_start, rel = pos, CHUNK - pos, (cols - rows).astype(F32)
    mask = _block_mask("lower_incl" if forward else "upper_strict")
    dmat = jnp.where(mask, jnp.exp(jnp.where(mask, rel, 0.0) * lg), 0.0)
    att = _bdot(q, k, "nt") * dmat
    inc_all = _bdot(v, _per_chunk_lanes(k * jnp.exp(to_end * lg)), "tn")
    entering, st1 = _carry_states(forward, st0, inc_all, lambda c: jnp.exp(CHUNK * lg))
    o = _bdot(att, v, "nn") + _own_chunk_lanes(_bdot(q, entering, "nt")) * jnp.exp(from_start * lg)
    return o, st1


def scan(kind, forward, q, k, v, aux, tc, name):
    heads = q.shape[1] // HEAD_PAD
    r_total = q.shape[0]
    nblk = r_total // ROW_TILE
    nctb = tc // ROW_TILE
    block_fn = functools.partial(_gla_block if kind == "gla" else _ret_block, forward)
    per_row_aux = kind == "gla"

    def blk(g):
        if forward:
            return g
        return jnp.where(g < nctb, nctb - 1 - g, nblk - 1 - (g - nctb))

    def specs(step_to_g):
        row = pl.BlockSpec((ROW_TILE, heads * HEAD_PAD), lambda s: (blk(step_to_g(s)), 0))
        aux_spec = row if per_row_aux else pl.BlockSpec((heads, 8, HEAD_PAD), lambda s: (0, 0, 0))
        st = pl.BlockSpec((1, heads, HEAD_PAD, HEAD_PAD), lambda s: (step_to_g(s), 0, 0, 0))
        return row, aux_spec, st

    def head_cols(h):
        return slice(h * HEAD_PAD, (h + 1) * HEAD_PAD)

    def fwd_call(q, k, v, aux):
        row, aux_spec, st_spec = specs(lambda s: s)

        def body(q_ref, k_ref, v_ref, a_ref, o_ref, st0_ref, st_ref):
            @pl.when(pl.program_id(0) == 0)
            def _():
                st_ref[...] = jnp.zeros_like(st_ref)

            qv, kv, vv = q_ref[...], k_ref[...], v_ref[...]
            outs = []
            for h in range(heads):
                st0 = st_ref[h]
                st0_ref[0, h] = st0
                a = a_ref[:, head_cols(h)] if per_row_aux else a_ref[h]
                o, st1 = block_fn(qv[:, head_cols(h)], kv[:, head_cols(h)], vv[:, head_cols(h)], a, st0)
                outs.append(o)
                st_ref[h] = st1
            o_ref[...] = jnp.concatenate(outs, axis=1)

        return pl.pallas_call(
            body, name=name + "_f", grid=(nblk,), in_specs=[row, row, row, aux_spec],
            out_specs=[row, st_spec],
            out_shape=[jax.ShapeDtypeStruct(q.shape, F32),
                       jax.ShapeDtypeStruct((nblk, heads, HEAD_PAD, HEAD_PAD), F32)],
            scratch_shapes=[pltpu.VMEM((heads, HEAD_PAD, HEAD_PAD), F32)],
            compiler_params=_cparams(1),
        )(q, k, v, aux)

    def bwd_call(q, k, v, aux, st0s, do):
        row, aux_spec, st_spec = specs(lambda s: nblk - 1 - s)

        def body(q_ref, k_ref, v_ref, a_ref, st0_ref, do_ref, dq_ref, dk_ref, dv_ref, da_ref, dst_ref):
            s = pl.program_id(0)

            @pl.when(s == 0)
            def _():
                dst_ref[...] = jnp.zeros_like(dst_ref)

            qv, kv, vv, dov = q_ref[...], k_ref[...], v_ref[...], do_ref[...]
            grads = []
            for h in range(heads):
                a = a_ref[:, head_cols(h)] if per_row_aux else a_ref[h]
                _, vjp = jax.vjp(block_fn, qv[:, head_cols(h)], kv[:, head_cols(h)], vv[:, head_cols(h)], a,
                                 st0_ref[0, h])
                dq, dk, dv, da, dst0 = vjp((dov[:, head_cols(h)], dst_ref[h]))
                dst_ref[h] = dst0
                grads.append((dq, dk, dv, da))
            dq_ref[...] = jnp.concatenate([g[0] for g in grads], axis=1)
            dk_ref[...] = jnp.concatenate([g[1] for g in grads], axis=1)
            dv_ref[...] = jnp.concatenate([g[2] for g in grads], axis=1)
            if per_row_aux:
                da_ref[...] = jnp.concatenate([g[3] for g in grads], axis=1)
            else:
                da = jnp.stack([g[3] for g in grads], axis=0)

                @pl.when(s == 0)
                def _():
                    da_ref[...] = da

                @pl.when(s != 0)
                def _():
                    da_ref[...] += da

        return pl.pallas_call(
            body, name=name + "_b", grid=(nblk,),
            in_specs=[row, row, row, aux_spec, st_spec, row],
            out_specs=[row, row, row, aux_spec],
            out_shape=[jax.ShapeDtypeStruct(q.shape, F32)] * 3 + [jax.ShapeDtypeStruct(aux.shape, F32)],
            scratch_shapes=[pltpu.VMEM((heads, HEAD_PAD, HEAD_PAD), F32)],
            compiler_params=_cparams(1),
        )(q, k, v, aux, st0s, do)

    @jax.custom_vjp
    def op(q, k, v, aux):
        return fwd_call(q, k, v, aux)[0]

    def fwd(q, k, v, aux):
        o, st0s = fwd_call(q, k, v, aux)
        return o, (q, k, v, aux, st0s)

    def bwd(res, do):
        return tuple(bwd_call(*res, do))

    op.defvjp(fwd, bwd)
    return op(q, k, v, aux)


HALO = 8


def _neighbours(main, prev8, next8, i, nct, n_tiles):
    has_prev = jnp.logical_and(i != 0, i != nct).astype(F32)
    has_next = jnp.logical_and(i != nct - 1, i != n_tiles - 1).astype(F32)
    row = lax.broadcasted_iota(jnp.int32, main.shape, 0)
    down = jnp.where(row == 0, prev8[HALO - 1:HALO] * has_prev, pltpu.roll(main, 1, 0))
    up = jnp.where(row == ROW_TILE - 1, next8[0:1] * has_next, pltpu.roll(main, ROW_TILE - 1, 0))
    return down, up


def dwconv(x, w8, b, tc, name):
    r_total, width = x.shape
    n_tiles = r_total // ROW_TILE
    nct = tc // ROW_TILE
    per = ROW_TILE // HALO
    main_spec = pl.BlockSpec((ROW_TILE, width), lambda i: (i, 0))
    prev_spec = pl.BlockSpec((HALO, width), lambda i: (jnp.maximum(i * per - 1, 0), 0))
    next_spec = pl.BlockSpec((HALO, width), lambda i: (jnp.minimum((i + 1) * per, r_total // HALO - 1), 0))
    w_spec = pl.BlockSpec((8, width), lambda i: (0, 0))
    b_spec = pl.BlockSpec((1, width), lambda i: (0, 0))

    def fwd_call(x, w8, b):
        def body(x_ref, p_ref, n_ref, w_ref, b_ref, o_ref):
            xv = x_ref[...]
            down, up = _neighbours(xv, p_ref[...], n_ref[...], pl.program_id(0), nct, n_tiles)
            o_ref[...] = w_ref[0:1] * down + w_ref[1:2] * xv + w_ref[2:3] * up + b_ref[...]

        return pl.pallas_call(
            body, name=name + "_f", grid=(n_tiles,), in_specs=[main_spec, prev_spec, next_spec, w_spec, b_spec],
            out_specs=main_spec, out_shape=jax.ShapeDtypeStruct(x.shape, F32), compiler_params=_cparams(1),
        )(x, x, x, w8, b)

    def bwd_call(x, w8, g):
        def body(x_ref, xp_ref, xn_ref, g_ref, gp_ref, gn_ref, w_ref, dx_ref, dw_ref, db_ref):
            i = pl.program_id(0)
            xv, gv = x_ref[...], g_ref[...]
            x_down, x_up = _neighbours(xv, xp_ref[...], xn_ref[...], i, nct, n_tiles)
            g_down, g_up = _neighbours(gv, gp_ref[...], gn_ref[...], i, nct, n_tiles)
            dx_ref[...] = w_ref[0:1] * g_up + w_ref[1:2] * gv + w_ref[2:3] * g_down
            dw = jnp.concatenate([jnp.sum(gv * x_down, axis=0, keepdims=True),
                                  jnp.sum(gv * xv, axis=0, keepdims=True),
                                  jnp.sum(gv * x_up, axis=0, keepdims=True),
                                  jnp.zeros((5, width), F32)], axis=0)
            db = jnp.sum(gv, axis=0, keepdims=True)

            @pl.when(i == 0)
            def _():
                dw_ref[...] = dw
                db_ref[...] = db

            @pl.when(i != 0)
            def _():
                dw_ref[...] += dw
                db_ref[...] += db

        return pl.pallas_call(
            body, name=name + "_b", grid=(n_tiles,),
            in_specs=[main_spec, prev_spec, next_spec, main_spec, prev_spec, next_spec, w_spec],
            out_specs=[main_spec, w_spec, b_spec],
            out_shape=[jax.ShapeDtypeStruct(x.shape, F32), jax.ShapeDtypeStruct((8, width), F32),
                       jax.ShapeDtypeStruct((1, width), F32)],
            compiler_params=_cparams(1),
        )(x, x, x, g, g, g, w8)

    @jax.custom_vjp
    def op(x, w8, b):
        return fwd_call(x, w8, b)

    def fwd(x, w8, b):
        return fwd_call(x, w8, b), (x, w8)

    def bwd(res, g):
        return tuple(bwd_call(*res, g))

    op.defvjp(fwd, bwd)
    return op(x, w8, b)


def loss_head(h, target, tc, name):
    r_total, width = h.shape
    n_tiles = r_total // ROW_TILE
    nct = tc // ROW_TILE

    def call(h, target):
        def body(h_ref, t_ref, dh_ref, loss_ref, acc_ref):
            i = pl.program_id(0)

            @pl.when(i == 0)
            def _():
                acc_ref[...] = jnp.zeros_like(acc_ref)

            @pl.when(i < nct)
            def _():
                dh_ref[...] = jnp.zeros_like(dh_ref)

            @pl.when(i >= nct)
            def _():
                err = h_ref[...] - t_ref[...]
                dh_ref[...] = err * (1.0 / width)
                acc_ref[...] += jnp.sum((err * err).reshape(ROW_TILE // 8, 8, width), axis=0)

            @pl.when(i == n_tiles - 1)
            def _():
                loss_ref[...] = jnp.sum(acc_ref[...]).reshape(1, 1) * (0.5 / width)

        row = pl.BlockSpec((ROW_TILE, width), lambda i: (i, 0))
        return pl.pallas_call(
            body, name=name, grid=(n_tiles,),
            in_specs=[row, pl.BlockSpec((ROW_TILE, width), lambda i: (jnp.maximum(i - nct, 0), 0))],
            out_specs=[row, pl.BlockSpec((1, 1), lambda i: (0, 0))],
            out_shape=[jax.ShapeDtypeStruct(h.shape, F32), jax.ShapeDtypeStruct((1, 1), F32)],
            scratch_shapes=[pltpu.VMEM((8, width), F32)], compiler_params=_cparams(1),
        )(h, target)

    @jax.custom_vjp
    def op(h, target):
        return call(h, target)[1][0, 0]

    def fwd(h, target):
        dh, loss = call(h, target)
        return loss[0, 0], (dh, target)

    def bwd(res, g):
        dh, target = res
        return dh * g, jnp.zeros_like(target)

    op.defvjp(fwd, bwd)
    return op(h, target)


PACK_W = 1024
PACK_TILE = 128


def slab_sum(slabs, name):
    n_slab, n, _ = slabs.shape

    def body(s_ref, o_ref):
        acc = s_ref[0]
        for j in range(1, n_slab):
            acc = acc + s_ref[j]
        o_ref[...] = acc

    return pl.pallas_call(
        body, name=name, grid=(n // PACK_TILE,),
        in_specs=[pl.BlockSpec((n_slab, PACK_TILE, PACK_W), lambda i: (0, i, 0))],
        out_specs=pl.BlockSpec((PACK_TILE, PACK_W), lambda i: (i, 0)),
        out_shape=jax.ShapeDtypeStruct((n, PACK_W), F32), compiler_params=_cparams(1),
    )(slabs)


def adamw(g_slabs, w, m, v, name):
    n_slab, n, _ = g_slabs.shape

    def body(g_ref, w_ref, m_ref, v_ref, go_ref, d_ref, mo_ref, vo_ref):
        g = g_ref[0].astype(F32)
        for j in range(1, n_slab):
            g = g + g_ref[j].astype(F32)
        m_new = ADAM_B1 * m_ref[...] + (1.0 - ADAM_B1) * g
        v_new = ADAM_B2 * v_ref[...] + (1.0 - ADAM_B2) * (g * g)
        m_hat = m_new / (1.0 - ADAM_B1 ** ADAM_STEP)
        v_hat = v_new / (1.0 - ADAM_B2 ** ADAM_STEP)
        go_ref[...] = g
        d_ref[...] = -ADAM_LR * (m_hat / (jnp.sqrt(v_hat) + ADAM_EPS) + ADAM_WD * w_ref[...])
        mo_ref[...] = m_new
        vo_ref[...] = v_new

    flat = pl.BlockSpec((PACK_TILE, PACK_W), lambda i: (i, 0))
    return pl.pallas_call(
        body, name=name, grid=(n // PACK_TILE,),
        in_specs=[pl.BlockSpec((n_slab, PACK_TILE, PACK_W), lambda i: (0, i, 0)), flat, flat, flat],
        out_specs=[flat] * 4, out_shape=[jax.ShapeDtypeStruct((n, PACK_W), F32)] * 4, compiler_params=_cparams(1),
    )(g_slabs, w, m, v)


def all_gather(x, name):
    m_per, n = x.shape

    def body(x_ref, out_ref, send_sems, recv_sems, local_sem):
        px, py, pc = lax.axis_index("x"), lax.axis_index("y"), lax.axis_index("c")
        me, sibling = (px, py, pc), (px, py, 1 - pc)
        chips = [(1 - px, py), (px, 1 - py), (1 - px, 1 - py)]

        def rows(bx, by, bc):
            return out_ref.at[pl.ds((4 * bx + 2 * by + bc) * m_per, m_per), :]

        def copy(k, block, to, src=None):
            return pltpu.make_async_remote_copy(
                src_ref=rows(*block) if src is None else src, dst_ref=rows(*block),
                send_sem=send_sems.at[k], recv_sem=recv_sems.at[k], device_id=to, device_id_type=MESH)

        mine = pltpu.make_async_copy(x_ref, rows(*me), local_sem)
        mine.start()
        first = [copy(0, me, sibling, src=x_ref)]
        first += [copy(1 + j, me, (*chip, pc), src=x_ref) for j, chip in enumerate(chips)]
        for cp in first:
            cp.start()
        passed = [copy(4 + j, (*chip, pc), sibling) for j, chip in enumerate(chips)]
        for j, chip in enumerate(chips):
            copy(1 + j, (*chip, pc), me).wait_recv()
            passed[j].start()
        copy(0, sibling, me).wait_recv()
        for j, chip in enumerate(chips):
            copy(4 + j, (*chip, 1 - pc), me).wait_recv()
        for cp in first + passed:
            cp.wait_send()
        mine.wait()

    return pl.pallas_call(
        body, name=name, out_shape=jax.ShapeDtypeStruct((N_DEV * m_per, n), x.dtype),
        in_specs=[pl.BlockSpec(memory_space=pl.ANY)], out_specs=pl.BlockSpec(memory_space=pl.ANY),
        scratch_shapes=[pltpu.SemaphoreType.DMA((7,)), pltpu.SemaphoreType.DMA((7,)), pltpu.SemaphoreType.DMA],
    )(x)


def all_to_all(x, name):
    _, n, cols = x.shape

    def body(x_ref, out_ref, send_sems, recv_sems, local_sem):
        px, py, pc = lax.axis_index("x"), lax.axis_index("y"), lax.axis_index("c")
        mine_idx = 4 * px + 2 * py + pc
        flips = [(fx, fy, fc) for fx in (0, 1) for fy in (0, 1) for fc in (0, 1)][1:]
        local = pltpu.make_async_copy(x_ref.at[mine_idx], out_ref.at[mine_idx], local_sem)
        local.start()
        copies = []
        for k, (fx, fy, fc) in enumerate(flips):
            qx, qy, qc = px ^ fx, py ^ fy, pc ^ fc
            peer_idx = 4 * qx + 2 * qy + qc
            copies.append((
                pltpu.make_async_remote_copy(
                    src_ref=x_ref.at[peer_idx], dst_ref=out_ref.at[mine_idx], send_sem=send_sems.at[k],
                    recv_sem=recv_sems.at[k], device_id=(qx, qy, qc), device_id_type=MESH),
                pltpu.make_async_remote_copy(
                    src_ref=x_ref.at[peer_idx], dst_ref=out_ref.at[peer_idx], send_sem=send_sems.at[k],
                    recv_sem=recv_sems.at[k], device_id=(qx, qy, qc), device_id_type=MESH)))
        for send, _ in copies:
            send.start()
        for _, landing in copies:
            landing.wait_recv()
        for send, _ in copies:
            send.wait_send()
        local.wait()

    return pl.pallas_call(
        body, name=name, out_shape=jax.ShapeDtypeStruct(x.shape, x.dtype),
        in_specs=[pl.BlockSpec(memory_space=pl.ANY)], out_specs=pl.BlockSpec(memory_space=pl.ANY),
        scratch_shapes=[pltpu.SemaphoreType.DMA((7,)), pltpu.SemaphoreType.DMA((7,)), pltpu.SemaphoreType.DMA],
    )(x)


IN_OFFSETS = {}
_off = 0
for _name, _width in (("mla_q", 256), ("mla_kv", 128), ("mla_kr", 32), ("gla_q", 512), ("gla_k", 512), ("gla_v", 512),
                      ("gla_g", 512), ("gla_rf", 16), ("gla_rb", 16), ("ret_q", 512), ("ret_k", 512), ("ret_v", 512),
                      ("ret_g", 512), ("gate_mla", 1024), ("gate_gla", 1024), ("gate_ret", 1024)):
    IN_OFFSETS[_name] = (_off, _off + _width)
    _off += _width
N_IN = _off

P_GLA, P_RET, P_GATE, P_MLAQ, P_MLAKV, P_MLAKR, P_RANK, P_END = 0, 2048, 4096, 7168, 7424, 7552, 7680, 7808


P_BOUNDS = ([P_GLA + n * 512 for n in range(4)] + [P_RET + n * 512 for n in range(4)]
            + [P_GATE + n * D for n in range(3)] + [P_MLAQ, P_MLAKV, P_MLAKR, P_RANK, P_END])


def _split_columns(p, starts):
    ends = list(starts[1:]) + [p.shape[1]]

    def split(p):
        return tuple(p[:, a:b] for a, b in zip(starts, ends))

    @jax.custom_vjp
    def op(p):
        return split(p)

    def fwd(p):
        return split(p), None

    def bwd(_, g):
        return (jnp.concatenate(g, axis=1),)

    op.defvjp(fwd, bwd)
    return op(p)


def _pad_in_proj(w):
    def cols(a, b):
        return w[:, IN_OFFSETS[a][0]:IN_OFFSETS[b][1]]

    def z(n):
        return jnp.zeros((w.shape[0], n), w.dtype)

    return jnp.concatenate([cols("gla_q", "gla_g"), cols("ret_q", "ret_g"), cols("gate_mla", "gate_ret"),
                            cols("mla_q", "mla_kv"), z(MLA_NOPE), cols("mla_kr", "mla_kr"),
                            z(HEAD_PAD - MLA_QK), cols("gla_rf", "gla_rb"), z(HEAD_PAD - 2 * GLA_RANK),
                            z(N_IN_PAD - P_END)], axis=1)


def _pad_last(a, n):
    return jnp.pad(a, [(0, 0)] * (a.ndim - 1) + [(0, n - a.shape[-1])])


def _position_tables(tc, t):
    pos = jnp.arange(t)
    inv = ROPE_THETA ** (-jnp.arange(MLA_ROPE // 4, dtype=F32) * 2.0 / (MLA_ROPE // 2))
    ang_r = (pos // GRID_W).astype(F32)[:, None] * inv[None, :]
    ang_c = (pos % GRID_W).astype(F32)[:, None] * inv[None, :]
    z8, z32, z64 = jnp.zeros((t, 8), F32), jnp.zeros((t, 32), F32), jnp.zeros((t, 64), F32)
    lat_c = jnp.concatenate([jnp.ones((t, 64), F32), jnp.cos(ang_r), jnp.cos(ang_r), jnp.cos(ang_c), jnp.cos(ang_c),
                             z32], axis=1)
    lat_sn = jnp.concatenate([z64, -jnp.sin(ang_r), z8, -jnp.sin(ang_c), z8, z32], axis=1)
    lat_sp = jnp.concatenate([z64, z8, jnp.sin(ang_r), z8, jnp.sin(ang_c), z32], axis=1)
    ctx_c = jnp.concatenate([jnp.ones((tc, MLA_QK), F32), jnp.zeros((tc, HEAD_PAD - MLA_QK), F32)], axis=1)
    ctx_z = jnp.zeros((tc, HEAD_PAD), F32)
    rinv = 1.0 / (RET_THETA ** jnp.linspace(0.0, 1.0, RET_DK // 2, dtype=F32))
    rang = jnp.arange(tc + t).astype(F32)[:, None] * rinv[None, :]
    return dict(c=jnp.concatenate([ctx_c, lat_c]), sn=jnp.concatenate([ctx_z, lat_sn]),
                sp=jnp.concatenate([ctx_z, lat_sp]),
                rc=jnp.concatenate([jnp.cos(rang), jnp.cos(rang)], axis=1),
                rs=jnp.concatenate([-jnp.sin(rang), jnp.sin(rang)], axis=1))


def _heads(x):
    return [x[:, h * HEAD_PAD:(h + 1) * HEAD_PAD] for h in range(x.shape[1] // HEAD_PAD)]


def _mla_rope(x, c, sn, sp):
    return x * c + _roll(x, HEAD_PAD - 8, 1) * sn + _roll(x, 8, 1) * sp


def _norm_mod_fn(shift_row, scale_row):
    def fn(h, mod, w):
        return (_rms(h, D, w) * (1.0 + mod[scale_row:scale_row + 1]) + mod[shift_row:shift_row + 1],)
    return fn


def _resid_fn(gate_row):
    def fn(h, y, mod):
        return (h + mod[gate_row:gate_row + 1] * y,)
    return fn


def _q_fn(cq, c, sn, sp, norm_a, w_qb, q_norm):
    qf = _bdot(_rms(cq, MLA_Q_LORA, norm_a), w_qb, "nn")
    return (jnp.concatenate([_mla_rope(_rms(qh, MLA_QK, q_norm), c, sn, sp) for qh in _heads(qf)], axis=1),)


def _kv_fn(ckv, kr, c, sn, sp, norm_a, w_k, w_v, k_norm):
    x = _rms(ckv, MLA_KV_LORA, norm_a)
    kf = _bdot(x, w_k, "nn")
    k = jnp.concatenate([_mla_rope(_rms(kh + kr, MLA_QK, k_norm), c, sn, sp) for kh in _heads(kf)], axis=1)
    return k, _bdot(x, w_v, "nn")


def _decay_fn(ranks, w2, b):
    la = _log_sigmoid(_bdot(ranks, w2, "nn") + b) * (1.0 / GLA_NORMALIZER)
    return la[:, :GLA_HEADS * GLA_DK], la[:, GLA_HEADS * GLA_DK:]


def _ret_rot_fn(q, k, rc, rs):
    def rot(x, scale):
        return jnp.concatenate([(xh * rc + _roll(xh, RET_DK // 2, 1) * rs) * scale for xh in _heads(x)], axis=1)
    return rot(q, 1.0), rot(k, RET_DK ** -0.5)


def _gla_out_fn(o_f, o_b, g, w):
    y = jnp.concatenate([_rms(oh, HEAD_PAD, w) for oh in _heads(o_f + o_b)], axis=1)
    return (y * _silu(g),)


def _ret_out_fn(o_f, o_b, g):
    y = jnp.concatenate([_rms(oh, HEAD_PAD) for oh in _heads(o_f + o_b)], axis=1)
    return (y * _silu(g),)


def _merge_fn(z0, z1, z2, g0, g1, g2, bg):
    return (jax.nn.sigmoid(g0 + bg[0:1]) * z0 + jax.nn.sigmoid(g1 + bg[1:2]) * z1 + jax.nn.sigmoid(g2 + bg[2:3]) * z2,)


def _ffn_act_fn(c, up):
    return (_gelu_tanh(c) * up,)


def _layer(l, h, mod, w, tabs, tc):
    nct = tc // ROW_TILE
    tag = f"_l{l}"
    row = lambda a: a[l][None]
    a = rowwise("norm1" + tag, _norm_mod_fn(0, 1), [h], [mod], [row(w["norm1_w"])], [D], nct)[0]
    p = linear(a, _pad_in_proj(w["w_in"][l]), "in_proj" + tag)
    pieces = _split_columns(p, P_BOUNDS)
    piece = lambda start, width: pieces[P_BOUNDS.index(start)]

    w_qb = _pad_last(w["mla_w_qb"][l].reshape(MLA_Q_LORA, MLA_HEADS, MLA_QK), HEAD_PAD).reshape(MLA_Q_LORA, -1)
    w_kvb = w["mla_w_kvb"][l].reshape(MLA_KV_LORA, MLA_HEADS, MLA_NOPE + MLA_V)
    w_k = _pad_last(w_kvb[:, :, :MLA_NOPE], HEAD_PAD).reshape(MLA_KV_LORA, -1)
    w_v = _pad_last(w_kvb[:, :, MLA_NOPE:], HEAD_PAD).reshape(MLA_KV_LORA, -1)
    rope = [tabs["c"], tabs["sn"], tabs["sp"]]
    q = rowwise("mla_q" + tag, _q_fn, [piece(P_MLAQ, MLA_Q_LORA)] + rope, [],
                [row(w["mla_q_norm_a"]), w_qb, _pad_last(row(w["mla_q_norm"]), HEAD_PAD)],
                [MLA_HEADS * HEAD_PAD], nct, diff_rows=[True, False, False, False])[0]
    k, v = rowwise("mla_kv" + tag, _kv_fn, [piece(P_MLAKV, MLA_KV_LORA), piece(P_MLAKR, HEAD_PAD)] + rope, [],
                   [row(w["mla_kv_norm_a"]), w_k, w_v, _pad_last(row(w["mla_k_norm"]), HEAD_PAD)],
                   [MLA_HEADS * HEAD_PAD] * 2, nct, diff_rows=[True, True, False, False, False])
    y_mla = attention(q, k, v, tc, "attn" + tag)
    wb_mla = _pad_last(w["w_branch"][l, 0].reshape(MLA_HEADS, MLA_V, D).transpose(0, 2, 1), HEAD_PAD)
    wb_mla = wb_mla.transpose(0, 2, 1).reshape(MLA_HEADS * HEAD_PAD, D)

    w2 = jnp.zeros((HEAD_PAD, 2 * GLA_HEADS * GLA_DK), F32)
    w2 = w2.at[:GLA_RANK, :GLA_HEADS * GLA_DK].set(w["gla_w_gk2"][l, 0])
    w2 = w2.at[GLA_RANK:2 * GLA_RANK, GLA_HEADS * GLA_DK:].set(w["gla_w_gk2"][l, 1])
    la_f, la_b = rowwise("gla_decay" + tag, _decay_fn, [piece(P_RANK, HEAD_PAD)], [],
                         [w2, w["gla_b_gk"][l].reshape(1, -1)], [GLA_HEADS * GLA_DK] * 2, nct)
    gq, gk, gv, gg = [piece(P_GLA + n * 512, 512) for n in range(4)]
    o_f = scan("gla", True, gq, gk, gv, la_f, tc, "gla_fw" + tag)
    o_b = scan("gla", False, gq, gk, gv, la_b, tc, "gla_bw" + tag)
    y_gla = rowwise("gla_out" + tag, _gla_out_fn, [o_f, o_b, gg], [], [row(w["gla_o_norm"])], [512], nct)[0]

    rq, rk = rowwise("ret_rot" + tag, _ret_rot_fn, [piece(P_RET, 512), piece(P_RET + 512, 512), tabs["rc"], tabs["rs"]],
                     [], [], [512, 512], nct, diff_rows=[True, True, False, False])
    rv, rg = piece(P_RET + 1024, 512), piece(P_RET + 1536, 512)
    rd = jnp.broadcast_to(w["ret_decay"][l][:, :, None, None], (2, RET_HEADS, 8, HEAD_PAD))
    r_f = scan("ret", True, rq, rk, rv, rd[0], tc, "ret_fw" + tag)
    r_b = scan("ret", False, rq, rk, rv, rd[1], tc, "ret_bw" + tag)
    y_ret = rowwise("ret_out" + tag, _ret_out_fn, [r_f, r_b, rg], [], [], [512], nct)[0]

    z = [linear(y_mla, wb_mla, "branch_mla" + tag), linear(y_gla, w["w_branch"][l, 1], "branch_gla" + tag),
         linear(y_ret, w["w_branch"][l, 2], "branch_ret" + tag)]
    gates = [piece(P_GATE + n * D, D) for n in range(3)]
    merged = rowwise("merge" + tag, _merge_fn, z + gates, [], [_pad_rows(w["b_gate"][l], 8)], [D], nct)[0]
    y = linear(merged, w["w_out"][l], "w_out" + tag)
    h = rowwise("resid1" + tag, _resid_fn(2), [h, y], [mod], [], [D], nct)[0]

    a2 = rowwise("norm2" + tag, _norm_mod_fn(3, 4), [h], [mod], [row(w["norm2_w"])], [D], nct)[0]
    gate = linear(a2, w["w_ffn_in"][l][:, :D_FF], "ffn_gate" + tag)
    up = linear(a2, w["w_ffn_in"][l][:, D_FF:], "ffn_up" + tag)
    conv = dwconv(gate, _pad_rows(w["w_dw"][l], 8), row(w["b_dw"]), tc, "dwconv" + tag)
    u = rowwise("ffn_act" + tag, _ffn_act_fn, [conv, up], [], [], [D_FF], nct)[0]
    f = linear(u, w["w_ffn_out"][l], "ffn_out" + tag)
    return rowwise("resid2" + tag, _resid_fn(5), [h, f], [mod], [], [D], nct)[0]


def _pad_rows(a, n):
    return jnp.pad(a, [(0, n - a.shape[0])] + [(0, 0)] * (a.ndim - 1))


def local_loss(w, mod, x, ctx, target):
    tc, t = ctx.shape[0], x.shape[0]
    tabs = _position_tables(tc, t)
    h = jnp.concatenate([ctx, x], axis=0)
    for l in range(DEPTH):
        h = _layer(l, h, mod[l], w, tabs, tc)
    return loss_head(h, target, tc, "loss_head")


ADA_ROWS = 16


def ada_forward(cond_in, w_ada, b_loc):
    cols = w_ada.shape[2]

    def body(x_ref, w_ref, b_ref, o_ref):
        s = _silu(x_ref[...])
        for l in range(DEPTH):
            o_ref[l] = _dg(s, w_ref[l], "nn") + b_ref[l]

    return pl.pallas_call(
        body, name="ada_forward", out_shape=jax.ShapeDtypeStruct((DEPTH, ADA_ROWS, cols), F32),
        compiler_params=pltpu.CompilerParams(vmem_limit_bytes=VMEM_LIMIT_BYTES),
    )(cond_in, w_ada, b_loc)


def ada_backward(cond_in, g_loc, dmod_own, w_ada):
    cols = w_ada.shape[2]

    def body(x_ref, g_ref, own_ref, w_ref, gw_ref, dc_ref, gb_ref):
        x = x_ref[...]
        s = _silu(x)
        dcond = jnp.zeros((8, D), F32)
        for l in range(DEPTH):
            g_ctx = jnp.sum(g_ref[2 * l], axis=0, keepdims=True)
            g_rows = jnp.concatenate([g_ref[2 * l + 1], jnp.broadcast_to(g_ctx, (8, cols))], axis=0)
            keep = lax.broadcasted_iota(jnp.int32, (ADA_ROWS, cols), 0) <= N_DEV
            gw_ref[l] = _dg(s, jnp.where(keep, g_rows, 0.0), "tn")
            dcond = dcond + _dg(jnp.broadcast_to(g_ctx, (8, cols)), w_ref[l], "nt")
            gb_ref[l:l + 1, :] = own_ref[2 * l:2 * l + 1, :] + own_ref[2 * l + 1:2 * l + 2, :]
        xc = x[N_DEV:N_DEV + 1]
        sig = jax.nn.sigmoid(xc)
        dc_ref[...] = dcond[0:1] * (sig * (1.0 + xc * (1.0 - sig)))

    return pl.pallas_call(
        body, name="ada_backward",
        out_shape=[jax.ShapeDtypeStruct(w_ada.shape, F32), jax.ShapeDtypeStruct((1, D), F32),
                   jax.ShapeDtypeStruct((DEPTH, 6 * D), F32)],
        compiler_params=pltpu.CompilerParams(vmem_limit_bytes=VMEM_LIMIT_BYTES),
    )(cond_in, g_loc, dmod_own, w_ada)


WEIGHTS = ["c_ctx", "w_ada", "b_ada", "norm1_w", "norm2_w", "w_in", "b_gate", "mla_q_norm_a", "mla_w_qb",
           "mla_kv_norm_a", "mla_w_kvb", "mla_q_norm", "mla_k_norm", "gla_w_gk2", "gla_b_gk", "gla_o_norm",
           "ret_decay", "w_branch", "w_out", "w_ffn_in", "w_dw", "b_dw", "w_ffn_out"]
INPUTS = ["x", "c", "ctx"] + WEIGHTS + ["loss_target"] + ["m_" + n for n in WEIGHTS] + ["v_" + n for n in WEIGHTS]
BIG = {"w_in": 2, "mla_w_qb": 2, "mla_w_kvb": 2, "w_branch": 3, "w_out": 1, "w_ffn_in": 2, "w_ffn_out": 1}
SMALL_SHARDED = {"b_gate": 2, "gla_w_gk2": 3, "gla_b_gk": 2, "w_dw": 2}
SMALL = ["c_ctx", "b_ada", "norm1_w", "norm2_w", "b_gate", "mla_q_norm_a", "mla_kv_norm_a", "mla_q_norm", "mla_k_norm",
         "gla_w_gk2", "gla_b_gk", "gla_o_norm", "ret_decay", "w_dw", "b_dw"]


def _entry_rows(size, align):
    return -(-size // (PACK_W * align)) * align


def _pack(arrays, rows, dtype, align, lead=0):
    parts = []
    for a in arrays:
        head = a.shape[:lead]
        size = math.prod(a.shape[lead:])
        r = _entry_rows(size, align)
        if r * PACK_W == size:
            parts.append(a.astype(dtype).reshape(head + (r, PACK_W)))
        else:
            flat = jnp.pad(a.astype(dtype).reshape(head + (size,)), [(0, 0)] * lead + [(0, r * PACK_W - size)])
            parts.append(flat.reshape(head + (r, PACK_W)))
    used = sum(p.shape[lead] for p in parts)
    if rows > used:
        parts.append(jnp.zeros(parts[0].shape[:lead] + (rows - used, PACK_W), dtype))
    return jnp.concatenate(parts, axis=lead)


def _pack_rows(shapes, align, multiple):
    used = sum(_entry_rows(math.prod(s), align) for s in shapes)
    return -(-used // multiple) * multiple


def _unpack(pack, shapes, align):
    head = pack.shape[:-2]
    out, off = [], 0
    for shape in shapes:
        size = math.prod(shape)
        r = _entry_rows(size, align)
        block = lax.slice_in_dim(pack, off, off + r, axis=len(head))
        if r * PACK_W != size:
            block = block.reshape(head + (r * PACK_W,))[..., :size]
        out.append(block.reshape(head + tuple(shape)))
        off += r
    return out


def _join_shards(stacked, axis):
    moved = jnp.moveaxis(stacked, 0, axis)
    shape = list(moved.shape)
    return moved.reshape(shape[:axis] + [shape[axis] * shape[axis + 1]] + shape[axis + 2:])


def _split_shards(full, axis):
    shape = list(full.shape)
    split = full.reshape(shape[:axis] + [N_DEV, shape[axis] // N_DEV] + shape[axis + 1:])
    return jnp.moveaxis(split, axis, 0)


def _gather_shards(local, axes, dtype, rows_multiple, name):
    names = list(axes)
    shapes = [local[n].shape for n in names]
    rows = _pack_rows(shapes, rows_multiple, rows_multiple)
    gathered = all_gather(_pack([local[n] for n in names], rows, dtype, rows_multiple), name)
    stacked = _unpack(gathered.reshape(N_DEV, rows, PACK_W), shapes, rows_multiple)
    return {n: _join_shards(s, axes[n]).astype(F32) for n, s in zip(names, stacked)}


def kernel(*args):
    a = dict(zip(INPUTS, args))
    me = 4 * lax.axis_index("x") + 2 * lax.axis_index("y") + lax.axis_index("c")
    cols = a["w_ada"].shape[2]

    c_all = all_gather(jnp.pad(a["c"], ((0, 7), (0, 0))), "gather_c").reshape(N_DEV, 8, D)[:, 0]
    cond_in = jnp.concatenate([c_all, a["c_ctx"][None], jnp.zeros((ADA_ROWS - N_DEV - 1, D), F32)], axis=0)
    b_loc = lax.dynamic_slice_in_dim(a["b_ada"], me * cols, cols, axis=1)[:, None, :]
    mod_loc = ada_forward(cond_in, a["w_ada"], b_loc)
    mod_all = all_gather(mod_loc.reshape(DEPTH * ADA_ROWS, cols), "gather_mod")
    mod_all = mod_all.reshape(N_DEV, DEPTH, ADA_ROWS, cols).transpose(1, 2, 0, 3).reshape(DEPTH, ADA_ROWS, 6, D)
    mod_me = lax.dynamic_index_in_dim(mod_all, me, axis=1, keepdims=False)
    mod = jnp.pad(jnp.stack([mod_all[:, N_DEV], mod_me], axis=1), ((0, 0), (0, 0), (0, 2), (0, 0)))

    w = _gather_shards(a, BIG, BF16, 16, "gather_weights")
    w.update(_gather_shards(a, SMALL_SHARDED, F32, 8, "gather_small"))
    for n in SMALL:
        if n not in SMALL_SHARDED and n not in ("c_ctx", "b_ada"):
            w[n] = a[n]

    loss, (gw, gmod, gx) = jax.value_and_grad(local_loss, argnums=(0, 1, 2))(
        w, mod, a["x"][0], a["ctx"][0], a["loss_target"][0])
    loss = lax.psum(loss, ("x", "y", "c"))

    dmod_own = gmod[:, :, :6].reshape(2 * DEPTH, 6 * D)
    g_all = all_gather(jnp.pad(dmod_own, ((0, 8 - 2 * DEPTH), (0, 0))), "gather_dmod").reshape(N_DEV, 8, 6 * D)
    g_loc = lax.dynamic_slice_in_dim(g_all[:, :2 * DEPTH], me * cols, cols, axis=2).transpose(1, 0, 2)
    g_w_ada, g_c_ctx, g_b_ada = ada_backward(cond_in, g_loc, dmod_own, a["w_ada"])

    small_part = dict(gw, c_ctx=g_c_ctx, b_ada=g_b_ada)
    small_shapes = [a[n].shape if n not in SMALL_SHARDED else gw[n].shape for n in SMALL]
    rows = _pack_rows(small_shapes, 8, PACK_TILE)
    parts = all_gather(_pack([small_part[n] for n in SMALL], rows, F32, 8), "gather_small_grads")
    small_sum = _unpack(slab_sum(parts.reshape(N_DEV, rows, PACK_W), "sum_small_grads"), small_shapes, 8)
    g_small = {}
    for n, g in zip(SMALL, small_sum):
        if n in SMALL_SHARDED:
            ax = SMALL_SHARDED[n]
            g = lax.dynamic_slice_in_dim(g, me * a[n].shape[ax], a[n].shape[ax], axis=ax)
        g_small[n] = g

    big_rows = _pack_rows([a[n].shape for n in BIG], 16, PACK_TILE)
    slabs = _pack([_split_shards(gw[n], ax) for n, ax in BIG.items()], big_rows, BF16, 16, lead=1)
    landed = all_to_all(slabs, "scatter_grads")

    def update(names, g_slabs, rows, align, label):
        shapes = [a[n].shape for n in names]
        packs = [_pack([a[pre + n] for n in names], rows, F32, align) for pre in ("", "m_", "v_")]
        outs = adamw(g_slabs, *packs, label)
        return [dict(zip(names, _unpack(o, shapes, align))) for o in outs]

    res_big = update(list(BIG), landed, big_rows, 16, "adamw_big")
    ada_rows = _pack_rows([a["w_ada"].shape], 8, PACK_TILE)
    res_ada = update(["w_ada"], _pack([g_w_ada], ada_rows, F32, 8)[None], ada_rows, 8, "adamw_ada")
    small_rows = _pack_rows([a[n].shape for n in SMALL], 8, PACK_TILE)
    res_small = update(SMALL, _pack([g_small[n] for n in SMALL], small_rows, F32, 8)[None], small_rows, 8,
                       "adamw_small")

    outs = [loss, gx[None]]
    for k in range(4):
        merged = {**res_big[k], **res_ada[k], **res_small[k]}
        outs += [merged[n] for n in WEIGHTS]
    return tuple(outs)
```

```python
import functools
import math

import jax
import jax.numpy as jnp
import numpy as np
from jax import lax
from jax.experimental import pallas as pl
from jax.experimental.pallas import tpu as pltpu

F32 = jnp.float32
BF16 = jnp.bfloat16

N_DEV = 8
D = 1024
DEPTH = 2
GRID_W = 64
MLA_HEADS = 8
MLA_NOPE = 64
MLA_ROPE = 32
MLA_QK = 96
MLA_V = 64
MLA_Q_LORA = 256
MLA_KV_LORA = 128
GLA_HEADS = 4
GLA_DK = 128
GLA_RANK = 16
GLA_NORMALIZER = 16.0
RET_HEADS = 4
RET_DK = 128
BRANCH_W = 512
D_FF = 2816
CHUNK = 64
ROPE_THETA = 10000.0
RET_THETA = 10000.0
EPS = 1e-6
HEAD_PAD = 128
N_IN_PAD = 8192

ADAM_LR = 0.001
ADAM_B1 = 0.9
ADAM_B2 = 0.999
ADAM_EPS = 1e-08
ADAM_WD = 0.01
ADAM_STEP = 10

ROW_TILE = 256
SCAN_CHUNKS = ROW_TILE // CHUNK
VMEM_LIMIT_BYTES = 56 * 1024 * 1024
MESH = pl.DeviceIdType.MESH


def _cparams(n_axes):
    return pltpu.CompilerParams(dimension_semantics=("arbitrary",) * n_axes, vmem_limit_bytes=VMEM_LIMIT_BYTES)


def _pick(dim, cands):
    for cand in cands:
        if dim % cand == 0:
            return cand
    return dim


_DOT_DIMS = {"nn": (((1,), (0,)), ((), ())), "nt": (((1,), (1,)), ((), ())), "tn": (((0,), (0,)), ((), ()))}


def _dg(a, b, mode):
    return lax.dot_general(a.astype(BF16), b.astype(BF16), _DOT_DIMS[mode], preferred_element_type=F32)


def _bdot(a, b, mode):
    @jax.custom_vjp
    def f(a, b):
        return _dg(a, b, mode)

    def fwd(a, b):
        return _dg(a, b, mode), (a, b)

    def bwd(res, g):
        a, b = res
        if mode == "nn":
            return _dg(g, b, "nt").astype(a.dtype), _dg(a, g, "tn").astype(b.dtype)
        if mode == "nt":
            return _dg(g, b, "nn").astype(a.dtype), _dg(g, a, "tn").astype(b.dtype)
        return _dg(b, g, "nt").astype(a.dtype), _dg(a, g, "nn").astype(b.dtype)

    f.defvjp(fwd, bwd)
    return f(a, b)


def _roll(x, shift, axis):
    n = x.shape[axis]
    shift = shift % n

    @jax.custom_vjp
    def f(x):
        return pltpu.roll(x, shift, axis)

    def fwd(x):
        return pltpu.roll(x, shift, axis), None

    def bwd(_, g):
        return (pltpu.roll(g, (n - shift) % n, axis),)

    f.defvjp(fwd, bwd)
    return f(x)


@jax.custom_jvp
def _log_sigmoid(x):
    return jnp.minimum(x, 0.0) - jnp.log(1.0 + jnp.exp(-jnp.abs(x)))


@_log_sigmoid.defjvp
def _log_sigmoid_jvp(primals, tangents):
    (x,), (t,) = primals, tangents
    return _log_sigmoid(x), t * jax.nn.sigmoid(-x)


def _rms(x, n, w=None):
    y = x * lax.rsqrt(jnp.sum(x * x, axis=-1, keepdims=True) * (1.0 / n) + EPS)
    return y if w is None else y * w


def _silu(x):
    return x * jax.nn.sigmoid(x)


def _gelu_tanh(x):
    return 0.5 * x * (1.0 + jnp.tanh(math.sqrt(2.0 / math.pi) * (x + 0.044715 * (x * x * x))))


def _mm(a, b, mode, name):
    if mode == "nn":
        (m, k), (_, n) = a.shape, b.shape
    elif mode == "nt":
        (m, k), (n, _) = a.shape, b.shape
    else:
        (k, m), (_, n) = a.shape, b.shape
    tm = _pick(m, (1024, 768, 1408, 512, 256, 128))
    tn = _pick(n, (1024, 1408, 512, 256, 128))
    tk = _pick(k, (1024, 768, 1408, 512, 256, 128))
    nk = k // tk
    if mode == "nn":
        a_spec = pl.BlockSpec((tm, tk), lambda i, j, kk: (i, kk))
        b_spec = pl.BlockSpec((tk, tn), lambda i, j, kk: (kk, j))
    elif mode == "nt":
        a_spec = pl.BlockSpec((tm, tk), lambda i, j, kk: (i, kk))
        b_spec = pl.BlockSpec((tn, tk), lambda i, j, kk: (j, kk))
    else:
        a_spec = pl.BlockSpec((tk, tm), lambda i, j, kk: (kk, i))
        b_spec = pl.BlockSpec((tk, tn), lambda i, j, kk: (kk, j))

    def body(a_ref, b_ref, o_ref, acc_ref):
        kk = pl.program_id(2)

        @pl.when(kk == 0)
        def _():
            acc_ref[...] = jnp.zeros_like(acc_ref)

        acc_ref[...] += _dg(a_ref[...], b_ref[...], mode)

        @pl.when(kk == nk - 1)
        def _():
            o_ref[...] = acc_ref[...]

    return pl.pallas_call(
        body, name=name, grid=(m // tm, n // tn, nk),
        in_specs=[a_spec, b_spec], out_specs=pl.BlockSpec((tm, tn), lambda i, j, kk: (i, j)),
        out_shape=jax.ShapeDtypeStruct((m, n), F32),
        scratch_shapes=[pltpu.VMEM((tm, tn), F32)],
        compiler_params=_cparams(3),
    )(a, b)


def linear(x, w, name):
    @jax.custom_vjp
    def op(x, w):
        return _mm(x, w.astype(BF16), "nn", name + "_f")

    def fwd(x, w):
        wb = w.astype(BF16)
        return _mm(x, wb, "nn", name + "_f"), (x, wb)

    def bwd(res, g):
        x, wb = res
        return _mm(g, wb, "nt", name + "_dx"), _mm(x, g, "tn", name + "_dw")

    op.defvjp(fwd, bwd)
    return op(x, w)


PIECE_W = 512
PIECE_ROWS = 384


def _mm_split(a, wb, name):
    (r, k), n = a.shape, wb.shape[1] // PIECE_W
    tm = _pick(r, (PIECE_ROWS, ROW_TILE))

    def body(a_ref, w_ref, *out_refs):
        j = pl.program_id(1)
        res = _dg(a_ref[...], w_ref[...], "nn")
        for jj, o_ref in enumerate(out_refs):
            @pl.when(j == jj)
            def _(o_ref=o_ref):
                o_ref[...] = res

    return pl.pallas_call(
        body, name=name, grid=(r // tm, n),
        in_specs=[pl.BlockSpec((tm, k), lambda i, j: (i, 0)), pl.BlockSpec((k, PIECE_W), lambda i, j: (0, j))],
        out_specs=[pl.BlockSpec((tm, PIECE_W), lambda i, j: (i, 0))] * n,
        out_shape=[jax.ShapeDtypeStruct((r, PIECE_W), F32)] * n, compiler_params=_cparams(2),
    )(a, wb)


def _mm_join(gs, wb, name):
    n, (r, _), k = len(gs), gs[0].shape, wb.shape[0]
    tm = _pick(r, (PIECE_ROWS, ROW_TILE))

    def body(*refs):
        g_refs, w_ref, o_ref, acc_ref = refs[:n], refs[n], refs[n + 1], refs[n + 2]
        j = pl.program_id(1)

        @pl.when(j == 0)
        def _():
            acc_ref[...] = jnp.zeros_like(acc_ref)

        for jj, g_ref in enumerate(g_refs):
            @pl.when(j == jj)
            def _(g_ref=g_ref):
                acc_ref[...] += _dg(g_ref[...], w_ref[...], "nt")

        @pl.when(j == n - 1)
        def _():
            o_ref[...] = acc_ref[...]

    return pl.pallas_call(
        body, name=name, grid=(r // tm, n),
        in_specs=[pl.BlockSpec((tm, PIECE_W), lambda i, j: (i, 0))] * n
        + [pl.BlockSpec((k, PIECE_W), lambda i, j: (0, j))],
        out_specs=pl.BlockSpec((tm, k), lambda i, j: (i, 0)), out_shape=jax.ShapeDtypeStruct((r, k), F32),
        scratch_shapes=[pltpu.VMEM((tm, k), F32)], compiler_params=_cparams(2),
    )(*gs, wb)


PIECE_GROUP = 4


def _mm_join_tn(a, gs, name):
    n, (r, k) = len(gs), a.shape
    tk = ROW_TILE
    nk = r // tk
    width = PIECE_GROUP * PIECE_W

    def body(*refs):
        a_ref, g_refs, o_ref, acc_ref = refs[0], refs[1:n + 1], refs[n + 1], refs[n + 2]
        j, kk = pl.program_id(0), pl.program_id(1)

        @pl.when(kk == 0)
        def _():
            acc_ref[...] = jnp.zeros_like(acc_ref)

        av = a_ref[...]
        for group in range(n // PIECE_GROUP):
            @pl.when(j == group)
            def _(group=group):
                for p in range(PIECE_GROUP):
                    cols = slice(p * PIECE_W, (p + 1) * PIECE_W)
                    acc_ref[:, cols] += _dg(av, g_refs[group * PIECE_GROUP + p][...], "tn")

        @pl.when(kk == nk - 1)
        def _():
            o_ref[...] = acc_ref[...]

    def g_spec(jj):
        return pl.BlockSpec((tk, PIECE_W), lambda j, kk: (jnp.where(j == jj // PIECE_GROUP, kk, 0), 0))

    return pl.pallas_call(
        body, name=name, grid=(n // PIECE_GROUP, nk),
        in_specs=[pl.BlockSpec((tk, k), lambda j, kk: (kk, 0))] + [g_spec(jj) for jj in range(n)],
        out_specs=pl.BlockSpec((k, width), lambda j, kk: (0, j)),
        out_shape=jax.ShapeDtypeStruct((k, n * PIECE_W), F32),
        scratch_shapes=[pltpu.VMEM((k, width), F32)], compiler_params=_cparams(2),
    )(a, *gs)


def linear_pieces(x, w, name):
    @jax.custom_vjp
    def op(x, w):
        return tuple(_mm_split(x, w.astype(BF16), name + "_f"))

    def fwd(x, w):
        wb = w.astype(BF16)
        return tuple(_mm_split(x, wb, name + "_f")), (x, wb)

    def bwd(res, gs):
        x, wb = res
        return _mm_join(list(gs), wb, name + "_dx"), _mm_join_tn(x, list(gs), name + "_dw")

    op.defvjp(fwd, bwd)
    return op(x, w)


def rowwise(name, fn, rows, segs, params, out_widths, nct, diff_rows=None):
    n_row, n_seg, n_par, n_out = len(rows), len(segs), len(params), len(out_widths)
    diff_rows = [True] * n_row if diff_rows is None else list(diff_rows)
    r_total = rows[0].shape[0]
    n_tiles = r_total // ROW_TILE

    def seg_of(i):
        return jnp.where(i < nct, 0, 1)

    def row_spec(width):
        return pl.BlockSpec((ROW_TILE, width), lambda i: (i, 0))

    def seg_spec(shape):
        nd = len(shape)
        return pl.BlockSpec((1,) + tuple(shape[1:]), lambda i: (seg_of(i),) + (0,) * (nd - 1))

    def par_spec(shape):
        nd = len(shape)
        return pl.BlockSpec(tuple(shape), lambda i: (0,) * nd)

    in_specs = ([row_spec(r.shape[1]) for r in rows] + [seg_spec(s.shape) for s in segs]
                + [par_spec(p.shape) for p in params])

    def load(refs):
        vals = [r[...].astype(F32) for r in refs[:n_row]]
        vals += [r[0].astype(F32) for r in refs[n_row:n_row + n_seg]]
        vals += [r[...].astype(F32) for r in refs[n_row + n_seg:n_row + n_seg + n_par]]
        return vals

    def fwd_call(arrs):
        def body(*refs):
            outs = fn(*load(refs))
            for o_ref, val in zip(refs[n_row + n_seg + n_par:], outs):
                o_ref[...] = val

        return pl.pallas_call(
            body, name=name + "_f", grid=(n_tiles,), in_specs=in_specs,
            out_specs=[row_spec(w) for w in out_widths],
            out_shape=[jax.ShapeDtypeStruct((r_total, w), F32) for w in out_widths],
            compiler_params=_cparams(1),
        )(*arrs)

    d_idx = [k for k in range(n_row) if diff_rows[k]]

    def bwd_call(arrs, douts):
        n_in = n_row + n_seg + n_par

        def body(*refs):
            i = pl.program_id(0)
            vals = load(refs[:n_in])
            gs = [r[...] for r in refs[n_in:n_in + n_out]]
            out_refs = refs[n_in + n_out:]
            diff_pos = d_idx + list(range(n_row, n_in))

            def f(*dv):
                full = list(vals)
                for pos, v in zip(diff_pos, dv):
                    full[pos] = v
                return tuple(fn(*full))

            _, vjp = jax.vjp(f, *[vals[p] for p in diff_pos])
            grads = vjp(tuple(gs))
            nd = len(d_idx)
            for o_ref, g in zip(out_refs[:nd], grads[:nd]):
                o_ref[...] = g
            first_seg = jnp.logical_or(i == 0, i == nct)
            for o_ref, g in zip(out_refs[nd:nd + n_seg], grads[nd:nd + n_seg]):
                @pl.when(first_seg)
                def _(o_ref=o_ref, g=g):
                    o_ref[0] = g

                @pl.when(jnp.logical_not(first_seg))
                def _(o_ref=o_ref, g=g):
                    o_ref[0] += g
            for o_ref, g in zip(out_refs[nd + n_seg:], grads[nd + n_seg:]):
                @pl.when(i == 0)
                def _(o_ref=o_ref, g=g):
                    o_ref[...] = g

                @pl.when(i != 0)
                def _(o_ref=o_ref, g=g):
                    o_ref[...] += g

        out_specs = ([row_spec(rows[k].shape[1]) for k in d_idx] + [seg_spec(s.shape) for s in segs]
                     + [par_spec(p.shape) for p in params])
        out_shape = ([jax.ShapeDtypeStruct(rows[k].shape, F32) for k in d_idx]
                     + [jax.ShapeDtypeStruct(s.shape, F32) for s in segs]
                     + [jax.ShapeDtypeStruct(p.shape, F32) for p in params])
        return pl.pallas_call(
            body, name=name + "_b", grid=(n_tiles,),
            in_specs=in_specs + [row_spec(w) for w in out_widths],
            out_specs=out_specs, out_shape=out_shape, compiler_params=_cparams(1),
        )(*arrs, *douts)

    @jax.custom_vjp
    def op(*arrs):
        return tuple(fwd_call(arrs))

    def op_fwd(*arrs):
        return tuple(fwd_call(arrs)), arrs

    def op_bwd(arrs, douts):
        grads = list(bwd_call(arrs, douts))
        nd = len(d_idx)
        row_grads = [jnp.zeros_like(arrs[k]) for k in range(n_row)]
        for k, g in zip(d_idx, grads[:nd]):
            row_grads[k] = g
        return tuple(row_grads + grads[nd:])

    op.defvjp(op_fwd, op_bwd)
    return op(*rows, *segs, *params)


ATT_SCALE = MLA_QK ** -0.5
LOG2E = math.log2(math.e)
ATT_KEY_CHUNKS = (768, 512, 256)


ATT_LATENT_TILE = 1024


def _query_rows_spec(row0, tq):
    return pl.BlockSpec((pl.Element(tq), pl.Element(HEAD_PAD)),
                        lambda h, i: (pl.multiple_of(row0 + i * tq, ROW_TILE), pl.multiple_of(h * HEAD_PAD, HEAD_PAD)))


def _key_chunks(nk):
    kc = _pick(nk, ATT_KEY_CHUNKS)
    return [(c * kc, kc) for c in range(nk // kc)]


def _attn_fwd_call(q, k, v, row0, n_rows, tq, nk, name):
    def body(q_ref, k_ref, v_ref, o_ref, lse_ref):
        qv = q_ref[...]
        m = jnp.full((tq, 1), -jnp.inf, F32)
        l = jnp.zeros((tq, 1), F32)
        acc = jnp.zeros((tq, HEAD_PAD), F32)
        for start, size in _key_chunks(nk):
            s = lax.dot_general(qv, k_ref[start:start + size, :], _DOT_DIMS["nt"], preferred_element_type=F32)
            m_new = jnp.maximum(m, jnp.max(s, axis=-1, keepdims=True))
            alpha = jnp.exp2(m - m_new)
            p = jnp.exp2(s - m_new)
            l = alpha * l + jnp.sum(p, axis=-1, keepdims=True)
            acc = alpha * acc + lax.dot_general(p.astype(BF16), v_ref[start:start + size, :], _DOT_DIMS["nn"],
                                                preferred_element_type=F32)
            m = m_new
        o_ref[...] = acc / l
        lse_ref[...] = jnp.broadcast_to(m + jnp.log2(l), (tq, HEAD_PAD))

    out_spec = pl.BlockSpec((tq, HEAD_PAD), lambda h, i: (i, h))
    kv_spec = pl.BlockSpec((nk, HEAD_PAD), lambda h, i: (0, h))
    out = jax.ShapeDtypeStruct((n_rows, q.shape[1]), F32)
    return pl.pallas_call(
        body, name=name, grid=(MLA_HEADS, n_rows // tq), in_specs=[_query_rows_spec(row0, tq), kv_spec, kv_spec],
        out_specs=[out_spec, out_spec], out_shape=[out, out], compiler_params=_cparams(2),
    )(q, k, v)


def _attn_bwd_call(q, k, v, o, lse, do, row0, n_rows, tq, nk, name):
    nq = n_rows // tq

    def body(q_ref, k_ref, v_ref, o_ref, lse_ref, do_ref, dq_ref, dk_ref, dv_ref):
        i = pl.program_id(1)

        @pl.when(i == 0)
        def _():
            dk_ref[...] = jnp.zeros_like(dk_ref)
            dv_ref[...] = jnp.zeros_like(dv_ref)

        qv = q_ref[...]
        dov = do_ref[...]
        dob = dov.astype(BF16)
        lse = lse_ref[:, 0:1]
        delta = jnp.sum(dov * o_ref[...], axis=-1, keepdims=True)
        dq = jnp.zeros((tq, HEAD_PAD), F32)
        for start, size in _key_chunks(nk):
            kk = k_ref[start:start + size, :]
            vv = v_ref[start:start + size, :]
            s = lax.dot_general(qv, kk, _DOT_DIMS["nt"], preferred_element_type=F32)
            p = jnp.exp2(s - lse)
            dp = lax.dot_general(dob, vv, _DOT_DIMS["nt"], preferred_element_type=F32)
            g = (p * (dp - delta)).astype(BF16)
            dk_ref[start:start + size, :] += lax.dot_general(g, qv, _DOT_DIMS["tn"], preferred_element_type=F32)
            dv_ref[start:start + size, :] += lax.dot_general(p.astype(BF16), dob, _DOT_DIMS["tn"],
                                                             preferred_element_type=F32)
            dq = dq + lax.dot_general(g, kk, _DOT_DIMS["nn"], preferred_element_type=F32)
        dq_ref[...] = dq * ATT_SCALE

        @pl.when(i == nq - 1)
        def _():
            dk_ref[...] = dk_ref[...] * (1.0 / LOG2E)

    own_spec = pl.BlockSpec((tq, HEAD_PAD), lambda h, i: (i, h))
    kv_spec = pl.BlockSpec((nk, HEAD_PAD), lambda h, i: (0, h))
    rows_spec = _query_rows_spec(row0, tq)
    return pl.pallas_call(
        body, name=name, grid=(MLA_HEADS, nq),
        in_specs=[rows_spec, kv_spec, kv_spec, own_spec, own_spec, rows_spec],
        out_specs=[own_spec, kv_spec, kv_spec],
        out_shape=[jax.ShapeDtypeStruct((n_rows, q.shape[1]), F32), jax.ShapeDtypeStruct((nk, q.shape[1]), F32),
                   jax.ShapeDtypeStruct((nk, q.shape[1]), F32)],
        compiler_params=_cparams(2),
    )(q, k, v, o, lse, do)


def attention(q, k, v, tc, name):
    r_total = q.shape[0]
    tq_lat = _pick(r_total - tc, (ATT_LATENT_TILE, ROW_TILE))
    ranges = [(0, tc, ROW_TILE, tc, "_ctx"), (tc, r_total - tc, tq_lat, r_total, "_lat")]

    def operands(q, k, v):
        return (q * (ATT_SCALE * LOG2E)).astype(BF16), k.astype(BF16), v.astype(BF16)

    def forward(qb, kb, vb):
        return [_attn_fwd_call(qb, kb, vb, row0, n_rows, tq, nk, name + tag + "_f")
                for row0, n_rows, tq, nk, tag in ranges]

    @jax.custom_vjp
    def op(q, k, v):
        return jnp.concatenate([o for o, _ in forward(*operands(q, k, v))], axis=0)

    def fwd(q, k, v):
        qb, kb, vb = operands(q, k, v)
        parts = forward(qb, kb, vb)
        return jnp.concatenate([o for o, _ in parts], axis=0), (qb, kb, vb, parts)

    def bwd(res, do):
        qb, kb, vb, parts = res
        (dq_c, dk_c, dv_c), (dq_l, dk_l, dv_l) = [
            _attn_bwd_call(qb, kb, vb, o, lse, do, row0, n_rows, tq, nk, name + tag + "_b")
            for (o, lse), (row0, n_rows, tq, nk, tag) in zip(parts, ranges)]
        grow = lambda part: jnp.pad(part, ((0, r_total - tc), (0, 0)))
        return jnp.concatenate([dq_c, dq_l], axis=0), dk_l + grow(dk_c), dv_l + grow(dv_c)

    op.defvjp(fwd, bwd)
    return op(q, k, v)


CHUNK_SHIFT = CHUNK.bit_length() - 1


def _block_pairs():
    rows = lax.broadcasted_iota(jnp.int32, (ROW_TILE, ROW_TILE), 0)
    cols = lax.broadcasted_iota(jnp.int32, (ROW_TILE, ROW_TILE), 1)
    same = lax.shift_right_logical(rows, CHUNK_SHIFT) == lax.shift_right_logical(cols, CHUNK_SHIFT)
    return rows, cols, same


def _block_mask(kind):
    rows, cols, same = _block_pairs()
    order = {"lower_incl": rows >= cols, "upper_incl": rows <= cols, "lower_strict": rows > cols,
             "upper_strict": rows < cols}[kind]
    return jnp.logical_and(same, order)


def _row_chunk():
    return lax.shift_right_logical(lax.broadcasted_iota(jnp.int32, (ROW_TILE, 1), 0), CHUNK_SHIFT)


def _dot01(kind, x):
    m = _block_mask(kind).astype(BF16)
    hi = x.astype(BF16)
    rest = x - hi.astype(F32)
    mid = rest.astype(BF16)
    lo = (rest - mid.astype(F32)).astype(BF16)
    terms = jnp.concatenate([hi, mid, lo], axis=1)
    out = lax.dot_general(m, terms, _DOT_DIMS["nn"], preferred_element_type=F32)
    n = x.shape[1]
    return out[:, :n] + out[:, n:2 * n] + out[:, 2 * n:]


def _chunk_sums(x, forward):
    kinds = ("lower_incl", "upper_strict") if forward else ("upper_incl", "lower_strict")
    transposed = ("upper_incl", "lower_strict") if forward else ("lower_incl", "upper_strict")

    @jax.custom_vjp
    def f(x):
        return _dot01(kinds[0], x), _dot01(kinds[1], x)

    def fwd(x):
        return (_dot01(kinds[0], x), _dot01(kinds[1], x)), None

    def bwd(_, g):
        return (_dot01(transposed[0], g[0]) + _dot01(transposed[1], g[1]),)

    f.defvjp(fwd, bwd)
    return f(x)


def _scan_order(forward):
    if forward:
        return list(range(SCAN_CHUNKS)), lambda c: c * CHUNK + CHUNK - 1
    return list(range(SCAN_CHUNKS - 1, -1, -1)), lambda c: c * CHUNK


def _carry_states(forward, st0, inc_all, decay_of):
    order, _ = _scan_order(forward)
    entering = [None] * SCAN_CHUNKS
    st = st0
    for c in order:
        entering[c] = st
        st = st * decay_of(c) + inc_all[:, c * HEAD_PAD:(c + 1) * HEAD_PAD]
    return jnp.concatenate(entering, axis=0), st


def _per_chunk_lanes(x):
    chunk = _row_chunk()
    return jnp.concatenate([jnp.where(chunk == c, x, 0.0) for c in range(SCAN_CHUNKS)], axis=1)


def _own_chunk_lanes(x4):
    chunk = _row_chunk()
    n = x4.shape[1] // SCAN_CHUNKS
    out = jnp.where(chunk == 0, x4[:, :n], 0.0)
    for c in range(1, SCAN_CHUNKS):
        out = out + jnp.where(chunk == c, x4[:, c * n:(c + 1) * n], 0.0)
    return out


def _gla_block(forward, q, k, v, la, st0):
    cum, after = _chunk_sums(la, forward)
    _, last_row = _scan_order(forward)
    q_dec = q * (jnp.exp(cum) * (GLA_DK ** -0.5))
    att = _bdot(q_dec, k * jnp.exp(-cum), "nt")
    att = jnp.where(_block_mask("lower_incl" if forward else "upper_strict"), att, 0.0)
    inc_all = _bdot(v, _per_chunk_lanes(k * jnp.exp(after)), "tn")
    entering, st1 = _carry_states(forward, st0, inc_all,
                                  lambda c: jnp.exp(cum[last_row(c):last_row(c) + 1, :]))
    o = _bdot(att, v, "nn") + _own_chunk_lanes(_bdot(q_dec, entering, "nt"))
    return o, st1


def _ret_block(forward, q, k, v, rd, st0):
    lg = -jnp.exp(rd[0:1, 0:1])
    rows, cols, _ = _block_pairs()
    pos = jnp.bitwise_and(lax.broadcasted_iota(jnp.int32, (ROW_TILE, 1), 0), CHUNK - 1).astype(F32)
    if forward:
        to_end, from_start, rel = CHUNK - 1.0 - pos, pos + 1.0, (rows - cols).astype(F32)
    else:
        to_end, from_start, rel = pos, CHUNK - pos, (cols - rows).astype(F32)
    mask = _block_mask("lower_incl" if forward else "upper_strict")
    dmat = jnp.where(mask, jnp.exp(jnp.where(mask, rel, 0.0) * lg), 0.0)
    att = _bdot(q, k, "nt") * dmat
    inc_all = _bdot(v, _per_chunk_lanes(k * jnp.exp(to_end * lg)), "tn")
    entering, st1 = _carry_states(forward, st0, inc_all, lambda c: jnp.exp(CHUNK * lg))
    o = _bdot(att, v, "nn") + _own_chunk_lanes(_bdot(q, entering, "nt")) * jnp.exp(from_start * lg)
    return o, st1


def scan(kind, forward, q, k, v, aux, tc, name):
    heads = q.shape[1] // HEAD_PAD
    r_total = q.shape[0]
    nblk = r_total // ROW_TILE
    nctb = tc // ROW_TILE
    block_fn = functools.partial(_gla_block if kind == "gla" else _ret_block, forward)
    per_row_aux = kind == "gla"

    def blk(g):
        if forward:
            return g
        return jnp.where(g < nctb, nctb - 1 - g, nblk - 1 - (g - nctb))

    def specs(step_to_g):
        row = pl.BlockSpec((ROW_TILE, heads * HEAD_PAD), lambda s: (blk(step_to_g(s)), 0))
        aux_spec = row if per_row_aux else pl.BlockSpec((heads, 8, HEAD_PAD), lambda s: (0, 0, 0))
        st = pl.BlockSpec((1, heads, HEAD_PAD, HEAD_PAD), lambda s: (step_to_g(s), 0, 0, 0))
        return row, aux_spec, st

    def head_cols(h):
        return slice(h * HEAD_PAD, (h + 1) * HEAD_PAD)

    def fwd_call(q, k, v, aux):
        row, aux_spec, st_spec = specs(lambda s: s)

        def body(q_ref, k_ref, v_ref, a_ref, o_ref, st0_ref, st_ref):
            @pl.when(pl.program_id(0) == 0)
            def _():
                st_ref[...] = jnp.zeros_like(st_ref)

            qv, kv, vv = q_ref[...], k_ref[...], v_ref[...]
            outs = []
            for h in range(heads):
                st0 = st_ref[h]
                st0_ref[0, h] = st0
                a = a_ref[:, head_cols(h)] if per_row_aux else a_ref[h]
                o, st1 = block_fn(qv[:, head_cols(h)], kv[:, head_cols(h)], vv[:, head_cols(h)], a, st0)
                outs.append(o)
                st_ref[h] = st1
            o_ref[...] = jnp.concatenate(outs, axis=1)

        return pl.pallas_call(
            body, name=name + "_f", grid=(nblk,), in_specs=[row, row, row, aux_spec],
            out_specs=[row, st_spec],
            out_shape=[jax.ShapeDtypeStruct(q.shape, F32),
                       jax.ShapeDtypeStruct((nblk, heads, HEAD_PAD, HEAD_PAD), F32)],
            scratch_shapes=[pltpu.VMEM((heads, HEAD_PAD, HEAD_PAD), F32)],
            compiler_params=_cparams(1),
        )(q, k, v, aux)

    def bwd_call(q, k, v, aux, st0s, do):
        row, aux_spec, st_spec = specs(lambda s: nblk - 1 - s)

        def body(q_ref, k_ref, v_ref, a_ref, st0_ref, do_ref, dq_ref, dk_ref, dv_ref, da_ref, dst_ref):
            s = pl.program_id(0)

            @pl.when(s == 0)
            def _():
                dst_ref[...] = jnp.zeros_like(dst_ref)

            qv, kv, vv, dov = q_ref[...], k_ref[...], v_ref[...], do_ref[...]
            grads = []
            for h in range(heads):
                a = a_ref[:, head_cols(h)] if per_row_aux else a_ref[h]
                _, vjp = jax.vjp(block_fn, qv[:, head_cols(h)], kv[:, head_cols(h)], vv[:, head_cols(h)], a,
                                 st0_ref[0, h])
                dq, dk, dv, da, dst0 = vjp((dov[:, head_cols(h)], dst_ref[h]))
                dst_ref[h] = dst0
                grads.append((dq, dk, dv, da))
            dq_ref[...] = jnp.concatenate([g[0] for g in grads], axis=1)
            dk_ref[...] = jnp.concatenate([g[1] for g in grads], axis=1)
            dv_ref[...] = jnp.concatenate([g[2] for g in grads], axis=1)
            if per_row_aux:
                da_ref[...] = jnp.concatenate([g[3] for g in grads], axis=1)
            else:
                da = jnp.stack([g[3] for g in grads], axis=0)

                @pl.when(s == 0)
                def _():
                    da_ref[...] = da

                @pl.when(s != 0)
                def _():
                    da_ref[...] += da

        return pl.pallas_call(
            body, name=name + "_b", grid=(nblk,),
            in_specs=[row, row, row, aux_spec, st_spec, row],
            out_specs=[row, row, row, aux_spec],
            out_shape=[jax.ShapeDtypeStruct(q.shape, F32)] * 3 + [jax.ShapeDtypeStruct(aux.shape, F32)],
            scratch_shapes=[pltpu.VMEM((heads, HEAD_PAD, HEAD_PAD), F32)],
            compiler_params=_cparams(1),
        )(q, k, v, aux, st0s, do)

    @jax.custom_vjp
    def op(q, k, v, aux):
        return fwd_call(q, k, v, aux)[0]

    def fwd(q, k, v, aux):
        o, st0s = fwd_call(q, k, v, aux)
        return o, (q, k, v, aux, st0s)

    def bwd(res, do):
        return tuple(bwd_call(*res, do))

    op.defvjp(fwd, bwd)
    return op(q, k, v, aux)


HALO = 8


def _neighbours(main, prev8, next8, i, nct, n_tiles):
    has_prev = jnp.logical_and(i != 0, i != nct).astype(F32)
    has_next = jnp.logical_and(i != nct - 1, i != n_tiles - 1).astype(F32)
    row = lax.broadcasted_iota(jnp.int32, main.shape, 0)
    down = jnp.where(row == 0, prev8[HALO - 1:HALO] * has_prev, pltpu.roll(main, 1, 0))
    up = jnp.where(row == ROW_TILE - 1, next8[0:1] * has_next, pltpu.roll(main, ROW_TILE - 1, 0))
    return down, up


def dwconv(x, w8, b, tc, name):
    r_total, width = x.shape
    n_tiles = r_total // ROW_TILE
    nct = tc // ROW_TILE
    per = ROW_TILE // HALO
    main_spec = pl.BlockSpec((ROW_TILE, width), lambda i: (i, 0))
    prev_spec = pl.BlockSpec((HALO, width), lambda i: (jnp.maximum(i * per - 1, 0), 0))
    next_spec = pl.BlockSpec((HALO, width), lambda i: (jnp.minimum((i + 1) * per, r_total // HALO - 1), 0))
    w_spec = pl.BlockSpec((8, width), lambda i: (0, 0))
    b_spec = pl.BlockSpec((1, width), lambda i: (0, 0))

    def fwd_call(x, w8, b):
        def body(x_ref, p_ref, n_ref, w_ref, b_ref, o_ref):
            xv = x_ref[...]
            down, up = _neighbours(xv, p_ref[...], n_ref[...], pl.program_id(0), nct, n_tiles)
            o_ref[...] = w_ref[0:1] * down + w_ref[1:2] * xv + w_ref[2:3] * up + b_ref[...]

        return pl.pallas_call(
            body, name=name + "_f", grid=(n_tiles,), in_specs=[main_spec, prev_spec, next_spec, w_spec, b_spec],
            out_specs=main_spec, out_shape=jax.ShapeDtypeStruct(x.shape, F32), compiler_params=_cparams(1),
        )(x, x, x, w8, b)

    def bwd_call(x, w8, g):
        def body(x_ref, xp_ref, xn_ref, g_ref, gp_ref, gn_ref, w_ref, dx_ref, dw_ref, db_ref):
            i = pl.program_id(0)
            xv, gv = x_ref[...], g_ref[...]
            x_down, x_up = _neighbours(xv, xp_ref[...], xn_ref[...], i, nct, n_tiles)
            g_down, g_up = _neighbours(gv, gp_ref[...], gn_ref[...], i, nct, n_tiles)
            dx_ref[...] = w_ref[0:1] * g_up + w_ref[1:2] * gv + w_ref[2:3] * g_down
            dw = jnp.concatenate([jnp.sum(gv * x_down, axis=0, keepdims=True),
                                  jnp.sum(gv * xv, axis=0, keepdims=True),
                                  jnp.sum(gv * x_up, axis=0, keepdims=True),
                                  jnp.zeros((5, width), F32)], axis=0)
            db = jnp.sum(gv, axis=0, keepdims=True)

            @pl.when(i == 0)
            def _():
                dw_ref[...] = dw
                db_ref[...] = db

            @pl.when(i != 0)
            def _():
                dw_ref[...] += dw
                db_ref[...] += db

        return pl.pallas_call(
            body, name=name + "_b", grid=(n_tiles,),
            in_specs=[main_spec, prev_spec, next_spec, main_spec, prev_spec, next_spec, w_spec],
            out_specs=[main_spec, w_spec, b_spec],
            out_shape=[jax.ShapeDtypeStruct(x.shape, F32), jax.ShapeDtypeStruct((8, width), F32),
                       jax.ShapeDtypeStruct((1, width), F32)],
            compiler_params=_cparams(1),
        )(x, x, x, g, g, g, w8)

    @jax.custom_vjp
    def op(x, w8, b):
        return fwd_call(x, w8, b)

    def fwd(x, w8, b):
        return fwd_call(x, w8, b), (x, w8)

    def bwd(res, g):
        return tuple(bwd_call(*res, g))

    op.defvjp(fwd, bwd)
    return op(x, w8, b)


def loss_head(h, target, tc, name):
    r_total, width = h.shape
    n_tiles = r_total // ROW_TILE
    nct = tc // ROW_TILE

    def call(h, target):
        def body(h_ref, t_ref, dh_ref, loss_ref, acc_ref):
            i = pl.program_id(0)

            @pl.when(i == 0)
            def _():
                acc_ref[...] = jnp.zeros_like(acc_ref)

            @pl.when(i < nct)
            def _():
                dh_ref[...] = jnp.zeros_like(dh_ref)

            @pl.when(i >= nct)
            def _():
                err = h_ref[...] - t_ref[...]
                dh_ref[...] = err * (1.0 / width)
                acc_ref[...] += jnp.sum((err * err).reshape(ROW_TILE // 8, 8, width), axis=0)

            @pl.when(i == n_tiles - 1)
            def _():
                loss_ref[...] = jnp.sum(acc_ref[...]).reshape(1, 1) * (0.5 / width)

        row = pl.BlockSpec((ROW_TILE, width), lambda i: (i, 0))
        return pl.pallas_call(
            body, name=name, grid=(n_tiles,),
            in_specs=[row, pl.BlockSpec((ROW_TILE, width), lambda i: (jnp.maximum(i - nct, 0), 0))],
            out_specs=[row, pl.BlockSpec((1, 1), lambda i: (0, 0))],
            out_shape=[jax.ShapeDtypeStruct(h.shape, F32), jax.ShapeDtypeStruct((1, 1), F32)],
            scratch_shapes=[pltpu.VMEM((8, width), F32)], compiler_params=_cparams(1),
        )(h, target)

    @jax.custom_vjp
    def op(h, target):
        return call(h, target)[1][0, 0]

    def fwd(h, target):
        dh, loss = call(h, target)
        return loss[0, 0], (dh, target)

    def bwd(res, g):
        dh, target = res
        return dh * g, jnp.zeros_like(target)

    op.defvjp(fwd, bwd)
    return op(h, target)


PACK_W = 1024
PACK_TILE = 128


def slab_sum(slabs, name):
    n_slab, n, _ = slabs.shape

    def body(s_ref, o_ref):
        acc = s_ref[0]
        for j in range(1, n_slab):
            acc = acc + s_ref[j]
        o_ref[...] = acc

    return pl.pallas_call(
        body, name=name, grid=(n // PACK_TILE,),
        in_specs=[pl.BlockSpec((n_slab, PACK_TILE, PACK_W), lambda i: (0, i, 0))],
        out_specs=pl.BlockSpec((PACK_TILE, PACK_W), lambda i: (i, 0)),
        out_shape=jax.ShapeDtypeStruct((n, PACK_W), F32), compiler_params=_cparams(1),
    )(slabs)


def adamw(g_slabs, w, m, v, name):
    n_slab, n, _ = g_slabs.shape

    def body(g_ref, w_ref, m_ref, v_ref, go_ref, d_ref, mo_ref, vo_ref):
        g = g_ref[0].astype(F32)
        for j in range(1, n_slab):
            g = g + g_ref[j].astype(F32)
        m_new = ADAM_B1 * m_ref[...] + (1.0 - ADAM_B1) * g
        v_new = ADAM_B2 * v_ref[...] + (1.0 - ADAM_B2) * (g * g)
        m_hat = m_new / (1.0 - ADAM_B1 ** ADAM_STEP)
        v_hat = v_new / (1.0 - ADAM_B2 ** ADAM_STEP)
        go_ref[...] = g
        d_ref[...] = -ADAM_LR * (m_hat / (jnp.sqrt(v_hat) + ADAM_EPS) + ADAM_WD * w_ref[...])
        mo_ref[...] = m_new
        vo_ref[...] = v_new

    flat = pl.BlockSpec((PACK_TILE, PACK_W), lambda i: (i, 0))
    return pl.pallas_call(
        body, name=name, grid=(n // PACK_TILE,),
        in_specs=[pl.BlockSpec((n_slab, PACK_TILE, PACK_W), lambda i: (0, i, 0)), flat, flat, flat],
        out_specs=[flat] * 4, out_shape=[jax.ShapeDtypeStruct((n, PACK_W), F32)] * 4, compiler_params=_cparams(1),
    )(g_slabs, w, m, v)


def all_gather(x, name):
    m_per, n = x.shape

    def body(x_ref, out_ref, send_sems, recv_sems, local_sem):
        px, py, pc = lax.axis_index("x"), lax.axis_index("y"), lax.axis_index("c")
        me, sibling = (px, py, pc), (px, py, 1 - pc)
        chips = [(1 - px, py), (px, 1 - py), (1 - px, 1 - py)]

        def rows(bx, by, bc):
            return out_ref.at[pl.ds((4 * bx + 2 * by + bc) * m_per, m_per), :]

        def copy(k, block, to, src=None):
            return pltpu.make_async_remote_copy(
                src_ref=rows(*block) if src is None else src, dst_ref=rows(*block),
                send_sem=send_sems.at[k], recv_sem=recv_sems.at[k], device_id=to, device_id_type=MESH)

        mine = pltpu.make_async_copy(x_ref, rows(*me), local_sem)
        mine.start()
        first = [copy(0, me, sibling, src=x_ref)]
        first += [copy(1 + j, me, (*chip, pc), src=x_ref) for j, chip in enumerate(chips)]
        for cp in first:
            cp.start()
        passed = [copy(4 + j, (*chip, pc), sibling) for j, chip in enumerate(chips)]
        for j, chip in enumerate(chips):
            copy(1 + j, (*chip, pc), me).wait_recv()
            passed[j].start()
        copy(0, sibling, me).wait_recv()
        for j, chip in enumerate(chips):
            copy(4 + j, (*chip, 1 - pc), me).wait_recv()
        for cp in first + passed:
            cp.wait_send()
        mine.wait()

    return pl.pallas_call(
        body, name=name, out_shape=jax.ShapeDtypeStruct((N_DEV * m_per, n), x.dtype),
        in_specs=[pl.BlockSpec(memory_space=pl.ANY)], out_specs=pl.BlockSpec(memory_space=pl.ANY),
        scratch_shapes=[pltpu.SemaphoreType.DMA((7,)), pltpu.SemaphoreType.DMA((7,)), pltpu.SemaphoreType.DMA],
    )(x)


N_CHIP = 4


def pair_swap(x, name):
    def body(x_ref, out_ref, send_sem, recv_sem):
        sibling = (lax.axis_index("x"), lax.axis_index("y"), 1 - lax.axis_index("c"))
        copy = pltpu.make_async_remote_copy(src_ref=x_ref, dst_ref=out_ref, send_sem=send_sem, recv_sem=recv_sem,
                                            device_id=sibling, device_id_type=MESH)
        copy.start()
        copy.wait()

    return pl.pallas_call(
        body, name=name, out_shape=jax.ShapeDtypeStruct(x.shape, x.dtype),
        in_specs=[pl.BlockSpec(memory_space=pl.ANY)], out_specs=pl.BlockSpec(memory_space=pl.ANY),
        scratch_shapes=[pltpu.SemaphoreType.DMA, pltpu.SemaphoreType.DMA],
    )(x)


def pair_add(a, b, name):
    n_slab, n, _ = a.shape

    def body(a_ref, b_ref, o_ref):
        o_ref[...] = (a_ref[...].astype(F32) + b_ref[...].astype(F32)).astype(o_ref.dtype)

    spec = pl.BlockSpec((1, PACK_TILE, PACK_W), lambda s, i: (s, i, 0))
    return pl.pallas_call(
        body, name=name, grid=(n_slab, n // PACK_TILE), in_specs=[spec, spec], out_specs=spec,
        out_shape=jax.ShapeDtypeStruct(a.shape, a.dtype), compiler_params=_cparams(2),
    )(a, b)


def chip_all_to_all(x, name):
    def body(x_ref, out_ref, send_sems, recv_sems, local_sem):
        px, py, pc = lax.axis_index("x"), lax.axis_index("y"), lax.axis_index("c")
        mine_idx = 2 * px + py
        local = pltpu.make_async_copy(x_ref.at[mine_idx], out_ref.at[mine_idx], local_sem)
        local.start()
        copies = []
        for k, (fx, fy) in enumerate(((0, 1), (1, 0), (1, 1))):
            qx, qy = px ^ fx, py ^ fy
            peer_idx = 2 * qx + qy
            copies.append((
                pltpu.make_async_remote_copy(
                    src_ref=x_ref.at[peer_idx], dst_ref=out_ref.at[mine_idx], send_sem=send_sems.at[k],
                    recv_sem=recv_sems.at[k], device_id=(qx, qy, pc), device_id_type=MESH),
                pltpu.make_async_remote_copy(
                    src_ref=x_ref.at[peer_idx], dst_ref=out_ref.at[peer_idx], send_sem=send_sems.at[k],
                    recv_sem=recv_sems.at[k], device_id=(qx, qy, pc), device_id_type=MESH)))
        for send, _ in copies:
            send.start()
        for _, landing in copies:
            landing.wait_recv()
        for send, _ in copies:
            send.wait_send()
        local.wait()

    return pl.pallas_call(
        body, name=name, out_shape=jax.ShapeDtypeStruct(x.shape, x.dtype),
        in_specs=[pl.BlockSpec(memory_space=pl.ANY)], out_specs=pl.BlockSpec(memory_space=pl.ANY),
        scratch_shapes=[pltpu.SemaphoreType.DMA((3,)), pltpu.SemaphoreType.DMA((3,)), pltpu.SemaphoreType.DMA],
    )(x)


IN_OFFSETS = {}
_off = 0
for _name, _width in (("mla_q", 256), ("mla_kv", 128), ("mla_kr", 32), ("gla_q", 512), ("gla_k", 512), ("gla_v", 512),
                      ("gla_g", 512), ("gla_rf", 16), ("gla_rb", 16), ("ret_q", 512), ("ret_k", 512), ("ret_v", 512),
                      ("ret_g", 512), ("gate_mla", 1024), ("gate_gla", 1024), ("gate_ret", 1024)):
    IN_OFFSETS[_name] = (_off, _off + _width)
    _off += _width
N_IN = _off

P_GLA, P_RET, P_GATE, P_MLAQ, P_MLAKV, P_MLAKR, P_RANK, P_END = 0, 2048, 4096, 7168, 7424, 7552, 7680, 7808


def _pad_in_proj(w):
    def cols(a, b):
        return w[:, IN_OFFSETS[a][0]:IN_OFFSETS[b][1]]

    def z(n):
        return jnp.zeros((w.shape[0], n), w.dtype)

    return jnp.concatenate([cols("gla_q", "gla_g"), cols("ret_q", "ret_g"), cols("gate_mla", "gate_ret"),
                            cols("mla_q", "mla_kv"), z(MLA_NOPE), cols("mla_kr", "mla_kr"),
                            z(HEAD_PAD - MLA_QK), cols("gla_rf", "gla_rb"), z(HEAD_PAD - 2 * GLA_RANK),
                            z(N_IN_PAD - P_END)], axis=1)


def _pad_last(a, n):
    return jnp.pad(a, [(0, 0)] * (a.ndim - 1) + [(0, n - a.shape[-1])])


def _position_tables(tc, t):
    pos = jnp.arange(t)
    inv = ROPE_THETA ** (-jnp.arange(MLA_ROPE // 4, dtype=F32) * 2.0 / (MLA_ROPE // 2))
    ang_r = (pos // GRID_W).astype(F32)[:, None] * inv[None, :]
    ang_c = (pos % GRID_W).astype(F32)[:, None] * inv[None, :]
    z8, z32, z64 = jnp.zeros((t, 8), F32), jnp.zeros((t, 32), F32), jnp.zeros((t, 64), F32)
    lat_c = jnp.concatenate([jnp.ones((t, 64), F32), jnp.cos(ang_r), jnp.cos(ang_r), jnp.cos(ang_c), jnp.cos(ang_c),
                             z32], axis=1)
    lat_sn = jnp.concatenate([z64, -jnp.sin(ang_r), z8, -jnp.sin(ang_c), z8, z32], axis=1)
    lat_sp = jnp.concatenate([z64, z8, jnp.sin(ang_r), z8, jnp.sin(ang_c), z32], axis=1)
    ctx_c = jnp.concatenate([jnp.ones((tc, MLA_QK), F32), jnp.zeros((tc, HEAD_PAD - MLA_QK), F32)], axis=1)
    ctx_z = jnp.zeros((tc, HEAD_PAD), F32)
    rinv = 1.0 / (RET_THETA ** jnp.linspace(0.0, 1.0, RET_DK // 2, dtype=F32))
    rang = jnp.arange(tc + t).astype(F32)[:, None] * rinv[None, :]
    return dict(c=jnp.concatenate([ctx_c, lat_c]), sn=jnp.concatenate([ctx_z, lat_sn]),
                sp=jnp.concatenate([ctx_z, lat_sp]),
                rc=jnp.concatenate([jnp.cos(rang), jnp.cos(rang)], axis=1),
                rs=jnp.concatenate([-jnp.sin(rang), jnp.sin(rang)], axis=1))


def _heads(x):
    return [x[:, h * HEAD_PAD:(h + 1) * HEAD_PAD] for h in range(x.shape[1] // HEAD_PAD)]


def _mla_rope(x, c, sn, sp):
    return x * c + _roll(x, HEAD_PAD - 8, 1) * sn + _roll(x, 8, 1) * sp


def _norm_mod_fn(shift_row, scale_row):
    def fn(h, mod, w):
        return (_rms(h, D, w) * (1.0 + mod[scale_row:scale_row + 1]) + mod[shift_row:shift_row + 1],)
    return fn


def _resid_fn(gate_row):
    def fn(h, y, mod):
        return (h + mod[gate_row:gate_row + 1] * y,)
    return fn


def _mla_prep_fn(x, c, sn, sp, q_norm_a, w_qb, q_norm, kv_norm_a, w_k, w_v, k_norm):
    cq, ckv = x[:, :MLA_Q_LORA], x[:, MLA_Q_LORA:MLA_Q_LORA + MLA_KV_LORA]
    kr = x[:, MLA_Q_LORA + MLA_KV_LORA:]
    qf = _bdot(_rms(cq, MLA_Q_LORA, q_norm_a), w_qb, "nn")
    q = jnp.concatenate([_mla_rope(_rms(qh, MLA_QK, q_norm), c, sn, sp) for qh in _heads(qf)], axis=1)
    xkv = _rms(ckv, MLA_KV_LORA, kv_norm_a)
    kf = _bdot(xkv, w_k, "nn")
    k = jnp.concatenate([_mla_rope(_rms(kh + kr, MLA_QK, k_norm), c, sn, sp) for kh in _heads(kf)], axis=1)
    return q, k, _bdot(xkv, w_v, "nn")


def _decay_fn(x, w2, b):
    la = _log_sigmoid(_bdot(x[:, :HEAD_PAD], w2, "nn") + b) * (1.0 / GLA_NORMALIZER)
    return la[:, :GLA_HEADS * GLA_DK], la[:, GLA_HEADS * GLA_DK:]


def _ret_rot_fn(q, k, rc, rs):
    def rot(x, scale):
        return jnp.concatenate([(xh * rc + _roll(xh, RET_DK // 2, 1) * rs) * scale for xh in _heads(x)], axis=1)
    return rot(q, 1.0), rot(k, RET_DK ** -0.5)


def _gla_out_fn(o_f, o_b, g, w):
    y = jnp.concatenate([_rms(oh, HEAD_PAD, w) for oh in _heads(o_f + o_b)], axis=1)
    return (y * _silu(g),)


def _ret_out_fn(o_f, o_b, g):
    y = jnp.concatenate([_rms(oh, HEAD_PAD) for oh in _heads(o_f + o_b)], axis=1)
    return (y * _silu(g),)


def _merge_fn(z0, z1, z2, g0a, g0b, g1a, g1b, g2a, g2b, bg):
    out = 0.0
    for n, (z, ga, gb) in enumerate(((z0, g0a, g0b), (z1, g1a, g1b), (z2, g2a, g2b))):
        out = out + jax.nn.sigmoid(jnp.concatenate([ga, gb], axis=1) + bg[n:n + 1]) * z
    return (out,)


def _ffn_act_fn(c, up):
    return (_gelu_tanh(c) * up,)


def _layer(l, h, mod, w, tabs, tc):
    nct = tc // ROW_TILE
    tag = f"_l{l}"
    row = lambda a: a[l][None]
    a = rowwise("norm1" + tag, _norm_mod_fn(0, 1), [h], [mod], [row(w["norm1_w"])], [D], nct)[0]
    pieces = linear_pieces(a, _pad_in_proj(w["w_in"][l]), "in_proj" + tag)
    piece = lambda start: pieces[start // PIECE_W]

    w_qb = _pad_last(w["mla_w_qb"][l].reshape(MLA_Q_LORA, MLA_HEADS, MLA_QK), HEAD_PAD).reshape(MLA_Q_LORA, -1)
    w_kvb = w["mla_w_kvb"][l].reshape(MLA_KV_LORA, MLA_HEADS, MLA_NOPE + MLA_V)
    w_k = _pad_last(w_kvb[:, :, :MLA_NOPE], HEAD_PAD).reshape(MLA_KV_LORA, -1)
    w_v = _pad_last(w_kvb[:, :, MLA_NOPE:], HEAD_PAD).reshape(MLA_KV_LORA, -1)
    rope = [tabs["c"], tabs["sn"], tabs["sp"]]
    q, k, v = rowwise("mla_prep" + tag, _mla_prep_fn, [piece(P_MLAQ)] + rope, [],
                      [row(w["mla_q_norm_a"]), w_qb, _pad_last(row(w["mla_q_norm"]), HEAD_PAD),
                       row(w["mla_kv_norm_a"]), w_k, w_v, _pad_last(row(w["mla_k_norm"]), HEAD_PAD)],
                      [MLA_HEADS * HEAD_PAD] * 3, nct, diff_rows=[True, False, False, False])
    y_mla = attention(q, k, v, tc, "attn" + tag)
    wb_mla = _pad_last(w["w_branch"][l, 0].reshape(MLA_HEADS, MLA_V, D).transpose(0, 2, 1), HEAD_PAD)
    wb_mla = wb_mla.transpose(0, 2, 1).reshape(MLA_HEADS * HEAD_PAD, D)

    w2 = jnp.zeros((HEAD_PAD, 2 * GLA_HEADS * GLA_DK), F32)
    w2 = w2.at[:GLA_RANK, :GLA_HEADS * GLA_DK].set(w["gla_w_gk2"][l, 0])
    w2 = w2.at[GLA_RANK:2 * GLA_RANK, GLA_HEADS * GLA_DK:].set(w["gla_w_gk2"][l, 1])
    la_f, la_b = rowwise("gla_decay" + tag, _decay_fn, [piece(P_RANK)], [],
                         [w2, w["gla_b_gk"][l].reshape(1, -1)], [GLA_HEADS * GLA_DK] * 2, nct)
    gq, gk, gv, gg = [piece(P_GLA + n * PIECE_W) for n in range(4)]
    o_f = scan("gla", True, gq, gk, gv, la_f, tc, "gla_fw" + tag)
    o_b = scan("gla", False, gq, gk, gv, la_b, tc, "gla_bw" + tag)
    y_gla = rowwise("gla_out" + tag, _gla_out_fn, [o_f, o_b, gg], [], [row(w["gla_o_norm"])], [512], nct)[0]

    rq, rk = rowwise("ret_rot" + tag, _ret_rot_fn, [piece(P_RET), piece(P_RET + PIECE_W), tabs["rc"], tabs["rs"]],
                     [], [], [512, 512], nct, diff_rows=[True, True, False, False])
    rv, rg = piece(P_RET + 2 * PIECE_W), piece(P_RET + 3 * PIECE_W)
    rd = jnp.broadcast_to(w["ret_decay"][l][:, :, None, None], (2, RET_HEADS, 8, HEAD_PAD))
    r_f = scan("ret", True, rq, rk, rv, rd[0], tc, "ret_fw" + tag)
    r_b = scan("ret", False, rq, rk, rv, rd[1], tc, "ret_bw" + tag)
    y_ret = rowwise("ret_out" + tag, _ret_out_fn, [r_f, r_b, rg], [], [], [512], nct)[0]

    z = [linear(y_mla, wb_mla, "branch_mla" + tag), linear(y_gla, w["w_branch"][l, 1], "branch_gla" + tag),
         linear(y_ret, w["w_branch"][l, 2], "branch_ret" + tag)]
    gates = [piece(P_GATE + n * PIECE_W) for n in range(6)]
    merged = rowwise("merge" + tag, _merge_fn, z + gates, [], [_pad_rows(w["b_gate"][l], 8)], [D], nct)[0]
    y = linear(merged, w["w_out"][l], "w_out" + tag)
    h = rowwise("resid1" + tag, _resid_fn(2), [h, y], [mod], [], [D], nct)[0]

    a2 = rowwise("norm2" + tag, _norm_mod_fn(3, 4), [h], [mod], [row(w["norm2_w"])], [D], nct)[0]
    gate = linear(a2, w["w_ffn_in"][l][:, :D_FF], "ffn_gate" + tag)
    up = linear(a2, w["w_ffn_in"][l][:, D_FF:], "ffn_up" + tag)
    conv = dwconv(gate, _pad_rows(w["w_dw"][l], 8), row(w["b_dw"]), tc, "dwconv" + tag)
    u = rowwise("ffn_act" + tag, _ffn_act_fn, [conv, up], [], [], [D_FF], nct)[0]
    f = linear(u, w["w_ffn_out"][l], "ffn_out" + tag)
    return rowwise("resid2" + tag, _resid_fn(5), [h, f], [mod], [], [D], nct)[0]


def _pad_rows(a, n):
    return jnp.pad(a, [(0, n - a.shape[0])] + [(0, 0)] * (a.ndim - 1))


def local_loss(w, mod, x, ctx, target):
    tc, t = ctx.shape[0], x.shape[0]
    tabs = _position_tables(tc, t)
    h = jnp.concatenate([ctx, x], axis=0)
    for l in range(DEPTH):
        h = _layer(l, h, mod[l], w, tabs, tc)
    return loss_head(h, target, tc, "loss_head")


ADA_ROWS = 16


def ada_forward(cond_in, w_ada, b_loc):
    cols = w_ada.shape[2]

    def body(x_ref, w_ref, b_ref, o_ref):
        s = _silu(x_ref[...])
        for l in range(DEPTH):
            o_ref[l] = _dg(s, w_ref[l], "nn") + b_ref[l]

    return pl.pallas_call(
        body, name="ada_forward", out_shape=jax.ShapeDtypeStruct((DEPTH, ADA_ROWS, cols), F32),
        compiler_params=pltpu.CompilerParams(vmem_limit_bytes=VMEM_LIMIT_BYTES),
    )(cond_in, w_ada, b_loc)


def ada_backward(cond_in, g_loc, dmod_own, w_ada):
    cols = w_ada.shape[2]

    def body(x_ref, g_ref, own_ref, w_ref, gw_ref, dc_ref, gb_ref):
        x = x_ref[...]
        s = _silu(x)
        dcond = jnp.zeros((8, D), F32)
        for l in range(DEPTH):
            g_ctx = jnp.sum(g_ref[2 * l], axis=0, keepdims=True)
            g_rows = jnp.concatenate([g_ref[2 * l + 1], jnp.broadcast_to(g_ctx, (8, cols))], axis=0)
            keep = lax.broadcasted_iota(jnp.int32, (ADA_ROWS, cols), 0) <= N_DEV
            gw_ref[l] = _dg(s, jnp.where(keep, g_rows, 0.0), "tn")
            dcond = dcond + _dg(jnp.broadcast_to(g_ctx, (8, cols)), w_ref[l], "nt")
            gb_ref[l:l + 1, :] = own_ref[2 * l:2 * l + 1, :] + own_ref[2 * l + 1:2 * l + 2, :]
        xc = x[N_DEV:N_DEV + 1]
        sig = jax.nn.sigmoid(xc)
        dc_ref[...] = dcond[0:1] * (sig * (1.0 + xc * (1.0 - sig)))

    return pl.pallas_call(
        body, name="ada_backward",
        out_shape=[jax.ShapeDtypeStruct(w_ada.shape, F32), jax.ShapeDtypeStruct((1, D), F32),
                   jax.ShapeDtypeStruct((DEPTH, 6 * D), F32)],
        compiler_params=pltpu.CompilerParams(vmem_limit_bytes=VMEM_LIMIT_BYTES),
    )(cond_in, g_loc, dmod_own, w_ada)


WEIGHTS = ["c_ctx", "w_ada", "b_ada", "norm1_w", "norm2_w", "w_in", "b_gate", "mla_q_norm_a", "mla_w_qb",
           "mla_kv_norm_a", "mla_w_kvb", "mla_q_norm", "mla_k_norm", "gla_w_gk2", "gla_b_gk", "gla_o_norm",
           "ret_decay", "w_branch", "w_out", "w_ffn_in", "w_dw", "b_dw", "w_ffn_out"]
INPUTS = ["x", "c", "ctx"] + WEIGHTS + ["loss_target"] + ["m_" + n for n in WEIGHTS] + ["v_" + n for n in WEIGHTS]
BIG = {"w_in": 2, "mla_w_qb": 2, "mla_w_kvb": 2, "w_branch": 3, "w_out": 1, "w_ffn_in": 2, "w_ffn_out": 1}
SMALL_SHARDED = {"b_gate": 2, "gla_w_gk2": 3, "gla_b_gk": 2, "w_dw": 2}
SMALL = ["c_ctx", "b_ada", "norm1_w", "norm2_w", "b_gate", "mla_q_norm_a", "mla_kv_norm_a", "mla_q_norm", "mla_k_norm",
         "gla_w_gk2", "gla_b_gk", "gla_o_norm", "ret_decay", "w_dw", "b_dw"]


def _entry_rows(size, align):
    return -(-size // (PACK_W * align)) * align


def _pack(arrays, rows, dtype, align, lead=0):
    parts = []
    for a in arrays:
        head = a.shape[:lead]
        size = math.prod(a.shape[lead:])
        r = _entry_rows(size, align)
        if r * PACK_W == size:
            parts.append(a.astype(dtype).reshape(head + (r, PACK_W)))
        else:
            flat = jnp.pad(a.astype(dtype).reshape(head + (size,)), [(0, 0)] * lead + [(0, r * PACK_W - size)])
            parts.append(flat.reshape(head + (r, PACK_W)))
    used = sum(p.shape[lead] for p in parts)
    if rows > used:
        parts.append(jnp.zeros(parts[0].shape[:lead] + (rows - used, PACK_W), dtype))
    return jnp.concatenate(parts, axis=lead)


def _pack_rows(shapes, align, multiple):
    used = sum(_entry_rows(math.prod(s), align) for s in shapes)
    return -(-used // multiple) * multiple


def _unpack(pack, shapes, align):
    head = pack.shape[:-2]
    out, off = [], 0
    for shape in shapes:
        size = math.prod(shape)
        r = _entry_rows(size, align)
        block = lax.slice_in_dim(pack, off, off + r, axis=len(head))
        if r * PACK_W != size:
            block = block.reshape(head + (r * PACK_W,))[..., :size]
        out.append(block.reshape(head + tuple(shape)))
        off += r
    return out


def _join_shards(stacked, axis):
    moved = jnp.moveaxis(stacked, 0, axis)
    shape = list(moved.shape)
    return moved.reshape(shape[:axis] + [shape[axis] * shape[axis + 1]] + shape[axis + 2:])


def _split_shards(full, axis):
    shape = list(full.shape)
    split = full.reshape(shape[:axis] + [N_DEV, shape[axis] // N_DEV] + shape[axis + 1:])
    return jnp.moveaxis(split, axis, 0)


def _gather_shards(local, axes, dtype, rows_multiple, name):
    names = list(axes)
    shapes = [local[n].shape for n in names]
    rows = _pack_rows(shapes, rows_multiple, rows_multiple)
    gathered = all_gather(_pack([local[n] for n in names], rows, dtype, rows_multiple), name)
    stacked = _unpack(gathered.reshape(N_DEV, rows, PACK_W), shapes, rows_multiple)
    return {n: _join_shards(s, axes[n]).astype(F32) for n, s in zip(names, stacked)}


def kernel(*args):
    a = dict(zip(INPUTS, args))
    me = 4 * lax.axis_index("x") + 2 * lax.axis_index("y") + lax.axis_index("c")
    cols = a["w_ada"].shape[2]

    small_names = list(SMALL_SHARDED)
    small_local = [a[n].shape for n in small_names]
    first_rows = _pack_rows([a["c"].shape] + small_local, 8, 8)
    first = all_gather(_pack([a["c"]] + [a[n] for n in small_names], first_rows, F32, 8), "gather_small")
    first = _unpack(first.reshape(N_DEV, first_rows, PACK_W), [a["c"].shape] + small_local, 8)
    c_all = first[0][:, 0]

    cond_in = jnp.concatenate([c_all, a["c_ctx"][None], jnp.zeros((ADA_ROWS - N_DEV - 1, D), F32)], axis=0)
    b_loc = lax.dynamic_slice_in_dim(a["b_ada"], me * cols, cols, axis=1)[:, None, :]
    mod_loc = ada_forward(cond_in, a["w_ada"], b_loc)
    mod_all = all_gather(mod_loc.reshape(DEPTH * ADA_ROWS, cols), "gather_mod")
    mod_all = mod_all.reshape(N_DEV, DEPTH, ADA_ROWS, cols).transpose(1, 2, 0, 3).reshape(DEPTH, ADA_ROWS, 6, D)
    mod_me = lax.dynamic_index_in_dim(mod_all, me, axis=1, keepdims=False)
    mod = jnp.pad(jnp.stack([mod_all[:, N_DEV], mod_me], axis=1), ((0, 0), (0, 0), (0, 2), (0, 0)))

    w = _gather_shards(a, BIG, BF16, 16, "gather_weights")
    w.update({n: _join_shards(s, SMALL_SHARDED[n]) for n, s in zip(small_names, first[1:])})
    for n in SMALL:
        if n not in SMALL_SHARDED and n not in ("c_ctx", "b_ada"):
            w[n] = a[n]

    loss, (gw, gmod, gx) = jax.value_and_grad(local_loss, argnums=(0, 1, 2))(
        w, mod, a["x"][0], a["ctx"][0], a["loss_target"][0])
    loss = lax.psum(loss, ("x", "y", "c"))

    dmod_own = gmod[:, :, :6].reshape(2 * DEPTH, 6 * D)
    g_all = all_gather(jnp.pad(dmod_own, ((0, 8 - 2 * DEPTH), (0, 0))), "gather_dmod").reshape(N_DEV, 8, 6 * D)
    g_loc = lax.dynamic_slice_in_dim(g_all[:, :2 * DEPTH], me * cols, cols, axis=2).transpose(1, 0, 2)
    g_w_ada, g_c_ctx, g_b_ada = ada_backward(cond_in, g_loc, dmod_own, a["w_ada"])

    small_part = dict(gw, c_ctx=g_c_ctx, b_ada=g_b_ada)
    small_shapes = [a[n].shape if n not in SMALL_SHARDED else gw[n].shape for n in SMALL]
    rows = _pack_rows(small_shapes, 8, PACK_TILE)
    parts = all_gather(_pack([small_part[n] for n in SMALL], rows, F32, 8), "gather_small_grads")
    small_sum = _unpack(slab_sum(parts.reshape(N_DEV, rows, PACK_W), "sum_small_grads"), small_shapes, 8)
    g_small = {}
    for n, g in zip(SMALL, small_sum):
        if n in SMALL_SHARDED:
            ax = SMALL_SHARDED[n]
            g = lax.dynamic_slice_in_dim(g, me * a[n].shape[ax], a[n].shape[ax], axis=ax)
        g_small[n] = g

    big_rows = _pack_rows([a[n].shape for n in BIG], 16, PACK_TILE)
    slabs = _pack([_split_shards(gw[n], ax) for n, ax in BIG.items()], big_rows, BF16, 16, lead=1)
    by_core = slabs.reshape(N_CHIP, 2, big_rows, PACK_W)
    my_core = lax.axis_index("c")
    keep = lax.dynamic_index_in_dim(by_core, my_core, axis=1, keepdims=False)
    give = lax.dynamic_index_in_dim(by_core, 1 - my_core, axis=1, keepdims=False)
    pair_sum = pair_add(keep, pair_swap(give, "swap_grads"), "add_pair_grads")
    landed = chip_all_to_all(pair_sum, "scatter_grads")

    def update(names, g_slabs, rows, align, label):
        shapes = [a[n].shape for n in names]
        packs = [_pack([a[pre + n] for n in names], rows, F32, align) for pre in ("", "m_", "v_")]
        outs = adamw(g_slabs, *packs, label)
        return [dict(zip(names, _unpack(o, shapes, align))) for o in outs]

    res_big = update(list(BIG), landed, big_rows, 16, "adamw_big")
    ada_rows = _pack_rows([a["w_ada"].shape], 8, PACK_TILE)
    res_ada = update(["w_ada"], _pack([g_w_ada], ada_rows, F32, 8)[None], ada_rows, 8, "adamw_ada")
    small_rows = _pack_rows([a[n].shape for n in SMALL], 8, PACK_TILE)
    res_small = update(SMALL, _pack([g_small[n] for n in SMALL], small_rows, F32, 8)[None], small_rows, 8,
                       "adamw_small")

    outs = [loss, gx[None]]
    for k in range(4):
        merged = {**res_big[k], **res_ada[k], **res_small[k]}
        outs += [merged[n] for n in WEIGHTS]
    return tuple(outs)
```

```python
import functools
import math

import jax
import jax.numpy as jnp
import numpy as np
from jax import lax
from jax.experimental import pallas as pl
from jax.experimental.pallas import tpu as pltpu

F32 = jnp.float32
BF16 = jnp.bfloat16

N_DEV = 8
D = 1024
DEPTH = 2
GRID_W = 64
MLA_HEADS = 8
MLA_NOPE = 64
MLA_ROPE = 32
MLA_QK = 96
MLA_V = 64
MLA_Q_LORA = 256
MLA_KV_LORA = 128
GLA_HEADS = 4
GLA_DK = 128
GLA_RANK = 16
GLA_NORMALIZER = 16.0
RET_HEADS = 4
RET_DK = 128
BRANCH_W = 512
D_FF = 2816
CHUNK = 64
ROPE_THETA = 10000.0
RET_THETA = 10000.0
EPS = 1e-6
HEAD_PAD = 128
N_IN_PAD = 8192

ADAM_LR = 0.001
ADAM_B1 = 0.9
ADAM_B2 = 0.999
ADAM_EPS = 1e-08
ADAM_WD = 0.01
ADAM_STEP = 10

ROW_TILE = 256
SCAN_CHUNKS = ROW_TILE // CHUNK
VMEM_LIMIT_BYTES = 56 * 1024 * 1024
MESH = pl.DeviceIdType.MESH


def _cparams(n_axes):
    return pltpu.CompilerParams(dimension_semantics=("arbitrary",) * n_axes, vmem_limit_bytes=VMEM_LIMIT_BYTES)


def _pick(dim, cands):
    for cand in cands:
        if dim % cand == 0:
            return cand
    return dim


_DOT_DIMS = {"nn": (((1,), (0,)), ((), ())), "nt": (((1,), (1,)), ((), ())), "tn": (((0,), (0,)), ((), ()))}


def _dg(a, b, mode):
    return lax.dot_general(a.astype(BF16), b.astype(BF16), _DOT_DIMS[mode], preferred_element_type=F32)


def _bdot(a, b, mode):
    @jax.custom_vjp
    def f(a, b):
        return _dg(a, b, mode)

    def fwd(a, b):
        return _dg(a, b, mode), (a, b)

    def bwd(res, g):
        a, b = res
        if mode == "nn":
            return _dg(g, b, "nt").astype(a.dtype), _dg(a, g, "tn").astype(b.dtype)
        if mode == "nt":
            return _dg(g, b, "nn").astype(a.dtype), _dg(g, a, "tn").astype(b.dtype)
        return _dg(b, g, "nt").astype(a.dtype), _dg(a, g, "nn").astype(b.dtype)

    f.defvjp(fwd, bwd)
    return f(a, b)


def _roll(x, shift, axis):
    n = x.shape[axis]
    shift = shift % n

    @jax.custom_vjp
    def f(x):
        return pltpu.roll(x, shift, axis)

    def fwd(x):
        return pltpu.roll(x, shift, axis), None

    def bwd(_, g):
        return (pltpu.roll(g, (n - shift) % n, axis),)

    f.defvjp(fwd, bwd)
    return f(x)


@jax.custom_jvp
def _log_sigmoid(x):
    return jnp.minimum(x, 0.0) - jnp.log(1.0 + jnp.exp(-jnp.abs(x)))


@_log_sigmoid.defjvp
def _log_sigmoid_jvp(primals, tangents):
    (x,), (t,) = primals, tangents
    return _log_sigmoid(x), t * jax.nn.sigmoid(-x)


def _rms(x, n, w=None):
    y = x * lax.rsqrt(jnp.sum(x * x, axis=-1, keepdims=True) * (1.0 / n) + EPS)
    return y if w is None else y * w


def _silu(x):
    return x * jax.nn.sigmoid(x)


def _gelu_tanh(x):
    return 0.5 * x * (1.0 + jnp.tanh(math.sqrt(2.0 / math.pi) * (x + 0.044715 * (x * x * x))))


def _mm(a, b, mode, name):
    if mode == "nn":
        (m, k), (_, n) = a.shape, b.shape
    elif mode == "nt":
        (m, k), (n, _) = a.shape, b.shape
    else:
        (k, m), (_, n) = a.shape, b.shape
    tm = _pick(m, (1024, 768, 1408, 512, 256, 128))
    tn = _pick(n, (1024, 1408, 512, 256, 128))
    tk = _pick(k, (1024, 768, 1408, 512, 256, 128))
    nk = k // tk
    if mode == "nn":
        a_spec = pl.BlockSpec((tm, tk), lambda i, j, kk: (i, kk))
        b_spec = pl.BlockSpec((tk, tn), lambda i, j, kk: (kk, j))
    elif mode == "nt":
        a_spec = pl.BlockSpec((tm, tk), lambda i, j, kk: (i, kk))
        b_spec = pl.BlockSpec((tn, tk), lambda i, j, kk: (j, kk))
    else:
        a_spec = pl.BlockSpec((tk, tm), lambda i, j, kk: (kk, i))
        b_spec = pl.BlockSpec((tk, tn), lambda i, j, kk: (kk, j))

    def body(a_ref, b_ref, o_ref, acc_ref):
        kk = pl.program_id(2)

        @pl.when(kk == 0)
        def _():
            acc_ref[...] = jnp.zeros_like(acc_ref)

        acc_ref[...] += _dg(a_ref[...], b_ref[...], mode)

        @pl.when(kk == nk - 1)
        def _():
            o_ref[...] = acc_ref[...]

    return pl.pallas_call(
        body, name=name, grid=(m // tm, n // tn, nk),
        in_specs=[a_spec, b_spec], out_specs=pl.BlockSpec((tm, tn), lambda i, j, kk: (i, j)),
        out_shape=jax.ShapeDtypeStruct((m, n), F32),
        scratch_shapes=[pltpu.VMEM((tm, tn), F32)],
        compiler_params=_cparams(3),
    )(a, b)


def linear(x, w, name):
    @jax.custom_vjp
    def op(x, w):
        return _mm(x, w.astype(BF16), "nn", name + "_f")

    def fwd(x, w):
        wb = w.astype(BF16)
        return _mm(x, wb, "nn", name + "_f"), (x, wb)

    def bwd(res, g):
        x, wb = res
        return _mm(g, wb, "nt", name + "_dx"), _mm(x, g, "tn", name + "_dw")

    op.defvjp(fwd, bwd)
    return op(x, w)


PIECE_W = 512
PIECE_ROWS = 384


def _mm_split(a, wb, name):
    (r, k), n = a.shape, wb.shape[1] // PIECE_W
    tm = _pick(r, (PIECE_ROWS, ROW_TILE))
    width = PIECE_GROUP * PIECE_W

    def body(a_ref, w_ref, *out_refs):
        j = pl.program_id(1)
        res = _dg(a_ref[...], w_ref[...], "nn")
        for group in range(n // PIECE_GROUP):
            @pl.when(j == group)
            def _(group=group):
                for p in range(PIECE_GROUP):
                    out_refs[group * PIECE_GROUP + p][...] = res[:, p * PIECE_W:(p + 1) * PIECE_W]

    return pl.pallas_call(
        body, name=name, grid=(r // tm, n // PIECE_GROUP),
        in_specs=[pl.BlockSpec((tm, k), lambda i, j: (i, 0)), pl.BlockSpec((k, width), lambda i, j: (0, j))],
        out_specs=[pl.BlockSpec((tm, PIECE_W), lambda i, j: (i, 0))] * n,
        out_shape=[jax.ShapeDtypeStruct((r, PIECE_W), F32)] * n, compiler_params=_cparams(2),
    )(a, wb)


def _mm_join(gs, wb, name):
    n, (r, _), k = len(gs), gs[0].shape, wb.shape[0]
    tm = _pick(r, (PIECE_ROWS, ROW_TILE))
    n_groups = n // PIECE_GROUP

    def body(*refs):
        g_refs, w_ref, o_ref, acc_ref = refs[:n], refs[n], refs[n + 1], refs[n + 2]
        j = pl.program_id(1)

        @pl.when(j == 0)
        def _():
            acc_ref[...] = jnp.zeros_like(acc_ref)

        for group in range(n_groups):
            @pl.when(j == group)
            def _(group=group):
                g = jnp.concatenate([g_refs[group * PIECE_GROUP + p][...].astype(BF16) for p in range(PIECE_GROUP)],
                                    axis=1)
                acc_ref[...] += _dg(g, w_ref[...], "nt")

        @pl.when(j == n_groups - 1)
        def _():
            o_ref[...] = acc_ref[...]

    return pl.pallas_call(
        body, name=name, grid=(r // tm, n_groups),
        in_specs=[pl.BlockSpec((tm, PIECE_W), lambda i, j: (i, 0))] * n
        + [pl.BlockSpec((k, PIECE_GROUP * PIECE_W), lambda i, j: (0, j))],
        out_specs=pl.BlockSpec((tm, k), lambda i, j: (i, 0)), out_shape=jax.ShapeDtypeStruct((r, k), F32),
        scratch_shapes=[pltpu.VMEM((tm, k), F32)], compiler_params=_cparams(2),
    )(*gs, wb)


PIECE_GROUP = 4


def _mm_join_tn(a, gs, name):
    n, (r, k) = len(gs), a.shape
    tk = ROW_TILE
    nk = r // tk
    width = PIECE_GROUP * PIECE_W

    def body(*refs):
        a_ref, g_refs, o_ref, acc_ref = refs[0], refs[1:n + 1], refs[n + 1], refs[n + 2]
        j, kk = pl.program_id(0), pl.program_id(1)

        @pl.when(kk == 0)
        def _():
            acc_ref[...] = jnp.zeros_like(acc_ref)

        av = a_ref[...]
        for group in range(n // PIECE_GROUP):
            @pl.when(j == group)
            def _(group=group):
                for p in range(PIECE_GROUP):
                    cols = slice(p * PIECE_W, (p + 1) * PIECE_W)
                    acc_ref[:, cols] += _dg(av, g_refs[group * PIECE_GROUP + p][...], "tn")

        @pl.when(kk == nk - 1)
        def _():
            o_ref[...] = acc_ref[...]

    def g_spec(jj):
        return pl.BlockSpec((tk, PIECE_W), lambda j, kk: (jnp.where(j == jj // PIECE_GROUP, kk, 0), 0))

    return pl.pallas_call(
        body, name=name, grid=(n // PIECE_GROUP, nk),
        in_specs=[pl.BlockSpec((tk, k), lambda j, kk: (kk, 0))] + [g_spec(jj) for jj in range(n)],
        out_specs=pl.BlockSpec((k, width), lambda j, kk: (0, j)),
        out_shape=jax.ShapeDtypeStruct((k, n * PIECE_W), F32),
        scratch_shapes=[pltpu.VMEM((k, width), F32)], compiler_params=_cparams(2),
    )(a, *gs)


def linear_pieces(x, w, name):
    @jax.custom_vjp
    def op(x, w):
        return tuple(_mm_split(x, w.astype(BF16), name + "_f"))

    def fwd(x, w):
        wb = w.astype(BF16)
        return tuple(_mm_split(x, wb, name + "_f")), (x, wb)

    def bwd(res, gs):
        x, wb = res
        return _mm_join(list(gs), wb, name + "_dx"), _mm_join_tn(x, list(gs), name + "_dw")

    op.defvjp(fwd, bwd)
    return op(x, w)


def rowwise(name, fn, rows, segs, params, out_widths, nct, diff_rows=None):
    n_row, n_seg, n_par, n_out = len(rows), len(segs), len(params), len(out_widths)
    diff_rows = [True] * n_row if diff_rows is None else list(diff_rows)
    r_total = rows[0].shape[0]
    n_tiles = r_total // ROW_TILE

    def seg_of(i):
        return jnp.where(i < nct, 0, 1)

    def row_spec(width):
        return pl.BlockSpec((ROW_TILE, width), lambda i: (i, 0))

    def seg_spec(shape):
        nd = len(shape)
        return pl.BlockSpec((1,) + tuple(shape[1:]), lambda i: (seg_of(i),) + (0,) * (nd - 1))

    def par_spec(shape):
        nd = len(shape)
        return pl.BlockSpec(tuple(shape), lambda i: (0,) * nd)

    in_specs = ([row_spec(r.shape[1]) for r in rows] + [seg_spec(s.shape) for s in segs]
                + [par_spec(p.shape) for p in params])

    def load(refs):
        vals = [r[...].astype(F32) for r in refs[:n_row]]
        vals += [r[0].astype(F32) for r in refs[n_row:n_row + n_seg]]
        vals += [r[...].astype(F32) for r in refs[n_row + n_seg:n_row + n_seg + n_par]]
        return vals

    def fwd_call(arrs):
        def body(*refs):
            outs = fn(*load(refs))
            for o_ref, val in zip(refs[n_row + n_seg + n_par:], outs):
                o_ref[...] = val

        return pl.pallas_call(
            body, name=name + "_f", grid=(n_tiles,), in_specs=in_specs,
            out_specs=[row_spec(w) for w in out_widths],
            out_shape=[jax.ShapeDtypeStruct((r_total, w), F32) for w in out_widths],
            compiler_params=_cparams(1),
        )(*arrs)

    d_idx = [k for k in range(n_row) if diff_rows[k]]

    def bwd_call(arrs, douts):
        n_in = n_row + n_seg + n_par

        def body(*refs):
            i = pl.program_id(0)
            vals = load(refs[:n_in])
            gs = [r[...] for r in refs[n_in:n_in + n_out]]
            out_refs = refs[n_in + n_out:]
            diff_pos = d_idx + list(range(n_row, n_in))

            def f(*dv):
                full = list(vals)
                for pos, v in zip(diff_pos, dv):
                    full[pos] = v
                return tuple(fn(*full))

            _, vjp = jax.vjp(f, *[vals[p] for p in diff_pos])
            grads = vjp(tuple(gs))
            nd = len(d_idx)
            for o_ref, g in zip(out_refs[:nd], grads[:nd]):
                o_ref[...] = g
            first_seg = jnp.logical_or(i == 0, i == nct)
            for o_ref, g in zip(out_refs[nd:nd + n_seg], grads[nd:nd + n_seg]):
                @pl.when(first_seg)
                def _(o_ref=o_ref, g=g):
                    o_ref[0] = g

                @pl.when(jnp.logical_not(first_seg))
                def _(o_ref=o_ref, g=g):
                    o_ref[0] += g
            for o_ref, g in zip(out_refs[nd + n_seg:], grads[nd + n_seg:]):
                @pl.when(i == 0)
                def _(o_ref=o_ref, g=g):
                    o_ref[...] = g

                @pl.when(i != 0)
                def _(o_ref=o_ref, g=g):
                    o_ref[...] += g

        out_specs = ([row_spec(rows[k].shape[1]) for k in d_idx] + [seg_spec(s.shape) for s in segs]
                     + [par_spec(p.shape) for p in params])
        out_shape = ([jax.ShapeDtypeStruct(rows[k].shape, F32) for k in d_idx]
                     + [jax.ShapeDtypeStruct(s.shape, F32) for s in segs]
                     + [jax.ShapeDtypeStruct(p.shape, F32) for p in params])
        return pl.pallas_call(
            body, name=name + "_b", grid=(n_tiles,),
            in_specs=in_specs + [row_spec(w) for w in out_widths],
            out_specs=out_specs, out_shape=out_shape, compiler_params=_cparams(1),
        )(*arrs, *douts)

    @jax.custom_vjp
    def op(*arrs):
        return tuple(fwd_call(arrs))

    def op_fwd(*arrs):
        return tuple(fwd_call(arrs)), arrs

    def op_bwd(arrs, douts):
        grads = list(bwd_call(arrs, douts))
        nd = len(d_idx)
        row_grads = [jnp.zeros_like(arrs[k]) for k in range(n_row)]
        for k, g in zip(d_idx, grads[:nd]):
            row_grads[k] = g
        return tuple(row_grads + grads[nd:])

    op.defvjp(op_fwd, op_bwd)
    return op(*rows, *segs, *params)


ATT_SCALE = MLA_QK ** -0.5
LOG2E = math.log2(math.e)
ATT_KEY_CHUNKS = (768, 512, 256)


ATT_LATENT_TILE = 1024


def _query_rows_spec(row0, tq):
    return pl.BlockSpec((pl.Element(tq), pl.Element(HEAD_PAD)),
                        lambda h, i: (pl.multiple_of(row0 + i * tq, ROW_TILE), pl.multiple_of(h * HEAD_PAD, HEAD_PAD)))


def _key_chunks(nk):
    kc = _pick(nk, ATT_KEY_CHUNKS)
    return [(c * kc, kc) for c in range(nk // kc)]


def _attn_fwd_call(q, k, v, row0, n_rows, tq, nk, name):
    def body(q_ref, k_ref, v_ref, o_ref, lse_ref):
        qv = q_ref[...]
        m = jnp.full((tq, 1), -jnp.inf, F32)
        l = jnp.zeros((tq, 1), F32)
        acc = jnp.zeros((tq, HEAD_PAD), F32)
        for start, size in _key_chunks(nk):
            s = lax.dot_general(qv, k_ref[start:start + size, :], _DOT_DIMS["nt"], preferred_element_type=F32)
            m_new = jnp.maximum(m, jnp.max(s, axis=-1, keepdims=True))
            alpha = jnp.exp2(m - m_new)
            p = jnp.exp2(s - m_new)
            l = alpha * l + jnp.sum(p, axis=-1, keepdims=True)
            acc = alpha * acc + lax.dot_general(p.astype(BF16), v_ref[start:start + size, :], _DOT_DIMS["nn"],
                                                preferred_element_type=F32)
            m = m_new
        o_ref[...] = acc / l
        lse_ref[...] = jnp.broadcast_to(m + jnp.log2(l), (tq, HEAD_PAD))

    out_spec = pl.BlockSpec((tq, HEAD_PAD), lambda h, i: (i, h))
    kv_spec = pl.BlockSpec((nk, HEAD_PAD), lambda h, i: (0, h))
    out = jax.ShapeDtypeStruct((n_rows, q.shape[1]), F32)
    return pl.pallas_call(
        body, name=name, grid=(MLA_HEADS, n_rows // tq), in_specs=[_query_rows_spec(row0, tq), kv_spec, kv_spec],
        out_specs=[out_spec, out_spec], out_shape=[out, out], compiler_params=_cparams(2),
    )(q, k, v)


def _attn_bwd_call(q, k, v, o, lse, do, row0, n_rows, tq, nk, name):
    nq = n_rows // tq

    def body(q_ref, k_ref, v_ref, o_ref, lse_ref, do_ref, dq_ref, dk_ref, dv_ref):
        i = pl.program_id(1)

        @pl.when(i == 0)
        def _():
            dk_ref[...] = jnp.zeros_like(dk_ref)
            dv_ref[...] = jnp.zeros_like(dv_ref)

        qv = q_ref[...]
        dov = do_ref[...]
        dob = dov.astype(BF16)
        lse = lse_ref[:, 0:1]
        delta = jnp.sum(dov * o_ref[...], axis=-1, keepdims=True)
        dq = jnp.zeros((tq, HEAD_PAD), F32)
        for start, size in _key_chunks(nk):
            kk = k_ref[start:start + size, :]
            vv = v_ref[start:start + size, :]
            s = lax.dot_general(qv, kk, _DOT_DIMS["nt"], preferred_element_type=F32)
            p = jnp.exp2(s - lse)
            dp = lax.dot_general(dob, vv, _DOT_DIMS["nt"], preferred_element_type=F32)
            g = (p * (dp - delta)).astype(BF16)
            dk_ref[start:start + size, :] += lax.dot_general(g, qv, _DOT_DIMS["tn"], preferred_element_type=F32)
            dv_ref[start:start + size, :] += lax.dot_general(p.astype(BF16), dob, _DOT_DIMS["tn"],
                                                             preferred_element_type=F32)
            dq = dq + lax.dot_general(g, kk, _DOT_DIMS["nn"], preferred_element_type=F32)
        dq_ref[...] = dq * ATT_SCALE

        @pl.when(i == nq - 1)
        def _():
            dk_ref[...] = dk_ref[...] * (1.0 / LOG2E)

    own_spec = pl.BlockSpec((tq, HEAD_PAD), lambda h, i: (i, h))
    kv_spec = pl.BlockSpec((nk, HEAD_PAD), lambda h, i: (0, h))
    rows_spec = _query_rows_spec(row0, tq)
    return pl.pallas_call(
        body, name=name, grid=(MLA_HEADS, nq),
        in_specs=[rows_spec, kv_spec, kv_spec, own_spec, own_spec, rows_spec],
        out_specs=[own_spec, kv_spec, kv_spec],
        out_shape=[jax.ShapeDtypeStruct((n_rows, q.shape[1]), F32), jax.ShapeDtypeStruct((nk, q.shape[1]), F32),
                   jax.ShapeDtypeStruct((nk, q.shape[1]), F32)],
        compiler_params=_cparams(2),
    )(q, k, v, o, lse, do)


def attention(q, k, v, tc, name):
    r_total = q.shape[0]
    tq_lat = _pick(r_total - tc, (ATT_LATENT_TILE, ROW_TILE))
    ranges = [(0, tc, ROW_TILE, tc, "_ctx"), (tc, r_total - tc, tq_lat, r_total, "_lat")]

    def operands(q, k, v):
        return (q * (ATT_SCALE * LOG2E)).astype(BF16), k.astype(BF16), v.astype(BF16)

    def forward(qb, kb, vb):
        return [_attn_fwd_call(qb, kb, vb, row0, n_rows, tq, nk, name + tag + "_f")
                for row0, n_rows, tq, nk, tag in ranges]

    @jax.custom_vjp
    def op(q, k, v):
        return jnp.concatenate([o for o, _ in forward(*operands(q, k, v))], axis=0)

    def fwd(q, k, v):
        qb, kb, vb = operands(q, k, v)
        parts = forward(qb, kb, vb)
        return jnp.concatenate([o for o, _ in parts], axis=0), (qb, kb, vb, parts)

    def bwd(res, do):
        qb, kb, vb, parts = res
        (dq_c, dk_c, dv_c), (dq_l, dk_l, dv_l) = [
            _attn_bwd_call(qb, kb, vb, o, lse, do, row0, n_rows, tq, nk, name + tag + "_b")
            for (o, lse), (row0, n_rows, tq, nk, tag) in zip(parts, ranges)]
        grow = lambda part: jnp.pad(part, ((0, r_total - tc), (0, 0)))
        return jnp.concatenate([dq_c, dq_l], axis=0), dk_l + grow(dk_c), dv_l + grow(dv_c)

    op.defvjp(fwd, bwd)
    return op(q, k, v)


CHUNK_SHIFT = CHUNK.bit_length() - 1


def _block_pairs():
    rows = lax.broadcasted_iota(jnp.int32, (ROW_TILE, ROW_TILE), 0)
    cols = lax.broadcasted_iota(jnp.int32, (ROW_TILE, ROW_TILE), 1)
    same = lax.shift_right_logical(rows, CHUNK_SHIFT) == lax.shift_right_logical(cols, CHUNK_SHIFT)
    return rows, cols, same


def _block_mask(kind):
    rows, cols, same = _block_pairs()
    order = {"lower_incl": rows >= cols, "upper_incl": rows <= cols, "lower_strict": rows > cols,
             "upper_strict": rows < cols}[kind]
    return jnp.logical_and(same, order)


def _row_chunk():
    return lax.shift_right_logical(lax.broadcasted_iota(jnp.int32, (ROW_TILE, 1), 0), CHUNK_SHIFT)


def _dot01(kind, x):
    m = _block_mask(kind).astype(BF16)
    hi = x.astype(BF16)
    rest = x - hi.astype(F32)
    mid = rest.astype(BF16)
    lo = (rest - mid.astype(F32)).astype(BF16)
    terms = jnp.concatenate([hi, mid, lo], axis=1)
    out = lax.dot_general(m, terms, _DOT_DIMS["nn"], preferred_element_type=F32)
    n = x.shape[1]
    return out[:, :n] + out[:, n:2 * n] + out[:, 2 * n:]


def _chunk_sums(x, forward):
    kinds = ("lower_incl", "upper_strict") if forward else ("upper_incl", "lower_strict")
    transposed = ("upper_incl", "lower_strict") if forward else ("lower_incl", "upper_strict")

    @jax.custom_vjp
    def f(x):
        return _dot01(kinds[0], x), _dot01(kinds[1], x)

    def fwd(x):
        return (_dot01(kinds[0], x), _dot01(kinds[1], x)), None

    def bwd(_, g):
        return (_dot01(transposed[0], g[0]) + _dot01(transposed[1], g[1]),)

    f.defvjp(fwd, bwd)
    return f(x)


def _scan_order(forward):
    if forward:
        return list(range(SCAN_CHUNKS)), lambda c: c * CHUNK + CHUNK - 1
    return list(range(SCAN_CHUNKS - 1, -1, -1)), lambda c: c * CHUNK


def _carry_states(forward, st0, inc_all, decay_of):
    order, _ = _scan_order(forward)
    entering = [None] * SCAN_CHUNKS
    st = st0
    for c in order:
        entering[c] = st
        st = st * decay_of(c) + inc_all[:, c * HEAD_PAD:(c + 1) * HEAD_PAD]
    return jnp.concatenate(entering, axis=0), st


def _per_chunk_lanes(x):
    chunk = _row_chunk()
    return jnp.concatenate([jnp.where(chunk == c, x, 0.0) for c in range(SCAN_CHUNKS)], axis=1)


def _own_chunk_lanes(x4):
    chunk = _row_chunk()
    n = x4.shape[1] // SCAN_CHUNKS
    out = jnp.where(chunk == 0, x4[:, :n], 0.0)
    for c in range(1, SCAN_CHUNKS):
        out = out + jnp.where(chunk == c, x4[:, c * n:(c + 1) * n], 0.0)
    return out


def _gla_block(forward, q, k, v, la, st0):
    cum, after = _chunk_sums(la, forward)
    _, last_row = _scan_order(forward)
    q_dec = q * (jnp.exp(cum) * (GLA_DK ** -0.5))
    att = _bdot(q_dec, k * jnp.exp(-cum), "nt")
    att = jnp.where(_block_mask("lower_incl" if forward else "upper_strict"), att, 0.0)
    inc_all = _bdot(v, _per_chunk_lanes(k * jnp.exp(after)), "tn")
    entering, st1 = _carry_states(forward, st0, inc_all,
                                  lambda c: jnp.exp(cum[last_row(c):last_row(c) + 1, :]))
    o = _bdot(att, v, "nn") + _own_chunk_lanes(_bdot(q_dec, entering, "nt"))
    return o, st1


def _ret_block(forward, q, k, v, rd, st0):
    lg = -jnp.exp(rd[0:1, 0:1])
    rows, cols, _ = _block_pairs()
    pos = jnp.bitwise_and(lax.broadcasted_iota(jnp.int32, (ROW_TILE, 1), 0), CHUNK - 1).astype(F32)
    if forward:
        to_end, from_start, rel = CHUNK - 1.0 - pos, pos + 1.0, (rows - cols).astype(F32)
    else:
        to_end, from_start, rel = pos, CHUNK - pos, (cols - rows).astype(F32)
    mask = _block_mask("lower_incl" if forward else "upper_strict")
    dmat = jnp.where(mask, jnp.exp(jnp.where(mask, rel, 0.0) * lg), 0.0)
    att = _bdot(q, k, "nt") * dmat
    inc_all = _bdot(v, _per_chunk_lanes(k * jnp.exp(to_end * lg)), "tn")
    entering, st1 = _carry_states(forward, st0, inc_all, lambda c: jnp.exp(CHUNK * lg))
    o = _bdot(att, v, "nn") + _own_chunk_lanes(_bdot(q, entering, "nt")) * jnp.exp(from_start * lg)
    return o, st1


def scan(kind, forward, q, k, v, aux, tc, name):
    heads = q.shape[1] // HEAD_PAD
    r_total = q.shape[0]
    nblk = r_total // ROW_TILE
    nctb = tc // ROW_TILE
    block_fn = functools.partial(_gla_block if kind == "gla" else _ret_block, forward)
    per_row_aux = kind == "gla"

    def blk(g):
        if forward:
            return g
        return jnp.where(g < nctb, nctb - 1 - g, nblk - 1 - (g - nctb))

    def specs(step_to_g):
        row = pl.BlockSpec((ROW_TILE, heads * HEAD_PAD), lambda s: (blk(step_to_g(s)), 0))
        aux_spec = row if per_row_aux else pl.BlockSpec((heads, 8, HEAD_PAD), lambda s: (0, 0, 0))
        st = pl.BlockSpec((1, heads, HEAD_PAD, HEAD_PAD), lambda s: (step_to_g(s), 0, 0, 0))
        return row, aux_spec, st

    def head_cols(h):
        return slice(h * HEAD_PAD, (h + 1) * HEAD_PAD)

    def fwd_call(q, k, v, aux):
        row, aux_spec, st_spec = specs(lambda s: s)

        def body(q_ref, k_ref, v_ref, a_ref, o_ref, st0_ref, st_ref):
            @pl.when(pl.program_id(0) == 0)
            def _():
                st_ref[...] = jnp.zeros_like(st_ref)

            qv, kv, vv = q_ref[...], k_ref[...], v_ref[...]
            outs = []
            for h in range(heads):
                st0 = st_ref[h]
                st0_ref[0, h] = st0
                a = a_ref[:, head_cols(h)] if per_row_aux else a_ref[h]
                o, st1 = block_fn(qv[:, head_cols(h)], kv[:, head_cols(h)], vv[:, head_cols(h)], a, st0)
                outs.append(o)
                st_ref[h] = st1
            o_ref[...] = jnp.concatenate(outs, axis=1)

        return pl.pallas_call(
            body, name=name + "_f", grid=(nblk,), in_specs=[row, row, row, aux_spec],
            out_specs=[row, st_spec],
            out_shape=[jax.ShapeDtypeStruct(q.shape, F32),
                       jax.ShapeDtypeStruct((nblk, heads, HEAD_PAD, HEAD_PAD), F32)],
            scratch_shapes=[pltpu.VMEM((heads, HEAD_PAD, HEAD_PAD), F32)],
            compiler_params=_cparams(1),
        )(q, k, v, aux)

    def bwd_call(q, k, v, aux, st0s, do):
        row, aux_spec, st_spec = specs(lambda s: nblk - 1 - s)

        def body(q_ref, k_ref, v_ref, a_ref, st0_ref, do_ref, dq_ref, dk_ref, dv_ref, da_ref, dst_ref):
            s = pl.program_id(0)

            @pl.when(s == 0)
            def _():
                dst_ref[...] = jnp.zeros_like(dst_ref)

            qv, kv, vv, dov = q_ref[...], k_ref[...], v_ref[...], do_ref[...]
            grads = []
            for h in range(heads):
                a = a_ref[:, head_cols(h)] if per_row_aux else a_ref[h]
                _, vjp = jax.vjp(block_fn, qv[:, head_cols(h)], kv[:, head_cols(h)], vv[:, head_cols(h)], a,
                                 st0_ref[0, h])
                dq, dk, dv, da, dst0 = vjp((dov[:, head_cols(h)], dst_ref[h]))
                dst_ref[h] = dst0
                grads.append((dq, dk, dv, da))
            dq_ref[...] = jnp.concatenate([g[0] for g in grads], axis=1)
            dk_ref[...] = jnp.concatenate([g[1] for g in grads], axis=1)
            dv_ref[...] = jnp.concatenate([g[2] for g in grads], axis=1)
            if per_row_aux:
                da_ref[...] = jnp.concatenate([g[3] for g in grads], axis=1)
            else:
                da = jnp.stack([g[3] for g in grads], axis=0)

                @pl.when(s == 0)
                def _():
                    da_ref[...] = da

                @pl.when(s != 0)
                def _():
                    da_ref[...] += da

        return pl.pallas_call(
            body, name=name + "_b", grid=(nblk,),
            in_specs=[row, row, row, aux_spec, st_spec, row],
            out_specs=[row, row, row, aux_spec],
            out_shape=[jax.ShapeDtypeStruct(q.shape, F32)] * 3 + [jax.ShapeDtypeStruct(aux.shape, F32)],
            scratch_shapes=[pltpu.VMEM((heads, HEAD_PAD, HEAD_PAD), F32)],
            compiler_params=_cparams(1),
        )(q, k, v, aux, st0s, do)

    @jax.custom_vjp
    def op(q, k, v, aux):
        return fwd_call(q, k, v, aux)[0]

    def fwd(q, k, v, aux):
        o, st0s = fwd_call(q, k, v, aux)
        return o, (q, k, v, aux, st0s)

    def bwd(res, do):
        return tuple(bwd_call(*res, do))

    op.defvjp(fwd, bwd)
    return op(q, k, v, aux)


HALO = 8


def _neighbours(main, prev8, next8, i, nct, n_tiles):
    has_prev = jnp.logical_and(i != 0, i != nct).astype(F32)
    has_next = jnp.logical_and(i != nct - 1, i != n_tiles - 1).astype(F32)
    row = lax.broadcasted_iota(jnp.int32, main.shape, 0)
    down = jnp.where(row == 0, prev8[HALO - 1:HALO] * has_prev, pltpu.roll(main, 1, 0))
    up = jnp.where(row == ROW_TILE - 1, next8[0:1] * has_next, pltpu.roll(main, ROW_TILE - 1, 0))
    return down, up


def dwconv(x, w8, b, tc, name):
    r_total, width = x.shape
    n_tiles = r_total // ROW_TILE
    nct = tc // ROW_TILE
    per = ROW_TILE // HALO
    main_spec = pl.BlockSpec((ROW_TILE, width), lambda i: (i, 0))
    prev_spec = pl.BlockSpec((HALO, width), lambda i: (jnp.maximum(i * per - 1, 0), 0))
    next_spec = pl.BlockSpec((HALO, width), lambda i: (jnp.minimum((i + 1) * per, r_total // HALO - 1), 0))
    w_spec = pl.BlockSpec((8, width), lambda i: (0, 0))
    b_spec = pl.BlockSpec((1, width), lambda i: (0, 0))

    def fwd_call(x, w8, b):
        def body(x_ref, p_ref, n_ref, w_ref, b_ref, o_ref):
            xv = x_ref[...]
            down, up = _neighbours(xv, p_ref[...], n_ref[...], pl.program_id(0), nct, n_tiles)
            o_ref[...] = w_ref[0:1] * down + w_ref[1:2] * xv + w_ref[2:3] * up + b_ref[...]

        return pl.pallas_call(
            body, name=name + "_f", grid=(n_tiles,), in_specs=[main_spec, prev_spec, next_spec, w_spec, b_spec],
            out_specs=main_spec, out_shape=jax.ShapeDtypeStruct(x.shape, F32), compiler_params=_cparams(1),
        )(x, x, x, w8, b)

    def bwd_call(x, w8, g):
        def body(x_ref, xp_ref, xn_ref, g_ref, gp_ref, gn_ref, w_ref, dx_ref, dw_ref, db_ref):
            i = pl.program_id(0)
            xv, gv = x_ref[...], g_ref[...]
            x_down, x_up = _neighbours(xv, xp_ref[...], xn_ref[...], i, nct, n_tiles)
            g_down, g_up = _neighbours(gv, gp_ref[...], gn_ref[...], i, nct, n_tiles)
            dx_ref[...] = w_ref[0:1] * g_up + w_ref[1:2] * gv + w_ref[2:3] * g_down
            dw = jnp.concatenate([jnp.sum(gv * x_down, axis=0, keepdims=True),
                                  jnp.sum(gv * xv, axis=0, keepdims=True),
                                  jnp.sum(gv * x_up, axis=0, keepdims=True),
                                  jnp.zeros((5, width), F32)], axis=0)
            db = jnp.sum(gv, axis=0, keepdims=True)

            @pl.when(i == 0)
            def _():
                dw_ref[...] = dw
                db_ref[...] = db

            @pl.when(i != 0)
            def _():
                dw_ref[...] += dw
                db_ref[...] += db

        return pl.pallas_call(
            body, name=name + "_b", grid=(n_tiles,),
            in_specs=[main_spec, prev_spec, next_spec, main_spec, prev_spec, next_spec, w_spec],
            out_specs=[main_spec, w_spec, b_spec],
            out_shape=[jax.ShapeDtypeStruct(x.shape, F32), jax.ShapeDtypeStruct((8, width), F32),
                       jax.ShapeDtypeStruct((1, width), F32)],
            compiler_params=_cparams(1),
        )(x, x, x, g, g, g, w8)

    @jax.custom_vjp
    def op(x, w8, b):
        return fwd_call(x, w8, b)

    def fwd(x, w8, b):
        return fwd_call(x, w8, b), (x, w8)

    def bwd(res, g):
        return tuple(bwd_call(*res, g))

    op.defvjp(fwd, bwd)
    return op(x, w8, b)


def loss_head(h, target, tc, name):
    r_total, width = h.shape
    n_tiles = r_total // ROW_TILE
    nct = tc // ROW_TILE

    def call(h, target):
        def body(h_ref, t_ref, dh_ref, loss_ref, acc_ref):
            i = pl.program_id(0)

            @pl.when(i == 0)
            def _():
                acc_ref[...] = jnp.zeros_like(acc_ref)

            @pl.when(i < nct)
            def _():
                dh_ref[...] = jnp.zeros_like(dh_ref)

            @pl.when(i >= nct)
            def _():
                err = h_ref[...] - t_ref[...]
                dh_ref[...] = err * (1.0 / width)
                acc_ref[...] += jnp.sum((err * err).reshape(ROW_TILE // 8, 8, width), axis=0)

            @pl.when(i == n_tiles - 1)
            def _():
                loss_ref[...] = jnp.sum(acc_ref[...]).reshape(1, 1) * (0.5 / width)

        row = pl.BlockSpec((ROW_TILE, width), lambda i: (i, 0))
        return pl.pallas_call(
            body, name=name, grid=(n_tiles,),
            in_specs=[row, pl.BlockSpec((ROW_TILE, width), lambda i: (jnp.maximum(i - nct, 0), 0))],
            out_specs=[row, pl.BlockSpec((1, 1), lambda i: (0, 0))],
            out_shape=[jax.ShapeDtypeStruct(h.shape, F32), jax.ShapeDtypeStruct((1, 1), F32)],
            scratch_shapes=[pltpu.VMEM((8, width), F32)], compiler_params=_cparams(1),
        )(h, target)

    @jax.custom_vjp
    def op(h, target):
        return call(h, target)[1][0, 0]

    def fwd(h, target):
        dh, loss = call(h, target)
        return loss[0, 0], (dh, target)

    def bwd(res, g):
        dh, target = res
        return dh * g, jnp.zeros_like(target)

    op.defvjp(fwd, bwd)
    return op(h, target)


PACK_W = 1024
PACK_TILE = 128


def slab_sum(slabs, name):
    n_slab, n, _ = slabs.shape

    def body(s_ref, o_ref):
        acc = s_ref[0]
        for j in range(1, n_slab):
            acc = acc + s_ref[j]
        o_ref[...] = acc

    return pl.pallas_call(
        body, name=name, grid=(n // PACK_TILE,),
        in_specs=[pl.BlockSpec((n_slab, PACK_TILE, PACK_W), lambda i: (0, i, 0))],
        out_specs=pl.BlockSpec((PACK_TILE, PACK_W), lambda i: (i, 0)),
        out_shape=jax.ShapeDtypeStruct((n, PACK_W), F32), compiler_params=_cparams(1),
    )(slabs)


def adamw(g_slabs, w, m, v, name):
    n_slab, n, _ = g_slabs.shape

    def body(g_ref, w_ref, m_ref, v_ref, go_ref, d_ref, mo_ref, vo_ref):
        g = g_ref[0].astype(F32)
        for j in range(1, n_slab):
            g = g + g_ref[j].astype(F32)
        m_new = ADAM_B1 * m_ref[...] + (1.0 - ADAM_B1) * g
        v_new = ADAM_B2 * v_ref[...] + (1.0 - ADAM_B2) * (g * g)
        m_hat = m_new / (1.0 - ADAM_B1 ** ADAM_STEP)
        v_hat = v_new / (1.0 - ADAM_B2 ** ADAM_STEP)
        go_ref[...] = g
        d_ref[...] = -ADAM_LR * (m_hat / (jnp.sqrt(v_hat) + ADAM_EPS) + ADAM_WD * w_ref[...])
        mo_ref[...] = m_new
        vo_ref[...] = v_new

    flat = pl.BlockSpec((PACK_TILE, PACK_W), lambda i: (i, 0))
    return pl.pallas_call(
        body, name=name, grid=(n // PACK_TILE,),
        in_specs=[pl.BlockSpec((n_slab, PACK_TILE, PACK_W), lambda i: (0, i, 0)), flat, flat, flat],
        out_specs=[flat] * 4, out_shape=[jax.ShapeDtypeStruct((n, PACK_W), F32)] * 4, compiler_params=_cparams(1),
    )(g_slabs, w, m, v)


def all_gather(x, name):
    m_per, n = x.shape

    def body(x_ref, out_ref, send_sems, recv_sems, local_sem):
        px, py, pc = lax.axis_index("x"), lax.axis_index("y"), lax.axis_index("c")
        me, sibling = (px, py, pc), (px, py, 1 - pc)
        chips = [(1 - px, py), (px, 1 - py), (1 - px, 1 - py)]

        def rows(bx, by, bc):
            return out_ref.at[pl.ds((4 * bx + 2 * by + bc) * m_per, m_per), :]

        def copy(k, block, to, src=None):
            return pltpu.make_async_remote_copy(
                src_ref=rows(*block) if src is None else src, dst_ref=rows(*block),
                send_sem=send_sems.at[k], recv_sem=recv_sems.at[k], device_id=to, device_id_type=MESH)

        mine = pltpu.make_async_copy(x_ref, rows(*me), local_sem)
        mine.start()
        first = [copy(0, me, sibling, src=x_ref)]
        first += [copy(1 + j, me, (*chip, pc), src=x_ref) for j, chip in enumerate(chips)]
        for cp in first:
            cp.start()
        passed = [copy(4 + j, (*chip, pc), sibling) for j, chip in enumerate(chips)]
        for j, chip in enumerate(chips):
            copy(1 + j, (*chip, pc), me).wait_recv()
            passed[j].start()
        copy(0, sibling, me).wait_recv()
        for j, chip in enumerate(chips):
            copy(4 + j, (*chip, 1 - pc), me).wait_recv()
        for cp in first + passed:
            cp.wait_send()
        mine.wait()

    return pl.pallas_call(
        body, name=name, out_shape=jax.ShapeDtypeStruct((N_DEV * m_per, n), x.dtype),
        in_specs=[pl.BlockSpec(memory_space=pl.ANY)], out_specs=pl.BlockSpec(memory_space=pl.ANY),
        scratch_shapes=[pltpu.SemaphoreType.DMA((7,)), pltpu.SemaphoreType.DMA((7,)), pltpu.SemaphoreType.DMA],
    )(x)


N_CHIP = 4


def pair_swap(x, name):
    def body(x_ref, out_ref, send_sem, recv_sem):
        sibling = (lax.axis_index("x"), lax.axis_index("y"), 1 - lax.axis_index("c"))
        copy = pltpu.make_async_remote_copy(src_ref=x_ref, dst_ref=out_ref, send_sem=send_sem, recv_sem=recv_sem,
                                            device_id=sibling, device_id_type=MESH)
        copy.start()
        copy.wait()

    return pl.pallas_call(
        body, name=name, out_shape=jax.ShapeDtypeStruct(x.shape, x.dtype),
        in_specs=[pl.BlockSpec(memory_space=pl.ANY)], out_specs=pl.BlockSpec(memory_space=pl.ANY),
        scratch_shapes=[pltpu.SemaphoreType.DMA, pltpu.SemaphoreType.DMA],
    )(x)


def pair_add(a, b, name):
    n_slab, n, _ = a.shape

    def body(a_ref, b_ref, o_ref):
        o_ref[...] = (a_ref[...].astype(F32) + b_ref[...].astype(F32)).astype(o_ref.dtype)

    spec = pl.BlockSpec((1, PACK_TILE, PACK_W), lambda s, i: (s, i, 0))
    return pl.pallas_call(
        body, name=name, grid=(n_slab, n // PACK_TILE), in_specs=[spec, spec], out_specs=spec,
        out_shape=jax.ShapeDtypeStruct(a.shape, a.dtype), compiler_params=_cparams(2),
    )(a, b)


def chip_all_to_all(x, name):
    def body(x_ref, out_ref, send_sems, recv_sems, local_sem):
        px, py, pc = lax.axis_index("x"), lax.axis_index("y"), lax.axis_index("c")
        mine_idx = 2 * px + py
        local = pltpu.make_async_copy(x_ref.at[mine_idx], out_ref.at[mine_idx], local_sem)
        local.start()
        copies = []
        for k, (fx, fy) in enumerate(((0, 1), (1, 0), (1, 1))):
            qx, qy = px ^ fx, py ^ fy
            peer_idx = 2 * qx + qy
            copies.append((
                pltpu.make_async_remote_copy(
                    src_ref=x_ref.at[peer_idx], dst_ref=out_ref.at[mine_idx], send_sem=send_sems.at[k],
                    recv_sem=recv_sems.at[k], device_id=(qx, qy, pc), device_id_type=MESH),
                pltpu.make_async_remote_copy(
                    src_ref=x_ref.at[peer_idx], dst_ref=out_ref.at[peer_idx], send_sem=send_sems.at[k],
                    recv_sem=recv_sems.at[k], device_id=(qx, qy, pc), device_id_type=MESH)))
        for send, _ in copies:
            send.start()
        for _, landing in copies:
            landing.wait_recv()
        for send, _ in copies:
            send.wait_send()
        local.wait()

    return pl.pallas_call(
        body, name=name, out_shape=jax.ShapeDtypeStruct(x.shape, x.dtype),
        in_specs=[pl.BlockSpec(memory_space=pl.ANY)], out_specs=pl.BlockSpec(memory_space=pl.ANY),
        scratch_shapes=[pltpu.SemaphoreType.DMA((3,)), pltpu.SemaphoreType.DMA((3,)), pltpu.SemaphoreType.DMA],
    )(x)


IN_OFFSETS = {}
_off = 0
for _name, _width in (("mla_q", 256), ("mla_kv", 128), ("mla_kr", 32), ("gla_q", 512), ("gla_k", 512), ("gla_v", 512),
                      ("gla_g", 512), ("gla_rf", 16), ("gla_rb", 16), ("ret_q", 512), ("ret_k", 512), ("ret_v", 512),
                      ("ret_g", 512), ("gate_mla", 1024), ("gate_gla", 1024), ("gate_ret", 1024)):
    IN_OFFSETS[_name] = (_off, _off + _width)
    _off += _width
N_IN = _off

P_GLA, P_RET, P_GATE, P_MLAQ, P_MLAKV, P_MLAKR, P_RANK, P_END = 0, 2048, 4096, 7168, 7424, 7552, 7680, 7808


def _pad_in_proj(w):
    def cols(a, b):
        return w[:, IN_OFFSETS[a][0]:IN_OFFSETS[b][1]]

    def z(n):
        return jnp.zeros((w.shape[0], n), w.dtype)

    return jnp.concatenate([cols("gla_q", "gla_g"), cols("ret_q", "ret_g"), cols("gate_mla", "gate_ret"),
                            cols("mla_q", "mla_kv"), z(MLA_NOPE), cols("mla_kr", "mla_kr"),
                            z(HEAD_PAD - MLA_QK), cols("gla_rf", "gla_rb"), z(HEAD_PAD - 2 * GLA_RANK),
                            z(N_IN_PAD - P_END)], axis=1)


def _pad_last(a, n):
    return jnp.pad(a, [(0, 0)] * (a.ndim - 1) + [(0, n - a.shape[-1])])


def _position_tables(tc, t):
    pos = jnp.arange(t)
    inv = ROPE_THETA ** (-jnp.arange(MLA_ROPE // 4, dtype=F32) * 2.0 / (MLA_ROPE // 2))
    ang_r = (pos // GRID_W).astype(F32)[:, None] * inv[None, :]
    ang_c = (pos % GRID_W).astype(F32)[:, None] * inv[None, :]
    z8, z32, z64 = jnp.zeros((t, 8), F32), jnp.zeros((t, 32), F32), jnp.zeros((t, 64), F32)
    lat_c = jnp.concatenate([jnp.ones((t, 64), F32), jnp.cos(ang_r), jnp.cos(ang_r), jnp.cos(ang_c), jnp.cos(ang_c),
                             z32], axis=1)
    lat_sn = jnp.concatenate([z64, -jnp.sin(ang_r), z8, -jnp.sin(ang_c), z8, z32], axis=1)
    lat_sp = jnp.concatenate([z64, z8, jnp.sin(ang_r), z8, jnp.sin(ang_c), z32], axis=1)
    ctx_c = jnp.concatenate([jnp.ones((tc, MLA_QK), F32), jnp.zeros((tc, HEAD_PAD - MLA_QK), F32)], axis=1)
    ctx_z = jnp.zeros((tc, HEAD_PAD), F32)
    rinv = 1.0 / (RET_THETA ** jnp.linspace(0.0, 1.0, RET_DK // 2, dtype=F32))
    rang = jnp.arange(tc + t).astype(F32)[:, None] * rinv[None, :]
    return dict(c=jnp.concatenate([ctx_c, lat_c]), sn=jnp.concatenate([ctx_z, lat_sn]),
                sp=jnp.concatenate([ctx_z, lat_sp]),
                rc=jnp.concatenate([jnp.cos(rang), jnp.cos(rang)], axis=1),
                rs=jnp.concatenate([-jnp.sin(rang), jnp.sin(rang)], axis=1))


def _heads(x):
    return [x[:, h * HEAD_PAD:(h + 1) * HEAD_PAD] for h in range(x.shape[1] // HEAD_PAD)]


def _mla_rope(x, c, sn, sp):
    return x * c + _roll(x, HEAD_PAD - 8, 1) * sn + _roll(x, 8, 1) * sp


def _norm_mod_fn(shift_row, scale_row):
    def fn(h, mod, w):
        return (_rms(h, D, w) * (1.0 + mod[scale_row:scale_row + 1]) + mod[shift_row:shift_row + 1],)
    return fn


def _resid_fn(gate_row):
    def fn(h, y, mod):
        return (h + mod[gate_row:gate_row + 1] * y,)
    return fn


def _mla_prep_fn(x, c, sn, sp, q_norm_a, w_qb, q_norm, kv_norm_a, w_k, w_v, k_norm):
    cq, ckv = x[:, :MLA_Q_LORA], x[:, MLA_Q_LORA:MLA_Q_LORA + MLA_KV_LORA]
    kr = x[:, MLA_Q_LORA + MLA_KV_LORA:]
    qf = _bdot(_rms(cq, MLA_Q_LORA, q_norm_a), w_qb, "nn")
    q = jnp.concatenate([_mla_rope(_rms(qh, MLA_QK, q_norm), c, sn, sp) for qh in _heads(qf)], axis=1)
    xkv = _rms(ckv, MLA_KV_LORA, kv_norm_a)
    kf = _bdot(xkv, w_k, "nn")
    k = jnp.concatenate([_mla_rope(_rms(kh + kr, MLA_QK, k_norm), c, sn, sp) for kh in _heads(kf)], axis=1)
    return q, k, _bdot(xkv, w_v, "nn")


def _decay_fn(x, w2, b):
    la = _log_sigmoid(_bdot(x[:, :HEAD_PAD], w2, "nn") + b) * (1.0 / GLA_NORMALIZER)
    return la[:, :GLA_HEADS * GLA_DK], la[:, GLA_HEADS * GLA_DK:]


def _ret_rot_fn(q, k, rc, rs):
    def rot(x, scale):
        return jnp.concatenate([(xh * rc + _roll(xh, RET_DK // 2, 1) * rs) * scale for xh in _heads(x)], axis=1)
    return rot(q, 1.0), rot(k, RET_DK ** -0.5)


def _gla_out_fn(o_f, o_b, g, w):
    y = jnp.concatenate([_rms(oh, HEAD_PAD, w) for oh in _heads(o_f + o_b)], axis=1)
    return (y * _silu(g),)


def _ret_out_fn(o_f, o_b, g):
    y = jnp.concatenate([_rms(oh, HEAD_PAD) for oh in _heads(o_f + o_b)], axis=1)
    return (y * _silu(g),)


def _merge_fn(z0, z1, z2, g0a, g0b, g1a, g1b, g2a, g2b, bg):
    out = 0.0
    for n, (z, ga, gb) in enumerate(((z0, g0a, g0b), (z1, g1a, g1b), (z2, g2a, g2b))):
        out = out + jax.nn.sigmoid(jnp.concatenate([ga, gb], axis=1) + bg[n:n + 1]) * z
    return (out,)


def _ffn_act_fn(c, up):
    return (_gelu_tanh(c) * up,)


def _layer(l, h, mod, w, tabs, tc):
    nct = tc // ROW_TILE
    tag = f"_l{l}"
    row = lambda a: a[l][None]
    a = rowwise("norm1" + tag, _norm_mod_fn(0, 1), [h], [mod], [row(w["norm1_w"])], [D], nct)[0]
    pieces = linear_pieces(a, _pad_in_proj(w["w_in"][l]), "in_proj" + tag)
    piece = lambda start: pieces[start // PIECE_W]

    w_qb = _pad_last(w["mla_w_qb"][l].reshape(MLA_Q_LORA, MLA_HEADS, MLA_QK), HEAD_PAD).reshape(MLA_Q_LORA, -1)
    w_kvb = w["mla_w_kvb"][l].reshape(MLA_KV_LORA, MLA_HEADS, MLA_NOPE + MLA_V)
    w_k = _pad_last(w_kvb[:, :, :MLA_NOPE], HEAD_PAD).reshape(MLA_KV_LORA, -1)
    w_v = _pad_last(w_kvb[:, :, MLA_NOPE:], HEAD_PAD).reshape(MLA_KV_LORA, -1)
    rope = [tabs["c"], tabs["sn"], tabs["sp"]]
    q, k, v = rowwise("mla_prep" + tag, _mla_prep_fn, [piece(P_MLAQ)] + rope, [],
                      [row(w["mla_q_norm_a"]), w_qb, _pad_last(row(w["mla_q_norm"]), HEAD_PAD),
                       row(w["mla_kv_norm_a"]), w_k, w_v, _pad_last(row(w["mla_k_norm"]), HEAD_PAD)],
                      [MLA_HEADS * HEAD_PAD] * 3, nct, diff_rows=[True, False, False, False])
    y_mla = attention(q, k, v, tc, "attn" + tag)
    wb_mla = _pad_last(w["w_branch"][l, 0].reshape(MLA_HEADS, MLA_V, D).transpose(0, 2, 1), HEAD_PAD)
    wb_mla = wb_mla.transpose(0, 2, 1).reshape(MLA_HEADS * HEAD_PAD, D)

    w2 = jnp.zeros((HEAD_PAD, 2 * GLA_HEADS * GLA_DK), F32)
    w2 = w2.at[:GLA_RANK, :GLA_HEADS * GLA_DK].set(w["gla_w_gk2"][l, 0])
    w2 = w2.at[GLA_RANK:2 * GLA_RANK, GLA_HEADS * GLA_DK:].set(w["gla_w_gk2"][l, 1])
    la_f, la_b = rowwise("gla_decay" + tag, _decay_fn, [piece(P_RANK)], [],
                         [w2, w["gla_b_gk"][l].reshape(1, -1)], [GLA_HEADS * GLA_DK] * 2, nct)
    gq, gk, gv, gg = [piece(P_GLA + n * PIECE_W) for n in range(4)]
    o_f = scan("gla", True, gq, gk, gv, la_f, tc, "gla_fw" + tag)
    o_b = scan("gla", False, gq, gk, gv, la_b, tc, "gla_bw" + tag)
    y_gla = rowwise("gla_out" + tag, _gla_out_fn, [o_f, o_b, gg], [], [row(w["gla_o_norm"])], [512], nct)[0]

    rq, rk = rowwise("ret_rot" + tag, _ret_rot_fn, [piece(P_RET), piece(P_RET + PIECE_W), tabs["rc"], tabs["rs"]],
                     [], [], [512, 512], nct, diff_rows=[True, True, False, False])
    rv, rg = piece(P_RET + 2 * PIECE_W), piece(P_RET + 3 * PIECE_W)
    rd = jnp.broadcast_to(w["ret_decay"][l][:, :, None, None], (2, RET_HEADS, 8, HEAD_PAD))
    r_f = scan("ret", True, rq, rk, rv, rd[0], tc, "ret_fw" + tag)
    r_b = scan("ret", False, rq, rk, rv, rd[1], tc, "ret_bw" + tag)
    y_ret = rowwise("ret_out" + tag, _ret_out_fn, [r_f, r_b, rg], [], [], [512], nct)[0]

    z = [linear(y_mla, wb_mla, "branch_mla" + tag), linear(y_gla, w["w_branch"][l, 1], "branch_gla" + tag),
         linear(y_ret, w["w_branch"][l, 2], "branch_ret" + tag)]
    gates = [piece(P_GATE + n * PIECE_W) for n in range(6)]
    merged = rowwise("merge" + tag, _merge_fn, z + gates, [], [_pad_rows(w["b_gate"][l], 8)], [D], nct)[0]
    y = linear(merged, w["w_out"][l], "w_out" + tag)
    h = rowwise("resid1" + tag, _resid_fn(2), [h, y], [mod], [], [D], nct)[0]

    a2 = rowwise("norm2" + tag, _norm_mod_fn(3, 4), [h], [mod], [row(w["norm2_w"])], [D], nct)[0]
    gate = linear(a2, w["w_ffn_in"][l][:, :D_FF], "ffn_gate" + tag)
    up = linear(a2, w["w_ffn_in"][l][:, D_FF:], "ffn_up" + tag)
    conv = dwconv(gate, _pad_rows(w["w_dw"][l], 8), row(w["b_dw"]), tc, "dwconv" + tag)
    u = rowwise("ffn_act" + tag, _ffn_act_fn, [conv, up], [], [], [D_FF], nct)[0]
    f = linear(u, w["w_ffn_out"][l], "ffn_out" + tag)
    return rowwise("resid2" + tag, _resid_fn(5), [h, f], [mod], [], [D], nct)[0]


def _pad_rows(a, n):
    return jnp.pad(a, [(0, n - a.shape[0])] + [(0, 0)] * (a.ndim - 1))


def local_loss(w, mod, x, ctx, target):
    tc, t = ctx.shape[0], x.shape[0]
    tabs = _position_tables(tc, t)
    h = jnp.concatenate([ctx, x], axis=0)
    for l in range(DEPTH):
        h = _layer(l, h, mod[l], w, tabs, tc)
    return loss_head(h, target, tc, "loss_head")


ADA_ROWS = 16


def ada_forward(cond_in, w_ada, b_loc):
    cols = w_ada.shape[2]

    def body(x_ref, w_ref, b_ref, o_ref):
        s = _silu(x_ref[...])
        for l in range(DEPTH):
            o_ref[l] = _dg(s, w_ref[l], "nn") + b_ref[l]

    return pl.pallas_call(
        body, name="ada_forward", out_shape=jax.ShapeDtypeStruct((DEPTH, ADA_ROWS, cols), F32),
        compiler_params=pltpu.CompilerParams(vmem_limit_bytes=VMEM_LIMIT_BYTES),
    )(cond_in, w_ada, b_loc)


def ada_backward(cond_in, g_loc, dmod_own, w_ada):
    cols = w_ada.shape[2]

    def body(x_ref, g_ref, own_ref, w_ref, gw_ref, dc_ref, gb_ref):
        x = x_ref[...]
        s = _silu(x)
        dcond = jnp.zeros((8, D), F32)
        for l in range(DEPTH):
            g_ctx = jnp.sum(g_ref[2 * l], axis=0, keepdims=True)
            g_rows = jnp.concatenate([g_ref[2 * l + 1], jnp.broadcast_to(g_ctx, (8, cols))], axis=0)
            keep = lax.broadcasted_iota(jnp.int32, (ADA_ROWS, cols), 0) <= N_DEV
            gw_ref[l] = _dg(s, jnp.where(keep, g_rows, 0.0), "tn")
            dcond = dcond + _dg(jnp.broadcast_to(g_ctx, (8, cols)), w_ref[l], "nt")
            gb_ref[l:l + 1, :] = own_ref[2 * l:2 * l + 1, :] + own_ref[2 * l + 1:2 * l + 2, :]
        xc = x[N_DEV:N_DEV + 1]
        sig = jax.nn.sigmoid(xc)
        dc_ref[...] = dcond[0:1] * (sig * (1.0 + xc * (1.0 - sig)))

    return pl.pallas_call(
        body, name="ada_backward",
        out_shape=[jax.ShapeDtypeStruct(w_ada.shape, F32), jax.ShapeDtypeStruct((1, D), F32),
                   jax.ShapeDtypeStruct((DEPTH, 6 * D), F32)],
        compiler_params=pltpu.CompilerParams(vmem_limit_bytes=VMEM_LIMIT_BYTES),
    )(cond_in, g_loc, dmod_own, w_ada)


WEIGHTS = ["c_ctx", "w_ada", "b_ada", "norm1_w", "norm2_w", "w_in", "b_gate", "mla_q_norm_a", "mla_w_qb",
           "mla_kv_norm_a", "mla_w_kvb", "mla_q_norm", "mla_k_norm", "gla_w_gk2", "gla_b_gk", "gla_o_norm",
           "ret_decay", "w_branch", "w_out", "w_ffn_in", "w_dw", "b_dw", "w_ffn_out"]
INPUTS = ["x", "c", "ctx"] + WEIGHTS + ["loss_target"] + ["m_" + n for n in WEIGHTS] + ["v_" + n for n in WEIGHTS]
BIG = {"w_in": 2, "mla_w_qb": 2, "mla_w_kvb": 2, "w_branch": 3, "w_out": 1, "w_ffn_in": 2, "w_ffn_out": 1}
SMALL_SHARDED = {"b_gate": 2, "gla_w_gk2": 3, "gla_b_gk": 2, "w_dw": 2}
SMALL = ["c_ctx", "b_ada", "norm1_w", "norm2_w", "b_gate", "mla_q_norm_a", "mla_kv_norm_a", "mla_q_norm", "mla_k_norm",
         "gla_w_gk2", "gla_b_gk", "gla_o_norm", "ret_decay", "w_dw", "b_dw"]


def _entry_rows(size, align):
    return -(-size // (PACK_W * align)) * align


def _pack(arrays, rows, dtype, align, lead=0):
    parts = []
    for a in arrays:
        head = a.shape[:lead]
        size = math.prod(a.shape[lead:])
        r = _entry_rows(size, align)
        if r * PACK_W == size:
            parts.append(a.astype(dtype).reshape(head + (r, PACK_W)))
        else:
            flat = jnp.pad(a.astype(dtype).reshape(head + (size,)), [(0, 0)] * lead + [(0, r * PACK_W - size)])
            parts.append(flat.reshape(head + (r, PACK_W)))
    used = sum(p.shape[lead] for p in parts)
    if rows > used:
        parts.append(jnp.zeros(parts[0].shape[:lead] + (rows - used, PACK_W), dtype))
    return jnp.concatenate(parts, axis=lead)


def _pack_rows(shapes, align, multiple):
    used = sum(_entry_rows(math.prod(s), align) for s in shapes)
    return -(-used // multiple) * multiple


def _unpack(pack, shapes, align):
    head = pack.shape[:-2]
    out, off = [], 0
    for shape in shapes:
        size = math.prod(shape)
        r = _entry_rows(size, align)
        block = lax.slice_in_dim(pack, off, off + r, axis=len(head))
        if r * PACK_W != size:
            block = block.reshape(head + (r * PACK_W,))[..., :size]
        out.append(block.reshape(head + tuple(shape)))
        off += r
    return out


def _join_shards(stacked, axis):
    moved = jnp.moveaxis(stacked, 0, axis)
    shape = list(moved.shape)
    return moved.reshape(shape[:axis] + [shape[axis] * shape[axis + 1]] + shape[axis + 2:])


def _split_shards(full, axis):
    shape = list(full.shape)
    split = full.reshape(shape[:axis] + [N_DEV, shape[axis] // N_DEV] + shape[axis + 1:])
    return jnp.moveaxis(split, axis, 0)


def _gather_shards(local, axes, dtype, rows_multiple, name):
    names = list(axes)
    shapes = [local[n].shape for n in names]
    rows = _pack_rows(shapes, rows_multiple, rows_multiple)
    gathered = all_gather(_pack([local[n] for n in names], rows, dtype, rows_multiple), name)
    stacked = _unpack(gathered.reshape(N_DEV, rows, PACK_W), shapes, rows_multiple)
    return {n: _join_shards(s, axes[n]).astype(F32) for n, s in zip(names, stacked)}


def kernel(*args):
    a = dict(zip(INPUTS, args))
    me = 4 * lax.axis_index("x") + 2 * lax.axis_index("y") + lax.axis_index("c")
    cols = a["w_ada"].shape[2]

    small_names = list(SMALL_SHARDED)
    small_local = [a[n].shape for n in small_names]
    first_rows = _pack_rows([a["c"].shape] + small_local, 8, 8)
    first = all_gather(_pack([a["c"]] + [a[n] for n in small_names], first_rows, F32, 8), "gather_small")
    first = _unpack(first.reshape(N_DEV, first_rows, PACK_W), [a["c"].shape] + small_local, 8)
    c_all = first[0][:, 0]

    cond_in = jnp.concatenate([c_all, a["c_ctx"][None], jnp.zeros((ADA_ROWS - N_DEV - 1, D), F32)], axis=0)
    b_loc = lax.dynamic_slice_in_dim(a["b_ada"], me * cols, cols, axis=1)[:, None, :]
    mod_loc = ada_forward(cond_in, a["w_ada"], b_loc)
    mod_all = all_gather(mod_loc.reshape(DEPTH * ADA_ROWS, cols), "gather_mod")
    mod_all = mod_all.reshape(N_DEV, DEPTH, ADA_ROWS, cols).transpose(1, 2, 0, 3).reshape(DEPTH, ADA_ROWS, 6, D)
    mod_me = lax.dynamic_index_in_dim(mod_all, me, axis=1, keepdims=False)
    mod = jnp.pad(jnp.stack([mod_all[:, N_DEV], mod_me], axis=1), ((0, 0), (0, 0), (0, 2), (0, 0)))

    w = _gather_shards(a, BIG, BF16, 16, "gather_weights")
    w.update({n: _join_shards(s, SMALL_SHARDED[n]) for n, s in zip(small_names, first[1:])})
    for n in SMALL:
        if n not in SMALL_SHARDED and n not in ("c_ctx", "b_ada"):
            w[n] = a[n]

    loss, (gw, gmod, gx) = jax.value_and_grad(local_loss, argnums=(0, 1, 2))(
        w, mod, a["x"][0], a["ctx"][0], a["loss_target"][0])
    loss = lax.psum(loss, ("x", "y", "c"))

    dmod_own = gmod[:, :, :6].reshape(2 * DEPTH, 6 * D)
    g_all = all_gather(jnp.pad(dmod_own, ((0, 8 - 2 * DEPTH), (0, 0))), "gather_dmod").reshape(N_DEV, 8, 6 * D)
    g_loc = lax.dynamic_slice_in_dim(g_all[:, :2 * DEPTH], me * cols, cols, axis=2).transpose(1, 0, 2)
    g_w_ada, g_c_ctx, g_b_ada = ada_backward(cond_in, g_loc, dmod_own, a["w_ada"])

    small_part = dict(gw, c_ctx=g_c_ctx, b_ada=g_b_ada)
    small_shapes = [a[n].shape if n not in SMALL_SHARDED else gw[n].shape for n in SMALL]
    rows = _pack_rows(small_shapes, 8, PACK_TILE)
    parts = all_gather(_pack([small_part[n] for n in SMALL], rows, F32, 8), "gather_small_grads")
    small_sum = _unpack(slab_sum(parts.reshape(N_DEV, rows, PACK_W), "sum_small_grads"), small_shapes, 8)
    g_small = {}
    for n, g in zip(SMALL, small_sum):
        if n in SMALL_SHARDED:
            ax = SMALL_SHARDED[n]
            g = lax.dynamic_slice_in_dim(g, me * a[n].shape[ax], a[n].shape[ax], axis=ax)
        g_small[n] = g

    big_rows = _pack_rows([a[n].shape for n in BIG], 16, PACK_TILE)
    slabs = _pack([_split_shards(gw[n], ax) for n, ax in BIG.items()], big_rows, BF16, 16, lead=1)
    by_core = slabs.reshape(N_CHIP, 2, big_rows, PACK_W)
    my_core = lax.axis_index("c")
    keep = lax.dynamic_index_in_dim(by_core, my_core, axis=1, keepdims=False)
    give = lax.dynamic_index_in_dim(by_core, 1 - my_core, axis=1, keepdims=False)
    pair_sum = pair_add(keep, pair_swap(give, "swap_grads"), "add_pair_grads")
    landed = chip_all_to_all(pair_sum, "scatter_grads")

    def update(names, g_slabs, rows, align, label):
        shapes = [a[n].shape for n in names]
        packs = [_pack([a[pre + n] for n in names], rows, F32, align) for pre in ("", "m_", "v_")]
        outs = adamw(g_slabs, *packs, label)
        return [dict(zip(names, _unpack(o, shapes, align))) for o in outs]

    res_big = update(list(BIG), landed, big_rows, 16, "adamw_big")
    ada_rows = _pack_rows([a["w_ada"].shape], 8, PACK_TILE)
    res_ada = update(["w_ada"], _pack([g_w_ada], ada_rows, F32, 8)[None], ada_rows, 8, "adamw_ada")
    small_rows = _pack_rows([a[n].shape for n in SMALL], 8, PACK_TILE)
    res_small = update(SMALL, _pack([g_small[n] for n in SMALL], small_rows, F32, 8)[None], small_rows, 8,
                       "adamw_small")

    outs = [loss, gx[None]]
    for k in range(4):
        merged = {**res_big[k], **res_ada[k], **res_small[k]}
        outs += [merged[n] for n in WEIGHTS]
    return tuple(outs)
```

```python
import functools
import math

import jax
import jax.numpy as jnp
import numpy as np
from jax import lax
from jax.experimental import pallas as pl
from jax.experimental.pallas import tpu as pltpu

F32 = jnp.float32
BF16 = jnp.bfloat16

N_DEV = 8
D = 1024
DEPTH = 2
GRID_W = 64
MLA_HEADS = 8
MLA_NOPE = 64
MLA_ROPE = 32
MLA_QK = 96
MLA_V = 64
MLA_Q_LORA = 256
MLA_KV_LORA = 128
GLA_HEADS = 4
GLA_DK = 128
GLA_RANK = 16
GLA_NORMALIZER = 16.0
RET_HEADS = 4
RET_DK = 128
BRANCH_W = 512
D_FF = 2816
CHUNK = 64
ROPE_THETA = 10000.0
RET_THETA = 10000.0
EPS = 1e-6
HEAD_PAD = 128
N_IN_PAD = 8192

ADAM_LR = 0.001
ADAM_B1 = 0.9
ADAM_B2 = 0.999
ADAM_EPS = 1e-08
ADAM_WD = 0.01
ADAM_STEP = 10

ROW_TILE = 256
SCAN_CHUNKS = ROW_TILE // CHUNK
VMEM_LIMIT_BYTES = 56 * 1024 * 1024
MESH = pl.DeviceIdType.MESH


def _cparams(n_axes):
    return pltpu.CompilerParams(dimension_semantics=("arbitrary",) * n_axes, vmem_limit_bytes=VMEM_LIMIT_BYTES)


def _pick(dim, cands):
    for cand in cands:
        if dim % cand == 0:
            return cand
    return dim


_DOT_DIMS = {"nn": (((1,), (0,)), ((), ())), "nt": (((1,), (1,)), ((), ())), "tn": (((0,), (0,)), ((), ()))}


def _dg(a, b, mode):
    return lax.dot_general(a.astype(BF16), b.astype(BF16), _DOT_DIMS[mode], preferred_element_type=F32)


def _bdot(a, b, mode):
    @jax.custom_vjp
    def f(a, b):
        return _dg(a, b, mode)

    def fwd(a, b):
        return _dg(a, b, mode), (a, b)

    def bwd(res, g):
        a, b = res
        if mode == "nn":
            return _dg(g, b, "nt").astype(a.dtype), _dg(a, g, "tn").astype(b.dtype)
        if mode == "nt":
            return _dg(g, b, "nn").astype(a.dtype), _dg(g, a, "tn").astype(b.dtype)
        return _dg(b, g, "nt").astype(a.dtype), _dg(a, g, "nn").astype(b.dtype)

    f.defvjp(fwd, bwd)
    return f(a, b)


def _roll(x, shift, axis):
    n = x.shape[axis]
    shift = shift % n

    @jax.custom_vjp
    def f(x):
        return pltpu.roll(x, shift, axis)

    def fwd(x):
        return pltpu.roll(x, shift, axis), None

    def bwd(_, g):
        return (pltpu.roll(g, (n - shift) % n, axis),)

    f.defvjp(fwd, bwd)
    return f(x)


@jax.custom_jvp
def _log_sigmoid(x):
    return jnp.minimum(x, 0.0) - jnp.log(1.0 + jnp.exp(-jnp.abs(x)))


@_log_sigmoid.defjvp
def _log_sigmoid_jvp(primals, tangents):
    (x,), (t,) = primals, tangents
    return _log_sigmoid(x), t * jax.nn.sigmoid(-x)


def _rms(x, n, w=None):
    y = x * lax.rsqrt(jnp.sum(x * x, axis=-1, keepdims=True) * (1.0 / n) + EPS)
    return y if w is None else y * w


def _silu(x):
    return x * jax.nn.sigmoid(x)


def _gelu_tanh(x):
    return 0.5 * x * (1.0 + jnp.tanh(math.sqrt(2.0 / math.pi) * (x + 0.044715 * (x * x * x))))


def _mm(a, b, mode, name):
    if mode == "nn":
        (m, k), (_, n) = a.shape, b.shape
    elif mode == "nt":
        (m, k), (n, _) = a.shape, b.shape
    else:
        (k, m), (_, n) = a.shape, b.shape
    tm = _pick(m, (1024, 768, 1408, 512, 256, 128))
    tn = _pick(n, (1024, 1408, 512, 256, 128))
    tk = _pick(k, (1024, 768, 1408, 512, 256, 128))
    nk = k // tk
    if mode == "nn":
        a_spec = pl.BlockSpec((tm, tk), lambda i, j, kk: (i, kk))
        b_spec = pl.BlockSpec((tk, tn), lambda i, j, kk: (kk, j))
    elif mode == "nt":
        a_spec = pl.BlockSpec((tm, tk), lambda i, j, kk: (i, kk))
        b_spec = pl.BlockSpec((tn, tk), lambda i, j, kk: (j, kk))
    else:
        a_spec = pl.BlockSpec((tk, tm), lambda i, j, kk: (kk, i))
        b_spec = pl.BlockSpec((tk, tn), lambda i, j, kk: (kk, j))

    def body(a_ref, b_ref, o_ref):
        kk = pl.program_id(2)
        part = _dg(a_ref[...], b_ref[...], mode)
        if nk == 1:
            o_ref[...] = part
        else:
            @pl.when(kk == 0)
            def _():
                o_ref[...] = part

            @pl.when(kk != 0)
            def _():
                o_ref[...] += part

    return pl.pallas_call(
        body, name=name, grid=(m // tm, n // tn, nk),
        in_specs=[a_spec, b_spec], out_specs=pl.BlockSpec((tm, tn), lambda i, j, kk: (i, j)),
        out_shape=jax.ShapeDtypeStruct((m, n), F32),
        compiler_params=_cparams(3),
    )(a, b)


def linear(x, w, name):
    @jax.custom_vjp
    def op(x, w):
        return _mm(x, w.astype(BF16), "nn", name + "_f")

    def fwd(x, w):
        wb = w.astype(BF16)
        return _mm(x, wb, "nn", name + "_f"), (x, wb)

    def bwd(res, g):
        x, wb = res
        return _mm(g, wb, "nt", name + "_dx"), _mm(x, g, "tn", name + "_dw")

    op.defvjp(fwd, bwd)
    return op(x, w)


PIECE_W = 512
PIECE_ROWS = 384


def _mm_split(a, wb, name):
    (r, k), n = a.shape, wb.shape[1] // PIECE_W
    tm = _pick(r, (PIECE_ROWS, ROW_TILE))
    width = PIECE_GROUP * PIECE_W

    n_tiles = r // tm

    def body(a_ref, w_ref, *out_refs):
        j = pl.program_id(0)
        res = _dg(a_ref[...], w_ref[...], "nn")
        for group in range(n // PIECE_GROUP):
            @pl.when(j == group)
            def _(group=group):
                for p in range(PIECE_GROUP):
                    out_refs[group * PIECE_GROUP + p][...] = res[:, p * PIECE_W:(p + 1) * PIECE_W]

    def out_spec(jj):
        group = jj // PIECE_GROUP
        return pl.BlockSpec((tm, PIECE_W),
                            lambda j, i: (jnp.where(j == group, i, jnp.where(j < group, 0, n_tiles - 1)), 0))

    return pl.pallas_call(
        body, name=name, grid=(n // PIECE_GROUP, n_tiles),
        in_specs=[pl.BlockSpec((tm, k), lambda j, i: (i, 0)), pl.BlockSpec((k, width), lambda j, i: (0, j))],
        out_specs=[out_spec(jj) for jj in range(n)],
        out_shape=[jax.ShapeDtypeStruct((r, PIECE_W), F32)] * n, compiler_params=_cparams(2),
    )(a, wb)


def _mm_join(gs, wb, name):
    n, (r, _), k = len(gs), gs[0].shape, wb.shape[0]
    tm = _pick(r, (PIECE_ROWS, ROW_TILE))
    n_groups = n // PIECE_GROUP

    def body(*refs):
        g_refs, w_ref, o_ref = refs[:n], refs[n], refs[n + 1]
        j = pl.program_id(1)
        for group in range(n_groups):
            @pl.when(j == group)
            def _(group=group):
                g = jnp.concatenate([g_refs[group * PIECE_GROUP + p][...].astype(BF16) for p in range(PIECE_GROUP)],
                                    axis=1)
                part = _dg(g, w_ref[...], "nt")
                if group == 0:
                    o_ref[...] = part
                else:
                    o_ref[...] += part

    return pl.pallas_call(
        body, name=name, grid=(r // tm, n_groups),
        in_specs=[pl.BlockSpec((tm, PIECE_W), lambda i, j: (i, 0))] * n
        + [pl.BlockSpec((k, PIECE_GROUP * PIECE_W), lambda i, j: (0, j))],
        out_specs=pl.BlockSpec((tm, k), lambda i, j: (i, 0)), out_shape=jax.ShapeDtypeStruct((r, k), F32),
        compiler_params=_cparams(2),
    )(*gs, wb)


PIECE_GROUP = 4


def _mm_join_tn(a, gs, name):
    n, (r, k) = len(gs), a.shape
    tk = _pick(r, (PIECE_ROWS, ROW_TILE))
    nk = r // tk
    width = PIECE_GROUP * PIECE_W

    def body(*refs):
        a_ref, g_refs, o_ref = refs[0], refs[1:n + 1], refs[n + 1]
        j, kk = pl.program_id(0), pl.program_id(1)

        @pl.when(kk == 0)
        def _():
            o_ref[...] = jnp.zeros_like(o_ref)

        av = a_ref[...].astype(BF16)
        for group in range(n // PIECE_GROUP):
            @pl.when(j == group)
            def _(group=group):
                g = jnp.concatenate([g_refs[group * PIECE_GROUP + p][...].astype(BF16) for p in range(PIECE_GROUP)],
                                    axis=1)
                o_ref[...] += _dg(av, g, "tn")

    def g_spec(jj):
        return pl.BlockSpec((tk, PIECE_W), lambda j, kk: (jnp.where(j == jj // PIECE_GROUP, kk, 0), 0))

    return pl.pallas_call(
        body, name=name, grid=(n // PIECE_GROUP, nk),
        in_specs=[pl.BlockSpec((tk, k), lambda j, kk: (kk, 0))] + [g_spec(jj) for jj in range(n)],
        out_specs=pl.BlockSpec((k, width), lambda j, kk: (0, j)),
        out_shape=jax.ShapeDtypeStruct((k, n * PIECE_W), F32), compiler_params=_cparams(2),
    )(a, *gs)


def linear_pieces(x, w, name):
    @jax.custom_vjp
    def op(x, w):
        return tuple(_mm_split(x, w.astype(BF16), name + "_f"))

    def fwd(x, w):
        wb = w.astype(BF16)
        return tuple(_mm_split(x, wb, name + "_f")), (x, wb)

    def bwd(res, gs):
        x, wb = res
        return _mm_join(list(gs), wb, name + "_dx"), _mm_join_tn(x, list(gs), name + "_dw")

    op.defvjp(fwd, bwd)
    return op(x, w)


def rowwise(name, fn, rows, segs, params, out_widths, nct, diff_rows=None):
    n_row, n_seg, n_par, n_out = len(rows), len(segs), len(params), len(out_widths)
    diff_rows = [True] * n_row if diff_rows is None else list(diff_rows)
    r_total = rows[0].shape[0]
    n_tiles = r_total // ROW_TILE

    def seg_of(i):
        return jnp.where(i < nct, 0, 1)

    def row_spec(width):
        return pl.BlockSpec((ROW_TILE, width), lambda i: (i, 0))

    def seg_spec(shape):
        nd = len(shape)
        return pl.BlockSpec((1,) + tuple(shape[1:]), lambda i: (seg_of(i),) + (0,) * (nd - 1))

    def par_spec(shape):
        nd = len(shape)
        return pl.BlockSpec(tuple(shape), lambda i: (0,) * nd)

    in_specs = ([row_spec(r.shape[1]) for r in rows] + [seg_spec(s.shape) for s in segs]
                + [par_spec(p.shape) for p in params])

    def load(refs):
        vals = [r[...].astype(F32) for r in refs[:n_row]]
        vals += [r[0].astype(F32) for r in refs[n_row:n_row + n_seg]]
        vals += [r[...].astype(F32) for r in refs[n_row + n_seg:n_row + n_seg + n_par]]
        return vals

    def fwd_call(arrs):
        def body(*refs):
            outs = fn(*load(refs))
            for o_ref, val in zip(refs[n_row + n_seg + n_par:], outs):
                o_ref[...] = val

        return pl.pallas_call(
            body, name=name + "_f", grid=(n_tiles,), in_specs=in_specs,
            out_specs=[row_spec(w) for w in out_widths],
            out_shape=[jax.ShapeDtypeStruct((r_total, w), F32) for w in out_widths],
            compiler_params=_cparams(1),
        )(*arrs)

    d_idx = [k for k in range(n_row) if diff_rows[k]]

    def bwd_call(arrs, douts):
        n_in = n_row + n_seg + n_par

        def body(*refs):
            i = pl.program_id(0)
            vals = load(refs[:n_in])
            gs = [r[...] for r in refs[n_in:n_in + n_out]]
            out_refs = refs[n_in + n_out:]
            diff_pos = d_idx + list(range(n_row, n_in))

            def f(*dv):
                full = list(vals)
                for pos, v in zip(diff_pos, dv):
                    full[pos] = v
                return tuple(fn(*full))

            _, vjp = jax.vjp(f, *[vals[p] for p in diff_pos])
            grads = vjp(tuple(gs))
            nd = len(d_idx)
            for o_ref, g in zip(out_refs[:nd], grads[:nd]):
                o_ref[...] = g
            first_seg = jnp.logical_or(i == 0, i == nct)
            for o_ref, g in zip(out_refs[nd:nd + n_seg], grads[nd:nd + n_seg]):
                @pl.when(first_seg)
                def _(o_ref=o_ref, g=g):
                    o_ref[0] = g

                @pl.when(jnp.logical_not(first_seg))
                def _(o_ref=o_ref, g=g):
                    o_ref[0] += g
            for o_ref, g in zip(out_refs[nd + n_seg:], grads[nd + n_seg:]):
                @pl.when(i == 0)
                def _(o_ref=o_ref, g=g):
                    o_ref[...] = g

                @pl.when(i != 0)
                def _(o_ref=o_ref, g=g):
                    o_ref[...] += g

        out_specs = ([row_spec(rows[k].shape[1]) for k in d_idx] + [seg_spec(s.shape) for s in segs]
                     + [par_spec(p.shape) for p in params])
        out_shape = ([jax.ShapeDtypeStruct(rows[k].shape, F32) for k in d_idx]
                     + [jax.ShapeDtypeStruct(s.shape, F32) for s in segs]
                     + [jax.ShapeDtypeStruct(p.shape, F32) for p in params])
        return pl.pallas_call(
            body, name=name + "_b", grid=(n_tiles,),
            in_specs=in_specs + [row_spec(w) for w in out_widths],
            out_specs=out_specs, out_shape=out_shape, compiler_params=_cparams(1),
        )(*arrs, *douts)

    @jax.custom_vjp
    def op(*arrs):
        return tuple(fwd_call(arrs))

    def op_fwd(*arrs):
        return tuple(fwd_call(arrs)), arrs

    def op_bwd(arrs, douts):
        grads = list(bwd_call(arrs, douts))
        nd = len(d_idx)
        row_grads = [jnp.zeros_like(arrs[k]) for k in range(n_row)]
        for k, g in zip(d_idx, grads[:nd]):
            row_grads[k] = g
        return tuple(row_grads + grads[nd:])

    op.defvjp(op_fwd, op_bwd)
    return op(*rows, *segs, *params)


ATT_SCALE = MLA_QK ** -0.5
LOG2E = math.log2(math.e)
ATT_KEY_CHUNKS = (768, 512, 256)


ATT_LATENT_TILE = 1024


def _query_rows_spec(row0, tq):
    return pl.BlockSpec((pl.Element(tq), pl.Element(HEAD_PAD)),
                        lambda h, i: (pl.multiple_of(row0 + i * tq, ROW_TILE), pl.multiple_of(h * HEAD_PAD, HEAD_PAD)))


def _key_chunks(nk):
    kc = _pick(nk, ATT_KEY_CHUNKS)
    return [(c * kc, kc) for c in range(nk // kc)]


def _attn_fwd_call(q, k, v, row0, n_rows, tq, nk, name):
    def body(q_ref, k_ref, v_ref, o_ref, lse_ref):
        qv = q_ref[...]
        m = jnp.full((tq, 1), -jnp.inf, F32)
        l = jnp.zeros((tq, 1), F32)
        acc = jnp.zeros((tq, HEAD_PAD), F32)
        for start, size in _key_chunks(nk):
            s = lax.dot_general(qv, k_ref[start:start + size, :], _DOT_DIMS["nt"], preferred_element_type=F32)
            m_new = jnp.maximum(m, jnp.max(s, axis=-1, keepdims=True))
            alpha = jnp.exp2(m - m_new)
            p = jnp.exp2(s - m_new)
            l = alpha * l + jnp.sum(p, axis=-1, keepdims=True)
            acc = alpha * acc + lax.dot_general(p.astype(BF16), v_ref[start:start + size, :], _DOT_DIMS["nn"],
                                                preferred_element_type=F32)
            m = m_new
        o_ref[...] = acc / l
        lse_ref[...] = jnp.broadcast_to(m + jnp.log2(l), (tq, HEAD_PAD))

    out_spec = pl.BlockSpec((tq, HEAD_PAD), lambda h, i: (i, h))
    kv_spec = pl.BlockSpec((nk, HEAD_PAD), lambda h, i: (0, h))
    out = jax.ShapeDtypeStruct((n_rows, q.shape[1]), F32)
    return pl.pallas_call(
        body, name=name, grid=(MLA_HEADS, n_rows // tq), in_specs=[_query_rows_spec(row0, tq), kv_spec, kv_spec],
        out_specs=[out_spec, out_spec], out_shape=[out, out], compiler_params=_cparams(2),
    )(q, k, v)


def _attn_bwd_call(q, k, v, o, lse, do, row0, n_rows, tq, nk, name):
    nq = n_rows // tq

    def body(q_ref, k_ref, v_ref, o_ref, lse_ref, do_ref, dq_ref, dk_ref, dv_ref):
        i = pl.program_id(1)

        @pl.when(i == 0)
        def _():
            dk_ref[...] = jnp.zeros_like(dk_ref)
            dv_ref[...] = jnp.zeros_like(dv_ref)

        qv = q_ref[...]
        dov = do_ref[...]
        dob = dov.astype(BF16)
        lse = lse_ref[:, 0:1]
        delta = jnp.sum(dov * o_ref[...], axis=-1, keepdims=True)
        dq = jnp.zeros((tq, HEAD_PAD), F32)
        for start, size in _key_chunks(nk):
            kk = k_ref[start:start + size, :]
            vv = v_ref[start:start + size, :]
            s = lax.dot_general(qv, kk, _DOT_DIMS["nt"], preferred_element_type=F32)
            p = jnp.exp2(s - lse)
            dp = lax.dot_general(dob, vv, _DOT_DIMS["nt"], preferred_element_type=F32)
            g = (p * (dp - delta)).astype(BF16)
            dk_ref[start:start + size, :] += lax.dot_general(g, qv, _DOT_DIMS["tn"], preferred_element_type=F32)
            dv_ref[start:start + size, :] += lax.dot_general(p.astype(BF16), dob, _DOT_DIMS["tn"],
                                                             preferred_element_type=F32)
            dq = dq + lax.dot_general(g, kk, _DOT_DIMS["nn"], preferred_element_type=F32)
        dq_ref[...] = dq * ATT_SCALE

        @pl.when(i == nq - 1)
        def _():
            dk_ref[...] = dk_ref[...] * (1.0 / LOG2E)

    own_spec = pl.BlockSpec((tq, HEAD_PAD), lambda h, i: (i, h))
    kv_spec = pl.BlockSpec((nk, HEAD_PAD), lambda h, i: (0, h))
    rows_spec = _query_rows_spec(row0, tq)
    return pl.pallas_call(
        body, name=name, grid=(MLA_HEADS, nq),
        in_specs=[rows_spec, kv_spec, kv_spec, own_spec, own_spec, rows_spec],
        out_specs=[own_spec, kv_spec, kv_spec],
        out_shape=[jax.ShapeDtypeStruct((n_rows, q.shape[1]), F32), jax.ShapeDtypeStruct((nk, q.shape[1]), F32),
                   jax.ShapeDtypeStruct((nk, q.shape[1]), F32)],
        compiler_params=_cparams(2),
    )(q, k, v, o, lse, do)


def attention(q, k, v, tc, name):
    r_total = q.shape[0]
    tq_lat = _pick(r_total - tc, (ATT_LATENT_TILE, ROW_TILE))
    ranges = [(0, tc, ROW_TILE, tc, "_ctx"), (tc, r_total - tc, tq_lat, r_total, "_lat")]

    def operands(q, k, v):
        return (q * (ATT_SCALE * LOG2E)).astype(BF16), k.astype(BF16), v.astype(BF16)

    def forward(qb, kb, vb):
        return [_attn_fwd_call(qb, kb, vb, row0, n_rows, tq, nk, name + tag + "_f")
                for row0, n_rows, tq, nk, tag in ranges]

    @jax.custom_vjp
    def op(q, k, v):
        return jnp.concatenate([o for o, _ in forward(*operands(q, k, v))], axis=0)

    def fwd(q, k, v):
        qb, kb, vb = operands(q, k, v)
        parts = forward(qb, kb, vb)
        return jnp.concatenate([o for o, _ in parts], axis=0), (qb, kb, vb, parts)

    def bwd(res, do):
        qb, kb, vb, parts = res
        (dq_c, dk_c, dv_c), (dq_l, dk_l, dv_l) = [
            _attn_bwd_call(qb, kb, vb, o, lse, do, row0, n_rows, tq, nk, name + tag + "_b")
            for (o, lse), (row0, n_rows, tq, nk, tag) in zip(parts, ranges)]
        grow = lambda part: jnp.pad(part, ((0, r_total - tc), (0, 0)))
        return jnp.concatenate([dq_c, dq_l], axis=0), dk_l + grow(dk_c), dv_l + grow(dv_c)

    op.defvjp(fwd, bwd)
    return op(q, k, v)


CHUNK_SHIFT = CHUNK.bit_length() - 1


def _block_pairs():
    rows = lax.broadcasted_iota(jnp.int32, (ROW_TILE, ROW_TILE), 0)
    cols = lax.broadcasted_iota(jnp.int32, (ROW_TILE, ROW_TILE), 1)
    same = lax.shift_right_logical(rows, CHUNK_SHIFT) == lax.shift_right_logical(cols, CHUNK_SHIFT)
    return rows, cols, same


def _block_mask(kind):
    rows, cols, same = _block_pairs()
    order = {"lower_incl": rows >= cols, "upper_incl": rows <= cols, "lower_strict": rows > cols,
             "upper_strict": rows < cols}[kind]
    return jnp.logical_and(same, order)


def _row_chunk():
    return lax.shift_right_logical(lax.broadcasted_iota(jnp.int32, (ROW_TILE, 1), 0), CHUNK_SHIFT)


def _dot01(kind, x):
    m = _block_mask(kind).astype(BF16)
    hi = x.astype(BF16)
    rest = x - hi.astype(F32)
    mid = rest.astype(BF16)
    lo = (rest - mid.astype(F32)).astype(BF16)
    terms = jnp.concatenate([hi, mid, lo], axis=1)
    out = lax.dot_general(m, terms, _DOT_DIMS["nn"], preferred_element_type=F32)
    n = x.shape[1]
    return out[:, :n] + out[:, n:2 * n] + out[:, 2 * n:]


def _chunk_sums(x, forward):
    kinds = ("lower_incl", "upper_strict") if forward else ("upper_incl", "lower_strict")
    transposed = ("upper_incl", "lower_strict") if forward else ("lower_incl", "upper_strict")

    @jax.custom_vjp
    def f(x):
        return _dot01(kinds[0], x), _dot01(kinds[1], x)

    def fwd(x):
        return (_dot01(kinds[0], x), _dot01(kinds[1], x)), None

    def bwd(_, g):
        return (_dot01(transposed[0], g[0]) + _dot01(transposed[1], g[1]),)

    f.defvjp(fwd, bwd)
    return f(x)


def _scan_order(forward):
    if forward:
        return list(range(SCAN_CHUNKS)), lambda c: c * CHUNK + CHUNK - 1
    return list(range(SCAN_CHUNKS - 1, -1, -1)), lambda c: c * CHUNK


def _carry_states(forward, st0, inc_all, decay_of):
    order, _ = _scan_order(forward)
    entering = [None] * SCAN_CHUNKS
    st = st0
    for c in order:
        entering[c] = st
        st = st * decay_of(c) + inc_all[:, c * HEAD_PAD:(c + 1) * HEAD_PAD]
    return jnp.concatenate(entering, axis=0), st


def _per_chunk_lanes(x):
    chunk = _row_chunk()
    return jnp.concatenate([jnp.where(chunk == c, x, 0.0) for c in range(SCAN_CHUNKS)], axis=1)


def _own_chunk_lanes(x4):
    chunk = _row_chunk()
    n = x4.shape[1] // SCAN_CHUNKS
    out = jnp.where(chunk == 0, x4[:, :n], 0.0)
    for c in range(1, SCAN_CHUNKS):
        out = out + jnp.where(chunk == c, x4[:, c * n:(c + 1) * n], 0.0)
    return out


def _gla_block(forward, q, k, v, la, st0):
    cum, after = _chunk_sums(la, forward)
    _, last_row = _scan_order(forward)
    q_dec = q * (jnp.exp(cum) * (GLA_DK ** -0.5))
    att = _bdot(q_dec, k * jnp.exp(-cum), "nt")
    att = jnp.where(_block_mask("lower_incl" if forward else "upper_strict"), att, 0.0)
    inc_all = _bdot(v, _per_chunk_lanes(k * jnp.exp(after)), "tn")
    entering, st1 = _carry_states(forward, st0, inc_all,
                                  lambda c: jnp.exp(cum[last_row(c):last_row(c) + 1, :]))
    o = _bdot(att, v, "nn") + _own_chunk_lanes(_bdot(q_dec, entering, "nt"))
    return o, st1


def _ret_block(forward, q, k, v, rd, st0):
    lg = -jnp.exp(rd[0:1, 0:1])
    rows, cols, _ = _block_pairs()
    pos = jnp.bitwise_and(lax.broadcasted_iota(jnp.int32, (ROW_TILE, 1), 0), CHUNK - 1).astype(F32)
    if forward:
        to_end, from_start, rel = CHUNK - 1.0 - pos, pos + 1.0, (rows - cols).astype(F32)
    else:
        to_end, from_start, rel = pos, CHUNK - pos, (cols - rows).astype(F32)
    mask = _block_mask("lower_incl" if forward else "upper_strict")
    dmat = jnp.where(mask, jnp.exp(jnp.where(mask, rel, 0.0) * lg), 0.0)
    att = _bdot(q, k, "nt") * dmat
    inc_all = _bdot(v, _per_chunk_lanes(k * jnp.exp(to_end * lg)), "tn")
    entering, st1 = _carry_states(forward, st0, inc_all, lambda c: jnp.exp(CHUNK * lg))
    o = _bdot(att, v, "nn") + _own_chunk_lanes(_bdot(q, entering, "nt")) * jnp.exp(from_start * lg)
    return o, st1


def scan(kind, forward, q, k, v, aux, tc, name):
    heads = q.shape[1] // HEAD_PAD
    r_total = q.shape[0]
    nblk = r_total // ROW_TILE
    nctb = tc // ROW_TILE
    block_fn = functools.partial(_gla_block if kind == "gla" else _ret_block, forward)
    per_row_aux = kind == "gla"

    def blk(g):
        if forward:
            return g
        return jnp.where(g < nctb, nctb - 1 - g, nblk - 1 - (g - nctb))

    def specs(step_to_g):
        row = pl.BlockSpec((ROW_TILE, heads * HEAD_PAD), lambda s: (blk(step_to_g(s)), 0))
        aux_spec = row if per_row_aux else pl.BlockSpec((heads, 8, HEAD_PAD), lambda s: (0, 0, 0))
        st = pl.BlockSpec((1, heads, HEAD_PAD, HEAD_PAD), lambda s: (step_to_g(s), 0, 0, 0))
        return row, aux_spec, st

    def head_cols(h):
        return slice(h * HEAD_PAD, (h + 1) * HEAD_PAD)

    def fwd_call(q, k, v, aux):
        row, aux_spec, st_spec = specs(lambda s: s)

        def body(q_ref, k_ref, v_ref, a_ref, o_ref, st0_ref, st_ref):
            @pl.when(pl.program_id(0) == 0)
            def _():
                st_ref[...] = jnp.zeros_like(st_ref)

            qv, kv, vv = q_ref[...], k_ref[...], v_ref[...]
            outs = []
            for h in range(heads):
                st0 = st_ref[h]
                st0_ref[0, h] = st0
                a = a_ref[:, head_cols(h)] if per_row_aux else a_ref[h]
                o, st1 = block_fn(qv[:, head_cols(h)], kv[:, head_cols(h)], vv[:, head_cols(h)], a, st0)
                outs.append(o)
                st_ref[h] = st1
            o_ref[...] = jnp.concatenate(outs, axis=1)

        return pl.pallas_call(
            body, name=name + "_f", grid=(nblk,), in_specs=[row, row, row, aux_spec],
            out_specs=[row, st_spec],
            out_shape=[jax.ShapeDtypeStruct(q.shape, F32),
                       jax.ShapeDtypeStruct((nblk, heads, HEAD_PAD, HEAD_PAD), F32)],
            scratch_shapes=[pltpu.VMEM((heads, HEAD_PAD, HEAD_PAD), F32)],
            compiler_params=_cparams(1),
        )(q, k, v, aux)

    def bwd_call(q, k, v, aux, st0s, do):
        row, aux_spec, st_spec = specs(lambda s: nblk - 1 - s)

        def body(q_ref, k_ref, v_ref, a_ref, st0_ref, do_ref, dq_ref, dk_ref, dv_ref, da_ref, dst_ref):
            s = pl.program_id(0)

            @pl.when(s == 0)
            def _():
                dst_ref[...] = jnp.zeros_like(dst_ref)

            qv, kv, vv, dov = q_ref[...], k_ref[...], v_ref[...], do_ref[...]
            grads = []
            for h in range(heads):
                a = a_ref[:, head_cols(h)] if per_row_aux else a_ref[h]
                _, vjp = jax.vjp(block_fn, qv[:, head_cols(h)], kv[:, head_cols(h)], vv[:, head_cols(h)], a,
                                 st0_ref[0, h])
                dq, dk, dv, da, dst0 = vjp((dov[:, head_cols(h)], dst_ref[h]))
                dst_ref[h] = dst0
                grads.append((dq, dk, dv, da))
            dq_ref[...] = jnp.concatenate([g[0] for g in grads], axis=1)
            dk_ref[...] = jnp.concatenate([g[1] for g in grads], axis=1)
            dv_ref[...] = jnp.concatenate([g[2] for g in grads], axis=1)
            if per_row_aux:
                da_ref[...] = jnp.concatenate([g[3] for g in grads], axis=1)
            else:
                da = jnp.stack([g[3] for g in grads], axis=0)

                @pl.when(s == 0)
                def _():
                    da_ref[...] = da

                @pl.when(s != 0)
                def _():
                    da_ref[...] += da

        return pl.pallas_call(
            body, name=name + "_b", grid=(nblk,),
            in_specs=[row, row, row, aux_spec, st_spec, row],
            out_specs=[row, row, row, aux_spec],
            out_shape=[jax.ShapeDtypeStruct(q.shape, F32)] * 3 + [jax.ShapeDtypeStruct(aux.shape, F32)],
            scratch_shapes=[pltpu.VMEM((heads, HEAD_PAD, HEAD_PAD), F32)],
            compiler_params=_cparams(1),
        )(q, k, v, aux, st0s, do)

    @jax.custom_vjp
    def op(q, k, v, aux):
        return fwd_call(q, k, v, aux)[0]

    def fwd(q, k, v, aux):
        o, st0s = fwd_call(q, k, v, aux)
        return o, (q, k, v, aux, st0s)

    def bwd(res, do):
        return tuple(bwd_call(*res, do))

    op.defvjp(fwd, bwd)
    return op(q, k, v, aux)


HALO = 8


def _neighbours(main, prev8, next8, i, nct, n_tiles):
    has_prev = jnp.logical_and(i != 0, i != nct).astype(F32)
    has_next = jnp.logical_and(i != nct - 1, i != n_tiles - 1).astype(F32)
    row = lax.broadcasted_iota(jnp.int32, main.shape, 0)
    down = jnp.where(row == 0, prev8[HALO - 1:HALO] * has_prev, pltpu.roll(main, 1, 0))
    up = jnp.where(row == ROW_TILE - 1, next8[0:1] * has_next, pltpu.roll(main, ROW_TILE - 1, 0))
    return down, up


def dwconv(x, w8, b, tc, name):
    r_total, width = x.shape
    n_tiles = r_total // ROW_TILE
    nct = tc // ROW_TILE
    per = ROW_TILE // HALO
    main_spec = pl.BlockSpec((ROW_TILE, width), lambda i: (i, 0))
    prev_spec = pl.BlockSpec((HALO, width), lambda i: (jnp.maximum(i * per - 1, 0), 0))
    next_spec = pl.BlockSpec((HALO, width), lambda i: (jnp.minimum((i + 1) * per, r_total // HALO - 1), 0))
    w_spec = pl.BlockSpec((8, width), lambda i: (0, 0))
    b_spec = pl.BlockSpec((1, width), lambda i: (0, 0))

    def fwd_call(x, w8, b):
        def body(x_ref, p_ref, n_ref, w_ref, b_ref, o_ref):
            xv = x_ref[...]
            down, up = _neighbours(xv, p_ref[...], n_ref[...], pl.program_id(0), nct, n_tiles)
            o_ref[...] = w_ref[0:1] * down + w_ref[1:2] * xv + w_ref[2:3] * up + b_ref[...]

        return pl.pallas_call(
            body, name=name + "_f", grid=(n_tiles,), in_specs=[main_spec, prev_spec, next_spec, w_spec, b_spec],
            out_specs=main_spec, out_shape=jax.ShapeDtypeStruct(x.shape, F32), compiler_params=_cparams(1),
        )(x, x, x, w8, b)

    def bwd_call(x, w8, g):
        def body(x_ref, xp_ref, xn_ref, g_ref, gp_ref, gn_ref, w_ref, dx_ref, dw_ref, db_ref):
            i = pl.program_id(0)
            xv, gv = x_ref[...], g_ref[...]
            x_down, x_up = _neighbours(xv, xp_ref[...], xn_ref[...], i, nct, n_tiles)
            g_down, g_up = _neighbours(gv, gp_ref[...], gn_ref[...], i, nct, n_tiles)
            dx_ref[...] = w_ref[0:1] * g_up + w_ref[1:2] * gv + w_ref[2:3] * g_down
            dw = jnp.concatenate([jnp.sum(gv * x_down, axis=0, keepdims=True),
                                  jnp.sum(gv * xv, axis=0, keepdims=True),
                                  jnp.sum(gv * x_up, axis=0, keepdims=True),
                                  jnp.zeros((5, width), F32)], axis=0)
            db = jnp.sum(gv, axis=0, keepdims=True)

            @pl.when(i == 0)
            def _():
                dw_ref[...] = dw
                db_ref[...] = db

            @pl.when(i != 0)
            def _():
                dw_ref[...] += dw
                db_ref[...] += db

        return pl.pallas_call(
            body, name=name + "_b", grid=(n_tiles,),
            in_specs=[main_spec, prev_spec, next_spec, main_spec, prev_spec, next_spec, w_spec],
            out_specs=[main_spec, w_spec, b_spec],
            out_shape=[jax.ShapeDtypeStruct(x.shape, F32), jax.ShapeDtypeStruct((8, width), F32),
                       jax.ShapeDtypeStruct((1, width), F32)],
            compiler_params=_cparams(1),
        )(x, x, x, g, g, g, w8)

    @jax.custom_vjp
    def op(x, w8, b):
        return fwd_call(x, w8, b)

    def fwd(x, w8, b):
        return fwd_call(x, w8, b), (x, w8)

    def bwd(res, g):
        return tuple(bwd_call(*res, g))

    op.defvjp(fwd, bwd)
    return op(x, w8, b)


def loss_head(h, target, tc, name):
    r_total, width = h.shape
    n_tiles = r_total // ROW_TILE
    nct = tc // ROW_TILE

    def call(h, target):
        def body(h_ref, t_ref, dh_ref, loss_ref, acc_ref):
            i = pl.program_id(0)

            @pl.when(i == 0)
            def _():
                acc_ref[...] = jnp.zeros_like(acc_ref)

            @pl.when(i < nct)
            def _():
                dh_ref[...] = jnp.zeros_like(dh_ref)

            @pl.when(i >= nct)
            def _():
                err = h_ref[...] - t_ref[...]
                dh_ref[...] = err * (1.0 / width)
                acc_ref[...] += jnp.sum((err * err).reshape(ROW_TILE // 8, 8, width), axis=0)

            @pl.when(i == n_tiles - 1)
            def _():
                loss_ref[...] = jnp.sum(acc_ref[...]).reshape(1, 1) * (0.5 / width)

        row = pl.BlockSpec((ROW_TILE, width), lambda i: (i, 0))
        return pl.pallas_call(
            body, name=name, grid=(n_tiles,),
            in_specs=[row, pl.BlockSpec((ROW_TILE, width), lambda i: (jnp.maximum(i - nct, 0), 0))],
            out_specs=[row, pl.BlockSpec((1, 1), lambda i: (0, 0))],
            out_shape=[jax.ShapeDtypeStruct(h.shape, F32), jax.ShapeDtypeStruct((1, 1), F32)],
            scratch_shapes=[pltpu.VMEM((8, width), F32)], compiler_params=_cparams(1),
        )(h, target)

    @jax.custom_vjp
    def op(h, target):
        return call(h, target)[1][0, 0]

    def fwd(h, target):
        dh, loss = call(h, target)
        return loss[0, 0], (dh, target)

    def bwd(res, g):
        dh, target = res
        return dh * g, jnp.zeros_like(target)

    op.defvjp(fwd, bwd)
    return op(h, target)


PACK_W = 1024
PACK_TILE = 128


def slab_sum(slabs, name):
    n_slab, n, _ = slabs.shape

    def body(s_ref, o_ref):
        acc = s_ref[0]
        for j in range(1, n_slab):
            acc = acc + s_ref[j]
        o_ref[...] = acc

    return pl.pallas_call(
        body, name=name, grid=(n // PACK_TILE,),
        in_specs=[pl.BlockSpec((n_slab, PACK_TILE, PACK_W), lambda i: (0, i, 0))],
        out_specs=pl.BlockSpec((PACK_TILE, PACK_W), lambda i: (i, 0)),
        out_shape=jax.ShapeDtypeStruct((n, PACK_W), F32), compiler_params=_cparams(1),
    )(slabs)


def adamw(g_slabs, w, m, v, name):
    n_slab, n, _ = g_slabs.shape

    def body(g_ref, w_ref, m_ref, v_ref, go_ref, d_ref, mo_ref, vo_ref):
        g = g_ref[0].astype(F32)
        for j in range(1, n_slab):
            g = g + g_ref[j].astype(F32)
        m_new = ADAM_B1 * m_ref[...] + (1.0 - ADAM_B1) * g
        v_new = ADAM_B2 * v_ref[...] + (1.0 - ADAM_B2) * (g * g)
        m_hat = m_new / (1.0 - ADAM_B1 ** ADAM_STEP)
        v_hat = v_new / (1.0 - ADAM_B2 ** ADAM_STEP)
        go_ref[...] = g
        d_ref[...] = -ADAM_LR * (m_hat / (jnp.sqrt(v_hat) + ADAM_EPS) + ADAM_WD * w_ref[...])
        mo_ref[...] = m_new
        vo_ref[...] = v_new

    flat = pl.BlockSpec((PACK_TILE, PACK_W), lambda i: (i, 0))
    return pl.pallas_call(
        body, name=name, grid=(n // PACK_TILE,),
        in_specs=[pl.BlockSpec((n_slab, PACK_TILE, PACK_W), lambda i: (0, i, 0)), flat, flat, flat],
        out_specs=[flat] * 4, out_shape=[jax.ShapeDtypeStruct((n, PACK_W), F32)] * 4, compiler_params=_cparams(1),
    )(g_slabs, w, m, v)


def all_gather(x, name):
    m_per, n = x.shape

    def body(x_ref, out_ref, send_sems, recv_sems, local_sem):
        px, py, pc = lax.axis_index("x"), lax.axis_index("y"), lax.axis_index("c")
        me, sibling = (px, py, pc), (px, py, 1 - pc)
        chips = [(1 - px, py), (px, 1 - py), (1 - px, 1 - py)]

        def rows(bx, by, bc):
            return out_ref.at[pl.ds((4 * bx + 2 * by + bc) * m_per, m_per), :]

        def copy(k, block, to, src=None):
            return pltpu.make_async_remote_copy(
                src_ref=rows(*block) if src is None else src, dst_ref=rows(*block),
                send_sem=send_sems.at[k], recv_sem=recv_sems.at[k], device_id=to, device_id_type=MESH)

        mine = pltpu.make_async_copy(x_ref, rows(*me), local_sem)
        mine.start()
        first = [copy(0, me, sibling, src=x_ref)]
        first += [copy(1 + j, me, (*chip, pc), src=x_ref) for j, chip in enumerate(chips)]
        for cp in first:
            cp.start()
        passed = [copy(4 + j, (*chip, pc), sibling) for j, chip in enumerate(chips)]
        for j, chip in enumerate(chips):
            copy(1 + j, (*chip, pc), me).wait_recv()
            passed[j].start()
        copy(0, sibling, me).wait_recv()
        for j, chip in enumerate(chips):
            copy(4 + j, (*chip, 1 - pc), me).wait_recv()
        for cp in first + passed:
            cp.wait_send()
        mine.wait()

    return pl.pallas_call(
        body, name=name, out_shape=jax.ShapeDtypeStruct((N_DEV * m_per, n), x.dtype),
        in_specs=[pl.BlockSpec(memory_space=pl.ANY)], out_specs=pl.BlockSpec(memory_space=pl.ANY),
        scratch_shapes=[pltpu.SemaphoreType.DMA((7,)), pltpu.SemaphoreType.DMA((7,)), pltpu.SemaphoreType.DMA],
    )(x)


N_CHIP = 4


def pair_swap(x, name):
    def body(x_ref, out_ref, send_sem, recv_sem):
        sibling = (lax.axis_index("x"), lax.axis_index("y"), 1 - lax.axis_index("c"))
        copy = pltpu.make_async_remote_copy(src_ref=x_ref, dst_ref=out_ref, send_sem=send_sem, recv_sem=recv_sem,
                                            device_id=sibling, device_id_type=MESH)
        copy.start()
        copy.wait()

    return pl.pallas_call(
        body, name=name, out_shape=jax.ShapeDtypeStruct(x.shape, x.dtype),
        in_specs=[pl.BlockSpec(memory_space=pl.ANY)], out_specs=pl.BlockSpec(memory_space=pl.ANY),
        scratch_shapes=[pltpu.SemaphoreType.DMA, pltpu.SemaphoreType.DMA],
    )(x)


def pair_add(a, b, name):
    n_slab, n, _ = a.shape

    def body(a_ref, b_ref, o_ref):
        o_ref[...] = (a_ref[...].astype(F32) + b_ref[...].astype(F32)).astype(o_ref.dtype)

    spec = pl.BlockSpec((1, PACK_TILE, PACK_W), lambda s, i: (s, i, 0))
    return pl.pallas_call(
        body, name=name, grid=(n_slab, n // PACK_TILE), in_specs=[spec, spec], out_specs=spec,
        out_shape=jax.ShapeDtypeStruct(a.shape, a.dtype), compiler_params=_cparams(2),
    )(a, b)


def chip_all_to_all(x, name):
    def body(x_ref, out_ref, send_sems, recv_sems, local_sem):
        px, py, pc = lax.axis_index("x"), lax.axis_index("y"), lax.axis_index("c")
        mine_idx = 2 * px + py
        local = pltpu.make_async_copy(x_ref.at[mine_idx], out_ref.at[mine_idx], local_sem)
        local.start()
        copies = []
        for k, (fx, fy) in enumerate(((0, 1), (1, 0), (1, 1))):
            qx, qy = px ^ fx, py ^ fy
            peer_idx = 2 * qx + qy
            copies.append((
                pltpu.make_async_remote_copy(
                    src_ref=x_ref.at[peer_idx], dst_ref=out_ref.at[mine_idx], send_sem=send_sems.at[k],
                    recv_sem=recv_sems.at[k], device_id=(qx, qy, pc), device_id_type=MESH),
                pltpu.make_async_remote_copy(
                    src_ref=x_ref.at[peer_idx], dst_ref=out_ref.at[peer_idx], send_sem=send_sems.at[k],
                    recv_sem=recv_sems.at[k], device_id=(qx, qy, pc), device_id_type=MESH)))
        for send, _ in copies:
            send.start()
        for _, landing in copies:
            landing.wait_recv()
        for send, _ in copies:
            send.wait_send()
        local.wait()

    return pl.pallas_call(
        body, name=name, out_shape=jax.ShapeDtypeStruct(x.shape, x.dtype),
        in_specs=[pl.BlockSpec(memory_space=pl.ANY)], out_specs=pl.BlockSpec(memory_space=pl.ANY),
        scratch_shapes=[pltpu.SemaphoreType.DMA((3,)), pltpu.SemaphoreType.DMA((3,)), pltpu.SemaphoreType.DMA],
    )(x)


IN_OFFSETS = {}
_off = 0
for _name, _width in (("mla_q", 256), ("mla_kv", 128), ("mla_kr", 32), ("gla_q", 512), ("gla_k", 512), ("gla_v", 512),
                      ("gla_g", 512), ("gla_rf", 16), ("gla_rb", 16), ("ret_q", 512), ("ret_k", 512), ("ret_v", 512),
                      ("ret_g", 512), ("gate_mla", 1024), ("gate_gla", 1024), ("gate_ret", 1024)):
    IN_OFFSETS[_name] = (_off, _off + _width)
    _off += _width
N_IN = _off

P_GLA, P_RET, P_GATE, P_MLAQ, P_MLAKV, P_MLAKR, P_RANK, P_END = 0, 2048, 4096, 7168, 7424, 7552, 7680, 7808


def _pad_in_proj(w):
    def cols(a, b):
        return w[:, IN_OFFSETS[a][0]:IN_OFFSETS[b][1]]

    def z(n):
        return jnp.zeros((w.shape[0], n), w.dtype)

    return jnp.concatenate([cols("gla_q", "gla_g"), cols("ret_q", "ret_g"), cols("gate_mla", "gate_ret"),
                            cols("mla_q", "mla_kv"), z(MLA_NOPE), cols("mla_kr", "mla_kr"),
                            z(HEAD_PAD - MLA_QK), cols("gla_rf", "gla_rb"), z(HEAD_PAD - 2 * GLA_RANK),
                            z(N_IN_PAD - P_END)], axis=1)


def _pad_last(a, n):
    return jnp.pad(a, [(0, 0)] * (a.ndim - 1) + [(0, n - a.shape[-1])])


def _position_tables(tc, t):
    pos = jnp.arange(t)
    inv = ROPE_THETA ** (-jnp.arange(MLA_ROPE // 4, dtype=F32) * 2.0 / (MLA_ROPE // 2))
    ang_r = (pos // GRID_W).astype(F32)[:, None] * inv[None, :]
    ang_c = (pos % GRID_W).astype(F32)[:, None] * inv[None, :]
    z8, z32, z64 = jnp.zeros((t, 8), F32), jnp.zeros((t, 32), F32), jnp.zeros((t, 64), F32)
    lat_c = jnp.concatenate([jnp.ones((t, 64), F32), jnp.cos(ang_r), jnp.cos(ang_r), jnp.cos(ang_c), jnp.cos(ang_c),
                             z32], axis=1)
    lat_sn = jnp.concatenate([z64, -jnp.sin(ang_r), z8, -jnp.sin(ang_c), z8, z32], axis=1)
    lat_sp = jnp.concatenate([z64, z8, jnp.sin(ang_r), z8, jnp.sin(ang_c), z32], axis=1)
    ctx_c = jnp.concatenate([jnp.ones((tc, MLA_QK), F32), jnp.zeros((tc, HEAD_PAD - MLA_QK), F32)], axis=1)
    ctx_z = jnp.zeros((tc, HEAD_PAD), F32)
    rinv = 1.0 / (RET_THETA ** jnp.linspace(0.0, 1.0, RET_DK // 2, dtype=F32))
    rang = jnp.arange(tc + t).astype(F32)[:, None] * rinv[None, :]
    return dict(c=jnp.concatenate([ctx_c, lat_c]), sn=jnp.concatenate([ctx_z, lat_sn]),
                sp=jnp.concatenate([ctx_z, lat_sp]),
                rc=jnp.concatenate([jnp.cos(rang), jnp.cos(rang)], axis=1),
                rs=jnp.concatenate([-jnp.sin(rang), jnp.sin(rang)], axis=1))


def _heads(x):
    return [x[:, h * HEAD_PAD:(h + 1) * HEAD_PAD] for h in range(x.shape[1] // HEAD_PAD)]


def _mla_rope(x, c, sn, sp):
    return x * c + _roll(x, HEAD_PAD - 8, 1) * sn + _roll(x, 8, 1) * sp


def _norm_mod_fn(shift_row, scale_row):
    def fn(h, mod, w):
        return (_rms(h, D, w) * (1.0 + mod[scale_row:scale_row + 1]) + mod[shift_row:shift_row + 1],)
    return fn


def _resid_fn(gate_row):
    def fn(h, y, mod):
        return (h + mod[gate_row:gate_row + 1] * y,)
    return fn


def _mla_prep_fn(x, c, sn, sp, q_norm_a, w_qb, q_norm, kv_norm_a, w_k, w_v, k_norm):
    cq, ckv = x[:, :MLA_Q_LORA], x[:, MLA_Q_LORA:MLA_Q_LORA + MLA_KV_LORA]
    kr = x[:, MLA_Q_LORA + MLA_KV_LORA:]
    qf = _bdot(_rms(cq, MLA_Q_LORA, q_norm_a), w_qb, "nn")
    q = jnp.concatenate([_mla_rope(_rms(qh, MLA_QK, q_norm), c, sn, sp) for qh in _heads(qf)], axis=1)
    xkv = _rms(ckv, MLA_KV_LORA, kv_norm_a)
    kf = _bdot(xkv, w_k, "nn")
    k = jnp.concatenate([_mla_rope(_rms(kh + kr, MLA_QK, k_norm), c, sn, sp) for kh in _heads(kf)], axis=1)
    return q, k, _bdot(xkv, w_v, "nn")


def _decay_fn(x, w2, b):
    la = _log_sigmoid(_bdot(x[:, :HEAD_PAD], w2, "nn") + b) * (1.0 / GLA_NORMALIZER)
    return la[:, :GLA_HEADS * GLA_DK], la[:, GLA_HEADS * GLA_DK:]


def _ret_rot_fn(q, k, rc, rs):
    def rot(x, scale):
        return jnp.concatenate([(xh * rc + _roll(xh, RET_DK // 2, 1) * rs) * scale for xh in _heads(x)], axis=1)
    return rot(q, 1.0), rot(k, RET_DK ** -0.5)


def _gla_out_fn(o_f, o_b, g, w):
    y = jnp.concatenate([_rms(oh, HEAD_PAD, w) for oh in _heads(o_f + o_b)], axis=1)
    return (y * _silu(g),)


def _ret_out_fn(o_f, o_b, g):
    y = jnp.concatenate([_rms(oh, HEAD_PAD) for oh in _heads(o_f + o_b)], axis=1)
    return (y * _silu(g),)


def _merge_fn(z0, z1, z2, g0a, g0b, g1a, g1b, g2a, g2b, bg):
    out = 0.0
    for n, (z, ga, gb) in enumerate(((z0, g0a, g0b), (z1, g1a, g1b), (z2, g2a, g2b))):
        out = out + jax.nn.sigmoid(jnp.concatenate([ga, gb], axis=1) + bg[n:n + 1]) * z
    return (out,)


def _ffn_act_fn(c, up):
    return (_gelu_tanh(c) * up,)


def _layer(l, h, mod, w, tabs, tc):
    nct = tc // ROW_TILE
    tag = f"_l{l}"
    row = lambda a: a[l][None]
    a = rowwise("norm1" + tag, _norm_mod_fn(0, 1), [h], [mod], [row(w["norm1_w"])], [D], nct)[0]
    pieces = linear_pieces(a, _pad_in_proj(w["w_in"][l]), "in_proj" + tag)
    piece = lambda start: pieces[start // PIECE_W]

    w_qb = _pad_last(w["mla_w_qb"][l].reshape(MLA_Q_LORA, MLA_HEADS, MLA_QK), HEAD_PAD).reshape(MLA_Q_LORA, -1)
    w_kvb = w["mla_w_kvb"][l].reshape(MLA_KV_LORA, MLA_HEADS, MLA_NOPE + MLA_V)
    w_k = _pad_last(w_kvb[:, :, :MLA_NOPE], HEAD_PAD).reshape(MLA_KV_LORA, -1)
    w_v = _pad_last(w_kvb[:, :, MLA_NOPE:], HEAD_PAD).reshape(MLA_KV_LORA, -1)
    rope = [tabs["c"], tabs["sn"], tabs["sp"]]
    q, k, v = rowwise("mla_prep" + tag, _mla_prep_fn, [piece(P_MLAQ)] + rope, [],
                      [row(w["mla_q_norm_a"]), w_qb, _pad_last(row(w["mla_q_norm"]), HEAD_PAD),
                       row(w["mla_kv_norm_a"]), w_k, w_v, _pad_last(row(w["mla_k_norm"]), HEAD_PAD)],
                      [MLA_HEADS * HEAD_PAD] * 3, nct, diff_rows=[True, False, False, False])
    y_mla = attention(q, k, v, tc, "attn" + tag)
    wb_mla = _pad_last(w["w_branch"][l, 0].reshape(MLA_HEADS, MLA_V, D).transpose(0, 2, 1), HEAD_PAD)
    wb_mla = wb_mla.transpose(0, 2, 1).reshape(MLA_HEADS * HEAD_PAD, D)

    w2 = jnp.zeros((HEAD_PAD, 2 * GLA_HEADS * GLA_DK), F32)
    w2 = w2.at[:GLA_RANK, :GLA_HEADS * GLA_DK].set(w["gla_w_gk2"][l, 0])
    w2 = w2.at[GLA_RANK:2 * GLA_RANK, GLA_HEADS * GLA_DK:].set(w["gla_w_gk2"][l, 1])
    la_f, la_b = rowwise("gla_decay" + tag, _decay_fn, [piece(P_RANK)], [],
                         [w2, w["gla_b_gk"][l].reshape(1, -1)], [GLA_HEADS * GLA_DK] * 2, nct)
    gq, gk, gv, gg = [piece(P_GLA + n * PIECE_W) for n in range(4)]
    o_f = scan("gla", True, gq, gk, gv, la_f, tc, "gla_fw" + tag)
    o_b = scan("gla", False, gq, gk, gv, la_b, tc, "gla_bw" + tag)
    y_gla = rowwise("gla_out" + tag, _gla_out_fn, [o_f, o_b, gg], [], [row(w["gla_o_norm"])], [512], nct)[0]

    rq, rk = rowwise("ret_rot" + tag, _ret_rot_fn, [piece(P_RET), piece(P_RET + PIECE_W), tabs["rc"], tabs["rs"]],
                     [], [], [512, 512], nct, diff_rows=[True, True, False, False])
    rv, rg = piece(P_RET + 2 * PIECE_W), piece(P_RET + 3 * PIECE_W)
    rd = jnp.broadcast_to(w["ret_decay"][l][:, :, None, None], (2, RET_HEADS, 8, HEAD_PAD))
    r_f = scan("ret", True, rq, rk, rv, rd[0], tc, "ret_fw" + tag)
    r_b = scan("ret", False, rq, rk, rv, rd[1], tc, "ret_bw" + tag)
    y_ret = rowwise("ret_out" + tag, _ret_out_fn, [r_f, r_b, rg], [], [], [512], nct)[0]

    z = [linear(y_mla, wb_mla, "branch_mla" + tag), linear(y_gla, w["w_branch"][l, 1], "branch_gla" + tag),
         linear(y_ret, w["w_branch"][l, 2], "branch_ret" + tag)]
    gates = [piece(P_GATE + n * PIECE_W) for n in range(6)]
    merged = rowwise("merge" + tag, _merge_fn, z + gates, [], [_pad_rows(w["b_gate"][l], 8)], [D], nct)[0]
    y = linear(merged, w["w_out"][l], "w_out" + tag)
    h = rowwise("resid1" + tag, _resid_fn(2), [h, y], [mod], [], [D], nct)[0]

    a2 = rowwise("norm2" + tag, _norm_mod_fn(3, 4), [h], [mod], [row(w["norm2_w"])], [D], nct)[0]
    gate = linear(a2, w["w_ffn_in"][l][:, :D_FF], "ffn_gate" + tag)
    up = linear(a2, w["w_ffn_in"][l][:, D_FF:], "ffn_up" + tag)
    conv = dwconv(gate, _pad_rows(w["w_dw"][l], 8), row(w["b_dw"]), tc, "dwconv" + tag)
    u = rowwise("ffn_act" + tag, _ffn_act_fn, [conv, up], [], [], [D_FF], nct)[0]
    f = linear(u, w["w_ffn_out"][l], "ffn_out" + tag)
    return rowwise("resid2" + tag, _resid_fn(5), [h, f], [mod], [], [D], nct)[0]


def _pad_rows(a, n):
    return jnp.pad(a, [(0, n - a.shape[0])] + [(0, 0)] * (a.ndim - 1))


def local_loss(w, mod, x, ctx, target):
    tc, t = ctx.shape[0], x.shape[0]
    tabs = _position_tables(tc, t)
    h = jnp.concatenate([ctx, x], axis=0)
    for l in range(DEPTH):
        h = _layer(l, h, mod[l], w, tabs, tc)
    return loss_head(h, target, tc, "loss_head")


ADA_ROWS = 16


def ada_forward(cond_in, w_ada, b_loc):
    cols = w_ada.shape[2]

    def body(x_ref, w_ref, b_ref, o_ref):
        s = _silu(x_ref[...])
        for l in range(DEPTH):
            o_ref[l] = _dg(s, w_ref[l], "nn") + b_ref[l]

    return pl.pallas_call(
        body, name="ada_forward", out_shape=jax.ShapeDtypeStruct((DEPTH, ADA_ROWS, cols), F32),
        compiler_params=pltpu.CompilerParams(vmem_limit_bytes=VMEM_LIMIT_BYTES),
    )(cond_in, w_ada, b_loc)


def ada_backward(cond_in, g_loc, dmod_own, w_ada):
    cols = w_ada.shape[2]

    def body(x_ref, g_ref, own_ref, w_ref, gw_ref, dc_ref, gb_ref):
        x = x_ref[...]
        s = _silu(x)
        dcond = jnp.zeros((8, D), F32)
        for l in range(DEPTH):
            g_ctx = jnp.sum(g_ref[2 * l], axis=0, keepdims=True)
            g_rows = jnp.concatenate([g_ref[2 * l + 1], jnp.broadcast_to(g_ctx, (8, cols))], axis=0)
            keep = lax.broadcasted_iota(jnp.int32, (ADA_ROWS, cols), 0) <= N_DEV
            gw_ref[l] = _dg(s, jnp.where(keep, g_rows, 0.0), "tn")
            dcond = dcond + _dg(jnp.broadcast_to(g_ctx, (8, cols)), w_ref[l], "nt")
            gb_ref[l:l + 1, :] = own_ref[2 * l:2 * l + 1, :] + own_ref[2 * l + 1:2 * l + 2, :]
        xc = x[N_DEV:N_DEV + 1]
        sig = jax.nn.sigmoid(xc)
        dc_ref[...] = dcond[0:1] * (sig * (1.0 + xc * (1.0 - sig)))

    return pl.pallas_call(
        body, name="ada_backward",
        out_shape=[jax.ShapeDtypeStruct(w_ada.shape, F32), jax.ShapeDtypeStruct((1, D), F32),
                   jax.ShapeDtypeStruct((DEPTH, 6 * D), F32)],
        compiler_params=pltpu.CompilerParams(vmem_limit_bytes=VMEM_LIMIT_BYTES),
    )(cond_in, g_loc, dmod_own, w_ada)


WEIGHTS = ["c_ctx", "w_ada", "b_ada", "norm1_w", "norm2_w", "w_in", "b_gate", "mla_q_norm_a", "mla_w_qb",
           "mla_kv_norm_a", "mla_w_kvb", "mla_q_norm", "mla_k_norm", "gla_w_gk2", "gla_b_gk", "gla_o_norm",
           "ret_decay", "w_branch", "w_out", "w_ffn_in", "w_dw", "b_dw", "w_ffn_out"]
INPUTS = ["x", "c", "ctx"] + WEIGHTS + ["loss_target"] + ["m_" + n for n in WEIGHTS] + ["v_" + n for n in WEIGHTS]
BIG = {"w_in": 2, "mla_w_qb": 2, "mla_w_kvb": 2, "w_branch": 3, "w_out": 1, "w_ffn_in": 2, "w_ffn_out": 1}
SMALL_SHARDED = {"b_gate": 2, "gla_w_gk2": 3, "gla_b_gk": 2, "w_dw": 2}
SMALL = ["c_ctx", "b_ada", "norm1_w", "norm2_w", "b_gate", "mla_q_norm_a", "mla_kv_norm_a", "mla_q_norm", "mla_k_norm",
         "gla_w_gk2", "gla_b_gk", "gla_o_norm", "ret_decay", "w_dw", "b_dw"]


def _entry_rows(size, align):
    return -(-size // (PACK_W * align)) * align


def _pack(arrays, rows, dtype, align, lead=0):
    parts = []
    for a in arrays:
        head = a.shape[:lead]
        size = math.prod(a.shape[lead:])
        r = _entry_rows(size, align)
        if r * PACK_W == size:
            parts.append(a.astype(dtype).reshape(head + (r, PACK_W)))
        else:
            flat = jnp.pad(a.astype(dtype).reshape(head + (size,)), [(0, 0)] * lead + [(0, r * PACK_W - size)])
            parts.append(flat.reshape(head + (r, PACK_W)))
    used = sum(p.shape[lead] for p in parts)
    if rows > used:
        parts.append(jnp.zeros(parts[0].shape[:lead] + (rows - used, PACK_W), dtype))
    return jnp.concatenate(parts, axis=lead)


def _pack_rows(shapes, align, multiple):
    used = sum(_entry_rows(math.prod(s), align) for s in shapes)
    return -(-used // multiple) * multiple


def _unpack(pack, shapes, align):
    head = pack.shape[:-2]
    out, off = [], 0
    for shape in shapes:
        size = math.prod(shape)
        r = _entry_rows(size, align)
        block = lax.slice_in_dim(pack, off, off + r, axis=len(head))
        if r * PACK_W != size:
            block = block.reshape(head + (r * PACK_W,))[..., :size]
        out.append(block.reshape(head + tuple(shape)))
        off += r
    return out


def _join_shards(stacked, axis):
    moved = jnp.moveaxis(stacked, 0, axis)
    shape = list(moved.shape)
    return moved.reshape(shape[:axis] + [shape[axis] * shape[axis + 1]] + shape[axis + 2:])


def _split_shards(full, axis):
    shape = list(full.shape)
    split = full.reshape(shape[:axis] + [N_DEV, shape[axis] // N_DEV] + shape[axis + 1:])
    return jnp.moveaxis(split, axis, 0)


def _gather_shards(local, axes, dtype, rows_multiple, name):
    names = list(axes)
    shapes = [local[n].shape for n in names]
    rows = _pack_rows(shapes, rows_multiple, rows_multiple)
    gathered = all_gather(_pack([local[n] for n in names], rows, dtype, rows_multiple), name)
    stacked = _unpack(gathered.reshape(N_DEV, rows, PACK_W), shapes, rows_multiple)
    return {n: _join_shards(s, axes[n]).astype(F32) for n, s in zip(names, stacked)}


def kernel(*args):
    a = dict(zip(INPUTS, args))
    me = 4 * lax.axis_index("x") + 2 * lax.axis_index("y") + lax.axis_index("c")
    cols = a["w_ada"].shape[2]

    small_names = list(SMALL_SHARDED)
    small_local = [a[n].shape for n in small_names]
    first_rows = _pack_rows([a["c"].shape] + small_local, 8, 8)
    first = all_gather(_pack([a["c"]] + [a[n] for n in small_names], first_rows, F32, 8), "gather_small")
    first = _unpack(first.reshape(N_DEV, first_rows, PACK_W), [a["c"].shape] + small_local, 8)
    c_all = first[0][:, 0]

    cond_in = jnp.concatenate([c_all, a["c_ctx"][None], jnp.zeros((ADA_ROWS - N_DEV - 1, D), F32)], axis=0)
    b_loc = lax.dynamic_slice_in_dim(a["b_ada"], me * cols, cols, axis=1)[:, None, :]
    mod_loc = ada_forward(cond_in, a["w_ada"], b_loc)
    mod_all = all_gather(mod_loc.reshape(DEPTH * ADA_ROWS, cols), "gather_mod")
    mod_all = mod_all.reshape(N_DEV, DEPTH, ADA_ROWS, cols).transpose(1, 2, 0, 3).reshape(DEPTH, ADA_ROWS, 6, D)
    mod_me = lax.dynamic_index_in_dim(mod_all, me, axis=1, keepdims=False)
    mod = jnp.pad(jnp.stack([mod_all[:, N_DEV], mod_me], axis=1), ((0, 0), (0, 0), (0, 2), (0, 0)))

    w = _gather_shards(a, BIG, BF16, 16, "gather_weights")
    w.update({n: _join_shards(s, SMALL_SHARDED[n]) for n, s in zip(small_names, first[1:])})
    for n in SMALL:
        if n not in SMALL_SHARDED and n not in ("c_ctx", "b_ada"):
            w[n] = a[n]

    loss, (gw, gmod, gx) = jax.value_and_grad(local_loss, argnums=(0, 1, 2))(
        w, mod, a["x"][0], a["ctx"][0], a["loss_target"][0])
    loss = lax.psum(loss, ("x", "y", "c"))

    dmod_own = gmod[:, :, :6].reshape(2 * DEPTH, 6 * D)
    g_all = all_gather(jnp.pad(dmod_own, ((0, 8 - 2 * DEPTH), (0, 0))), "gather_dmod").reshape(N_DEV, 8, 6 * D)
    g_loc = lax.dynamic_slice_in_dim(g_all[:, :2 * DEPTH], me * cols, cols, axis=2).transpose(1, 0, 2)
    g_w_ada, g_c_ctx, g_b_ada = ada_backward(cond_in, g_loc, dmod_own, a["w_ada"])

    small_part = dict(gw, c_ctx=g_c_ctx, b_ada=g_b_ada)
    small_shapes = [a[n].shape if n not in SMALL_SHARDED else gw[n].shape for n in SMALL]
    rows = _pack_rows(small_shapes, 8, PACK_TILE)
    parts = all_gather(_pack([small_part[n] for n in SMALL], rows, F32, 8), "gather_small_grads")
    small_sum = _unpack(slab_sum(parts.reshape(N_DEV, rows, PACK_W), "sum_small_grads"), small_shapes, 8)
    g_small = {}
    for n, g in zip(SMALL, small_sum):
        if n in SMALL_SHARDED:
            ax = SMALL_SHARDED[n]
            g = lax.dynamic_slice_in_dim(g, me * a[n].shape[ax], a[n].shape[ax], axis=ax)
        g_small[n] = g

    big_rows = _pack_rows([a[n].shape for n in BIG], 16, PACK_TILE)
    slabs = _pack([_split_shards(gw[n], ax) for n, ax in BIG.items()], big_rows, BF16, 16, lead=1)
    by_core = slabs.reshape(N_CHIP, 2, big_rows, PACK_W)
    my_core = lax.axis_index("c")
    keep = lax.dynamic_index_in_dim(by_core, my_core, axis=1, keepdims=False)
    give = lax.dynamic_index_in_dim(by_core, 1 - my_core, axis=1, keepdims=False)
    pair_sum = pair_add(keep, pair_swap(give, "swap_grads"), "add_pair_grads")
    landed = chip_all_to_all(pair_sum, "scatter_grads")

    def update(names, g_slabs, rows, align, label):
        shapes = [a[n].shape for n in names]
        packs = [_pack([a[pre + n] for n in names], rows, F32, align) for pre in ("", "m_", "v_")]
        outs = adamw(g_slabs, *packs, label)
        return [dict(zip(names, _unpack(o, shapes, align))) for o in outs]

    res_big = update(list(BIG), landed, big_rows, 16, "adamw_big")
    ada_rows = _pack_rows([a["w_ada"].shape], 8, PACK_TILE)
    res_ada = update(["w_ada"], _pack([g_w_ada], ada_rows, F32, 8)[None], ada_rows, 8, "adamw_ada")
    small_rows = _pack_rows([a[n].shape for n in SMALL], 8, PACK_TILE)
    res_small = update(SMALL, _pack([g_small[n] for n in SMALL], small_rows, F32, 8)[None], small_rows, 8,
                       "adamw_small")

    outs = [loss, gx[None]]
    for k in range(4):
        merged = {**res_big[k], **res_ada[k], **res_small[k]}
        outs += [merged[n] for n in WEIGHTS]
    return tuple(outs)
```

```python
import functools
import math

import jax
import jax.numpy as jnp
import numpy as np
from jax import lax
from jax.experimental import pallas as pl
from jax.experimental.pallas import tpu as pltpu

F32 = jnp.float32
BF16 = jnp.bfloat16

N_DEV = 8
D = 1024
DEPTH = 2
GRID_W = 64
MLA_HEADS = 8
MLA_NOPE = 64
MLA_ROPE = 32
MLA_QK = 96
MLA_V = 64
MLA_Q_LORA = 256
MLA_KV_LORA = 128
GLA_HEADS = 4
GLA_DK = 128
GLA_RANK = 16
GLA_NORMALIZER = 16.0
RET_HEADS = 4
RET_DK = 128
BRANCH_W = 512
D_FF = 2816
CHUNK = 64
ROPE_THETA = 10000.0
RET_THETA = 10000.0
EPS = 1e-6
HEAD_PAD = 128
N_IN_PAD = 8192

ADAM_LR = 0.001
ADAM_B1 = 0.9
ADAM_B2 = 0.999
ADAM_EPS = 1e-08
ADAM_WD = 0.01
ADAM_STEP = 10

ROW_TILE = 256
SCAN_CHUNKS = ROW_TILE // CHUNK
VMEM_LIMIT_BYTES = 56 * 1024 * 1024
MESH = pl.DeviceIdType.MESH


def _cparams(n_axes):
    return pltpu.CompilerParams(dimension_semantics=("arbitrary",) * n_axes, vmem_limit_bytes=VMEM_LIMIT_BYTES)


def _pick(dim, cands):
    for cand in cands:
        if dim % cand == 0:
            return cand
    return dim


_DOT_DIMS = {"nn": (((1,), (0,)), ((), ())), "nt": (((1,), (1,)), ((), ())), "tn": (((0,), (0,)), ((), ()))}


def _dg(a, b, mode):
    return lax.dot_general(a.astype(BF16), b.astype(BF16), _DOT_DIMS[mode], preferred_element_type=F32)


def _bdot(a, b, mode):
    @jax.custom_vjp
    def f(a, b):
        return _dg(a, b, mode)

    def fwd(a, b):
        return _dg(a, b, mode), (a, b)

    def bwd(res, g):
        a, b = res
        if mode == "nn":
            return _dg(g, b, "nt").astype(a.dtype), _dg(a, g, "tn").astype(b.dtype)
        if mode == "nt":
            return _dg(g, b, "nn").astype(a.dtype), _dg(g, a, "tn").astype(b.dtype)
        return _dg(b, g, "nt").astype(a.dtype), _dg(a, g, "nn").astype(b.dtype)

    f.defvjp(fwd, bwd)
    return f(a, b)


def _roll(x, shift, axis):
    n = x.shape[axis]
    shift = shift % n

    @jax.custom_vjp
    def f(x):
        return pltpu.roll(x, shift, axis)

    def fwd(x):
        return pltpu.roll(x, shift, axis), None

    def bwd(_, g):
        return (pltpu.roll(g, (n - shift) % n, axis),)

    f.defvjp(fwd, bwd)
    return f(x)


@jax.custom_jvp
def _log_sigmoid(x):
    return jnp.minimum(x, 0.0) - jnp.log(1.0 + jnp.exp(-jnp.abs(x)))


@_log_sigmoid.defjvp
def _log_sigmoid_jvp(primals, tangents):
    (x,), (t,) = primals, tangents
    return _log_sigmoid(x), t * jax.nn.sigmoid(-x)


def _rms(x, n, w=None):
    y = x * lax.rsqrt(jnp.sum(x * x, axis=-1, keepdims=True) * (1.0 / n) + EPS)
    return y if w is None else y * w


def _silu(x):
    return x * jax.nn.sigmoid(x)


def _gelu_tanh(x):
    return 0.5 * x * (1.0 + jnp.tanh(math.sqrt(2.0 / math.pi) * (x + 0.044715 * (x * x * x))))


def _mm(a, b, mode, name):
    if mode == "nn":
        (m, k), (_, n) = a.shape, b.shape
    elif mode == "nt":
        (m, k), (n, _) = a.shape, b.shape
    else:
        (k, m), (_, n) = a.shape, b.shape
    tm = _pick(m, (1024, 768, 1408, 512, 256, 128))
    tn = _pick(n, (1024, 1408, 512, 256, 128))
    tk = _pick(k, (1024, 768, 1408, 512, 256, 128))
    nk = k // tk
    if mode == "nn":
        a_spec = pl.BlockSpec((tm, tk), lambda i, j, kk: (i, kk))
        b_spec = pl.BlockSpec((tk, tn), lambda i, j, kk: (kk, j))
    elif mode == "nt":
        a_spec = pl.BlockSpec((tm, tk), lambda i, j, kk: (i, kk))
        b_spec = pl.BlockSpec((tn, tk), lambda i, j, kk: (j, kk))
    else:
        a_spec = pl.BlockSpec((tk, tm), lambda i, j, kk: (kk, i))
        b_spec = pl.BlockSpec((tk, tn), lambda i, j, kk: (kk, j))

    def body(a_ref, b_ref, o_ref):
        kk = pl.program_id(2)
        part = _dg(a_ref[...], b_ref[...], mode)
        if nk == 1:
            o_ref[...] = part
        else:
            @pl.when(kk == 0)
            def _():
                o_ref[...] = part

            @pl.when(kk != 0)
            def _():
                o_ref[...] += part

    return pl.pallas_call(
        body, name=name, grid=(m // tm, n // tn, nk),
        in_specs=[a_spec, b_spec], out_specs=pl.BlockSpec((tm, tn), lambda i, j, kk: (i, j)),
        out_shape=jax.ShapeDtypeStruct((m, n), F32),
        compiler_params=_cparams(3),
    )(a, b)


def linear(x, w, name):
    @jax.custom_vjp
    def op(x, w):
        return _mm(x, w.astype(BF16), "nn", name + "_f")

    def fwd(x, w):
        wb = w.astype(BF16)
        return _mm(x, wb, "nn", name + "_f"), (x, wb)

    def bwd(res, g):
        x, wb = res
        return _mm(g, wb, "nt", name + "_dx"), _mm(x, g, "tn", name + "_dw")

    op.defvjp(fwd, bwd)
    return op(x, w)


PIECE_W = 512
PIECE_ROWS = 384


def _mm_split(a, wb, name):
    (r, k), n = a.shape, wb.shape[1] // PIECE_W
    tm = _pick(r, (PIECE_ROWS, ROW_TILE))
    width = PIECE_GROUP * PIECE_W

    n_tiles = r // tm

    def body(a_ref, w_ref, *out_refs):
        j = pl.program_id(0)
        res = _dg(a_ref[...], w_ref[...], "nn")
        for group in range(n // PIECE_GROUP):
            @pl.when(j == group)
            def _(group=group):
                for p in range(PIECE_GROUP):
                    out_refs[group * PIECE_GROUP + p][...] = res[:, p * PIECE_W:(p + 1) * PIECE_W]

    def out_spec(jj):
        group = jj // PIECE_GROUP
        return pl.BlockSpec((tm, PIECE_W),
                            lambda j, i: (jnp.where(j == group, i, jnp.where(j < group, 0, n_tiles - 1)), 0))

    return pl.pallas_call(
        body, name=name, grid=(n // PIECE_GROUP, n_tiles),
        in_specs=[pl.BlockSpec((tm, k), lambda j, i: (i, 0)), pl.BlockSpec((k, width), lambda j, i: (0, j))],
        out_specs=[out_spec(jj) for jj in range(n)],
        out_shape=[jax.ShapeDtypeStruct((r, PIECE_W), F32)] * n, compiler_params=_cparams(2),
    )(a, wb)


def _mm_join(gs, wb, name):
    n, (r, _), k = len(gs), gs[0].shape, wb.shape[0]
    tm = _pick(r, (PIECE_ROWS, ROW_TILE))
    n_groups = n // PIECE_GROUP

    def body(*refs):
        g_refs, w_ref, o_ref = refs[:n], refs[n], refs[n + 1]
        j = pl.program_id(1)
        for group in range(n_groups):
            @pl.when(j == group)
            def _(group=group):
                g = jnp.concatenate([g_refs[group * PIECE_GROUP + p][...].astype(BF16) for p in range(PIECE_GROUP)],
                                    axis=1)
                part = _dg(g, w_ref[...], "nt")
                if group == 0:
                    o_ref[...] = part
                else:
                    o_ref[...] += part

    return pl.pallas_call(
        body, name=name, grid=(r // tm, n_groups),
        in_specs=[pl.BlockSpec((tm, PIECE_W), lambda i, j: (i, 0))] * n
        + [pl.BlockSpec((k, PIECE_GROUP * PIECE_W), lambda i, j: (0, j))],
        out_specs=pl.BlockSpec((tm, k), lambda i, j: (i, 0)), out_shape=jax.ShapeDtypeStruct((r, k), F32),
        compiler_params=_cparams(2),
    )(*gs, wb)


PIECE_GROUP = 4


def _mm_join_tn(a, gs, name):
    n, (r, k) = len(gs), a.shape
    tk = _pick(r, (PIECE_ROWS, ROW_TILE))
    nk = r // tk
    width = PIECE_GROUP * PIECE_W

    def body(*refs):
        a_ref, g_refs, o_ref = refs[0], refs[1:n + 1], refs[n + 1]
        j, kk = pl.program_id(0), pl.program_id(1)

        @pl.when(kk == 0)
        def _():
            o_ref[...] = jnp.zeros_like(o_ref)

        av = a_ref[...].astype(BF16)
        for group in range(n // PIECE_GROUP):
            @pl.when(j == group)
            def _(group=group):
                g = jnp.concatenate([g_refs[group * PIECE_GROUP + p][...].astype(BF16) for p in range(PIECE_GROUP)],
                                    axis=1)
                o_ref[...] += _dg(av, g, "tn")

    def g_spec(jj):
        return pl.BlockSpec((tk, PIECE_W), lambda j, kk: (jnp.where(j == jj // PIECE_GROUP, kk, 0), 0))

    return pl.pallas_call(
        body, name=name, grid=(n // PIECE_GROUP, nk),
        in_specs=[pl.BlockSpec((tk, k), lambda j, kk: (kk, 0))] + [g_spec(jj) for jj in range(n)],
        out_specs=pl.BlockSpec((k, width), lambda j, kk: (0, j)),
        out_shape=jax.ShapeDtypeStruct((k, n * PIECE_W), F32), compiler_params=_cparams(2),
    )(a, *gs)


def rowwise(name, fn, rows, segs, params, out_widths, nct, diff_rows=None, then=None):
    n_row, n_seg, n_par, n_out = len(rows), len(segs), len(params), len(out_widths)
    out_dtype = F32 if then is None else BF16
    diff_rows = [True] * n_row if diff_rows is None else list(diff_rows)
    r_total = rows[0].shape[0]
    n_tiles = r_total // ROW_TILE

    def seg_of(i):
        return jnp.where(i < nct, 0, 1)

    def row_spec(width):
        return pl.BlockSpec((ROW_TILE, width), lambda i: (i, 0))

    def seg_spec(shape):
        nd = len(shape)
        return pl.BlockSpec((1,) + tuple(shape[1:]), lambda i: (seg_of(i),) + (0,) * (nd - 1))

    def par_spec(shape):
        nd = len(shape)
        return pl.BlockSpec(tuple(shape), lambda i: (0,) * nd)

    in_specs = ([row_spec(r.shape[1]) for r in rows] + [seg_spec(s.shape) for s in segs]
                + [par_spec(p.shape) for p in params])

    def load(refs):
        vals = [r[...].astype(F32) for r in refs[:n_row]]
        vals += [r[0].astype(F32) for r in refs[n_row:n_row + n_seg]]
        vals += [r[...].astype(F32) for r in refs[n_row + n_seg:n_row + n_seg + n_par]]
        return vals

    def fwd_call(arrs):
        def body(*refs):
            outs = fn(*load(refs))
            for o_ref, val in zip(refs[n_row + n_seg + n_par:], outs):
                o_ref[...] = val.astype(o_ref.dtype)

        return pl.pallas_call(
            body, name=name + "_f", grid=(n_tiles,), in_specs=in_specs,
            out_specs=[row_spec(w) for w in out_widths],
            out_shape=[jax.ShapeDtypeStruct((r_total, w), out_dtype) for w in out_widths],
            compiler_params=_cparams(1),
        )(*arrs)

    d_idx = [k for k in range(n_row) if diff_rows[k]]

    def bwd_call(arrs, douts):
        n_in = n_row + n_seg + n_par

        def body(*refs):
            i = pl.program_id(0)
            vals = load(refs[:n_in])
            gs = [r[...] for r in refs[n_in:n_in + n_out]]
            out_refs = refs[n_in + n_out:]
            diff_pos = d_idx + list(range(n_row, n_in))

            def f(*dv):
                full = list(vals)
                for pos, v in zip(diff_pos, dv):
                    full[pos] = v
                return tuple(fn(*full))

            _, vjp = jax.vjp(f, *[vals[p] for p in diff_pos])
            grads = vjp(tuple(gs))
            nd = len(d_idx)
            for o_ref, g in zip(out_refs[:nd], grads[:nd]):
                o_ref[...] = g
            first_seg = jnp.logical_or(i == 0, i == nct)
            for o_ref, g in zip(out_refs[nd:nd + n_seg], grads[nd:nd + n_seg]):
                @pl.when(first_seg)
                def _(o_ref=o_ref, g=g):
                    o_ref[0] = g

                @pl.when(jnp.logical_not(first_seg))
                def _(o_ref=o_ref, g=g):
                    o_ref[0] += g
            for o_ref, g in zip(out_refs[nd + n_seg:], grads[nd + n_seg:]):
                @pl.when(i == 0)
                def _(o_ref=o_ref, g=g):
                    o_ref[...] = g

                @pl.when(i != 0)
                def _(o_ref=o_ref, g=g):
                    o_ref[...] += g

        out_specs = ([row_spec(rows[k].shape[1]) for k in d_idx] + [seg_spec(s.shape) for s in segs]
                     + [par_spec(p.shape) for p in params])
        out_shape = ([jax.ShapeDtypeStruct(rows[k].shape, F32) for k in d_idx]
                     + [jax.ShapeDtypeStruct(s.shape, F32) for s in segs]
                     + [jax.ShapeDtypeStruct(p.shape, F32) for p in params])
        return pl.pallas_call(
            body, name=name + "_b", grid=(n_tiles,),
            in_specs=in_specs + [row_spec(w) for w in out_widths],
            out_specs=out_specs, out_shape=out_shape, compiler_params=_cparams(1),
        )(*arrs, *douts)

    @jax.custom_vjp
    def op(*arrs):
        return tuple(fwd_call(arrs))

    def op_fwd(*arrs):
        return tuple(fwd_call(arrs)), arrs

    def op_bwd(arrs, douts):
        grads = list(bwd_call(arrs, douts))
        nd = len(d_idx)
        row_grads = [jnp.zeros_like(arrs[k]) for k in range(n_row)]
        for k, g in zip(d_idx, grads[:nd]):
            row_grads[k] = g
        return tuple(row_grads + grads[nd:])

    if then is None:
        op.defvjp(op_fwd, op_bwd)
        return op(*rows, *segs, *params)

    kind, w, mm_name = then

    def project(u, wb):
        if kind == "linear":
            return _mm(u, wb, "nn", mm_name + "_f")
        return tuple(_mm_split(u, wb, mm_name + "_f"))

    @jax.custom_vjp
    def fused(w, *arrs):
        return project(fwd_call(arrs)[0], w.astype(BF16))

    def fused_fwd(w, *arrs):
        u, wb = fwd_call(arrs)[0], w.astype(BF16)
        return project(u, wb), (arrs, u, wb)

    def fused_bwd(res, g):
        arrs, u, wb = res
        if kind == "linear":
            du, dw = _mm(g, wb, "nt", mm_name + "_dx"), _mm(u, g, "tn", mm_name + "_dw")
        else:
            du, dw = _mm_join(list(g), wb, mm_name + "_dx"), _mm_join_tn(u, list(g), mm_name + "_dw")
        return (dw,) + op_bwd(arrs, [du])

    fused.defvjp(fused_fwd, fused_bwd)
    return fused(w, *rows, *segs, *params)


ATT_SCALE = MLA_QK ** -0.5
LOG2E = math.log2(math.e)
ATT_KEY_CHUNKS = (768, 512, 256)


ATT_LATENT_TILE = 1024


def _query_rows_spec(row0, tq):
    return pl.BlockSpec((pl.Element(tq), pl.Element(HEAD_PAD)),
                        lambda h, i: (pl.multiple_of(row0 + i * tq, ROW_TILE), pl.multiple_of(h * HEAD_PAD, HEAD_PAD)))


def _key_chunks(nk):
    kc = _pick(nk, ATT_KEY_CHUNKS)
    return [(c * kc, kc) for c in range(nk // kc)]


def _attn_fwd_call(q, k, v, row0, n_rows, tq, nk, name):
    def body(q_ref, k_ref, v_ref, o_ref, lse_ref):
        qv = q_ref[...]
        m = jnp.full((tq, 1), -jnp.inf, F32)
        l = jnp.zeros((tq, 1), F32)
        acc = jnp.zeros((tq, HEAD_PAD), F32)
        for start, size in _key_chunks(nk):
            s = lax.dot_general(qv, k_ref[start:start + size, :], _DOT_DIMS["nt"], preferred_element_type=F32)
            m_new = jnp.maximum(m, jnp.max(s, axis=-1, keepdims=True))
            alpha = jnp.exp2(m - m_new)
            p = jnp.exp2(s - m_new)
            l = alpha * l + jnp.sum(p, axis=-1, keepdims=True)
            acc = alpha * acc + lax.dot_general(p.astype(BF16), v_ref[start:start + size, :], _DOT_DIMS["nn"],
                                                preferred_element_type=F32)
            m = m_new
        o_ref[...] = acc / l
        lse_ref[...] = jnp.broadcast_to(m + jnp.log2(l), (tq, HEAD_PAD))

    out_spec = pl.BlockSpec((tq, HEAD_PAD), lambda h, i: (i, h))
    kv_spec = pl.BlockSpec((nk, HEAD_PAD), lambda h, i: (0, h))
    out = jax.ShapeDtypeStruct((n_rows, q.shape[1]), F32)
    return pl.pallas_call(
        body, name=name, grid=(MLA_HEADS, n_rows // tq), in_specs=[_query_rows_spec(row0, tq), kv_spec, kv_spec],
        out_specs=[out_spec, out_spec], out_shape=[out, out], compiler_params=_cparams(2),
    )(q, k, v)


def _attn_bwd_call(q, k, v, o, lse, do, row0, n_rows, tq, nk, name):
    nq = n_rows // tq

    def body(q_ref, k_ref, v_ref, o_ref, lse_ref, do_ref, dq_ref, dk_ref, dv_ref):
        i = pl.program_id(1)

        @pl.when(i == 0)
        def _():
            dk_ref[...] = jnp.zeros_like(dk_ref)
            dv_ref[...] = jnp.zeros_like(dv_ref)

        qv = q_ref[...]
        dov = do_ref[...]
        dob = dov.astype(BF16)
        lse = lse_ref[:, 0:1]
        delta = jnp.sum(dov * o_ref[...], axis=-1, keepdims=True)
        dq = jnp.zeros((tq, HEAD_PAD), F32)
        for start, size in _key_chunks(nk):
            kk = k_ref[start:start + size, :]
            vv = v_ref[start:start + size, :]
            s = lax.dot_general(qv, kk, _DOT_DIMS["nt"], preferred_element_type=F32)
            p = jnp.exp2(s - lse)
            dp = lax.dot_general(dob, vv, _DOT_DIMS["nt"], preferred_element_type=F32)
            g = (p * (dp - delta)).astype(BF16)
            dk_ref[start:start + size, :] += lax.dot_general(g, qv, _DOT_DIMS["tn"], preferred_element_type=F32)
            dv_ref[start:start + size, :] += lax.dot_general(p.astype(BF16), dob, _DOT_DIMS["tn"],
                                                             preferred_element_type=F32)
            dq = dq + lax.dot_general(g, kk, _DOT_DIMS["nn"], preferred_element_type=F32)
        dq_ref[...] = dq * ATT_SCALE

        @pl.when(i == nq - 1)
        def _():
            dk_ref[...] = dk_ref[...] * (1.0 / LOG2E)

    own_spec = pl.BlockSpec((tq, HEAD_PAD), lambda h, i: (i, h))
    kv_spec = pl.BlockSpec((nk, HEAD_PAD), lambda h, i: (0, h))
    rows_spec = _query_rows_spec(row0, tq)
    return pl.pallas_call(
        body, name=name, grid=(MLA_HEADS, nq),
        in_specs=[rows_spec, kv_spec, kv_spec, own_spec, own_spec, rows_spec],
        out_specs=[own_spec, kv_spec, kv_spec],
        out_shape=[jax.ShapeDtypeStruct((n_rows, q.shape[1]), F32), jax.ShapeDtypeStruct((nk, q.shape[1]), F32),
                   jax.ShapeDtypeStruct((nk, q.shape[1]), F32)],
        compiler_params=_cparams(2),
    )(q, k, v, o, lse, do)


def attention(q, k, v, tc, name):
    r_total = q.shape[0]
    tq_lat = _pick(r_total - tc, (ATT_LATENT_TILE, ROW_TILE))
    ranges = [(0, tc, ROW_TILE, tc, "_ctx"), (tc, r_total - tc, tq_lat, r_total, "_lat")]

    def operands(q, k, v):
        return (q * (ATT_SCALE * LOG2E)).astype(BF16), k.astype(BF16), v.astype(BF16)

    def forward(qb, kb, vb):
        return [_attn_fwd_call(qb, kb, vb, row0, n_rows, tq, nk, name + tag + "_f")
                for row0, n_rows, tq, nk, tag in ranges]

    @jax.custom_vjp
    def op(q, k, v):
        return jnp.concatenate([o for o, _ in forward(*operands(q, k, v))], axis=0)

    def fwd(q, k, v):
        qb, kb, vb = operands(q, k, v)
        parts = forward(qb, kb, vb)
        return jnp.concatenate([o for o, _ in parts], axis=0), (qb, kb, vb, parts)

    def bwd(res, do):
        qb, kb, vb, parts = res
        (dq_c, dk_c, dv_c), (dq_l, dk_l, dv_l) = [
            _attn_bwd_call(qb, kb, vb, o, lse, do, row0, n_rows, tq, nk, name + tag + "_b")
            for (o, lse), (row0, n_rows, tq, nk, tag) in zip(parts, ranges)]
        grow = lambda part: jnp.pad(part, ((0, r_total - tc), (0, 0)))
        return jnp.concatenate([dq_c, dq_l], axis=0), dk_l + grow(dk_c), dv_l + grow(dv_c)

    op.defvjp(fwd, bwd)
    return op(q, k, v)


CHUNK_SHIFT = CHUNK.bit_length() - 1


def _block_pairs():
    rows = lax.broadcasted_iota(jnp.int32, (ROW_TILE, ROW_TILE), 0)
    cols = lax.broadcasted_iota(jnp.int32, (ROW_TILE, ROW_TILE), 1)
    same = lax.shift_right_logical(rows, CHUNK_SHIFT) == lax.shift_right_logical(cols, CHUNK_SHIFT)
    return rows, cols, same


def _block_mask(kind):
    rows, cols, same = _block_pairs()
    order = {"lower_incl": rows >= cols, "upper_incl": rows <= cols, "lower_strict": rows > cols,
             "upper_strict": rows < cols}[kind]
    return jnp.logical_and(same, order)


def _row_chunk():
    return lax.shift_right_logical(lax.broadcasted_iota(jnp.int32, (ROW_TILE, 1), 0), CHUNK_SHIFT)


def _dot01(kind, x):
    m = _block_mask(kind).astype(BF16)
    hi = x.astype(BF16)
    rest = x - hi.astype(F32)
    mid = rest.astype(BF16)
    lo = (rest - mid.astype(F32)).astype(BF16)
    terms = jnp.concatenate([hi, mid, lo], axis=1)
    out = lax.dot_general(m, terms, _DOT_DIMS["nn"], preferred_element_type=F32)
    n = x.shape[1]
    return out[:, :n] + out[:, n:2 * n] + out[:, 2 * n:]


def _chunk_sums(x, forward):
    kinds = ("lower_incl", "upper_strict") if forward else ("upper_incl", "lower_strict")
    transposed = ("upper_incl", "lower_strict") if forward else ("lower_incl", "upper_strict")

    @jax.custom_vjp
    def f(x):
        return _dot01(kinds[0], x), _dot01(kinds[1], x)

    def fwd(x):
        return (_dot01(kinds[0], x), _dot01(kinds[1], x)), None

    def bwd(_, g):
        return (_dot01(transposed[0], g[0]) + _dot01(transposed[1], g[1]),)

    f.defvjp(fwd, bwd)
    return f(x)


def _scan_order(forward):
    if forward:
        return list(range(SCAN_CHUNKS)), lambda c: c * CHUNK + CHUNK - 1
    return list(range(SCAN_CHUNKS - 1, -1, -1)), lambda c: c * CHUNK


def _carry_states(forward, st0, inc_all, decay_of):
    order, _ = _scan_order(forward)
    entering = [None] * SCAN_CHUNKS
    st = st0
    for c in order:
        entering[c] = st
        st = st * decay_of(c) + inc_all[:, c * HEAD_PAD:(c + 1) * HEAD_PAD]
    return jnp.concatenate(entering, axis=0), st


def _per_chunk_lanes(x):
    chunk = _row_chunk()
    return jnp.concatenate([jnp.where(chunk == c, x, 0.0) for c in range(SCAN_CHUNKS)], axis=1)


def _own_chunk_lanes(x4):
    chunk = _row_chunk()
    n = x4.shape[1] // SCAN_CHUNKS
    out = jnp.where(chunk == 0, x4[:, :n], 0.0)
    for c in range(1, SCAN_CHUNKS):
        out = out + jnp.where(chunk == c, x4[:, c * n:(c + 1) * n], 0.0)
    return out


def _gla_block(forward, q, k, v, la, st0):
    cum, after = _chunk_sums(la, forward)
    _, last_row = _scan_order(forward)
    q_dec = q * (jnp.exp(cum) * (GLA_DK ** -0.5))
    att = _bdot(q_dec, k * jnp.exp(-cum), "nt")
    att = jnp.where(_block_mask("lower_incl" if forward else "upper_strict"), att, 0.0)
    inc_all = _bdot(v, _per_chunk_lanes(k * jnp.exp(after)), "tn")
    entering, st1 = _carry_states(forward, st0, inc_all,
                                  lambda c: jnp.exp(cum[last_row(c):last_row(c) + 1, :]))
    o = _bdot(att, v, "nn") + _own_chunk_lanes(_bdot(q_dec, entering, "nt"))
    return o, st1


def _ret_block(forward, q, k, v, rd, st0):
    lg = -jnp.exp(rd[0:1, 0:1])
    rows, cols, _ = _block_pairs()
    pos = jnp.bitwise_and(lax.broadcasted_iota(jnp.int32, (ROW_TILE, 1), 0), CHUNK - 1).astype(F32)
    if forward:
        to_end, from_start, rel = CHUNK - 1.0 - pos, pos + 1.0, (rows - cols).astype(F32)
    else:
        to_end, from_start, rel = pos, CHUNK - pos, (cols - rows).astype(F32)
    mask = _block_mask("lower_incl" if forward else "upper_strict")
    dmat = jnp.where(mask, jnp.exp(jnp.where(mask, rel, 0.0) * lg), 0.0)
    att = _bdot(q, k, "nt") * dmat
    inc_all = _bdot(v, _per_chunk_lanes(k * jnp.exp(to_end * lg)), "tn")
    entering, st1 = _carry_states(forward, st0, inc_all, lambda c: jnp.exp(CHUNK * lg))
    o = _bdot(att, v, "nn") + _own_chunk_lanes(_bdot(q, entering, "nt")) * jnp.exp(from_start * lg)
    return o, st1


def scan(kind, forward, q, k, v, aux, tc, name):
    heads = q.shape[1] // HEAD_PAD
    r_total = q.shape[0]
    nblk = r_total // ROW_TILE
    nctb = tc // ROW_TILE
    block_fn = functools.partial(_gla_block if kind == "gla" else _ret_block, forward)
    per_row_aux = kind == "gla"

    def blk(g):
        if forward:
            return g
        return jnp.where(g < nctb, nctb - 1 - g, nblk - 1 - (g - nctb))

    def specs(step_to_g):
        row = pl.BlockSpec((ROW_TILE, heads * HEAD_PAD), lambda s: (blk(step_to_g(s)), 0))
        aux_spec = row if per_row_aux else pl.BlockSpec((heads, 8, HEAD_PAD), lambda s: (0, 0, 0))
        st = pl.BlockSpec((1, heads, HEAD_PAD, HEAD_PAD), lambda s: (step_to_g(s), 0, 0, 0))
        return row, aux_spec, st

    def head_cols(h):
        return slice(h * HEAD_PAD, (h + 1) * HEAD_PAD)

    def fwd_call(q, k, v, aux):
        row, aux_spec, st_spec = specs(lambda s: s)

        def body(q_ref, k_ref, v_ref, a_ref, o_ref, st0_ref, st_ref):
            @pl.when(pl.program_id(0) == 0)
            def _():
                st_ref[...] = jnp.zeros_like(st_ref)

            qv, kv, vv = q_ref[...], k_ref[...], v_ref[...]
            outs = []
            for h in range(heads):
                st0 = st_ref[h]
                st0_ref[0, h] = st0
                a = a_ref[:, head_cols(h)] if per_row_aux else a_ref[h]
                o, st1 = block_fn(qv[:, head_cols(h)], kv[:, head_cols(h)], vv[:, head_cols(h)], a, st0)
                outs.append(o)
                st_ref[h] = st1
            o_ref[...] = jnp.concatenate(outs, axis=1)

        return pl.pallas_call(
            body, name=name + "_f", grid=(nblk,), in_specs=[row, row, row, aux_spec],
            out_specs=[row, st_spec],
            out_shape=[jax.ShapeDtypeStruct(q.shape, F32),
                       jax.ShapeDtypeStruct((nblk, heads, HEAD_PAD, HEAD_PAD), F32)],
            scratch_shapes=[pltpu.VMEM((heads, HEAD_PAD, HEAD_PAD), F32)],
            compiler_params=_cparams(1),
        )(q, k, v, aux)

    def bwd_call(q, k, v, aux, st0s, do):
        row, aux_spec, st_spec = specs(lambda s: nblk - 1 - s)

        def body(q_ref, k_ref, v_ref, a_ref, st0_ref, do_ref, dq_ref, dk_ref, dv_ref, da_ref, dst_ref):
            s = pl.program_id(0)

            @pl.when(s == 0)
            def _():
                dst_ref[...] = jnp.zeros_like(dst_ref)

            qv, kv, vv, dov = q_ref[...], k_ref[...], v_ref[...], do_ref[...]
            grads = []
            for h in range(heads):
                a = a_ref[:, head_cols(h)] if per_row_aux else a_ref[h]
                _, vjp = jax.vjp(block_fn, qv[:, head_cols(h)], kv[:, head_cols(h)], vv[:, head_cols(h)], a,
                                 st0_ref[0, h])
                dq, dk, dv, da, dst0 = vjp((dov[:, head_cols(h)], dst_ref[h]))
                dst_ref[h] = dst0
                grads.append((dq, dk, dv, da))
            dq_ref[...] = jnp.concatenate([g[0] for g in grads], axis=1)
            dk_ref[...] = jnp.concatenate([g[1] for g in grads], axis=1)
            dv_ref[...] = jnp.concatenate([g[2] for g in grads], axis=1)
            if per_row_aux:
                da_ref[...] = jnp.concatenate([g[3] for g in grads], axis=1)
            else:
                da = jnp.stack([g[3] for g in grads], axis=0)

                @pl.when(s == 0)
                def _():
                    da_ref[...] = da

                @pl.when(s != 0)
                def _():
                    da_ref[...] += da

        return pl.pallas_call(
            body, name=name + "_b", grid=(nblk,),
            in_specs=[row, row, row, aux_spec, st_spec, row],
            out_specs=[row, row, row, aux_spec],
            out_shape=[jax.ShapeDtypeStruct(q.shape, F32)] * 3 + [jax.ShapeDtypeStruct(aux.shape, F32)],
            scratch_shapes=[pltpu.VMEM((heads, HEAD_PAD, HEAD_PAD), F32)],
            compiler_params=_cparams(1),
        )(q, k, v, aux, st0s, do)

    @jax.custom_vjp
    def op(q, k, v, aux):
        return fwd_call(q, k, v, aux)[0]

    def fwd(q, k, v, aux):
        o, st0s = fwd_call(q, k, v, aux)
        return o, (q, k, v, aux, st0s)

    def bwd(res, do):
        return tuple(bwd_call(*res, do))

    op.defvjp(fwd, bwd)
    return op(q, k, v, aux)


HALO = 8


def _neighbours(main, prev8, next8, i, nct, n_tiles):
    has_prev = jnp.logical_and(i != 0, i != nct).astype(F32)
    has_next = jnp.logical_and(i != nct - 1, i != n_tiles - 1).astype(F32)
    row = lax.broadcasted_iota(jnp.int32, main.shape, 0)
    down = jnp.where(row == 0, prev8[HALO - 1:HALO] * has_prev, pltpu.roll(main, 1, 0))
    up = jnp.where(row == ROW_TILE - 1, next8[0:1] * has_next, pltpu.roll(main, ROW_TILE - 1, 0))
    return down, up


def dwconv(x, w8, b, tc, name):
    r_total, width = x.shape
    n_tiles = r_total // ROW_TILE
    nct = tc // ROW_TILE
    per = ROW_TILE // HALO
    main_spec = pl.BlockSpec((ROW_TILE, width), lambda i: (i, 0))
    prev_spec = pl.BlockSpec((HALO, width), lambda i: (jnp.maximum(i * per - 1, 0), 0))
    next_spec = pl.BlockSpec((HALO, width), lambda i: (jnp.minimum((i + 1) * per, r_total // HALO - 1), 0))
    w_spec = pl.BlockSpec((8, width), lambda i: (0, 0))
    b_spec = pl.BlockSpec((1, width), lambda i: (0, 0))

    def fwd_call(x, w8, b):
        def body(x_ref, p_ref, n_ref, w_ref, b_ref, o_ref):
            xv = x_ref[...]
            down, up = _neighbours(xv, p_ref[...], n_ref[...], pl.program_id(0), nct, n_tiles)
            o_ref[...] = w_ref[0:1] * down + w_ref[1:2] * xv + w_ref[2:3] * up + b_ref[...]

        return pl.pallas_call(
            body, name=name + "_f", grid=(n_tiles,), in_specs=[main_spec, prev_spec, next_spec, w_spec, b_spec],
            out_specs=main_spec, out_shape=jax.ShapeDtypeStruct(x.shape, F32), compiler_params=_cparams(1),
        )(x, x, x, w8, b)

    def bwd_call(x, w8, g):
        def body(x_ref, xp_ref, xn_ref, g_ref, gp_ref, gn_ref, w_ref, dx_ref, dw_ref, db_ref):
            i = pl.program_id(0)
            xv, gv = x_ref[...], g_ref[...]
            x_down, x_up = _neighbours(xv, xp_ref[...], xn_ref[...], i, nct, n_tiles)
            g_down, g_up = _neighbours(gv, gp_ref[...], gn_ref[...], i, nct, n_tiles)
            dx_ref[...] = w_ref[0:1] * g_up + w_ref[1:2] * gv + w_ref[2:3] * g_down
            dw = jnp.concatenate([jnp.sum(gv * x_down, axis=0, keepdims=True),
                                  jnp.sum(gv * xv, axis=0, keepdims=True),
                                  jnp.sum(gv * x_up, axis=0, keepdims=True),
                                  jnp.zeros((5, width), F32)], axis=0)
            db = jnp.sum(gv, axis=0, keepdims=True)

            @pl.when(i == 0)
            def _():
                dw_ref[...] = dw
                db_ref[...] = db

            @pl.when(i != 0)
            def _():
                dw_ref[...] += dw
                db_ref[...] += db

        return pl.pallas_call(
            body, name=name + "_b", grid=(n_tiles,),
            in_specs=[main_spec, prev_spec, next_spec, main_spec, prev_spec, next_spec, w_spec],
            out_specs=[main_spec, w_spec, b_spec],
            out_shape=[jax.ShapeDtypeStruct(x.shape, F32), jax.ShapeDtypeStruct((8, width), F32),
                       jax.ShapeDtypeStruct((1, width), F32)],
            compiler_params=_cparams(1),
        )(x, x, x, g, g, g, w8)

    @jax.custom_vjp
    def op(x, w8, b):
        return fwd_call(x, w8, b)

    def fwd(x, w8, b):
        return fwd_call(x, w8, b), (x, w8)

    def bwd(res, g):
        return tuple(bwd_call(*res, g))

    op.defvjp(fwd, bwd)
    return op(x, w8, b)


def loss_head(h, target, tc, name):
    r_total, width = h.shape
    n_tiles = r_total // ROW_TILE
    nct = tc // ROW_TILE

    def call(h, target):
        def body(h_ref, t_ref, dh_ref, loss_ref, acc_ref):
            i = pl.program_id(0)

            @pl.when(i == 0)
            def _():
                acc_ref[...] = jnp.zeros_like(acc_ref)

            @pl.when(i < nct)
            def _():
                dh_ref[...] = jnp.zeros_like(dh_ref)

            @pl.when(i >= nct)
            def _():
                err = h_ref[...] - t_ref[...]
                dh_ref[...] = err * (1.0 / width)
                acc_ref[...] += jnp.sum((err * err).reshape(ROW_TILE // 8, 8, width), axis=0)

            @pl.when(i == n_tiles - 1)
            def _():
                loss_ref[...] = jnp.sum(acc_ref[...]).reshape(1, 1) * (0.5 / width)

        row = pl.BlockSpec((ROW_TILE, width), lambda i: (i, 0))
        return pl.pallas_call(
            body, name=name, grid=(n_tiles,),
            in_specs=[row, pl.BlockSpec((ROW_TILE, width), lambda i: (jnp.maximum(i - nct, 0), 0))],
            out_specs=[row, pl.BlockSpec((1, 1), lambda i: (0, 0))],
            out_shape=[jax.ShapeDtypeStruct(h.shape, F32), jax.ShapeDtypeStruct((1, 1), F32)],
            scratch_shapes=[pltpu.VMEM((8, width), F32)], compiler_params=_cparams(1),
        )(h, target)

    @jax.custom_vjp
    def op(h, target):
        return call(h, target)[1][0, 0]

    def fwd(h, target):
        dh, loss = call(h, target)
        return loss[0, 0], (dh, target)

    def bwd(res, g):
        dh, target = res
        return dh * g, jnp.zeros_like(target)

    op.defvjp(fwd, bwd)
    return op(h, target)


PACK_W = 1024
PACK_TILE = 128


def slab_sum(slabs, name):
    n_slab, n, _ = slabs.shape

    def body(s_ref, o_ref):
        acc = s_ref[0]
        for j in range(1, n_slab):
            acc = acc + s_ref[j]
        o_ref[...] = acc

    return pl.pallas_call(
        body, name=name, grid=(n // PACK_TILE,),
        in_specs=[pl.BlockSpec((n_slab, PACK_TILE, PACK_W), lambda i: (0, i, 0))],
        out_specs=pl.BlockSpec((PACK_TILE, PACK_W), lambda i: (i, 0)),
        out_shape=jax.ShapeDtypeStruct((n, PACK_W), F32), compiler_params=_cparams(1),
    )(slabs)


def adamw(g_slabs, w, m, v, name):
    n_slab, n, _ = g_slabs.shape

    def body(g_ref, w_ref, m_ref, v_ref, go_ref, d_ref, mo_ref, vo_ref):
        g = g_ref[0].astype(F32)
        for j in range(1, n_slab):
            g = g + g_ref[j].astype(F32)
        m_new = ADAM_B1 * m_ref[...] + (1.0 - ADAM_B1) * g
        v_new = ADAM_B2 * v_ref[...] + (1.0 - ADAM_B2) * (g * g)
        m_hat = m_new / (1.0 - ADAM_B1 ** ADAM_STEP)
        v_hat = v_new / (1.0 - ADAM_B2 ** ADAM_STEP)
        go_ref[...] = g
        d_ref[...] = -ADAM_LR * (m_hat / (jnp.sqrt(v_hat) + ADAM_EPS) + ADAM_WD * w_ref[...])
        mo_ref[...] = m_new
        vo_ref[...] = v_new

    flat = pl.BlockSpec((PACK_TILE, PACK_W), lambda i: (i, 0))
    return pl.pallas_call(
        body, name=name, grid=(n // PACK_TILE,),
        in_specs=[pl.BlockSpec((n_slab, PACK_TILE, PACK_W), lambda i: (0, i, 0)), flat, flat, flat],
        out_specs=[flat] * 4, out_shape=[jax.ShapeDtypeStruct((n, PACK_W), F32)] * 4, compiler_params=_cparams(1),
    )(g_slabs, w, m, v)


def all_gather(x, name):
    m_per, n = x.shape

    def body(x_ref, out_ref, send_sems, recv_sems, local_sem):
        px, py, pc = lax.axis_index("x"), lax.axis_index("y"), lax.axis_index("c")
        me, sibling = (px, py, pc), (px, py, 1 - pc)
        chips = [(1 - px, py), (px, 1 - py), (1 - px, 1 - py)]

        def rows(bx, by, bc):
            return out_ref.at[pl.ds((4 * bx + 2 * by + bc) * m_per, m_per), :]

        def copy(k, block, to, src=None):
            return pltpu.make_async_remote_copy(
                src_ref=rows(*block) if src is None else src, dst_ref=rows(*block),
                send_sem=send_sems.at[k], recv_sem=recv_sems.at[k], device_id=to, device_id_type=MESH)

        mine = pltpu.make_async_copy(x_ref, rows(*me), local_sem)
        mine.start()
        first = [copy(0, me, sibling, src=x_ref)]
        first += [copy(1 + j, me, (*chip, pc), src=x_ref) for j, chip in enumerate(chips)]
        for cp in first:
            cp.start()
        passed = [copy(4 + j, (*chip, pc), sibling) for j, chip in enumerate(chips)]
        for j, chip in enumerate(chips):
            copy(1 + j, (*chip, pc), me).wait_recv()
            passed[j].start()
        copy(0, sibling, me).wait_recv()
        for j, chip in enumerate(chips):
            copy(4 + j, (*chip, 1 - pc), me).wait_recv()
        for cp in first + passed:
            cp.wait_send()
        mine.wait()

    return pl.pallas_call(
        body, name=name, out_shape=jax.ShapeDtypeStruct((N_DEV * m_per, n), x.dtype),
        in_specs=[pl.BlockSpec(memory_space=pl.ANY)], out_specs=pl.BlockSpec(memory_space=pl.ANY),
        scratch_shapes=[pltpu.SemaphoreType.DMA((7,)), pltpu.SemaphoreType.DMA((7,)), pltpu.SemaphoreType.DMA],
    )(x)


N_CHIP = 4


def pair_swap(x, name):
    def body(x_ref, out_ref, send_sem, recv_sem):
        sibling = (lax.axis_index("x"), lax.axis_index("y"), 1 - lax.axis_index("c"))
        copy = pltpu.make_async_remote_copy(src_ref=x_ref, dst_ref=out_ref, send_sem=send_sem, recv_sem=recv_sem,
                                            device_id=sibling, device_id_type=MESH)
        copy.start()
        copy.wait()

    return pl.pallas_call(
        body, name=name, out_shape=jax.ShapeDtypeStruct(x.shape, x.dtype),
        in_specs=[pl.BlockSpec(memory_space=pl.ANY)], out_specs=pl.BlockSpec(memory_space=pl.ANY),
        scratch_shapes=[pltpu.SemaphoreType.DMA, pltpu.SemaphoreType.DMA],
    )(x)


def pair_add(a, b, name):
    n_slab, n, _ = a.shape

    def body(a_ref, b_ref, o_ref):
        o_ref[...] = (a_ref[...].astype(F32) + b_ref[...].astype(F32)).astype(o_ref.dtype)

    spec = pl.BlockSpec((1, PACK_TILE, PACK_W), lambda s, i: (s, i, 0))
    return pl.pallas_call(
        body, name=name, grid=(n_slab, n // PACK_TILE), in_specs=[spec, spec], out_specs=spec,
        out_shape=jax.ShapeDtypeStruct(a.shape, a.dtype), compiler_params=_cparams(2),
    )(a, b)


def chip_all_to_all(x, name):
    def body(x_ref, out_ref, send_sems, recv_sems, local_sem):
        px, py, pc = lax.axis_index("x"), lax.axis_index("y"), lax.axis_index("c")
        mine_idx = 2 * px + py
        local = pltpu.make_async_copy(x_ref.at[mine_idx], out_ref.at[mine_idx], local_sem)
        local.start()
        copies = []
        for k, (fx, fy) in enumerate(((0, 1), (1, 0), (1, 1))):
            qx, qy = px ^ fx, py ^ fy
            peer_idx = 2 * qx + qy
            copies.append((
                pltpu.make_async_remote_copy(
                    src_ref=x_ref.at[peer_idx], dst_ref=out_ref.at[mine_idx], send_sem=send_sems.at[k],
                    recv_sem=recv_sems.at[k], device_id=(qx, qy, pc), device_id_type=MESH),
                pltpu.make_async_remote_copy(
                    src_ref=x_ref.at[peer_idx], dst_ref=out_ref.at[peer_idx], send_sem=send_sems.at[k],
                    recv_sem=recv_sems.at[k], device_id=(qx, qy, pc), device_id_type=MESH)))
        for send, _ in copies:
            send.start()
        for _, landing in copies:
            landing.wait_recv()
        for send, _ in copies:
            send.wait_send()
        local.wait()

    return pl.pallas_call(
        body, name=name, out_shape=jax.ShapeDtypeStruct(x.shape, x.dtype),
        in_specs=[pl.BlockSpec(memory_space=pl.ANY)], out_specs=pl.BlockSpec(memory_space=pl.ANY),
        scratch_shapes=[pltpu.SemaphoreType.DMA((3,)), pltpu.SemaphoreType.DMA((3,)), pltpu.SemaphoreType.DMA],
    )(x)


IN_OFFSETS = {}
_off = 0
for _name, _width in (("mla_q", 256), ("mla_kv", 128), ("mla_kr", 32), ("gla_q", 512), ("gla_k", 512), ("gla_v", 512),
                      ("gla_g", 512), ("gla_rf", 16), ("gla_rb", 16), ("ret_q", 512), ("ret_k", 512), ("ret_v", 512),
                      ("ret_g", 512), ("gate_mla", 1024), ("gate_gla", 1024), ("gate_ret", 1024)):
    IN_OFFSETS[_name] = (_off, _off + _width)
    _off += _width
N_IN = _off

P_GLA, P_RET, P_GATE, P_MLAQ, P_MLAKV, P_MLAKR, P_RANK, P_END = 0, 2048, 4096, 7168, 7424, 7552, 7680, 7808


def _pad_in_proj(w):
    def cols(a, b):
        return w[:, IN_OFFSETS[a][0]:IN_OFFSETS[b][1]]

    def z(n):
        return jnp.zeros((w.shape[0], n), w.dtype)

    return jnp.concatenate([cols("gla_q", "gla_g"), cols("ret_q", "ret_g"), cols("gate_mla", "gate_ret"),
                            cols("mla_q", "mla_kv"), z(MLA_NOPE), cols("mla_kr", "mla_kr"),
                            z(HEAD_PAD - MLA_QK), cols("gla_rf", "gla_rb"), z(HEAD_PAD - 2 * GLA_RANK),
                            z(N_IN_PAD - P_END)], axis=1)


def _pad_last(a, n):
    return jnp.pad(a, [(0, 0)] * (a.ndim - 1) + [(0, n - a.shape[-1])])


def _position_tables(tc, t):
    pos = jnp.arange(t)
    inv = ROPE_THETA ** (-jnp.arange(MLA_ROPE // 4, dtype=F32) * 2.0 / (MLA_ROPE // 2))
    ang_r = (pos // GRID_W).astype(F32)[:, None] * inv[None, :]
    ang_c = (pos % GRID_W).astype(F32)[:, None] * inv[None, :]
    z8, z32, z64 = jnp.zeros((t, 8), F32), jnp.zeros((t, 32), F32), jnp.zeros((t, 64), F32)
    lat_c = jnp.concatenate([jnp.ones((t, 64), F32), jnp.cos(ang_r), jnp.cos(ang_r), jnp.cos(ang_c), jnp.cos(ang_c),
                             z32], axis=1)
    lat_sn = jnp.concatenate([z64, -jnp.sin(ang_r), z8, -jnp.sin(ang_c), z8, z32], axis=1)
    lat_sp = jnp.concatenate([z64, z8, jnp.sin(ang_r), z8, jnp.sin(ang_c), z32], axis=1)
    ctx_c = jnp.concatenate([jnp.ones((tc, MLA_QK), F32), jnp.zeros((tc, HEAD_PAD - MLA_QK), F32)], axis=1)
    ctx_z = jnp.zeros((tc, HEAD_PAD), F32)
    rinv = 1.0 / (RET_THETA ** jnp.linspace(0.0, 1.0, RET_DK // 2, dtype=F32))
    rang = jnp.arange(tc + t).astype(F32)[:, None] * rinv[None, :]
    return dict(c=jnp.concatenate([ctx_c, lat_c]), sn=jnp.concatenate([ctx_z, lat_sn]),
                sp=jnp.concatenate([ctx_z, lat_sp]),
                rc=jnp.concatenate([jnp.cos(rang), jnp.cos(rang)], axis=1),
                rs=jnp.concatenate([-jnp.sin(rang), jnp.sin(rang)], axis=1))


def _heads(x):
    return [x[:, h * HEAD_PAD:(h + 1) * HEAD_PAD] for h in range(x.shape[1] // HEAD_PAD)]


def _mla_rope(x, c, sn, sp):
    return x * c + _roll(x, HEAD_PAD - 8, 1) * sn + _roll(x, 8, 1) * sp


def _norm_mod_fn(shift_row, scale_row):
    def fn(h, mod, w):
        return (_rms(h, D, w) * (1.0 + mod[scale_row:scale_row + 1]) + mod[shift_row:shift_row + 1],)
    return fn


def _resid_fn(gate_row):
    def fn(h, y, mod):
        return (h + mod[gate_row:gate_row + 1] * y,)
    return fn


def _mla_prep_fn(x, c, sn, sp, q_norm_a, w_qb, q_norm, kv_norm_a, w_k, w_v, k_norm):
    cq, ckv = x[:, :MLA_Q_LORA], x[:, MLA_Q_LORA:MLA_Q_LORA + MLA_KV_LORA]
    kr = x[:, MLA_Q_LORA + MLA_KV_LORA:]
    qf = _bdot(_rms(cq, MLA_Q_LORA, q_norm_a), w_qb, "nn")
    q = jnp.concatenate([_mla_rope(_rms(qh, MLA_QK, q_norm), c, sn, sp) for qh in _heads(qf)], axis=1)
    xkv = _rms(ckv, MLA_KV_LORA, kv_norm_a)
    kf = _bdot(xkv, w_k, "nn")
    k = jnp.concatenate([_mla_rope(_rms(kh + kr, MLA_QK, k_norm), c, sn, sp) for kh in _heads(kf)], axis=1)
    return q, k, _bdot(xkv, w_v, "nn")


def _decay_fn(x, w2, b):
    la = _log_sigmoid(_bdot(x[:, :HEAD_PAD], w2, "nn") + b) * (1.0 / GLA_NORMALIZER)
    return la[:, :GLA_HEADS * GLA_DK], la[:, GLA_HEADS * GLA_DK:]


def _ret_rot_fn(q, k, rc, rs):
    def rot(x, scale):
        return jnp.concatenate([(xh * rc + _roll(xh, RET_DK // 2, 1) * rs) * scale for xh in _heads(x)], axis=1)
    return rot(q, 1.0), rot(k, RET_DK ** -0.5)


def _gla_out_fn(o_f, o_b, g, w):
    y = jnp.concatenate([_rms(oh, HEAD_PAD, w) for oh in _heads(o_f + o_b)], axis=1)
    return (y * _silu(g),)


def _ret_out_fn(o_f, o_b, g):
    y = jnp.concatenate([_rms(oh, HEAD_PAD) for oh in _heads(o_f + o_b)], axis=1)
    return (y * _silu(g),)


def _merge_fn(z0, z1, z2, g0a, g0b, g1a, g1b, g2a, g2b, bg):
    out = 0.0
    for n, (z, ga, gb) in enumerate(((z0, g0a, g0b), (z1, g1a, g1b), (z2, g2a, g2b))):
        out = out + jax.nn.sigmoid(jnp.concatenate([ga, gb], axis=1) + bg[n:n + 1]) * z
    return (out,)


def _ffn_act_fn(c, up):
    return (_gelu_tanh(c) * up,)


def _layer(l, h, mod, w, tabs, tc):
    nct = tc // ROW_TILE
    tag = f"_l{l}"
    row = lambda a: a[l][None]
    pieces = rowwise("norm1" + tag, _norm_mod_fn(0, 1), [h], [mod], [row(w["norm1_w"])], [D], nct,
                     then=("pieces", _pad_in_proj(w["w_in"][l]), "in_proj" + tag))
    piece = lambda start: pieces[start // PIECE_W]

    w_qb = _pad_last(w["mla_w_qb"][l].reshape(MLA_Q_LORA, MLA_HEADS, MLA_QK), HEAD_PAD).reshape(MLA_Q_LORA, -1)
    w_kvb = w["mla_w_kvb"][l].reshape(MLA_KV_LORA, MLA_HEADS, MLA_NOPE + MLA_V)
    w_k = _pad_last(w_kvb[:, :, :MLA_NOPE], HEAD_PAD).reshape(MLA_KV_LORA, -1)
    w_v = _pad_last(w_kvb[:, :, MLA_NOPE:], HEAD_PAD).reshape(MLA_KV_LORA, -1)
    rope = [tabs["c"], tabs["sn"], tabs["sp"]]
    q, k, v = rowwise("mla_prep" + tag, _mla_prep_fn, [piece(P_MLAQ)] + rope, [],
                      [row(w["mla_q_norm_a"]), w_qb, _pad_last(row(w["mla_q_norm"]), HEAD_PAD),
                       row(w["mla_kv_norm_a"]), w_k, w_v, _pad_last(row(w["mla_k_norm"]), HEAD_PAD)],
                      [MLA_HEADS * HEAD_PAD] * 3, nct, diff_rows=[True, False, False, False])
    y_mla = attention(q, k, v, tc, "attn" + tag)
    wb_mla = _pad_last(w["w_branch"][l, 0].reshape(MLA_HEADS, MLA_V, D).transpose(0, 2, 1), HEAD_PAD)
    wb_mla = wb_mla.transpose(0, 2, 1).reshape(MLA_HEADS * HEAD_PAD, D)

    w2 = jnp.zeros((HEAD_PAD, 2 * GLA_HEADS * GLA_DK), F32)
    w2 = w2.at[:GLA_RANK, :GLA_HEADS * GLA_DK].set(w["gla_w_gk2"][l, 0])
    w2 = w2.at[GLA_RANK:2 * GLA_RANK, GLA_HEADS * GLA_DK:].set(w["gla_w_gk2"][l, 1])
    la_f, la_b = rowwise("gla_decay" + tag, _decay_fn, [piece(P_RANK)], [],
                         [w2, w["gla_b_gk"][l].reshape(1, -1)], [GLA_HEADS * GLA_DK] * 2, nct)
    gq, gk, gv, gg = [piece(P_GLA + n * PIECE_W) for n in range(4)]
    o_f = scan("gla", True, gq, gk, gv, la_f, tc, "gla_fw" + tag)
    o_b = scan("gla", False, gq, gk, gv, la_b, tc, "gla_bw" + tag)
    z_gla = rowwise("gla_out" + tag, _gla_out_fn, [o_f, o_b, gg], [], [row(w["gla_o_norm"])], [512], nct,
                    then=("linear", w["w_branch"][l, 1], "branch_gla" + tag))

    rq, rk = rowwise("ret_rot" + tag, _ret_rot_fn, [piece(P_RET), piece(P_RET + PIECE_W), tabs["rc"], tabs["rs"]],
                     [], [], [512, 512], nct, diff_rows=[True, True, False, False])
    rv, rg = piece(P_RET + 2 * PIECE_W), piece(P_RET + 3 * PIECE_W)
    rd = jnp.broadcast_to(w["ret_decay"][l][:, :, None, None], (2, RET_HEADS, 8, HEAD_PAD))
    r_f = scan("ret", True, rq, rk, rv, rd[0], tc, "ret_fw" + tag)
    r_b = scan("ret", False, rq, rk, rv, rd[1], tc, "ret_bw" + tag)
    z_ret = rowwise("ret_out" + tag, _ret_out_fn, [r_f, r_b, rg], [], [], [512], nct,
                    then=("linear", w["w_branch"][l, 2], "branch_ret" + tag))

    z = [linear(y_mla, wb_mla, "branch_mla" + tag), z_gla, z_ret]
    gates = [piece(P_GATE + n * PIECE_W) for n in range(6)]
    y = rowwise("merge" + tag, _merge_fn, z + gates, [], [_pad_rows(w["b_gate"][l], 8)], [D], nct,
                then=("linear", w["w_out"][l], "w_out" + tag))
    h = rowwise("resid1" + tag, _resid_fn(2), [h, y], [mod], [], [D], nct)[0]

    a2 = rowwise("norm2" + tag, _norm_mod_fn(3, 4), [h], [mod], [row(w["norm2_w"])], [D], nct)[0]
    gate = linear(a2, w["w_ffn_in"][l][:, :D_FF], "ffn_gate" + tag)
    up = linear(a2, w["w_ffn_in"][l][:, D_FF:], "ffn_up" + tag)
    conv = dwconv(gate, _pad_rows(w["w_dw"][l], 8), row(w["b_dw"]), tc, "dwconv" + tag)
    f = rowwise("ffn_act" + tag, _ffn_act_fn, [conv, up], [], [], [D_FF], nct,
                then=("linear", w["w_ffn_out"][l], "ffn_out" + tag))
    return rowwise("resid2" + tag, _resid_fn(5), [h, f], [mod], [], [D], nct)[0]


def _pad_rows(a, n):
    return jnp.pad(a, [(0, n - a.shape[0])] + [(0, 0)] * (a.ndim - 1))


def local_loss(w, mod, x, ctx, target):
    tc, t = ctx.shape[0], x.shape[0]
    tabs = _position_tables(tc, t)
    h = jnp.concatenate([ctx, x], axis=0)
    for l in range(DEPTH):
        h = _layer(l, h, mod[l], w, tabs, tc)
    return loss_head(h, target, tc, "loss_head")


ADA_ROWS = 16


def ada_forward(cond_in, w_ada, b_loc):
    cols = w_ada.shape[2]

    def body(x_ref, w_ref, b_ref, o_ref):
        s = _silu(x_ref[...])
        for l in range(DEPTH):
            o_ref[l] = _dg(s, w_ref[l], "nn") + b_ref[l]

    return pl.pallas_call(
        body, name="ada_forward", out_shape=jax.ShapeDtypeStruct((DEPTH, ADA_ROWS, cols), F32),
        compiler_params=pltpu.CompilerParams(vmem_limit_bytes=VMEM_LIMIT_BYTES),
    )(cond_in, w_ada, b_loc)


def ada_backward(cond_in, g_loc, dmod_own, w_ada):
    cols = w_ada.shape[2]

    def body(x_ref, g_ref, own_ref, w_ref, gw_ref, dc_ref, gb_ref):
        x = x_ref[...]
        s = _silu(x)
        dcond = jnp.zeros((8, D), F32)
        for l in range(DEPTH):
            g_ctx = jnp.sum(g_ref[2 * l], axis=0, keepdims=True)
            g_rows = jnp.concatenate([g_ref[2 * l + 1], jnp.broadcast_to(g_ctx, (8, cols))], axis=0)
            keep = lax.broadcasted_iota(jnp.int32, (ADA_ROWS, cols), 0) <= N_DEV
            gw_ref[l] = _dg(s, jnp.where(keep, g_rows, 0.0), "tn")
            dcond = dcond + _dg(jnp.broadcast_to(g_ctx, (8, cols)), w_ref[l], "nt")
            gb_ref[l:l + 1, :] = own_ref[2 * l:2 * l + 1, :] + own_ref[2 * l + 1:2 * l + 2, :]
        xc = x[N_DEV:N_DEV + 1]
        sig = jax.nn.sigmoid(xc)
        dc_ref[...] = dcond[0:1] * (sig * (1.0 + xc * (1.0 - sig)))

    return pl.pallas_call(
        body, name="ada_backward",
        out_shape=[jax.ShapeDtypeStruct(w_ada.shape, F32), jax.ShapeDtypeStruct((1, D), F32),
                   jax.ShapeDtypeStruct((DEPTH, 6 * D), F32)],
        compiler_params=pltpu.CompilerParams(vmem_limit_bytes=VMEM_LIMIT_BYTES),
    )(cond_in, g_loc, dmod_own, w_ada)


WEIGHTS = ["c_ctx", "w_ada", "b_ada", "norm1_w", "norm2_w", "w_in", "b_gate", "mla_q_norm_a", "mla_w_qb",
           "mla_kv_norm_a", "mla_w_kvb", "mla_q_norm", "mla_k_norm", "gla_w_gk2", "gla_b_gk", "gla_o_norm",
           "ret_decay", "w_branch", "w_out", "w_ffn_in", "w_dw", "b_dw", "w_ffn_out"]
INPUTS = ["x", "c", "ctx"] + WEIGHTS + ["loss_target"] + ["m_" + n for n in WEIGHTS] + ["v_" + n for n in WEIGHTS]
BIG = {"w_in": 2, "mla_w_qb": 2, "mla_w_kvb": 2, "w_branch": 3, "w_out": 1, "w_ffn_in": 2, "w_ffn_out": 1}
SMALL_SHARDED = {"b_gate": 2, "gla_w_gk2": 3, "gla_b_gk": 2, "w_dw": 2}
SMALL = ["c_ctx", "b_ada", "norm1_w", "norm2_w", "b_gate", "mla_q_norm_a", "mla_kv_norm_a", "mla_q_norm", "mla_k_norm",
         "gla_w_gk2", "gla_b_gk", "gla_o_norm", "ret_decay", "w_dw", "b_dw"]


def _entry_rows(size, align):
    return -(-size // (PACK_W * align)) * align


def _pack(arrays, rows, dtype, align, lead=0):
    parts = []
    for a in arrays:
        head = a.shape[:lead]
        size = math.prod(a.shape[lead:])
        r = _entry_rows(size, align)
        if r * PACK_W == size:
            parts.append(a.astype(dtype).reshape(head + (r, PACK_W)))
        else:
            flat = jnp.pad(a.astype(dtype).reshape(head + (size,)), [(0, 0)] * lead + [(0, r * PACK_W - size)])
            parts.append(flat.reshape(head + (r, PACK_W)))
    used = sum(p.shape[lead] for p in parts)
    if rows > used:
        parts.append(jnp.zeros(parts[0].shape[:lead] + (rows - used, PACK_W), dtype))
    return jnp.concatenate(parts, axis=lead)


def _pack_rows(shapes, align, multiple):
    used = sum(_entry_rows(math.prod(s), align) for s in shapes)
    return -(-used // multiple) * multiple


def _unpack(pack, shapes, align):
    head = pack.shape[:-2]
    out, off = [], 0
    for shape in shapes:
        size = math.prod(shape)
        r = _entry_rows(size, align)
        block = lax.slice_in_dim(pack, off, off + r, axis=len(head))
        if r * PACK_W != size:
            block = block.reshape(head + (r * PACK_W,))[..., :size]
        out.append(block.reshape(head + tuple(shape)))
        off += r
    return out


def _join_shards(stacked, axis):
    moved = jnp.moveaxis(stacked, 0, axis)
    shape = list(moved.shape)
    return moved.reshape(shape[:axis] + [shape[axis] * shape[axis + 1]] + shape[axis + 2:])


def _split_shards(full, axis):
    shape = list(full.shape)
    split = full.reshape(shape[:axis] + [N_DEV, shape[axis] // N_DEV] + shape[axis + 1:])
    return jnp.moveaxis(split, axis, 0)


def _gather_shards(local, axes, dtype, rows_multiple, name):
    names = list(axes)
    shapes = [local[n].shape for n in names]
    rows = _pack_rows(shapes, rows_multiple, rows_multiple)
    gathered = all_gather(_pack([local[n] for n in names], rows, dtype, rows_multiple), name)
    stacked = _unpack(gathered.reshape(N_DEV, rows, PACK_W), shapes, rows_multiple)
    return {n: _join_shards(s, axes[n]).astype(F32) for n, s in zip(names, stacked)}


def kernel(*args):
    a = dict(zip(INPUTS, args))
    me = 4 * lax.axis_index("x") + 2 * lax.axis_index("y") + lax.axis_index("c")
    cols = a["w_ada"].shape[2]

    small_names = list(SMALL_SHARDED)
    small_local = [a[n].shape for n in small_names]
    first_rows = _pack_rows([a["c"].shape] + small_local, 8, 8)
    first = all_gather(_pack([a["c"]] + [a[n] for n in small_names], first_rows, F32, 8), "gather_small")
    first = _unpack(first.reshape(N_DEV, first_rows, PACK_W), [a["c"].shape] + small_local, 8)
    c_all = first[0][:, 0]

    cond_in = jnp.concatenate([c_all, a["c_ctx"][None], jnp.zeros((ADA_ROWS - N_DEV - 1, D), F32)], axis=0)
    b_loc = lax.dynamic_slice_in_dim(a["b_ada"], me * cols, cols, axis=1)[:, None, :]
    mod_loc = ada_forward(cond_in, a["w_ada"], b_loc)
    mod_all = all_gather(mod_loc.reshape(DEPTH * ADA_ROWS, cols), "gather_mod")
    mod_all = mod_all.reshape(N_DEV, DEPTH, ADA_ROWS, cols).transpose(1, 2, 0, 3).reshape(DEPTH, ADA_ROWS, 6, D)
    mod_me = lax.dynamic_index_in_dim(mod_all, me, axis=1, keepdims=False)
    mod = jnp.pad(jnp.stack([mod_all[:, N_DEV], mod_me], axis=1), ((0, 0), (0, 0), (0, 2), (0, 0)))

    w = _gather_shards(a, BIG, BF16, 16, "gather_weights")
    w.update({n: _join_shards(s, SMALL_SHARDED[n]) for n, s in zip(small_names, first[1:])})
    for n in SMALL:
        if n not in SMALL_SHARDED and n not in ("c_ctx", "b_ada"):
            w[n] = a[n]

    loss, (gw, gmod, gx) = jax.value_and_grad(local_loss, argnums=(0, 1, 2))(
        w, mod, a["x"][0], a["ctx"][0], a["loss_target"][0])
    loss = lax.psum(loss, ("x", "y", "c"))

    dmod_own = gmod[:, :, :6].reshape(2 * DEPTH, 6 * D)
    g_all = all_gather(jnp.pad(dmod_own, ((0, 8 - 2 * DEPTH), (0, 0))), "gather_dmod").reshape(N_DEV, 8, 6 * D)
    g_loc = lax.dynamic_slice_in_dim(g_all[:, :2 * DEPTH], me * cols, cols, axis=2).transpose(1, 0, 2)
    g_w_ada, g_c_ctx, g_b_ada = ada_backward(cond_in, g_loc, dmod_own, a["w_ada"])

    small_part = dict(gw, c_ctx=g_c_ctx, b_ada=g_b_ada)
    small_shapes = [a[n].shape if n not in SMALL_SHARDED else gw[n].shape for n in SMALL]
    rows = _pack_rows(small_shapes, 8, PACK_TILE)
    parts = all_gather(_pack([small_part[n] for n in SMALL], rows, F32, 8), "gather_small_grads")
    small_sum = _unpack(slab_sum(parts.reshape(N_DEV, rows, PACK_W), "sum_small_grads"), small_shapes, 8)
    g_small = {}
    for n, g in zip(SMALL, small_sum):
        if n in SMALL_SHARDED:
            ax = SMALL_SHARDED[n]
            g = lax.dynamic_slice_in_dim(g, me * a[n].shape[ax], a[n].shape[ax], axis=ax)
        g_small[n] = g

    big_rows = _pack_rows([a[n].shape for n in BIG], 16, PACK_TILE)
    slabs = _pack([_split_shards(gw[n], ax) for n, ax in BIG.items()], big_rows, BF16, 16, lead=1)
    by_core = slabs.reshape(N_CHIP, 2, big_rows, PACK_W)
    my_core = lax.axis_index("c")
    keep = lax.dynamic_index_in_dim(by_core, my_core, axis=1, keepdims=False)
    give = lax.dynamic_index_in_dim(by_core, 1 - my_core, axis=1, keepdims=False)
    pair_sum = pair_add(keep, pair_swap(give, "swap_grads"), "add_pair_grads")
    landed = chip_all_to_all(pair_sum, "scatter_grads")

    def update(names, g_slabs, rows, align, label):
        shapes = [a[n].shape for n in names]
        packs = [_pack([a[pre + n] for n in names], rows, F32, align) for pre in ("", "m_", "v_")]
        outs = adamw(g_slabs, *packs, label)
        return [dict(zip(names, _unpack(o, shapes, align))) for o in outs]

    res_big = update(list(BIG), landed, big_rows, 16, "adamw_big")
    ada_rows = _pack_rows([a["w_ada"].shape], 8, PACK_TILE)
    res_ada = update(["w_ada"], _pack([g_w_ada], ada_rows, F32, 8)[None], ada_rows, 8, "adamw_ada")
    small_rows = _pack_rows([a[n].shape for n in SMALL], 8, PACK_TILE)
    res_small = update(SMALL, _pack([g_small[n] for n in SMALL], small_rows, F32, 8)[None], small_rows, 8,
                       "adamw_small")

    outs = [loss, gx[None]]
    for k in range(4):
        merged = {**res_big[k], **res_ada[k], **res_small[k]}
        outs += [merged[n] for n in WEIGHTS]
    return tuple(outs)
```

```python
import functools
import math

import jax
import jax.numpy as jnp
import numpy as np
from jax import lax
from jax.experimental import pallas as pl
from jax.experimental.pallas import tpu as pltpu

F32 = jnp.float32
BF16 = jnp.bfloat16

N_DEV = 8
D = 1024
DEPTH = 2
GRID_W = 64
MLA_HEADS = 8
MLA_NOPE = 64
MLA_ROPE = 32
MLA_QK = 96
MLA_V = 64
MLA_Q_LORA = 256
MLA_KV_LORA = 128
GLA_HEADS = 4
GLA_DK = 128
GLA_RANK = 16
GLA_NORMALIZER = 16.0
RET_HEADS = 4
RET_DK = 128
BRANCH_W = 512
D_FF = 2816
CHUNK = 64
ROPE_THETA = 10000.0
RET_THETA = 10000.0
EPS = 1e-6
HEAD_PAD = 128
N_IN_PAD = 8192

ADAM_LR = 0.001
ADAM_B1 = 0.9
ADAM_B2 = 0.999
ADAM_EPS = 1e-08
ADAM_WD = 0.01
ADAM_STEP = 10

ROW_TILE = 256
SCAN_CHUNKS = ROW_TILE // CHUNK
VMEM_LIMIT_BYTES = 56 * 1024 * 1024
MESH = pl.DeviceIdType.MESH


def _cparams(n_axes):
    return pltpu.CompilerParams(dimension_semantics=("arbitrary",) * n_axes, vmem_limit_bytes=VMEM_LIMIT_BYTES)


def _pick(dim, cands):
    for cand in cands:
        if dim % cand == 0:
            return cand
    return dim


_DOT_DIMS = {"nn": (((1,), (0,)), ((), ())), "nt": (((1,), (1,)), ((), ())), "tn": (((0,), (0,)), ((), ()))}


def _dg(a, b, mode):
    return lax.dot_general(a.astype(BF16), b.astype(BF16), _DOT_DIMS[mode], preferred_element_type=F32)


def _bdot(a, b, mode):
    @jax.custom_vjp
    def f(a, b):
        return _dg(a, b, mode)

    def fwd(a, b):
        return _dg(a, b, mode), (a, b)

    def bwd(res, g):
        a, b = res
        if mode == "nn":
            return _dg(g, b, "nt").astype(a.dtype), _dg(a, g, "tn").astype(b.dtype)
        if mode == "nt":
            return _dg(g, b, "nn").astype(a.dtype), _dg(g, a, "tn").astype(b.dtype)
        return _dg(b, g, "nt").astype(a.dtype), _dg(a, g, "nn").astype(b.dtype)

    f.defvjp(fwd, bwd)
    return f(a, b)


def _roll(x, shift, axis):
    n = x.shape[axis]
    shift = shift % n

    @jax.custom_vjp
    def f(x):
        return pltpu.roll(x, shift, axis)

    def fwd(x):
        return pltpu.roll(x, shift, axis), None

    def bwd(_, g):
        return (pltpu.roll(g, (n - shift) % n, axis),)

    f.defvjp(fwd, bwd)
    return f(x)


@jax.custom_jvp
def _log_sigmoid(x):
    return jnp.minimum(x, 0.0) - jnp.log(1.0 + jnp.exp(-jnp.abs(x)))


@_log_sigmoid.defjvp
def _log_sigmoid_jvp(primals, tangents):
    (x,), (t,) = primals, tangents
    return _log_sigmoid(x), t * jax.nn.sigmoid(-x)


def _rms(x, n, w=None):
    y = x * lax.rsqrt(jnp.sum(x * x, axis=-1, keepdims=True) * (1.0 / n) + EPS)
    return y if w is None else y * w


def _silu(x):
    return x * jax.nn.sigmoid(x)


def _gelu_tanh(x):
    return 0.5 * x * (1.0 + jnp.tanh(math.sqrt(2.0 / math.pi) * (x + 0.044715 * (x * x * x))))


def _mm(a, b, mode, name):
    if mode == "nn":
        (m, k), (_, n) = a.shape, b.shape
    elif mode == "nt":
        (m, k), (n, _) = a.shape, b.shape
    else:
        (k, m), (_, n) = a.shape, b.shape
    tm = _pick(m, (1024, 768, 1408, 512, 256, 128))
    tn = _pick(n, (1024, 1408, 512, 256, 128))
    tk = _pick(k, (1024, 768, 1408, 512, 256, 128))
    nk = k // tk
    if mode == "nn":
        a_spec = pl.BlockSpec((tm, tk), lambda i, j, kk: (i, kk))
        b_spec = pl.BlockSpec((tk, tn), lambda i, j, kk: (kk, j))
    elif mode == "nt":
        a_spec = pl.BlockSpec((tm, tk), lambda i, j, kk: (i, kk))
        b_spec = pl.BlockSpec((tn, tk), lambda i, j, kk: (j, kk))
    else:
        a_spec = pl.BlockSpec((tk, tm), lambda i, j, kk: (kk, i))
        b_spec = pl.BlockSpec((tk, tn), lambda i, j, kk: (kk, j))

    def body(a_ref, b_ref, o_ref):
        kk = pl.program_id(2)
        part = _dg(a_ref[...], b_ref[...], mode)
        if nk == 1:
            o_ref[...] = part
        else:
            @pl.when(kk == 0)
            def _():
                o_ref[...] = part

            @pl.when(kk != 0)
            def _():
                o_ref[...] += part

    return pl.pallas_call(
        body, name=name, grid=(m // tm, n // tn, nk),
        in_specs=[a_spec, b_spec], out_specs=pl.BlockSpec((tm, tn), lambda i, j, kk: (i, j)),
        out_shape=jax.ShapeDtypeStruct((m, n), F32),
        compiler_params=_cparams(3),
    )(a, b)


def linear(x, w, name):
    @jax.custom_vjp
    def op(x, w):
        return _mm(x, w.astype(BF16), "nn", name + "_f")

    def fwd(x, w):
        wb = w.astype(BF16)
        return _mm(x, wb, "nn", name + "_f"), (x, wb)

    def bwd(res, g):
        x, wb = res
        return _mm(g, wb, "nt", name + "_dx"), _mm(x, g, "tn", name + "_dw")

    op.defvjp(fwd, bwd)
    return op(x, w)


PIECE_W = 512
PIECE_ROWS = 384


def _mm_split(a, wb, name):
    (r, k), n = a.shape, wb.shape[1] // PIECE_W
    tm = _pick(r, (PIECE_ROWS, ROW_TILE))
    width = PIECE_GROUP * PIECE_W

    n_tiles = r // tm

    def body(a_ref, w_ref, *out_refs):
        j = pl.program_id(0)
        res = _dg(a_ref[...], w_ref[...], "nn")
        for group in range(n // PIECE_GROUP):
            @pl.when(j == group)
            def _(group=group):
                for p in range(PIECE_GROUP):
                    out_refs[group * PIECE_GROUP + p][...] = res[:, p * PIECE_W:(p + 1) * PIECE_W]

    def out_spec(jj):
        group = jj // PIECE_GROUP
        return pl.BlockSpec((tm, PIECE_W),
                            lambda j, i: (jnp.where(j == group, i, jnp.where(j < group, 0, n_tiles - 1)), 0))

    return pl.pallas_call(
        body, name=name, grid=(n // PIECE_GROUP, n_tiles),
        in_specs=[pl.BlockSpec((tm, k), lambda j, i: (i, 0)), pl.BlockSpec((k, width), lambda j, i: (0, j))],
        out_specs=[out_spec(jj) for jj in range(n)],
        out_shape=[jax.ShapeDtypeStruct((r, PIECE_W), F32)] * n, compiler_params=_cparams(2),
    )(a, wb)


def _mm_join(gs, wb, name):
    n, (r, _), k = len(gs), gs[0].shape, wb.shape[0]
    tm = _pick(r, (PIECE_ROWS, ROW_TILE))
    n_groups = n // PIECE_GROUP

    def body(*refs):
        g_refs, w_ref, o_ref = refs[:n], refs[n], refs[n + 1]
        j = pl.program_id(1)
        for group in range(n_groups):
            @pl.when(j == group)
            def _(group=group):
                g = jnp.concatenate([g_refs[group * PIECE_GROUP + p][...].astype(BF16) for p in range(PIECE_GROUP)],
                                    axis=1)
                part = _dg(g, w_ref[...], "nt")
                if group == 0:
                    o_ref[...] = part
                else:
                    o_ref[...] += part

    return pl.pallas_call(
        body, name=name, grid=(r // tm, n_groups),
        in_specs=[pl.BlockSpec((tm, PIECE_W), lambda i, j: (i, 0))] * n
        + [pl.BlockSpec((k, PIECE_GROUP * PIECE_W), lambda i, j: (0, j))],
        out_specs=pl.BlockSpec((tm, k), lambda i, j: (i, 0)), out_shape=jax.ShapeDtypeStruct((r, k), F32),
        compiler_params=_cparams(2),
    )(*gs, wb)


PIECE_GROUP = 4


def _mm_join_tn(a, gs, name):
    n, (r, k) = len(gs), a.shape
    tk = _pick(r, (768, 512, ROW_TILE))
    width = PIECE_GROUP * PIECE_W

    def group_call(group):
        def body(a_ref, *refs):
            g_refs, o_ref = refs[:PIECE_GROUP], refs[PIECE_GROUP]
            part = _dg(a_ref[...], jnp.concatenate([g_ref[...].astype(BF16) for g_ref in g_refs], axis=1), "tn")

            @pl.when(pl.program_id(0) == 0)
            def _():
                o_ref[...] = part

            @pl.when(pl.program_id(0) != 0)
            def _():
                o_ref[...] += part

        return pl.pallas_call(
            body, name=f"{name}{group}", grid=(r // tk,),
            in_specs=[pl.BlockSpec((tk, k), lambda kk: (kk, 0))]
            + [pl.BlockSpec((tk, PIECE_W), lambda kk: (kk, 0))] * PIECE_GROUP,
            out_specs=pl.BlockSpec((k, width), lambda kk: (0, 0)),
            out_shape=jax.ShapeDtypeStruct((k, width), F32), compiler_params=_cparams(1),
        )(a, *gs[group * PIECE_GROUP:(group + 1) * PIECE_GROUP])

    return jnp.concatenate([group_call(group) for group in range(n // PIECE_GROUP)], axis=1)


def rowwise(name, fn, rows, segs, params, out_widths, nct, diff_rows=None, then=None):
    n_row, n_seg, n_par, n_out = len(rows), len(segs), len(params), len(out_widths)
    out_dtype = F32 if then is None else BF16
    diff_rows = [True] * n_row if diff_rows is None else list(diff_rows)
    r_total = rows[0].shape[0]
    n_tiles = r_total // ROW_TILE

    def seg_of(i):
        return jnp.where(i < nct, 0, 1)

    def row_spec(width):
        return pl.BlockSpec((ROW_TILE, width), lambda i: (i, 0))

    def seg_spec(shape):
        nd = len(shape)
        return pl.BlockSpec((1,) + tuple(shape[1:]), lambda i: (seg_of(i),) + (0,) * (nd - 1))

    def par_spec(shape):
        nd = len(shape)
        return pl.BlockSpec(tuple(shape), lambda i: (0,) * nd)

    in_specs = ([row_spec(r.shape[1]) for r in rows] + [seg_spec(s.shape) for s in segs]
                + [par_spec(p.shape) for p in params])

    def load(refs):
        vals = [r[...].astype(F32) for r in refs[:n_row]]
        vals += [r[0].astype(F32) for r in refs[n_row:n_row + n_seg]]
        vals += [r[...].astype(F32) for r in refs[n_row + n_seg:n_row + n_seg + n_par]]
        return vals

    def fwd_call(arrs):
        def body(*refs):
            outs = fn(*load(refs))
            for o_ref, val in zip(refs[n_row + n_seg + n_par:], outs):
                o_ref[...] = val.astype(o_ref.dtype)

        return pl.pallas_call(
            body, name=name + "_f", grid=(n_tiles,), in_specs=in_specs,
            out_specs=[row_spec(w) for w in out_widths],
            out_shape=[jax.ShapeDtypeStruct((r_total, w), out_dtype) for w in out_widths],
            compiler_params=_cparams(1),
        )(*arrs)

    d_idx = [k for k in range(n_row) if diff_rows[k]]

    def bwd_call(arrs, douts):
        n_in = n_row + n_seg + n_par

        def body(*refs):
            i = pl.program_id(0)
            vals = load(refs[:n_in])
            gs = [r[...] for r in refs[n_in:n_in + n_out]]
            out_refs = refs[n_in + n_out:]
            diff_pos = d_idx + list(range(n_row, n_in))

            def f(*dv):
                full = list(vals)
                for pos, v in zip(diff_pos, dv):
                    full[pos] = v
                return tuple(fn(*full))

            _, vjp = jax.vjp(f, *[vals[p] for p in diff_pos])
            grads = vjp(tuple(gs))
            nd = len(d_idx)
            for o_ref, g in zip(out_refs[:nd], grads[:nd]):
                o_ref[...] = g
            first_seg = jnp.logical_or(i == 0, i == nct)
            for o_ref, g in zip(out_refs[nd:nd + n_seg], grads[nd:nd + n_seg]):
                @pl.when(first_seg)
                def _(o_ref=o_ref, g=g):
                    o_ref[0] = g

                @pl.when(jnp.logical_not(first_seg))
                def _(o_ref=o_ref, g=g):
                    o_ref[0] += g
            for o_ref, g in zip(out_refs[nd + n_seg:], grads[nd + n_seg:]):
                @pl.when(i == 0)
                def _(o_ref=o_ref, g=g):
                    o_ref[...] = g

                @pl.when(i != 0)
                def _(o_ref=o_ref, g=g):
                    o_ref[...] += g

        out_specs = ([row_spec(rows[k].shape[1]) for k in d_idx] + [seg_spec(s.shape) for s in segs]
                     + [par_spec(p.shape) for p in params])
        out_shape = ([jax.ShapeDtypeStruct(rows[k].shape, F32) for k in d_idx]
                     + [jax.ShapeDtypeStruct(s.shape, F32) for s in segs]
                     + [jax.ShapeDtypeStruct(p.shape, F32) for p in params])
        return pl.pallas_call(
            body, name=name + "_b", grid=(n_tiles,),
            in_specs=in_specs + [row_spec(w) for w in out_widths],
            out_specs=out_specs, out_shape=out_shape, compiler_params=_cparams(1),
        )(*arrs, *douts)

    @jax.custom_vjp
    def op(*arrs):
        return tuple(fwd_call(arrs))

    def op_fwd(*arrs):
        return tuple(fwd_call(arrs)), arrs

    def op_bwd(arrs, douts):
        grads = list(bwd_call(arrs, douts))
        nd = len(d_idx)
        row_grads = [jnp.zeros_like(arrs[k]) for k in range(n_row)]
        for k, g in zip(d_idx, grads[:nd]):
            row_grads[k] = g
        return tuple(row_grads + grads[nd:])

    if then is None:
        op.defvjp(op_fwd, op_bwd)
        return op(*rows, *segs, *params)

    kind, w, mm_name = then

    def project(u, wb):
        if kind == "linear":
            return _mm(u, wb, "nn", mm_name + "_f")
        return tuple(_mm_split(u, wb, mm_name + "_f"))

    @jax.custom_vjp
    def fused(w, *arrs):
        return project(fwd_call(arrs)[0], w.astype(BF16))

    def fused_fwd(w, *arrs):
        u, wb = fwd_call(arrs)[0], w.astype(BF16)
        return project(u, wb), (arrs, u, wb)

    def fused_bwd(res, g):
        arrs, u, wb = res
        if kind == "linear":
            du, dw = _mm(g, wb, "nt", mm_name + "_dx"), _mm(u, g, "tn", mm_name + "_dw")
        else:
            du, dw = _mm_join(list(g), wb, mm_name + "_dx"), _mm_join_tn(u, list(g), mm_name + "_dw")
        return (dw,) + op_bwd(arrs, [du])

    fused.defvjp(fused_fwd, fused_bwd)
    return fused(w, *rows, *segs, *params)


ATT_SCALE = MLA_QK ** -0.5
LOG2E = math.log2(math.e)
ATT_KEY_CHUNKS = (768, 512, 256)


ATT_LATENT_TILE = 1024


def _query_rows_spec(row0, tq):
    return pl.BlockSpec((pl.Element(tq), pl.Element(HEAD_PAD)),
                        lambda h, i: (pl.multiple_of(row0 + i * tq, ROW_TILE), pl.multiple_of(h * HEAD_PAD, HEAD_PAD)))


def _key_chunks(nk):
    kc = _pick(nk, ATT_KEY_CHUNKS)
    return [(c * kc, kc) for c in range(nk // kc)]


def _attn_fwd_call(q, k, v, row0, n_rows, tq, nk, name):
    def body(q_ref, k_ref, v_ref, o_ref, lse_ref):
        qv = q_ref[...]
        m = jnp.full((tq, 1), -jnp.inf, F32)
        l = jnp.zeros((tq, 1), F32)
        acc = jnp.zeros((tq, HEAD_PAD), F32)
        for start, size in _key_chunks(nk):
            s = lax.dot_general(qv, k_ref[start:start + size, :], _DOT_DIMS["nt"], preferred_element_type=F32)
            m_new = jnp.maximum(m, jnp.max(s, axis=-1, keepdims=True))
            alpha = jnp.exp2(m - m_new)
            p = jnp.exp2(s - m_new)
            l = alpha * l + jnp.sum(p, axis=-1, keepdims=True)
            acc = alpha * acc + lax.dot_general(p.astype(BF16), v_ref[start:start + size, :], _DOT_DIMS["nn"],
                                                preferred_element_type=F32)
            m = m_new
        o_ref[...] = acc / l
        lse_ref[...] = jnp.broadcast_to(m + jnp.log2(l), (tq, HEAD_PAD))

    out_spec = pl.BlockSpec((tq, HEAD_PAD), lambda h, i: (i, h))
    kv_spec = pl.BlockSpec((nk, HEAD_PAD), lambda h, i: (0, h))
    out = jax.ShapeDtypeStruct((n_rows, q.shape[1]), F32)
    return pl.pallas_call(
        body, name=name, grid=(MLA_HEADS, n_rows // tq), in_specs=[_query_rows_spec(row0, tq), kv_spec, kv_spec],
        out_specs=[out_spec, out_spec], out_shape=[out, out], compiler_params=_cparams(2),
    )(q, k, v)


def _attn_bwd_call(q, k, v, o, lse, do, row0, n_rows, tq, nk, name):
    nq = n_rows // tq

    def body(q_ref, k_ref, v_ref, o_ref, lse_ref, do_ref, dq_ref, dk_ref, dv_ref):
        i = pl.program_id(1)

        @pl.when(i == 0)
        def _():
            dk_ref[...] = jnp.zeros_like(dk_ref)
            dv_ref[...] = jnp.zeros_like(dv_ref)

        qv = q_ref[...]
        dov = do_ref[...]
        dob = dov.astype(BF16)
        lse = lse_ref[:, 0:1]
        delta = jnp.sum(dov * o_ref[...], axis=-1, keepdims=True)
        dq = jnp.zeros((tq, HEAD_PAD), F32)
        for start, size in _key_chunks(nk):
            kk = k_ref[start:start + size, :]
            vv = v_ref[start:start + size, :]
            s = lax.dot_general(qv, kk, _DOT_DIMS["nt"], preferred_element_type=F32)
            p = jnp.exp2(s - lse)
            dp = lax.dot_general(dob, vv, _DOT_DIMS["nt"], preferred_element_type=F32)
            g = (p * (dp - delta)).astype(BF16)
            dk_ref[start:start + size, :] += lax.dot_general(g, qv, _DOT_DIMS["tn"], preferred_element_type=F32)
            dv_ref[start:start + size, :] += lax.dot_general(p.astype(BF16), dob, _DOT_DIMS["tn"],
                                                             preferred_element_type=F32)
            dq = dq + lax.dot_general(g, kk, _DOT_DIMS["nn"], preferred_element_type=F32)
        dq_ref[...] = dq * ATT_SCALE

        @pl.when(i == nq - 1)
        def _():
            dk_ref[...] = dk_ref[...] * (1.0 / LOG2E)

    own_spec = pl.BlockSpec((tq, HEAD_PAD), lambda h, i: (i, h))
    kv_spec = pl.BlockSpec((nk, HEAD_PAD), lambda h, i: (0, h))
    rows_spec = _query_rows_spec(row0, tq)
    return pl.pallas_call(
        body, name=name, grid=(MLA_HEADS, nq),
        in_specs=[rows_spec, kv_spec, kv_spec, own_spec, own_spec, rows_spec],
        out_specs=[own_spec, kv_spec, kv_spec],
        out_shape=[jax.ShapeDtypeStruct((n_rows, q.shape[1]), F32), jax.ShapeDtypeStruct((nk, q.shape[1]), F32),
                   jax.ShapeDtypeStruct((nk, q.shape[1]), F32)],
        compiler_params=_cparams(2),
    )(q, k, v, o, lse, do)


def attention(q, k, v, tc, name):
    r_total = q.shape[0]
    tq_lat = _pick(r_total - tc, (ATT_LATENT_TILE, ROW_TILE))
    ranges = [(0, tc, ROW_TILE, tc, "_ctx"), (tc, r_total - tc, tq_lat, r_total, "_lat")]

    def operands(q, k, v):
        return (q * (ATT_SCALE * LOG2E)).astype(BF16), k.astype(BF16), v.astype(BF16)

    def forward(qb, kb, vb):
        return [_attn_fwd_call(qb, kb, vb, row0, n_rows, tq, nk, name + tag + "_f")
                for row0, n_rows, tq, nk, tag in ranges]

    @jax.custom_vjp
    def op(q, k, v):
        return jnp.concatenate([o for o, _ in forward(*operands(q, k, v))], axis=0)

    def fwd(q, k, v):
        qb, kb, vb = operands(q, k, v)
        parts = forward(qb, kb, vb)
        return jnp.concatenate([o for o, _ in parts], axis=0), (qb, kb, vb, parts)

    def bwd(res, do):
        qb, kb, vb, parts = res
        (dq_c, dk_c, dv_c), (dq_l, dk_l, dv_l) = [
            _attn_bwd_call(qb, kb, vb, o, lse, do, row0, n_rows, tq, nk, name + tag + "_b")
            for (o, lse), (row0, n_rows, tq, nk, tag) in zip(parts, ranges)]
        grow = lambda part: jnp.pad(part, ((0, r_total - tc), (0, 0)))
        return jnp.concatenate([dq_c, dq_l], axis=0), dk_l + grow(dk_c), dv_l + grow(dv_c)

    op.defvjp(fwd, bwd)
    return op(q, k, v)


CHUNK_SHIFT = CHUNK.bit_length() - 1


def _block_pairs():
    rows = lax.broadcasted_iota(jnp.int32, (ROW_TILE, ROW_TILE), 0)
    cols = lax.broadcasted_iota(jnp.int32, (ROW_TILE, ROW_TILE), 1)
    same = lax.shift_right_logical(rows, CHUNK_SHIFT) == lax.shift_right_logical(cols, CHUNK_SHIFT)
    return rows, cols, same


def _block_mask(kind):
    rows, cols, same = _block_pairs()
    order = {"lower_incl": rows >= cols, "upper_incl": rows <= cols, "lower_strict": rows > cols,
             "upper_strict": rows < cols}[kind]
    return jnp.logical_and(same, order)


def _row_chunk():
    return lax.shift_right_logical(lax.broadcasted_iota(jnp.int32, (ROW_TILE, 1), 0), CHUNK_SHIFT)


def _dot01(kind, x):
    m = _block_mask(kind).astype(BF16)
    hi = x.astype(BF16)
    rest = x - hi.astype(F32)
    mid = rest.astype(BF16)
    lo = (rest - mid.astype(F32)).astype(BF16)
    terms = jnp.concatenate([hi, mid, lo], axis=1)
    out = lax.dot_general(m, terms, _DOT_DIMS["nn"], preferred_element_type=F32)
    n = x.shape[1]
    return out[:, :n] + out[:, n:2 * n] + out[:, 2 * n:]


def _chunk_sums(x, forward):
    kinds = ("lower_incl", "upper_strict") if forward else ("upper_incl", "lower_strict")
    transposed = ("upper_incl", "lower_strict") if forward else ("lower_incl", "upper_strict")

    @jax.custom_vjp
    def f(x):
        return _dot01(kinds[0], x), _dot01(kinds[1], x)

    def fwd(x):
        return (_dot01(kinds[0], x), _dot01(kinds[1], x)), None

    def bwd(_, g):
        return (_dot01(transposed[0], g[0]) + _dot01(transposed[1], g[1]),)

    f.defvjp(fwd, bwd)
    return f(x)


def _scan_order(forward):
    if forward:
        return list(range(SCAN_CHUNKS)), lambda c: c * CHUNK + CHUNK - 1
    return list(range(SCAN_CHUNKS - 1, -1, -1)), lambda c: c * CHUNK


def _carry_states(forward, st0, inc_all, decay_of):
    order, _ = _scan_order(forward)
    entering = [None] * SCAN_CHUNKS
    st = st0
    for c in order:
        entering[c] = st
        st = st * decay_of(c) + inc_all[:, c * HEAD_PAD:(c + 1) * HEAD_PAD]
    return jnp.concatenate(entering, axis=0), st


def _per_chunk_lanes(x):
    chunk = _row_chunk()
    return jnp.concatenate([jnp.where(chunk == c, x, 0.0) for c in range(SCAN_CHUNKS)], axis=1)


def _own_chunk_lanes(x4):
    chunk = _row_chunk()
    n = x4.shape[1] // SCAN_CHUNKS
    out = jnp.where(chunk == 0, x4[:, :n], 0.0)
    for c in range(1, SCAN_CHUNKS):
        out = out + jnp.where(chunk == c, x4[:, c * n:(c + 1) * n], 0.0)
    return out


def _gla_block(forward, q, k, v, la, st0):
    cum, after = _chunk_sums(la, forward)
    _, last_row = _scan_order(forward)
    q_dec = q * (jnp.exp(cum) * (GLA_DK ** -0.5))
    att = _bdot(q_dec, k * jnp.exp(-cum), "nt")
    att = jnp.where(_block_mask("lower_incl" if forward else "upper_strict"), att, 0.0)
    inc_all = _bdot(v, _per_chunk_lanes(k * jnp.exp(after)), "tn")
    entering, st1 = _carry_states(forward, st0, inc_all,
                                  lambda c: jnp.exp(cum[last_row(c):last_row(c) + 1, :]))
    o = _bdot(att, v, "nn") + _own_chunk_lanes(_bdot(q_dec, entering, "nt"))
    return o, st1


def _ret_block(forward, q, k, v, rd, st0):
    lg = -jnp.exp(rd[0:1, 0:1])
    rows, cols, _ = _block_pairs()
    pos = jnp.bitwise_and(lax.broadcasted_iota(jnp.int32, (ROW_TILE, 1), 0), CHUNK - 1).astype(F32)
    if forward:
        to_end, from_start, rel = CHUNK - 1.0 - pos, pos + 1.0, (rows - cols).astype(F32)
    else:
        to_end, from_start, rel = pos, CHUNK - pos, (cols - rows).astype(F32)
    mask = _block_mask("lower_incl" if forward else "upper_strict")
    dmat = jnp.where(mask, jnp.exp(jnp.where(mask, rel, 0.0) * lg), 0.0)
    att = _bdot(q, k, "nt") * dmat
    inc_all = _bdot(v, _per_chunk_lanes(k * jnp.exp(to_end * lg)), "tn")
    entering, st1 = _carry_states(forward, st0, inc_all, lambda c: jnp.exp(CHUNK * lg))
    o = _bdot(att, v, "nn") + _own_chunk_lanes(_bdot(q, entering, "nt")) * jnp.exp(from_start * lg)
    return o, st1


def scan(kind, forward, q, k, v, aux, tc, name):
    heads = q.shape[1] // HEAD_PAD
    r_total = q.shape[0]
    nblk = r_total // ROW_TILE
    nctb = tc // ROW_TILE
    block_fn = functools.partial(_gla_block if kind == "gla" else _ret_block, forward)
    per_row_aux = kind == "gla"

    def blk(g):
        if forward:
            return g
        return jnp.where(g < nctb, nctb - 1 - g, nblk - 1 - (g - nctb))

    def specs(step_to_g):
        row = pl.BlockSpec((ROW_TILE, heads * HEAD_PAD), lambda s: (blk(step_to_g(s)), 0))
        aux_spec = row if per_row_aux else pl.BlockSpec((heads, 8, HEAD_PAD), lambda s: (0, 0, 0))
        st = pl.BlockSpec((1, heads, HEAD_PAD, HEAD_PAD), lambda s: (step_to_g(s), 0, 0, 0))
        return row, aux_spec, st

    def head_cols(h):
        return slice(h * HEAD_PAD, (h + 1) * HEAD_PAD)

    def fwd_call(q, k, v, aux):
        row, aux_spec, st_spec = specs(lambda s: s)

        def body(q_ref, k_ref, v_ref, a_ref, o_ref, st0_ref, st_ref):
            @pl.when(pl.program_id(0) == 0)
            def _():
                st_ref[...] = jnp.zeros_like(st_ref)

            qv, kv, vv = q_ref[...], k_ref[...], v_ref[...]
            outs = []
            for h in range(heads):
                st0 = st_ref[h]
                st0_ref[0, h] = st0
                a = a_ref[:, head_cols(h)] if per_row_aux else a_ref[h]
                o, st1 = block_fn(qv[:, head_cols(h)], kv[:, head_cols(h)], vv[:, head_cols(h)], a, st0)
                outs.append(o)
                st_ref[h] = st1
            o_ref[...] = jnp.concatenate(outs, axis=1)

        return pl.pallas_call(
            body, name=name + "_f", grid=(nblk,), in_specs=[row, row, row, aux_spec],
            out_specs=[row, st_spec],
            out_shape=[jax.ShapeDtypeStruct(q.shape, F32),
                       jax.ShapeDtypeStruct((nblk, heads, HEAD_PAD, HEAD_PAD), F32)],
            scratch_shapes=[pltpu.VMEM((heads, HEAD_PAD, HEAD_PAD), F32)],
            compiler_params=_cparams(1),
        )(q, k, v, aux)

    def bwd_call(q, k, v, aux, st0s, do):
        row, aux_spec, st_spec = specs(lambda s: nblk - 1 - s)

        def body(q_ref, k_ref, v_ref, a_ref, st0_ref, do_ref, dq_ref, dk_ref, dv_ref, da_ref, dst_ref):
            s = pl.program_id(0)

            @pl.when(s == 0)
            def _():
                dst_ref[...] = jnp.zeros_like(dst_ref)

            qv, kv, vv, dov = q_ref[...], k_ref[...], v_ref[...], do_ref[...]
            grads = []
            for h in range(heads):
                a = a_ref[:, head_cols(h)] if per_row_aux else a_ref[h]
                _, vjp = jax.vjp(block_fn, qv[:, head_cols(h)], kv[:, head_cols(h)], vv[:, head_cols(h)], a,
                                 st0_ref[0, h])
                dq, dk, dv, da, dst0 = vjp((dov[:, head_cols(h)], dst_ref[h]))
                dst_ref[h] = dst0
                grads.append((dq, dk, dv, da))
            dq_ref[...] = jnp.concatenate([g[0] for g in grads], axis=1)
            dk_ref[...] = jnp.concatenate([g[1] for g in grads], axis=1)
            dv_ref[...] = jnp.concatenate([g[2] for g in grads], axis=1)
            if per_row_aux:
                da_ref[...] = jnp.concatenate([g[3] for g in grads], axis=1)
            else:
                da = jnp.stack([g[3] for g in grads], axis=0)

                @pl.when(s == 0)
                def _():
                    da_ref[...] = da

                @pl.when(s != 0)
                def _():
                    da_ref[...] += da

        return pl.pallas_call(
            body, name=name + "_b", grid=(nblk,),
            in_specs=[row, row, row, aux_spec, st_spec, row],
            out_specs=[row, row, row, aux_spec],
            out_shape=[jax.ShapeDtypeStruct(q.shape, F32)] * 3 + [jax.ShapeDtypeStruct(aux.shape, F32)],
            scratch_shapes=[pltpu.VMEM((heads, HEAD_PAD, HEAD_PAD), F32)],
            compiler_params=_cparams(1),
        )(q, k, v, aux, st0s, do)

    @jax.custom_vjp
    def op(q, k, v, aux):
        return fwd_call(q, k, v, aux)[0]

    def fwd(q, k, v, aux):
        o, st0s = fwd_call(q, k, v, aux)
        return o, (q, k, v, aux, st0s)

    def bwd(res, do):
        return tuple(bwd_call(*res, do))

    op.defvjp(fwd, bwd)
    return op(q, k, v, aux)


HALO = 8


def _neighbours(main, prev8, next8, i, nct, n_tiles):
    has_prev = jnp.logical_and(i != 0, i != nct).astype(F32)
    has_next = jnp.logical_and(i != nct - 1, i != n_tiles - 1).astype(F32)
    row = lax.broadcasted_iota(jnp.int32, main.shape, 0)
    down = jnp.where(row == 0, prev8[HALO - 1:HALO] * has_prev, pltpu.roll(main, 1, 0))
    up = jnp.where(row == ROW_TILE - 1, next8[0:1] * has_next, pltpu.roll(main, ROW_TILE - 1, 0))
    return down, up


GELU_K = math.sqrt(2.0 / math.pi)
GELU_A = 0.044715


def _gelu_tanh_grad(x):
    t = jnp.tanh(GELU_K * (x + GELU_A * (x * x * x)))
    return 0.5 * x * (1.0 + t), 0.5 * (1.0 + t) + 0.5 * x * (1.0 - t * t) * (GELU_K * (1.0 + 3.0 * GELU_A * (x * x)))


def conv_ffn_out(gate, up, w8, b, w_out, tc, name):
    r_total, width = gate.shape
    n_tiles = r_total // ROW_TILE
    nct = tc // ROW_TILE
    per = ROW_TILE // HALO
    main_spec = pl.BlockSpec((ROW_TILE, width), lambda i: (i, 0))
    prev_spec = pl.BlockSpec((HALO, width), lambda i: (jnp.maximum(i * per - 1, 0), 0))
    next_spec = pl.BlockSpec((HALO, width), lambda i: (jnp.minimum((i + 1) * per, r_total // HALO - 1), 0))
    w_spec = pl.BlockSpec((8, width), lambda i: (0, 0))
    b_spec = pl.BlockSpec((1, width), lambda i: (0, 0))
    halo3 = [main_spec, prev_spec, next_spec]

    def conv(w_ref, b_ref, down, mid, upn):
        return w_ref[0:1] * down + w_ref[1:2] * mid + w_ref[2:3] * upn + b_ref[...]

    def fwd_call(gate, up, w8, b):
        def body(g_ref, gp_ref, gn_ref, up_ref, w_ref, b_ref, o_ref):
            gv = g_ref[...]
            down, upn = _neighbours(gv, gp_ref[...], gn_ref[...], pl.program_id(0), nct, n_tiles)
            o_ref[...] = (_gelu_tanh(conv(w_ref, b_ref, down, gv, upn)) * up_ref[...]).astype(BF16)

        return pl.pallas_call(
            body, name=name + "_f", grid=(n_tiles,), in_specs=halo3 + [main_spec, w_spec, b_spec],
            out_specs=main_spec, out_shape=jax.ShapeDtypeStruct(gate.shape, BF16), compiler_params=_cparams(1),
        )(gate, gate, gate, up, w8, b)

    def bwd_call(gate, up, w8, b, du):
        def body(g_ref, gp_ref, gn_ref, up_ref, upp_ref, upn_ref, du_ref, dup_ref, dun_ref, w_ref, b_ref,
                 dg_ref, dupo_ref, dw_ref, db_ref):
            i = pl.program_id(0)
            has_prev = jnp.logical_and(i != 0, i != nct).astype(F32)
            has_next = jnp.logical_and(i != nct - 1, i != n_tiles - 1).astype(F32)
            gv, gp, gn = g_ref[...], gp_ref[...], gn_ref[...]
            g_down, g_up = _neighbours(gv, gp, gn, i, nct, n_tiles)
            act, slope = _gelu_tanh_grad(conv(w_ref, b_ref, g_down, gv, g_up))
            duv = du_ref[...]
            dupo_ref[...] = duv * act
            dc = duv * up_ref[...] * slope
            c_above = conv(w_ref, b_ref, gp[HALO - 2:HALO - 1], gp[HALO - 1:HALO], gv[0:1])
            c_below = conv(w_ref, b_ref, gv[ROW_TILE - 1:ROW_TILE], gn[0:1], gn[1:2])
            dc_above = dup_ref[HALO - 1:HALO] * upp_ref[HALO - 1:HALO] * _gelu_tanh_grad(c_above)[1] * has_prev
            dc_below = dun_ref[0:1] * upn_ref[0:1] * _gelu_tanh_grad(c_below)[1] * has_next
            row = lax.broadcasted_iota(jnp.int32, dc.shape, 0)
            dc_down = jnp.where(row == 0, dc_above, pltpu.roll(dc, 1, 0))
            dc_up = jnp.where(row == ROW_TILE - 1, dc_below, pltpu.roll(dc, ROW_TILE - 1, 0))
            dg_ref[...] = w_ref[0:1] * dc_up + w_ref[1:2] * dc + w_ref[2:3] * dc_down
            dw = jnp.concatenate([jnp.sum(dc * g_down, axis=0, keepdims=True),
                                  jnp.sum(dc * gv, axis=0, keepdims=True),
                                  jnp.sum(dc * g_up, axis=0, keepdims=True),
                                  jnp.zeros((5, width), F32)], axis=0)
            db = jnp.sum(dc, axis=0, keepdims=True)

            @pl.when(i == 0)
            def _():
                dw_ref[...] = dw
                db_ref[...] = db

            @pl.when(i != 0)
            def _():
                dw_ref[...] += dw
                db_ref[...] += db

        return pl.pallas_call(
            body, name=name + "_b", grid=(n_tiles,), in_specs=halo3 * 3 + [w_spec, b_spec],
            out_specs=[main_spec, main_spec, w_spec, b_spec],
            out_shape=[jax.ShapeDtypeStruct(gate.shape, F32), jax.ShapeDtypeStruct(gate.shape, F32),
                       jax.ShapeDtypeStruct((8, width), F32), jax.ShapeDtypeStruct((1, width), F32)],
            compiler_params=_cparams(1),
        )(gate, gate, gate, up, up, up, du, du, du, w8, b)

    @jax.custom_vjp
    def op(gate, up, w8, b, w_out):
        return _mm(fwd_call(gate, up, w8, b), w_out.astype(BF16), "nn", name + "_out_f")

    def fwd(gate, up, w8, b, w_out):
        u, wb = fwd_call(gate, up, w8, b), w_out.astype(BF16)
        return _mm(u, wb, "nn", name + "_out_f"), (gate, up, w8, b, u, wb)

    def bwd(res, g):
        gate, up, w8, b, u, wb = res
        du = _mm(g, wb, "nt", name + "_out_dx")
        d_gate, d_up, d_w8, d_b = bwd_call(gate, up, w8, b, du)
        return d_gate, d_up, d_w8, d_b, _mm(u, g, "tn", name + "_out_dw")

    op.defvjp(fwd, bwd)
    return op(gate, up, w8, b, w_out)


def loss_head(h, target, tc, name):
    r_total, width = h.shape
    n_tiles = r_total // ROW_TILE
    nct = tc // ROW_TILE

    def call(h, target):
        def body(h_ref, t_ref, dh_ref, loss_ref, acc_ref):
            i = pl.program_id(0)

            @pl.when(i == 0)
            def _():
                acc_ref[...] = jnp.zeros_like(acc_ref)

            @pl.when(i < nct)
            def _():
                dh_ref[...] = jnp.zeros_like(dh_ref)

            @pl.when(i >= nct)
            def _():
                err = h_ref[...] - t_ref[...]
                dh_ref[...] = err * (1.0 / width)
                acc_ref[...] += jnp.sum((err * err).reshape(ROW_TILE // 8, 8, width), axis=0)

            @pl.when(i == n_tiles - 1)
            def _():
                loss_ref[...] = jnp.sum(acc_ref[...]).reshape(1, 1) * (0.5 / width)

        row = pl.BlockSpec((ROW_TILE, width), lambda i: (i, 0))
        return pl.pallas_call(
            body, name=name, grid=(n_tiles,),
            in_specs=[row, pl.BlockSpec((ROW_TILE, width), lambda i: (jnp.maximum(i - nct, 0), 0))],
            out_specs=[row, pl.BlockSpec((1, 1), lambda i: (0, 0))],
            out_shape=[jax.ShapeDtypeStruct(h.shape, F32), jax.ShapeDtypeStruct((1, 1), F32)],
            scratch_shapes=[pltpu.VMEM((8, width), F32)], compiler_params=_cparams(1),
        )(h, target)

    @jax.custom_vjp
    def op(h, target):
        return call(h, target)[1][0, 0]

    def fwd(h, target):
        dh, loss = call(h, target)
        return loss[0, 0], (dh, target)

    def bwd(res, g):
        dh, target = res
        return dh * g, jnp.zeros_like(target)

    op.defvjp(fwd, bwd)
    return op(h, target)


PACK_W = 1024
PACK_TILE = 128


def slab_sum(slabs, name):
    n_slab, n, _ = slabs.shape

    def body(s_ref, o_ref):
        acc = s_ref[0]
        for j in range(1, n_slab):
            acc = acc + s_ref[j]
        o_ref[...] = acc

    return pl.pallas_call(
        body, name=name, grid=(n // PACK_TILE,),
        in_specs=[pl.BlockSpec((n_slab, PACK_TILE, PACK_W), lambda i: (0, i, 0))],
        out_specs=pl.BlockSpec((PACK_TILE, PACK_W), lambda i: (i, 0)),
        out_shape=jax.ShapeDtypeStruct((n, PACK_W), F32), compiler_params=_cparams(1),
    )(slabs)


def adamw(g_slabs, w, m, v, name):
    n_slab, n, _ = g_slabs.shape

    def body(g_ref, w_ref, m_ref, v_ref, go_ref, d_ref, mo_ref, vo_ref):
        g = g_ref[0].astype(F32)
        for j in range(1, n_slab):
            g = g + g_ref[j].astype(F32)
        m_new = ADAM_B1 * m_ref[...] + (1.0 - ADAM_B1) * g
        v_new = ADAM_B2 * v_ref[...] + (1.0 - ADAM_B2) * (g * g)
        m_hat = m_new / (1.0 - ADAM_B1 ** ADAM_STEP)
        v_hat = v_new / (1.0 - ADAM_B2 ** ADAM_STEP)
        go_ref[...] = g
        d_ref[...] = -ADAM_LR * (m_hat / (jnp.sqrt(v_hat) + ADAM_EPS) + ADAM_WD * w_ref[...])
        mo_ref[...] = m_new
        vo_ref[...] = v_new

    flat = pl.BlockSpec((PACK_TILE, PACK_W), lambda i: (i, 0))
    return pl.pallas_call(
        body, name=name, grid=(n // PACK_TILE,),
        in_specs=[pl.BlockSpec((n_slab, PACK_TILE, PACK_W), lambda i: (0, i, 0)), flat, flat, flat],
        out_specs=[flat] * 4, out_shape=[jax.ShapeDtypeStruct((n, PACK_W), F32)] * 4, compiler_params=_cparams(1),
    )(g_slabs, w, m, v)


def all_gather(x, name):
    m_per, n = x.shape

    def body(x_ref, out_ref, send_sems, recv_sems, local_sem):
        px, py, pc = lax.axis_index("x"), lax.axis_index("y"), lax.axis_index("c")
        me, sibling = (px, py, pc), (px, py, 1 - pc)
        chips = [(1 - px, py), (px, 1 - py), (1 - px, 1 - py)]

        def rows(bx, by, bc):
            return out_ref.at[pl.ds((4 * bx + 2 * by + bc) * m_per, m_per), :]

        def copy(k, block, to, src=None):
            return pltpu.make_async_remote_copy(
                src_ref=rows(*block) if src is None else src, dst_ref=rows(*block),
                send_sem=send_sems.at[k], recv_sem=recv_sems.at[k], device_id=to, device_id_type=MESH)

        mine = pltpu.make_async_copy(x_ref, rows(*me), local_sem)
        mine.start()
        first = [copy(0, me, sibling, src=x_ref)]
        first += [copy(1 + j, me, (*chip, pc), src=x_ref) for j, chip in enumerate(chips)]
        for cp in first:
            cp.start()
        passed = [copy(4 + j, (*chip, pc), sibling) for j, chip in enumerate(chips)]
        for j, chip in enumerate(chips):
            copy(1 + j, (*chip, pc), me).wait_recv()
            passed[j].start()
        copy(0, sibling, me).wait_recv()
        for j, chip in enumerate(chips):
            copy(4 + j, (*chip, 1 - pc), me).wait_recv()
        for cp in first + passed:
            cp.wait_send()
        mine.wait()

    return pl.pallas_call(
        body, name=name, out_shape=jax.ShapeDtypeStruct((N_DEV * m_per, n), x.dtype),
        in_specs=[pl.BlockSpec(memory_space=pl.ANY)], out_specs=pl.BlockSpec(memory_space=pl.ANY),
        scratch_shapes=[pltpu.SemaphoreType.DMA((7,)), pltpu.SemaphoreType.DMA((7,)), pltpu.SemaphoreType.DMA],
    )(x)


N_CHIP = 4


def pair_swap(x, name):
    def body(x_ref, out_ref, send_sem, recv_sem):
        sibling = (lax.axis_index("x"), lax.axis_index("y"), 1 - lax.axis_index("c"))
        copy = pltpu.make_async_remote_copy(src_ref=x_ref, dst_ref=out_ref, send_sem=send_sem, recv_sem=recv_sem,
                                            device_id=sibling, device_id_type=MESH)
        copy.start()
        copy.wait()

    return pl.pallas_call(
        body, name=name, out_shape=jax.ShapeDtypeStruct(x.shape, x.dtype),
        in_specs=[pl.BlockSpec(memory_space=pl.ANY)], out_specs=pl.BlockSpec(memory_space=pl.ANY),
        scratch_shapes=[pltpu.SemaphoreType.DMA, pltpu.SemaphoreType.DMA],
    )(x)


def pair_add(a, b, name):
    n_slab, n, _ = a.shape

    def body(a_ref, b_ref, o_ref):
        o_ref[...] = (a_ref[...].astype(F32) + b_ref[...].astype(F32)).astype(o_ref.dtype)

    spec = pl.BlockSpec((1, PACK_TILE, PACK_W), lambda s, i: (s, i, 0))
    return pl.pallas_call(
        body, name=name, grid=(n_slab, n // PACK_TILE), in_specs=[spec, spec], out_specs=spec,
        out_shape=jax.ShapeDtypeStruct(a.shape, a.dtype), compiler_params=_cparams(2),
    )(a, b)


def chip_all_to_all(x, name):
    def body(x_ref, out_ref, send_sems, recv_sems, local_sem):
        px, py, pc = lax.axis_index("x"), lax.axis_index("y"), lax.axis_index("c")
        mine_idx = 2 * px + py
        local = pltpu.make_async_copy(x_ref.at[mine_idx], out_ref.at[mine_idx], local_sem)
        local.start()
        copies = []
        for k, (fx, fy) in enumerate(((0, 1), (1, 0), (1, 1))):
            qx, qy = px ^ fx, py ^ fy
            peer_idx = 2 * qx + qy
            copies.append((
                pltpu.make_async_remote_copy(
                    src_ref=x_ref.at[peer_idx], dst_ref=out_ref.at[mine_idx], send_sem=send_sems.at[k],
                    recv_sem=recv_sems.at[k], device_id=(qx, qy, pc), device_id_type=MESH),
                pltpu.make_async_remote_copy(
                    src_ref=x_ref.at[peer_idx], dst_ref=out_ref.at[peer_idx], send_sem=send_sems.at[k],
                    recv_sem=recv_sems.at[k], device_id=(qx, qy, pc), device_id_type=MESH)))
        for send, _ in copies:
            send.start()
        for _, landing in copies:
            landing.wait_recv()
        for send, _ in copies:
            send.wait_send()
        local.wait()

    return pl.pallas_call(
        body, name=name, out_shape=jax.ShapeDtypeStruct(x.shape, x.dtype),
        in_specs=[pl.BlockSpec(memory_space=pl.ANY)], out_specs=pl.BlockSpec(memory_space=pl.ANY),
        scratch_shapes=[pltpu.SemaphoreType.DMA((3,)), pltpu.SemaphoreType.DMA((3,)), pltpu.SemaphoreType.DMA],
    )(x)


IN_OFFSETS = {}
_off = 0
for _name, _width in (("mla_q", 256), ("mla_kv", 128), ("mla_kr", 32), ("gla_q", 512), ("gla_k", 512), ("gla_v", 512),
                      ("gla_g", 512), ("gla_rf", 16), ("gla_rb", 16), ("ret_q", 512), ("ret_k", 512), ("ret_v", 512),
                      ("ret_g", 512), ("gate_mla", 1024), ("gate_gla", 1024), ("gate_ret", 1024)):
    IN_OFFSETS[_name] = (_off, _off + _width)
    _off += _width
N_IN = _off

P_GLA, P_RET, P_GATE, P_MLAQ, P_MLAKV, P_MLAKR, P_RANK, P_END = 0, 2048, 4096, 7168, 7424, 7552, 7680, 7808


def _pad_in_proj(w):
    def cols(a, b):
        return w[:, IN_OFFSETS[a][0]:IN_OFFSETS[b][1]]

    def z(n):
        return jnp.zeros((w.shape[0], n), w.dtype)

    return jnp.concatenate([cols("gla_q", "gla_g"), cols("ret_q", "ret_g"), cols("gate_mla", "gate_ret"),
                            cols("mla_q", "mla_kv"), z(MLA_NOPE), cols("mla_kr", "mla_kr"),
                            z(HEAD_PAD - MLA_QK), cols("gla_rf", "gla_rb"), z(HEAD_PAD - 2 * GLA_RANK),
                            z(N_IN_PAD - P_END)], axis=1)


def _pad_last(a, n):
    return jnp.pad(a, [(0, 0)] * (a.ndim - 1) + [(0, n - a.shape[-1])])


def _position_tables(tc, t):
    pos = jnp.arange(t)
    inv = ROPE_THETA ** (-jnp.arange(MLA_ROPE // 4, dtype=F32) * 2.0 / (MLA_ROPE // 2))
    ang_r = (pos // GRID_W).astype(F32)[:, None] * inv[None, :]
    ang_c = (pos % GRID_W).astype(F32)[:, None] * inv[None, :]
    z8, z32, z64 = jnp.zeros((t, 8), F32), jnp.zeros((t, 32), F32), jnp.zeros((t, 64), F32)
    lat_c = jnp.concatenate([jnp.ones((t, 64), F32), jnp.cos(ang_r), jnp.cos(ang_r), jnp.cos(ang_c), jnp.cos(ang_c),
                             z32], axis=1)
    lat_sn = jnp.concatenate([z64, -jnp.sin(ang_r), z8, -jnp.sin(ang_c), z8, z32], axis=1)
    lat_sp = jnp.concatenate([z64, z8, jnp.sin(ang_r), z8, jnp.sin(ang_c), z32], axis=1)
    ctx_c = jnp.concatenate([jnp.ones((tc, MLA_QK), F32), jnp.zeros((tc, HEAD_PAD - MLA_QK), F32)], axis=1)
    ctx_z = jnp.zeros((tc, HEAD_PAD), F32)
    rinv = 1.0 / (RET_THETA ** jnp.linspace(0.0, 1.0, RET_DK // 2, dtype=F32))
    rang = jnp.arange(tc + t).astype(F32)[:, None] * rinv[None, :]
    return dict(c=jnp.concatenate([ctx_c, lat_c]), sn=jnp.concatenate([ctx_z, lat_sn]),
                sp=jnp.concatenate([ctx_z, lat_sp]),
                rc=jnp.concatenate([jnp.cos(rang), jnp.cos(rang)], axis=1),
                rs=jnp.concatenate([-jnp.sin(rang), jnp.sin(rang)], axis=1))


def _heads(x):
    return [x[:, h * HEAD_PAD:(h + 1) * HEAD_PAD] for h in range(x.shape[1] // HEAD_PAD)]


def _mla_rope(x, c, sn, sp):
    return x * c + _roll(x, HEAD_PAD - 8, 1) * sn + _roll(x, 8, 1) * sp


def _norm_mod_fn(shift_row, scale_row):
    def fn(h, mod, w):
        return (_rms(h, D, w) * (1.0 + mod[scale_row:scale_row + 1]) + mod[shift_row:shift_row + 1],)
    return fn


def _resid_fn(gate_row):
    def fn(h, y, mod):
        return (h + mod[gate_row:gate_row + 1] * y,)
    return fn


def _mla_prep_fn(x, c, sn, sp, q_norm_a, w_qb, q_norm, kv_norm_a, w_k, w_v, k_norm):
    cq, ckv = x[:, :MLA_Q_LORA], x[:, MLA_Q_LORA:MLA_Q_LORA + MLA_KV_LORA]
    kr = x[:, MLA_Q_LORA + MLA_KV_LORA:]
    qf = _bdot(_rms(cq, MLA_Q_LORA, q_norm_a), w_qb, "nn")
    q = jnp.concatenate([_mla_rope(_rms(qh, MLA_QK, q_norm), c, sn, sp) for qh in _heads(qf)], axis=1)
    xkv = _rms(ckv, MLA_KV_LORA, kv_norm_a)
    kf = _bdot(xkv, w_k, "nn")
    k = jnp.concatenate([_mla_rope(_rms(kh + kr, MLA_QK, k_norm), c, sn, sp) for kh in _heads(kf)], axis=1)
    return q, k, _bdot(xkv, w_v, "nn")


def _decay_fn(x, w2, b):
    la = _log_sigmoid(_bdot(x[:, :HEAD_PAD], w2, "nn") + b) * (1.0 / GLA_NORMALIZER)
    return la[:, :GLA_HEADS * GLA_DK], la[:, GLA_HEADS * GLA_DK:]


def _ret_rot_fn(q, k, rc, rs):
    def rot(x, scale):
        return jnp.concatenate([(xh * rc + _roll(xh, RET_DK // 2, 1) * rs) * scale for xh in _heads(x)], axis=1)
    return rot(q, 1.0), rot(k, RET_DK ** -0.5)


def _gla_out_fn(o_f, o_b, g, w):
    y = jnp.concatenate([_rms(oh, HEAD_PAD, w) for oh in _heads(o_f + o_b)], axis=1)
    return (y * _silu(g),)


def _ret_out_fn(o_f, o_b, g):
    y = jnp.concatenate([_rms(oh, HEAD_PAD) for oh in _heads(o_f + o_b)], axis=1)
    return (y * _silu(g),)


def _merge_fn(z0, z1, z2, g0a, g0b, g1a, g1b, g2a, g2b, bg):
    out = 0.0
    for n, (z, ga, gb) in enumerate(((z0, g0a, g0b), (z1, g1a, g1b), (z2, g2a, g2b))):
        out = out + jax.nn.sigmoid(jnp.concatenate([ga, gb], axis=1) + bg[n:n + 1]) * z
    return (out,)


def _layer(l, h, mod, w, tabs, tc):
    nct = tc // ROW_TILE
    tag = f"_l{l}"
    row = lambda a: a[l][None]
    pieces = rowwise("norm1" + tag, _norm_mod_fn(0, 1), [h], [mod], [row(w["norm1_w"])], [D], nct,
                     then=("pieces", _pad_in_proj(w["w_in"][l]), "in_proj" + tag))
    piece = lambda start: pieces[start // PIECE_W]

    w_qb = _pad_last(w["mla_w_qb"][l].reshape(MLA_Q_LORA, MLA_HEADS, MLA_QK), HEAD_PAD).reshape(MLA_Q_LORA, -1)
    w_kvb = w["mla_w_kvb"][l].reshape(MLA_KV_LORA, MLA_HEADS, MLA_NOPE + MLA_V)
    w_k = _pad_last(w_kvb[:, :, :MLA_NOPE], HEAD_PAD).reshape(MLA_KV_LORA, -1)
    w_v = _pad_last(w_kvb[:, :, MLA_NOPE:], HEAD_PAD).reshape(MLA_KV_LORA, -1)
    rope = [tabs["c"], tabs["sn"], tabs["sp"]]
    q, k, v = rowwise("mla_prep" + tag, _mla_prep_fn, [piece(P_MLAQ)] + rope, [],
                      [row(w["mla_q_norm_a"]), w_qb, _pad_last(row(w["mla_q_norm"]), HEAD_PAD),
                       row(w["mla_kv_norm_a"]), w_k, w_v, _pad_last(row(w["mla_k_norm"]), HEAD_PAD)],
                      [MLA_HEADS * HEAD_PAD] * 3, nct, diff_rows=[True, False, False, False])
    y_mla = attention(q, k, v, tc, "attn" + tag)
    wb_mla = _pad_last(w["w_branch"][l, 0].reshape(MLA_HEADS, MLA_V, D).transpose(0, 2, 1), HEAD_PAD)
    wb_mla = wb_mla.transpose(0, 2, 1).reshape(MLA_HEADS * HEAD_PAD, D)

    w2 = jnp.zeros((HEAD_PAD, 2 * GLA_HEADS * GLA_DK), F32)
    w2 = w2.at[:GLA_RANK, :GLA_HEADS * GLA_DK].set(w["gla_w_gk2"][l, 0])
    w2 = w2.at[GLA_RANK:2 * GLA_RANK, GLA_HEADS * GLA_DK:].set(w["gla_w_gk2"][l, 1])
    la_f, la_b = rowwise("gla_decay" + tag, _decay_fn, [piece(P_RANK)], [],
                         [w2, w["gla_b_gk"][l].reshape(1, -1)], [GLA_HEADS * GLA_DK] * 2, nct)
    gq, gk, gv, gg = [piece(P_GLA + n * PIECE_W) for n in range(4)]
    o_f = scan("gla", True, gq, gk, gv, la_f, tc, "gla_fw" + tag)
    o_b = scan("gla", False, gq, gk, gv, la_b, tc, "gla_bw" + tag)
    z_gla = rowwise("gla_out" + tag, _gla_out_fn, [o_f, o_b, gg], [], [row(w["gla_o_norm"])], [512], nct,
                    then=("linear", w["w_branch"][l, 1], "branch_gla" + tag))

    rq, rk = rowwise("ret_rot" + tag, _ret_rot_fn, [piece(P_RET), piece(P_RET + PIECE_W), tabs["rc"], tabs["rs"]],
                     [], [], [512, 512], nct, diff_rows=[True, True, False, False])
    rv, rg = piece(P_RET + 2 * PIECE_W), piece(P_RET + 3 * PIECE_W)
    rd = jnp.broadcast_to(w["ret_decay"][l][:, :, None, None], (2, RET_HEADS, 8, HEAD_PAD))
    r_f = scan("ret", True, rq, rk, rv, rd[0], tc, "ret_fw" + tag)
    r_b = scan("ret", False, rq, rk, rv, rd[1], tc, "ret_bw" + tag)
    z_ret = rowwise("ret_out" + tag, _ret_out_fn, [r_f, r_b, rg], [], [], [512], nct,
                    then=("linear", w["w_branch"][l, 2], "branch_ret" + tag))

    z = [linear(y_mla, wb_mla, "branch_mla" + tag), z_gla, z_ret]
    gates = [piece(P_GATE + n * PIECE_W) for n in range(6)]
    y = rowwise("merge" + tag, _merge_fn, z + gates, [], [_pad_rows(w["b_gate"][l], 8)], [D], nct,
                then=("linear", w["w_out"][l], "w_out" + tag))
    h = rowwise("resid1" + tag, _resid_fn(2), [h, y], [mod], [], [D], nct)[0]

    a2 = rowwise("norm2" + tag, _norm_mod_fn(3, 4), [h], [mod], [row(w["norm2_w"])], [D], nct)[0]
    gate = linear(a2, w["w_ffn_in"][l][:, :D_FF], "ffn_gate" + tag)
    up = linear(a2, w["w_ffn_in"][l][:, D_FF:], "ffn_up" + tag)
    f = conv_ffn_out(gate, up, _pad_rows(w["w_dw"][l], 8), row(w["b_dw"]), w["w_ffn_out"][l], tc, "ffn_mid" + tag)
    return rowwise("resid2" + tag, _resid_fn(5), [h, f], [mod], [], [D], nct)[0]


def _pad_rows(a, n):
    return jnp.pad(a, [(0, n - a.shape[0])] + [(0, 0)] * (a.ndim - 1))


def local_loss(w, mod, x, ctx, target):
    tc, t = ctx.shape[0], x.shape[0]
    tabs = _position_tables(tc, t)
    h = jnp.concatenate([ctx, x], axis=0)
    for l in range(DEPTH):
        h = _layer(l, h, mod[l], w, tabs, tc)
    return loss_head(h, target, tc, "loss_head")


ADA_ROWS = 16


def ada_forward(cond_in, w_ada, b_loc):
    cols = w_ada.shape[2]

    def body(x_ref, w_ref, b_ref, o_ref):
        s = _silu(x_ref[...])
        for l in range(DEPTH):
            o_ref[l] = _dg(s, w_ref[l], "nn") + b_ref[l]

    return pl.pallas_call(
        body, name="ada_forward", out_shape=jax.ShapeDtypeStruct((DEPTH, ADA_ROWS, cols), F32),
        compiler_params=pltpu.CompilerParams(vmem_limit_bytes=VMEM_LIMIT_BYTES),
    )(cond_in, w_ada, b_loc)


def ada_backward(cond_in, g_loc, dmod_own, w_ada):
    cols = w_ada.shape[2]

    def body(x_ref, g_ref, own_ref, w_ref, gw_ref, dc_ref, gb_ref):
        x = x_ref[...]
        s = _silu(x)
        dcond = jnp.zeros((8, D), F32)
        for l in range(DEPTH):
            g_ctx = jnp.sum(g_ref[2 * l], axis=0, keepdims=True)
            g_rows = jnp.concatenate([g_ref[2 * l + 1], jnp.broadcast_to(g_ctx, (8, cols))], axis=0)
            keep = lax.broadcasted_iota(jnp.int32, (ADA_ROWS, cols), 0) <= N_DEV
            gw_ref[l] = _dg(s, jnp.where(keep, g_rows, 0.0), "tn")
            dcond = dcond + _dg(jnp.broadcast_to(g_ctx, (8, cols)), w_ref[l], "nt")
            gb_ref[l:l + 1, :] = own_ref[2 * l:2 * l + 1, :] + own_ref[2 * l + 1:2 * l + 2, :]
        xc = x[N_DEV:N_DEV + 1]
        sig = jax.nn.sigmoid(xc)
        dc_ref[...] = dcond[0:1] * (sig * (1.0 + xc * (1.0 - sig)))

    return pl.pallas_call(
        body, name="ada_backward",
        out_shape=[jax.ShapeDtypeStruct(w_ada.shape, F32), jax.ShapeDtypeStruct((1, D), F32),
                   jax.ShapeDtypeStruct((DEPTH, 6 * D), F32)],
        compiler_params=pltpu.CompilerParams(vmem_limit_bytes=VMEM_LIMIT_BYTES),
    )(cond_in, g_loc, dmod_own, w_ada)


WEIGHTS = ["c_ctx", "w_ada", "b_ada", "norm1_w", "norm2_w", "w_in", "b_gate", "mla_q_norm_a", "mla_w_qb",
           "mla_kv_norm_a", "mla_w_kvb", "mla_q_norm", "mla_k_norm", "gla_w_gk2", "gla_b_gk", "gla_o_norm",
           "ret_decay", "w_branch", "w_out", "w_ffn_in", "w_dw", "b_dw", "w_ffn_out"]
INPUTS = ["x", "c", "ctx"] + WEIGHTS + ["loss_target"] + ["m_" + n for n in WEIGHTS] + ["v_" + n for n in WEIGHTS]
BIG = {"w_in": 2, "mla_w_qb": 2, "mla_w_kvb": 2, "w_branch": 3, "w_out": 1, "w_ffn_in": 2, "w_ffn_out": 1}
SMALL_SHARDED = {"b_gate": 2, "gla_w_gk2": 3, "gla_b_gk": 2, "w_dw": 2}
SMALL = ["c_ctx", "b_ada", "norm1_w", "norm2_w", "b_gate", "mla_q_norm_a", "mla_kv_norm_a", "mla_q_norm", "mla_k_norm",
         "gla_w_gk2", "gla_b_gk", "gla_o_norm", "ret_decay", "w_dw", "b_dw"]


def _entry_rows(size, align):
    return -(-size // (PACK_W * align)) * align


def _pack(arrays, rows, dtype, align, lead=0):
    parts = []
    for a in arrays:
        head = a.shape[:lead]
        size = math.prod(a.shape[lead:])
        r = _entry_rows(size, align)
        if r * PACK_W == size:
            parts.append(a.astype(dtype).reshape(head + (r, PACK_W)))
        else:
            flat = jnp.pad(a.astype(dtype).reshape(head + (size,)), [(0, 0)] * lead + [(0, r * PACK_W - size)])
            parts.append(flat.reshape(head + (r, PACK_W)))
    used = sum(p.shape[lead] for p in parts)
    if rows > used:
        parts.append(jnp.zeros(parts[0].shape[:lead] + (rows - used, PACK_W), dtype))
    return jnp.concatenate(parts, axis=lead)


def _pack_rows(shapes, align, multiple):
    used = sum(_entry_rows(math.prod(s), align) for s in shapes)
    return -(-used // multiple) * multiple


def _unpack(pack, shapes, align):
    head = pack.shape[:-2]
    out, off = [], 0
    for shape in shapes:
        size = math.prod(shape)
        r = _entry_rows(size, align)
        block = lax.slice_in_dim(pack, off, off + r, axis=len(head))
        if r * PACK_W != size:
            block = block.reshape(head + (r * PACK_W,))[..., :size]
        out.append(block.reshape(head + tuple(shape)))
        off += r
    return out


def _join_shards(stacked, axis):
    moved = jnp.moveaxis(stacked, 0, axis)
    shape = list(moved.shape)
    return moved.reshape(shape[:axis] + [shape[axis] * shape[axis + 1]] + shape[axis + 2:])


def _split_shards(full, axis):
    shape = list(full.shape)
    split = full.reshape(shape[:axis] + [N_DEV, shape[axis] // N_DEV] + shape[axis + 1:])
    return jnp.moveaxis(split, axis, 0)


def _gather_shards(local, axes, dtype, rows_multiple, name):
    names = list(axes)
    shapes = [local[n].shape for n in names]
    rows = _pack_rows(shapes, rows_multiple, rows_multiple)
    gathered = all_gather(_pack([local[n] for n in names], rows, dtype, rows_multiple), name)
    stacked = _unpack(gathered.reshape(N_DEV, rows, PACK_W), shapes, rows_multiple)
    return {n: _join_shards(s, axes[n]).astype(F32) for n, s in zip(names, stacked)}


def kernel(*args):
    a = dict(zip(INPUTS, args))
    me = 4 * lax.axis_index("x") + 2 * lax.axis_index("y") + lax.axis_index("c")
    cols = a["w_ada"].shape[2]

    small_names = list(SMALL_SHARDED)
    small_local = [a[n].shape for n in small_names]
    first_rows = _pack_rows([a["c"].shape] + small_local, 8, 8)
    first = all_gather(_pack([a["c"]] + [a[n] for n in small_names], first_rows, F32, 8), "gather_small")
    first = _unpack(first.reshape(N_DEV, first_rows, PACK_W), [a["c"].shape] + small_local, 8)
    c_all = first[0][:, 0]

    cond_in = jnp.concatenate([c_all, a["c_ctx"][None], jnp.zeros((ADA_ROWS - N_DEV - 1, D), F32)], axis=0)
    b_loc = lax.dynamic_slice_in_dim(a["b_ada"], me * cols, cols, axis=1)[:, None, :]
    mod_loc = ada_forward(cond_in, a["w_ada"], b_loc)
    mod_all = all_gather(mod_loc.reshape(DEPTH * ADA_ROWS, cols), "gather_mod")
    mod_all = mod_all.reshape(N_DEV, DEPTH, ADA_ROWS, cols).transpose(1, 2, 0, 3).reshape(DEPTH, ADA_ROWS, 6, D)
    mod_me = lax.dynamic_index_in_dim(mod_all, me, axis=1, keepdims=False)
    mod = jnp.pad(jnp.stack([mod_all[:, N_DEV], mod_me], axis=1), ((0, 0), (0, 0), (0, 2), (0, 0)))

    w = _gather_shards(a, BIG, BF16, 16, "gather_weights")
    w.update({n: _join_shards(s, SMALL_SHARDED[n]) for n, s in zip(small_names, first[1:])})
    for n in SMALL:
        if n not in SMALL_SHARDED and n not in ("c_ctx", "b_ada"):
            w[n] = a[n]

    loss, (gw, gmod, gx) = jax.value_and_grad(local_loss, argnums=(0, 1, 2))(
        w, mod, a["x"][0], a["ctx"][0], a["loss_target"][0])
    loss = lax.psum(loss, ("x", "y", "c"))

    dmod_own = gmod[:, :, :6].reshape(2 * DEPTH, 6 * D)
    g_all = all_gather(jnp.pad(dmod_own, ((0, 8 - 2 * DEPTH), (0, 0))), "gather_dmod").reshape(N_DEV, 8, 6 * D)
    g_loc = lax.dynamic_slice_in_dim(g_all[:, :2 * DEPTH], me * cols, cols, axis=2).transpose(1, 0, 2)
    g_w_ada, g_c_ctx, g_b_ada = ada_backward(cond_in, g_loc, dmod_own, a["w_ada"])

    small_part = dict(gw, c_ctx=g_c_ctx, b_ada=g_b_ada)
    small_shapes = [a[n].shape if n not in SMALL_SHARDED else gw[n].shape for n in SMALL]
    rows = _pack_rows(small_shapes, 8, PACK_TILE)
    parts = all_gather(_pack([small_part[n] for n in SMALL], rows, F32, 8), "gather_small_grads")
    small_sum = _unpack(slab_sum(parts.reshape(N_DEV, rows, PACK_W), "sum_small_grads"), small_shapes, 8)
    g_small = {}
    for n, g in zip(SMALL, small_sum):
        if n in SMALL_SHARDED:
            ax = SMALL_SHARDED[n]
            g = lax.dynamic_slice_in_dim(g, me * a[n].shape[ax], a[n].shape[ax], axis=ax)
        g_small[n] = g

    big_rows = _pack_rows([a[n].shape for n in BIG], 16, PACK_TILE)
    slabs = _pack([_split_shards(gw[n], ax) for n, ax in BIG.items()], big_rows, BF16, 16, lead=1)
    by_core = slabs.reshape(N_CHIP, 2, big_rows, PACK_W)
    my_core = lax.axis_index("c")
    keep = lax.dynamic_index_in_dim(by_core, my_core, axis=1, keepdims=False)
    give = lax.dynamic_index_in_dim(by_core, 1 - my_core, axis=1, keepdims=False)
    pair_sum = pair_add(keep, pair_swap(give, "swap_grads"), "add_pair_grads")
    landed = chip_all_to_all(pair_sum, "scatter_grads")

    def update(names, g_slabs, rows, align, label):
        shapes = [a[n].shape for n in names]
        packs = [_pack([a[pre + n] for n in names], rows, F32, align) for pre in ("", "m_", "v_")]
        outs = adamw(g_slabs, *packs, label)
        return [dict(zip(names, _unpack(o, shapes, align))) for o in outs]

    res_big = update(list(BIG), landed, big_rows, 16, "adamw_big")
    ada_rows = _pack_rows([a["w_ada"].shape], 8, PACK_TILE)
    res_ada = update(["w_ada"], _pack([g_w_ada], ada_rows, F32, 8)[None], ada_rows, 8, "adamw_ada")
    small_rows = _pack_rows([a[n].shape for n in SMALL], 8, PACK_TILE)
    res_small = update(SMALL, _pack([g_small[n] for n in SMALL], small_rows, F32, 8)[None], small_rows, 8,
                       "adamw_small")

    outs = [loss, gx[None]]
    for k in range(4):
        merged = {**res_big[k], **res_ada[k], **res_small[k]}
        outs += [merged[n] for n in WEIGHTS]
    return tuple(outs)
```

```python
import functools
import math

import jax
import jax.numpy as jnp
import numpy as np
from jax import lax
from jax.experimental import pallas as pl
from jax.experimental.pallas import tpu as pltpu

F32 = jnp.float32
BF16 = jnp.bfloat16

N_DEV = 8
D = 1024
DEPTH = 2
GRID_W = 64
MLA_HEADS = 8
MLA_NOPE = 64
MLA_ROPE = 32
MLA_QK = 96
MLA_V = 64
MLA_Q_LORA = 256
MLA_KV_LORA = 128
GLA_HEADS = 4
GLA_DK = 128
GLA_RANK = 16
GLA_NORMALIZER = 16.0
RET_HEADS = 4
RET_DK = 128
BRANCH_W = 512
D_FF = 2816
CHUNK = 64
ROPE_THETA = 10000.0
RET_THETA = 10000.0
EPS = 1e-6
HEAD_PAD = 128
N_IN_PAD = 8192

ADAM_LR = 0.001
ADAM_B1 = 0.9
ADAM_B2 = 0.999
ADAM_EPS = 1e-08
ADAM_WD = 0.01
ADAM_STEP = 10

ROW_TILE = 256
SCAN_CHUNKS = ROW_TILE // CHUNK
VMEM_LIMIT_BYTES = 56 * 1024 * 1024
MESH = pl.DeviceIdType.MESH


def _cparams(n_axes):
    return pltpu.CompilerParams(dimension_semantics=("arbitrary",) * n_axes, vmem_limit_bytes=VMEM_LIMIT_BYTES)


def _pick(dim, cands):
    for cand in cands:
        if dim % cand == 0:
            return cand
    return dim


_DOT_DIMS = {"nn": (((1,), (0,)), ((), ())), "nt": (((1,), (1,)), ((), ())), "tn": (((0,), (0,)), ((), ()))}


def _dg(a, b, mode):
    return lax.dot_general(a.astype(BF16), b.astype(BF16), _DOT_DIMS[mode], preferred_element_type=F32)


def _bdot(a, b, mode):
    @jax.custom_vjp
    def f(a, b):
        return _dg(a, b, mode)

    def fwd(a, b):
        return _dg(a, b, mode), (a, b)

    def bwd(res, g):
        a, b = res
        if mode == "nn":
            return _dg(g, b, "nt").astype(a.dtype), _dg(a, g, "tn").astype(b.dtype)
        if mode == "nt":
            return _dg(g, b, "nn").astype(a.dtype), _dg(g, a, "tn").astype(b.dtype)
        return _dg(b, g, "nt").astype(a.dtype), _dg(a, g, "nn").astype(b.dtype)

    f.defvjp(fwd, bwd)
    return f(a, b)


def _roll(x, shift, axis):
    n = x.shape[axis]
    shift = shift % n

    @jax.custom_vjp
    def f(x):
        return pltpu.roll(x, shift, axis)

    def fwd(x):
        return pltpu.roll(x, shift, axis), None

    def bwd(_, g):
        return (pltpu.roll(g, (n - shift) % n, axis),)

    f.defvjp(fwd, bwd)
    return f(x)


@jax.custom_jvp
def _log_sigmoid(x):
    return jnp.minimum(x, 0.0) - jnp.log(1.0 + jnp.exp(-jnp.abs(x)))


@_log_sigmoid.defjvp
def _log_sigmoid_jvp(primals, tangents):
    (x,), (t,) = primals, tangents
    return _log_sigmoid(x), t * jax.nn.sigmoid(-x)


def _rms(x, n, w=None):
    y = x * lax.rsqrt(jnp.sum(x * x, axis=-1, keepdims=True) * (1.0 / n) + EPS)
    return y if w is None else y * w


def _silu(x):
    return x * jax.nn.sigmoid(x)


def _gelu_tanh(x):
    return 0.5 * x * (1.0 + jnp.tanh(math.sqrt(2.0 / math.pi) * (x + 0.044715 * (x * x * x))))


def _mm(a, b, mode, name):
    if mode == "nn":
        (m, k), (_, n) = a.shape, b.shape
    elif mode == "nt":
        (m, k), (n, _) = a.shape, b.shape
    else:
        (k, m), (_, n) = a.shape, b.shape
    tm = _pick(m, (1024, 768, 1408, 512, 256, 128))
    tn = _pick(n, (1024, 1408, 512, 256, 128))
    tk = _pick(k, (1024, 768, 1408, 512, 256, 128))
    nk = k // tk
    if mode == "nn":
        a_spec = pl.BlockSpec((tm, tk), lambda i, j, kk: (i, kk))
        b_spec = pl.BlockSpec((tk, tn), lambda i, j, kk: (kk, j))
    elif mode == "nt":
        a_spec = pl.BlockSpec((tm, tk), lambda i, j, kk: (i, kk))
        b_spec = pl.BlockSpec((tn, tk), lambda i, j, kk: (j, kk))
    else:
        a_spec = pl.BlockSpec((tk, tm), lambda i, j, kk: (kk, i))
        b_spec = pl.BlockSpec((tk, tn), lambda i, j, kk: (kk, j))

    def body(a_ref, b_ref, o_ref):
        kk = pl.program_id(2)
        part = _dg(a_ref[...], b_ref[...], mode)
        if nk == 1:
            o_ref[...] = part
        else:
            @pl.when(kk == 0)
            def _():
                o_ref[...] = part

            @pl.when(kk != 0)
            def _():
                o_ref[...] += part

    return pl.pallas_call(
        body, name=name, grid=(m // tm, n // tn, nk),
        in_specs=[a_spec, b_spec], out_specs=pl.BlockSpec((tm, tn), lambda i, j, kk: (i, j)),
        out_shape=jax.ShapeDtypeStruct((m, n), F32),
        compiler_params=_cparams(3),
    )(a, b)


def linear(x, w, name):
    @jax.custom_vjp
    def op(x, w):
        return _mm(x, w.astype(BF16), "nn", name + "_f")

    def fwd(x, w):
        wb = w.astype(BF16)
        return _mm(x, wb, "nn", name + "_f"), (x, wb)

    def bwd(res, g):
        x, wb = res
        return _mm(g, wb, "nt", name + "_dx"), _mm(x, g, "tn", name + "_dw")

    op.defvjp(fwd, bwd)
    return op(x, w)


PIECE_W = 512
PIECE_ROWS = 384


def _mm_split(a, wb, name):
    (r, k), n = a.shape, wb.shape[1] // PIECE_W
    tm = _pick(r, (PIECE_ROWS, ROW_TILE))
    width = PIECE_GROUP * PIECE_W

    n_tiles = r // tm

    def body(a_ref, w_ref, *out_refs):
        j = pl.program_id(0)
        res = _dg(a_ref[...], w_ref[...], "nn")
        for group in range(n // PIECE_GROUP):
            @pl.when(j == group)
            def _(group=group):
                for p in range(PIECE_GROUP):
                    out_refs[group * PIECE_GROUP + p][...] = res[:, p * PIECE_W:(p + 1) * PIECE_W]

    def out_spec(jj):
        group = jj // PIECE_GROUP
        return pl.BlockSpec((tm, PIECE_W),
                            lambda j, i: (jnp.where(j == group, i, jnp.where(j < group, 0, n_tiles - 1)), 0))

    return pl.pallas_call(
        body, name=name, grid=(n // PIECE_GROUP, n_tiles),
        in_specs=[pl.BlockSpec((tm, k), lambda j, i: (i, 0)), pl.BlockSpec((k, width), lambda j, i: (0, j))],
        out_specs=[out_spec(jj) for jj in range(n)],
        out_shape=[jax.ShapeDtypeStruct((r, PIECE_W), F32)] * n, compiler_params=_cparams(2),
    )(a, wb)


def _mm_join(gs, wb, name):
    n, (r, _), k = len(gs), gs[0].shape, wb.shape[0]
    tm = _pick(r, (PIECE_ROWS, ROW_TILE))
    n_groups = n // PIECE_GROUP

    def body(*refs):
        g_refs, w_ref, o_ref = refs[:n], refs[n], refs[n + 1]
        j = pl.program_id(1)
        for group in range(n_groups):
            @pl.when(j == group)
            def _(group=group):
                g = jnp.concatenate([g_refs[group * PIECE_GROUP + p][...].astype(BF16) for p in range(PIECE_GROUP)],
                                    axis=1)
                part = _dg(g, w_ref[...], "nt")
                if group == 0:
                    o_ref[...] = part
                else:
                    o_ref[...] += part

    return pl.pallas_call(
        body, name=name, grid=(r // tm, n_groups),
        in_specs=[pl.BlockSpec((tm, PIECE_W), lambda i, j: (i, 0))] * n
        + [pl.BlockSpec((k, PIECE_GROUP * PIECE_W), lambda i, j: (0, j))],
        out_specs=pl.BlockSpec((tm, k), lambda i, j: (i, 0)), out_shape=jax.ShapeDtypeStruct((r, k), F32),
        compiler_params=_cparams(2),
    )(*gs, wb)


PIECE_GROUP = 4


def _mm_join_tn(a, gs, name):
    n, (r, k) = len(gs), a.shape
    tk = _pick(r, (768, 512, ROW_TILE))
    width = PIECE_GROUP * PIECE_W

    def group_call(group):
        def body(a_ref, *refs):
            g_refs, o_ref = refs[:PIECE_GROUP], refs[PIECE_GROUP]
            part = _dg(a_ref[...], jnp.concatenate([g_ref[...].astype(BF16) for g_ref in g_refs], axis=1), "tn")

            @pl.when(pl.program_id(0) == 0)
            def _():
                o_ref[...] = part

            @pl.when(pl.program_id(0) != 0)
            def _():
                o_ref[...] += part

        return pl.pallas_call(
            body, name=f"{name}{group}", grid=(r // tk,),
            in_specs=[pl.BlockSpec((tk, k), lambda kk: (kk, 0))]
            + [pl.BlockSpec((tk, PIECE_W), lambda kk: (kk, 0))] * PIECE_GROUP,
            out_specs=pl.BlockSpec((k, width), lambda kk: (0, 0)),
            out_shape=jax.ShapeDtypeStruct((k, width), F32), compiler_params=_cparams(1),
        )(a, *gs[group * PIECE_GROUP:(group + 1) * PIECE_GROUP])

    return jnp.concatenate([group_call(group) for group in range(n // PIECE_GROUP)], axis=1)


def rowwise(name, fn, rows, segs, params, out_widths, nct, diff_rows=None, then=None):
    n_row, n_seg, n_par, n_out = len(rows), len(segs), len(params), len(out_widths)
    out_dtype = F32 if then is None else BF16
    diff_rows = [True] * n_row if diff_rows is None else list(diff_rows)
    r_total = rows[0].shape[0]
    n_tiles = r_total // ROW_TILE

    def seg_of(i):
        return jnp.where(i < nct, 0, 1)

    def row_spec(width):
        return pl.BlockSpec((ROW_TILE, width), lambda i: (i, 0))

    def seg_spec(shape):
        nd = len(shape)
        return pl.BlockSpec((1,) + tuple(shape[1:]), lambda i: (seg_of(i),) + (0,) * (nd - 1))

    def par_spec(shape):
        nd = len(shape)
        return pl.BlockSpec(tuple(shape), lambda i: (0,) * nd)

    in_specs = ([row_spec(r.shape[1]) for r in rows] + [seg_spec(s.shape) for s in segs]
                + [par_spec(p.shape) for p in params])

    def load(refs):
        vals = [r[...].astype(F32) for r in refs[:n_row]]
        vals += [r[0].astype(F32) for r in refs[n_row:n_row + n_seg]]
        vals += [r[...].astype(F32) for r in refs[n_row + n_seg:n_row + n_seg + n_par]]
        return vals

    def fwd_call(arrs):
        def body(*refs):
            outs = fn(*load(refs))
            for o_ref, val in zip(refs[n_row + n_seg + n_par:], outs):
                o_ref[...] = val.astype(o_ref.dtype)

        return pl.pallas_call(
            body, name=name + "_f", grid=(n_tiles,), in_specs=in_specs,
            out_specs=[row_spec(w) for w in out_widths],
            out_shape=[jax.ShapeDtypeStruct((r_total, w), out_dtype) for w in out_widths],
            compiler_params=_cparams(1),
        )(*arrs)

    d_idx = [k for k in range(n_row) if diff_rows[k]]

    def bwd_call(arrs, douts):
        n_in = n_row + n_seg + n_par

        def body(*refs):
            i = pl.program_id(0)
            vals = load(refs[:n_in])
            gs = [r[...] for r in refs[n_in:n_in + n_out]]
            out_refs = refs[n_in + n_out:]
            diff_pos = d_idx + list(range(n_row, n_in))

            def f(*dv):
                full = list(vals)
                for pos, v in zip(diff_pos, dv):
                    full[pos] = v
                return tuple(fn(*full))

            _, vjp = jax.vjp(f, *[vals[p] for p in diff_pos])
            grads = vjp(tuple(gs))
            nd = len(d_idx)
            for o_ref, g in zip(out_refs[:nd], grads[:nd]):
                o_ref[...] = g
            first_seg = jnp.logical_or(i == 0, i == nct)
            for o_ref, g in zip(out_refs[nd:nd + n_seg], grads[nd:nd + n_seg]):
                @pl.when(first_seg)
                def _(o_ref=o_ref, g=g):
                    o_ref[0] = g

                @pl.when(jnp.logical_not(first_seg))
                def _(o_ref=o_ref, g=g):
                    o_ref[0] += g
            for o_ref, g in zip(out_refs[nd + n_seg:], grads[nd + n_seg:]):
                @pl.when(i == 0)
                def _(o_ref=o_ref, g=g):
                    o_ref[...] = g

                @pl.when(i != 0)
                def _(o_ref=o_ref, g=g):
                    o_ref[...] += g

        out_specs = ([row_spec(rows[k].shape[1]) for k in d_idx] + [seg_spec(s.shape) for s in segs]
                     + [par_spec(p.shape) for p in params])
        out_shape = ([jax.ShapeDtypeStruct(rows[k].shape, F32) for k in d_idx]
                     + [jax.ShapeDtypeStruct(s.shape, F32) for s in segs]
                     + [jax.ShapeDtypeStruct(p.shape, F32) for p in params])
        return pl.pallas_call(
            body, name=name + "_b", grid=(n_tiles,),
            in_specs=in_specs + [row_spec(w) for w in out_widths],
            out_specs=out_specs, out_shape=out_shape, compiler_params=_cparams(1),
        )(*arrs, *douts)

    @jax.custom_vjp
    def op(*arrs):
        return tuple(fwd_call(arrs))

    def op_fwd(*arrs):
        return tuple(fwd_call(arrs)), arrs

    def op_bwd(arrs, douts):
        grads = list(bwd_call(arrs, douts))
        nd = len(d_idx)
        row_grads = [jnp.zeros_like(arrs[k]) for k in range(n_row)]
        for k, g in zip(d_idx, grads[:nd]):
            row_grads[k] = g
        return tuple(row_grads + grads[nd:])

    if then is None:
        op.defvjp(op_fwd, op_bwd)
        return op(*rows, *segs, *params)

    kind, w, mm_name = then

    if kind == "attention":
        @jax.custom_vjp
        def attended(*arrs):
            return attention_forward(*fwd_call(arrs), w, mm_name)[0]

        def attended_fwd(*arrs):
            qkv = fwd_call(arrs)
            o, parts = attention_forward(*qkv, w, mm_name)
            return o, (arrs, qkv, parts)

        def attended_bwd(res, do):
            arrs, qkv, parts = res
            return op_bwd(arrs, list(attention_backward(*qkv, parts, do, w, mm_name)))

        attended.defvjp(attended_fwd, attended_bwd)
        return attended(*rows, *segs, *params)

    def project(u, wb):
        if kind == "linear":
            return _mm(u, wb, "nn", mm_name + "_f")
        return tuple(_mm_split(u, wb, mm_name + "_f"))

    @jax.custom_vjp
    def fused(w, *arrs):
        return project(fwd_call(arrs)[0], w.astype(BF16))

    def fused_fwd(w, *arrs):
        u, wb = fwd_call(arrs)[0], w.astype(BF16)
        return project(u, wb), (arrs, u, wb)

    def fused_bwd(res, g):
        arrs, u, wb = res
        if kind == "linear":
            du, dw = _mm(g, wb, "nt", mm_name + "_dx"), _mm(u, g, "tn", mm_name + "_dw")
        else:
            du, dw = _mm_join(list(g), wb, mm_name + "_dx"), _mm_join_tn(u, list(g), mm_name + "_dw")
        return (dw,) + op_bwd(arrs, [du])

    fused.defvjp(fused_fwd, fused_bwd)
    return fused(w, *rows, *segs, *params)


ATT_SCALE = MLA_QK ** -0.5
LOG2E = math.log2(math.e)
ATT_KEY_CHUNKS = (768, 512, 256)


ATT_LATENT_TILE = 1024


def _query_rows_spec(row0, tq):
    return pl.BlockSpec((pl.Element(tq), pl.Element(HEAD_PAD)),
                        lambda h, i: (pl.multiple_of(row0 + i * tq, ROW_TILE), pl.multiple_of(h * HEAD_PAD, HEAD_PAD)))


def _key_chunks(nk):
    kc = _pick(nk, ATT_KEY_CHUNKS)
    return [(c * kc, kc) for c in range(nk // kc)]


def _attn_fwd_call(q, k, v, row0, n_rows, tq, nk, name):
    def body(q_ref, k_ref, v_ref, o_ref, lse_ref):
        qv = q_ref[...]
        m = jnp.full((tq, 1), -jnp.inf, F32)
        l = jnp.zeros((tq, 1), F32)
        acc = jnp.zeros((tq, HEAD_PAD), F32)
        for start, size in _key_chunks(nk):
            s = lax.dot_general(qv, k_ref[start:start + size, :], _DOT_DIMS["nt"], preferred_element_type=F32)
            m_new = jnp.maximum(m, jnp.max(s, axis=-1, keepdims=True))
            alpha = jnp.exp2(m - m_new)
            p = jnp.exp2(s - m_new)
            l = alpha * l + jnp.sum(p, axis=-1, keepdims=True)
            acc = alpha * acc + lax.dot_general(p.astype(BF16), v_ref[start:start + size, :], _DOT_DIMS["nn"],
                                                preferred_element_type=F32)
            m = m_new
        o_ref[...] = acc / l
        lse_ref[...] = jnp.broadcast_to(m + jnp.log2(l), (tq, HEAD_PAD))

    out_spec = pl.BlockSpec((tq, HEAD_PAD), lambda h, i: (i, h))
    kv_spec = pl.BlockSpec((nk, HEAD_PAD), lambda h, i: (0, h))
    out = jax.ShapeDtypeStruct((n_rows, q.shape[1]), F32)
    return pl.pallas_call(
        body, name=name, grid=(MLA_HEADS, n_rows // tq), in_specs=[_query_rows_spec(row0, tq), kv_spec, kv_spec],
        out_specs=[out_spec, out_spec], out_shape=[out, out], compiler_params=_cparams(2),
    )(q, k, v)


def _attn_bwd_call(q, k, v, o, lse, do, row0, n_rows, tq, nk, name):
    nq = n_rows // tq

    def body(q_ref, k_ref, v_ref, o_ref, lse_ref, do_ref, dq_ref, dk_ref, dv_ref):
        i = pl.program_id(1)

        @pl.when(i == 0)
        def _():
            dk_ref[...] = jnp.zeros_like(dk_ref)
            dv_ref[...] = jnp.zeros_like(dv_ref)

        qv = q_ref[...]
        dov = do_ref[...]
        dob = dov.astype(BF16)
        lse = lse_ref[:, 0:1]
        delta = jnp.sum(dov * o_ref[...], axis=-1, keepdims=True)
        dq = jnp.zeros((tq, HEAD_PAD), F32)
        for start, size in _key_chunks(nk):
            kk = k_ref[start:start + size, :]
            vv = v_ref[start:start + size, :]
            s = lax.dot_general(qv, kk, _DOT_DIMS["nt"], preferred_element_type=F32)
            p = jnp.exp2(s - lse)
            dp = lax.dot_general(dob, vv, _DOT_DIMS["nt"], preferred_element_type=F32)
            g = (p * (dp - delta)).astype(BF16)
            dk_ref[start:start + size, :] += lax.dot_general(g, qv, _DOT_DIMS["tn"], preferred_element_type=F32)
            dv_ref[start:start + size, :] += lax.dot_general(p.astype(BF16), dob, _DOT_DIMS["tn"],
                                                             preferred_element_type=F32)
            dq = dq + lax.dot_general(g, kk, _DOT_DIMS["nn"], preferred_element_type=F32)
        dq_ref[...] = dq * (1.0 / LOG2E)

        @pl.when(i == nq - 1)
        def _():
            dk_ref[...] = dk_ref[...] * (1.0 / LOG2E)

    own_spec = pl.BlockSpec((tq, HEAD_PAD), lambda h, i: (i, h))
    kv_spec = pl.BlockSpec((nk, HEAD_PAD), lambda h, i: (0, h))
    rows_spec = _query_rows_spec(row0, tq)
    return pl.pallas_call(
        body, name=name, grid=(MLA_HEADS, nq),
        in_specs=[rows_spec, kv_spec, kv_spec, own_spec, own_spec, rows_spec],
        out_specs=[own_spec, kv_spec, kv_spec],
        out_shape=[jax.ShapeDtypeStruct((n_rows, q.shape[1]), F32), jax.ShapeDtypeStruct((nk, q.shape[1]), F32),
                   jax.ShapeDtypeStruct((nk, q.shape[1]), F32)],
        compiler_params=_cparams(2),
    )(q, k, v, o, lse, do)


def _attn_ranges(r_total, tc):
    tq_lat = _pick(r_total - tc, (ATT_LATENT_TILE, ROW_TILE))
    return [(0, tc, ROW_TILE, tc, "_ctx"), (tc, r_total - tc, tq_lat, r_total, "_lat")]


def attention_forward(qs, k, v, tc, name):
    parts = [_attn_fwd_call(qs, k, v, row0, n_rows, tq, nk, name + tag + "_f")
             for row0, n_rows, tq, nk, tag in _attn_ranges(qs.shape[0], tc)]
    return jnp.concatenate([o for o, _ in parts], axis=0), parts


def attention_backward(qs, k, v, parts, do, tc, name):
    r_total = qs.shape[0]
    (dq_c, dk_c, dv_c), (dq_l, dk_l, dv_l) = [
        _attn_bwd_call(qs, k, v, o, lse, do, row0, n_rows, tq, nk, name + tag + "_b")
        for (o, lse), (row0, n_rows, tq, nk, tag) in zip(parts, _attn_ranges(r_total, tc))]
    grow = lambda part: jnp.pad(part, ((0, r_total - tc), (0, 0)))
    return jnp.concatenate([dq_c, dq_l], axis=0), dk_l + grow(dk_c), dv_l + grow(dv_c)


CHUNK_SHIFT = CHUNK.bit_length() - 1


def _block_pairs():
    rows = lax.broadcasted_iota(jnp.int32, (ROW_TILE, ROW_TILE), 0)
    cols = lax.broadcasted_iota(jnp.int32, (ROW_TILE, ROW_TILE), 1)
    same = lax.shift_right_logical(rows, CHUNK_SHIFT) == lax.shift_right_logical(cols, CHUNK_SHIFT)
    return rows, cols, same


def _block_mask(kind):
    rows, cols, same = _block_pairs()
    order = {"lower_incl": rows >= cols, "upper_incl": rows <= cols, "lower_strict": rows > cols,
             "upper_strict": rows < cols}[kind]
    return jnp.logical_and(same, order)


def _row_chunk():
    return lax.shift_right_logical(lax.broadcasted_iota(jnp.int32, (ROW_TILE, 1), 0), CHUNK_SHIFT)


def _dot01(kind, x):
    m = _block_mask(kind).astype(BF16)
    hi = x.astype(BF16)
    rest = x - hi.astype(F32)
    mid = rest.astype(BF16)
    lo = (rest - mid.astype(F32)).astype(BF16)
    terms = jnp.concatenate([hi, mid, lo], axis=1)
    out = lax.dot_general(m, terms, _DOT_DIMS["nn"], preferred_element_type=F32)
    n = x.shape[1]
    return out[:, :n] + out[:, n:2 * n] + out[:, 2 * n:]


def _chunk_sums(x, forward):
    kinds = ("lower_incl", "upper_strict") if forward else ("upper_incl", "lower_strict")
    transposed = ("upper_incl", "lower_strict") if forward else ("lower_incl", "upper_strict")

    @jax.custom_vjp
    def f(x):
        return _dot01(kinds[0], x), _dot01(kinds[1], x)

    def fwd(x):
        return (_dot01(kinds[0], x), _dot01(kinds[1], x)), None

    def bwd(_, g):
        return (_dot01(transposed[0], g[0]) + _dot01(transposed[1], g[1]),)

    f.defvjp(fwd, bwd)
    return f(x)


def _scan_order(forward):
    if forward:
        return list(range(SCAN_CHUNKS)), lambda c: c * CHUNK + CHUNK - 1
    return list(range(SCAN_CHUNKS - 1, -1, -1)), lambda c: c * CHUNK


def _carry_states(forward, st0, inc_all, decay_of):
    order, _ = _scan_order(forward)
    entering = [None] * SCAN_CHUNKS
    st = st0
    for c in order:
        entering[c] = st
        st = st * decay_of(c) + inc_all[:, c * HEAD_PAD:(c + 1) * HEAD_PAD]
    return jnp.concatenate(entering, axis=0), st


def _per_chunk_lanes(x):
    chunk = _row_chunk()
    return jnp.concatenate([jnp.where(chunk == c, x, 0.0) for c in range(SCAN_CHUNKS)], axis=1)


def _own_chunk_lanes(x4):
    chunk = _row_chunk()
    n = x4.shape[1] // SCAN_CHUNKS
    out = jnp.where(chunk == 0, x4[:, :n], 0.0)
    for c in range(1, SCAN_CHUNKS):
        out = out + jnp.where(chunk == c, x4[:, c * n:(c + 1) * n], 0.0)
    return out


def _gla_block(forward, q, k, v, la, st0):
    cum, after = _chunk_sums(la, forward)
    _, last_row = _scan_order(forward)
    q_dec = q * (jnp.exp(cum) * (GLA_DK ** -0.5))
    att = _bdot(q_dec, k * jnp.exp(-cum), "nt")
    att = jnp.where(_block_mask("lower_incl" if forward else "upper_strict"), att, 0.0)
    inc_all = _bdot(v, _per_chunk_lanes(k * jnp.exp(after)), "tn")
    entering, st1 = _carry_states(forward, st0, inc_all,
                                  lambda c: jnp.exp(cum[last_row(c):last_row(c) + 1, :]))
    o = _bdot(att, v, "nn") + _own_chunk_lanes(_bdot(q_dec, entering, "nt"))
    return o, st1


def _ret_block(forward, q, k, v, rd, st0):
    lg = -jnp.exp(rd[0:1, 0:1])
    rows, cols, _ = _block_pairs()
    pos = jnp.bitwise_and(lax.broadcasted_iota(jnp.int32, (ROW_TILE, 1), 0), CHUNK - 1).astype(F32)
    if forward:
        to_end, from_start, rel = CHUNK - 1.0 - pos, pos + 1.0, (rows - cols).astype(F32)
    else:
        to_end, from_start, rel = pos, CHUNK - pos, (cols - rows).astype(F32)
    mask = _block_mask("lower_incl" if forward else "upper_strict")
    dmat = jnp.where(mask, jnp.exp(jnp.where(mask, rel, 0.0) * lg), 0.0)
    att = _bdot(q, k, "nt") * dmat
    inc_all = _bdot(v, _per_chunk_lanes(k * jnp.exp(to_end * lg)), "tn")
    entering, st1 = _carry_states(forward, st0, inc_all, lambda c: jnp.exp(CHUNK * lg))
    o = _bdot(att, v, "nn") + _own_chunk_lanes(_bdot(q, entering, "nt")) * jnp.exp(from_start * lg)
    return o, st1


def scan(kind, forward, q, k, v, aux, tc, name):
    heads = q.shape[1] // HEAD_PAD
    r_total = q.shape[0]
    nblk = r_total // ROW_TILE
    nctb = tc // ROW_TILE
    block_fn = functools.partial(_gla_block if kind == "gla" else _ret_block, forward)
    per_row_aux = kind == "gla"

    def blk(g):
        if forward:
            return g
        return jnp.where(g < nctb, nctb - 1 - g, nblk - 1 - (g - nctb))

    def specs(step_to_g):
        row = pl.BlockSpec((ROW_TILE, heads * HEAD_PAD), lambda s: (blk(step_to_g(s)), 0))
        aux_spec = row if per_row_aux else pl.BlockSpec((heads, 8, HEAD_PAD), lambda s: (0, 0, 0))
        st = pl.BlockSpec((1, heads, HEAD_PAD, HEAD_PAD), lambda s: (step_to_g(s), 0, 0, 0))
        return row, aux_spec, st

    def head_cols(h):
        return slice(h * HEAD_PAD, (h + 1) * HEAD_PAD)

    def fwd_call(q, k, v, aux):
        row, aux_spec, st_spec = specs(lambda s: s)

        def body(q_ref, k_ref, v_ref, a_ref, o_ref, st0_ref, st_ref):
            @pl.when(pl.program_id(0) == 0)
            def _():
                st_ref[...] = jnp.zeros_like(st_ref)

            qv, kv, vv = q_ref[...], k_ref[...], v_ref[...]
            outs = []
            for h in range(heads):
                st0 = st_ref[h]
                st0_ref[0, h] = st0
                a = a_ref[:, head_cols(h)] if per_row_aux else a_ref[h]
                o, st1 = block_fn(qv[:, head_cols(h)], kv[:, head_cols(h)], vv[:, head_cols(h)], a, st0)
                outs.append(o)
                st_ref[h] = st1
            o_ref[...] = jnp.concatenate(outs, axis=1)

        return pl.pallas_call(
            body, name=name + "_f", grid=(nblk,), in_specs=[row, row, row, aux_spec],
            out_specs=[row, st_spec],
            out_shape=[jax.ShapeDtypeStruct(q.shape, F32),
                       jax.ShapeDtypeStruct((nblk, heads, HEAD_PAD, HEAD_PAD), F32)],
            scratch_shapes=[pltpu.VMEM((heads, HEAD_PAD, HEAD_PAD), F32)],
            compiler_params=_cparams(1),
        )(q, k, v, aux)

    def bwd_call(q, k, v, aux, st0s, do):
        row, aux_spec, st_spec = specs(lambda s: nblk - 1 - s)

        def body(q_ref, k_ref, v_ref, a_ref, st0_ref, do_ref, dq_ref, dk_ref, dv_ref, da_ref, dst_ref):
            s = pl.program_id(0)

            @pl.when(s == 0)
            def _():
                dst_ref[...] = jnp.zeros_like(dst_ref)

            qv, kv, vv, dov = q_ref[...], k_ref[...], v_ref[...], do_ref[...]
            grads = []
            for h in range(heads):
                a = a_ref[:, head_cols(h)] if per_row_aux else a_ref[h]
                _, vjp = jax.vjp(block_fn, qv[:, head_cols(h)], kv[:, head_cols(h)], vv[:, head_cols(h)], a,
                                 st0_ref[0, h])
                dq, dk, dv, da, dst0 = vjp((dov[:, head_cols(h)], dst_ref[h]))
                dst_ref[h] = dst0
                grads.append((dq, dk, dv, da))
            dq_ref[...] = jnp.concatenate([g[0] for g in grads], axis=1)
            dk_ref[...] = jnp.concatenate([g[1] for g in grads], axis=1)
            dv_ref[...] = jnp.concatenate([g[2] for g in grads], axis=1)
            if per_row_aux:
                da_ref[...] = jnp.concatenate([g[3] for g in grads], axis=1)
            else:
                da = jnp.stack([g[3] for g in grads], axis=0)

                @pl.when(s == 0)
                def _():
                    da_ref[...] = da

                @pl.when(s != 0)
                def _():
                    da_ref[...] += da

        return pl.pallas_call(
            body, name=name + "_b", grid=(nblk,),
            in_specs=[row, row, row, aux_spec, st_spec, row],
            out_specs=[row, row, row, aux_spec],
            out_shape=[jax.ShapeDtypeStruct(q.shape, F32)] * 3 + [jax.ShapeDtypeStruct(aux.shape, F32)],
            scratch_shapes=[pltpu.VMEM((heads, HEAD_PAD, HEAD_PAD), F32)],
            compiler_params=_cparams(1),
        )(q, k, v, aux, st0s, do)

    @jax.custom_vjp
    def op(q, k, v, aux):
        return fwd_call(q, k, v, aux)[0]

    def fwd(q, k, v, aux):
        o, st0s = fwd_call(q, k, v, aux)
        return o, (q, k, v, aux, st0s)

    def bwd(res, do):
        return tuple(bwd_call(*res, do))

    op.defvjp(fwd, bwd)
    return op(q, k, v, aux)


HALO = 8


def _neighbours(main, prev8, next8, i, nct, n_tiles):
    has_prev = jnp.logical_and(i != 0, i != nct).astype(F32)
    has_next = jnp.logical_and(i != nct - 1, i != n_tiles - 1).astype(F32)
    row = lax.broadcasted_iota(jnp.int32, main.shape, 0)
    down = jnp.where(row == 0, prev8[HALO - 1:HALO] * has_prev, pltpu.roll(main, 1, 0))
    up = jnp.where(row == ROW_TILE - 1, next8[0:1] * has_next, pltpu.roll(main, ROW_TILE - 1, 0))
    return down, up


GELU_K = math.sqrt(2.0 / math.pi)
GELU_A = 0.044715


def _gelu_tanh_grad(x):
    t = jnp.tanh(GELU_K * (x + GELU_A * (x * x * x)))
    return 0.5 * x * (1.0 + t), 0.5 * (1.0 + t) + 0.5 * x * (1.0 - t * t) * (GELU_K * (1.0 + 3.0 * GELU_A * (x * x)))


def conv_ffn_out(gate, up, w8, b, w_out, tc, name):
    r_total, width = gate.shape
    n_tiles = r_total // ROW_TILE
    nct = tc // ROW_TILE
    per = ROW_TILE // HALO
    main_spec = pl.BlockSpec((ROW_TILE, width), lambda i: (i, 0))
    prev_spec = pl.BlockSpec((HALO, width), lambda i: (jnp.maximum(i * per - 1, 0), 0))
    next_spec = pl.BlockSpec((HALO, width), lambda i: (jnp.minimum((i + 1) * per, r_total // HALO - 1), 0))
    w_spec = pl.BlockSpec((8, width), lambda i: (0, 0))
    b_spec = pl.BlockSpec((1, width), lambda i: (0, 0))
    halo3 = [main_spec, prev_spec, next_spec]

    def conv(w_ref, b_ref, down, mid, upn):
        return w_ref[0:1] * down + w_ref[1:2] * mid + w_ref[2:3] * upn + b_ref[...]

    def fwd_call(gate, up, w8, b):
        def body(g_ref, gp_ref, gn_ref, up_ref, w_ref, b_ref, o_ref):
            gv = g_ref[...]
            down, upn = _neighbours(gv, gp_ref[...], gn_ref[...], pl.program_id(0), nct, n_tiles)
            o_ref[...] = (_gelu_tanh(conv(w_ref, b_ref, down, gv, upn)) * up_ref[...]).astype(BF16)

        return pl.pallas_call(
            body, name=name + "_f", grid=(n_tiles,), in_specs=halo3 + [main_spec, w_spec, b_spec],
            out_specs=main_spec, out_shape=jax.ShapeDtypeStruct(gate.shape, BF16), compiler_params=_cparams(1),
        )(gate, gate, gate, up, w8, b)

    def bwd_call(gate, up, w8, b, du):
        def body(g_ref, gp_ref, gn_ref, up_ref, upp_ref, upn_ref, du_ref, dup_ref, dun_ref, w_ref, b_ref,
                 dg_ref, dupo_ref, dw_ref, db_ref):
            i = pl.program_id(0)
            has_prev = jnp.logical_and(i != 0, i != nct).astype(F32)
            has_next = jnp.logical_and(i != nct - 1, i != n_tiles - 1).astype(F32)
            gv, gp, gn = g_ref[...], gp_ref[...], gn_ref[...]
            g_down, g_up = _neighbours(gv, gp, gn, i, nct, n_tiles)
            act, slope = _gelu_tanh_grad(conv(w_ref, b_ref, g_down, gv, g_up))
            duv = du_ref[...]
            dupo_ref[...] = duv * act
            dc = duv * up_ref[...] * slope
            c_above = conv(w_ref, b_ref, gp[HALO - 2:HALO - 1], gp[HALO - 1:HALO], gv[0:1])
            c_below = conv(w_ref, b_ref, gv[ROW_TILE - 1:ROW_TILE], gn[0:1], gn[1:2])
            dc_above = dup_ref[HALO - 1:HALO] * upp_ref[HALO - 1:HALO] * _gelu_tanh_grad(c_above)[1] * has_prev
            dc_below = dun_ref[0:1] * upn_ref[0:1] * _gelu_tanh_grad(c_below)[1] * has_next
            row = lax.broadcasted_iota(jnp.int32, dc.shape, 0)
            dc_down = jnp.where(row == 0, dc_above, pltpu.roll(dc, 1, 0))
            dc_up = jnp.where(row == ROW_TILE - 1, dc_below, pltpu.roll(dc, ROW_TILE - 1, 0))
            dg_ref[...] = w_ref[0:1] * dc_up + w_ref[1:2] * dc + w_ref[2:3] * dc_down
            dw = jnp.concatenate([jnp.sum(dc * g_down, axis=0, keepdims=True),
                                  jnp.sum(dc * gv, axis=0, keepdims=True),
                                  jnp.sum(dc * g_up, axis=0, keepdims=True),
                                  jnp.zeros((5, width), F32)], axis=0)
            db = jnp.sum(dc, axis=0, keepdims=True)

            @pl.when(i == 0)
            def _():
                dw_ref[...] = dw
                db_ref[...] = db

            @pl.when(i != 0)
            def _():
                dw_ref[...] += dw
                db_ref[...] += db

        return pl.pallas_call(
            body, name=name + "_b", grid=(n_tiles,), in_specs=halo3 * 3 + [w_spec, b_spec],
            out_specs=[main_spec, main_spec, w_spec, b_spec],
            out_shape=[jax.ShapeDtypeStruct(gate.shape, F32), jax.ShapeDtypeStruct(gate.shape, F32),
                       jax.ShapeDtypeStruct((8, width), F32), jax.ShapeDtypeStruct((1, width), F32)],
            compiler_params=_cparams(1),
        )(gate, gate, gate, up, up, up, du, du, du, w8, b)

    @jax.custom_vjp
    def op(gate, up, w8, b, w_out):
        return _mm(fwd_call(gate, up, w8, b), w_out.astype(BF16), "nn", name + "_out_f")

    def fwd(gate, up, w8, b, w_out):
        u, wb = fwd_call(gate, up, w8, b), w_out.astype(BF16)
        return _mm(u, wb, "nn", name + "_out_f"), (gate, up, w8, b, u, wb)

    def bwd(res, g):
        gate, up, w8, b, u, wb = res
        du = _mm(g, wb, "nt", name + "_out_dx")
        d_gate, d_up, d_w8, d_b = bwd_call(gate, up, w8, b, du)
        return d_gate, d_up, d_w8, d_b, _mm(u, g, "tn", name + "_out_dw")

    op.defvjp(fwd, bwd)
    return op(gate, up, w8, b, w_out)


def loss_head(h, target, tc, name):
    r_total, width = h.shape
    n_tiles = r_total // ROW_TILE
    nct = tc // ROW_TILE

    def call(h, target):
        def body(h_ref, t_ref, dh_ref, loss_ref, acc_ref):
            i = pl.program_id(0)

            @pl.when(i == 0)
            def _():
                acc_ref[...] = jnp.zeros_like(acc_ref)

            @pl.when(i < nct)
            def _():
                dh_ref[...] = jnp.zeros_like(dh_ref)

            @pl.when(i >= nct)
            def _():
                err = h_ref[...] - t_ref[...]
                dh_ref[...] = err * (1.0 / width)
                acc_ref[...] += jnp.sum((err * err).reshape(ROW_TILE // 8, 8, width), axis=0)

            @pl.when(i == n_tiles - 1)
            def _():
                loss_ref[...] = jnp.sum(acc_ref[...]).reshape(1, 1) * (0.5 / width)

        row = pl.BlockSpec((ROW_TILE, width), lambda i: (i, 0))
        return pl.pallas_call(
            body, name=name, grid=(n_tiles,),
            in_specs=[row, pl.BlockSpec((ROW_TILE, width), lambda i: (jnp.maximum(i - nct, 0), 0))],
            out_specs=[row, pl.BlockSpec((1, 1), lambda i: (0, 0))],
            out_shape=[jax.ShapeDtypeStruct(h.shape, F32), jax.ShapeDtypeStruct((1, 1), F32)],
            scratch_shapes=[pltpu.VMEM((8, width), F32)], compiler_params=_cparams(1),
        )(h, target)

    @jax.custom_vjp
    def op(h, target):
        return call(h, target)[1][0, 0]

    def fwd(h, target):
        dh, loss = call(h, target)
        return loss[0, 0], (dh, target)

    def bwd(res, g):
        dh, target = res
        return dh * g, jnp.zeros_like(target)

    op.defvjp(fwd, bwd)
    return op(h, target)


PACK_W = 1024
PACK_TILE = 128


def slab_sum(slabs, name):
    n_slab, n, _ = slabs.shape

    def body(s_ref, o_ref):
        acc = s_ref[0]
        for j in range(1, n_slab):
            acc = acc + s_ref[j]
        o_ref[...] = acc

    return pl.pallas_call(
        body, name=name, grid=(n // PACK_TILE,),
        in_specs=[pl.BlockSpec((n_slab, PACK_TILE, PACK_W), lambda i: (0, i, 0))],
        out_specs=pl.BlockSpec((PACK_TILE, PACK_W), lambda i: (i, 0)),
        out_shape=jax.ShapeDtypeStruct((n, PACK_W), F32), compiler_params=_cparams(1),
    )(slabs)


def adamw(g_slabs, w, m, v, name):
    n_slab, n, _ = g_slabs.shape

    def body(g_ref, w_ref, m_ref, v_ref, go_ref, d_ref, mo_ref, vo_ref):
        g = g_ref[0].astype(F32)
        for j in range(1, n_slab):
            g = g + g_ref[j].astype(F32)
        m_new = ADAM_B1 * m_ref[...] + (1.0 - ADAM_B1) * g
        v_new = ADAM_B2 * v_ref[...] + (1.0 - ADAM_B2) * (g * g)
        m_hat = m_new / (1.0 - ADAM_B1 ** ADAM_STEP)
        v_hat = v_new / (1.0 - ADAM_B2 ** ADAM_STEP)
        go_ref[...] = g
        d_ref[...] = -ADAM_LR * (m_hat / (jnp.sqrt(v_hat) + ADAM_EPS) + ADAM_WD * w_ref[...])
        mo_ref[...] = m_new
        vo_ref[...] = v_new

    flat = pl.BlockSpec((PACK_TILE, PACK_W), lambda i: (i, 0))
    return pl.pallas_call(
        body, name=name, grid=(n // PACK_TILE,),
        in_specs=[pl.BlockSpec((n_slab, PACK_TILE, PACK_W), lambda i: (0, i, 0)), flat, flat, flat],
        out_specs=[flat] * 4, out_shape=[jax.ShapeDtypeStruct((n, PACK_W), F32)] * 4, compiler_params=_cparams(1),
    )(g_slabs, w, m, v)


def all_gather(x, name):
    m_per, n = x.shape

    def body(x_ref, out_ref, send_sems, recv_sems, local_sem):
        px, py, pc = lax.axis_index("x"), lax.axis_index("y"), lax.axis_index("c")
        me, sibling = (px, py, pc), (px, py, 1 - pc)
        chips = [(1 - px, py), (px, 1 - py), (1 - px, 1 - py)]

        def rows(bx, by, bc):
            return out_ref.at[pl.ds((4 * bx + 2 * by + bc) * m_per, m_per), :]

        def copy(k, block, to, src=None):
            return pltpu.make_async_remote_copy(
                src_ref=rows(*block) if src is None else src, dst_ref=rows(*block),
                send_sem=send_sems.at[k], recv_sem=recv_sems.at[k], device_id=to, device_id_type=MESH)

        mine = pltpu.make_async_copy(x_ref, rows(*me), local_sem)
        mine.start()
        first = [copy(0, me, sibling, src=x_ref)]
        first += [copy(1 + j, me, (*chip, pc), src=x_ref) for j, chip in enumerate(chips)]
        for cp in first:
            cp.start()
        passed = [copy(4 + j, (*chip, pc), sibling) for j, chip in enumerate(chips)]
        for j, chip in enumerate(chips):
            copy(1 + j, (*chip, pc), me).wait_recv()
            passed[j].start()
        copy(0, sibling, me).wait_recv()
        for j, chip in enumerate(chips):
            copy(4 + j, (*chip, 1 - pc), me).wait_recv()
        for cp in first + passed:
            cp.wait_send()
        mine.wait()

    return pl.pallas_call(
        body, name=name, out_shape=jax.ShapeDtypeStruct((N_DEV * m_per, n), x.dtype),
        in_specs=[pl.BlockSpec(memory_space=pl.ANY)], out_specs=pl.BlockSpec(memory_space=pl.ANY),
        scratch_shapes=[pltpu.SemaphoreType.DMA((7,)), pltpu.SemaphoreType.DMA((7,)), pltpu.SemaphoreType.DMA],
    )(x)


N_CHIP = 4


def pair_swap(x, name):
    def body(x_ref, out_ref, send_sem, recv_sem):
        sibling = (lax.axis_index("x"), lax.axis_index("y"), 1 - lax.axis_index("c"))
        copy = pltpu.make_async_remote_copy(src_ref=x_ref, dst_ref=out_ref, send_sem=send_sem, recv_sem=recv_sem,
                                            device_id=sibling, device_id_type=MESH)
        copy.start()
        copy.wait()

    return pl.pallas_call(
        body, name=name, out_shape=jax.ShapeDtypeStruct(x.shape, x.dtype),
        in_specs=[pl.BlockSpec(memory_space=pl.ANY)], out_specs=pl.BlockSpec(memory_space=pl.ANY),
        scratch_shapes=[pltpu.SemaphoreType.DMA, pltpu.SemaphoreType.DMA],
    )(x)


def pair_add(a, b, name):
    n_slab, n, _ = a.shape

    def body(a_ref, b_ref, o_ref):
        o_ref[...] = (a_ref[...].astype(F32) + b_ref[...].astype(F32)).astype(o_ref.dtype)

    spec = pl.BlockSpec((1, PACK_TILE, PACK_W), lambda s, i: (s, i, 0))
    return pl.pallas_call(
        body, name=name, grid=(n_slab, n // PACK_TILE), in_specs=[spec, spec], out_specs=spec,
        out_shape=jax.ShapeDtypeStruct(a.shape, a.dtype), compiler_params=_cparams(2),
    )(a, b)


def chip_all_to_all(x, name):
    def body(x_ref, out_ref, send_sems, recv_sems, local_sem):
        px, py, pc = lax.axis_index("x"), lax.axis_index("y"), lax.axis_index("c")
        mine_idx = 2 * px + py
        local = pltpu.make_async_copy(x_ref.at[mine_idx], out_ref.at[mine_idx], local_sem)
        local.start()
        copies = []
        for k, (fx, fy) in enumerate(((0, 1), (1, 0), (1, 1))):
            qx, qy = px ^ fx, py ^ fy
            peer_idx = 2 * qx + qy
            copies.append((
                pltpu.make_async_remote_copy(
                    src_ref=x_ref.at[peer_idx], dst_ref=out_ref.at[mine_idx], send_sem=send_sems.at[k],
                    recv_sem=recv_sems.at[k], device_id=(qx, qy, pc), device_id_type=MESH),
                pltpu.make_async_remote_copy(
                    src_ref=x_ref.at[peer_idx], dst_ref=out_ref.at[peer_idx], send_sem=send_sems.at[k],
                    recv_sem=recv_sems.at[k], device_id=(qx, qy, pc), device_id_type=MESH)))
        for send, _ in copies:
            send.start()
        for _, landing in copies:
            landing.wait_recv()
        for send, _ in copies:
            send.wait_send()
        local.wait()

    return pl.pallas_call(
        body, name=name, out_shape=jax.ShapeDtypeStruct(x.shape, x.dtype),
        in_specs=[pl.BlockSpec(memory_space=pl.ANY)], out_specs=pl.BlockSpec(memory_space=pl.ANY),
        scratch_shapes=[pltpu.SemaphoreType.DMA((3,)), pltpu.SemaphoreType.DMA((3,)), pltpu.SemaphoreType.DMA],
    )(x)


IN_OFFSETS = {}
_off = 0
for _name, _width in (("mla_q", 256), ("mla_kv", 128), ("mla_kr", 32), ("gla_q", 512), ("gla_k", 512), ("gla_v", 512),
                      ("gla_g", 512), ("gla_rf", 16), ("gla_rb", 16), ("ret_q", 512), ("ret_k", 512), ("ret_v", 512),
                      ("ret_g", 512), ("gate_mla", 1024), ("gate_gla", 1024), ("gate_ret", 1024)):
    IN_OFFSETS[_name] = (_off, _off + _width)
    _off += _width
N_IN = _off

P_GLA, P_RET, P_GATE, P_MLAQ, P_MLAKV, P_MLAKR, P_RANK, P_END = 0, 2048, 4096, 7168, 7424, 7552, 7680, 7808


def _pad_in_proj(w):
    def cols(a, b):
        return w[:, IN_OFFSETS[a][0]:IN_OFFSETS[b][1]]

    def z(n):
        return jnp.zeros((w.shape[0], n), w.dtype)

    return jnp.concatenate([cols("gla_q", "gla_g"), cols("ret_q", "ret_g"), cols("gate_mla", "gate_ret"),
                            cols("mla_q", "mla_kv"), z(MLA_NOPE), cols("mla_kr", "mla_kr"),
                            z(HEAD_PAD - MLA_QK), cols("gla_rf", "gla_rb"), z(HEAD_PAD - 2 * GLA_RANK),
                            z(N_IN_PAD - P_END)], axis=1)


def _pad_last(a, n):
    return jnp.pad(a, [(0, 0)] * (a.ndim - 1) + [(0, n - a.shape[-1])])


def _position_tables(tc, t):
    pos = jnp.arange(t)
    inv = ROPE_THETA ** (-jnp.arange(MLA_ROPE // 4, dtype=F32) * 2.0 / (MLA_ROPE // 2))
    ang_r = (pos // GRID_W).astype(F32)[:, None] * inv[None, :]
    ang_c = (pos % GRID_W).astype(F32)[:, None] * inv[None, :]
    z8, z32, z64 = jnp.zeros((t, 8), F32), jnp.zeros((t, 32), F32), jnp.zeros((t, 64), F32)
    lat_c = jnp.concatenate([jnp.ones((t, 64), F32), jnp.cos(ang_r), jnp.cos(ang_r), jnp.cos(ang_c), jnp.cos(ang_c),
                             z32], axis=1)
    lat_sn = jnp.concatenate([z64, -jnp.sin(ang_r), z8, -jnp.sin(ang_c), z8, z32], axis=1)
    lat_sp = jnp.concatenate([z64, z8, jnp.sin(ang_r), z8, jnp.sin(ang_c), z32], axis=1)
    ctx_c = jnp.concatenate([jnp.ones((tc, MLA_QK), F32), jnp.zeros((tc, HEAD_PAD - MLA_QK), F32)], axis=1)
    ctx_z = jnp.zeros((tc, HEAD_PAD), F32)
    rinv = 1.0 / (RET_THETA ** jnp.linspace(0.0, 1.0, RET_DK // 2, dtype=F32))
    rang = jnp.arange(tc + t).astype(F32)[:, None] * rinv[None, :]
    return dict(c=jnp.concatenate([ctx_c, lat_c]), sn=jnp.concatenate([ctx_z, lat_sn]),
                sp=jnp.concatenate([ctx_z, lat_sp]),
                rc=jnp.concatenate([jnp.cos(rang), jnp.cos(rang)], axis=1),
                rs=jnp.concatenate([-jnp.sin(rang), jnp.sin(rang)], axis=1))


def _heads(x):
    return [x[:, h * HEAD_PAD:(h + 1) * HEAD_PAD] for h in range(x.shape[1] // HEAD_PAD)]


def _mla_rope(x, c, sn, sp):
    return x * c + _roll(x, HEAD_PAD - 8, 1) * sn + _roll(x, 8, 1) * sp


def _norm_mod_fn(shift_row, scale_row):
    def fn(h, mod, w):
        return (_rms(h, D, w) * (1.0 + mod[scale_row:scale_row + 1]) + mod[shift_row:shift_row + 1],)
    return fn


def _resid_fn(gate_row):
    def fn(h, y, mod):
        return (h + mod[gate_row:gate_row + 1] * y,)
    return fn


def _resid_norm_fn(gate_row, shift_row, scale_row):
    def fn(h, y, mod, w):
        h1 = h + mod[gate_row:gate_row + 1] * y
        return h1, _norm_mod_fn(shift_row, scale_row)(h1, mod, w)[0]
    return fn


def _mla_prep_fn(x, c, sn, sp, q_norm_a, w_qb, q_norm, kv_norm_a, w_k, w_v, k_norm):
    cq, ckv = x[:, :MLA_Q_LORA], x[:, MLA_Q_LORA:MLA_Q_LORA + MLA_KV_LORA]
    kr = x[:, MLA_Q_LORA + MLA_KV_LORA:]
    qf = _bdot(_rms(cq, MLA_Q_LORA, q_norm_a), w_qb, "nn")
    q = jnp.concatenate([_mla_rope(_rms(qh, MLA_QK, q_norm), c, sn, sp) for qh in _heads(qf)], axis=1)
    xkv = _rms(ckv, MLA_KV_LORA, kv_norm_a)
    kf = _bdot(xkv, w_k, "nn")
    k = jnp.concatenate([_mla_rope(_rms(kh + kr, MLA_QK, k_norm), c, sn, sp) for kh in _heads(kf)], axis=1)
    return q * (ATT_SCALE * LOG2E), k, _bdot(xkv, w_v, "nn")


def _decay_fn(x, w2, b):
    la = _log_sigmoid(_bdot(x[:, :HEAD_PAD], w2, "nn") + b) * (1.0 / GLA_NORMALIZER)
    return la[:, :GLA_HEADS * GLA_DK], la[:, GLA_HEADS * GLA_DK:]


def _ret_rot_fn(q, k, rc, rs):
    def rot(x, scale):
        return jnp.concatenate([(xh * rc + _roll(xh, RET_DK // 2, 1) * rs) * scale for xh in _heads(x)], axis=1)
    return rot(q, 1.0), rot(k, RET_DK ** -0.5)


def _gla_out_fn(o_f, o_b, g, w):
    y = jnp.concatenate([_rms(oh, HEAD_PAD, w) for oh in _heads(o_f + o_b)], axis=1)
    return (y * _silu(g),)


def _ret_out_fn(o_f, o_b, g):
    y = jnp.concatenate([_rms(oh, HEAD_PAD) for oh in _heads(o_f + o_b)], axis=1)
    return (y * _silu(g),)


def _merge_fn(z0, z1, z2, g0a, g0b, g1a, g1b, g2a, g2b, bg):
    out = 0.0
    for n, (z, ga, gb) in enumerate(((z0, g0a, g0b), (z1, g1a, g1b), (z2, g2a, g2b))):
        out = out + jax.nn.sigmoid(jnp.concatenate([ga, gb], axis=1) + bg[n:n + 1]) * z
    return (out,)


def _layer(l, h, mod, w, tabs, tc):
    nct = tc // ROW_TILE
    tag = f"_l{l}"
    row = lambda a: a[l][None]
    pieces = rowwise("norm1" + tag, _norm_mod_fn(0, 1), [h], [mod], [row(w["norm1_w"])], [D], nct,
                     then=("pieces", _pad_in_proj(w["w_in"][l]), "in_proj" + tag))
    piece = lambda start: pieces[start // PIECE_W]

    w_qb = _pad_last(w["mla_w_qb"][l].reshape(MLA_Q_LORA, MLA_HEADS, MLA_QK), HEAD_PAD).reshape(MLA_Q_LORA, -1)
    w_kvb = w["mla_w_kvb"][l].reshape(MLA_KV_LORA, MLA_HEADS, MLA_NOPE + MLA_V)
    w_k = _pad_last(w_kvb[:, :, :MLA_NOPE], HEAD_PAD).reshape(MLA_KV_LORA, -1)
    w_v = _pad_last(w_kvb[:, :, MLA_NOPE:], HEAD_PAD).reshape(MLA_KV_LORA, -1)
    rope = [tabs["c"], tabs["sn"], tabs["sp"]]
    y_mla = rowwise("mla_prep" + tag, _mla_prep_fn, [piece(P_MLAQ)] + rope, [],
                    [row(w["mla_q_norm_a"]), w_qb, _pad_last(row(w["mla_q_norm"]), HEAD_PAD),
                     row(w["mla_kv_norm_a"]), w_k, w_v, _pad_last(row(w["mla_k_norm"]), HEAD_PAD)],
                    [MLA_HEADS * HEAD_PAD] * 3, nct, diff_rows=[True, False, False, False],
                    then=("attention", tc, "attn" + tag))
    wb_mla = _pad_last(w["w_branch"][l, 0].reshape(MLA_HEADS, MLA_V, D).transpose(0, 2, 1), HEAD_PAD)
    wb_mla = wb_mla.transpose(0, 2, 1).reshape(MLA_HEADS * HEAD_PAD, D)

    w2 = jnp.zeros((HEAD_PAD, 2 * GLA_HEADS * GLA_DK), F32)
    w2 = w2.at[:GLA_RANK, :GLA_HEADS * GLA_DK].set(w["gla_w_gk2"][l, 0])
    w2 = w2.at[GLA_RANK:2 * GLA_RANK, GLA_HEADS * GLA_DK:].set(w["gla_w_gk2"][l, 1])
    la_f, la_b = rowwise("gla_decay" + tag, _decay_fn, [piece(P_RANK)], [],
                         [w2, w["gla_b_gk"][l].reshape(1, -1)], [GLA_HEADS * GLA_DK] * 2, nct)
    gq, gk, gv, gg = [piece(P_GLA + n * PIECE_W) for n in range(4)]
    o_f = scan("gla", True, gq, gk, gv, la_f, tc, "gla_fw" + tag)
    o_b = scan("gla", False, gq, gk, gv, la_b, tc, "gla_bw" + tag)
    z_gla = rowwise("gla_out" + tag, _gla_out_fn, [o_f, o_b, gg], [], [row(w["gla_o_norm"])], [512], nct,
                    then=("linear", w["w_branch"][l, 1], "branch_gla" + tag))

    rq, rk = rowwise("ret_rot" + tag, _ret_rot_fn, [piece(P_RET), piece(P_RET + PIECE_W), tabs["rc"], tabs["rs"]],
                     [], [], [512, 512], nct, diff_rows=[True, True, False, False])
    rv, rg = piece(P_RET + 2 * PIECE_W), piece(P_RET + 3 * PIECE_W)
    rd = jnp.broadcast_to(w["ret_decay"][l][:, :, None, None], (2, RET_HEADS, 8, HEAD_PAD))
    r_f = scan("ret", True, rq, rk, rv, rd[0], tc, "ret_fw" + tag)
    r_b = scan("ret", False, rq, rk, rv, rd[1], tc, "ret_bw" + tag)
    z_ret = rowwise("ret_out" + tag, _ret_out_fn, [r_f, r_b, rg], [], [], [512], nct,
                    then=("linear", w["w_branch"][l, 2], "branch_ret" + tag))

    z = [linear(y_mla, wb_mla, "branch_mla" + tag), z_gla, z_ret]
    gates = [piece(P_GATE + n * PIECE_W) for n in range(6)]
    y = rowwise("merge" + tag, _merge_fn, z + gates, [], [_pad_rows(w["b_gate"][l], 8)], [D], nct,
                then=("linear", w["w_out"][l], "w_out" + tag))
    h, a2 = rowwise("resid1_norm2" + tag, _resid_norm_fn(2, 3, 4), [h, y], [mod], [row(w["norm2_w"])], [D, D], nct)
    gate = linear(a2, w["w_ffn_in"][l][:, :D_FF], "ffn_gate" + tag)
    up = linear(a2, w["w_ffn_in"][l][:, D_FF:], "ffn_up" + tag)
    f = conv_ffn_out(gate, up, _pad_rows(w["w_dw"][l], 8), row(w["b_dw"]), w["w_ffn_out"][l], tc, "ffn_mid" + tag)
    return rowwise("resid2" + tag, _resid_fn(5), [h, f], [mod], [], [D], nct)[0]


def _pad_rows(a, n):
    return jnp.pad(a, [(0, n - a.shape[0])] + [(0, 0)] * (a.ndim - 1))


def local_loss(w, mod, x, ctx, target):
    tc, t = ctx.shape[0], x.shape[0]
    tabs = _position_tables(tc, t)
    h = jnp.concatenate([ctx, x], axis=0)
    for l in range(DEPTH):
        h = _layer(l, h, mod[l], w, tabs, tc)
    return loss_head(h, target, tc, "loss_head")


ADA_ROWS = 16


def ada_forward(cond_in, w_ada, b_loc):
    cols = w_ada.shape[2]

    def body(x_ref, w_ref, b_ref, o_ref):
        s = _silu(x_ref[...])
        for l in range(DEPTH):
            o_ref[l] = _dg(s, w_ref[l], "nn") + b_ref[l]

    return pl.pallas_call(
        body, name="ada_forward", out_shape=jax.ShapeDtypeStruct((DEPTH, ADA_ROWS, cols), F32),
        compiler_params=pltpu.CompilerParams(vmem_limit_bytes=VMEM_LIMIT_BYTES),
    )(cond_in, w_ada, b_loc)


def ada_backward(cond_in, g_loc, dmod_own, w_ada):
    cols = w_ada.shape[2]

    def body(x_ref, g_ref, own_ref, w_ref, gw_ref, dc_ref, gb_ref):
        x = x_ref[...]
        s = _silu(x)
        dcond = jnp.zeros((8, D), F32)
        for l in range(DEPTH):
            g_ctx = jnp.sum(g_ref[2 * l], axis=0, keepdims=True)
            g_rows = jnp.concatenate([g_ref[2 * l + 1], jnp.broadcast_to(g_ctx, (8, cols))], axis=0)
            keep = lax.broadcasted_iota(jnp.int32, (ADA_ROWS, cols), 0) <= N_DEV
            gw_ref[l] = _dg(s, jnp.where(keep, g_rows, 0.0), "tn")
            dcond = dcond + _dg(jnp.broadcast_to(g_ctx, (8, cols)), w_ref[l], "nt")
            gb_ref[l:l + 1, :] = own_ref[2 * l:2 * l + 1, :] + own_ref[2 * l + 1:2 * l + 2, :]
        xc = x[N_DEV:N_DEV + 1]
        sig = jax.nn.sigmoid(xc)
        dc_ref[...] = dcond[0:1] * (sig * (1.0 + xc * (1.0 - sig)))

    return pl.pallas_call(
        body, name="ada_backward",
        out_shape=[jax.ShapeDtypeStruct(w_ada.shape, F32), jax.ShapeDtypeStruct((1, D), F32),
                   jax.ShapeDtypeStruct((DEPTH, 6 * D), F32)],
        compiler_params=pltpu.CompilerParams(vmem_limit_bytes=VMEM_LIMIT_BYTES),
    )(cond_in, g_loc, dmod_own, w_ada)


WEIGHTS = ["c_ctx", "w_ada", "b_ada", "norm1_w", "norm2_w", "w_in", "b_gate", "mla_q_norm_a", "mla_w_qb",
           "mla_kv_norm_a", "mla_w_kvb", "mla_q_norm", "mla_k_norm", "gla_w_gk2", "gla_b_gk", "gla_o_norm",
           "ret_decay", "w_branch", "w_out", "w_ffn_in", "w_dw", "b_dw", "w_ffn_out"]
INPUTS = ["x", "c", "ctx"] + WEIGHTS + ["loss_target"] + ["m_" + n for n in WEIGHTS] + ["v_" + n for n in WEIGHTS]
BIG = {"w_in": 2, "mla_w_qb": 2, "mla_w_kvb": 2, "w_branch": 3, "w_out": 1, "w_ffn_in": 2, "w_ffn_out": 1}
SMALL_SHARDED = {"b_gate": 2, "gla_w_gk2": 3, "gla_b_gk": 2, "w_dw": 2}
SMALL = ["c_ctx", "b_ada", "norm1_w", "norm2_w", "b_gate", "mla_q_norm_a", "mla_kv_norm_a", "mla_q_norm", "mla_k_norm",
         "gla_w_gk2", "gla_b_gk", "gla_o_norm", "ret_decay", "w_dw", "b_dw"]


def _entry_rows(size, align):
    return -(-size // (PACK_W * align)) * align


def _pack(arrays, rows, dtype, align, lead=0):
    parts = []
    for a in arrays:
        head = a.shape[:lead]
        size = math.prod(a.shape[lead:])
        r = _entry_rows(size, align)
        if r * PACK_W == size:
            parts.append(a.astype(dtype).reshape(head + (r, PACK_W)))
        else:
            flat = jnp.pad(a.astype(dtype).reshape(head + (size,)), [(0, 0)] * lead + [(0, r * PACK_W - size)])
            parts.append(flat.reshape(head + (r, PACK_W)))
    used = sum(p.shape[lead] for p in parts)
    if rows > used:
        parts.append(jnp.zeros(parts[0].shape[:lead] + (rows - used, PACK_W), dtype))
    return jnp.concatenate(parts, axis=lead)


def _pack_rows(shapes, align, multiple):
    used = sum(_entry_rows(math.prod(s), align) for s in shapes)
    return -(-used // multiple) * multiple


def _unpack(pack, shapes, align):
    head = pack.shape[:-2]
    out, off = [], 0
    for shape in shapes:
        size = math.prod(shape)
        r = _entry_rows(size, align)
        block = lax.slice_in_dim(pack, off, off + r, axis=len(head))
        if r * PACK_W != size:
            block = block.reshape(head + (r * PACK_W,))[..., :size]
        out.append(block.reshape(head + tuple(shape)))
        off += r
    return out


def _join_shards(stacked, axis):
    moved = jnp.moveaxis(stacked, 0, axis)
    shape = list(moved.shape)
    return moved.reshape(shape[:axis] + [shape[axis] * shape[axis + 1]] + shape[axis + 2:])


def _split_shards(full, axis):
    shape = list(full.shape)
    split = full.reshape(shape[:axis] + [N_DEV, shape[axis] // N_DEV] + shape[axis + 1:])
    return jnp.moveaxis(split, axis, 0)


def _gather_shards(local, axes, dtype, rows_multiple, name):
    names = list(axes)
    shapes = [local[n].shape for n in names]
    rows = _pack_rows(shapes, rows_multiple, rows_multiple)
    gathered = all_gather(_pack([local[n] for n in names], rows, dtype, rows_multiple), name)
    stacked = _unpack(gathered.reshape(N_DEV, rows, PACK_W), shapes, rows_multiple)
    return {n: _join_shards(s, axes[n]).astype(F32) for n, s in zip(names, stacked)}


def kernel(*args):
    a = dict(zip(INPUTS, args))
    me = 4 * lax.axis_index("x") + 2 * lax.axis_index("y") + lax.axis_index("c")
    cols = a["w_ada"].shape[2]

    small_names = list(SMALL_SHARDED)
    small_local = [a[n].shape for n in small_names]
    first_rows = _pack_rows([a["c"].shape] + small_local, 8, 8)
    first = all_gather(_pack([a["c"]] + [a[n] for n in small_names], first_rows, F32, 8), "gather_small")
    first = _unpack(first.reshape(N_DEV, first_rows, PACK_W), [a["c"].shape] + small_local, 8)
    c_all = first[0][:, 0]

    cond_in = jnp.concatenate([c_all, a["c_ctx"][None], jnp.zeros((ADA_ROWS - N_DEV - 1, D), F32)], axis=0)
    b_loc = lax.dynamic_slice_in_dim(a["b_ada"], me * cols, cols, axis=1)[:, None, :]
    mod_loc = ada_forward(cond_in, a["w_ada"], b_loc)
    mod_all = all_gather(mod_loc.reshape(DEPTH * ADA_ROWS, cols), "gather_mod")
    mod_all = mod_all.reshape(N_DEV, DEPTH, ADA_ROWS, cols).transpose(1, 2, 0, 3).reshape(DEPTH, ADA_ROWS, 6, D)
    mod_me = lax.dynamic_index_in_dim(mod_all, me, axis=1, keepdims=False)
    mod = jnp.pad(jnp.stack([mod_all[:, N_DEV], mod_me], axis=1), ((0, 0), (0, 0), (0, 2), (0, 0)))

    w = _gather_shards(a, BIG, BF16, 16, "gather_weights")
    w.update({n: _join_shards(s, SMALL_SHARDED[n]) for n, s in zip(small_names, first[1:])})
    for n in SMALL:
        if n not in SMALL_SHARDED and n not in ("c_ctx", "b_ada"):
            w[n] = a[n]

    loss, (gw, gmod, gx) = jax.value_and_grad(local_loss, argnums=(0, 1, 2))(
        w, mod, a["x"][0], a["ctx"][0], a["loss_target"][0])
    loss = lax.psum(loss, ("x", "y", "c"))

    dmod_own = gmod[:, :, :6].reshape(2 * DEPTH, 6 * D)
    g_all = all_gather(jnp.pad(dmod_own, ((0, 8 - 2 * DEPTH), (0, 0))), "gather_dmod").reshape(N_DEV, 8, 6 * D)
    g_loc = lax.dynamic_slice_in_dim(g_all[:, :2 * DEPTH], me * cols, cols, axis=2).transpose(1, 0, 2)
    g_w_ada, g_c_ctx, g_b_ada = ada_backward(cond_in, g_loc, dmod_own, a["w_ada"])

    small_part = dict(gw, c_ctx=g_c_ctx, b_ada=g_b_ada)
    small_shapes = [a[n].shape if n not in SMALL_SHARDED else gw[n].shape for n in SMALL]
    rows = _pack_rows(small_shapes, 8, PACK_TILE)
    parts = all_gather(_pack([small_part[n] for n in SMALL], rows, F32, 8), "gather_small_grads")
    small_sum = _unpack(slab_sum(parts.reshape(N_DEV, rows, PACK_W), "sum_small_grads"), small_shapes, 8)
    g_small = {}
    for n, g in zip(SMALL, small_sum):
        if n in SMALL_SHARDED:
            ax = SMALL_SHARDED[n]
            g = lax.dynamic_slice_in_dim(g, me * a[n].shape[ax], a[n].shape[ax], axis=ax)
        g_small[n] = g

    big_rows = _pack_rows([a[n].shape for n in BIG], 16, PACK_TILE)
    slabs = _pack([_split_shards(gw[n], ax) for n, ax in BIG.items()], big_rows, BF16, 16, lead=1)
    by_core = slabs.reshape(N_CHIP, 2, big_rows, PACK_W)
    my_core = lax.axis_index("c")
    keep = lax.dynamic_index_in_dim(by_core, my_core, axis=1, keepdims=False)
    give = lax.dynamic_index_in_dim(by_core, 1 - my_core, axis=1, keepdims=False)
    pair_sum = pair_add(keep, pair_swap(give, "swap_grads"), "add_pair_grads")
    landed = chip_all_to_all(pair_sum, "scatter_grads")

    def update(names, g_slabs, rows, align, label):
        shapes = [a[n].shape for n in names]
        packs = [_pack([a[pre + n] for n in names], rows, F32, align) for pre in ("", "m_", "v_")]
        outs = adamw(g_slabs, *packs, label)
        return [dict(zip(names, _unpack(o, shapes, align))) for o in outs]

    res_big = update(list(BIG), landed, big_rows, 16, "adamw_big")
    ada_rows = _pack_rows([a["w_ada"].shape], 8, PACK_TILE)
    res_ada = update(["w_ada"], _pack([g_w_ada], ada_rows, F32, 8)[None], ada_rows, 8, "adamw_ada")
    small_rows = _pack_rows([a[n].shape for n in SMALL], 8, PACK_TILE)
    res_small = update(SMALL, _pack([g_small[n] for n in SMALL], small_rows, F32, 8)[None], small_rows, 8,
                       "adamw_small")

    outs = [loss, gx[None]]
    for k in range(4):
        merged = {**res_big[k], **res_ada[k], **res_small[k]}
        outs += [merged[n] for n in WEIGHTS]
    return tuple(outs)
```

```python
import functools
import math

import jax
import jax.numpy as jnp
import numpy as np
from jax import lax
from jax.experimental import pallas as pl
from jax.experimental.pallas import tpu as pltpu

F32 = jnp.float32
BF16 = jnp.bfloat16

N_DEV = 8
D = 1024
DEPTH = 2
GRID_W = 64
MLA_HEADS = 8
MLA_NOPE = 64
MLA_ROPE = 32
MLA_QK = 96
MLA_V = 64
MLA_Q_LORA = 256
MLA_KV_LORA = 128
GLA_HEADS = 4
GLA_DK = 128
GLA_RANK = 16
GLA_NORMALIZER = 16.0
RET_HEADS = 4
RET_DK = 128
BRANCH_W = 512
D_FF = 2816
CHUNK = 64
ROPE_THETA = 10000.0
RET_THETA = 10000.0
EPS = 1e-6
HEAD_PAD = 128
N_IN_PAD = 8192

ADAM_LR = 0.001
ADAM_B1 = 0.9
ADAM_B2 = 0.999
ADAM_EPS = 1e-08
ADAM_WD = 0.01
ADAM_STEP = 10

ROW_TILE = 256
SCAN_CHUNKS = ROW_TILE // CHUNK
VMEM_LIMIT_BYTES = 56 * 1024 * 1024
MESH = pl.DeviceIdType.MESH


def _cparams(n_axes):
    return pltpu.CompilerParams(dimension_semantics=("arbitrary",) * n_axes, vmem_limit_bytes=VMEM_LIMIT_BYTES)


def _pick(dim, cands):
    for cand in cands:
        if dim % cand == 0:
            return cand
    return dim


_DOT_DIMS = {"nn": (((1,), (0,)), ((), ())), "nt": (((1,), (1,)), ((), ())), "tn": (((0,), (0,)), ((), ()))}


def _dg(a, b, mode):
    return lax.dot_general(a.astype(BF16), b.astype(BF16), _DOT_DIMS[mode], preferred_element_type=F32)


def _bdot(a, b, mode):
    @jax.custom_vjp
    def f(a, b):
        return _dg(a, b, mode)

    def fwd(a, b):
        return _dg(a, b, mode), (a, b)

    def bwd(res, g):
        a, b = res
        if mode == "nn":
            return _dg(g, b, "nt").astype(a.dtype), _dg(a, g, "tn").astype(b.dtype)
        if mode == "nt":
            return _dg(g, b, "nn").astype(a.dtype), _dg(g, a, "tn").astype(b.dtype)
        return _dg(b, g, "nt").astype(a.dtype), _dg(a, g, "nn").astype(b.dtype)

    f.defvjp(fwd, bwd)
    return f(a, b)


def _roll(x, shift, axis):
    n = x.shape[axis]
    shift = shift % n

    @jax.custom_vjp
    def f(x):
        return pltpu.roll(x, shift, axis)

    def fwd(x):
        return pltpu.roll(x, shift, axis), None

    def bwd(_, g):
        return (pltpu.roll(g, (n - shift) % n, axis),)

    f.defvjp(fwd, bwd)
    return f(x)


@jax.custom_jvp
def _log_sigmoid(x):
    return jnp.minimum(x, 0.0) - jnp.log(1.0 + jnp.exp(-jnp.abs(x)))


@_log_sigmoid.defjvp
def _log_sigmoid_jvp(primals, tangents):
    (x,), (t,) = primals, tangents
    return _log_sigmoid(x), t * jax.nn.sigmoid(-x)


def _rms(x, n, w=None):
    y = x * lax.rsqrt(jnp.sum(x * x, axis=-1, keepdims=True) * (1.0 / n) + EPS)
    return y if w is None else y * w


def _silu(x):
    return x * jax.nn.sigmoid(x)


def _gelu_tanh(x):
    return 0.5 * x * (1.0 + jnp.tanh(math.sqrt(2.0 / math.pi) * (x + 0.044715 * (x * x * x))))


def _mm(a, b, mode, name):
    if mode == "nn":
        (m, k), (_, n) = a.shape, b.shape
    elif mode == "nt":
        (m, k), (n, _) = a.shape, b.shape
    else:
        (k, m), (_, n) = a.shape, b.shape
    tm = _pick(m, (1024, 768, 1408, 512, 256, 128))
    tn = _pick(n, (1024, 1408, 512, 256, 128))
    tk = _pick(k, (1024, 768, 1408, 512, 256, 128))
    nk = k // tk
    if mode == "nn":
        a_spec = pl.BlockSpec((tm, tk), lambda i, j, kk: (i, kk))
        b_spec = pl.BlockSpec((tk, tn), lambda i, j, kk: (kk, j))
    elif mode == "nt":
        a_spec = pl.BlockSpec((tm, tk), lambda i, j, kk: (i, kk))
        b_spec = pl.BlockSpec((tn, tk), lambda i, j, kk: (j, kk))
    else:
        a_spec = pl.BlockSpec((tk, tm), lambda i, j, kk: (kk, i))
        b_spec = pl.BlockSpec((tk, tn), lambda i, j, kk: (kk, j))

    def body(a_ref, b_ref, o_ref):
        kk = pl.program_id(2)
        part = _dg(a_ref[...], b_ref[...], mode)
        if nk == 1:
            o_ref[...] = part
        else:
            @pl.when(kk == 0)
            def _():
                o_ref[...] = part

            @pl.when(kk != 0)
            def _():
                o_ref[...] += part

    return pl.pallas_call(
        body, name=name, grid=(m // tm, n // tn, nk),
        in_specs=[a_spec, b_spec], out_specs=pl.BlockSpec((tm, tn), lambda i, j, kk: (i, j)),
        out_shape=jax.ShapeDtypeStruct((m, n), F32),
        compiler_params=_cparams(3),
    )(a, b)


def linear(x, w, name):
    @jax.custom_vjp
    def op(x, w):
        return _mm(x, w.astype(BF16), "nn", name + "_f")

    def fwd(x, w):
        wb = w.astype(BF16)
        return _mm(x, wb, "nn", name + "_f"), (x, wb)

    def bwd(res, g):
        x, wb = res
        return _mm(g, wb, "nt", name + "_dx"), _mm(x, g, "tn", name + "_dw")

    op.defvjp(fwd, bwd)
    return op(x, w)


PIECE_W = 512
PIECE_ROWS = 384


def _mm_split(a, wb, name):
    (r, k), n = a.shape, wb.shape[1] // PIECE_W
    tm = _pick(r, (PIECE_ROWS, ROW_TILE))
    width = PIECE_GROUP * PIECE_W

    n_tiles = r // tm

    def body(a_ref, w_ref, *out_refs):
        j = pl.program_id(0)
        res = _dg(a_ref[...], w_ref[...], "nn")
        for group in range(n // PIECE_GROUP):
            @pl.when(j == group)
            def _(group=group):
                for p in range(PIECE_GROUP):
                    out_refs[group * PIECE_GROUP + p][...] = res[:, p * PIECE_W:(p + 1) * PIECE_W]

    def out_spec(jj):
        group = jj // PIECE_GROUP
        return pl.BlockSpec((tm, PIECE_W),
                            lambda j, i: (jnp.where(j == group, i, jnp.where(j < group, 0, n_tiles - 1)), 0))

    return pl.pallas_call(
        body, name=name, grid=(n // PIECE_GROUP, n_tiles),
        in_specs=[pl.BlockSpec((tm, k), lambda j, i: (i, 0)), pl.BlockSpec((k, width), lambda j, i: (0, j))],
        out_specs=[out_spec(jj) for jj in range(n)],
        out_shape=[jax.ShapeDtypeStruct((r, PIECE_W), F32)] * n, compiler_params=_cparams(2),
    )(a, wb)


def _mm_join(gs, wb, name):
    n, (r, _), k = len(gs), gs[0].shape, wb.shape[0]
    tm = _pick(r, (PIECE_ROWS, ROW_TILE))
    n_groups = n // PIECE_GROUP

    def body(*refs):
        g_refs, w_ref, o_ref = refs[:n], refs[n], refs[n + 1]
        j = pl.program_id(1)
        for group in range(n_groups):
            @pl.when(j == group)
            def _(group=group):
                g = jnp.concatenate([g_refs[group * PIECE_GROUP + p][...].astype(BF16) for p in range(PIECE_GROUP)],
                                    axis=1)
                part = _dg(g, w_ref[...], "nt")
                if group == 0:
                    o_ref[...] = part
                else:
                    o_ref[...] += part

    return pl.pallas_call(
        body, name=name, grid=(r // tm, n_groups),
        in_specs=[pl.BlockSpec((tm, PIECE_W), lambda i, j: (i, 0))] * n
        + [pl.BlockSpec((k, PIECE_GROUP * PIECE_W), lambda i, j: (0, j))],
        out_specs=pl.BlockSpec((tm, k), lambda i, j: (i, 0)), out_shape=jax.ShapeDtypeStruct((r, k), F32),
        compiler_params=_cparams(2),
    )(*gs, wb)


PIECE_GROUP = 4


def _mm_join_tn(a, gs, name):
    n, (r, k) = len(gs), a.shape
    tk = _pick(r, (768, 512, ROW_TILE))
    width = PIECE_GROUP * PIECE_W

    def group_call(group):
        def body(a_ref, *refs):
            g_refs, o_ref = refs[:PIECE_GROUP], refs[PIECE_GROUP]
            part = _dg(a_ref[...], jnp.concatenate([g_ref[...].astype(BF16) for g_ref in g_refs], axis=1), "tn")

            @pl.when(pl.program_id(0) == 0)
            def _():
                o_ref[...] = part

            @pl.when(pl.program_id(0) != 0)
            def _():
                o_ref[...] += part

        return pl.pallas_call(
            body, name=f"{name}{group}", grid=(r // tk,),
            in_specs=[pl.BlockSpec((tk, k), lambda kk: (kk, 0))]
            + [pl.BlockSpec((tk, PIECE_W), lambda kk: (kk, 0))] * PIECE_GROUP,
            out_specs=pl.BlockSpec((k, width), lambda kk: (0, 0)),
            out_shape=jax.ShapeDtypeStruct((k, width), F32), compiler_params=_cparams(1),
        )(a, *gs[group * PIECE_GROUP:(group + 1) * PIECE_GROUP])

    return jnp.concatenate([group_call(group) for group in range(n // PIECE_GROUP)], axis=1)


def rowwise(name, fn, rows, segs, params, out_widths, nct, diff_rows=None, then=None):
    n_row, n_seg, n_par, n_out = len(rows), len(segs), len(params), len(out_widths)
    out_dtype = F32 if then is None else BF16
    diff_rows = [True] * n_row if diff_rows is None else list(diff_rows)
    r_total = rows[0].shape[0]
    n_tiles = r_total // ROW_TILE

    def seg_of(i):
        return jnp.where(i < nct, 0, 1)

    def row_spec(width):
        return pl.BlockSpec((ROW_TILE, width), lambda i: (i, 0))

    def seg_spec(shape):
        nd = len(shape)
        return pl.BlockSpec((1,) + tuple(shape[1:]), lambda i: (seg_of(i),) + (0,) * (nd - 1))

    def par_spec(shape):
        nd = len(shape)
        return pl.BlockSpec(tuple(shape), lambda i: (0,) * nd)

    in_specs = ([row_spec(r.shape[1]) for r in rows] + [seg_spec(s.shape) for s in segs]
                + [par_spec(p.shape) for p in params])

    def load(refs):
        vals = [r[...].astype(F32) for r in refs[:n_row]]
        vals += [r[0].astype(F32) for r in refs[n_row:n_row + n_seg]]
        vals += [r[...].astype(F32) for r in refs[n_row + n_seg:n_row + n_seg + n_par]]
        return vals

    def fwd_call(arrs):
        def body(*refs):
            outs = fn(*load(refs))
            for o_ref, val in zip(refs[n_row + n_seg + n_par:], outs):
                o_ref[...] = val.astype(o_ref.dtype)

        return pl.pallas_call(
            body, name=name + "_f", grid=(n_tiles,), in_specs=in_specs,
            out_specs=[row_spec(w) for w in out_widths],
            out_shape=[jax.ShapeDtypeStruct((r_total, w), out_dtype) for w in out_widths],
            compiler_params=_cparams(1),
        )(*arrs)

    d_idx = [k for k in range(n_row) if diff_rows[k]]

    def bwd_call(arrs, douts):
        n_in = n_row + n_seg + n_par

        def body(*refs):
            i = pl.program_id(0)
            vals = load(refs[:n_in])
            gs = [r[...] for r in refs[n_in:n_in + n_out]]
            out_refs = refs[n_in + n_out:]
            diff_pos = d_idx + list(range(n_row, n_in))

            def f(*dv):
                full = list(vals)
                for pos, v in zip(diff_pos, dv):
                    full[pos] = v
                return tuple(fn(*full))

            _, vjp = jax.vjp(f, *[vals[p] for p in diff_pos])
            grads = vjp(tuple(gs))
            nd = len(d_idx)
            for o_ref, g in zip(out_refs[:nd], grads[:nd]):
                o_ref[...] = g
            first_seg = jnp.logical_or(i == 0, i == nct)
            for o_ref, g in zip(out_refs[nd:nd + n_seg], grads[nd:nd + n_seg]):
                @pl.when(first_seg)
                def _(o_ref=o_ref, g=g):
                    o_ref[0] = g

                @pl.when(jnp.logical_not(first_seg))
                def _(o_ref=o_ref, g=g):
                    o_ref[0] += g
            for o_ref, g in zip(out_refs[nd + n_seg:], grads[nd + n_seg:]):
                @pl.when(i == 0)
                def _(o_ref=o_ref, g=g):
                    o_ref[...] = g

                @pl.when(i != 0)
                def _(o_ref=o_ref, g=g):
                    o_ref[...] += g

        out_specs = ([row_spec(rows[k].shape[1]) for k in d_idx] + [seg_spec(s.shape) for s in segs]
                     + [par_spec(p.shape) for p in params])
        out_shape = ([jax.ShapeDtypeStruct(rows[k].shape, F32) for k in d_idx]
                     + [jax.ShapeDtypeStruct(s.shape, F32) for s in segs]
                     + [jax.ShapeDtypeStruct(p.shape, F32) for p in params])
        return pl.pallas_call(
            body, name=name + "_b", grid=(n_tiles,),
            in_specs=in_specs + [row_spec(w) for w in out_widths],
            out_specs=out_specs, out_shape=out_shape, compiler_params=_cparams(1),
        )(*arrs, *douts)

    @jax.custom_vjp
    def op(*arrs):
        return tuple(fwd_call(arrs))

    def op_fwd(*arrs):
        return tuple(fwd_call(arrs)), arrs

    def op_bwd(arrs, douts):
        grads = list(bwd_call(arrs, douts))
        nd = len(d_idx)
        row_grads = [jnp.zeros_like(arrs[k]) for k in range(n_row)]
        for k, g in zip(d_idx, grads[:nd]):
            row_grads[k] = g
        return tuple(row_grads + grads[nd:])

    if then is None:
        op.defvjp(op_fwd, op_bwd)
        return op(*rows, *segs, *params)

    kind, w, mm_name = then

    if kind == "attention":
        @jax.custom_vjp
        def attended(*arrs):
            return attention_forward(*fwd_call(arrs), w, mm_name)[0]

        def attended_fwd(*arrs):
            qkv = fwd_call(arrs)
            o, parts = attention_forward(*qkv, w, mm_name)
            return o, (arrs, qkv, parts)

        def attended_bwd(res, do):
            arrs, qkv, parts = res
            return op_bwd(arrs, list(attention_backward(*qkv, parts, do, w, mm_name)))

        attended.defvjp(attended_fwd, attended_bwd)
        return attended(*rows, *segs, *params)

    def project(u, wb):
        if kind == "linear":
            return _mm(u, wb, "nn", mm_name + "_f")
        return tuple(_mm_split(u, wb, mm_name + "_f"))

    @jax.custom_vjp
    def fused(w, *arrs):
        return project(fwd_call(arrs)[0], w.astype(BF16))

    def fused_fwd(w, *arrs):
        u, wb = fwd_call(arrs)[0], w.astype(BF16)
        return project(u, wb), (arrs, u, wb)

    def fused_bwd(res, g):
        arrs, u, wb = res
        if kind == "linear":
            du, dw = _mm(g, wb, "nt", mm_name + "_dx"), _mm(u, g, "tn", mm_name + "_dw")
        else:
            du, dw = _mm_join(list(g), wb, mm_name + "_dx"), _mm_join_tn(u, list(g), mm_name + "_dw")
        return (dw,) + op_bwd(arrs, [du])

    fused.defvjp(fused_fwd, fused_bwd)
    return fused(w, *rows, *segs, *params)


ATT_SCALE = MLA_QK ** -0.5
LOG2E = math.log2(math.e)
ATT_KEY_CHUNKS = (768, 512, 256)


ATT_LATENT_TILE = 1024


def _query_rows_spec(row0, tq):
    return pl.BlockSpec((pl.Element(tq), pl.Element(HEAD_PAD)),
                        lambda h, i: (pl.multiple_of(row0 + i * tq, ROW_TILE), pl.multiple_of(h * HEAD_PAD, HEAD_PAD)))


def _key_chunks(nk):
    kc = _pick(nk, ATT_KEY_CHUNKS)
    return [(c * kc, kc) for c in range(nk // kc)]


def _attn_fwd_call(q, k, v, row0, n_rows, tq, nk, name):
    def body(q_ref, k_ref, v_ref, o_ref, lse_ref):
        qv = q_ref[...]
        m = jnp.full((tq, 1), -jnp.inf, F32)
        l = jnp.zeros((tq, 1), F32)
        acc = jnp.zeros((tq, HEAD_PAD), F32)
        for start, size in _key_chunks(nk):
            s = lax.dot_general(qv, k_ref[start:start + size, :], _DOT_DIMS["nt"], preferred_element_type=F32)
            m_new = jnp.maximum(m, jnp.max(s, axis=-1, keepdims=True))
            alpha = jnp.exp2(m - m_new)
            p = jnp.exp2(s - m_new)
            l = alpha * l + jnp.sum(p, axis=-1, keepdims=True)
            acc = alpha * acc + lax.dot_general(p.astype(BF16), v_ref[start:start + size, :], _DOT_DIMS["nn"],
                                                preferred_element_type=F32)
            m = m_new
        o_ref[...] = acc / l
        lse_ref[...] = jnp.broadcast_to(m + jnp.log2(l), (tq, HEAD_PAD))

    out_spec = pl.BlockSpec((tq, HEAD_PAD), lambda h, i: (i, h))
    kv_spec = pl.BlockSpec((nk, HEAD_PAD), lambda h, i: (0, h))
    out = jax.ShapeDtypeStruct((n_rows, q.shape[1]), F32)
    return pl.pallas_call(
        body, name=name, grid=(MLA_HEADS, n_rows // tq), in_specs=[_query_rows_spec(row0, tq), kv_spec, kv_spec],
        out_specs=[out_spec, out_spec], out_shape=[out, out], compiler_params=_cparams(2),
    )(q, k, v)


def _attn_bwd_call(q, k, v, o, lse, do, row0, n_rows, tq, nk, name):
    nq = n_rows // tq

    def body(q_ref, k_ref, v_ref, o_ref, lse_ref, do_ref, dq_ref, dk_ref, dv_ref):
        i = pl.program_id(1)

        @pl.when(i == 0)
        def _():
            dk_ref[...] = jnp.zeros_like(dk_ref)
            dv_ref[...] = jnp.zeros_like(dv_ref)

        qv = q_ref[...]
        dov = do_ref[...]
        dob = dov.astype(BF16)
        lse = lse_ref[:, 0:1]
        delta = jnp.sum(dov * o_ref[...], axis=-1, keepdims=True)
        dq = jnp.zeros((tq, HEAD_PAD), F32)
        for start, size in _key_chunks(nk):
            kk = k_ref[start:start + size, :]
            vv = v_ref[start:start + size, :]
            s = lax.dot_general(qv, kk, _DOT_DIMS["nt"], preferred_element_type=F32)
            p = jnp.exp2(s - lse)
            dp = lax.dot_general(dob, vv, _DOT_DIMS["nt"], preferred_element_type=F32)
            g = (p * (dp - delta)).astype(BF16)
            dk_ref[start:start + size, :] += lax.dot_general(g, qv, _DOT_DIMS["tn"], preferred_element_type=F32)
            dv_ref[start:start + size, :] += lax.dot_general(p.astype(BF16), dob, _DOT_DIMS["tn"],
                                                             preferred_element_type=F32)
            dq = dq + lax.dot_general(g, kk, _DOT_DIMS["nn"], preferred_element_type=F32)
        dq_ref[...] = dq * (1.0 / LOG2E)

        @pl.when(i == nq - 1)
        def _():
            dk_ref[...] = dk_ref[...] * (1.0 / LOG2E)

    own_spec = pl.BlockSpec((tq, HEAD_PAD), lambda h, i: (i, h))
    kv_spec = pl.BlockSpec((nk, HEAD_PAD), lambda h, i: (0, h))
    rows_spec = _query_rows_spec(row0, tq)
    return pl.pallas_call(
        body, name=name, grid=(MLA_HEADS, nq),
        in_specs=[rows_spec, kv_spec, kv_spec, own_spec, own_spec, rows_spec],
        out_specs=[own_spec, kv_spec, kv_spec],
        out_shape=[jax.ShapeDtypeStruct((n_rows, q.shape[1]), F32), jax.ShapeDtypeStruct((nk, q.shape[1]), F32),
                   jax.ShapeDtypeStruct((nk, q.shape[1]), F32)],
        compiler_params=_cparams(2),
    )(q, k, v, o, lse, do)


def _attn_ranges(r_total, tc):
    tq_lat = _pick(r_total - tc, (ATT_LATENT_TILE, ROW_TILE))
    return [(0, tc, ROW_TILE, tc, "_ctx"), (tc, r_total - tc, tq_lat, r_total, "_lat")]


def attention_forward(qs, k, v, tc, name):
    parts = [_attn_fwd_call(qs, k, v, row0, n_rows, tq, nk, name + tag + "_f")
             for row0, n_rows, tq, nk, tag in _attn_ranges(qs.shape[0], tc)]
    return jnp.concatenate([o for o, _ in parts], axis=0), parts


def attention_backward(qs, k, v, parts, do, tc, name):
    r_total = qs.shape[0]
    (dq_c, dk_c, dv_c), (dq_l, dk_l, dv_l) = [
        _attn_bwd_call(qs, k, v, o, lse, do, row0, n_rows, tq, nk, name + tag + "_b")
        for (o, lse), (row0, n_rows, tq, nk, tag) in zip(parts, _attn_ranges(r_total, tc))]
    grow = lambda part: jnp.pad(part, ((0, r_total - tc), (0, 0)))
    return jnp.concatenate([dq_c, dq_l], axis=0), dk_l + grow(dk_c), dv_l + grow(dv_c)


CHUNK_SHIFT = CHUNK.bit_length() - 1


def _block_pairs():
    rows = lax.broadcasted_iota(jnp.int32, (ROW_TILE, ROW_TILE), 0)
    cols = lax.broadcasted_iota(jnp.int32, (ROW_TILE, ROW_TILE), 1)
    same = lax.shift_right_logical(rows, CHUNK_SHIFT) == lax.shift_right_logical(cols, CHUNK_SHIFT)
    return rows, cols, same


def _block_mask(kind):
    rows, cols, same = _block_pairs()
    order = {"lower_incl": rows >= cols, "upper_incl": rows <= cols, "lower_strict": rows > cols,
             "upper_strict": rows < cols}[kind]
    return jnp.logical_and(same, order)


def _row_chunk():
    return lax.shift_right_logical(lax.broadcasted_iota(jnp.int32, (ROW_TILE, 1), 0), CHUNK_SHIFT)


def _dot01(kind, x):
    m = _block_mask(kind).astype(BF16)
    hi = x.astype(BF16)
    rest = x - hi.astype(F32)
    mid = rest.astype(BF16)
    lo = (rest - mid.astype(F32)).astype(BF16)
    terms = jnp.concatenate([hi, mid, lo], axis=1)
    out = lax.dot_general(m, terms, _DOT_DIMS["nn"], preferred_element_type=F32)
    n = x.shape[1]
    return out[:, :n] + out[:, n:2 * n] + out[:, 2 * n:]


def _chunk_sums(x, forward):
    kinds = ("lower_incl", "upper_strict") if forward else ("upper_incl", "lower_strict")
    transposed = ("upper_incl", "lower_strict") if forward else ("lower_incl", "upper_strict")

    @jax.custom_vjp
    def f(x):
        return _dot01(kinds[0], x), _dot01(kinds[1], x)

    def fwd(x):
        return (_dot01(kinds[0], x), _dot01(kinds[1], x)), None

    def bwd(_, g):
        return (_dot01(transposed[0], g[0]) + _dot01(transposed[1], g[1]),)

    f.defvjp(fwd, bwd)
    return f(x)


def _scan_order(forward):
    if forward:
        return list(range(SCAN_CHUNKS)), lambda c: c * CHUNK + CHUNK - 1
    return list(range(SCAN_CHUNKS - 1, -1, -1)), lambda c: c * CHUNK


def _carry_states(forward, st0, inc_all, decay_of):
    order, _ = _scan_order(forward)
    entering = [None] * SCAN_CHUNKS
    st = st0
    for c in order:
        entering[c] = st
        st = st * decay_of(c) + inc_all[:, c * HEAD_PAD:(c + 1) * HEAD_PAD]
    return jnp.concatenate(entering, axis=0), st


def _per_chunk_lanes(x):
    chunk = _row_chunk()
    return jnp.concatenate([jnp.where(chunk == c, x, 0.0) for c in range(SCAN_CHUNKS)], axis=1)


def _own_chunk_lanes(x4):
    chunk = _row_chunk()
    n = x4.shape[1] // SCAN_CHUNKS
    out = jnp.where(chunk == 0, x4[:, :n], 0.0)
    for c in range(1, SCAN_CHUNKS):
        out = out + jnp.where(chunk == c, x4[:, c * n:(c + 1) * n], 0.0)
    return out


def _gla_block(forward, q, k, v, la, st0):
    cum, after = _chunk_sums(la, forward)
    _, last_row = _scan_order(forward)
    q_dec = q * (jnp.exp(cum) * (GLA_DK ** -0.5))
    att = _bdot(q_dec, k * jnp.exp(-cum), "nt")
    att = jnp.where(_block_mask("lower_incl" if forward else "upper_strict"), att, 0.0)
    inc_all = _bdot(v, _per_chunk_lanes(k * jnp.exp(after)), "tn")
    entering, st1 = _carry_states(forward, st0, inc_all,
                                  lambda c: jnp.exp(cum[last_row(c):last_row(c) + 1, :]))
    o = _bdot(att, v, "nn") + _own_chunk_lanes(_bdot(q_dec, entering, "nt"))
    return o, st1


def _ret_block(forward, q, k, v, rd, st0):
    lg = -jnp.exp(rd[0:1, 0:1])
    rows, cols, _ = _block_pairs()
    pos = jnp.bitwise_and(lax.broadcasted_iota(jnp.int32, (ROW_TILE, 1), 0), CHUNK - 1).astype(F32)
    if forward:
        to_end, from_start, rel = CHUNK - 1.0 - pos, pos + 1.0, (rows - cols).astype(F32)
    else:
        to_end, from_start, rel = pos, CHUNK - pos, (cols - rows).astype(F32)
    mask = _block_mask("lower_incl" if forward else "upper_strict")
    dmat = jnp.where(mask, jnp.exp(jnp.where(mask, rel, 0.0) * lg), 0.0)
    att = _bdot(q, k, "nt") * dmat
    inc_all = _bdot(v, _per_chunk_lanes(k * jnp.exp(to_end * lg)), "tn")
    entering, st1 = _carry_states(forward, st0, inc_all, lambda c: jnp.exp(CHUNK * lg))
    o = _bdot(att, v, "nn") + _own_chunk_lanes(_bdot(q, entering, "nt")) * jnp.exp(from_start * lg)
    return o, st1


def scan(kind, forward, q, k, v, aux, tc, name, rot=None):
    heads = q.shape[1] // HEAD_PAD
    r_total = q.shape[0]
    nblk = r_total // ROW_TILE
    nctb = tc // ROW_TILE
    block_fn = functools.partial(_gla_block if kind == "gla" else _ret_block, forward)
    per_row_aux = kind == "gla"

    def blk(g):
        if forward:
            return g
        return jnp.where(g < nctb, nctb - 1 - g, nblk - 1 - (g - nctb))

    def specs(step_to_g):
        row = pl.BlockSpec((ROW_TILE, heads * HEAD_PAD), lambda s: (blk(step_to_g(s)), 0))
        aux_spec = row if per_row_aux else pl.BlockSpec((heads, 8, HEAD_PAD), lambda s: (0, 0, 0))
        st = pl.BlockSpec((1, heads, HEAD_PAD, HEAD_PAD), lambda s: (step_to_g(s), 0, 0, 0))
        return row, aux_spec, st

    def head_cols(h):
        return slice(h * HEAD_PAD, (h + 1) * HEAD_PAD)

    n_rot = 0 if rot is None else 2
    rot_arrays = [] if rot is None else list(rot)

    def rot_specs(step_to_g):
        return [pl.BlockSpec((ROW_TILE, HEAD_PAD), lambda s: (blk(step_to_g(s)), 0))] * n_rot

    def head_fn(rot_refs):
        if not rot_refs:
            return block_fn
        rc, rs = rot_refs[0][...], rot_refs[1][...]

        def turned(x, scale):
            return (x * rc + _roll(x, HEAD_PAD // 2, 1) * rs) * scale

        return lambda q, k, v, a, st0: block_fn(turned(q, 1.0), turned(k, RET_DK ** -0.5), v, a, st0)

    def fwd_call(q, k, v, aux):
        row, aux_spec, st_spec = specs(lambda s: s)

        def body(*refs):
            q_ref, k_ref, v_ref, a_ref = refs[:4]
            o_ref, st0_ref, st_ref = refs[4 + n_rot:]
            fn = head_fn(refs[4:4 + n_rot])

            @pl.when(pl.program_id(0) == 0)
            def _():
                st_ref[...] = jnp.zeros_like(st_ref)

            qv, kv, vv = q_ref[...], k_ref[...], v_ref[...]
            outs = []
            for h in range(heads):
                st0 = st_ref[h]
                st0_ref[0, h] = st0
                a = a_ref[:, head_cols(h)] if per_row_aux else a_ref[h]
                o, st1 = fn(qv[:, head_cols(h)], kv[:, head_cols(h)], vv[:, head_cols(h)], a, st0)
                outs.append(o)
                st_ref[h] = st1
            o_ref[...] = jnp.concatenate(outs, axis=1)

        return pl.pallas_call(
            body, name=name + "_f", grid=(nblk,), in_specs=[row, row, row, aux_spec] + rot_specs(lambda s: s),
            out_specs=[row, st_spec],
            out_shape=[jax.ShapeDtypeStruct(q.shape, F32),
                       jax.ShapeDtypeStruct((nblk, heads, HEAD_PAD, HEAD_PAD), F32)],
            scratch_shapes=[pltpu.VMEM((heads, HEAD_PAD, HEAD_PAD), F32)],
            compiler_params=_cparams(1),
        )(q, k, v, aux, *rot_arrays)

    def bwd_call(q, k, v, aux, st0s, do):
        row, aux_spec, st_spec = specs(lambda s: nblk - 1 - s)

        def body(*refs):
            q_ref, k_ref, v_ref, a_ref = refs[:4]
            st0_ref, do_ref, dq_ref, dk_ref, dv_ref, da_ref, dst_ref = refs[4 + n_rot:]
            fn = head_fn(refs[4:4 + n_rot])
            s = pl.program_id(0)

            @pl.when(s == 0)
            def _():
                dst_ref[...] = jnp.zeros_like(dst_ref)

            qv, kv, vv, dov = q_ref[...], k_ref[...], v_ref[...], do_ref[...]
            grads = []
            for h in range(heads):
                a = a_ref[:, head_cols(h)] if per_row_aux else a_ref[h]
                _, vjp = jax.vjp(fn, qv[:, head_cols(h)], kv[:, head_cols(h)], vv[:, head_cols(h)], a,
                                 st0_ref[0, h])
                dq, dk, dv, da, dst0 = vjp((dov[:, head_cols(h)], dst_ref[h]))
                dst_ref[h] = dst0
                grads.append((dq, dk, dv, da))
            dq_ref[...] = jnp.concatenate([g[0] for g in grads], axis=1)
            dk_ref[...] = jnp.concatenate([g[1] for g in grads], axis=1)
            dv_ref[...] = jnp.concatenate([g[2] for g in grads], axis=1)
            if per_row_aux:
                da_ref[...] = jnp.concatenate([g[3] for g in grads], axis=1)
            else:
                da = jnp.stack([g[3] for g in grads], axis=0)

                @pl.when(s == 0)
                def _():
                    da_ref[...] = da

                @pl.when(s != 0)
                def _():
                    da_ref[...] += da

        return pl.pallas_call(
            body, name=name + "_b", grid=(nblk,),
            in_specs=[row, row, row, aux_spec] + rot_specs(lambda s: nblk - 1 - s) + [st_spec, row],
            out_specs=[row, row, row, aux_spec],
            out_shape=[jax.ShapeDtypeStruct(q.shape, F32)] * 3 + [jax.ShapeDtypeStruct(aux.shape, F32)],
            scratch_shapes=[pltpu.VMEM((heads, HEAD_PAD, HEAD_PAD), F32)],
            compiler_params=_cparams(1),
        )(q, k, v, aux, *rot_arrays, st0s, do)

    @jax.custom_vjp
    def op(q, k, v, aux):
        return fwd_call(q, k, v, aux)[0]

    def fwd(q, k, v, aux):
        o, st0s = fwd_call(q, k, v, aux)
        return o, (q, k, v, aux, st0s)

    def bwd(res, do):
        return tuple(bwd_call(*res, do))

    op.defvjp(fwd, bwd)
    return op(q, k, v, aux)


HALO = 8


def _neighbours(main, prev8, next8, i, nct, n_tiles):
    has_prev = jnp.logical_and(i != 0, i != nct).astype(F32)
    has_next = jnp.logical_and(i != nct - 1, i != n_tiles - 1).astype(F32)
    row = lax.broadcasted_iota(jnp.int32, main.shape, 0)
    down = jnp.where(row == 0, prev8[HALO - 1:HALO] * has_prev, pltpu.roll(main, 1, 0))
    up = jnp.where(row == ROW_TILE - 1, next8[0:1] * has_next, pltpu.roll(main, ROW_TILE - 1, 0))
    return down, up


GELU_K = math.sqrt(2.0 / math.pi)
GELU_A = 0.044715


def _gelu_tanh_grad(x):
    t = jnp.tanh(GELU_K * (x + GELU_A * (x * x * x)))
    return 0.5 * x * (1.0 + t), 0.5 * (1.0 + t) + 0.5 * x * (1.0 - t * t) * (GELU_K * (1.0 + 3.0 * GELU_A * (x * x)))


def conv_ffn_out(gate, up, w8, b, w_out, tc, name):
    r_total, width = gate.shape
    n_tiles = r_total // ROW_TILE
    nct = tc // ROW_TILE
    per = ROW_TILE // HALO
    main_spec = pl.BlockSpec((ROW_TILE, width), lambda i: (i, 0))
    prev_spec = pl.BlockSpec((HALO, width), lambda i: (jnp.maximum(i * per - 1, 0), 0))
    next_spec = pl.BlockSpec((HALO, width), lambda i: (jnp.minimum((i + 1) * per, r_total // HALO - 1), 0))
    w_spec = pl.BlockSpec((8, width), lambda i: (0, 0))
    b_spec = pl.BlockSpec((1, width), lambda i: (0, 0))
    halo3 = [main_spec, prev_spec, next_spec]

    def conv(w_ref, b_ref, down, mid, upn):
        return w_ref[0:1] * down + w_ref[1:2] * mid + w_ref[2:3] * upn + b_ref[...]

    def fwd_call(gate, up, w8, b):
        def body(g_ref, gp_ref, gn_ref, up_ref, w_ref, b_ref, o_ref):
            gv = g_ref[...]
            down, upn = _neighbours(gv, gp_ref[...], gn_ref[...], pl.program_id(0), nct, n_tiles)
            o_ref[...] = (_gelu_tanh(conv(w_ref, b_ref, down, gv, upn)) * up_ref[...]).astype(BF16)

        return pl.pallas_call(
            body, name=name + "_f", grid=(n_tiles,), in_specs=halo3 + [main_spec, w_spec, b_spec],
            out_specs=main_spec, out_shape=jax.ShapeDtypeStruct(gate.shape, BF16), compiler_params=_cparams(1),
        )(gate, gate, gate, up, w8, b)

    def bwd_call(gate, up, w8, b, du):
        def body(g_ref, gp_ref, gn_ref, up_ref, upp_ref, upn_ref, du_ref, dup_ref, dun_ref, w_ref, b_ref,
                 dg_ref, dupo_ref, dw_ref, db_ref):
            i = pl.program_id(0)
            has_prev = jnp.logical_and(i != 0, i != nct).astype(F32)
            has_next = jnp.logical_and(i != nct - 1, i != n_tiles - 1).astype(F32)
            gv, gp, gn = g_ref[...], gp_ref[...], gn_ref[...]
            g_down, g_up = _neighbours(gv, gp, gn, i, nct, n_tiles)
            act, slope = _gelu_tanh_grad(conv(w_ref, b_ref, g_down, gv, g_up))
            duv = du_ref[...]
            dupo_ref[...] = duv * act
            dc = duv * up_ref[...] * slope
            c_above = conv(w_ref, b_ref, gp[HALO - 2:HALO - 1], gp[HALO - 1:HALO], gv[0:1])
            c_below = conv(w_ref, b_ref, gv[ROW_TILE - 1:ROW_TILE], gn[0:1], gn[1:2])
            dc_above = dup_ref[HALO - 1:HALO] * upp_ref[HALO - 1:HALO] * _gelu_tanh_grad(c_above)[1] * has_prev
            dc_below = dun_ref[0:1] * upn_ref[0:1] * _gelu_tanh_grad(c_below)[1] * has_next
            row = lax.broadcasted_iota(jnp.int32, dc.shape, 0)
            dc_down = jnp.where(row == 0, dc_above, pltpu.roll(dc, 1, 0))
            dc_up = jnp.where(row == ROW_TILE - 1, dc_below, pltpu.roll(dc, ROW_TILE - 1, 0))
            dg_ref[...] = w_ref[0:1] * dc_up + w_ref[1:2] * dc + w_ref[2:3] * dc_down
            dw = jnp.concatenate([jnp.sum(dc * g_down, axis=0, keepdims=True),
                                  jnp.sum(dc * gv, axis=0, keepdims=True),
                                  jnp.sum(dc * g_up, axis=0, keepdims=True),
                                  jnp.zeros((5, width), F32)], axis=0)
            db = jnp.sum(dc, axis=0, keepdims=True)

            @pl.when(i == 0)
            def _():
                dw_ref[...] = dw
                db_ref[...] = db

            @pl.when(i != 0)
            def _():
                dw_ref[...] += dw
                db_ref[...] += db

        return pl.pallas_call(
            body, name=name + "_b", grid=(n_tiles,), in_specs=halo3 * 3 + [w_spec, b_spec],
            out_specs=[main_spec, main_spec, w_spec, b_spec],
            out_shape=[jax.ShapeDtypeStruct(gate.shape, F32), jax.ShapeDtypeStruct(gate.shape, F32),
                       jax.ShapeDtypeStruct((8, width), F32), jax.ShapeDtypeStruct((1, width), F32)],
            compiler_params=_cparams(1),
        )(gate, gate, gate, up, up, up, du, du, du, w8, b)

    @jax.custom_vjp
    def op(gate, up, w8, b, w_out):
        return _mm(fwd_call(gate, up, w8, b), w_out.astype(BF16), "nn", name + "_out_f")

    def fwd(gate, up, w8, b, w_out):
        u, wb = fwd_call(gate, up, w8, b), w_out.astype(BF16)
        return _mm(u, wb, "nn", name + "_out_f"), (gate, up, w8, b, u, wb)

    def bwd(res, g):
        gate, up, w8, b, u, wb = res
        du = _mm(g, wb, "nt", name + "_out_dx")
        d_gate, d_up, d_w8, d_b = bwd_call(gate, up, w8, b, du)
        return d_gate, d_up, d_w8, d_b, _mm(u, g, "tn", name + "_out_dw")

    op.defvjp(fwd, bwd)
    return op(gate, up, w8, b, w_out)


def loss_head(h, target, tc, name):
    r_total, width = h.shape
    n_tiles = r_total // ROW_TILE
    nct = tc // ROW_TILE

    def call(h, target):
        def body(h_ref, t_ref, dh_ref, loss_ref, acc_ref):
            i = pl.program_id(0)

            @pl.when(i == 0)
            def _():
                acc_ref[...] = jnp.zeros_like(acc_ref)

            @pl.when(i < nct)
            def _():
                dh_ref[...] = jnp.zeros_like(dh_ref)

            @pl.when(i >= nct)
            def _():
                err = h_ref[...] - t_ref[...]
                dh_ref[...] = err * (1.0 / width)
                acc_ref[...] += jnp.sum((err * err).reshape(ROW_TILE // 8, 8, width), axis=0)

            @pl.when(i == n_tiles - 1)
            def _():
                loss_ref[...] = jnp.sum(acc_ref[...]).reshape(1, 1) * (0.5 / width)

        row = pl.BlockSpec((ROW_TILE, width), lambda i: (i, 0))
        return pl.pallas_call(
            body, name=name, grid=(n_tiles,),
            in_specs=[row, pl.BlockSpec((ROW_TILE, width), lambda i: (jnp.maximum(i - nct, 0), 0))],
            out_specs=[row, pl.BlockSpec((1, 1), lambda i: (0, 0))],
            out_shape=[jax.ShapeDtypeStruct(h.shape, F32), jax.ShapeDtypeStruct((1, 1), F32)],
            scratch_shapes=[pltpu.VMEM((8, width), F32)], compiler_params=_cparams(1),
        )(h, target)

    @jax.custom_vjp
    def op(h, target):
        return call(h, target)[1][0, 0]

    def fwd(h, target):
        dh, loss = call(h, target)
        return loss[0, 0], (dh, target)

    def bwd(res, g):
        dh, target = res
        return dh * g, jnp.zeros_like(target)

    op.defvjp(fwd, bwd)
    return op(h, target)


PACK_W = 1024
PACK_TILE = 128


def slab_sum(slabs, name):
    n_slab, n, _ = slabs.shape

    def body(s_ref, o_ref):
        acc = s_ref[0]
        for j in range(1, n_slab):
            acc = acc + s_ref[j]
        o_ref[...] = acc

    return pl.pallas_call(
        body, name=name, grid=(n // PACK_TILE,),
        in_specs=[pl.BlockSpec((n_slab, PACK_TILE, PACK_W), lambda i: (0, i, 0))],
        out_specs=pl.BlockSpec((PACK_TILE, PACK_W), lambda i: (i, 0)),
        out_shape=jax.ShapeDtypeStruct((n, PACK_W), F32), compiler_params=_cparams(1),
    )(slabs)


def adamw(g_slabs, w, m, v, name):
    n_slab, n, _ = g_slabs.shape

    def body(g_ref, w_ref, m_ref, v_ref, go_ref, d_ref, mo_ref, vo_ref):
        g = g_ref[0].astype(F32)
        for j in range(1, n_slab):
            g = g + g_ref[j].astype(F32)
        m_new = ADAM_B1 * m_ref[...] + (1.0 - ADAM_B1) * g
        v_new = ADAM_B2 * v_ref[...] + (1.0 - ADAM_B2) * (g * g)
        m_hat = m_new / (1.0 - ADAM_B1 ** ADAM_STEP)
        v_hat = v_new / (1.0 - ADAM_B2 ** ADAM_STEP)
        go_ref[...] = g
        d_ref[...] = -ADAM_LR * (m_hat / (jnp.sqrt(v_hat) + ADAM_EPS) + ADAM_WD * w_ref[...])
        mo_ref[...] = m_new
        vo_ref[...] = v_new

    flat = pl.BlockSpec((PACK_TILE, PACK_W), lambda i: (i, 0))
    return pl.pallas_call(
        body, name=name, grid=(n // PACK_TILE,),
        in_specs=[pl.BlockSpec((n_slab, PACK_TILE, PACK_W), lambda i: (0, i, 0)), flat, flat, flat],
        out_specs=[flat] * 4, out_shape=[jax.ShapeDtypeStruct((n, PACK_W), F32)] * 4, compiler_params=_cparams(1),
    )(g_slabs, w, m, v)


def all_gather(x, name):
    m_per, n = x.shape

    def body(x_ref, out_ref, send_sems, recv_sems, local_sem):
        px, py, pc = lax.axis_index("x"), lax.axis_index("y"), lax.axis_index("c")
        me, sibling = (px, py, pc), (px, py, 1 - pc)
        chips = [(1 - px, py), (px, 1 - py), (1 - px, 1 - py)]

        def rows(bx, by, bc):
            return out_ref.at[pl.ds((4 * bx + 2 * by + bc) * m_per, m_per), :]

        def copy(k, block, to, src=None):
            return pltpu.make_async_remote_copy(
                src_ref=rows(*block) if src is None else src, dst_ref=rows(*block),
                send_sem=send_sems.at[k], recv_sem=recv_sems.at[k], device_id=to, device_id_type=MESH)

        mine = pltpu.make_async_copy(x_ref, rows(*me), local_sem)
        mine.start()
        first = [copy(0, me, sibling, src=x_ref)]
        first += [copy(1 + j, me, (*chip, pc), src=x_ref) for j, chip in enumerate(chips)]
        for cp in first:
            cp.start()
        passed = [copy(4 + j, (*chip, pc), sibling) for j, chip in enumerate(chips)]
        for j, chip in enumerate(chips):
            copy(1 + j, (*chip, pc), me).wait_recv()
            passed[j].start()
        copy(0, sibling, me).wait_recv()
        for j, chip in enumerate(chips):
            copy(4 + j, (*chip, 1 - pc), me).wait_recv()
        for cp in first + passed:
            cp.wait_send()
        mine.wait()

    return pl.pallas_call(
        body, name=name, out_shape=jax.ShapeDtypeStruct((N_DEV * m_per, n), x.dtype),
        in_specs=[pl.BlockSpec(memory_space=pl.ANY)], out_specs=pl.BlockSpec(memory_space=pl.ANY),
        scratch_shapes=[pltpu.SemaphoreType.DMA((7,)), pltpu.SemaphoreType.DMA((7,)), pltpu.SemaphoreType.DMA],
    )(x)


N_CHIP = 4


def pair_swap(x, name):
    def body(x_ref, out_ref, send_sem, recv_sem):
        sibling = (lax.axis_index("x"), lax.axis_index("y"), 1 - lax.axis_index("c"))
        copy = pltpu.make_async_remote_copy(src_ref=x_ref, dst_ref=out_ref, send_sem=send_sem, recv_sem=recv_sem,
                                            device_id=sibling, device_id_type=MESH)
        copy.start()
        copy.wait()

    return pl.pallas_call(
        body, name=name, out_shape=jax.ShapeDtypeStruct(x.shape, x.dtype),
        in_specs=[pl.BlockSpec(memory_space=pl.ANY)], out_specs=pl.BlockSpec(memory_space=pl.ANY),
        scratch_shapes=[pltpu.SemaphoreType.DMA, pltpu.SemaphoreType.DMA],
    )(x)


def pair_add(a, b, name):
    n_slab, n, _ = a.shape

    def body(a_ref, b_ref, o_ref):
        o_ref[...] = (a_ref[...].astype(F32) + b_ref[...].astype(F32)).astype(o_ref.dtype)

    spec = pl.BlockSpec((1, PACK_TILE, PACK_W), lambda s, i: (s, i, 0))
    return pl.pallas_call(
        body, name=name, grid=(n_slab, n // PACK_TILE), in_specs=[spec, spec], out_specs=spec,
        out_shape=jax.ShapeDtypeStruct(a.shape, a.dtype), compiler_params=_cparams(2),
    )(a, b)


def chip_all_to_all(x, name):
    def body(x_ref, out_ref, send_sems, recv_sems, local_sem):
        px, py, pc = lax.axis_index("x"), lax.axis_index("y"), lax.axis_index("c")
        mine_idx = 2 * px + py
        local = pltpu.make_async_copy(x_ref.at[mine_idx], out_ref.at[mine_idx], local_sem)
        local.start()
        copies = []
        for k, (fx, fy) in enumerate(((0, 1), (1, 0), (1, 1))):
            qx, qy = px ^ fx, py ^ fy
            peer_idx = 2 * qx + qy
            copies.append((
                pltpu.make_async_remote_copy(
                    src_ref=x_ref.at[peer_idx], dst_ref=out_ref.at[mine_idx], send_sem=send_sems.at[k],
                    recv_sem=recv_sems.at[k], device_id=(qx, qy, pc), device_id_type=MESH),
                pltpu.make_async_remote_copy(
                    src_ref=x_ref.at[peer_idx], dst_ref=out_ref.at[peer_idx], send_sem=send_sems.at[k],
                    recv_sem=recv_sems.at[k], device_id=(qx, qy, pc), device_id_type=MESH)))
        for send, _ in copies:
            send.start()
        for _, landing in copies:
            landing.wait_recv()
        for send, _ in copies:
            send.wait_send()
        local.wait()

    return pl.pallas_call(
        body, name=name, out_shape=jax.ShapeDtypeStruct(x.shape, x.dtype),
        in_specs=[pl.BlockSpec(memory_space=pl.ANY)], out_specs=pl.BlockSpec(memory_space=pl.ANY),
        scratch_shapes=[pltpu.SemaphoreType.DMA((3,)), pltpu.SemaphoreType.DMA((3,)), pltpu.SemaphoreType.DMA],
    )(x)


IN_OFFSETS = {}
_off = 0
for _name, _width in (("mla_q", 256), ("mla_kv", 128), ("mla_kr", 32), ("gla_q", 512), ("gla_k", 512), ("gla_v", 512),
                      ("gla_g", 512), ("gla_rf", 16), ("gla_rb", 16), ("ret_q", 512), ("ret_k", 512), ("ret_v", 512),
                      ("ret_g", 512), ("gate_mla", 1024), ("gate_gla", 1024), ("gate_ret", 1024)):
    IN_OFFSETS[_name] = (_off, _off + _width)
    _off += _width
N_IN = _off

P_GLA, P_RET, P_GATE, P_MLAQ, P_MLAKV, P_MLAKR, P_RANK, P_END = 0, 2048, 4096, 7168, 7424, 7552, 7680, 7808


def _pad_in_proj(w):
    def cols(a, b):
        return w[:, IN_OFFSETS[a][0]:IN_OFFSETS[b][1]]

    def z(n):
        return jnp.zeros((w.shape[0], n), w.dtype)

    return jnp.concatenate([cols("gla_q", "gla_g"), cols("ret_q", "ret_g"), cols("gate_mla", "gate_ret"),
                            cols("mla_q", "mla_kv"), z(MLA_NOPE), cols("mla_kr", "mla_kr"),
                            z(HEAD_PAD - MLA_QK), cols("gla_rf", "gla_rb"), z(HEAD_PAD - 2 * GLA_RANK),
                            z(N_IN_PAD - P_END)], axis=1)


def _pad_last(a, n):
    return jnp.pad(a, [(0, 0)] * (a.ndim - 1) + [(0, n - a.shape[-1])])


def _position_tables(tc, t):
    pos = jnp.arange(t)
    inv = ROPE_THETA ** (-jnp.arange(MLA_ROPE // 4, dtype=F32) * 2.0 / (MLA_ROPE // 2))
    ang_r = (pos // GRID_W).astype(F32)[:, None] * inv[None, :]
    ang_c = (pos % GRID_W).astype(F32)[:, None] * inv[None, :]
    z8, z32, z64 = jnp.zeros((t, 8), F32), jnp.zeros((t, 32), F32), jnp.zeros((t, 64), F32)
    lat_c = jnp.concatenate([jnp.ones((t, 64), F32), jnp.cos(ang_r), jnp.cos(ang_r), jnp.cos(ang_c), jnp.cos(ang_c),
                             z32], axis=1)
    lat_sn = jnp.concatenate([z64, -jnp.sin(ang_r), z8, -jnp.sin(ang_c), z8, z32], axis=1)
    lat_sp = jnp.concatenate([z64, z8, jnp.sin(ang_r), z8, jnp.sin(ang_c), z32], axis=1)
    ctx_c = jnp.concatenate([jnp.ones((tc, MLA_QK), F32), jnp.zeros((tc, HEAD_PAD - MLA_QK), F32)], axis=1)
    ctx_z = jnp.zeros((tc, HEAD_PAD), F32)
    rinv = 1.0 / (RET_THETA ** jnp.linspace(0.0, 1.0, RET_DK // 2, dtype=F32))
    rang = jnp.arange(tc + t).astype(F32)[:, None] * rinv[None, :]
    return dict(c=jnp.concatenate([ctx_c, lat_c]), sn=jnp.concatenate([ctx_z, lat_sn]),
                sp=jnp.concatenate([ctx_z, lat_sp]),
                rc=jnp.concatenate([jnp.cos(rang), jnp.cos(rang)], axis=1),
                rs=jnp.concatenate([-jnp.sin(rang), jnp.sin(rang)], axis=1))


def _heads(x):
    return [x[:, h * HEAD_PAD:(h + 1) * HEAD_PAD] for h in range(x.shape[1] // HEAD_PAD)]


def _mla_rope(x, c, sn, sp):
    return x * c + _roll(x, HEAD_PAD - 8, 1) * sn + _roll(x, 8, 1) * sp


def _norm_mod_fn(shift_row, scale_row):
    def fn(h, mod, w):
        return (_rms(h, D, w) * (1.0 + mod[scale_row:scale_row + 1]) + mod[shift_row:shift_row + 1],)
    return fn


def _resid_fn(gate_row):
    def fn(h, y, mod):
        return (h + mod[gate_row:gate_row + 1] * y,)
    return fn


def _resid_norm_fn(gate_row, shift_row, scale_row):
    def fn(h, y, mod, w):
        h1 = h + mod[gate_row:gate_row + 1] * y
        return h1, _norm_mod_fn(shift_row, scale_row)(h1, mod, w)[0]
    return fn


def _mla_prep_fn(x, c, sn, sp, q_norm_a, w_qb, q_norm, kv_norm_a, w_k, w_v, k_norm):
    cq, ckv = x[:, :MLA_Q_LORA], x[:, MLA_Q_LORA:MLA_Q_LORA + MLA_KV_LORA]
    kr = x[:, MLA_Q_LORA + MLA_KV_LORA:]
    qf = _bdot(_rms(cq, MLA_Q_LORA, q_norm_a), w_qb, "nn")
    q = jnp.concatenate([_mla_rope(_rms(qh, MLA_QK, q_norm), c, sn, sp) for qh in _heads(qf)], axis=1)
    xkv = _rms(ckv, MLA_KV_LORA, kv_norm_a)
    kf = _bdot(xkv, w_k, "nn")
    k = jnp.concatenate([_mla_rope(_rms(kh + kr, MLA_QK, k_norm), c, sn, sp) for kh in _heads(kf)], axis=1)
    return q * (ATT_SCALE * LOG2E), k, _bdot(xkv, w_v, "nn")


def _decay_fn(x, w2, b):
    la = _log_sigmoid(_bdot(x[:, :HEAD_PAD], w2, "nn") + b) * (1.0 / GLA_NORMALIZER)
    return la[:, :GLA_HEADS * GLA_DK], la[:, GLA_HEADS * GLA_DK:]


def _gla_out_fn(o_f, o_b, g, w):
    y = jnp.concatenate([_rms(oh, HEAD_PAD, w) for oh in _heads(o_f + o_b)], axis=1)
    return (y * _silu(g),)


def _ret_out_fn(o_f, o_b, g):
    y = jnp.concatenate([_rms(oh, HEAD_PAD) for oh in _heads(o_f + o_b)], axis=1)
    return (y * _silu(g),)


def _merge_fn(z0, z1, z2, g0a, g0b, g1a, g1b, g2a, g2b, bg):
    out = 0.0
    for n, (z, ga, gb) in enumerate(((z0, g0a, g0b), (z1, g1a, g1b), (z2, g2a, g2b))):
        out = out + jax.nn.sigmoid(jnp.concatenate([ga, gb], axis=1) + bg[n:n + 1]) * z
    return (out,)


def _layer(l, h, mod, w, tabs, tc):
    nct = tc // ROW_TILE
    tag = f"_l{l}"
    row = lambda a: a[l][None]
    pieces = rowwise("norm1" + tag, _norm_mod_fn(0, 1), [h], [mod], [row(w["norm1_w"])], [D], nct,
                     then=("pieces", _pad_in_proj(w["w_in"][l]), "in_proj" + tag))
    piece = lambda start: pieces[start // PIECE_W]

    w_qb = _pad_last(w["mla_w_qb"][l].reshape(MLA_Q_LORA, MLA_HEADS, MLA_QK), HEAD_PAD).reshape(MLA_Q_LORA, -1)
    w_kvb = w["mla_w_kvb"][l].reshape(MLA_KV_LORA, MLA_HEADS, MLA_NOPE + MLA_V)
    w_k = _pad_last(w_kvb[:, :, :MLA_NOPE], HEAD_PAD).reshape(MLA_KV_LORA, -1)
    w_v = _pad_last(w_kvb[:, :, MLA_NOPE:], HEAD_PAD).reshape(MLA_KV_LORA, -1)
    rope = [tabs["c"], tabs["sn"], tabs["sp"]]
    y_mla = rowwise("mla_prep" + tag, _mla_prep_fn, [piece(P_MLAQ)] + rope, [],
                    [row(w["mla_q_norm_a"]), w_qb, _pad_last(row(w["mla_q_norm"]), HEAD_PAD),
                     row(w["mla_kv_norm_a"]), w_k, w_v, _pad_last(row(w["mla_k_norm"]), HEAD_PAD)],
                    [MLA_HEADS * HEAD_PAD] * 3, nct, diff_rows=[True, False, False, False],
                    then=("attention", tc, "attn" + tag))
    wb_mla = _pad_last(w["w_branch"][l, 0].reshape(MLA_HEADS, MLA_V, D).transpose(0, 2, 1), HEAD_PAD)
    wb_mla = wb_mla.transpose(0, 2, 1).reshape(MLA_HEADS * HEAD_PAD, D)

    w2 = jnp.zeros((HEAD_PAD, 2 * GLA_HEADS * GLA_DK), F32)
    w2 = w2.at[:GLA_RANK, :GLA_HEADS * GLA_DK].set(w["gla_w_gk2"][l, 0])
    w2 = w2.at[GLA_RANK:2 * GLA_RANK, GLA_HEADS * GLA_DK:].set(w["gla_w_gk2"][l, 1])
    la_f, la_b = rowwise("gla_decay" + tag, _decay_fn, [piece(P_RANK)], [],
                         [w2, w["gla_b_gk"][l].reshape(1, -1)], [GLA_HEADS * GLA_DK] * 2, nct)
    gq, gk, gv, gg = [piece(P_GLA + n * PIECE_W) for n in range(4)]
    o_f = scan("gla", True, gq, gk, gv, la_f, tc, "gla_fw" + tag)
    o_b = scan("gla", False, gq, gk, gv, la_b, tc, "gla_bw" + tag)
    z_gla = rowwise("gla_out" + tag, _gla_out_fn, [o_f, o_b, gg], [], [row(w["gla_o_norm"])], [512], nct,
                    then=("linear", w["w_branch"][l, 1], "branch_gla" + tag))

    rq, rk, rv, rg = [piece(P_RET + n * PIECE_W) for n in range(4)]
    rd = jnp.broadcast_to(w["ret_decay"][l][:, :, None, None], (2, RET_HEADS, 8, HEAD_PAD))
    turn = (tabs["rc"], tabs["rs"])
    r_f = scan("ret", True, rq, rk, rv, rd[0], tc, "ret_fw" + tag, rot=turn)
    r_b = scan("ret", False, rq, rk, rv, rd[1], tc, "ret_bw" + tag, rot=turn)
    z_ret = rowwise("ret_out" + tag, _ret_out_fn, [r_f, r_b, rg], [], [], [512], nct,
                    then=("linear", w["w_branch"][l, 2], "branch_ret" + tag))

    z = [linear(y_mla, wb_mla, "branch_mla" + tag), z_gla, z_ret]
    gates = [piece(P_GATE + n * PIECE_W) for n in range(6)]
    y = rowwise("merge" + tag, _merge_fn, z + gates, [], [_pad_rows(w["b_gate"][l], 8)], [D], nct,
                then=("linear", w["w_out"][l], "w_out" + tag))
    h, a2 = rowwise("resid1_norm2" + tag, _resid_norm_fn(2, 3, 4), [h, y], [mod], [row(w["norm2_w"])], [D, D], nct)
    gate = linear(a2, w["w_ffn_in"][l][:, :D_FF], "ffn_gate" + tag)
    up = linear(a2, w["w_ffn_in"][l][:, D_FF:], "ffn_up" + tag)
    f = conv_ffn_out(gate, up, _pad_rows(w["w_dw"][l], 8), row(w["b_dw"]), w["w_ffn_out"][l], tc, "ffn_mid" + tag)
    return rowwise("resid2" + tag, _resid_fn(5), [h, f], [mod], [], [D], nct)[0]


def _pad_rows(a, n):
    return jnp.pad(a, [(0, n - a.shape[0])] + [(0, 0)] * (a.ndim - 1))


def local_loss(w, mod, x, ctx, target):
    tc, t = ctx.shape[0], x.shape[0]
    tabs = _position_tables(tc, t)
    h = jnp.concatenate([ctx, x], axis=0)
    for l in range(DEPTH):
        h = _layer(l, h, mod[l], w, tabs, tc)
    return loss_head(h, target, tc, "loss_head")


ADA_ROWS = 16


def ada_forward(cond_in, w_ada, b_loc):
    cols = w_ada.shape[2]

    def body(x_ref, w_ref, b_ref, o_ref):
        s = _silu(x_ref[...])
        for l in range(DEPTH):
            o_ref[l] = _dg(s, w_ref[l], "nn") + b_ref[l]

    return pl.pallas_call(
        body, name="ada_forward", out_shape=jax.ShapeDtypeStruct((DEPTH, ADA_ROWS, cols), F32),
        compiler_params=pltpu.CompilerParams(vmem_limit_bytes=VMEM_LIMIT_BYTES),
    )(cond_in, w_ada, b_loc)


def ada_backward(cond_in, g_loc, dmod_own, w_ada):
    cols = w_ada.shape[2]

    def body(x_ref, g_ref, own_ref, w_ref, gw_ref, dc_ref, gb_ref):
        x = x_ref[...]
        s = _silu(x)
        dcond = jnp.zeros((8, D), F32)
        for l in range(DEPTH):
            g_ctx = jnp.sum(g_ref[2 * l], axis=0, keepdims=True)
            g_rows = jnp.concatenate([g_ref[2 * l + 1], jnp.broadcast_to(g_ctx, (8, cols))], axis=0)
            keep = lax.broadcasted_iota(jnp.int32, (ADA_ROWS, cols), 0) <= N_DEV
            gw_ref[l] = _dg(s, jnp.where(keep, g_rows, 0.0), "tn")
            dcond = dcond + _dg(jnp.broadcast_to(g_ctx, (8, cols)), w_ref[l], "nt")
            gb_ref[l:l + 1, :] = own_ref[2 * l:2 * l + 1, :] + own_ref[2 * l + 1:2 * l + 2, :]
        xc = x[N_DEV:N_DEV + 1]
        sig = jax.nn.sigmoid(xc)
        dc_ref[...] = dcond[0:1] * (sig * (1.0 + xc * (1.0 - sig)))

    return pl.pallas_call(
        body, name="ada_backward",
        out_shape=[jax.ShapeDtypeStruct(w_ada.shape, F32), jax.ShapeDtypeStruct((1, D), F32),
                   jax.ShapeDtypeStruct((DEPTH, 6 * D), F32)],
        compiler_params=pltpu.CompilerParams(vmem_limit_bytes=VMEM_LIMIT_BYTES),
    )(cond_in, g_loc, dmod_own, w_ada)


WEIGHTS = ["c_ctx", "w_ada", "b_ada", "norm1_w", "norm2_w", "w_in", "b_gate", "mla_q_norm_a", "mla_w_qb",
           "mla_kv_norm_a", "mla_w_kvb", "mla_q_norm", "mla_k_norm", "gla_w_gk2", "gla_b_gk", "gla_o_norm",
           "ret_decay", "w_branch", "w_out", "w_ffn_in", "w_dw", "b_dw", "w_ffn_out"]
INPUTS = ["x", "c", "ctx"] + WEIGHTS + ["loss_target"] + ["m_" + n for n in WEIGHTS] + ["v_" + n for n in WEIGHTS]
BIG = {"w_in": 2, "mla_w_qb": 2, "mla_w_kvb": 2, "w_branch": 3, "w_out": 1, "w_ffn_in": 2, "w_ffn_out": 1}
SMALL_SHARDED = {"b_gate": 2, "gla_w_gk2": 3, "gla_b_gk": 2, "w_dw": 2}
SMALL = ["c_ctx", "b_ada", "norm1_w", "norm2_w", "b_gate", "mla_q_norm_a", "mla_kv_norm_a", "mla_q_norm", "mla_k_norm",
         "gla_w_gk2", "gla_b_gk", "gla_o_norm", "ret_decay", "w_dw", "b_dw"]


def _entry_rows(size, align):
    return -(-size // (PACK_W * align)) * align


def _pack(arrays, rows, dtype, align, lead=0):
    parts = []
    for a in arrays:
        head = a.shape[:lead]
        size = math.prod(a.shape[lead:])
        r = _entry_rows(size, align)
        if r * PACK_W == size:
            parts.append(a.astype(dtype).reshape(head + (r, PACK_W)))
        else:
            flat = jnp.pad(a.astype(dtype).reshape(head + (size,)), [(0, 0)] * lead + [(0, r * PACK_W - size)])
            parts.append(flat.reshape(head + (r, PACK_W)))
    used = sum(p.shape[lead] for p in parts)
    if rows > used:
        parts.append(jnp.zeros(parts[0].shape[:lead] + (rows - used, PACK_W), dtype))
    return jnp.concatenate(parts, axis=lead)


def _pack_rows(shapes, align, multiple):
    used = sum(_entry_rows(math.prod(s), align) for s in shapes)
    return -(-used // multiple) * multiple


def _unpack(pack, shapes, align):
    head = pack.shape[:-2]
    out, off = [], 0
    for shape in shapes:
        size = math.prod(shape)
        r = _entry_rows(size, align)
        block = lax.slice_in_dim(pack, off, off + r, axis=len(head))
        if r * PACK_W != size:
            block = block.reshape(head + (r * PACK_W,))[..., :size]
        out.append(block.reshape(head + tuple(shape)))
        off += r
    return out


def _join_shards(stacked, axis):
    moved = jnp.moveaxis(stacked, 0, axis)
    shape = list(moved.shape)
    return moved.reshape(shape[:axis] + [shape[axis] * shape[axis + 1]] + shape[axis + 2:])


def _split_shards(full, axis):
    shape = list(full.shape)
    split = full.reshape(shape[:axis] + [N_DEV, shape[axis] // N_DEV] + shape[axis + 1:])
    return jnp.moveaxis(split, axis, 0)


def _gather_shards(local, axes, dtype, rows_multiple, name):
    names = list(axes)
    shapes = [local[n].shape for n in names]
    rows = _pack_rows(shapes, rows_multiple, rows_multiple)
    gathered = all_gather(_pack([local[n] for n in names], rows, dtype, rows_multiple), name)
    stacked = _unpack(gathered.reshape(N_DEV, rows, PACK_W), shapes, rows_multiple)
    return {n: _join_shards(s, axes[n]).astype(F32) for n, s in zip(names, stacked)}


def kernel(*args):
    a = dict(zip(INPUTS, args))
    me = 4 * lax.axis_index("x") + 2 * lax.axis_index("y") + lax.axis_index("c")
    cols = a["w_ada"].shape[2]

    small_names = list(SMALL_SHARDED)
    small_local = [a[n].shape for n in small_names]
    first_rows = _pack_rows([a["c"].shape] + small_local, 8, 8)
    first = all_gather(_pack([a["c"]] + [a[n] for n in small_names], first_rows, F32, 8), "gather_small")
    first = _unpack(first.reshape(N_DEV, first_rows, PACK_W), [a["c"].shape] + small_local, 8)
    c_all = first[0][:, 0]

    cond_in = jnp.concatenate([c_all, a["c_ctx"][None], jnp.zeros((ADA_ROWS - N_DEV - 1, D), F32)], axis=0)
    b_loc = lax.dynamic_slice_in_dim(a["b_ada"], me * cols, cols, axis=1)[:, None, :]
    mod_loc = ada_forward(cond_in, a["w_ada"], b_loc)
    mod_all = all_gather(mod_loc.reshape(DEPTH * ADA_ROWS, cols), "gather_mod")
    mod_all = mod_all.reshape(N_DEV, DEPTH, ADA_ROWS, cols).transpose(1, 2, 0, 3).reshape(DEPTH, ADA_ROWS, 6, D)
    mod_me = lax.dynamic_index_in_dim(mod_all, me, axis=1, keepdims=False)
    mod = jnp.pad(jnp.stack([mod_all[:, N_DEV], mod_me], axis=1), ((0, 0), (0, 0), (0, 2), (0, 0)))

    w = _gather_shards(a, BIG, BF16, 16, "gather_weights")
    w.update({n: _join_shards(s, SMALL_SHARDED[n]) for n, s in zip(small_names, first[1:])})
    for n in SMALL:
        if n not in SMALL_SHARDED and n not in ("c_ctx", "b_ada"):
            w[n] = a[n]

    loss, (gw, gmod, gx) = jax.value_and_grad(local_loss, argnums=(0, 1, 2))(
        w, mod, a["x"][0], a["ctx"][0], a["loss_target"][0])
    loss = lax.psum(loss, ("x", "y", "c"))

    dmod_own = gmod[:, :, :6].reshape(2 * DEPTH, 6 * D)
    g_all = all_gather(jnp.pad(dmod_own, ((0, 8 - 2 * DEPTH), (0, 0))), "gather_dmod").reshape(N_DEV, 8, 6 * D)
    g_loc = lax.dynamic_slice_in_dim(g_all[:, :2 * DEPTH], me * cols, cols, axis=2).transpose(1, 0, 2)
    g_w_ada, g_c_ctx, g_b_ada = ada_backward(cond_in, g_loc, dmod_own, a["w_ada"])

    small_part = dict(gw, c_ctx=g_c_ctx, b_ada=g_b_ada)
    small_shapes = [a[n].shape if n not in SMALL_SHARDED else gw[n].shape for n in SMALL]
    rows = _pack_rows(small_shapes, 8, PACK_TILE)
    parts = all_gather(_pack([small_part[n] for n in SMALL], rows, F32, 8), "gather_small_grads")
    small_sum = _unpack(slab_sum(parts.reshape(N_DEV, rows, PACK_W), "sum_small_grads"), small_shapes, 8)
    g_small = {}
    for n, g in zip(SMALL, small_sum):
        if n in SMALL_SHARDED:
            ax = SMALL_SHARDED[n]
            g = lax.dynamic_slice_in_dim(g, me * a[n].shape[ax], a[n].shape[ax], axis=ax)
        g_small[n] = g

    big_rows = _pack_rows([a[n].shape for n in BIG], 16, PACK_TILE)
    slabs = _pack([_split_shards(gw[n], ax) for n, ax in BIG.items()], big_rows, BF16, 16, lead=1)
    by_core = slabs.reshape(N_CHIP, 2, big_rows, PACK_W)
    my_core = lax.axis_index("c")
    keep = lax.dynamic_index_in_dim(by_core, my_core, axis=1, keepdims=False)
    give = lax.dynamic_index_in_dim(by_core, 1 - my_core, axis=1, keepdims=False)
    pair_sum = pair_add(keep, pair_swap(give, "swap_grads"), "add_pair_grads")
    landed = chip_all_to_all(pair_sum, "scatter_grads")

    def update(names, g_slabs, rows, align, label):
        shapes = [a[n].shape for n in names]
        packs = [_pack([a[pre + n] for n in names], rows, F32, align) for pre in ("", "m_", "v_")]
        outs = adamw(g_slabs, *packs, label)
        return [dict(zip(names, _unpack(o, shapes, align))) for o in outs]

    res_big = update(list(BIG), landed, big_rows, 16, "adamw_big")
    ada_rows = _pack_rows([a["w_ada"].shape], 8, PACK_TILE)
    res_ada = update(["w_ada"], _pack([g_w_ada], ada_rows, F32, 8)[None], ada_rows, 8, "adamw_ada")
    small_rows = _pack_rows([a[n].shape for n in SMALL], 8, PACK_TILE)
    res_small = update(SMALL, _pack([g_small[n] for n in SMALL], small_rows, F32, 8)[None], small_rows, 8,
                       "adamw_small")

    outs = [loss, gx[None]]
    for k in range(4):
        merged = {**res_big[k], **res_ada[k], **res_small[k]}
        outs += [merged[n] for n in WEIGHTS]
    return tuple(outs)
```

```python
import functools
import math

import jax
import jax.numpy as jnp
from jax import lax
from jax.experimental import pallas as pl
from jax.experimental.pallas import tpu as pltpu

F32 = jnp.float32
BF16 = jnp.bfloat16

N_DEV = 8
D = 1024
DEPTH = 2
GRID_W = 64
MLA_HEADS = 8
MLA_NOPE = 64
MLA_ROPE = 32
MLA_QK = 96
MLA_V = 64
MLA_Q_LORA = 256
MLA_KV_LORA = 128
GLA_HEADS = 4
GLA_DK = 128
GLA_RANK = 16
GLA_NORMALIZER = 16.0
RET_HEADS = 4
RET_DK = 128
BRANCH_W = 512
D_FF = 2816
CHUNK = 64
ROPE_THETA = 10000.0
RET_THETA = 10000.0
EPS = 1e-6
HEAD_PAD = 128
N_IN_PAD = 8192

ADAM_LR = 0.001
ADAM_B1 = 0.9
ADAM_B2 = 0.999
ADAM_EPS = 1e-08
ADAM_WD = 0.01
ADAM_STEP = 10

ROW_TILE = 256
SCAN_CHUNKS = ROW_TILE // CHUNK
VMEM_LIMIT_BYTES = 56 * 1024 * 1024
MESH = pl.DeviceIdType.MESH


def _cparams(n_axes):
    return pltpu.CompilerParams(dimension_semantics=("arbitrary",) * n_axes, vmem_limit_bytes=VMEM_LIMIT_BYTES)


def _pick(dim, cands):
    for cand in cands:
        if dim % cand == 0:
            return cand
    return dim


_DOT_DIMS = {"nn": (((1,), (0,)), ((), ())), "nt": (((1,), (1,)), ((), ())), "tn": (((0,), (0,)), ((), ()))}


def _dg(a, b, mode):
    return lax.dot_general(a.astype(BF16), b.astype(BF16), _DOT_DIMS[mode], preferred_element_type=F32)


def _bdot(a, b, mode):
    @jax.custom_vjp
    def f(a, b):
        return _dg(a, b, mode)

    def fwd(a, b):
        return _dg(a, b, mode), (a, b)

    def bwd(res, g):
        a, b = res
        if mode == "nn":
            return _dg(g, b, "nt").astype(a.dtype), _dg(a, g, "tn").astype(b.dtype)
        if mode == "nt":
            return _dg(g, b, "nn").astype(a.dtype), _dg(g, a, "tn").astype(b.dtype)
        return _dg(b, g, "nt").astype(a.dtype), _dg(a, g, "nn").astype(b.dtype)

    f.defvjp(fwd, bwd)
    return f(a, b)


def _roll(x, shift, axis):
    n = x.shape[axis]
    shift = shift % n

    @jax.custom_vjp
    def f(x):
        return pltpu.roll(x, shift, axis)

    def fwd(x):
        return pltpu.roll(x, shift, axis), None

    def bwd(_, g):
        return (pltpu.roll(g, (n - shift) % n, axis),)

    f.defvjp(fwd, bwd)
    return f(x)


@jax.custom_jvp
def _log_sigmoid(x):
    return jnp.minimum(x, 0.0) - jnp.log(1.0 + jnp.exp(-jnp.abs(x)))


@_log_sigmoid.defjvp
def _log_sigmoid_jvp(primals, tangents):
    (x,), (t,) = primals, tangents
    return _log_sigmoid(x), t * jax.nn.sigmoid(-x)


def _rms(x, n, w=None):
    y = x * lax.rsqrt(jnp.sum(x * x, axis=-1, keepdims=True) * (1.0 / n) + EPS)
    return y if w is None else y * w


def _silu(x):
    return x * jax.nn.sigmoid(x)


def _gelu_tanh(x):
    return 0.5 * x * (1.0 + jnp.tanh(math.sqrt(2.0 / math.pi) * (x + 0.044715 * (x * x * x))))


MM_RESIDENT_BYTES = 6 * 1024 * 1024


def _mm(a, b, mode, name):
    if mode == "nn":
        (m, k), (_, n) = a.shape, b.shape
    elif mode == "nt":
        (m, k), (n, _) = a.shape, b.shape
    else:
        (k, m), (_, n) = a.shape, b.shape
    tm = _pick(m, (1024, 768, 1408, 512, 256, 128))
    tn = _pick(n, (1024, 1408, 512, 256, 128))
    tk = _pick(k, (1024, 768, 1408, 512, 256, 128))
    nk = k // tk
    if mode == "nn" and nk == 1 and k * n * b.dtype.itemsize <= MM_RESIDENT_BYTES:
        tn = n
    if mode == "nn":
        a_spec = pl.BlockSpec((tm, tk), lambda i, j, kk: (i, kk))
        b_spec = pl.BlockSpec((tk, tn), lambda i, j, kk: (kk, j))
    elif mode == "nt":
        a_spec = pl.BlockSpec((tm, tk), lambda i, j, kk: (i, kk))
        b_spec = pl.BlockSpec((tn, tk), lambda i, j, kk: (j, kk))
    else:
        a_spec = pl.BlockSpec((tk, tm), lambda i, j, kk: (kk, i))
        b_spec = pl.BlockSpec((tk, tn), lambda i, j, kk: (kk, j))

    def body(a_ref, b_ref, o_ref):
        kk = pl.program_id(2)
        part = _dg(a_ref[...], b_ref[...], mode)
        if nk == 1:
            o_ref[...] = part
        else:
            @pl.when(kk == 0)
            def _():
                o_ref[...] = part

            @pl.when(kk != 0)
            def _():
                o_ref[...] += part

    return pl.pallas_call(
        body, name=name, grid=(m // tm, n // tn, nk),
        in_specs=[a_spec, b_spec], out_specs=pl.BlockSpec((tm, tn), lambda i, j, kk: (i, j)),
        out_shape=jax.ShapeDtypeStruct((m, n), F32),
        compiler_params=_cparams(3),
    )(a, b)


def linear(x, w, name):
    @jax.custom_vjp
    def op(x, w):
        return _mm(x, w.astype(BF16), "nn", name + "_f")

    def fwd(x, w):
        wb = w.astype(BF16)
        return _mm(x, wb, "nn", name + "_f"), (x, wb)

    def bwd(res, g):
        x, wb = res
        return _mm(g, wb, "nt", name + "_dx"), _mm(x, g, "tn", name + "_dw")

    op.defvjp(fwd, bwd)
    return op(x, w)


PIECE_W = 512
PIECE_ROWS = 384
PIECE_GROUP = 4


def _mm_split(a, wb, name):
    (r, k), n = a.shape, wb.shape[1] // PIECE_W
    tm = _pick(r, (PIECE_ROWS, ROW_TILE))
    width = PIECE_GROUP * PIECE_W

    n_tiles = r // tm

    def body(a_ref, w_ref, *out_refs):
        j = pl.program_id(0)
        res = _dg(a_ref[...], w_ref[...], "nn")
        for group in range(n // PIECE_GROUP):
            @pl.when(j == group)
            def _(group=group):
                for p in range(PIECE_GROUP):
                    out_refs[group * PIECE_GROUP + p][...] = res[:, p * PIECE_W:(p + 1) * PIECE_W]

    def out_spec(jj):
        group = jj // PIECE_GROUP
        return pl.BlockSpec((tm, PIECE_W),
                            lambda j, i: (jnp.where(j == group, i, jnp.where(j < group, 0, n_tiles - 1)), 0))

    return pl.pallas_call(
        body, name=name, grid=(n // PIECE_GROUP, n_tiles),
        in_specs=[pl.BlockSpec((tm, k), lambda j, i: (i, 0)), pl.BlockSpec((k, width), lambda j, i: (0, j))],
        out_specs=[out_spec(jj) for jj in range(n)],
        out_shape=[jax.ShapeDtypeStruct((r, PIECE_W), F32)] * n, compiler_params=_cparams(2),
    )(a, wb)


def _mm_join(gs, wb, name):
    n, (r, _), k = len(gs), gs[0].shape, wb.shape[0]
    tm = _pick(r, (PIECE_ROWS, ROW_TILE))
    n_groups = n // PIECE_GROUP

    def body(*refs):
        g_refs, w_ref, o_ref = refs[:n], refs[n], refs[n + 1]
        j = pl.program_id(1)
        for group in range(n_groups):
            @pl.when(j == group)
            def _(group=group):
                g = jnp.concatenate([g_refs[group * PIECE_GROUP + p][...].astype(BF16) for p in range(PIECE_GROUP)],
                                    axis=1)
                part = _dg(g, w_ref[...], "nt")
                if group == 0:
                    o_ref[...] = part
                else:
                    o_ref[...] += part

    return pl.pallas_call(
        body, name=name, grid=(r // tm, n_groups),
        in_specs=[pl.BlockSpec((tm, PIECE_W), lambda i, j: (i, 0))] * n
        + [pl.BlockSpec((k, PIECE_GROUP * PIECE_W), lambda i, j: (0, j))],
        out_specs=pl.BlockSpec((tm, k), lambda i, j: (i, 0)), out_shape=jax.ShapeDtypeStruct((r, k), F32),
        compiler_params=_cparams(2),
    )(*gs, wb)


def _mm_join_tn(a, gs, name):
    n, (r, k) = len(gs), a.shape
    tk = _pick(r, (768, 512, ROW_TILE))
    width = PIECE_GROUP * PIECE_W

    def group_call(group):
        def body(a_ref, *refs):
            g_refs, o_ref = refs[:PIECE_GROUP], refs[PIECE_GROUP]
            part = _dg(a_ref[...], jnp.concatenate([g_ref[...].astype(BF16) for g_ref in g_refs], axis=1), "tn")

            @pl.when(pl.program_id(0) == 0)
            def _():
                o_ref[...] = part

            @pl.when(pl.program_id(0) != 0)
            def _():
                o_ref[...] += part

        return pl.pallas_call(
            body, name=f"{name}{group}", grid=(r // tk,),
            in_specs=[pl.BlockSpec((tk, k), lambda kk: (kk, 0))]
            + [pl.BlockSpec((tk, PIECE_W), lambda kk: (kk, 0))] * PIECE_GROUP,
            out_specs=pl.BlockSpec((k, width), lambda kk: (0, 0)),
            out_shape=jax.ShapeDtypeStruct((k, width), F32), compiler_params=_cparams(1),
        )(a, *gs[group * PIECE_GROUP:(group + 1) * PIECE_GROUP])

    return jnp.concatenate([group_call(group) for group in range(n // PIECE_GROUP)], axis=1)


def rowwise(name, fn, rows, segs, params, out_widths, nct, diff_rows=None, then=None):
    n_row, n_seg, n_par, n_out = len(rows), len(segs), len(params), len(out_widths)
    out_dtype = F32 if then is None else BF16
    diff_rows = [True] * n_row if diff_rows is None else list(diff_rows)
    r_total = rows[0].shape[0]
    n_tiles = r_total // ROW_TILE

    def seg_of(i):
        return jnp.where(i < nct, 0, 1)

    def row_spec(width):
        return pl.BlockSpec((ROW_TILE, width), lambda i: (i, 0))

    def seg_spec(shape):
        nd = len(shape)
        return pl.BlockSpec((1,) + tuple(shape[1:]), lambda i: (seg_of(i),) + (0,) * (nd - 1))

    def par_spec(shape):
        nd = len(shape)
        return pl.BlockSpec(tuple(shape), lambda i: (0,) * nd)

    in_specs = ([row_spec(r.shape[1]) for r in rows] + [seg_spec(s.shape) for s in segs]
                + [par_spec(p.shape) for p in params])

    def load(refs):
        vals = [r[...].astype(F32) for r in refs[:n_row]]
        vals += [r[0].astype(F32) for r in refs[n_row:n_row + n_seg]]
        vals += [r[...].astype(F32) for r in refs[n_row + n_seg:n_row + n_seg + n_par]]
        return vals

    def fwd_call(arrs):
        def body(*refs):
            outs = fn(*load(refs))
            for o_ref, val in zip(refs[n_row + n_seg + n_par:], outs):
                o_ref[...] = val.astype(o_ref.dtype)

        return pl.pallas_call(
            body, name=name + "_f", grid=(n_tiles,), in_specs=in_specs,
            out_specs=[row_spec(w) for w in out_widths],
            out_shape=[jax.ShapeDtypeStruct((r_total, w), out_dtype) for w in out_widths],
            compiler_params=_cparams(1),
        )(*arrs)

    d_idx = [k for k in range(n_row) if diff_rows[k]]

    def bwd_call(arrs, douts):
        n_in = n_row + n_seg + n_par

        def body(*refs):
            i = pl.program_id(0)
            vals = load(refs[:n_in])
            gs = [r[...] for r in refs[n_in:n_in + n_out]]
            out_refs = refs[n_in + n_out:]
            diff_pos = d_idx + list(range(n_row, n_in))

            def f(*dv):
                full = list(vals)
                for pos, v in zip(diff_pos, dv):
                    full[pos] = v
                return tuple(fn(*full))

            _, vjp = jax.vjp(f, *[vals[p] for p in diff_pos])
            grads = vjp(tuple(gs))
            nd = len(d_idx)
            for o_ref, g in zip(out_refs[:nd], grads[:nd]):
                o_ref[...] = g
            first_seg = jnp.logical_or(i == 0, i == nct)
            for o_ref, g in zip(out_refs[nd:nd + n_seg], grads[nd:nd + n_seg]):
                @pl.when(first_seg)
                def _(o_ref=o_ref, g=g):
                    o_ref[0] = g

                @pl.when(jnp.logical_not(first_seg))
                def _(o_ref=o_ref, g=g):
                    o_ref[0] += g
            for o_ref, g in zip(out_refs[nd + n_seg:], grads[nd + n_seg:]):
                @pl.when(i == 0)
                def _(o_ref=o_ref, g=g):
                    o_ref[...] = g

                @pl.when(i != 0)
                def _(o_ref=o_ref, g=g):
                    o_ref[...] += g

        out_specs = ([row_spec(rows[k].shape[1]) for k in d_idx] + [seg_spec(s.shape) for s in segs]
                     + [par_spec(p.shape) for p in params])
        out_shape = ([jax.ShapeDtypeStruct(rows[k].shape, F32) for k in d_idx]
                     + [jax.ShapeDtypeStruct(s.shape, F32) for s in segs]
                     + [jax.ShapeDtypeStruct(p.shape, F32) for p in params])
        return pl.pallas_call(
            body, name=name + "_b", grid=(n_tiles,),
            in_specs=in_specs + [row_spec(w) for w in out_widths],
            out_specs=out_specs, out_shape=out_shape, compiler_params=_cparams(1),
        )(*arrs, *douts)

    @jax.custom_vjp
    def op(*arrs):
        return tuple(fwd_call(arrs))

    def op_fwd(*arrs):
        return tuple(fwd_call(arrs)), arrs

    def op_bwd(arrs, douts):
        grads = list(bwd_call(arrs, douts))
        nd = len(d_idx)
        row_grads = [jnp.zeros_like(arrs[k]) for k in range(n_row)]
        for k, g in zip(d_idx, grads[:nd]):
            row_grads[k] = g
        return tuple(row_grads + grads[nd:])

    if then is None:
        op.defvjp(op_fwd, op_bwd)
        return op(*rows, *segs, *params)

    kind, w, mm_name = then

    if kind == "attention":
        @jax.custom_vjp
        def attended(*arrs):
            return attention_forward(*fwd_call(arrs), w, mm_name)[0]

        def attended_fwd(*arrs):
            qkv = fwd_call(arrs)
            o, parts = attention_forward(*qkv, w, mm_name)
            return o, (arrs, qkv, parts)

        def attended_bwd(res, do):
            arrs, qkv, parts = res
            return op_bwd(arrs, list(attention_backward(*qkv, parts, do, w, mm_name)))

        attended.defvjp(attended_fwd, attended_bwd)
        return attended(*rows, *segs, *params)

    def project(u, wb):
        if kind == "linear":
            return _mm(u, wb, "nn", mm_name + "_f")
        return tuple(_mm_split(u, wb, mm_name + "_f"))

    @jax.custom_vjp
    def fused(w, *arrs):
        return project(fwd_call(arrs)[0], w.astype(BF16))

    def fused_fwd(w, *arrs):
        u, wb = fwd_call(arrs)[0], w.astype(BF16)
        return project(u, wb), (arrs, u, wb)

    def fused_bwd(res, g):
        arrs, u, wb = res
        if kind == "linear":
            du, dw = _mm(g, wb, "nt", mm_name + "_dx"), _mm(u, g, "tn", mm_name + "_dw")
        else:
            du, dw = _mm_join(list(g), wb, mm_name + "_dx"), _mm_join_tn(u, list(g), mm_name + "_dw")
        return (dw,) + op_bwd(arrs, [du])

    fused.defvjp(fused_fwd, fused_bwd)
    return fused(w, *rows, *segs, *params)


ATT_SCALE = MLA_QK ** -0.5
LOG2E = math.log2(math.e)
ATT_KEY_CHUNKS = (768, 512, 256)


ATT_LATENT_TILE = 1024


def _query_rows_spec(row0, tq):
    return pl.BlockSpec((pl.Element(tq), pl.Element(HEAD_PAD)),
                        lambda h, i: (pl.multiple_of(row0 + i * tq, ROW_TILE), pl.multiple_of(h * HEAD_PAD, HEAD_PAD)))


def _key_chunks(nk):
    kc = _pick(nk, ATT_KEY_CHUNKS)
    return [(c * kc, kc) for c in range(nk // kc)]


def _attn_fwd_call(q, k, v, row0, n_rows, tq, nk, name):
    def body(q_ref, k_ref, v_ref, o_ref, lse_ref):
        qv = q_ref[...]
        m = jnp.full((tq, 1), -jnp.inf, F32)
        l = jnp.zeros((tq, 1), F32)
        acc = jnp.zeros((tq, HEAD_PAD), F32)
        for start, size in _key_chunks(nk):
            s = lax.dot_general(qv, k_ref[start:start + size, :], _DOT_DIMS["nt"], preferred_element_type=F32)
            m_new = jnp.maximum(m, jnp.max(s, axis=-1, keepdims=True))
            alpha = jnp.exp2(m - m_new)
            p = jnp.exp2(s - m_new)
            l = alpha * l + jnp.sum(p, axis=-1, keepdims=True)
            acc = alpha * acc + lax.dot_general(p.astype(BF16), v_ref[start:start + size, :], _DOT_DIMS["nn"],
                                                preferred_element_type=F32)
            m = m_new
        o_ref[...] = acc / l
        lse_ref[...] = jnp.broadcast_to(m + jnp.log2(l), (tq, HEAD_PAD))

    out_spec = pl.BlockSpec((tq, HEAD_PAD), lambda h, i: (i, h))
    kv_spec = pl.BlockSpec((nk, HEAD_PAD), lambda h, i: (0, h))
    out = jax.ShapeDtypeStruct((n_rows, q.shape[1]), F32)
    return pl.pallas_call(
        body, name=name, grid=(MLA_HEADS, n_rows // tq), in_specs=[_query_rows_spec(row0, tq), kv_spec, kv_spec],
        out_specs=[out_spec, out_spec], out_shape=[out, out], compiler_params=_cparams(2),
    )(q, k, v)


def _attn_bwd_call(q, k, v, o, lse, do, row0, n_rows, tq, nk, name):
    nq = n_rows // tq

    def body(q_ref, k_ref, v_ref, o_ref, lse_ref, do_ref, dq_ref, dk_ref, dv_ref):
        i = pl.program_id(1)

        @pl.when(i == 0)
        def _():
            dk_ref[...] = jnp.zeros_like(dk_ref)
            dv_ref[...] = jnp.zeros_like(dv_ref)

        qv = q_ref[...]
        dov = do_ref[...]
        dob = dov.astype(BF16)
        lse = lse_ref[:, 0:1]
        delta = jnp.sum(dov * o_ref[...], axis=-1, keepdims=True)
        dq = jnp.zeros((tq, HEAD_PAD), F32)
        for start, size in _key_chunks(nk):
            kk = k_ref[start:start + size, :]
            vv = v_ref[start:start + size, :]
            s = lax.dot_general(qv, kk, _DOT_DIMS["nt"], preferred_element_type=F32)
            p = jnp.exp2(s - lse)
            dp = lax.dot_general(dob, vv, _DOT_DIMS["nt"], preferred_element_type=F32)
            g = (p * (dp - delta)).astype(BF16)
            dk_ref[start:start + size, :] += lax.dot_general(g, qv, _DOT_DIMS["tn"], preferred_element_type=F32)
            dv_ref[start:start + size, :] += lax.dot_general(p.astype(BF16), dob, _DOT_DIMS["tn"],
                                                             preferred_element_type=F32)
            dq = dq + lax.dot_general(g, kk, _DOT_DIMS["nn"], preferred_element_type=F32)
        dq_ref[...] = dq * (1.0 / LOG2E)

        @pl.when(i == nq - 1)
        def _():
            dk_ref[...] = dk_ref[...] * (1.0 / LOG2E)

    own_spec = pl.BlockSpec((tq, HEAD_PAD), lambda h, i: (i, h))
    kv_spec = pl.BlockSpec((nk, HEAD_PAD), lambda h, i: (0, h))
    rows_spec = _query_rows_spec(row0, tq)
    return pl.pallas_call(
        body, name=name, grid=(MLA_HEADS, nq),
        in_specs=[rows_spec, kv_spec, kv_spec, own_spec, own_spec, rows_spec],
        out_specs=[own_spec, kv_spec, kv_spec],
        out_shape=[jax.ShapeDtypeStruct((n_rows, q.shape[1]), F32), jax.ShapeDtypeStruct((nk, q.shape[1]), F32),
                   jax.ShapeDtypeStruct((nk, q.shape[1]), F32)],
        compiler_params=_cparams(2),
    )(q, k, v, o, lse, do)


def _attn_ranges(r_total, tc):
    tq_lat = _pick(r_total - tc, (ATT_LATENT_TILE, ROW_TILE))
    return [(0, tc, ROW_TILE, tc, "_ctx"), (tc, r_total - tc, tq_lat, r_total, "_lat")]


def attention_forward(qs, k, v, tc, name):
    parts = [_attn_fwd_call(qs, k, v, row0, n_rows, tq, nk, name + tag + "_f")
             for row0, n_rows, tq, nk, tag in _attn_ranges(qs.shape[0], tc)]
    return jnp.concatenate([o for o, _ in parts], axis=0), parts


def attention_backward(qs, k, v, parts, do, tc, name):
    r_total = qs.shape[0]
    (dq_c, dk_c, dv_c), (dq_l, dk_l, dv_l) = [
        _attn_bwd_call(qs, k, v, o, lse, do, row0, n_rows, tq, nk, name + tag + "_b")
        for (o, lse), (row0, n_rows, tq, nk, tag) in zip(parts, _attn_ranges(r_total, tc))]
    grow = lambda part: jnp.pad(part, ((0, r_total - tc), (0, 0)))
    return jnp.concatenate([dq_c, dq_l], axis=0), dk_l + grow(dk_c), dv_l + grow(dv_c)


CHUNK_SHIFT = CHUNK.bit_length() - 1


def _block_pairs():
    rows = lax.broadcasted_iota(jnp.int32, (ROW_TILE, ROW_TILE), 0)
    cols = lax.broadcasted_iota(jnp.int32, (ROW_TILE, ROW_TILE), 1)
    same = lax.shift_right_logical(rows, CHUNK_SHIFT) == lax.shift_right_logical(cols, CHUNK_SHIFT)
    return rows, cols, same


def _block_mask(kind):
    rows, cols, same = _block_pairs()
    order = {"lower_incl": rows >= cols, "upper_incl": rows <= cols, "lower_strict": rows > cols,
             "upper_strict": rows < cols}[kind]
    return jnp.logical_and(same, order)


def _row_chunk():
    return lax.shift_right_logical(lax.broadcasted_iota(jnp.int32, (ROW_TILE, 1), 0), CHUNK_SHIFT)


def _dot01(kind, x):
    m = _block_mask(kind).astype(BF16)
    hi = x.astype(BF16)
    rest = x - hi.astype(F32)
    mid = rest.astype(BF16)
    lo = (rest - mid.astype(F32)).astype(BF16)
    terms = jnp.concatenate([hi, mid, lo], axis=1)
    out = lax.dot_general(m, terms, _DOT_DIMS["nn"], preferred_element_type=F32)
    n = x.shape[1]
    return out[:, :n] + out[:, n:2 * n] + out[:, 2 * n:]


def _chunk_sums(x, forward):
    kinds = ("lower_incl", "upper_strict") if forward else ("upper_incl", "lower_strict")
    transposed = ("upper_incl", "lower_strict") if forward else ("lower_incl", "upper_strict")

    @jax.custom_vjp
    def f(x):
        return _dot01(kinds[0], x), _dot01(kinds[1], x)

    def fwd(x):
        return (_dot01(kinds[0], x), _dot01(kinds[1], x)), None

    def bwd(_, g):
        return (_dot01(transposed[0], g[0]) + _dot01(transposed[1], g[1]),)

    f.defvjp(fwd, bwd)
    return f(x)


def _scan_order(forward):
    if forward:
        return list(range(SCAN_CHUNKS)), lambda c: c * CHUNK + CHUNK - 1
    return list(range(SCAN_CHUNKS - 1, -1, -1)), lambda c: c * CHUNK


def _carry_states(forward, st0, inc_all, decay_of):
    order, _ = _scan_order(forward)
    entering = [None] * SCAN_CHUNKS
    st = st0
    for c in order:
        entering[c] = st
        st = st * decay_of(c) + inc_all[:, c * HEAD_PAD:(c + 1) * HEAD_PAD]
    return jnp.concatenate(entering, axis=0), st


def _per_chunk_lanes(x):
    chunk = _row_chunk()
    return jnp.concatenate([jnp.where(chunk == c, x, 0.0) for c in range(SCAN_CHUNKS)], axis=1)


def _own_chunk_lanes(x4):
    chunk = _row_chunk()
    n = x4.shape[1] // SCAN_CHUNKS
    out = jnp.where(chunk == 0, x4[:, :n], 0.0)
    for c in range(1, SCAN_CHUNKS):
        out = out + jnp.where(chunk == c, x4[:, c * n:(c + 1) * n], 0.0)
    return out


def _gla_block(forward, q, k, v, la, st0):
    cum, after = _chunk_sums(la, forward)
    _, last_row = _scan_order(forward)
    q_dec = q * (jnp.exp(cum) * (GLA_DK ** -0.5))
    att = _bdot(q_dec, k * jnp.exp(-cum), "nt")
    att = jnp.where(_block_mask("lower_incl" if forward else "upper_strict"), att, 0.0)
    inc_all = _bdot(v, _per_chunk_lanes(k * jnp.exp(after)), "tn")
    entering, st1 = _carry_states(forward, st0, inc_all,
                                  lambda c: jnp.exp(cum[last_row(c):last_row(c) + 1, :]))
    o = _bdot(att, v, "nn") + _own_chunk_lanes(_bdot(q_dec, entering, "nt"))
    return o, st1


def _ret_block(forward, q, k, v, rd, st0):
    lg = -jnp.exp(rd[0:1, 0:1])
    rows, cols, _ = _block_pairs()
    pos = jnp.bitwise_and(lax.broadcasted_iota(jnp.int32, (ROW_TILE, 1), 0), CHUNK - 1).astype(F32)
    if forward:
        to_end, from_start, rel = CHUNK - 1.0 - pos, pos + 1.0, (rows - cols).astype(F32)
    else:
        to_end, from_start, rel = pos, CHUNK - pos, (cols - rows).astype(F32)
    mask = _block_mask("lower_incl" if forward else "upper_strict")
    dmat = jnp.where(mask, jnp.exp(jnp.where(mask, rel, 0.0) * lg), 0.0)
    att = _bdot(q, k, "nt") * dmat
    inc_all = _bdot(v, _per_chunk_lanes(k * jnp.exp(to_end * lg)), "tn")
    entering, st1 = _carry_states(forward, st0, inc_all, lambda c: jnp.exp(CHUNK * lg))
    o = _bdot(att, v, "nn") + _own_chunk_lanes(_bdot(q, entering, "nt")) * jnp.exp(from_start * lg)
    return o, st1


def scan(kind, forward, q, k, v, aux, tc, name, rot=None):
    heads = q.shape[1] // HEAD_PAD
    r_total = q.shape[0]
    nblk = r_total // ROW_TILE
    nctb = tc // ROW_TILE
    block_fn = functools.partial(_gla_block if kind == "gla" else _ret_block, forward)
    per_row_aux = kind == "gla"

    def blk(g):
        if forward:
            return g
        return jnp.where(g < nctb, nctb - 1 - g, nblk - 1 - (g - nctb))

    def specs(step_to_g):
        row = pl.BlockSpec((ROW_TILE, heads * HEAD_PAD), lambda s: (blk(step_to_g(s)), 0))
        aux_spec = row if per_row_aux else pl.BlockSpec((heads, 8, HEAD_PAD), lambda s: (0, 0, 0))
        st = pl.BlockSpec((1, heads, HEAD_PAD, HEAD_PAD), lambda s: (step_to_g(s), 0, 0, 0))
        return row, aux_spec, st

    def head_cols(h):
        return slice(h * HEAD_PAD, (h + 1) * HEAD_PAD)

    n_rot = 0 if rot is None else 2
    rot_arrays = [] if rot is None else list(rot)

    def rot_specs(step_to_g):
        return [pl.BlockSpec((ROW_TILE, HEAD_PAD), lambda s: (blk(step_to_g(s)), 0))] * n_rot

    def head_fn(rot_refs):
        if not rot_refs:
            return block_fn
        rc, rs = rot_refs[0][...], rot_refs[1][...]

        def turned(x, scale):
            return (x * rc + _roll(x, HEAD_PAD // 2, 1) * rs) * scale

        return lambda q, k, v, a, st0: block_fn(turned(q, 1.0), turned(k, RET_DK ** -0.5), v, a, st0)

    def fwd_call(q, k, v, aux):
        row, aux_spec, st_spec = specs(lambda s: s)

        def body(*refs):
            q_ref, k_ref, v_ref, a_ref = refs[:4]
            o_ref, st0_ref, st_ref = refs[4 + n_rot:]
            fn = head_fn(refs[4:4 + n_rot])

            @pl.when(pl.program_id(0) == 0)
            def _():
                st_ref[...] = jnp.zeros_like(st_ref)

            qv, kv, vv = q_ref[...], k_ref[...], v_ref[...]
            outs = []
            for h in range(heads):
                st0 = st_ref[h]
                st0_ref[0, h] = st0
                a = a_ref[:, head_cols(h)] if per_row_aux else a_ref[h]
                o, st1 = fn(qv[:, head_cols(h)], kv[:, head_cols(h)], vv[:, head_cols(h)], a, st0)
                outs.append(o)
                st_ref[h] = st1
            o_ref[...] = jnp.concatenate(outs, axis=1)

        return pl.pallas_call(
            body, name=name + "_f", grid=(nblk,), in_specs=[row, row, row, aux_spec] + rot_specs(lambda s: s),
            out_specs=[row, st_spec],
            out_shape=[jax.ShapeDtypeStruct(q.shape, F32),
                       jax.ShapeDtypeStruct((nblk, heads, HEAD_PAD, HEAD_PAD), F32)],
            scratch_shapes=[pltpu.VMEM((heads, HEAD_PAD, HEAD_PAD), F32)],
            compiler_params=_cparams(1),
        )(q, k, v, aux, *rot_arrays)

    def bwd_call(q, k, v, aux, st0s, do):
        row, aux_spec, st_spec = specs(lambda s: nblk - 1 - s)

        def body(*refs):
            q_ref, k_ref, v_ref, a_ref = refs[:4]
            st0_ref, do_ref, dq_ref, dk_ref, dv_ref, da_ref, dst_ref = refs[4 + n_rot:]
            fn = head_fn(refs[4:4 + n_rot])
            s = pl.program_id(0)

            @pl.when(s == 0)
            def _():
                dst_ref[...] = jnp.zeros_like(dst_ref)

            qv, kv, vv, dov = q_ref[...], k_ref[...], v_ref[...], do_ref[...]
            grads = []
            for h in range(heads):
                a = a_ref[:, head_cols(h)] if per_row_aux else a_ref[h]
                _, vjp = jax.vjp(fn, qv[:, head_cols(h)], kv[:, head_cols(h)], vv[:, head_cols(h)], a,
                                 st0_ref[0, h])
                dq, dk, dv, da, dst0 = vjp((dov[:, head_cols(h)], dst_ref[h]))
                dst_ref[h] = dst0
                grads.append((dq, dk, dv, da))
            dq_ref[...] = jnp.concatenate([g[0] for g in grads], axis=1)
            dk_ref[...] = jnp.concatenate([g[1] for g in grads], axis=1)
            dv_ref[...] = jnp.concatenate([g[2] for g in grads], axis=1)
            if per_row_aux:
                da_ref[...] = jnp.concatenate([g[3] for g in grads], axis=1)
            else:
                da = jnp.stack([g[3] for g in grads], axis=0)

                @pl.when(s == 0)
                def _():
                    da_ref[...] = da

                @pl.when(s != 0)
                def _():
                    da_ref[...] += da

        return pl.pallas_call(
            body, name=name + "_b", grid=(nblk,),
            in_specs=[row, row, row, aux_spec] + rot_specs(lambda s: nblk - 1 - s) + [st_spec, row],
            out_specs=[row, row, row, aux_spec],
            out_shape=[jax.ShapeDtypeStruct(q.shape, F32)] * 3 + [jax.ShapeDtypeStruct(aux.shape, F32)],
            scratch_shapes=[pltpu.VMEM((heads, HEAD_PAD, HEAD_PAD), F32)],
            compiler_params=_cparams(1),
        )(q, k, v, aux, *rot_arrays, st0s, do)

    @jax.custom_vjp
    def op(q, k, v, aux):
        return fwd_call(q, k, v, aux)[0]

    def fwd(q, k, v, aux):
        o, st0s = fwd_call(q, k, v, aux)
        return o, (q, k, v, aux, st0s)

    def bwd(res, do):
        return tuple(bwd_call(*res, do))

    op.defvjp(fwd, bwd)
    return op(q, k, v, aux)


HALO = 8


def _neighbours(main, prev8, next8, i, nct, n_tiles):
    has_prev = jnp.logical_and(i != 0, i != nct).astype(F32)
    has_next = jnp.logical_and(i != nct - 1, i != n_tiles - 1).astype(F32)
    row = lax.broadcasted_iota(jnp.int32, main.shape, 0)
    down = jnp.where(row == 0, prev8[HALO - 1:HALO] * has_prev, pltpu.roll(main, 1, 0))
    up = jnp.where(row == ROW_TILE - 1, next8[0:1] * has_next, pltpu.roll(main, ROW_TILE - 1, 0))
    return down, up


GELU_K = math.sqrt(2.0 / math.pi)
GELU_A = 0.044715


def _gelu_tanh_grad(x):
    t = jnp.tanh(GELU_K * (x + GELU_A * (x * x * x)))
    return 0.5 * x * (1.0 + t), 0.5 * (1.0 + t) + 0.5 * x * (1.0 - t * t) * (GELU_K * (1.0 + 3.0 * GELU_A * (x * x)))


def conv_ffn_out(gate, up, w8, b, w_out, tc, name):
    r_total, width = gate.shape
    n_tiles = r_total // ROW_TILE
    nct = tc // ROW_TILE
    per = ROW_TILE // HALO
    main_spec = pl.BlockSpec((ROW_TILE, width), lambda i: (i, 0))
    prev_spec = pl.BlockSpec((HALO, width), lambda i: (jnp.maximum(i * per - 1, 0), 0))
    next_spec = pl.BlockSpec((HALO, width), lambda i: (jnp.minimum((i + 1) * per, r_total // HALO - 1), 0))
    w_spec = pl.BlockSpec((8, width), lambda i: (0, 0))
    b_spec = pl.BlockSpec((1, width), lambda i: (0, 0))
    halo3 = [main_spec, prev_spec, next_spec]

    def conv(w_ref, b_ref, down, mid, upn):
        return w_ref[0:1] * down + w_ref[1:2] * mid + w_ref[2:3] * upn + b_ref[...]

    def fwd_call(gate, up, w8, b):
        def body(g_ref, gp_ref, gn_ref, up_ref, w_ref, b_ref, o_ref):
            gv = g_ref[...]
            down, upn = _neighbours(gv, gp_ref[...], gn_ref[...], pl.program_id(0), nct, n_tiles)
            o_ref[...] = (_gelu_tanh(conv(w_ref, b_ref, down, gv, upn)) * up_ref[...]).astype(BF16)

        return pl.pallas_call(
            body, name=name + "_f", grid=(n_tiles,), in_specs=halo3 + [main_spec, w_spec, b_spec],
            out_specs=main_spec, out_shape=jax.ShapeDtypeStruct(gate.shape, BF16), compiler_params=_cparams(1),
        )(gate, gate, gate, up, w8, b)

    def bwd_call(gate, up, w8, b, du):
        def body(g_ref, gp_ref, gn_ref, up_ref, upp_ref, upn_ref, du_ref, dup_ref, dun_ref, w_ref, b_ref,
                 dg_ref, dupo_ref, dw_ref, db_ref):
            i = pl.program_id(0)
            has_prev = jnp.logical_and(i != 0, i != nct).astype(F32)
            has_next = jnp.logical_and(i != nct - 1, i != n_tiles - 1).astype(F32)
            gv, gp, gn = g_ref[...], gp_ref[...], gn_ref[...]
            g_down, g_up = _neighbours(gv, gp, gn, i, nct, n_tiles)
            act, slope = _gelu_tanh_grad(conv(w_ref, b_ref, g_down, gv, g_up))
            duv = du_ref[...]
            dupo_ref[...] = duv * act
            dc = duv * up_ref[...] * slope
            c_above = conv(w_ref, b_ref, gp[HALO - 2:HALO - 1], gp[HALO - 1:HALO], gv[0:1])
            c_below = conv(w_ref, b_ref, gv[ROW_TILE - 1:ROW_TILE], gn[0:1], gn[1:2])
            dc_above = dup_ref[HALO - 1:HALO] * upp_ref[HALO - 1:HALO] * _gelu_tanh_grad(c_above)[1] * has_prev
            dc_below = dun_ref[0:1] * upn_ref[0:1] * _gelu_tanh_grad(c_below)[1] * has_next
            row = lax.broadcasted_iota(jnp.int32, dc.shape, 0)
            dc_down = jnp.where(row == 0, dc_above, pltpu.roll(dc, 1, 0))
            dc_up = jnp.where(row == ROW_TILE - 1, dc_below, pltpu.roll(dc, ROW_TILE - 1, 0))
            dg_ref[...] = w_ref[0:1] * dc_up + w_ref[1:2] * dc + w_ref[2:3] * dc_down
            dw = jnp.concatenate([jnp.sum(dc * g_down, axis=0, keepdims=True),
                                  jnp.sum(dc * gv, axis=0, keepdims=True),
                                  jnp.sum(dc * g_up, axis=0, keepdims=True),
                                  jnp.zeros((5, width), F32)], axis=0)
            db = jnp.sum(dc, axis=0, keepdims=True)

            @pl.when(i == 0)
            def _():
                dw_ref[...] = dw
                db_ref[...] = db

            @pl.when(i != 0)
            def _():
                dw_ref[...] += dw
                db_ref[...] += db

        return pl.pallas_call(
            body, name=name + "_b", grid=(n_tiles,), in_specs=halo3 * 3 + [w_spec, b_spec],
            out_specs=[main_spec, main_spec, w_spec, b_spec],
            out_shape=[jax.ShapeDtypeStruct(gate.shape, F32), jax.ShapeDtypeStruct(gate.shape, F32),
                       jax.ShapeDtypeStruct((8, width), F32), jax.ShapeDtypeStruct((1, width), F32)],
            compiler_params=_cparams(1),
        )(gate, gate, gate, up, up, up, du, du, du, w8, b)

    @jax.custom_vjp
    def op(gate, up, w8, b, w_out):
        return _mm(fwd_call(gate, up, w8, b), w_out.astype(BF16), "nn", name + "_out_f")

    def fwd(gate, up, w8, b, w_out):
        u, wb = fwd_call(gate, up, w8, b), w_out.astype(BF16)
        return _mm(u, wb, "nn", name + "_out_f"), (gate, up, w8, b, u, wb)

    def bwd(res, g):
        gate, up, w8, b, u, wb = res
        du = _mm(g, wb, "nt", name + "_out_dx")
        d_gate, d_up, d_w8, d_b = bwd_call(gate, up, w8, b, du)
        return d_gate, d_up, d_w8, d_b, _mm(u, g, "tn", name + "_out_dw")

    op.defvjp(fwd, bwd)
    return op(gate, up, w8, b, w_out)


def loss_head(h, target, tc, name):
    r_total, width = h.shape
    n_tiles = r_total // ROW_TILE
    nct = tc // ROW_TILE

    def call(h, target):
        def body(h_ref, t_ref, dh_ref, loss_ref, acc_ref):
            i = pl.program_id(0)

            @pl.when(i == 0)
            def _():
                acc_ref[...] = jnp.zeros_like(acc_ref)

            @pl.when(i < nct)
            def _():
                dh_ref[...] = jnp.zeros_like(dh_ref)

            @pl.when(i >= nct)
            def _():
                err = h_ref[...] - t_ref[...]
                dh_ref[...] = err * (1.0 / width)
                acc_ref[...] += jnp.sum((err * err).reshape(ROW_TILE // 8, 8, width), axis=0)

            @pl.when(i == n_tiles - 1)
            def _():
                loss_ref[...] = jnp.sum(acc_ref[...]).reshape(1, 1) * (0.5 / width)

        row = pl.BlockSpec((ROW_TILE, width), lambda i: (i, 0))
        return pl.pallas_call(
            body, name=name, grid=(n_tiles,),
            in_specs=[row, pl.BlockSpec((ROW_TILE, width), lambda i: (jnp.maximum(i - nct, 0), 0))],
            out_specs=[row, pl.BlockSpec((1, 1), lambda i: (0, 0))],
            out_shape=[jax.ShapeDtypeStruct(h.shape, F32), jax.ShapeDtypeStruct((1, 1), F32)],
            scratch_shapes=[pltpu.VMEM((8, width), F32)], compiler_params=_cparams(1),
        )(h, target)

    @jax.custom_vjp
    def op(h, target):
        return call(h, target)[1][0, 0]

    def fwd(h, target):
        dh, loss = call(h, target)
        return loss[0, 0], (dh, target)

    def bwd(res, g):
        dh, target = res
        return dh * g, jnp.zeros_like(target)

    op.defvjp(fwd, bwd)
    return op(h, target)


PACK_W = 1024
PACK_TILE = 128


def slab_sum(slabs, name):
    n_slab, n, _ = slabs.shape

    def body(s_ref, o_ref):
        acc = s_ref[0]
        for j in range(1, n_slab):
            acc = acc + s_ref[j]
        o_ref[...] = acc

    return pl.pallas_call(
        body, name=name, grid=(n // PACK_TILE,),
        in_specs=[pl.BlockSpec((n_slab, PACK_TILE, PACK_W), lambda i: (0, i, 0))],
        out_specs=pl.BlockSpec((PACK_TILE, PACK_W), lambda i: (i, 0)),
        out_shape=jax.ShapeDtypeStruct((n, PACK_W), F32), compiler_params=_cparams(1),
    )(slabs)


def adamw(g_slabs, w, m, v, name):
    n_slab, n, _ = g_slabs.shape

    def body(g_ref, w_ref, m_ref, v_ref, go_ref, d_ref, mo_ref, vo_ref):
        g = g_ref[0].astype(F32)
        for j in range(1, n_slab):
            g = g + g_ref[j].astype(F32)
        m_new = ADAM_B1 * m_ref[...] + (1.0 - ADAM_B1) * g
        v_new = ADAM_B2 * v_ref[...] + (1.0 - ADAM_B2) * (g * g)
        m_hat = m_new / (1.0 - ADAM_B1 ** ADAM_STEP)
        v_hat = v_new / (1.0 - ADAM_B2 ** ADAM_STEP)
        go_ref[...] = g
        d_ref[...] = -ADAM_LR * (m_hat / (jnp.sqrt(v_hat) + ADAM_EPS) + ADAM_WD * w_ref[...])
        mo_ref[...] = m_new
        vo_ref[...] = v_new

    flat = pl.BlockSpec((PACK_TILE, PACK_W), lambda i: (i, 0))
    return pl.pallas_call(
        body, name=name, grid=(n // PACK_TILE,),
        in_specs=[pl.BlockSpec((n_slab, PACK_TILE, PACK_W), lambda i: (0, i, 0)), flat, flat, flat],
        out_specs=[flat] * 4, out_shape=[jax.ShapeDtypeStruct((n, PACK_W), F32)] * 4, compiler_params=_cparams(1),
    )(g_slabs, w, m, v)


def all_gather(x, name):
    m_per, n = x.shape

    def body(x_ref, out_ref, send_sems, recv_sems, local_sem):
        px, py, pc = lax.axis_index("x"), lax.axis_index("y"), lax.axis_index("c")
        me, sibling = (px, py, pc), (px, py, 1 - pc)
        chips = [(1 - px, py), (px, 1 - py), (1 - px, 1 - py)]

        def rows(bx, by, bc):
            return out_ref.at[pl.ds((4 * bx + 2 * by + bc) * m_per, m_per), :]

        def copy(k, block, to, src=None):
            return pltpu.make_async_remote_copy(
                src_ref=rows(*block) if src is None else src, dst_ref=rows(*block),
                send_sem=send_sems.at[k], recv_sem=recv_sems.at[k], device_id=to, device_id_type=MESH)

        mine = pltpu.make_async_copy(x_ref, rows(*me), local_sem)
        mine.start()
        first = [copy(0, me, sibling, src=x_ref)]
        first += [copy(1 + j, me, (*chip, pc), src=x_ref) for j, chip in enumerate(chips)]
        for cp in first:
            cp.start()
        passed = [copy(4 + j, (*chip, pc), sibling) for j, chip in enumerate(chips)]
        for j, chip in enumerate(chips):
            copy(1 + j, (*chip, pc), me).wait_recv()
            passed[j].start()
        copy(0, sibling, me).wait_recv()
        for j, chip in enumerate(chips):
            copy(4 + j, (*chip, 1 - pc), me).wait_recv()
        for cp in first + passed:
            cp.wait_send()
        mine.wait()

    return pl.pallas_call(
        body, name=name, out_shape=jax.ShapeDtypeStruct((N_DEV * m_per, n), x.dtype),
        in_specs=[pl.BlockSpec(memory_space=pl.ANY)], out_specs=pl.BlockSpec(memory_space=pl.ANY),
        scratch_shapes=[pltpu.SemaphoreType.DMA((7,)), pltpu.SemaphoreType.DMA((7,)), pltpu.SemaphoreType.DMA],
    )(x)


N_CHIP = 4


def pair_swap(x, name):
    def body(x_ref, out_ref, send_sem, recv_sem):
        sibling = (lax.axis_index("x"), lax.axis_index("y"), 1 - lax.axis_index("c"))
        copy = pltpu.make_async_remote_copy(src_ref=x_ref, dst_ref=out_ref, send_sem=send_sem, recv_sem=recv_sem,
                                            device_id=sibling, device_id_type=MESH)
        copy.start()
        copy.wait()

    return pl.pallas_call(
        body, name=name, out_shape=jax.ShapeDtypeStruct(x.shape, x.dtype),
        in_specs=[pl.BlockSpec(memory_space=pl.ANY)], out_specs=pl.BlockSpec(memory_space=pl.ANY),
        scratch_shapes=[pltpu.SemaphoreType.DMA, pltpu.SemaphoreType.DMA],
    )(x)


def pair_add(a, b, name):
    n_slab, n, _ = a.shape

    def body(a_ref, b_ref, o_ref):
        o_ref[...] = (a_ref[...].astype(F32) + b_ref[...].astype(F32)).astype(o_ref.dtype)

    spec = pl.BlockSpec((1, PACK_TILE, PACK_W), lambda s, i: (s, i, 0))
    return pl.pallas_call(
        body, name=name, grid=(n_slab, n // PACK_TILE), in_specs=[spec, spec], out_specs=spec,
        out_shape=jax.ShapeDtypeStruct(a.shape, a.dtype), compiler_params=_cparams(2),
    )(a, b)


def chip_all_to_all(x, name):
    def body(x_ref, out_ref, send_sems, recv_sems, local_sem):
        px, py, pc = lax.axis_index("x"), lax.axis_index("y"), lax.axis_index("c")
        mine_idx = 2 * px + py
        local = pltpu.make_async_copy(x_ref.at[mine_idx], out_ref.at[mine_idx], local_sem)
        local.start()
        copies = []
        for k, (fx, fy) in enumerate(((0, 1), (1, 0), (1, 1))):
            qx, qy = px ^ fx, py ^ fy
            peer_idx = 2 * qx + qy
            copies.append((
                pltpu.make_async_remote_copy(
                    src_ref=x_ref.at[peer_idx], dst_ref=out_ref.at[mine_idx], send_sem=send_sems.at[k],
                    recv_sem=recv_sems.at[k], device_id=(qx, qy, pc), device_id_type=MESH),
                pltpu.make_async_remote_copy(
                    src_ref=x_ref.at[peer_idx], dst_ref=out_ref.at[peer_idx], send_sem=send_sems.at[k],
                    recv_sem=recv_sems.at[k], device_id=(qx, qy, pc), device_id_type=MESH)))
        for send, _ in copies:
            send.start()
        for _, landing in copies:
            landing.wait_recv()
        for send, _ in copies:
            send.wait_send()
        local.wait()

    return pl.pallas_call(
        body, name=name, out_shape=jax.ShapeDtypeStruct(x.shape, x.dtype),
        in_specs=[pl.BlockSpec(memory_space=pl.ANY)], out_specs=pl.BlockSpec(memory_space=pl.ANY),
        scratch_shapes=[pltpu.SemaphoreType.DMA((3,)), pltpu.SemaphoreType.DMA((3,)), pltpu.SemaphoreType.DMA],
    )(x)


IN_OFFSETS = {}
_off = 0
for _name, _width in (("mla_q", 256), ("mla_kv", 128), ("mla_kr", 32), ("gla_q", 512), ("gla_k", 512), ("gla_v", 512),
                      ("gla_g", 512), ("gla_rf", 16), ("gla_rb", 16), ("ret_q", 512), ("ret_k", 512), ("ret_v", 512),
                      ("ret_g", 512), ("gate_mla", 1024), ("gate_gla", 1024), ("gate_ret", 1024)):
    IN_OFFSETS[_name] = (_off, _off + _width)
    _off += _width
N_IN = _off

P_GLA, P_RET, P_GATE, P_MLAQ, P_MLAKV, P_MLAKR, P_RANK, P_END = 0, 2048, 4096, 7168, 7424, 7552, 7680, 7808


def _pad_in_proj(w):
    def cols(a, b):
        return w[:, IN_OFFSETS[a][0]:IN_OFFSETS[b][1]]

    def z(n):
        return jnp.zeros((w.shape[0], n), w.dtype)

    return jnp.concatenate([cols("gla_q", "gla_g"), cols("ret_q", "ret_g"), cols("gate_mla", "gate_ret"),
                            cols("mla_q", "mla_kv"), z(MLA_NOPE), cols("mla_kr", "mla_kr"),
                            z(HEAD_PAD - MLA_QK), cols("gla_rf", "gla_rb"), z(HEAD_PAD - 2 * GLA_RANK),
                            z(N_IN_PAD - P_END)], axis=1)


def _pad_last(a, n):
    return jnp.pad(a, [(0, 0)] * (a.ndim - 1) + [(0, n - a.shape[-1])])


def _position_tables(tc, t):
    pos = jnp.arange(t)
    inv = ROPE_THETA ** (-jnp.arange(MLA_ROPE // 4, dtype=F32) * 2.0 / (MLA_ROPE // 2))
    ang_r = (pos // GRID_W).astype(F32)[:, None] * inv[None, :]
    ang_c = (pos % GRID_W).astype(F32)[:, None] * inv[None, :]
    z8, z32, z64 = jnp.zeros((t, 8), F32), jnp.zeros((t, 32), F32), jnp.zeros((t, 64), F32)
    lat_c = jnp.concatenate([jnp.ones((t, 64), F32), jnp.cos(ang_r), jnp.cos(ang_r), jnp.cos(ang_c), jnp.cos(ang_c),
                             z32], axis=1)
    lat_sn = jnp.concatenate([z64, -jnp.sin(ang_r), z8, -jnp.sin(ang_c), z8, z32], axis=1)
    lat_sp = jnp.concatenate([z64, z8, jnp.sin(ang_r), z8, jnp.sin(ang_c), z32], axis=1)
    ctx_c = jnp.concatenate([jnp.ones((tc, MLA_QK), F32), jnp.zeros((tc, HEAD_PAD - MLA_QK), F32)], axis=1)
    ctx_z = jnp.zeros((tc, HEAD_PAD), F32)
    rinv = 1.0 / (RET_THETA ** jnp.linspace(0.0, 1.0, RET_DK // 2, dtype=F32))
    rang = jnp.arange(tc + t).astype(F32)[:, None] * rinv[None, :]
    return dict(c=jnp.concatenate([ctx_c, lat_c]), sn=jnp.concatenate([ctx_z, lat_sn]),
                sp=jnp.concatenate([ctx_z, lat_sp]),
                rc=jnp.concatenate([jnp.cos(rang), jnp.cos(rang)], axis=1),
                rs=jnp.concatenate([-jnp.sin(rang), jnp.sin(rang)], axis=1))


def _heads(x):
    return [x[:, h * HEAD_PAD:(h + 1) * HEAD_PAD] for h in range(x.shape[1] // HEAD_PAD)]


def _mla_rope(x, c, sn, sp):
    return x * c + _roll(x, HEAD_PAD - 8, 1) * sn + _roll(x, 8, 1) * sp


def _norm_mod_fn(shift_row, scale_row):
    def fn(h, mod, w):
        return (_rms(h, D, w) * (1.0 + mod[scale_row:scale_row + 1]) + mod[shift_row:shift_row + 1],)
    return fn


def _resid_fn(gate_row):
    def fn(h, y, mod):
        return (h + mod[gate_row:gate_row + 1] * y,)
    return fn


def _resid_norm_fn(gate_row, shift_row, scale_row):
    def fn(h, y, mod, w):
        h1 = h + mod[gate_row:gate_row + 1] * y
        return h1, _norm_mod_fn(shift_row, scale_row)(h1, mod, w)[0]
    return fn


def _mla_prep_fn(x, c, sn, sp, q_norm_a, w_qb, q_norm, kv_norm_a, w_k, w_v, k_norm):
    cq, ckv = x[:, :MLA_Q_LORA], x[:, MLA_Q_LORA:MLA_Q_LORA + MLA_KV_LORA]
    kr = x[:, MLA_Q_LORA + MLA_KV_LORA:]
    qf = _bdot(_rms(cq, MLA_Q_LORA, q_norm_a), w_qb, "nn")
    q = jnp.concatenate([_mla_rope(_rms(qh, MLA_QK, q_norm), c, sn, sp) for qh in _heads(qf)], axis=1)
    xkv = _rms(ckv, MLA_KV_LORA, kv_norm_a)
    kf = _bdot(xkv, w_k, "nn")
    k = jnp.concatenate([_mla_rope(_rms(kh + kr, MLA_QK, k_norm), c, sn, sp) for kh in _heads(kf)], axis=1)
    return q * (ATT_SCALE * LOG2E), k, _bdot(xkv, w_v, "nn")


def _decay_fn(x, w2, b):
    la = _log_sigmoid(_bdot(x[:, :HEAD_PAD], w2, "nn") + b) * (1.0 / GLA_NORMALIZER)
    return la[:, :GLA_HEADS * GLA_DK], la[:, GLA_HEADS * GLA_DK:]


def _gla_out_fn(o_f, o_b, g, w):
    y = jnp.concatenate([_rms(oh, HEAD_PAD, w) for oh in _heads(o_f + o_b)], axis=1)
    return (y * _silu(g),)


def _ret_out_fn(o_f, o_b, g):
    y = jnp.concatenate([_rms(oh, HEAD_PAD) for oh in _heads(o_f + o_b)], axis=1)
    return (y * _silu(g),)


def _merge_fn(z0, z1, z2, g0a, g0b, g1a, g1b, g2a, g2b, bg):
    out = 0.0
    for n, (z, ga, gb) in enumerate(((z0, g0a, g0b), (z1, g1a, g1b), (z2, g2a, g2b))):
        out = out + jax.nn.sigmoid(jnp.concatenate([ga, gb], axis=1) + bg[n:n + 1]) * z
    return (out,)


def _layer(l, h, mod, w, tabs, tc):
    nct = tc // ROW_TILE
    tag = f"_l{l}"
    row = lambda a: a[l][None]
    pieces = rowwise("norm1" + tag, _norm_mod_fn(0, 1), [h], [mod], [row(w["norm1_w"])], [D], nct,
                     then=("pieces", _pad_in_proj(w["w_in"][l]), "in_proj" + tag))
    piece = lambda start: pieces[start // PIECE_W]

    w_qb = _pad_last(w["mla_w_qb"][l].reshape(MLA_Q_LORA, MLA_HEADS, MLA_QK), HEAD_PAD).reshape(MLA_Q_LORA, -1)
    w_kvb = w["mla_w_kvb"][l].reshape(MLA_KV_LORA, MLA_HEADS, MLA_NOPE + MLA_V)
    w_k = _pad_last(w_kvb[:, :, :MLA_NOPE], HEAD_PAD).reshape(MLA_KV_LORA, -1)
    w_v = _pad_last(w_kvb[:, :, MLA_NOPE:], HEAD_PAD).reshape(MLA_KV_LORA, -1)
    rope = [tabs["c"], tabs["sn"], tabs["sp"]]
    y_mla = rowwise("mla_prep" + tag, _mla_prep_fn, [piece(P_MLAQ)] + rope, [],
                    [row(w["mla_q_norm_a"]), w_qb, _pad_last(row(w["mla_q_norm"]), HEAD_PAD),
                     row(w["mla_kv_norm_a"]), w_k, w_v, _pad_last(row(w["mla_k_norm"]), HEAD_PAD)],
                    [MLA_HEADS * HEAD_PAD] * 3, nct, diff_rows=[True, False, False, False],
                    then=("attention", tc, "attn" + tag))
    wb_mla = _pad_last(w["w_branch"][l, 0].reshape(MLA_HEADS, MLA_V, D).transpose(0, 2, 1), HEAD_PAD)
    wb_mla = wb_mla.transpose(0, 2, 1).reshape(MLA_HEADS * HEAD_PAD, D)

    w2 = jnp.zeros((HEAD_PAD, 2 * GLA_HEADS * GLA_DK), F32)
    w2 = w2.at[:GLA_RANK, :GLA_HEADS * GLA_DK].set(w["gla_w_gk2"][l, 0])
    w2 = w2.at[GLA_RANK:2 * GLA_RANK, GLA_HEADS * GLA_DK:].set(w["gla_w_gk2"][l, 1])
    la_f, la_b = rowwise("gla_decay" + tag, _decay_fn, [piece(P_RANK)], [],
                         [w2, w["gla_b_gk"][l].reshape(1, -1)], [GLA_HEADS * GLA_DK] * 2, nct)
    gq, gk, gv, gg = [piece(P_GLA + n * PIECE_W) for n in range(4)]
    o_f = scan("gla", True, gq, gk, gv, la_f, tc, "gla_fw" + tag)
    o_b = scan("gla", False, gq, gk, gv, la_b, tc, "gla_bw" + tag)
    z_gla = rowwise("gla_out" + tag, _gla_out_fn, [o_f, o_b, gg], [], [row(w["gla_o_norm"])], [512], nct,
                    then=("linear", w["w_branch"][l, 1], "branch_gla" + tag))

    rq, rk, rv, rg = [piece(P_RET + n * PIECE_W) for n in range(4)]
    rd = jnp.broadcast_to(w["ret_decay"][l][:, :, None, None], (2, RET_HEADS, 8, HEAD_PAD))
    turn = (tabs["rc"], tabs["rs"])
    r_f = scan("ret", True, rq, rk, rv, rd[0], tc, "ret_fw" + tag, rot=turn)
    r_b = scan("ret", False, rq, rk, rv, rd[1], tc, "ret_bw" + tag, rot=turn)
    z_ret = rowwise("ret_out" + tag, _ret_out_fn, [r_f, r_b, rg], [], [], [512], nct,
                    then=("linear", w["w_branch"][l, 2], "branch_ret" + tag))

    z = [linear(y_mla, wb_mla, "branch_mla" + tag), z_gla, z_ret]
    gates = [piece(P_GATE + n * PIECE_W) for n in range(6)]
    y = rowwise("merge" + tag, _merge_fn, z + gates, [], [_pad_rows(w["b_gate"][l], 8)], [D], nct,
                then=("linear", w["w_out"][l], "w_out" + tag))
    h, a2 = rowwise("resid1_norm2" + tag, _resid_norm_fn(2, 3, 4), [h, y], [mod], [row(w["norm2_w"])], [D, D], nct)
    gate = linear(a2, w["w_ffn_in"][l][:, :D_FF], "ffn_gate" + tag)
    up = linear(a2, w["w_ffn_in"][l][:, D_FF:], "ffn_up" + tag)
    f = conv_ffn_out(gate, up, _pad_rows(w["w_dw"][l], 8), row(w["b_dw"]), w["w_ffn_out"][l], tc, "ffn_mid" + tag)
    return rowwise("resid2" + tag, _resid_fn(5), [h, f], [mod], [], [D], nct)[0]


def _pad_rows(a, n):
    return jnp.pad(a, [(0, n - a.shape[0])] + [(0, 0)] * (a.ndim - 1))


def local_loss(w, mod, x, ctx, target):
    tc, t = ctx.shape[0], x.shape[0]
    tabs = _position_tables(tc, t)
    h = jnp.concatenate([ctx, x], axis=0)
    for l in range(DEPTH):
        h = _layer(l, h, mod[l], w, tabs, tc)
    return loss_head(h, target, tc, "loss_head")


ADA_ROWS = 16


def ada_forward(cond_in, w_ada, b_loc):
    cols = w_ada.shape[2]

    def body(x_ref, w_ref, b_ref, o_ref):
        s = _silu(x_ref[...])
        for l in range(DEPTH):
            o_ref[l] = _dg(s, w_ref[l], "nn") + b_ref[l]

    return pl.pallas_call(
        body, name="ada_forward", out_shape=jax.ShapeDtypeStruct((DEPTH, ADA_ROWS, cols), F32),
        compiler_params=pltpu.CompilerParams(vmem_limit_bytes=VMEM_LIMIT_BYTES),
    )(cond_in, w_ada, b_loc)


def ada_backward(cond_in, g_loc, dmod_own, w_ada):
    cols = w_ada.shape[2]

    def body(x_ref, g_ref, own_ref, w_ref, gw_ref, dc_ref, gb_ref):
        x = x_ref[...]
        s = _silu(x)
        dcond = jnp.zeros((8, D), F32)
        for l in range(DEPTH):
            g_ctx = jnp.sum(g_ref[2 * l], axis=0, keepdims=True)
            g_rows = jnp.concatenate([g_ref[2 * l + 1], jnp.broadcast_to(g_ctx, (8, cols))], axis=0)
            keep = lax.broadcasted_iota(jnp.int32, (ADA_ROWS, cols), 0) <= N_DEV
            gw_ref[l] = _dg(s, jnp.where(keep, g_rows, 0.0), "tn")
            dcond = dcond + _dg(jnp.broadcast_to(g_ctx, (8, cols)), w_ref[l], "nt")
            gb_ref[l:l + 1, :] = own_ref[2 * l:2 * l + 1, :] + own_ref[2 * l + 1:2 * l + 2, :]
        xc = x[N_DEV:N_DEV + 1]
        sig = jax.nn.sigmoid(xc)
        dc_ref[...] = dcond[0:1] * (sig * (1.0 + xc * (1.0 - sig)))

    return pl.pallas_call(
        body, name="ada_backward",
        out_shape=[jax.ShapeDtypeStruct(w_ada.shape, F32), jax.ShapeDtypeStruct((1, D), F32),
                   jax.ShapeDtypeStruct((DEPTH, 6 * D), F32)],
        compiler_params=pltpu.CompilerParams(vmem_limit_bytes=VMEM_LIMIT_BYTES),
    )(cond_in, g_loc, dmod_own, w_ada)


WEIGHTS = ["c_ctx", "w_ada", "b_ada", "norm1_w", "norm2_w", "w_in", "b_gate", "mla_q_norm_a", "mla_w_qb",
           "mla_kv_norm_a", "mla_w_kvb", "mla_q_norm", "mla_k_norm", "gla_w_gk2", "gla_b_gk", "gla_o_norm",
           "ret_decay", "w_branch", "w_out", "w_ffn_in", "w_dw", "b_dw", "w_ffn_out"]
INPUTS = ["x", "c", "ctx"] + WEIGHTS + ["loss_target"] + ["m_" + n for n in WEIGHTS] + ["v_" + n for n in WEIGHTS]
BIG = {"w_in": 2, "mla_w_qb": 2, "mla_w_kvb": 2, "w_branch": 3, "w_out": 1, "w_ffn_in": 2, "w_ffn_out": 1}
SMALL_SHARDED = {"b_gate": 2, "gla_w_gk2": 3, "gla_b_gk": 2, "w_dw": 2}
SMALL = ["c_ctx", "b_ada", "norm1_w", "norm2_w", "b_gate", "mla_q_norm_a", "mla_kv_norm_a", "mla_q_norm", "mla_k_norm",
         "gla_w_gk2", "gla_b_gk", "gla_o_norm", "ret_decay", "w_dw", "b_dw"]


def _entry_rows(size, align):
    return -(-size // (PACK_W * align)) * align


def _pack(arrays, rows, dtype, align, lead=0):
    parts = []
    for a in arrays:
        head = a.shape[:lead]
        size = math.prod(a.shape[lead:])
        r = _entry_rows(size, align)
        if r * PACK_W == size:
            parts.append(a.astype(dtype).reshape(head + (r, PACK_W)))
        else:
            flat = jnp.pad(a.astype(dtype).reshape(head + (size,)), [(0, 0)] * lead + [(0, r * PACK_W - size)])
            parts.append(flat.reshape(head + (r, PACK_W)))
    used = sum(p.shape[lead] for p in parts)
    if rows > used:
        parts.append(jnp.zeros(parts[0].shape[:lead] + (rows - used, PACK_W), dtype))
    return jnp.concatenate(parts, axis=lead)


def _pack_rows(shapes, align, multiple):
    used = sum(_entry_rows(math.prod(s), align) for s in shapes)
    return -(-used // multiple) * multiple


def _unpack(pack, shapes, align):
    head = pack.shape[:-2]
    out, off = [], 0
    for shape in shapes:
        size = math.prod(shape)
        r = _entry_rows(size, align)
        block = lax.slice_in_dim(pack, off, off + r, axis=len(head))
        if r * PACK_W != size:
            block = block.reshape(head + (r * PACK_W,))[..., :size]
        out.append(block.reshape(head + tuple(shape)))
        off += r
    return out


def _join_shards(stacked, axis):
    moved = jnp.moveaxis(stacked, 0, axis)
    shape = list(moved.shape)
    return moved.reshape(shape[:axis] + [shape[axis] * shape[axis + 1]] + shape[axis + 2:])


def _split_shards(full, axis):
    shape = list(full.shape)
    split = full.reshape(shape[:axis] + [N_DEV, shape[axis] // N_DEV] + shape[axis + 1:])
    return jnp.moveaxis(split, axis, 0)


def _gather_shards(local, axes, dtype, rows_multiple, name):
    names = list(axes)
    shapes = [local[n].shape for n in names]
    rows = _pack_rows(shapes, rows_multiple, rows_multiple)
    gathered = all_gather(_pack([local[n] for n in names], rows, dtype, rows_multiple), name)
    stacked = _unpack(gathered.reshape(N_DEV, rows, PACK_W), shapes, rows_multiple)
    return {n: _join_shards(s, axes[n]).astype(F32) for n, s in zip(names, stacked)}


def kernel(*args):
    a = dict(zip(INPUTS, args))
    me = 4 * lax.axis_index("x") + 2 * lax.axis_index("y") + lax.axis_index("c")
    cols = a["w_ada"].shape[2]

    small_names = list(SMALL_SHARDED)
    small_local = [a[n].shape for n in small_names]
    first_rows = _pack_rows([a["c"].shape] + small_local, 8, 8)
    first = all_gather(_pack([a["c"]] + [a[n] for n in small_names], first_rows, F32, 8), "gather_small")
    first = _unpack(first.reshape(N_DEV, first_rows, PACK_W), [a["c"].shape] + small_local, 8)
    c_all = first[0][:, 0]

    cond_in = jnp.concatenate([c_all, a["c_ctx"][None], jnp.zeros((ADA_ROWS - N_DEV - 1, D), F32)], axis=0)
    b_loc = lax.dynamic_slice_in_dim(a["b_ada"], me * cols, cols, axis=1)[:, None, :]
    mod_loc = ada_forward(cond_in, a["w_ada"], b_loc)
    mod_all = all_gather(mod_loc.reshape(DEPTH * ADA_ROWS, cols), "gather_mod")
    mod_all = mod_all.reshape(N_DEV, DEPTH, ADA_ROWS, cols).transpose(1, 2, 0, 3).reshape(DEPTH, ADA_ROWS, 6, D)
    mod_me = lax.dynamic_index_in_dim(mod_all, me, axis=1, keepdims=False)
    mod = jnp.pad(jnp.stack([mod_all[:, N_DEV], mod_me], axis=1), ((0, 0), (0, 0), (0, 2), (0, 0)))

    w = _gather_shards(a, BIG, BF16, 16, "gather_weights")
    w.update({n: _join_shards(s, SMALL_SHARDED[n]) for n, s in zip(small_names, first[1:])})
    for n in SMALL:
        if n not in SMALL_SHARDED and n not in ("c_ctx", "b_ada"):
            w[n] = a[n]

    loss, (gw, gmod, gx) = jax.value_and_grad(local_loss, argnums=(0, 1, 2))(
        w, mod, a["x"][0], a["ctx"][0], a["loss_target"][0])
    loss = lax.psum(loss, ("x", "y", "c"))

    dmod_own = gmod[:, :, :6].reshape(2 * DEPTH, 6 * D)
    g_all = all_gather(jnp.pad(dmod_own, ((0, 8 - 2 * DEPTH), (0, 0))), "gather_dmod").reshape(N_DEV, 8, 6 * D)
    g_loc = lax.dynamic_slice_in_dim(g_all[:, :2 * DEPTH], me * cols, cols, axis=2).transpose(1, 0, 2)
    g_w_ada, g_c_ctx, g_b_ada = ada_backward(cond_in, g_loc, dmod_own, a["w_ada"])

    small_part = dict(gw, c_ctx=g_c_ctx, b_ada=g_b_ada)
    small_shapes = [a[n].shape if n not in SMALL_SHARDED else gw[n].shape for n in SMALL]
    rows = _pack_rows(small_shapes, 8, PACK_TILE)
    parts = all_gather(_pack([small_part[n] for n in SMALL], rows, F32, 8), "gather_small_grads")
    small_sum = _unpack(slab_sum(parts.reshape(N_DEV, rows, PACK_W), "sum_small_grads"), small_shapes, 8)
    g_small = {}
    for n, g in zip(SMALL, small_sum):
        if n in SMALL_SHARDED:
            ax = SMALL_SHARDED[n]
            g = lax.dynamic_slice_in_dim(g, me * a[n].shape[ax], a[n].shape[ax], axis=ax)
        g_small[n] = g

    big_rows = _pack_rows([a[n].shape for n in BIG], 16, PACK_TILE)
    slabs = _pack([_split_shards(gw[n], ax) for n, ax in BIG.items()], big_rows, BF16, 16, lead=1)
    by_core = slabs.reshape(N_CHIP, 2, big_rows, PACK_W)
    my_core = lax.axis_index("c")
    keep = lax.dynamic_index_in_dim(by_core, my_core, axis=1, keepdims=False)
    give = lax.dynamic_index_in_dim(by_core, 1 - my_core, axis=1, keepdims=False)
    pair_sum = pair_add(keep, pair_swap(give, "swap_grads"), "add_pair_grads")
    landed = chip_all_to_all(pair_sum, "scatter_grads")

    def update(names, g_slabs, rows, align, label):
        shapes = [a[n].shape for n in names]
        packs = [_pack([a[pre + n] for n in names], rows, F32, align) for pre in ("", "m_", "v_")]
        outs = adamw(g_slabs, *packs, label)
        return [dict(zip(names, _unpack(o, shapes, align))) for o in outs]

    res_big = update(list(BIG), landed, big_rows, 16, "adamw_big")
    ada_rows = _pack_rows([a["w_ada"].shape], 8, PACK_TILE)
    res_ada = update(["w_ada"], _pack([g_w_ada], ada_rows, F32, 8)[None], ada_rows, 8, "adamw_ada")
    small_rows = _pack_rows([a[n].shape for n in SMALL], 8, PACK_TILE)
    res_small = update(SMALL, _pack([g_small[n] for n in SMALL], small_rows, F32, 8)[None], small_rows, 8,
                       "adamw_small")

    outs = [loss, gx[None]]
    for k in range(4):
        merged = {**res_big[k], **res_ada[k], **res_small[k]}
        outs += [merged[n] for n in WEIGHTS]
    return tuple(outs)
```

```python
import functools
import math

import jax
import jax.numpy as jnp
from jax import lax
from jax.experimental import pallas as pl
from jax.experimental.pallas import tpu as pltpu

F32 = jnp.float32
BF16 = jnp.bfloat16

N_DEV = 8
D = 1024
DEPTH = 2
GRID_W = 64
MLA_HEADS = 8
MLA_NOPE = 64
MLA_ROPE = 32
MLA_QK = 96
MLA_V = 64
MLA_Q_LORA = 256
MLA_KV_LORA = 128
GLA_HEADS = 4
GLA_DK = 128
GLA_RANK = 16
GLA_NORMALIZER = 16.0
RET_HEADS = 4
RET_DK = 128
BRANCH_W = 512
D_FF = 2816
CHUNK = 64
ROPE_THETA = 10000.0
RET_THETA = 10000.0
EPS = 1e-6
HEAD_PAD = 128
N_IN_PAD = 8192

ADAM_LR = 0.001
ADAM_B1 = 0.9
ADAM_B2 = 0.999
ADAM_EPS = 1e-08
ADAM_WD = 0.01
ADAM_STEP = 10

ROW_TILE = 256
SCAN_CHUNKS = ROW_TILE // CHUNK
VMEM_LIMIT_BYTES = 56 * 1024 * 1024
MESH = pl.DeviceIdType.MESH


def _cparams(n_axes):
    return pltpu.CompilerParams(dimension_semantics=("arbitrary",) * n_axes, vmem_limit_bytes=VMEM_LIMIT_BYTES)


def _pick(dim, cands):
    for cand in cands:
        if dim % cand == 0:
            return cand
    return dim


_DOT_DIMS = {"nn": (((1,), (0,)), ((), ())), "nt": (((1,), (1,)), ((), ())), "tn": (((0,), (0,)), ((), ()))}


def _dg(a, b, mode):
    return lax.dot_general(a.astype(BF16), b.astype(BF16), _DOT_DIMS[mode], preferred_element_type=F32)


def _bdot(a, b, mode):
    @jax.custom_vjp
    def f(a, b):
        return _dg(a, b, mode)

    def fwd(a, b):
        return _dg(a, b, mode), (a, b)

    def bwd(res, g):
        a, b = res
        if mode == "nn":
            return _dg(g, b, "nt").astype(a.dtype), _dg(a, g, "tn").astype(b.dtype)
        if mode == "nt":
            return _dg(g, b, "nn").astype(a.dtype), _dg(g, a, "tn").astype(b.dtype)
        return _dg(b, g, "nt").astype(a.dtype), _dg(a, g, "nn").astype(b.dtype)

    f.defvjp(fwd, bwd)
    return f(a, b)


def _roll(x, shift, axis):
    n = x.shape[axis]
    shift = shift % n

    @jax.custom_vjp
    def f(x):
        return pltpu.roll(x, shift, axis)

    def fwd(x):
        return pltpu.roll(x, shift, axis), None

    def bwd(_, g):
        return (pltpu.roll(g, (n - shift) % n, axis),)

    f.defvjp(fwd, bwd)
    return f(x)


@jax.custom_jvp
def _log_sigmoid(x):
    return jnp.minimum(x, 0.0) - jnp.log(1.0 + jnp.exp(-jnp.abs(x)))


@_log_sigmoid.defjvp
def _log_sigmoid_jvp(primals, tangents):
    (x,), (t,) = primals, tangents
    return _log_sigmoid(x), t * jax.nn.sigmoid(-x)


def _rms(x, n, w=None):
    y = x * lax.rsqrt(jnp.sum(x * x, axis=-1, keepdims=True) * (1.0 / n) + EPS)
    return y if w is None else y * w


def _silu(x):
    return x * jax.nn.sigmoid(x)


def _gelu_tanh(x):
    return 0.5 * x * (1.0 + jnp.tanh(math.sqrt(2.0 / math.pi) * (x + 0.044715 * (x * x * x))))


MM_RESIDENT_BYTES = 6 * 1024 * 1024
MM_TILE_BYTES = 9 * 1024 * 1024


def _mm(a, b, mode, name):
    if mode == "nn":
        (m, k), (_, n) = a.shape, b.shape
    elif mode == "nt":
        (m, k), (n, _) = a.shape, b.shape
    else:
        (k, m), (_, n) = a.shape, b.shape
    tm = _pick(m, (1024, 768, 1408, 512, 256, 128))
    tn = _pick(n, (1024, 1408, 512, 256, 128))
    tk = _pick(k, (1024, 768, 1408, 512, 256, 128))
    whole_b = k * n * b.dtype.itemsize <= MM_RESIDENT_BYTES
    if mode != "tn" and whole_b and tm * max(k, n) * 4 <= MM_TILE_BYTES:
        tn, tk = n, k
    nk = k // tk
    if mode == "nn":
        a_spec = pl.BlockSpec((tm, tk), lambda i, j, kk: (i, kk))
        b_spec = pl.BlockSpec((tk, tn), lambda i, j, kk: (kk, j))
    elif mode == "nt":
        a_spec = pl.BlockSpec((tm, tk), lambda i, j, kk: (i, kk))
        b_spec = pl.BlockSpec((tn, tk), lambda i, j, kk: (j, kk))
    else:
        a_spec = pl.BlockSpec((tk, tm), lambda i, j, kk: (kk, i))
        b_spec = pl.BlockSpec((tk, tn), lambda i, j, kk: (kk, j))

    def body(a_ref, b_ref, o_ref):
        kk = pl.program_id(2)
        part = _dg(a_ref[...], b_ref[...], mode)
        if nk == 1:
            o_ref[...] = part
        else:
            @pl.when(kk == 0)
            def _():
                o_ref[...] = part

            @pl.when(kk != 0)
            def _():
                o_ref[...] += part

    return pl.pallas_call(
        body, name=name, grid=(m // tm, n // tn, nk),
        in_specs=[a_spec, b_spec], out_specs=pl.BlockSpec((tm, tn), lambda i, j, kk: (i, j)),
        out_shape=jax.ShapeDtypeStruct((m, n), F32),
        compiler_params=_cparams(3),
    )(a, b)


def linear(x, w, name):
    @jax.custom_vjp
    def op(x, w):
        return _mm(x, w.astype(BF16), "nn", name + "_f")

    def fwd(x, w):
        wb = w.astype(BF16)
        return _mm(x, wb, "nn", name + "_f"), (x, wb)

    def bwd(res, g):
        x, wb = res
        return _mm(g, wb, "nt", name + "_dx"), _mm(x, g, "tn", name + "_dw")

    op.defvjp(fwd, bwd)
    return op(x, w)


PIECE_W = 512
PIECE_ROWS = 384
PIECE_GROUP = 4


def _mm_split(a, wb, name):
    (r, k), n = a.shape, wb.shape[1] // PIECE_W
    tm = _pick(r, (PIECE_ROWS, ROW_TILE))
    width = PIECE_GROUP * PIECE_W

    n_tiles = r // tm

    def body(a_ref, w_ref, *out_refs):
        j = pl.program_id(0)
        res = _dg(a_ref[...], w_ref[...], "nn")
        for group in range(n // PIECE_GROUP):
            @pl.when(j == group)
            def _(group=group):
                for p in range(PIECE_GROUP):
                    out_refs[group * PIECE_GROUP + p][...] = res[:, p * PIECE_W:(p + 1) * PIECE_W]

    def out_spec(jj):
        group = jj // PIECE_GROUP
        return pl.BlockSpec((tm, PIECE_W),
                            lambda j, i: (jnp.where(j == group, i, jnp.where(j < group, 0, n_tiles - 1)), 0))

    return pl.pallas_call(
        body, name=name, grid=(n // PIECE_GROUP, n_tiles),
        in_specs=[pl.BlockSpec((tm, k), lambda j, i: (i, 0)), pl.BlockSpec((k, width), lambda j, i: (0, j))],
        out_specs=[out_spec(jj) for jj in range(n)],
        out_shape=[jax.ShapeDtypeStruct((r, PIECE_W), F32)] * n, compiler_params=_cparams(2),
    )(a, wb)


def _mm_join(gs, wb, name):
    n, (r, _), k = len(gs), gs[0].shape, wb.shape[0]
    tm = _pick(r, (PIECE_ROWS, ROW_TILE))
    n_groups = n // PIECE_GROUP

    def body(*refs):
        g_refs, w_ref, o_ref = refs[:n], refs[n], refs[n + 1]
        j = pl.program_id(1)
        for group in range(n_groups):
            @pl.when(j == group)
            def _(group=group):
                g = jnp.concatenate([g_refs[group * PIECE_GROUP + p][...].astype(BF16) for p in range(PIECE_GROUP)],
                                    axis=1)
                part = _dg(g, w_ref[...], "nt")
                if group == 0:
                    o_ref[...] = part
                else:
                    o_ref[...] += part

    return pl.pallas_call(
        body, name=name, grid=(r // tm, n_groups),
        in_specs=[pl.BlockSpec((tm, PIECE_W), lambda i, j: (i, 0))] * n
        + [pl.BlockSpec((k, PIECE_GROUP * PIECE_W), lambda i, j: (0, j))],
        out_specs=pl.BlockSpec((tm, k), lambda i, j: (i, 0)), out_shape=jax.ShapeDtypeStruct((r, k), F32),
        compiler_params=_cparams(2),
    )(*gs, wb)


def _mm_join_tn(a, gs, name):
    n, (r, k) = len(gs), a.shape
    tk = _pick(r, (768, 512, ROW_TILE))
    width = PIECE_GROUP * PIECE_W

    def group_call(group):
        def body(a_ref, *refs):
            g_refs, o_ref = refs[:PIECE_GROUP], refs[PIECE_GROUP]
            part = _dg(a_ref[...], jnp.concatenate([g_ref[...].astype(BF16) for g_ref in g_refs], axis=1), "tn")

            @pl.when(pl.program_id(0) == 0)
            def _():
                o_ref[...] = part

            @pl.when(pl.program_id(0) != 0)
            def _():
                o_ref[...] += part

        return pl.pallas_call(
            body, name=f"{name}{group}", grid=(r // tk,),
            in_specs=[pl.BlockSpec((tk, k), lambda kk: (kk, 0))]
            + [pl.BlockSpec((tk, PIECE_W), lambda kk: (kk, 0))] * PIECE_GROUP,
            out_specs=pl.BlockSpec((k, width), lambda kk: (0, 0)),
            out_shape=jax.ShapeDtypeStruct((k, width), F32), compiler_params=_cparams(1),
        )(a, *gs[group * PIECE_GROUP:(group + 1) * PIECE_GROUP])

    return jnp.concatenate([group_call(group) for group in range(n // PIECE_GROUP)], axis=1)


def rowwise(name, fn, rows, segs, params, out_widths, nct, diff_rows=None, then=None):
    n_row, n_seg, n_par, n_out = len(rows), len(segs), len(params), len(out_widths)
    out_dtype = F32 if then is None else BF16
    diff_rows = [True] * n_row if diff_rows is None else list(diff_rows)
    r_total = rows[0].shape[0]
    n_tiles = r_total // ROW_TILE

    def seg_of(i):
        return jnp.where(i < nct, 0, 1)

    def row_spec(width):
        return pl.BlockSpec((ROW_TILE, width), lambda i: (i, 0))

    def seg_spec(shape):
        nd = len(shape)
        return pl.BlockSpec((1,) + tuple(shape[1:]), lambda i: (seg_of(i),) + (0,) * (nd - 1))

    def par_spec(shape):
        nd = len(shape)
        return pl.BlockSpec(tuple(shape), lambda i: (0,) * nd)

    in_specs = ([row_spec(r.shape[1]) for r in rows] + [seg_spec(s.shape) for s in segs]
                + [par_spec(p.shape) for p in params])

    def load(refs):
        vals = [r[...].astype(F32) for r in refs[:n_row]]
        vals += [r[0].astype(F32) for r in refs[n_row:n_row + n_seg]]
        vals += [r[...].astype(F32) for r in refs[n_row + n_seg:n_row + n_seg + n_par]]
        return vals

    def fwd_call(arrs):
        def body(*refs):
            outs = fn(*load(refs))
            for o_ref, val in zip(refs[n_row + n_seg + n_par:], outs):
                o_ref[...] = val.astype(o_ref.dtype)

        return pl.pallas_call(
            body, name=name + "_f", grid=(n_tiles,), in_specs=in_specs,
            out_specs=[row_spec(w) for w in out_widths],
            out_shape=[jax.ShapeDtypeStruct((r_total, w), out_dtype) for w in out_widths],
            compiler_params=_cparams(1),
        )(*arrs)

    d_idx = [k for k in range(n_row) if diff_rows[k]]

    def bwd_call(arrs, douts):
        n_in = n_row + n_seg + n_par

        def body(*refs):
            i = pl.program_id(0)
            vals = load(refs[:n_in])
            gs = [r[...] for r in refs[n_in:n_in + n_out]]
            out_refs = refs[n_in + n_out:]
            diff_pos = d_idx + list(range(n_row, n_in))

            def f(*dv):
                full = list(vals)
                for pos, v in zip(diff_pos, dv):
                    full[pos] = v
                return tuple(fn(*full))

            _, vjp = jax.vjp(f, *[vals[p] for p in diff_pos])
            grads = vjp(tuple(gs))
            nd = len(d_idx)
            for o_ref, g in zip(out_refs[:nd], grads[:nd]):
                o_ref[...] = g
            first_seg = jnp.logical_or(i == 0, i == nct)
            for o_ref, g in zip(out_refs[nd:nd + n_seg], grads[nd:nd + n_seg]):
                @pl.when(first_seg)
                def _(o_ref=o_ref, g=g):
                    o_ref[0] = g

                @pl.when(jnp.logical_not(first_seg))
                def _(o_ref=o_ref, g=g):
                    o_ref[0] += g
            for o_ref, g in zip(out_refs[nd + n_seg:], grads[nd + n_seg:]):
                @pl.when(i == 0)
                def _(o_ref=o_ref, g=g):
                    o_ref[...] = g

                @pl.when(i != 0)
                def _(o_ref=o_ref, g=g):
                    o_ref[...] += g

        out_specs = ([row_spec(rows[k].shape[1]) for k in d_idx] + [seg_spec(s.shape) for s in segs]
                     + [par_spec(p.shape) for p in params])
        out_shape = ([jax.ShapeDtypeStruct(rows[k].shape, F32) for k in d_idx]
                     + [jax.ShapeDtypeStruct(s.shape, F32) for s in segs]
                     + [jax.ShapeDtypeStruct(p.shape, F32) for p in params])
        return pl.pallas_call(
            body, name=name + "_b", grid=(n_tiles,),
            in_specs=in_specs + [row_spec(w) for w in out_widths],
            out_specs=out_specs, out_shape=out_shape, compiler_params=_cparams(1),
        )(*arrs, *douts)

    @jax.custom_vjp
    def op(*arrs):
        return tuple(fwd_call(arrs))

    def op_fwd(*arrs):
        return tuple(fwd_call(arrs)), arrs

    def op_bwd(arrs, douts):
        grads = list(bwd_call(arrs, douts))
        nd = len(d_idx)
        row_grads = [jnp.zeros_like(arrs[k]) for k in range(n_row)]
        for k, g in zip(d_idx, grads[:nd]):
            row_grads[k] = g
        return tuple(row_grads + grads[nd:])

    if then is None:
        op.defvjp(op_fwd, op_bwd)
        return op(*rows, *segs, *params)

    kind, w, mm_name = then

    if kind == "attention":
        @jax.custom_vjp
        def attended(*arrs):
            return attention_forward(*fwd_call(arrs), w, mm_name)[0]

        def attended_fwd(*arrs):
            qkv = fwd_call(arrs)
            o, parts = attention_forward(*qkv, w, mm_name)
            return o, (arrs, qkv, parts)

        def attended_bwd(res, do):
            arrs, qkv, parts = res
            return op_bwd(arrs, list(attention_backward(*qkv, parts, do, w, mm_name)))

        attended.defvjp(attended_fwd, attended_bwd)
        return attended(*rows, *segs, *params)

    def project(u, wb):
        if kind == "linear":
            return _mm(u, wb, "nn", mm_name + "_f")
        return tuple(_mm_split(u, wb, mm_name + "_f"))

    @jax.custom_vjp
    def fused(w, *arrs):
        return project(fwd_call(arrs)[0], w.astype(BF16))

    def fused_fwd(w, *arrs):
        u, wb = fwd_call(arrs)[0], w.astype(BF16)
        return project(u, wb), (arrs, u, wb)

    def fused_bwd(res, g):
        arrs, u, wb = res
        if kind == "linear":
            du, dw = _mm(g, wb, "nt", mm_name + "_dx"), _mm(u, g, "tn", mm_name + "_dw")
        else:
            du, dw = _mm_join(list(g), wb, mm_name + "_dx"), _mm_join_tn(u, list(g), mm_name + "_dw")
        return (dw,) + op_bwd(arrs, [du])

    fused.defvjp(fused_fwd, fused_bwd)
    return fused(w, *rows, *segs, *params)


ATT_SCALE = MLA_QK ** -0.5
LOG2E = math.log2(math.e)
ATT_KEY_CHUNKS = (768, 512, 256)


ATT_LATENT_TILE = 1024


def _query_rows_spec(row0, tq):
    return pl.BlockSpec((pl.Element(tq), pl.Element(HEAD_PAD)),
                        lambda h, i: (pl.multiple_of(row0 + i * tq, ROW_TILE), pl.multiple_of(h * HEAD_PAD, HEAD_PAD)))


def _key_chunks(nk):
    kc = _pick(nk, ATT_KEY_CHUNKS)
    return [(c * kc, kc) for c in range(nk // kc)]


def _attn_fwd_call(q, k, v, row0, n_rows, tq, nk, name):
    def body(q_ref, k_ref, v_ref, o_ref, lse_ref):
        qv = q_ref[...]
        m = jnp.full((tq, 1), -jnp.inf, F32)
        l = jnp.zeros((tq, 1), F32)
        acc = jnp.zeros((tq, HEAD_PAD), F32)
        for start, size in _key_chunks(nk):
            s = lax.dot_general(qv, k_ref[start:start + size, :], _DOT_DIMS["nt"], preferred_element_type=F32)
            m_new = jnp.maximum(m, jnp.max(s, axis=-1, keepdims=True))
            alpha = jnp.exp2(m - m_new)
            p = jnp.exp2(s - m_new)
            l = alpha * l + jnp.sum(p, axis=-1, keepdims=True)
            acc = alpha * acc + lax.dot_general(p.astype(BF16), v_ref[start:start + size, :], _DOT_DIMS["nn"],
                                                preferred_element_type=F32)
            m = m_new
        o_ref[...] = acc / l
        lse_ref[...] = jnp.broadcast_to(m + jnp.log2(l), (tq, HEAD_PAD))

    out_spec = pl.BlockSpec((tq, HEAD_PAD), lambda h, i: (i, h))
    kv_spec = pl.BlockSpec((nk, HEAD_PAD), lambda h, i: (0, h))
    out = jax.ShapeDtypeStruct((n_rows, q.shape[1]), F32)
    return pl.pallas_call(
        body, name=name, grid=(MLA_HEADS, n_rows // tq), in_specs=[_query_rows_spec(row0, tq), kv_spec, kv_spec],
        out_specs=[out_spec, out_spec], out_shape=[out, out], compiler_params=_cparams(2),
    )(q, k, v)


def _attn_bwd_call(q, k, v, o, lse, do, row0, n_rows, tq, nk, name):
    nq = n_rows // tq

    def body(q_ref, k_ref, v_ref, o_ref, lse_ref, do_ref, dq_ref, dk_ref, dv_ref):
        i = pl.program_id(1)

        @pl.when(i == 0)
        def _():
            dk_ref[...] = jnp.zeros_like(dk_ref)
            dv_ref[...] = jnp.zeros_like(dv_ref)

        qv = q_ref[...]
        dov = do_ref[...]
        dob = dov.astype(BF16)
        lse = lse_ref[:, 0:1]
        delta = jnp.sum(dov * o_ref[...], axis=-1, keepdims=True)
        dq = jnp.zeros((tq, HEAD_PAD), F32)
        for start, size in _key_chunks(nk):
            kk = k_ref[start:start + size, :]
            vv = v_ref[start:start + size, :]
            s = lax.dot_general(qv, kk, _DOT_DIMS["nt"], preferred_element_type=F32)
            p = jnp.exp2(s - lse)
            dp = lax.dot_general(dob, vv, _DOT_DIMS["nt"], preferred_element_type=F32)
            g = (p * (dp - delta)).astype(BF16)
            dk_ref[start:start + size, :] += lax.dot_general(g, qv, _DOT_DIMS["tn"], preferred_element_type=F32)
            dv_ref[start:start + size, :] += lax.dot_general(p.astype(BF16), dob, _DOT_DIMS["tn"],
                                                             preferred_element_type=F32)
            dq = dq + lax.dot_general(g, kk, _DOT_DIMS["nn"], preferred_element_type=F32)
        dq_ref[...] = dq * (1.0 / LOG2E)

        @pl.when(i == nq - 1)
        def _():
            dk_ref[...] = dk_ref[...] * (1.0 / LOG2E)

    own_spec = pl.BlockSpec((tq, HEAD_PAD), lambda h, i: (i, h))
    kv_spec = pl.BlockSpec((nk, HEAD_PAD), lambda h, i: (0, h))
    rows_spec = _query_rows_spec(row0, tq)
    return pl.pallas_call(
        body, name=name, grid=(MLA_HEADS, nq),
        in_specs=[rows_spec, kv_spec, kv_spec, own_spec, own_spec, rows_spec],
        out_specs=[own_spec, kv_spec, kv_spec],
        out_shape=[jax.ShapeDtypeStruct((n_rows, q.shape[1]), F32), jax.ShapeDtypeStruct((nk, q.shape[1]), F32),
                   jax.ShapeDtypeStruct((nk, q.shape[1]), F32)],
        compiler_params=_cparams(2),
    )(q, k, v, o, lse, do)


def _attn_ranges(r_total, tc):
    tq_lat = _pick(r_total - tc, (ATT_LATENT_TILE, ROW_TILE))
    return [(0, tc, ROW_TILE, tc, "_ctx"), (tc, r_total - tc, tq_lat, r_total, "_lat")]


def attention_forward(qs, k, v, tc, name):
    parts = [_attn_fwd_call(qs, k, v, row0, n_rows, tq, nk, name + tag + "_f")
             for row0, n_rows, tq, nk, tag in _attn_ranges(qs.shape[0], tc)]
    return jnp.concatenate([o for o, _ in parts], axis=0), parts


def attention_backward(qs, k, v, parts, do, tc, name):
    r_total = qs.shape[0]
    (dq_c, dk_c, dv_c), (dq_l, dk_l, dv_l) = [
        _attn_bwd_call(qs, k, v, o, lse, do, row0, n_rows, tq, nk, name + tag + "_b")
        for (o, lse), (row0, n_rows, tq, nk, tag) in zip(parts, _attn_ranges(r_total, tc))]
    grow = lambda part: jnp.pad(part, ((0, r_total - tc), (0, 0)))
    return jnp.concatenate([dq_c, dq_l], axis=0), dk_l + grow(dk_c), dv_l + grow(dv_c)


CHUNK_SHIFT = CHUNK.bit_length() - 1


def _block_pairs():
    rows = lax.broadcasted_iota(jnp.int32, (ROW_TILE, ROW_TILE), 0)
    cols = lax.broadcasted_iota(jnp.int32, (ROW_TILE, ROW_TILE), 1)
    same = lax.shift_right_logical(rows, CHUNK_SHIFT) == lax.shift_right_logical(cols, CHUNK_SHIFT)
    return rows, cols, same


def _block_mask(kind):
    rows, cols, same = _block_pairs()
    order = {"lower_incl": rows >= cols, "upper_incl": rows <= cols, "lower_strict": rows > cols,
             "upper_strict": rows < cols}[kind]
    return jnp.logical_and(same, order)


def _row_chunk():
    return lax.shift_right_logical(lax.broadcasted_iota(jnp.int32, (ROW_TILE, 1), 0), CHUNK_SHIFT)


def _dot01(kind, x):
    m = _block_mask(kind).astype(BF16)
    hi = x.astype(BF16)
    rest = x - hi.astype(F32)
    mid = rest.astype(BF16)
    lo = (rest - mid.astype(F32)).astype(BF16)
    terms = jnp.concatenate([hi, mid, lo], axis=1)
    out = lax.dot_general(m, terms, _DOT_DIMS["nn"], preferred_element_type=F32)
    n = x.shape[1]
    return out[:, :n] + out[:, n:2 * n] + out[:, 2 * n:]


def _chunk_sums(x, forward):
    kinds = ("lower_incl", "upper_strict") if forward else ("upper_incl", "lower_strict")
    transposed = ("upper_incl", "lower_strict") if forward else ("lower_incl", "upper_strict")

    @jax.custom_vjp
    def f(x):
        return _dot01(kinds[0], x), _dot01(kinds[1], x)

    def fwd(x):
        return (_dot01(kinds[0], x), _dot01(kinds[1], x)), None

    def bwd(_, g):
        return (_dot01(transposed[0], g[0]) + _dot01(transposed[1], g[1]),)

    f.defvjp(fwd, bwd)
    return f(x)


def _scan_order(forward):
    if forward:
        return list(range(SCAN_CHUNKS)), lambda c: c * CHUNK + CHUNK - 1
    return list(range(SCAN_CHUNKS - 1, -1, -1)), lambda c: c * CHUNK


def _carry_states(forward, st0, inc_all, decay_of):
    order, _ = _scan_order(forward)
    entering = [None] * SCAN_CHUNKS
    st = st0
    for c in order:
        entering[c] = st
        st = st * decay_of(c) + inc_all[:, c * HEAD_PAD:(c + 1) * HEAD_PAD]
    return jnp.concatenate(entering, axis=0), st


def _per_chunk_lanes(x):
    chunk = _row_chunk()
    return jnp.concatenate([jnp.where(chunk == c, x, 0.0) for c in range(SCAN_CHUNKS)], axis=1)


def _own_chunk_lanes(x4):
    chunk = _row_chunk()
    n = x4.shape[1] // SCAN_CHUNKS
    out = jnp.where(chunk == 0, x4[:, :n], 0.0)
    for c in range(1, SCAN_CHUNKS):
        out = out + jnp.where(chunk == c, x4[:, c * n:(c + 1) * n], 0.0)
    return out


def _gla_block(forward, q, k, v, la, st0):
    cum, after = _chunk_sums(la, forward)
    _, last_row = _scan_order(forward)
    q_dec = q * (jnp.exp(cum) * (GLA_DK ** -0.5))
    att = _bdot(q_dec, k * jnp.exp(-cum), "nt")
    att = jnp.where(_block_mask("lower_incl" if forward else "upper_strict"), att, 0.0)
    inc_all = _bdot(v, _per_chunk_lanes(k * jnp.exp(after)), "tn")
    entering, st1 = _carry_states(forward, st0, inc_all,
                                  lambda c: jnp.exp(cum[last_row(c):last_row(c) + 1, :]))
    o = _bdot(att, v, "nn") + _own_chunk_lanes(_bdot(q_dec, entering, "nt"))
    return o, st1


def _ret_block(forward, q, k, v, rd, st0):
    lg = -jnp.exp(rd[0:1, 0:1])
    rows, cols, _ = _block_pairs()
    pos = jnp.bitwise_and(lax.broadcasted_iota(jnp.int32, (ROW_TILE, 1), 0), CHUNK - 1).astype(F32)
    if forward:
        to_end, from_start, rel = CHUNK - 1.0 - pos, pos + 1.0, (rows - cols).astype(F32)
    else:
        to_end, from_start, rel = pos, CHUNK - pos, (cols - rows).astype(F32)
    mask = _block_mask("lower_incl" if forward else "upper_strict")
    dmat = jnp.where(mask, jnp.exp(jnp.where(mask, rel, 0.0) * lg), 0.0)
    att = _bdot(q, k, "nt") * dmat
    inc_all = _bdot(v, _per_chunk_lanes(k * jnp.exp(to_end * lg)), "tn")
    entering, st1 = _carry_states(forward, st0, inc_all, lambda c: jnp.exp(CHUNK * lg))
    o = _bdot(att, v, "nn") + _own_chunk_lanes(_bdot(q, entering, "nt")) * jnp.exp(from_start * lg)
    return o, st1


def scan(kind, forward, q, k, v, aux, tc, name, rot=None):
    heads = q.shape[1] // HEAD_PAD
    r_total = q.shape[0]
    nblk = r_total // ROW_TILE
    nctb = tc // ROW_TILE
    block_fn = functools.partial(_gla_block if kind == "gla" else _ret_block, forward)
    per_row_aux = kind == "gla"

    def blk(g):
        if forward:
            return g
        return jnp.where(g < nctb, nctb - 1 - g, nblk - 1 - (g - nctb))

    def specs(step_to_g):
        row = pl.BlockSpec((ROW_TILE, heads * HEAD_PAD), lambda s: (blk(step_to_g(s)), 0))
        aux_spec = row if per_row_aux else pl.BlockSpec((heads, 8, HEAD_PAD), lambda s: (0, 0, 0))
        st = pl.BlockSpec((1, heads, HEAD_PAD, HEAD_PAD), lambda s: (step_to_g(s), 0, 0, 0))
        return row, aux_spec, st

    def head_cols(h):
        return slice(h * HEAD_PAD, (h + 1) * HEAD_PAD)

    n_rot = 0 if rot is None else 2
    rot_arrays = [] if rot is None else list(rot)

    def rot_specs(step_to_g):
        return [pl.BlockSpec((ROW_TILE, HEAD_PAD), lambda s: (blk(step_to_g(s)), 0))] * n_rot

    def head_fn(rot_refs):
        if not rot_refs:
            return block_fn
        rc, rs = rot_refs[0][...], rot_refs[1][...]

        def turned(x, scale):
            return (x * rc + _roll(x, HEAD_PAD // 2, 1) * rs) * scale

        return lambda q, k, v, a, st0: block_fn(turned(q, 1.0), turned(k, RET_DK ** -0.5), v, a, st0)

    def fwd_call(q, k, v, aux):
        row, aux_spec, st_spec = specs(lambda s: s)

        def body(*refs):
            q_ref, k_ref, v_ref, a_ref = refs[:4]
            o_ref, st0_ref, st_ref = refs[4 + n_rot:]
            fn = head_fn(refs[4:4 + n_rot])

            @pl.when(pl.program_id(0) == 0)
            def _():
                st_ref[...] = jnp.zeros_like(st_ref)

            qv, kv, vv = q_ref[...], k_ref[...], v_ref[...]
            outs = []
            for h in range(heads):
                st0 = st_ref[h]
                st0_ref[0, h] = st0
                a = a_ref[:, head_cols(h)] if per_row_aux else a_ref[h]
                o, st1 = fn(qv[:, head_cols(h)], kv[:, head_cols(h)], vv[:, head_cols(h)], a, st0)
                outs.append(o)
                st_ref[h] = st1
            o_ref[...] = jnp.concatenate(outs, axis=1)

        return pl.pallas_call(
            body, name=name + "_f", grid=(nblk,), in_specs=[row, row, row, aux_spec] + rot_specs(lambda s: s),
            out_specs=[row, st_spec],
            out_shape=[jax.ShapeDtypeStruct(q.shape, F32),
                       jax.ShapeDtypeStruct((nblk, heads, HEAD_PAD, HEAD_PAD), F32)],
            scratch_shapes=[pltpu.VMEM((heads, HEAD_PAD, HEAD_PAD), F32)],
            compiler_params=_cparams(1),
        )(q, k, v, aux, *rot_arrays)

    def bwd_call(q, k, v, aux, st0s, do):
        row, aux_spec, st_spec = specs(lambda s: nblk - 1 - s)

        def body(*refs):
            q_ref, k_ref, v_ref, a_ref = refs[:4]
            st0_ref, do_ref, dq_ref, dk_ref, dv_ref, da_ref, dst_ref = refs[4 + n_rot:]
            fn = head_fn(refs[4:4 + n_rot])
            s = pl.program_id(0)

            @pl.when(s == 0)
            def _():
                dst_ref[...] = jnp.zeros_like(dst_ref)

            qv, kv, vv, dov = q_ref[...], k_ref[...], v_ref[...], do_ref[...]
            grads = []
            for h in range(heads):
                a = a_ref[:, head_cols(h)] if per_row_aux else a_ref[h]
                _, vjp = jax.vjp(fn, qv[:, head_cols(h)], kv[:, head_cols(h)], vv[:, head_cols(h)], a,
                                 st0_ref[0, h])
                dq, dk, dv, da, dst0 = vjp((dov[:, head_cols(h)], dst_ref[h]))
                dst_ref[h] = dst0
                grads.append((dq, dk, dv, da))
            dq_ref[...] = jnp.concatenate([g[0] for g in grads], axis=1)
            dk_ref[...] = jnp.concatenate([g[1] for g in grads], axis=1)
            dv_ref[...] = jnp.concatenate([g[2] for g in grads], axis=1)
            if per_row_aux:
                da_ref[...] = jnp.concatenate([g[3] for g in grads], axis=1)
            else:
                da = jnp.stack([g[3] for g in grads], axis=0)

                @pl.when(s == 0)
                def _():
                    da_ref[...] = da

                @pl.when(s != 0)
                def _():
                    da_ref[...] += da

        return pl.pallas_call(
            body, name=name + "_b", grid=(nblk,),
            in_specs=[row, row, row, aux_spec] + rot_specs(lambda s: nblk - 1 - s) + [st_spec, row],
            out_specs=[row, row, row, aux_spec],
            out_shape=[jax.ShapeDtypeStruct(q.shape, F32)] * 3 + [jax.ShapeDtypeStruct(aux.shape, F32)],
            scratch_shapes=[pltpu.VMEM((heads, HEAD_PAD, HEAD_PAD), F32)],
            compiler_params=_cparams(1),
        )(q, k, v, aux, *rot_arrays, st0s, do)

    @jax.custom_vjp
    def op(q, k, v, aux):
        return fwd_call(q, k, v, aux)[0]

    def fwd(q, k, v, aux):
        o, st0s = fwd_call(q, k, v, aux)
        return o, (q, k, v, aux, st0s)

    def bwd(res, do):
        return tuple(bwd_call(*res, do))

    op.defvjp(fwd, bwd)
    return op(q, k, v, aux)


HALO = 8


def _neighbours(main, prev8, next8, i, nct, n_tiles):
    has_prev = jnp.logical_and(i != 0, i != nct).astype(F32)
    has_next = jnp.logical_and(i != nct - 1, i != n_tiles - 1).astype(F32)
    row = lax.broadcasted_iota(jnp.int32, main.shape, 0)
    down = jnp.where(row == 0, prev8[HALO - 1:HALO] * has_prev, pltpu.roll(main, 1, 0))
    up = jnp.where(row == ROW_TILE - 1, next8[0:1] * has_next, pltpu.roll(main, ROW_TILE - 1, 0))
    return down, up


GELU_K = math.sqrt(2.0 / math.pi)
GELU_A = 0.044715


def _gelu_tanh_grad(x):
    t = jnp.tanh(GELU_K * (x + GELU_A * (x * x * x)))
    return 0.5 * x * (1.0 + t), 0.5 * (1.0 + t) + 0.5 * x * (1.0 - t * t) * (GELU_K * (1.0 + 3.0 * GELU_A * (x * x)))


def conv_ffn_out(gate, up, w8, b, w_out, tc, name):
    r_total, width = gate.shape
    n_tiles = r_total // ROW_TILE
    nct = tc // ROW_TILE
    per = ROW_TILE // HALO
    main_spec = pl.BlockSpec((ROW_TILE, width), lambda i: (i, 0))
    prev_spec = pl.BlockSpec((HALO, width), lambda i: (jnp.maximum(i * per - 1, 0), 0))
    next_spec = pl.BlockSpec((HALO, width), lambda i: (jnp.minimum((i + 1) * per, r_total // HALO - 1), 0))
    w_spec = pl.BlockSpec((8, width), lambda i: (0, 0))
    b_spec = pl.BlockSpec((1, width), lambda i: (0, 0))
    halo3 = [main_spec, prev_spec, next_spec]

    def conv(w_ref, b_ref, down, mid, upn):
        return w_ref[0:1] * down + w_ref[1:2] * mid + w_ref[2:3] * upn + b_ref[...]

    def fwd_call(gate, up, w8, b):
        def body(g_ref, gp_ref, gn_ref, up_ref, w_ref, b_ref, o_ref):
            gv = g_ref[...]
            down, upn = _neighbours(gv, gp_ref[...], gn_ref[...], pl.program_id(0), nct, n_tiles)
            o_ref[...] = (_gelu_tanh(conv(w_ref, b_ref, down, gv, upn)) * up_ref[...]).astype(BF16)

        return pl.pallas_call(
            body, name=name + "_f", grid=(n_tiles,), in_specs=halo3 + [main_spec, w_spec, b_spec],
            out_specs=main_spec, out_shape=jax.ShapeDtypeStruct(gate.shape, BF16), compiler_params=_cparams(1),
        )(gate, gate, gate, up, w8, b)

    def bwd_call(gate, up, w8, b, du):
        def body(g_ref, gp_ref, gn_ref, up_ref, upp_ref, upn_ref, du_ref, dup_ref, dun_ref, w_ref, b_ref,
                 dg_ref, dupo_ref, dw_ref, db_ref):
            i = pl.program_id(0)
            has_prev = jnp.logical_and(i != 0, i != nct).astype(F32)
            has_next = jnp.logical_and(i != nct - 1, i != n_tiles - 1).astype(F32)
            gv, gp, gn = g_ref[...], gp_ref[...], gn_ref[...]
            g_down, g_up = _neighbours(gv, gp, gn, i, nct, n_tiles)
            act, slope = _gelu_tanh_grad(conv(w_ref, b_ref, g_down, gv, g_up))
            duv = du_ref[...]
            dupo_ref[...] = duv * act
            dc = duv * up_ref[...] * slope
            c_above = conv(w_ref, b_ref, gp[HALO - 2:HALO - 1], gp[HALO - 1:HALO], gv[0:1])
            c_below = conv(w_ref, b_ref, gv[ROW_TILE - 1:ROW_TILE], gn[0:1], gn[1:2])
            dc_above = dup_ref[HALO - 1:HALO] * upp_ref[HALO - 1:HALO] * _gelu_tanh_grad(c_above)[1] * has_prev
            dc_below = dun_ref[0:1] * upn_ref[0:1] * _gelu_tanh_grad(c_below)[1] * has_next
            row = lax.broadcasted_iota(jnp.int32, dc.shape, 0)
            dc_down = jnp.where(row == 0, dc_above, pltpu.roll(dc, 1, 0))
            dc_up = jnp.where(row == ROW_TILE - 1, dc_below, pltpu.roll(dc, ROW_TILE - 1, 0))
            dg_ref[...] = w_ref[0:1] * dc_up + w_ref[1:2] * dc + w_ref[2:3] * dc_down
            dw = jnp.concatenate([jnp.sum(dc * g_down, axis=0, keepdims=True),
                                  jnp.sum(dc * gv, axis=0, keepdims=True),
                                  jnp.sum(dc * g_up, axis=0, keepdims=True),
                                  jnp.zeros((5, width), F32)], axis=0)
            db = jnp.sum(dc, axis=0, keepdims=True)

            @pl.when(i == 0)
            def _():
                dw_ref[...] = dw
                db_ref[...] = db

            @pl.when(i != 0)
            def _():
                dw_ref[...] += dw
                db_ref[...] += db

        return pl.pallas_call(
            body, name=name + "_b", grid=(n_tiles,), in_specs=halo3 * 3 + [w_spec, b_spec],
            out_specs=[main_spec, main_spec, w_spec, b_spec],
            out_shape=[jax.ShapeDtypeStruct(gate.shape, F32), jax.ShapeDtypeStruct(gate.shape, F32),
                       jax.ShapeDtypeStruct((8, width), F32), jax.ShapeDtypeStruct((1, width), F32)],
            compiler_params=_cparams(1),
        )(gate, gate, gate, up, up, up, du, du, du, w8, b)

    @jax.custom_vjp
    def op(gate, up, w8, b, w_out):
        return _mm(fwd_call(gate, up, w8, b), w_out.astype(BF16), "nn", name + "_out_f")

    def fwd(gate, up, w8, b, w_out):
        u, wb = fwd_call(gate, up, w8, b), w_out.astype(BF16)
        return _mm(u, wb, "nn", name + "_out_f"), (gate, up, w8, b, u, wb)

    def bwd(res, g):
        gate, up, w8, b, u, wb = res
        du = _mm(g, wb, "nt", name + "_out_dx")
        d_gate, d_up, d_w8, d_b = bwd_call(gate, up, w8, b, du)
        return d_gate, d_up, d_w8, d_b, _mm(u, g, "tn", name + "_out_dw")

    op.defvjp(fwd, bwd)
    return op(gate, up, w8, b, w_out)


def loss_head(h, target, tc, name):
    r_total, width = h.shape
    n_tiles = r_total // ROW_TILE
    nct = tc // ROW_TILE

    def call(h, target):
        def body(h_ref, t_ref, dh_ref, loss_ref, acc_ref):
            i = pl.program_id(0)

            @pl.when(i == 0)
            def _():
                acc_ref[...] = jnp.zeros_like(acc_ref)

            @pl.when(i < nct)
            def _():
                dh_ref[...] = jnp.zeros_like(dh_ref)

            @pl.when(i >= nct)
            def _():
                err = h_ref[...] - t_ref[...]
                dh_ref[...] = err * (1.0 / width)
                acc_ref[...] += jnp.sum((err * err).reshape(ROW_TILE // 8, 8, width), axis=0)

            @pl.when(i == n_tiles - 1)
            def _():
                loss_ref[...] = jnp.sum(acc_ref[...]).reshape(1, 1) * (0.5 / width)

        row = pl.BlockSpec((ROW_TILE, width), lambda i: (i, 0))
        return pl.pallas_call(
            body, name=name, grid=(n_tiles,),
            in_specs=[row, pl.BlockSpec((ROW_TILE, width), lambda i: (jnp.maximum(i - nct, 0), 0))],
            out_specs=[row, pl.BlockSpec((1, 1), lambda i: (0, 0))],
            out_shape=[jax.ShapeDtypeStruct(h.shape, F32), jax.ShapeDtypeStruct((1, 1), F32)],
            scratch_shapes=[pltpu.VMEM((8, width), F32)], compiler_params=_cparams(1),
        )(h, target)

    @jax.custom_vjp
    def op(h, target):
        return call(h, target)[1][0, 0]

    def fwd(h, target):
        dh, loss = call(h, target)
        return loss[0, 0], (dh, target)

    def bwd(res, g):
        dh, target = res
        return dh * g, jnp.zeros_like(target)

    op.defvjp(fwd, bwd)
    return op(h, target)


PACK_W = 1024
PACK_TILE = 128


def slab_sum(slabs, name):
    n_slab, n, _ = slabs.shape

    def body(s_ref, o_ref):
        acc = s_ref[0]
        for j in range(1, n_slab):
            acc = acc + s_ref[j]
        o_ref[...] = acc

    return pl.pallas_call(
        body, name=name, grid=(n // PACK_TILE,),
        in_specs=[pl.BlockSpec((n_slab, PACK_TILE, PACK_W), lambda i: (0, i, 0))],
        out_specs=pl.BlockSpec((PACK_TILE, PACK_W), lambda i: (i, 0)),
        out_shape=jax.ShapeDtypeStruct((n, PACK_W), F32), compiler_params=_cparams(1),
    )(slabs)


def adamw(g_slabs, w, m, v, name):
    n_slab, n, _ = g_slabs.shape

    def body(g_ref, w_ref, m_ref, v_ref, go_ref, d_ref, mo_ref, vo_ref):
        g = g_ref[0].astype(F32)
        for j in range(1, n_slab):
            g = g + g_ref[j].astype(F32)
        m_new = ADAM_B1 * m_ref[...] + (1.0 - ADAM_B1) * g
        v_new = ADAM_B2 * v_ref[...] + (1.0 - ADAM_B2) * (g * g)
        m_hat = m_new / (1.0 - ADAM_B1 ** ADAM_STEP)
        v_hat = v_new / (1.0 - ADAM_B2 ** ADAM_STEP)
        go_ref[...] = g
        d_ref[...] = -ADAM_LR * (m_hat / (jnp.sqrt(v_hat) + ADAM_EPS) + ADAM_WD * w_ref[...])
        mo_ref[...] = m_new
        vo_ref[...] = v_new

    flat = pl.BlockSpec((PACK_TILE, PACK_W), lambda i: (i, 0))
    return pl.pallas_call(
        body, name=name, grid=(n // PACK_TILE,),
        in_specs=[pl.BlockSpec((n_slab, PACK_TILE, PACK_W), lambda i: (0, i, 0)), flat, flat, flat],
        out_specs=[flat] * 4, out_shape=[jax.ShapeDtypeStruct((n, PACK_W), F32)] * 4, compiler_params=_cparams(1),
    )(g_slabs, w, m, v)


def all_gather(x, name):
    m_per, n = x.shape

    def body(x_ref, out_ref, send_sems, recv_sems, local_sem):
        px, py, pc = lax.axis_index("x"), lax.axis_index("y"), lax.axis_index("c")
        me, sibling = (px, py, pc), (px, py, 1 - pc)
        chips = [(1 - px, py), (px, 1 - py), (1 - px, 1 - py)]

        def rows(bx, by, bc):
            return out_ref.at[pl.ds((4 * bx + 2 * by + bc) * m_per, m_per), :]

        def copy(k, block, to, src=None):
            return pltpu.make_async_remote_copy(
                src_ref=rows(*block) if src is None else src, dst_ref=rows(*block),
                send_sem=send_sems.at[k], recv_sem=recv_sems.at[k], device_id=to, device_id_type=MESH)

        mine = pltpu.make_async_copy(x_ref, rows(*me), local_sem)
        mine.start()
        first = [copy(0, me, sibling, src=x_ref)]
        first += [copy(1 + j, me, (*chip, pc), src=x_ref) for j, chip in enumerate(chips)]
        for cp in first:
            cp.start()
        passed = [copy(4 + j, (*chip, pc), sibling) for j, chip in enumerate(chips)]
        for j, chip in enumerate(chips):
            copy(1 + j, (*chip, pc), me).wait_recv()
            passed[j].start()
        copy(0, sibling, me).wait_recv()
        for j, chip in enumerate(chips):
            copy(4 + j, (*chip, 1 - pc), me).wait_recv()
        for cp in first + passed:
            cp.wait_send()
        mine.wait()

    return pl.pallas_call(
        body, name=name, out_shape=jax.ShapeDtypeStruct((N_DEV * m_per, n), x.dtype),
        in_specs=[pl.BlockSpec(memory_space=pl.ANY)], out_specs=pl.BlockSpec(memory_space=pl.ANY),
        scratch_shapes=[pltpu.SemaphoreType.DMA((7,)), pltpu.SemaphoreType.DMA((7,)), pltpu.SemaphoreType.DMA],
    )(x)


N_CHIP = 4


def pair_swap(x, name):
    def body(x_ref, out_ref, send_sem, recv_sem):
        sibling = (lax.axis_index("x"), lax.axis_index("y"), 1 - lax.axis_index("c"))
        copy = pltpu.make_async_remote_copy(src_ref=x_ref, dst_ref=out_ref, send_sem=send_sem, recv_sem=recv_sem,
                                            device_id=sibling, device_id_type=MESH)
        copy.start()
        copy.wait()

    return pl.pallas_call(
        body, name=name, out_shape=jax.ShapeDtypeStruct(x.shape, x.dtype),
        in_specs=[pl.BlockSpec(memory_space=pl.ANY)], out_specs=pl.BlockSpec(memory_space=pl.ANY),
        scratch_shapes=[pltpu.SemaphoreType.DMA, pltpu.SemaphoreType.DMA],
    )(x)


def pair_add(a, b, name):
    n_slab, n, _ = a.shape

    def body(a_ref, b_ref, o_ref):
        o_ref[...] = (a_ref[...].astype(F32) + b_ref[...].astype(F32)).astype(o_ref.dtype)

    spec = pl.BlockSpec((1, PACK_TILE, PACK_W), lambda s, i: (s, i, 0))
    return pl.pallas_call(
        body, name=name, grid=(n_slab, n // PACK_TILE), in_specs=[spec, spec], out_specs=spec,
        out_shape=jax.ShapeDtypeStruct(a.shape, a.dtype), compiler_params=_cparams(2),
    )(a, b)


def chip_all_to_all(x, name):
    def body(x_ref, out_ref, send_sems, recv_sems, local_sem):
        px, py, pc = lax.axis_index("x"), lax.axis_index("y"), lax.axis_index("c")
        mine_idx = 2 * px + py
        local = pltpu.make_async_copy(x_ref.at[mine_idx], out_ref.at[mine_idx], local_sem)
        local.start()
        copies = []
        for k, (fx, fy) in enumerate(((0, 1), (1, 0), (1, 1))):
            qx, qy = px ^ fx, py ^ fy
            peer_idx = 2 * qx + qy
            copies.append((
                pltpu.make_async_remote_copy(
                    src_ref=x_ref.at[peer_idx], dst_ref=out_ref.at[mine_idx], send_sem=send_sems.at[k],
                    recv_sem=recv_sems.at[k], device_id=(qx, qy, pc), device_id_type=MESH),
                pltpu.make_async_remote_copy(
                    src_ref=x_ref.at[peer_idx], dst_ref=out_ref.at[peer_idx], send_sem=send_sems.at[k],
                    recv_sem=recv_sems.at[k], device_id=(qx, qy, pc), device_id_type=MESH)))
        for send, _ in copies:
            send.start()
        for _, landing in copies:
            landing.wait_recv()
        for send, _ in copies:
            send.wait_send()
        local.wait()

    return pl.pallas_call(
        body, name=name, out_shape=jax.ShapeDtypeStruct(x.shape, x.dtype),
        in_specs=[pl.BlockSpec(memory_space=pl.ANY)], out_specs=pl.BlockSpec(memory_space=pl.ANY),
        scratch_shapes=[pltpu.SemaphoreType.DMA((3,)), pltpu.SemaphoreType.DMA((3,)), pltpu.SemaphoreType.DMA],
    )(x)


IN_OFFSETS = {}
_off = 0
for _name, _width in (("mla_q", 256), ("mla_kv", 128), ("mla_kr", 32), ("gla_q", 512), ("gla_k", 512), ("gla_v", 512),
                      ("gla_g", 512), ("gla_rf", 16), ("gla_rb", 16), ("ret_q", 512), ("ret_k", 512), ("ret_v", 512),
                      ("ret_g", 512), ("gate_mla", 1024), ("gate_gla", 1024), ("gate_ret", 1024)):
    IN_OFFSETS[_name] = (_off, _off + _width)
    _off += _width
N_IN = _off

P_GLA, P_RET, P_GATE, P_MLAQ, P_MLAKV, P_MLAKR, P_RANK, P_END = 0, 2048, 4096, 7168, 7424, 7552, 7680, 7808


def _pad_in_proj(w):
    def cols(a, b):
        return w[:, IN_OFFSETS[a][0]:IN_OFFSETS[b][1]]

    def z(n):
        return jnp.zeros((w.shape[0], n), w.dtype)

    return jnp.concatenate([cols("gla_q", "gla_g"), cols("ret_q", "ret_g"), cols("gate_mla", "gate_ret"),
                            cols("mla_q", "mla_kv"), z(MLA_NOPE), cols("mla_kr", "mla_kr"),
                            z(HEAD_PAD - MLA_QK), cols("gla_rf", "gla_rb"), z(HEAD_PAD - 2 * GLA_RANK),
                            z(N_IN_PAD - P_END)], axis=1)


def _pad_last(a, n):
    return jnp.pad(a, [(0, 0)] * (a.ndim - 1) + [(0, n - a.shape[-1])])


def _position_tables(tc, t):
    pos = jnp.arange(t)
    inv = ROPE_THETA ** (-jnp.arange(MLA_ROPE // 4, dtype=F32) * 2.0 / (MLA_ROPE // 2))
    ang_r = (pos // GRID_W).astype(F32)[:, None] * inv[None, :]
    ang_c = (pos % GRID_W).astype(F32)[:, None] * inv[None, :]
    z8, z32, z64 = jnp.zeros((t, 8), F32), jnp.zeros((t, 32), F32), jnp.zeros((t, 64), F32)
    lat_c = jnp.concatenate([jnp.ones((t, 64), F32), jnp.cos(ang_r), jnp.cos(ang_r), jnp.cos(ang_c), jnp.cos(ang_c),
                             z32], axis=1)
    lat_sn = jnp.concatenate([z64, -jnp.sin(ang_r), z8, -jnp.sin(ang_c), z8, z32], axis=1)
    lat_sp = jnp.concatenate([z64, z8, jnp.sin(ang_r), z8, jnp.sin(ang_c), z32], axis=1)
    ctx_c = jnp.concatenate([jnp.ones((tc, MLA_QK), F32), jnp.zeros((tc, HEAD_PAD - MLA_QK), F32)], axis=1)
    ctx_z = jnp.zeros((tc, HEAD_PAD), F32)
    rinv = 1.0 / (RET_THETA ** jnp.linspace(0.0, 1.0, RET_DK // 2, dtype=F32))
    rang = jnp.arange(tc + t).astype(F32)[:, None] * rinv[None, :]
    return dict(c=jnp.concatenate([ctx_c, lat_c]), sn=jnp.concatenate([ctx_z, lat_sn]),
                sp=jnp.concatenate([ctx_z, lat_sp]),
                rc=jnp.concatenate([jnp.cos(rang), jnp.cos(rang)], axis=1),
                rs=jnp.concatenate([-jnp.sin(rang), jnp.sin(rang)], axis=1))


def _heads(x):
    return [x[:, h * HEAD_PAD:(h + 1) * HEAD_PAD] for h in range(x.shape[1] // HEAD_PAD)]


def _mla_rope(x, c, sn, sp):
    return x * c + _roll(x, HEAD_PAD - 8, 1) * sn + _roll(x, 8, 1) * sp


def _norm_mod_fn(shift_row, scale_row):
    def fn(h, mod, w):
        return (_rms(h, D, w) * (1.0 + mod[scale_row:scale_row + 1]) + mod[shift_row:shift_row + 1],)
    return fn


def _resid_fn(gate_row):
    def fn(h, y, mod):
        return (h + mod[gate_row:gate_row + 1] * y,)
    return fn


def _resid_norm_fn(gate_row, shift_row, scale_row):
    def fn(h, y, mod, w):
        h1 = h + mod[gate_row:gate_row + 1] * y
        return h1, _norm_mod_fn(shift_row, scale_row)(h1, mod, w)[0]
    return fn


def _mla_prep_fn(x, c, sn, sp, q_norm_a, w_qb, q_norm, kv_norm_a, w_k, w_v, k_norm):
    cq, ckv = x[:, :MLA_Q_LORA], x[:, MLA_Q_LORA:MLA_Q_LORA + MLA_KV_LORA]
    kr = x[:, MLA_Q_LORA + MLA_KV_LORA:]
    qf = _bdot(_rms(cq, MLA_Q_LORA, q_norm_a), w_qb, "nn")
    q = jnp.concatenate([_mla_rope(_rms(qh, MLA_QK, q_norm), c, sn, sp) for qh in _heads(qf)], axis=1)
    xkv = _rms(ckv, MLA_KV_LORA, kv_norm_a)
    kf = _bdot(xkv, w_k, "nn")
    k = jnp.concatenate([_mla_rope(_rms(kh + kr, MLA_QK, k_norm), c, sn, sp) for kh in _heads(kf)], axis=1)
    return q * (ATT_SCALE * LOG2E), k, _bdot(xkv, w_v, "nn")


def _decay_fn(x, w2, b):
    la = _log_sigmoid(_bdot(x[:, :HEAD_PAD], w2, "nn") + b) * (1.0 / GLA_NORMALIZER)
    return la[:, :GLA_HEADS * GLA_DK], la[:, GLA_HEADS * GLA_DK:]


def _gla_out_fn(o_f, o_b, g, w):
    y = jnp.concatenate([_rms(oh, HEAD_PAD, w) for oh in _heads(o_f + o_b)], axis=1)
    return (y * _silu(g),)


def _ret_out_fn(o_f, o_b, g):
    y = jnp.concatenate([_rms(oh, HEAD_PAD) for oh in _heads(o_f + o_b)], axis=1)
    return (y * _silu(g),)


def _merge_fn(z0, z1, z2, g0a, g0b, g1a, g1b, g2a, g2b, bg):
    out = 0.0
    for n, (z, ga, gb) in enumerate(((z0, g0a, g0b), (z1, g1a, g1b), (z2, g2a, g2b))):
        out = out + jax.nn.sigmoid(jnp.concatenate([ga, gb], axis=1) + bg[n:n + 1]) * z
    return (out,)


def _layer(l, h, mod, w, tabs, tc):
    nct = tc // ROW_TILE
    tag = f"_l{l}"
    row = lambda a: a[l][None]
    pieces = rowwise("norm1" + tag, _norm_mod_fn(0, 1), [h], [mod], [row(w["norm1_w"])], [D], nct,
                     then=("pieces", _pad_in_proj(w["w_in"][l]), "in_proj" + tag))
    piece = lambda start: pieces[start // PIECE_W]

    w_qb = _pad_last(w["mla_w_qb"][l].reshape(MLA_Q_LORA, MLA_HEADS, MLA_QK), HEAD_PAD).reshape(MLA_Q_LORA, -1)
    w_kvb = w["mla_w_kvb"][l].reshape(MLA_KV_LORA, MLA_HEADS, MLA_NOPE + MLA_V)
    w_k = _pad_last(w_kvb[:, :, :MLA_NOPE], HEAD_PAD).reshape(MLA_KV_LORA, -1)
    w_v = _pad_last(w_kvb[:, :, MLA_NOPE:], HEAD_PAD).reshape(MLA_KV_LORA, -1)
    rope = [tabs["c"], tabs["sn"], tabs["sp"]]
    y_mla = rowwise("mla_prep" + tag, _mla_prep_fn, [piece(P_MLAQ)] + rope, [],
                    [row(w["mla_q_norm_a"]), w_qb, _pad_last(row(w["mla_q_norm"]), HEAD_PAD),
                     row(w["mla_kv_norm_a"]), w_k, w_v, _pad_last(row(w["mla_k_norm"]), HEAD_PAD)],
                    [MLA_HEADS * HEAD_PAD] * 3, nct, diff_rows=[True, False, False, False],
                    then=("attention", tc, "attn" + tag))
    wb_mla = _pad_last(w["w_branch"][l, 0].reshape(MLA_HEADS, MLA_V, D).transpose(0, 2, 1), HEAD_PAD)
    wb_mla = wb_mla.transpose(0, 2, 1).reshape(MLA_HEADS * HEAD_PAD, D)

    w2 = jnp.zeros((HEAD_PAD, 2 * GLA_HEADS * GLA_DK), F32)
    w2 = w2.at[:GLA_RANK, :GLA_HEADS * GLA_DK].set(w["gla_w_gk2"][l, 0])
    w2 = w2.at[GLA_RANK:2 * GLA_RANK, GLA_HEADS * GLA_DK:].set(w["gla_w_gk2"][l, 1])
    la_f, la_b = rowwise("gla_decay" + tag, _decay_fn, [piece(P_RANK)], [],
                         [w2, w["gla_b_gk"][l].reshape(1, -1)], [GLA_HEADS * GLA_DK] * 2, nct)
    gq, gk, gv, gg = [piece(P_GLA + n * PIECE_W) for n in range(4)]
    o_f = scan("gla", True, gq, gk, gv, la_f, tc, "gla_fw" + tag)
    o_b = scan("gla", False, gq, gk, gv, la_b, tc, "gla_bw" + tag)
    z_gla = rowwise("gla_out" + tag, _gla_out_fn, [o_f, o_b, gg], [], [row(w["gla_o_norm"])], [512], nct,
                    then=("linear", w["w_branch"][l, 1], "branch_gla" + tag))

    rq, rk, rv, rg = [piece(P_RET + n * PIECE_W) for n in range(4)]
    rd = jnp.broadcast_to(w["ret_decay"][l][:, :, None, None], (2, RET_HEADS, 8, HEAD_PAD))
    turn = (tabs["rc"], tabs["rs"])
    r_f = scan("ret", True, rq, rk, rv, rd[0], tc, "ret_fw" + tag, rot=turn)
    r_b = scan("ret", False, rq, rk, rv, rd[1], tc, "ret_bw" + tag, rot=turn)
    z_ret = rowwise("ret_out" + tag, _ret_out_fn, [r_f, r_b, rg], [], [], [512], nct,
                    then=("linear", w["w_branch"][l, 2], "branch_ret" + tag))

    z = [linear(y_mla, wb_mla, "branch_mla" + tag), z_gla, z_ret]
    gates = [piece(P_GATE + n * PIECE_W) for n in range(6)]
    y = rowwise("merge" + tag, _merge_fn, z + gates, [], [_pad_rows(w["b_gate"][l], 8)], [D], nct,
                then=("linear", w["w_out"][l], "w_out" + tag))
    h, a2 = rowwise("resid1_norm2" + tag, _resid_norm_fn(2, 3, 4), [h, y], [mod], [row(w["norm2_w"])], [D, D], nct)
    gate = linear(a2, w["w_ffn_in"][l][:, :D_FF], "ffn_gate" + tag)
    up = linear(a2, w["w_ffn_in"][l][:, D_FF:], "ffn_up" + tag)
    f = conv_ffn_out(gate, up, _pad_rows(w["w_dw"][l], 8), row(w["b_dw"]), w["w_ffn_out"][l], tc, "ffn_mid" + tag)
    return rowwise("resid2" + tag, _resid_fn(5), [h, f], [mod], [], [D], nct)[0]


def _pad_rows(a, n):
    return jnp.pad(a, [(0, n - a.shape[0])] + [(0, 0)] * (a.ndim - 1))


def local_loss(w, mod, x, ctx, target):
    tc, t = ctx.shape[0], x.shape[0]
    tabs = _position_tables(tc, t)
    h = jnp.concatenate([ctx, x], axis=0)
    for l in range(DEPTH):
        h = _layer(l, h, mod[l], w, tabs, tc)
    return loss_head(h, target, tc, "loss_head")


ADA_ROWS = 16


def ada_forward(cond_in, w_ada, b_loc):
    cols = w_ada.shape[2]

    def body(x_ref, w_ref, b_ref, o_ref):
        s = _silu(x_ref[...])
        for l in range(DEPTH):
            o_ref[l] = _dg(s, w_ref[l], "nn") + b_ref[l]

    return pl.pallas_call(
        body, name="ada_forward", out_shape=jax.ShapeDtypeStruct((DEPTH, ADA_ROWS, cols), F32),
        compiler_params=pltpu.CompilerParams(vmem_limit_bytes=VMEM_LIMIT_BYTES),
    )(cond_in, w_ada, b_loc)


def ada_backward(cond_in, g_loc, dmod_own, w_ada):
    cols = w_ada.shape[2]

    def body(x_ref, g_ref, own_ref, w_ref, gw_ref, dc_ref, gb_ref):
        x = x_ref[...]
        s = _silu(x)
        dcond = jnp.zeros((8, D), F32)
        for l in range(DEPTH):
            g_ctx = jnp.sum(g_ref[2 * l], axis=0, keepdims=True)
            g_rows = jnp.concatenate([g_ref[2 * l + 1], jnp.broadcast_to(g_ctx, (8, cols))], axis=0)
            keep = lax.broadcasted_iota(jnp.int32, (ADA_ROWS, cols), 0) <= N_DEV
            gw_ref[l] = _dg(s, jnp.where(keep, g_rows, 0.0), "tn")
            dcond = dcond + _dg(jnp.broadcast_to(g_ctx, (8, cols)), w_ref[l], "nt")
            gb_ref[l:l + 1, :] = own_ref[2 * l:2 * l + 1, :] + own_ref[2 * l + 1:2 * l + 2, :]
        xc = x[N_DEV:N_DEV + 1]
        sig = jax.nn.sigmoid(xc)
        dc_ref[...] = dcond[0:1] * (sig * (1.0 + xc * (1.0 - sig)))

    return pl.pallas_call(
        body, name="ada_backward",
        out_shape=[jax.ShapeDtypeStruct(w_ada.shape, F32), jax.ShapeDtypeStruct((1, D), F32),
                   jax.ShapeDtypeStruct((DEPTH, 6 * D), F32)],
        compiler_params=pltpu.CompilerParams(vmem_limit_bytes=VMEM_LIMIT_BYTES),
    )(cond_in, g_loc, dmod_own, w_ada)


WEIGHTS = ["c_ctx", "w_ada", "b_ada", "norm1_w", "norm2_w", "w_in", "b_gate", "mla_q_norm_a", "mla_w_qb",
           "mla_kv_norm_a", "mla_w_kvb", "mla_q_norm", "mla_k_norm", "gla_w_gk2", "gla_b_gk", "gla_o_norm",
           "ret_decay", "w_branch", "w_out", "w_ffn_in", "w_dw", "b_dw", "w_ffn_out"]
INPUTS = ["x", "c", "ctx"] + WEIGHTS + ["loss_target"] + ["m_" + n for n in WEIGHTS] + ["v_" + n for n in WEIGHTS]
BIG = {"w_in": 2, "mla_w_qb": 2, "mla_w_kvb": 2, "w_branch": 3, "w_out": 1, "w_ffn_in": 2, "w_ffn_out": 1}
SMALL_SHARDED = {"b_gate": 2, "gla_w_gk2": 3, "gla_b_gk": 2, "w_dw": 2}
SMALL = ["c_ctx", "b_ada", "norm1_w", "norm2_w", "b_gate", "mla_q_norm_a", "mla_kv_norm_a", "mla_q_norm", "mla_k_norm",
         "gla_w_gk2", "gla_b_gk", "gla_o_norm", "ret_decay", "w_dw", "b_dw"]


def _entry_rows(size, align):
    return -(-size // (PACK_W * align)) * align


def _pack(arrays, rows, dtype, align, lead=0):
    parts = []
    for a in arrays:
        head = a.shape[:lead]
        size = math.prod(a.shape[lead:])
        r = _entry_rows(size, align)
        if r * PACK_W == size:
            parts.append(a.astype(dtype).reshape(head + (r, PACK_W)))
        else:
            flat = jnp.pad(a.astype(dtype).reshape(head + (size,)), [(0, 0)] * lead + [(0, r * PACK_W - size)])
            parts.append(flat.reshape(head + (r, PACK_W)))
    used = sum(p.shape[lead] for p in parts)
    if rows > used:
        parts.append(jnp.zeros(parts[0].shape[:lead] + (rows - used, PACK_W), dtype))
    return jnp.concatenate(parts, axis=lead)


def _pack_rows(shapes, align, multiple):
    used = sum(_entry_rows(math.prod(s), align) for s in shapes)
    return -(-used // multiple) * multiple


def _unpack(pack, shapes, align):
    head = pack.shape[:-2]
    out, off = [], 0
    for shape in shapes:
        size = math.prod(shape)
        r = _entry_rows(size, align)
        block = lax.slice_in_dim(pack, off, off + r, axis=len(head))
        if r * PACK_W != size:
            block = block.reshape(head + (r * PACK_W,))[..., :size]
        out.append(block.reshape(head + tuple(shape)))
        off += r
    return out


def _join_shards(stacked, axis):
    moved = jnp.moveaxis(stacked, 0, axis)
    shape = list(moved.shape)
    return moved.reshape(shape[:axis] + [shape[axis] * shape[axis + 1]] + shape[axis + 2:])


def _split_shards(full, axis):
    shape = list(full.shape)
    split = full.reshape(shape[:axis] + [N_DEV, shape[axis] // N_DEV] + shape[axis + 1:])
    return jnp.moveaxis(split, axis, 0)


def _gather_shards(local, axes, dtype, rows_multiple, name):
    names = list(axes)
    shapes = [local[n].shape for n in names]
    rows = _pack_rows(shapes, rows_multiple, rows_multiple)
    gathered = all_gather(_pack([local[n] for n in names], rows, dtype, rows_multiple), name)
    stacked = _unpack(gathered.reshape(N_DEV, rows, PACK_W), shapes, rows_multiple)
    return {n: _join_shards(s, axes[n]).astype(F32) for n, s in zip(names, stacked)}


def kernel(*args):
    a = dict(zip(INPUTS, args))
    me = 4 * lax.axis_index("x") + 2 * lax.axis_index("y") + lax.axis_index("c")
    cols = a["w_ada"].shape[2]

    small_names = list(SMALL_SHARDED)
    small_local = [a[n].shape for n in small_names]
    first_rows = _pack_rows([a["c"].shape] + small_local, 8, 8)
    first = all_gather(_pack([a["c"]] + [a[n] for n in small_names], first_rows, F32, 8), "gather_small")
    first = _unpack(first.reshape(N_DEV, first_rows, PACK_W), [a["c"].shape] + small_local, 8)
    c_all = first[0][:, 0]

    cond_in = jnp.concatenate([c_all, a["c_ctx"][None], jnp.zeros((ADA_ROWS - N_DEV - 1, D), F32)], axis=0)
    b_loc = lax.dynamic_slice_in_dim(a["b_ada"], me * cols, cols, axis=1)[:, None, :]
    mod_loc = ada_forward(cond_in, a["w_ada"], b_loc)
    mod_all = all_gather(mod_loc.reshape(DEPTH * ADA_ROWS, cols), "gather_mod")
    mod_all = mod_all.reshape(N_DEV, DEPTH, ADA_ROWS, cols).transpose(1, 2, 0, 3).reshape(DEPTH, ADA_ROWS, 6, D)
    mod_me = lax.dynamic_index_in_dim(mod_all, me, axis=1, keepdims=False)
    mod = jnp.pad(jnp.stack([mod_all[:, N_DEV], mod_me], axis=1), ((0, 0), (0, 0), (0, 2), (0, 0)))

    w = _gather_shards(a, BIG, BF16, 16, "gather_weights")
    w.update({n: _join_shards(s, SMALL_SHARDED[n]) for n, s in zip(small_names, first[1:])})
    for n in SMALL:
        if n not in SMALL_SHARDED and n not in ("c_ctx", "b_ada"):
            w[n] = a[n]

    loss, (gw, gmod, gx) = jax.value_and_grad(local_loss, argnums=(0, 1, 2))(
        w, mod, a["x"][0], a["ctx"][0], a["loss_target"][0])
    loss = lax.psum(loss, ("x", "y", "c"))

    dmod_own = gmod[:, :, :6].reshape(2 * DEPTH, 6 * D)
    g_all = all_gather(jnp.pad(dmod_own, ((0, 8 - 2 * DEPTH), (0, 0))), "gather_dmod").reshape(N_DEV, 8, 6 * D)
    g_loc = lax.dynamic_slice_in_dim(g_all[:, :2 * DEPTH], me * cols, cols, axis=2).transpose(1, 0, 2)
    g_w_ada, g_c_ctx, g_b_ada = ada_backward(cond_in, g_loc, dmod_own, a["w_ada"])

    small_part = dict(gw, c_ctx=g_c_ctx, b_ada=g_b_ada)
    small_shapes = [a[n].shape if n not in SMALL_SHARDED else gw[n].shape for n in SMALL]
    rows = _pack_rows(small_shapes, 8, PACK_TILE)
    parts = all_gather(_pack([small_part[n] for n in SMALL], rows, F32, 8), "gather_small_grads")
    small_sum = _unpack(slab_sum(parts.reshape(N_DEV, rows, PACK_W), "sum_small_grads"), small_shapes, 8)
    g_small = {}
    for n, g in zip(SMALL, small_sum):
        if n in SMALL_SHARDED:
            ax = SMALL_SHARDED[n]
            g = lax.dynamic_slice_in_dim(g, me * a[n].shape[ax], a[n].shape[ax], axis=ax)
        g_small[n] = g

    big_rows = _pack_rows([a[n].shape for n in BIG], 16, PACK_TILE)
    slabs = _pack([_split_shards(gw[n], ax) for n, ax in BIG.items()], big_rows, BF16, 16, lead=1)
    by_core = slabs.reshape(N_CHIP, 2, big_rows, PACK_W)
    my_core = lax.axis_index("c")
    keep = lax.dynamic_index_in_dim(by_core, my_core, axis=1, keepdims=False)
    give = lax.dynamic_index_in_dim(by_core, 1 - my_core, axis=1, keepdims=False)
    pair_sum = pair_add(keep, pair_swap(give, "swap_grads"), "add_pair_grads")
    landed = chip_all_to_all(pair_sum, "scatter_grads")

    def update(names, g_slabs, rows, align, label):
        shapes = [a[n].shape for n in names]
        packs = [_pack([a[pre + n] for n in names], rows, F32, align) for pre in ("", "m_", "v_")]
        outs = adamw(g_slabs, *packs, label)
        return [dict(zip(names, _unpack(o, shapes, align))) for o in outs]

    res_big = update(list(BIG), landed, big_rows, 16, "adamw_big")
    ada_rows = _pack_rows([a["w_ada"].shape], 8, PACK_TILE)
    res_ada = update(["w_ada"], _pack([g_w_ada], ada_rows, F32, 8)[None], ada_rows, 8, "adamw_ada")
    small_rows = _pack_rows([a[n].shape for n in SMALL], 8, PACK_TILE)
    res_small = update(SMALL, _pack([g_small[n] for n in SMALL], small_rows, F32, 8)[None], small_rows, 8,
                       "adamw_small")

    outs = [loss, gx[None]]
    for k in range(4):
        merged = {**res_big[k], **res_ada[k], **res_small[k]}
        outs += [merged[n] for n in WEIGHTS]
    return tuple(outs)
```

```python
import functools
import math

import jax
import jax.numpy as jnp
from jax import lax
from jax.experimental import pallas as pl
from jax.experimental.pallas import tpu as pltpu

F32 = jnp.float32
BF16 = jnp.bfloat16

N_DEV = 8
D = 1024
DEPTH = 2
GRID_W = 64
MLA_HEADS = 8
MLA_NOPE = 64
MLA_ROPE = 32
MLA_QK = 96
MLA_V = 64
MLA_Q_LORA = 256
MLA_KV_LORA = 128
GLA_HEADS = 4
GLA_DK = 128
GLA_RANK = 16
GLA_NORMALIZER = 16.0
RET_HEADS = 4
RET_DK = 128
BRANCH_W = 512
D_FF = 2816
CHUNK = 64
ROPE_THETA = 10000.0
RET_THETA = 10000.0
EPS = 1e-6
HEAD_PAD = 128
N_IN_PAD = 8192

ADAM_LR = 0.001
ADAM_B1 = 0.9
ADAM_B2 = 0.999
ADAM_EPS = 1e-08
ADAM_WD = 0.01
ADAM_STEP = 10

ROW_TILE = 256
SCAN_CHUNKS = ROW_TILE // CHUNK
VMEM_LIMIT_BYTES = 56 * 1024 * 1024
MESH = pl.DeviceIdType.MESH


def _cparams(n_axes):
    return pltpu.CompilerParams(dimension_semantics=("arbitrary",) * n_axes, vmem_limit_bytes=VMEM_LIMIT_BYTES)


def _pick(dim, cands):
    for cand in cands:
        if dim % cand == 0:
            return cand
    return dim


_DOT_DIMS = {"nn": (((1,), (0,)), ((), ())), "nt": (((1,), (1,)), ((), ())), "tn": (((0,), (0,)), ((), ()))}


def _dg(a, b, mode):
    return lax.dot_general(a.astype(BF16), b.astype(BF16), _DOT_DIMS[mode], preferred_element_type=F32)


def _bdot(a, b, mode):
    @jax.custom_vjp
    def f(a, b):
        return _dg(a, b, mode)

    def fwd(a, b):
        return _dg(a, b, mode), (a, b)

    def bwd(res, g):
        a, b = res
        if mode == "nn":
            return _dg(g, b, "nt").astype(a.dtype), _dg(a, g, "tn").astype(b.dtype)
        if mode == "nt":
            return _dg(g, b, "nn").astype(a.dtype), _dg(g, a, "tn").astype(b.dtype)
        return _dg(b, g, "nt").astype(a.dtype), _dg(a, g, "nn").astype(b.dtype)

    f.defvjp(fwd, bwd)
    return f(a, b)


def _roll(x, shift, axis):
    n = x.shape[axis]
    shift = shift % n

    @jax.custom_vjp
    def f(x):
        return pltpu.roll(x, shift, axis)

    def fwd(x):
        return pltpu.roll(x, shift, axis), None

    def bwd(_, g):
        return (pltpu.roll(g, (n - shift) % n, axis),)

    f.defvjp(fwd, bwd)
    return f(x)


@jax.custom_jvp
def _log_sigmoid(x):
    return jnp.minimum(x, 0.0) - jnp.log(1.0 + jnp.exp(-jnp.abs(x)))


@_log_sigmoid.defjvp
def _log_sigmoid_jvp(primals, tangents):
    (x,), (t,) = primals, tangents
    return _log_sigmoid(x), t * jax.nn.sigmoid(-x)


def _rms(x, n, w=None):
    y = x * lax.rsqrt(jnp.sum(x * x, axis=-1, keepdims=True) * (1.0 / n) + EPS)
    return y if w is None else y * w


def _silu(x):
    return x * jax.nn.sigmoid(x)


def _gelu_tanh(x):
    return 0.5 * x * (1.0 + jnp.tanh(math.sqrt(2.0 / math.pi) * (x + 0.044715 * (x * x * x))))


MM_RESIDENT_BYTES = 6 * 1024 * 1024
MM_TILE_BYTES = 9 * 1024 * 1024


def _mm(a, b, mode, name):
    if mode == "nn":
        (m, k), (_, n) = a.shape, b.shape
    elif mode == "nt":
        (m, k), (n, _) = a.shape, b.shape
    else:
        (k, m), (_, n) = a.shape, b.shape
    tm = _pick(m, (1024, 768, 1408, 512, 256, 128))
    tn = _pick(n, (1024, 1408, 512, 256, 128))
    tk = _pick(k, (1408, 1024, 768, 512, 256, 128) if mode == "tn" else (1024, 768, 1408, 512, 256, 128))
    whole_b = k * n * b.dtype.itemsize <= MM_RESIDENT_BYTES
    if mode != "tn" and whole_b and tm * max(k, n) * 4 <= MM_TILE_BYTES:
        tn, tk = n, k
    nk = k // tk
    if mode == "nn":
        a_spec = pl.BlockSpec((tm, tk), lambda i, j, kk: (i, kk))
        b_spec = pl.BlockSpec((tk, tn), lambda i, j, kk: (kk, j))
    elif mode == "nt":
        a_spec = pl.BlockSpec((tm, tk), lambda i, j, kk: (i, kk))
        b_spec = pl.BlockSpec((tn, tk), lambda i, j, kk: (j, kk))
    else:
        a_spec = pl.BlockSpec((tk, tm), lambda i, j, kk: (kk, i))
        b_spec = pl.BlockSpec((tk, tn), lambda i, j, kk: (kk, j))

    def body(a_ref, b_ref, o_ref):
        kk = pl.program_id(2)
        part = _dg(a_ref[...], b_ref[...], mode)
        if nk == 1:
            o_ref[...] = part
        else:
            @pl.when(kk == 0)
            def _():
                o_ref[...] = part

            @pl.when(kk != 0)
            def _():
                o_ref[...] += part

    return pl.pallas_call(
        body, name=name, grid=(m // tm, n // tn, nk),
        in_specs=[a_spec, b_spec], out_specs=pl.BlockSpec((tm, tn), lambda i, j, kk: (i, j)),
        out_shape=jax.ShapeDtypeStruct((m, n), F32),
        compiler_params=_cparams(3),
    )(a, b)


def linear(x, w, name):
    @jax.custom_vjp
    def op(x, w):
        return _mm(x, w.astype(BF16), "nn", name + "_f")

    def fwd(x, w):
        wb = w.astype(BF16)
        return _mm(x, wb, "nn", name + "_f"), (x, wb)

    def bwd(res, g):
        x, wb = res
        return _mm(g, wb, "nt", name + "_dx"), _mm(x, g, "tn", name + "_dw")

    op.defvjp(fwd, bwd)
    return op(x, w)


PIECE_W = 512
PIECE_ROWS = 384
PIECE_GROUP = 4


def _mm_split(a, wb, name):
    (r, k), n = a.shape, wb.shape[1] // PIECE_W
    tm = _pick(r, (PIECE_ROWS, ROW_TILE))
    width = PIECE_GROUP * PIECE_W

    n_tiles = r // tm

    def body(a_ref, w_ref, *out_refs):
        j = pl.program_id(0)
        res = _dg(a_ref[...], w_ref[...], "nn")
        for group in range(n // PIECE_GROUP):
            @pl.when(j == group)
            def _(group=group):
                for p in range(PIECE_GROUP):
                    out_refs[group * PIECE_GROUP + p][...] = res[:, p * PIECE_W:(p + 1) * PIECE_W]

    def out_spec(jj):
        group = jj // PIECE_GROUP
        return pl.BlockSpec((tm, PIECE_W),
                            lambda j, i: (jnp.where(j == group, i, jnp.where(j < group, 0, n_tiles - 1)), 0))

    return pl.pallas_call(
        body, name=name, grid=(n // PIECE_GROUP, n_tiles),
        in_specs=[pl.BlockSpec((tm, k), lambda j, i: (i, 0)), pl.BlockSpec((k, width), lambda j, i: (0, j))],
        out_specs=[out_spec(jj) for jj in range(n)],
        out_shape=[jax.ShapeDtypeStruct((r, PIECE_W), F32)] * n, compiler_params=_cparams(2),
    )(a, wb)


def _mm_join(gs, wb, name):
    n, (r, _), k = len(gs), gs[0].shape, wb.shape[0]
    tm = _pick(r, (PIECE_ROWS, ROW_TILE))
    n_groups = n // PIECE_GROUP

    def body(*refs):
        g_refs, w_ref, o_ref = refs[:n], refs[n], refs[n + 1]
        j = pl.program_id(1)
        for group in range(n_groups):
            @pl.when(j == group)
            def _(group=group):
                g = jnp.concatenate([g_refs[group * PIECE_GROUP + p][...].astype(BF16) for p in range(PIECE_GROUP)],
                                    axis=1)
                part = _dg(g, w_ref[...], "nt")
                if group == 0:
                    o_ref[...] = part
                else:
                    o_ref[...] += part

    return pl.pallas_call(
        body, name=name, grid=(r // tm, n_groups),
        in_specs=[pl.BlockSpec((tm, PIECE_W), lambda i, j: (i, 0))] * n
        + [pl.BlockSpec((k, PIECE_GROUP * PIECE_W), lambda i, j: (0, j))],
        out_specs=pl.BlockSpec((tm, k), lambda i, j: (i, 0)), out_shape=jax.ShapeDtypeStruct((r, k), F32),
        compiler_params=_cparams(2),
    )(*gs, wb)


def _mm_join_tn(a, gs, name):
    n, (r, k) = len(gs), a.shape
    tk = _pick(r, (768, 512, ROW_TILE))
    width = PIECE_GROUP * PIECE_W

    def group_call(group):
        def body(a_ref, *refs):
            g_refs, o_ref = refs[:PIECE_GROUP], refs[PIECE_GROUP]
            part = _dg(a_ref[...], jnp.concatenate([g_ref[...].astype(BF16) for g_ref in g_refs], axis=1), "tn")

            @pl.when(pl.program_id(0) == 0)
            def _():
                o_ref[...] = part

            @pl.when(pl.program_id(0) != 0)
            def _():
                o_ref[...] += part

        return pl.pallas_call(
            body, name=f"{name}{group}", grid=(r // tk,),
            in_specs=[pl.BlockSpec((tk, k), lambda kk: (kk, 0))]
            + [pl.BlockSpec((tk, PIECE_W), lambda kk: (kk, 0))] * PIECE_GROUP,
            out_specs=pl.BlockSpec((k, width), lambda kk: (0, 0)),
            out_shape=jax.ShapeDtypeStruct((k, width), F32), compiler_params=_cparams(1),
        )(a, *gs[group * PIECE_GROUP:(group + 1) * PIECE_GROUP])

    return jnp.concatenate([group_call(group) for group in range(n // PIECE_GROUP)], axis=1)


def rowwise(name, fn, rows, segs, params, out_widths, nct, diff_rows=None, then=None):
    n_row, n_seg, n_par, n_out = len(rows), len(segs), len(params), len(out_widths)
    out_dtype = F32 if then is None else BF16
    diff_rows = [True] * n_row if diff_rows is None else list(diff_rows)
    r_total = rows[0].shape[0]
    n_tiles = r_total // ROW_TILE

    def seg_of(i):
        return jnp.where(i < nct, 0, 1)

    def row_spec(width):
        return pl.BlockSpec((ROW_TILE, width), lambda i: (i, 0))

    def seg_spec(shape):
        nd = len(shape)
        return pl.BlockSpec((1,) + tuple(shape[1:]), lambda i: (seg_of(i),) + (0,) * (nd - 1))

    def par_spec(shape):
        nd = len(shape)
        return pl.BlockSpec(tuple(shape), lambda i: (0,) * nd)

    in_specs = ([row_spec(r.shape[1]) for r in rows] + [seg_spec(s.shape) for s in segs]
                + [par_spec(p.shape) for p in params])

    def load(refs):
        vals = [r[...].astype(F32) for r in refs[:n_row]]
        vals += [r[0].astype(F32) for r in refs[n_row:n_row + n_seg]]
        vals += [r[...].astype(F32) for r in refs[n_row + n_seg:n_row + n_seg + n_par]]
        return vals

    def fwd_call(arrs):
        def body(*refs):
            outs = fn(*load(refs))
            for o_ref, val in zip(refs[n_row + n_seg + n_par:], outs):
                o_ref[...] = val.astype(o_ref.dtype)

        return pl.pallas_call(
            body, name=name + "_f", grid=(n_tiles,), in_specs=in_specs,
            out_specs=[row_spec(w) for w in out_widths],
            out_shape=[jax.ShapeDtypeStruct((r_total, w), out_dtype) for w in out_widths],
            compiler_params=_cparams(1),
        )(*arrs)

    d_idx = [k for k in range(n_row) if diff_rows[k]]

    def bwd_call(arrs, douts):
        n_in = n_row + n_seg + n_par

        def body(*refs):
            i = pl.program_id(0)
            vals = load(refs[:n_in])
            gs = [r[...] for r in refs[n_in:n_in + n_out]]
            out_refs = refs[n_in + n_out:]
            diff_pos = d_idx + list(range(n_row, n_in))

            def f(*dv):
                full = list(vals)
                for pos, v in zip(diff_pos, dv):
                    full[pos] = v
                return tuple(fn(*full))

            _, vjp = jax.vjp(f, *[vals[p] for p in diff_pos])
            grads = vjp(tuple(gs))
            nd = len(d_idx)
            for o_ref, g in zip(out_refs[:nd], grads[:nd]):
                o_ref[...] = g
            first_seg = jnp.logical_or(i == 0, i == nct)
            for o_ref, g in zip(out_refs[nd:nd + n_seg], grads[nd:nd + n_seg]):
                @pl.when(first_seg)
                def _(o_ref=o_ref, g=g):
                    o_ref[0] = g

                @pl.when(jnp.logical_not(first_seg))
                def _(o_ref=o_ref, g=g):
                    o_ref[0] += g
            for o_ref, g in zip(out_refs[nd + n_seg:], grads[nd + n_seg:]):
                @pl.when(i == 0)
                def _(o_ref=o_ref, g=g):
                    o_ref[...] = g

                @pl.when(i != 0)
                def _(o_ref=o_ref, g=g):
                    o_ref[...] += g

        out_specs = ([row_spec(rows[k].shape[1]) for k in d_idx] + [seg_spec(s.shape) for s in segs]
                     + [par_spec(p.shape) for p in params])
        out_shape = ([jax.ShapeDtypeStruct(rows[k].shape, F32) for k in d_idx]
                     + [jax.ShapeDtypeStruct(s.shape, F32) for s in segs]
                     + [jax.ShapeDtypeStruct(p.shape, F32) for p in params])
        return pl.pallas_call(
            body, name=name + "_b", grid=(n_tiles,),
            in_specs=in_specs + [row_spec(w) for w in out_widths],
            out_specs=out_specs, out_shape=out_shape, compiler_params=_cparams(1),
        )(*arrs, *douts)

    @jax.custom_vjp
    def op(*arrs):
        return tuple(fwd_call(arrs))

    def op_fwd(*arrs):
        return tuple(fwd_call(arrs)), arrs

    def op_bwd(arrs, douts):
        grads = list(bwd_call(arrs, douts))
        nd = len(d_idx)
        row_grads = [jnp.zeros_like(arrs[k]) for k in range(n_row)]
        for k, g in zip(d_idx, grads[:nd]):
            row_grads[k] = g
        return tuple(row_grads + grads[nd:])

    if then is None:
        op.defvjp(op_fwd, op_bwd)
        return op(*rows, *segs, *params)

    kind, w, mm_name = then

    if kind == "attention":
        @jax.custom_vjp
        def attended(*arrs):
            return attention_forward(*fwd_call(arrs), w, mm_name)[0]

        def attended_fwd(*arrs):
            qkv = fwd_call(arrs)
            o, parts = attention_forward(*qkv, w, mm_name)
            return o, (arrs, qkv, parts)

        def attended_bwd(res, do):
            arrs, qkv, parts = res
            return op_bwd(arrs, list(attention_backward(*qkv, parts, do, w, mm_name)))

        attended.defvjp(attended_fwd, attended_bwd)
        return attended(*rows, *segs, *params)

    def project(u, wb):
        if kind == "linear":
            return _mm(u, wb, "nn", mm_name + "_f")
        return tuple(_mm_split(u, wb, mm_name + "_f"))

    @jax.custom_vjp
    def fused(w, *arrs):
        return project(fwd_call(arrs)[0], w.astype(BF16))

    def fused_fwd(w, *arrs):
        u, wb = fwd_call(arrs)[0], w.astype(BF16)
        return project(u, wb), (arrs, u, wb)

    def fused_bwd(res, g):
        arrs, u, wb = res
        if kind == "linear":
            du, dw = _mm(g, wb, "nt", mm_name + "_dx"), _mm(u, g, "tn", mm_name + "_dw")
        else:
            du, dw = _mm_join(list(g), wb, mm_name + "_dx"), _mm_join_tn(u, list(g), mm_name + "_dw")
        return (dw,) + op_bwd(arrs, [du])

    fused.defvjp(fused_fwd, fused_bwd)
    return fused(w, *rows, *segs, *params)


ATT_SCALE = MLA_QK ** -0.5
LOG2E = math.log2(math.e)
ATT_KEY_CHUNKS = (768, 512, 256)


ATT_LATENT_TILE = 1024


def _query_rows_spec(row0, tq):
    return pl.BlockSpec((pl.Element(tq), pl.Element(HEAD_PAD)),
                        lambda h, i: (pl.multiple_of(row0 + i * tq, ROW_TILE), pl.multiple_of(h * HEAD_PAD, HEAD_PAD)))


def _key_chunks(nk):
    kc = _pick(nk, ATT_KEY_CHUNKS)
    return [(c * kc, kc) for c in range(nk // kc)]


def _attn_fwd_call(q, k, v, row0, n_rows, tq, nk, name):
    def body(q_ref, k_ref, v_ref, o_ref, lse_ref):
        qv = q_ref[...]
        m = jnp.full((tq, 1), -jnp.inf, F32)
        l = jnp.zeros((tq, 1), F32)
        acc = jnp.zeros((tq, HEAD_PAD), F32)
        for start, size in _key_chunks(nk):
            s = lax.dot_general(qv, k_ref[start:start + size, :], _DOT_DIMS["nt"], preferred_element_type=F32)
            m_new = jnp.maximum(m, jnp.max(s, axis=-1, keepdims=True))
            alpha = jnp.exp2(m - m_new)
            p = jnp.exp2(s - m_new)
            l = alpha * l + jnp.sum(p, axis=-1, keepdims=True)
            acc = alpha * acc + lax.dot_general(p.astype(BF16), v_ref[start:start + size, :], _DOT_DIMS["nn"],
                                                preferred_element_type=F32)
            m = m_new
        o_ref[...] = acc / l
        lse_ref[...] = jnp.broadcast_to(m + jnp.log2(l), (tq, HEAD_PAD))

    out_spec = pl.BlockSpec((tq, HEAD_PAD), lambda h, i: (i, h))
    kv_spec = pl.BlockSpec((nk, HEAD_PAD), lambda h, i: (0, h))
    out = jax.ShapeDtypeStruct((n_rows, q.shape[1]), F32)
    return pl.pallas_call(
        body, name=name, grid=(MLA_HEADS, n_rows // tq), in_specs=[_query_rows_spec(row0, tq), kv_spec, kv_spec],
        out_specs=[out_spec, out_spec], out_shape=[out, out], compiler_params=_cparams(2),
    )(q, k, v)


def _attn_bwd_call(q, k, v, o, lse, do, row0, n_rows, tq, nk, name):
    nq = n_rows // tq

    def body(q_ref, k_ref, v_ref, o_ref, lse_ref, do_ref, dq_ref, dk_ref, dv_ref):
        i = pl.program_id(1)

        @pl.when(i == 0)
        def _():
            dk_ref[...] = jnp.zeros_like(dk_ref)
            dv_ref[...] = jnp.zeros_like(dv_ref)

        qv = q_ref[...]
        dov = do_ref[...]
        dob = dov.astype(BF16)
        lse = lse_ref[:, 0:1]
        delta = jnp.sum(dov * o_ref[...], axis=-1, keepdims=True)
        dq = jnp.zeros((tq, HEAD_PAD), F32)
        for start, size in _key_chunks(nk):
            kk = k_ref[start:start + size, :]
            vv = v_ref[start:start + size, :]
            s = lax.dot_general(qv, kk, _DOT_DIMS["nt"], preferred_element_type=F32)
            p = jnp.exp2(s - lse)
            dp = lax.dot_general(dob, vv, _DOT_DIMS["nt"], preferred_element_type=F32)
            g = (p * (dp - delta)).astype(BF16)
            dk_ref[start:start + size, :] += lax.dot_general(g, qv, _DOT_DIMS["tn"], preferred_element_type=F32)
            dv_ref[start:start + size, :] += lax.dot_general(p.astype(BF16), dob, _DOT_DIMS["tn"],
                                                             preferred_element_type=F32)
            dq = dq + lax.dot_general(g, kk, _DOT_DIMS["nn"], preferred_element_type=F32)
        dq_ref[...] = dq * (1.0 / LOG2E)

        @pl.when(i == nq - 1)
        def _():
            dk_ref[...] = dk_ref[...] * (1.0 / LOG2E)

    own_spec = pl.BlockSpec((tq, HEAD_PAD), lambda h, i: (i, h))
    kv_spec = pl.BlockSpec((nk, HEAD_PAD), lambda h, i: (0, h))
    rows_spec = _query_rows_spec(row0, tq)
    return pl.pallas_call(
        body, name=name, grid=(MLA_HEADS, nq),
        in_specs=[rows_spec, kv_spec, kv_spec, own_spec, own_spec, rows_spec],
        out_specs=[own_spec, kv_spec, kv_spec],
        out_shape=[jax.ShapeDtypeStruct((n_rows, q.shape[1]), F32), jax.ShapeDtypeStruct((nk, q.shape[1]), F32),
                   jax.ShapeDtypeStruct((nk, q.shape[1]), F32)],
        compiler_params=_cparams(2),
    )(q, k, v, o, lse, do)


def _attn_ranges(r_total, tc):
    tq_lat = _pick(r_total - tc, (ATT_LATENT_TILE, ROW_TILE))
    return [(0, tc, ROW_TILE, tc, "_ctx"), (tc, r_total - tc, tq_lat, r_total, "_lat")]


def attention_forward(qs, k, v, tc, name):
    parts = [_attn_fwd_call(qs, k, v, row0, n_rows, tq, nk, name + tag + "_f")
             for row0, n_rows, tq, nk, tag in _attn_ranges(qs.shape[0], tc)]
    return jnp.concatenate([o for o, _ in parts], axis=0), parts


def attention_backward(qs, k, v, parts, do, tc, name):
    r_total = qs.shape[0]
    (dq_c, dk_c, dv_c), (dq_l, dk_l, dv_l) = [
        _attn_bwd_call(qs, k, v, o, lse, do, row0, n_rows, tq, nk, name + tag + "_b")
        for (o, lse), (row0, n_rows, tq, nk, tag) in zip(parts, _attn_ranges(r_total, tc))]
    grow = lambda part: jnp.pad(part, ((0, r_total - tc), (0, 0)))
    return jnp.concatenate([dq_c, dq_l], axis=0), dk_l + grow(dk_c), dv_l + grow(dv_c)


CHUNK_SHIFT = CHUNK.bit_length() - 1


def _block_pairs():
    rows = lax.broadcasted_iota(jnp.int32, (ROW_TILE, ROW_TILE), 0)
    cols = lax.broadcasted_iota(jnp.int32, (ROW_TILE, ROW_TILE), 1)
    same = lax.shift_right_logical(rows, CHUNK_SHIFT) == lax.shift_right_logical(cols, CHUNK_SHIFT)
    return rows, cols, same


def _block_mask(kind):
    rows, cols, same = _block_pairs()
    order = {"lower_incl": rows >= cols, "upper_incl": rows <= cols, "lower_strict": rows > cols,
             "upper_strict": rows < cols}[kind]
    return jnp.logical_and(same, order)


def _row_chunk():
    return lax.shift_right_logical(lax.broadcasted_iota(jnp.int32, (ROW_TILE, 1), 0), CHUNK_SHIFT)


def _dot01(kind, x):
    m = _block_mask(kind).astype(BF16)
    hi = x.astype(BF16)
    rest = x - hi.astype(F32)
    mid = rest.astype(BF16)
    lo = (rest - mid.astype(F32)).astype(BF16)
    terms = jnp.concatenate([hi, mid, lo], axis=1)
    out = lax.dot_general(m, terms, _DOT_DIMS["nn"], preferred_element_type=F32)
    n = x.shape[1]
    return out[:, :n] + out[:, n:2 * n] + out[:, 2 * n:]


def _chunk_sums(x, forward):
    kinds = ("lower_incl", "upper_strict") if forward else ("upper_incl", "lower_strict")
    transposed = ("upper_incl", "lower_strict") if forward else ("lower_incl", "upper_strict")

    @jax.custom_vjp
    def f(x):
        return _dot01(kinds[0], x), _dot01(kinds[1], x)

    def fwd(x):
        return (_dot01(kinds[0], x), _dot01(kinds[1], x)), None

    def bwd(_, g):
        return (_dot01(transposed[0], g[0]) + _dot01(transposed[1], g[1]),)

    f.defvjp(fwd, bwd)
    return f(x)


def _scan_order(forward):
    if forward:
        return list(range(SCAN_CHUNKS)), lambda c: c * CHUNK + CHUNK - 1
    return list(range(SCAN_CHUNKS - 1, -1, -1)), lambda c: c * CHUNK


def _carry_states(forward, st0, inc_all, decay_of):
    order, _ = _scan_order(forward)
    entering = [None] * SCAN_CHUNKS
    st = st0
    for c in order:
        entering[c] = st
        st = st * decay_of(c) + inc_all[:, c * HEAD_PAD:(c + 1) * HEAD_PAD]
    return jnp.concatenate(entering, axis=0), st


def _per_chunk_lanes(x):
    chunk = _row_chunk()
    return jnp.concatenate([jnp.where(chunk == c, x, 0.0) for c in range(SCAN_CHUNKS)], axis=1)


def _own_chunk_lanes(x4):
    chunk = _row_chunk()
    n = x4.shape[1] // SCAN_CHUNKS
    out = jnp.where(chunk == 0, x4[:, :n], 0.0)
    for c in range(1, SCAN_CHUNKS):
        out = out + jnp.where(chunk == c, x4[:, c * n:(c + 1) * n], 0.0)
    return out


def _gla_block(forward, q, k, v, la, st0):
    cum, after = _chunk_sums(la, forward)
    _, last_row = _scan_order(forward)
    q_dec = q * (jnp.exp(cum) * (GLA_DK ** -0.5))
    att = _bdot(q_dec, k * jnp.exp(-cum), "nt")
    att = jnp.where(_block_mask("lower_incl" if forward else "upper_strict"), att, 0.0)
    inc_all = _bdot(v, _per_chunk_lanes(k * jnp.exp(after)), "tn")
    entering, st1 = _carry_states(forward, st0, inc_all,
                                  lambda c: jnp.exp(cum[last_row(c):last_row(c) + 1, :]))
    o = _bdot(att, v, "nn") + _own_chunk_lanes(_bdot(q_dec, entering, "nt"))
    return o, st1


def _ret_block(forward, q, k, v, rd, st0):
    lg = -jnp.exp(rd[0:1, 0:1])
    rows, cols, _ = _block_pairs()
    pos = jnp.bitwise_and(lax.broadcasted_iota(jnp.int32, (ROW_TILE, 1), 0), CHUNK - 1).astype(F32)
    if forward:
        to_end, from_start, rel = CHUNK - 1.0 - pos, pos + 1.0, (rows - cols).astype(F32)
    else:
        to_end, from_start, rel = pos, CHUNK - pos, (cols - rows).astype(F32)
    mask = _block_mask("lower_incl" if forward else "upper_strict")
    dmat = jnp.where(mask, jnp.exp(jnp.where(mask, rel, 0.0) * lg), 0.0)
    att = _bdot(q, k, "nt") * dmat
    inc_all = _bdot(v, _per_chunk_lanes(k * jnp.exp(to_end * lg)), "tn")
    entering, st1 = _carry_states(forward, st0, inc_all, lambda c: jnp.exp(CHUNK * lg))
    o = _bdot(att, v, "nn") + _own_chunk_lanes(_bdot(q, entering, "nt")) * jnp.exp(from_start * lg)
    return o, st1


def scan(kind, forward, q, k, v, aux, tc, name, rot=None):
    heads = q.shape[1] // HEAD_PAD
    r_total = q.shape[0]
    nblk = r_total // ROW_TILE
    nctb = tc // ROW_TILE
    block_fn = functools.partial(_gla_block if kind == "gla" else _ret_block, forward)
    per_row_aux = kind == "gla"

    def blk(g):
        if forward:
            return g
        return jnp.where(g < nctb, nctb - 1 - g, nblk - 1 - (g - nctb))

    def specs(step_to_g):
        row = pl.BlockSpec((ROW_TILE, heads * HEAD_PAD), lambda s: (blk(step_to_g(s)), 0))
        aux_spec = row if per_row_aux else pl.BlockSpec((heads, 8, HEAD_PAD), lambda s: (0, 0, 0))
        st = pl.BlockSpec((1, heads, HEAD_PAD, HEAD_PAD), lambda s: (step_to_g(s), 0, 0, 0))
        return row, aux_spec, st

    def head_cols(h):
        return slice(h * HEAD_PAD, (h + 1) * HEAD_PAD)

    n_rot = 0 if rot is None else 2
    rot_arrays = [] if rot is None else list(rot)

    def rot_specs(step_to_g):
        return [pl.BlockSpec((ROW_TILE, HEAD_PAD), lambda s: (blk(step_to_g(s)), 0))] * n_rot

    def head_fn(rot_refs):
        if not rot_refs:
            return block_fn
        rc, rs = rot_refs[0][...], rot_refs[1][...]

        def turned(x, scale):
            return (x * rc + _roll(x, HEAD_PAD // 2, 1) * rs) * scale

        return lambda q, k, v, a, st0: block_fn(turned(q, 1.0), turned(k, RET_DK ** -0.5), v, a, st0)

    def fwd_call(q, k, v, aux):
        row, aux_spec, st_spec = specs(lambda s: s)

        def body(*refs):
            q_ref, k_ref, v_ref, a_ref = refs[:4]
            o_ref, st0_ref, st_ref = refs[4 + n_rot:]
            fn = head_fn(refs[4:4 + n_rot])

            @pl.when(pl.program_id(0) == 0)
            def _():
                st_ref[...] = jnp.zeros_like(st_ref)

            qv, kv, vv = q_ref[...], k_ref[...], v_ref[...]
            outs = []
            for h in range(heads):
                st0 = st_ref[h]
                st0_ref[0, h] = st0
                a = a_ref[:, head_cols(h)] if per_row_aux else a_ref[h]
                o, st1 = fn(qv[:, head_cols(h)], kv[:, head_cols(h)], vv[:, head_cols(h)], a, st0)
                outs.append(o)
                st_ref[h] = st1
            o_ref[...] = jnp.concatenate(outs, axis=1)

        return pl.pallas_call(
            body, name=name + "_f", grid=(nblk,), in_specs=[row, row, row, aux_spec] + rot_specs(lambda s: s),
            out_specs=[row, st_spec],
            out_shape=[jax.ShapeDtypeStruct(q.shape, F32),
                       jax.ShapeDtypeStruct((nblk, heads, HEAD_PAD, HEAD_PAD), F32)],
            scratch_shapes=[pltpu.VMEM((heads, HEAD_PAD, HEAD_PAD), F32)],
            compiler_params=_cparams(1),
        )(q, k, v, aux, *rot_arrays)

    def bwd_call(q, k, v, aux, st0s, do):
        row, aux_spec, st_spec = specs(lambda s: nblk - 1 - s)

        def body(*refs):
            q_ref, k_ref, v_ref, a_ref = refs[:4]
            st0_ref, do_ref, dq_ref, dk_ref, dv_ref, da_ref, dst_ref = refs[4 + n_rot:]
            fn = head_fn(refs[4:4 + n_rot])
            s = pl.program_id(0)

            @pl.when(s == 0)
            def _():
                dst_ref[...] = jnp.zeros_like(dst_ref)

            qv, kv, vv, dov = q_ref[...], k_ref[...], v_ref[...], do_ref[...]
            grads = []
            for h in range(heads):
                a = a_ref[:, head_cols(h)] if per_row_aux else a_ref[h]
                _, vjp = jax.vjp(fn, qv[:, head_cols(h)], kv[:, head_cols(h)], vv[:, head_cols(h)], a,
                                 st0_ref[0, h])
                dq, dk, dv, da, dst0 = vjp((dov[:, head_cols(h)], dst_ref[h]))
                dst_ref[h] = dst0
                grads.append((dq, dk, dv, da))
            dq_ref[...] = jnp.concatenate([g[0] for g in grads], axis=1)
            dk_ref[...] = jnp.concatenate([g[1] for g in grads], axis=1)
            dv_ref[...] = jnp.concatenate([g[2] for g in grads], axis=1)
            if per_row_aux:
                da_ref[...] = jnp.concatenate([g[3] for g in grads], axis=1)
            else:
                da = jnp.stack([g[3] for g in grads], axis=0)

                @pl.when(s == 0)
                def _():
                    da_ref[...] = da

                @pl.when(s != 0)
                def _():
                    da_ref[...] += da

        return pl.pallas_call(
            body, name=name + "_b", grid=(nblk,),
            in_specs=[row, row, row, aux_spec] + rot_specs(lambda s: nblk - 1 - s) + [st_spec, row],
            out_specs=[row, row, row, aux_spec],
            out_shape=[jax.ShapeDtypeStruct(q.shape, F32)] * 3 + [jax.ShapeDtypeStruct(aux.shape, F32)],
            scratch_shapes=[pltpu.VMEM((heads, HEAD_PAD, HEAD_PAD), F32)],
            compiler_params=_cparams(1),
        )(q, k, v, aux, *rot_arrays, st0s, do)

    @jax.custom_vjp
    def op(q, k, v, aux):
        return fwd_call(q, k, v, aux)[0]

    def fwd(q, k, v, aux):
        o, st0s = fwd_call(q, k, v, aux)
        return o, (q, k, v, aux, st0s)

    def bwd(res, do):
        return tuple(bwd_call(*res, do))

    op.defvjp(fwd, bwd)
    return op(q, k, v, aux)


HALO = 8


def _neighbours(main, prev8, next8, i, nct, n_tiles):
    has_prev = jnp.logical_and(i != 0, i != nct).astype(F32)
    has_next = jnp.logical_and(i != nct - 1, i != n_tiles - 1).astype(F32)
    row = lax.broadcasted_iota(jnp.int32, main.shape, 0)
    down = jnp.where(row == 0, prev8[HALO - 1:HALO] * has_prev, pltpu.roll(main, 1, 0))
    up = jnp.where(row == ROW_TILE - 1, next8[0:1] * has_next, pltpu.roll(main, ROW_TILE - 1, 0))
    return down, up


GELU_K = math.sqrt(2.0 / math.pi)
GELU_A = 0.044715


def _gelu_tanh_grad(x):
    t = jnp.tanh(GELU_K * (x + GELU_A * (x * x * x)))
    return 0.5 * x * (1.0 + t), 0.5 * (1.0 + t) + 0.5 * x * (1.0 - t * t) * (GELU_K * (1.0 + 3.0 * GELU_A * (x * x)))


def conv_ffn_out(gate, up, w8, b, w_out, tc, name):
    r_total, width = gate.shape
    n_tiles = r_total // ROW_TILE
    nct = tc // ROW_TILE
    per = ROW_TILE // HALO
    main_spec = pl.BlockSpec((ROW_TILE, width), lambda i: (i, 0))
    prev_spec = pl.BlockSpec((HALO, width), lambda i: (jnp.maximum(i * per - 1, 0), 0))
    next_spec = pl.BlockSpec((HALO, width), lambda i: (jnp.minimum((i + 1) * per, r_total // HALO - 1), 0))
    w_spec = pl.BlockSpec((8, width), lambda i: (0, 0))
    b_spec = pl.BlockSpec((1, width), lambda i: (0, 0))
    halo3 = [main_spec, prev_spec, next_spec]

    def conv(w_ref, b_ref, down, mid, upn):
        return w_ref[0:1] * down + w_ref[1:2] * mid + w_ref[2:3] * upn + b_ref[...]

    def fwd_call(gate, up, w8, b):
        def body(g_ref, gp_ref, gn_ref, up_ref, w_ref, b_ref, o_ref):
            gv = g_ref[...]
            down, upn = _neighbours(gv, gp_ref[...], gn_ref[...], pl.program_id(0), nct, n_tiles)
            o_ref[...] = (_gelu_tanh(conv(w_ref, b_ref, down, gv, upn)) * up_ref[...]).astype(BF16)

        return pl.pallas_call(
            body, name=name + "_f", grid=(n_tiles,), in_specs=halo3 + [main_spec, w_spec, b_spec],
            out_specs=main_spec, out_shape=jax.ShapeDtypeStruct(gate.shape, BF16), compiler_params=_cparams(1),
        )(gate, gate, gate, up, w8, b)

    def bwd_call(gate, up, w8, b, du):
        def body(g_ref, gp_ref, gn_ref, up_ref, upp_ref, upn_ref, du_ref, dup_ref, dun_ref, w_ref, b_ref,
                 dg_ref, dupo_ref, dw_ref, db_ref):
            i = pl.program_id(0)
            has_prev = jnp.logical_and(i != 0, i != nct).astype(F32)
            has_next = jnp.logical_and(i != nct - 1, i != n_tiles - 1).astype(F32)
            gv, gp, gn = g_ref[...], gp_ref[...], gn_ref[...]
            g_down, g_up = _neighbours(gv, gp, gn, i, nct, n_tiles)
            act, slope = _gelu_tanh_grad(conv(w_ref, b_ref, g_down, gv, g_up))
            duv = du_ref[...]
            dupo_ref[...] = duv * act
            dc = duv * up_ref[...] * slope
            c_above = conv(w_ref, b_ref, gp[HALO - 2:HALO - 1], gp[HALO - 1:HALO], gv[0:1])
            c_below = conv(w_ref, b_ref, gv[ROW_TILE - 1:ROW_TILE], gn[0:1], gn[1:2])
            dc_above = dup_ref[HALO - 1:HALO] * upp_ref[HALO - 1:HALO] * _gelu_tanh_grad(c_above)[1] * has_prev
            dc_below = dun_ref[0:1] * upn_ref[0:1] * _gelu_tanh_grad(c_below)[1] * has_next
            row = lax.broadcasted_iota(jnp.int32, dc.shape, 0)
            dc_down = jnp.where(row == 0, dc_above, pltpu.roll(dc, 1, 0))
            dc_up = jnp.where(row == ROW_TILE - 1, dc_below, pltpu.roll(dc, ROW_TILE - 1, 0))
            dg_ref[...] = w_ref[0:1] * dc_up + w_ref[1:2] * dc + w_ref[2:3] * dc_down
            dw = jnp.concatenate([jnp.sum(dc * g_down, axis=0, keepdims=True),
                                  jnp.sum(dc * gv, axis=0, keepdims=True),
                                  jnp.sum(dc * g_up, axis=0, keepdims=True),
                                  jnp.zeros((5, width), F32)], axis=0)
            db = jnp.sum(dc, axis=0, keepdims=True)

            @pl.when(i == 0)
            def _():
                dw_ref[...] = dw
                db_ref[...] = db

            @pl.when(i != 0)
            def _():
                dw_ref[...] += dw
                db_ref[...] += db

        return pl.pallas_call(
            body, name=name + "_b", grid=(n_tiles,), in_specs=halo3 * 3 + [w_spec, b_spec],
            out_specs=[main_spec, main_spec, w_spec, b_spec],
            out_shape=[jax.ShapeDtypeStruct(gate.shape, F32), jax.ShapeDtypeStruct(gate.shape, F32),
                       jax.ShapeDtypeStruct((8, width), F32), jax.ShapeDtypeStruct((1, width), F32)],
            compiler_params=_cparams(1),
        )(gate, gate, gate, up, up, up, du, du, du, w8, b)

    @jax.custom_vjp
    def op(gate, up, w8, b, w_out):
        return _mm(fwd_call(gate, up, w8, b), w_out.astype(BF16), "nn", name + "_out_f")

    def fwd(gate, up, w8, b, w_out):
        u, wb = fwd_call(gate, up, w8, b), w_out.astype(BF16)
        return _mm(u, wb, "nn", name + "_out_f"), (gate, up, w8, b, u, wb)

    def bwd(res, g):
        gate, up, w8, b, u, wb = res
        du = _mm(g, wb, "nt", name + "_out_dx")
        d_gate, d_up, d_w8, d_b = bwd_call(gate, up, w8, b, du)
        return d_gate, d_up, d_w8, d_b, _mm(u, g, "tn", name + "_out_dw")

    op.defvjp(fwd, bwd)
    return op(gate, up, w8, b, w_out)


def loss_head(h, target, tc, name):
    r_total, width = h.shape
    n_tiles = r_total // ROW_TILE
    nct = tc // ROW_TILE

    def call(h, target):
        def body(h_ref, t_ref, dh_ref, loss_ref, acc_ref):
            i = pl.program_id(0)

            @pl.when(i == 0)
            def _():
                acc_ref[...] = jnp.zeros_like(acc_ref)

            @pl.when(i < nct)
            def _():
                dh_ref[...] = jnp.zeros_like(dh_ref)

            @pl.when(i >= nct)
            def _():
                err = h_ref[...] - t_ref[...]
                dh_ref[...] = err * (1.0 / width)
                acc_ref[...] += jnp.sum((err * err).reshape(ROW_TILE // 8, 8, width), axis=0)

            @pl.when(i == n_tiles - 1)
            def _():
                loss_ref[...] = jnp.sum(acc_ref[...]).reshape(1, 1) * (0.5 / width)

        row = pl.BlockSpec((ROW_TILE, width), lambda i: (i, 0))
        return pl.pallas_call(
            body, name=name, grid=(n_tiles,),
            in_specs=[row, pl.BlockSpec((ROW_TILE, width), lambda i: (jnp.maximum(i - nct, 0), 0))],
            out_specs=[row, pl.BlockSpec((1, 1), lambda i: (0, 0))],
            out_shape=[jax.ShapeDtypeStruct(h.shape, F32), jax.ShapeDtypeStruct((1, 1), F32)],
            scratch_shapes=[pltpu.VMEM((8, width), F32)], compiler_params=_cparams(1),
        )(h, target)

    @jax.custom_vjp
    def op(h, target):
        return call(h, target)[1][0, 0]

    def fwd(h, target):
        dh, loss = call(h, target)
        return loss[0, 0], (dh, target)

    def bwd(res, g):
        dh, target = res
        return dh * g, jnp.zeros_like(target)

    op.defvjp(fwd, bwd)
    return op(h, target)


PACK_W = 1024
PACK_TILE = 128


def slab_sum(slabs, name):
    n_slab, n, _ = slabs.shape

    def body(s_ref, o_ref):
        acc = s_ref[0]
        for j in range(1, n_slab):
            acc = acc + s_ref[j]
        o_ref[...] = acc

    return pl.pallas_call(
        body, name=name, grid=(n // PACK_TILE,),
        in_specs=[pl.BlockSpec((n_slab, PACK_TILE, PACK_W), lambda i: (0, i, 0))],
        out_specs=pl.BlockSpec((PACK_TILE, PACK_W), lambda i: (i, 0)),
        out_shape=jax.ShapeDtypeStruct((n, PACK_W), F32), compiler_params=_cparams(1),
    )(slabs)


def adamw(g_slabs, w, m, v, name):
    n_slab, n, _ = g_slabs.shape

    def body(g_ref, w_ref, m_ref, v_ref, go_ref, d_ref, mo_ref, vo_ref):
        g = g_ref[0].astype(F32)
        for j in range(1, n_slab):
            g = g + g_ref[j].astype(F32)
        m_new = ADAM_B1 * m_ref[...] + (1.0 - ADAM_B1) * g
        v_new = ADAM_B2 * v_ref[...] + (1.0 - ADAM_B2) * (g * g)
        m_hat = m_new / (1.0 - ADAM_B1 ** ADAM_STEP)
        v_hat = v_new / (1.0 - ADAM_B2 ** ADAM_STEP)
        go_ref[...] = g
        d_ref[...] = -ADAM_LR * (m_hat / (jnp.sqrt(v_hat) + ADAM_EPS) + ADAM_WD * w_ref[...])
        mo_ref[...] = m_new
        vo_ref[...] = v_new

    flat = pl.BlockSpec((PACK_TILE, PACK_W), lambda i: (i, 0))
    return pl.pallas_call(
        body, name=name, grid=(n // PACK_TILE,),
        in_specs=[pl.BlockSpec((n_slab, PACK_TILE, PACK_W), lambda i: (0, i, 0)), flat, flat, flat],
        out_specs=[flat] * 4, out_shape=[jax.ShapeDtypeStruct((n, PACK_W), F32)] * 4, compiler_params=_cparams(1),
    )(g_slabs, w, m, v)


def all_gather(x, name):
    m_per, n = x.shape

    def body(x_ref, out_ref, send_sems, recv_sems, local_sem):
        px, py, pc = lax.axis_index("x"), lax.axis_index("y"), lax.axis_index("c")
        me, sibling = (px, py, pc), (px, py, 1 - pc)
        chips = [(1 - px, py), (px, 1 - py), (1 - px, 1 - py)]

        def rows(bx, by, bc):
            return out_ref.at[pl.ds((4 * bx + 2 * by + bc) * m_per, m_per), :]

        def copy(k, block, to, src=None):
            return pltpu.make_async_remote_copy(
                src_ref=rows(*block) if src is None else src, dst_ref=rows(*block),
                send_sem=send_sems.at[k], recv_sem=recv_sems.at[k], device_id=to, device_id_type=MESH)

        mine = pltpu.make_async_copy(x_ref, rows(*me), local_sem)
        mine.start()
        first = [copy(0, me, sibling, src=x_ref)]
        first += [copy(1 + j, me, (*chip, pc), src=x_ref) for j, chip in enumerate(chips)]
        for cp in first:
            cp.start()
        passed = [copy(4 + j, (*chip, pc), sibling) for j, chip in enumerate(chips)]
        for j, chip in enumerate(chips):
            copy(1 + j, (*chip, pc), me).wait_recv()
            passed[j].start()
        copy(0, sibling, me).wait_recv()
        for j, chip in enumerate(chips):
            copy(4 + j, (*chip, 1 - pc), me).wait_recv()
        for cp in first + passed:
            cp.wait_send()
        mine.wait()

    return pl.pallas_call(
        body, name=name, out_shape=jax.ShapeDtypeStruct((N_DEV * m_per, n), x.dtype),
        in_specs=[pl.BlockSpec(memory_space=pl.ANY)], out_specs=pl.BlockSpec(memory_space=pl.ANY),
        scratch_shapes=[pltpu.SemaphoreType.DMA((7,)), pltpu.SemaphoreType.DMA((7,)), pltpu.SemaphoreType.DMA],
    )(x)


N_CHIP = 4


def pair_swap(x, name):
    def body(x_ref, out_ref, send_sem, recv_sem):
        sibling = (lax.axis_index("x"), lax.axis_index("y"), 1 - lax.axis_index("c"))
        copy = pltpu.make_async_remote_copy(src_ref=x_ref, dst_ref=out_ref, send_sem=send_sem, recv_sem=recv_sem,
                                            device_id=sibling, device_id_type=MESH)
        copy.start()
        copy.wait()

    return pl.pallas_call(
        body, name=name, out_shape=jax.ShapeDtypeStruct(x.shape, x.dtype),
        in_specs=[pl.BlockSpec(memory_space=pl.ANY)], out_specs=pl.BlockSpec(memory_space=pl.ANY),
        scratch_shapes=[pltpu.SemaphoreType.DMA, pltpu.SemaphoreType.DMA],
    )(x)


def pair_add(a, b, name):
    n_slab, n, _ = a.shape

    def body(a_ref, b_ref, o_ref):
        o_ref[...] = (a_ref[...].astype(F32) + b_ref[...].astype(F32)).astype(o_ref.dtype)

    spec = pl.BlockSpec((1, PACK_TILE, PACK_W), lambda s, i: (s, i, 0))
    return pl.pallas_call(
        body, name=name, grid=(n_slab, n // PACK_TILE), in_specs=[spec, spec], out_specs=spec,
        out_shape=jax.ShapeDtypeStruct(a.shape, a.dtype), compiler_params=_cparams(2),
    )(a, b)


def chip_all_to_all(x, name):
    def body(x_ref, out_ref, send_sems, recv_sems, local_sem):
        px, py, pc = lax.axis_index("x"), lax.axis_index("y"), lax.axis_index("c")
        mine_idx = 2 * px + py
        local = pltpu.make_async_copy(x_ref.at[mine_idx], out_ref.at[mine_idx], local_sem)
        local.start()
        copies = []
        for k, (fx, fy) in enumerate(((0, 1), (1, 0), (1, 1))):
            qx, qy = px ^ fx, py ^ fy
            peer_idx = 2 * qx + qy
            copies.append((
                pltpu.make_async_remote_copy(
                    src_ref=x_ref.at[peer_idx], dst_ref=out_ref.at[mine_idx], send_sem=send_sems.at[k],
                    recv_sem=recv_sems.at[k], device_id=(qx, qy, pc), device_id_type=MESH),
                pltpu.make_async_remote_copy(
                    src_ref=x_ref.at[peer_idx], dst_ref=out_ref.at[peer_idx], send_sem=send_sems.at[k],
                    recv_sem=recv_sems.at[k], device_id=(qx, qy, pc), device_id_type=MESH)))
        for send, _ in copies:
            send.start()
        for _, landing in copies:
            landing.wait_recv()
        for send, _ in copies:
            send.wait_send()
        local.wait()

    return pl.pallas_call(
        body, name=name, out_shape=jax.ShapeDtypeStruct(x.shape, x.dtype),
        in_specs=[pl.BlockSpec(memory_space=pl.ANY)], out_specs=pl.BlockSpec(memory_space=pl.ANY),
        scratch_shapes=[pltpu.SemaphoreType.DMA((3,)), pltpu.SemaphoreType.DMA((3,)), pltpu.SemaphoreType.DMA],
    )(x)


IN_OFFSETS = {}
_off = 0
for _name, _width in (("mla_q", 256), ("mla_kv", 128), ("mla_kr", 32), ("gla_q", 512), ("gla_k", 512), ("gla_v", 512),
                      ("gla_g", 512), ("gla_rf", 16), ("gla_rb", 16), ("ret_q", 512), ("ret_k", 512), ("ret_v", 512),
                      ("ret_g", 512), ("gate_mla", 1024), ("gate_gla", 1024), ("gate_ret", 1024)):
    IN_OFFSETS[_name] = (_off, _off + _width)
    _off += _width
N_IN = _off

P_GLA, P_RET, P_GATE, P_MLAQ, P_MLAKV, P_MLAKR, P_RANK, P_END = 0, 2048, 4096, 7168, 7424, 7552, 7680, 7808


def _pad_in_proj(w):
    def cols(a, b):
        return w[:, IN_OFFSETS[a][0]:IN_OFFSETS[b][1]]

    def z(n):
        return jnp.zeros((w.shape[0], n), w.dtype)

    return jnp.concatenate([cols("gla_q", "gla_g"), cols("ret_q", "ret_g"), cols("gate_mla", "gate_ret"),
                            cols("mla_q", "mla_kv"), z(MLA_NOPE), cols("mla_kr", "mla_kr"),
                            z(HEAD_PAD - MLA_QK), cols("gla_rf", "gla_rb"), z(HEAD_PAD - 2 * GLA_RANK),
                            z(N_IN_PAD - P_END)], axis=1)


def _pad_last(a, n):
    return jnp.pad(a, [(0, 0)] * (a.ndim - 1) + [(0, n - a.shape[-1])])


def _position_tables(tc, t):
    pos = jnp.arange(t)
    inv = ROPE_THETA ** (-jnp.arange(MLA_ROPE // 4, dtype=F32) * 2.0 / (MLA_ROPE // 2))
    ang_r = (pos // GRID_W).astype(F32)[:, None] * inv[None, :]
    ang_c = (pos % GRID_W).astype(F32)[:, None] * inv[None, :]
    z8, z32, z64 = jnp.zeros((t, 8), F32), jnp.zeros((t, 32), F32), jnp.zeros((t, 64), F32)
    lat_c = jnp.concatenate([jnp.ones((t, 64), F32), jnp.cos(ang_r), jnp.cos(ang_r), jnp.cos(ang_c), jnp.cos(ang_c),
                             z32], axis=1)
    lat_sn = jnp.concatenate([z64, -jnp.sin(ang_r), z8, -jnp.sin(ang_c), z8, z32], axis=1)
    lat_sp = jnp.concatenate([z64, z8, jnp.sin(ang_r), z8, jnp.sin(ang_c), z32], axis=1)
    ctx_c = jnp.concatenate([jnp.ones((tc, MLA_QK), F32), jnp.zeros((tc, HEAD_PAD - MLA_QK), F32)], axis=1)
    ctx_z = jnp.zeros((tc, HEAD_PAD), F32)
    rinv = 1.0 / (RET_THETA ** jnp.linspace(0.0, 1.0, RET_DK // 2, dtype=F32))
    rang = jnp.arange(tc + t).astype(F32)[:, None] * rinv[None, :]
    return dict(c=jnp.concatenate([ctx_c, lat_c]), sn=jnp.concatenate([ctx_z, lat_sn]),
                sp=jnp.concatenate([ctx_z, lat_sp]),
                rc=jnp.concatenate([jnp.cos(rang), jnp.cos(rang)], axis=1),
                rs=jnp.concatenate([-jnp.sin(rang), jnp.sin(rang)], axis=1))


def _heads(x):
    return [x[:, h * HEAD_PAD:(h + 1) * HEAD_PAD] for h in range(x.shape[1] // HEAD_PAD)]


def _mla_rope(x, c, sn, sp):
    return x * c + _roll(x, HEAD_PAD - 8, 1) * sn + _roll(x, 8, 1) * sp


def _norm_mod_fn(shift_row, scale_row):
    def fn(h, mod, w):
        return (_rms(h, D, w) * (1.0 + mod[scale_row:scale_row + 1]) + mod[shift_row:shift_row + 1],)
    return fn


def _resid_fn(gate_row):
    def fn(h, y, mod):
        return (h + mod[gate_row:gate_row + 1] * y,)
    return fn


def _resid_norm_fn(gate_row, shift_row, scale_row):
    def fn(h, y, mod, w):
        h1 = h + mod[gate_row:gate_row + 1] * y
        return h1, _norm_mod_fn(shift_row, scale_row)(h1, mod, w)[0]
    return fn


def _mla_prep_fn(x, c, sn, sp, q_norm_a, w_qb, q_norm, kv_norm_a, w_k, w_v, k_norm):
    cq, ckv = x[:, :MLA_Q_LORA], x[:, MLA_Q_LORA:MLA_Q_LORA + MLA_KV_LORA]
    kr = x[:, MLA_Q_LORA + MLA_KV_LORA:]
    qf = _bdot(_rms(cq, MLA_Q_LORA, q_norm_a), w_qb, "nn")
    q = jnp.concatenate([_mla_rope(_rms(qh, MLA_QK, q_norm), c, sn, sp) for qh in _heads(qf)], axis=1)
    xkv = _rms(ckv, MLA_KV_LORA, kv_norm_a)
    kf = _bdot(xkv, w_k, "nn")
    k = jnp.concatenate([_mla_rope(_rms(kh + kr, MLA_QK, k_norm), c, sn, sp) for kh in _heads(kf)], axis=1)
    return q * (ATT_SCALE * LOG2E), k, _bdot(xkv, w_v, "nn")


def _decay_fn(x, w2, b):
    la = _log_sigmoid(_bdot(x[:, :HEAD_PAD], w2, "nn") + b) * (1.0 / GLA_NORMALIZER)
    return la[:, :GLA_HEADS * GLA_DK], la[:, GLA_HEADS * GLA_DK:]


def _gla_out_fn(o_f, o_b, g, w):
    y = jnp.concatenate([_rms(oh, HEAD_PAD, w) for oh in _heads(o_f + o_b)], axis=1)
    return (y * _silu(g),)


def _ret_out_fn(o_f, o_b, g):
    y = jnp.concatenate([_rms(oh, HEAD_PAD) for oh in _heads(o_f + o_b)], axis=1)
    return (y * _silu(g),)


def _merge_fn(z0, z1, z2, g0a, g0b, g1a, g1b, g2a, g2b, bg):
    out = 0.0
    for n, (z, ga, gb) in enumerate(((z0, g0a, g0b), (z1, g1a, g1b), (z2, g2a, g2b))):
        out = out + jax.nn.sigmoid(jnp.concatenate([ga, gb], axis=1) + bg[n:n + 1]) * z
    return (out,)


def _layer(l, h, mod, w, tabs, tc):
    nct = tc // ROW_TILE
    tag = f"_l{l}"
    row = lambda a: a[l][None]
    pieces = rowwise("norm1" + tag, _norm_mod_fn(0, 1), [h], [mod], [row(w["norm1_w"])], [D], nct,
                     then=("pieces", _pad_in_proj(w["w_in"][l]), "in_proj" + tag))
    piece = lambda start: pieces[start // PIECE_W]

    w_qb = _pad_last(w["mla_w_qb"][l].reshape(MLA_Q_LORA, MLA_HEADS, MLA_QK), HEAD_PAD).reshape(MLA_Q_LORA, -1)
    w_kvb = w["mla_w_kvb"][l].reshape(MLA_KV_LORA, MLA_HEADS, MLA_NOPE + MLA_V)
    w_k = _pad_last(w_kvb[:, :, :MLA_NOPE], HEAD_PAD).reshape(MLA_KV_LORA, -1)
    w_v = _pad_last(w_kvb[:, :, MLA_NOPE:], HEAD_PAD).reshape(MLA_KV_LORA, -1)
    rope = [tabs["c"], tabs["sn"], tabs["sp"]]
    y_mla = rowwise("mla_prep" + tag, _mla_prep_fn, [piece(P_MLAQ)] + rope, [],
                    [row(w["mla_q_norm_a"]), w_qb, _pad_last(row(w["mla_q_norm"]), HEAD_PAD),
                     row(w["mla_kv_norm_a"]), w_k, w_v, _pad_last(row(w["mla_k_norm"]), HEAD_PAD)],
                    [MLA_HEADS * HEAD_PAD] * 3, nct, diff_rows=[True, False, False, False],
                    then=("attention", tc, "attn" + tag))
    wb_mla = _pad_last(w["w_branch"][l, 0].reshape(MLA_HEADS, MLA_V, D).transpose(0, 2, 1), HEAD_PAD)
    wb_mla = wb_mla.transpose(0, 2, 1).reshape(MLA_HEADS * HEAD_PAD, D)

    w2 = jnp.zeros((HEAD_PAD, 2 * GLA_HEADS * GLA_DK), F32)
    w2 = w2.at[:GLA_RANK, :GLA_HEADS * GLA_DK].set(w["gla_w_gk2"][l, 0])
    w2 = w2.at[GLA_RANK:2 * GLA_RANK, GLA_HEADS * GLA_DK:].set(w["gla_w_gk2"][l, 1])
    la_f, la_b = rowwise("gla_decay" + tag, _decay_fn, [piece(P_RANK)], [],
                         [w2, w["gla_b_gk"][l].reshape(1, -1)], [GLA_HEADS * GLA_DK] * 2, nct)
    gq, gk, gv, gg = [piece(P_GLA + n * PIECE_W) for n in range(4)]
    o_f = scan("gla", True, gq, gk, gv, la_f, tc, "gla_fw" + tag)
    o_b = scan("gla", False, gq, gk, gv, la_b, tc, "gla_bw" + tag)
    z_gla = rowwise("gla_out" + tag, _gla_out_fn, [o_f, o_b, gg], [], [row(w["gla_o_norm"])], [512], nct,
                    then=("linear", w["w_branch"][l, 1], "branch_gla" + tag))

    rq, rk, rv, rg = [piece(P_RET + n * PIECE_W) for n in range(4)]
    rd = jnp.broadcast_to(w["ret_decay"][l][:, :, None, None], (2, RET_HEADS, 8, HEAD_PAD))
    turn = (tabs["rc"], tabs["rs"])
    r_f = scan("ret", True, rq, rk, rv, rd[0], tc, "ret_fw" + tag, rot=turn)
    r_b = scan("ret", False, rq, rk, rv, rd[1], tc, "ret_bw" + tag, rot=turn)
    z_ret = rowwise("ret_out" + tag, _ret_out_fn, [r_f, r_b, rg], [], [], [512], nct,
                    then=("linear", w["w_branch"][l, 2], "branch_ret" + tag))

    z = [linear(y_mla, wb_mla, "branch_mla" + tag), z_gla, z_ret]
    gates = [piece(P_GATE + n * PIECE_W) for n in range(6)]
    y = rowwise("merge" + tag, _merge_fn, z + gates, [], [_pad_rows(w["b_gate"][l], 8)], [D], nct,
                then=("linear", w["w_out"][l], "w_out" + tag))
    h, a2 = rowwise("resid1_norm2" + tag, _resid_norm_fn(2, 3, 4), [h, y], [mod], [row(w["norm2_w"])], [D, D], nct)
    gate = linear(a2, w["w_ffn_in"][l][:, :D_FF], "ffn_gate" + tag)
    up = linear(a2, w["w_ffn_in"][l][:, D_FF:], "ffn_up" + tag)
    f = conv_ffn_out(gate, up, _pad_rows(w["w_dw"][l], 8), row(w["b_dw"]), w["w_ffn_out"][l], tc, "ffn_mid" + tag)
    return rowwise("resid2" + tag, _resid_fn(5), [h, f], [mod], [], [D], nct)[0]


def _pad_rows(a, n):
    return jnp.pad(a, [(0, n - a.shape[0])] + [(0, 0)] * (a.ndim - 1))


def local_loss(w, mod, x, ctx, target):
    tc, t = ctx.shape[0], x.shape[0]
    tabs = _position_tables(tc, t)
    h = jnp.concatenate([ctx, x], axis=0)
    for l in range(DEPTH):
        h = _layer(l, h, mod[l], w, tabs, tc)
    return loss_head(h, target, tc, "loss_head")


ADA_ROWS = 16


def ada_forward(cond_in, w_ada, b_loc):
    cols = w_ada.shape[2]

    def body(x_ref, w_ref, b_ref, o_ref):
        s = _silu(x_ref[...])
        for l in range(DEPTH):
            o_ref[l] = _dg(s, w_ref[l], "nn") + b_ref[l]

    return pl.pallas_call(
        body, name="ada_forward", out_shape=jax.ShapeDtypeStruct((DEPTH, ADA_ROWS, cols), F32),
        compiler_params=pltpu.CompilerParams(vmem_limit_bytes=VMEM_LIMIT_BYTES),
    )(cond_in, w_ada, b_loc)


def ada_backward(cond_in, g_loc, dmod_own, w_ada):
    cols = w_ada.shape[2]

    def body(x_ref, g_ref, own_ref, w_ref, gw_ref, dc_ref, gb_ref):
        x = x_ref[...]
        s = _silu(x)
        dcond = jnp.zeros((8, D), F32)
        for l in range(DEPTH):
            g_ctx = jnp.sum(g_ref[2 * l], axis=0, keepdims=True)
            g_rows = jnp.concatenate([g_ref[2 * l + 1], jnp.broadcast_to(g_ctx, (8, cols))], axis=0)
            keep = lax.broadcasted_iota(jnp.int32, (ADA_ROWS, cols), 0) <= N_DEV
            gw_ref[l] = _dg(s, jnp.where(keep, g_rows, 0.0), "tn")
            dcond = dcond + _dg(jnp.broadcast_to(g_ctx, (8, cols)), w_ref[l], "nt")
            gb_ref[l:l + 1, :] = own_ref[2 * l:2 * l + 1, :] + own_ref[2 * l + 1:2 * l + 2, :]
        xc = x[N_DEV:N_DEV + 1]
        sig = jax.nn.sigmoid(xc)
        dc_ref[...] = dcond[0:1] * (sig * (1.0 + xc * (1.0 - sig)))

    return pl.pallas_call(
        body, name="ada_backward",
        out_shape=[jax.ShapeDtypeStruct(w_ada.shape, F32), jax.ShapeDtypeStruct((1, D), F32),
                   jax.ShapeDtypeStruct((DEPTH, 6 * D), F32)],
        compiler_params=pltpu.CompilerParams(vmem_limit_bytes=VMEM_LIMIT_BYTES),
    )(cond_in, g_loc, dmod_own, w_ada)


WEIGHTS = ["c_ctx", "w_ada", "b_ada", "norm1_w", "norm2_w", "w_in", "b_gate", "mla_q_norm_a", "mla_w_qb",
           "mla_kv_norm_a", "mla_w_kvb", "mla_q_norm", "mla_k_norm", "gla_w_gk2", "gla_b_gk", "gla_o_norm",
           "ret_decay", "w_branch", "w_out", "w_ffn_in", "w_dw", "b_dw", "w_ffn_out"]
INPUTS = ["x", "c", "ctx"] + WEIGHTS + ["loss_target"] + ["m_" + n for n in WEIGHTS] + ["v_" + n for n in WEIGHTS]
BIG = {"w_in": 2, "mla_w_qb": 2, "mla_w_kvb": 2, "w_branch": 3, "w_out": 1, "w_ffn_in": 2, "w_ffn_out": 1}
SMALL_SHARDED = {"b_gate": 2, "gla_w_gk2": 3, "gla_b_gk": 2, "w_dw": 2}
SMALL = ["c_ctx", "b_ada", "norm1_w", "norm2_w", "b_gate", "mla_q_norm_a", "mla_kv_norm_a", "mla_q_norm", "mla_k_norm",
         "gla_w_gk2", "gla_b_gk", "gla_o_norm", "ret_decay", "w_dw", "b_dw"]


def _entry_rows(size, align):
    return -(-size // (PACK_W * align)) * align


def _pack(arrays, rows, dtype, align, lead=0):
    parts = []
    for a in arrays:
        head = a.shape[:lead]
        size = math.prod(a.shape[lead:])
        r = _entry_rows(size, align)
        if r * PACK_W == size:
            parts.append(a.astype(dtype).reshape(head + (r, PACK_W)))
        else:
            flat = jnp.pad(a.astype(dtype).reshape(head + (size,)), [(0, 0)] * lead + [(0, r * PACK_W - size)])
            parts.append(flat.reshape(head + (r, PACK_W)))
    used = sum(p.shape[lead] for p in parts)
    if rows > used:
        parts.append(jnp.zeros(parts[0].shape[:lead] + (rows - used, PACK_W), dtype))
    return jnp.concatenate(parts, axis=lead)


def _pack_rows(shapes, align, multiple):
    used = sum(_entry_rows(math.prod(s), align) for s in shapes)
    return -(-used // multiple) * multiple


def _unpack(pack, shapes, align):
    head = pack.shape[:-2]
    out, off = [], 0
    for shape in shapes:
        size = math.prod(shape)
        r = _entry_rows(size, align)
        block = lax.slice_in_dim(pack, off, off + r, axis=len(head))
        if r * PACK_W != size:
            block = block.reshape(head + (r * PACK_W,))[..., :size]
        out.append(block.reshape(head + tuple(shape)))
        off += r
    return out


def _join_shards(stacked, axis):
    moved = jnp.moveaxis(stacked, 0, axis)
    shape = list(moved.shape)
    return moved.reshape(shape[:axis] + [shape[axis] * shape[axis + 1]] + shape[axis + 2:])


def _split_shards(full, axis):
    shape = list(full.shape)
    split = full.reshape(shape[:axis] + [N_DEV, shape[axis] // N_DEV] + shape[axis + 1:])
    return jnp.moveaxis(split, axis, 0)


def _gather_shards(local, axes, dtype, rows_multiple, name):
    names = list(axes)
    shapes = [local[n].shape for n in names]
    rows = _pack_rows(shapes, rows_multiple, rows_multiple)
    gathered = all_gather(_pack([local[n] for n in names], rows, dtype, rows_multiple), name)
    stacked = _unpack(gathered.reshape(N_DEV, rows, PACK_W), shapes, rows_multiple)
    return {n: _join_shards(s, axes[n]).astype(F32) for n, s in zip(names, stacked)}


def kernel(*args):
    a = dict(zip(INPUTS, args))
    me = 4 * lax.axis_index("x") + 2 * lax.axis_index("y") + lax.axis_index("c")
    cols = a["w_ada"].shape[2]

    small_names = list(SMALL_SHARDED)
    small_local = [a[n].shape for n in small_names]
    first_rows = _pack_rows([a["c"].shape] + small_local, 8, 8)
    first = all_gather(_pack([a["c"]] + [a[n] for n in small_names], first_rows, F32, 8), "gather_small")
    first = _unpack(first.reshape(N_DEV, first_rows, PACK_W), [a["c"].shape] + small_local, 8)
    c_all = first[0][:, 0]

    cond_in = jnp.concatenate([c_all, a["c_ctx"][None], jnp.zeros((ADA_ROWS - N_DEV - 1, D), F32)], axis=0)
    b_loc = lax.dynamic_slice_in_dim(a["b_ada"], me * cols, cols, axis=1)[:, None, :]
    mod_loc = ada_forward(cond_in, a["w_ada"], b_loc)
    mod_all = all_gather(mod_loc.reshape(DEPTH * ADA_ROWS, cols), "gather_mod")
    mod_all = mod_all.reshape(N_DEV, DEPTH, ADA_ROWS, cols).transpose(1, 2, 0, 3).reshape(DEPTH, ADA_ROWS, 6, D)
    mod_me = lax.dynamic_index_in_dim(mod_all, me, axis=1, keepdims=False)
    mod = jnp.pad(jnp.stack([mod_all[:, N_DEV], mod_me], axis=1), ((0, 0), (0, 0), (0, 2), (0, 0)))

    w = _gather_shards(a, BIG, BF16, 16, "gather_weights")
    w.update({n: _join_shards(s, SMALL_SHARDED[n]) for n, s in zip(small_names, first[1:])})
    for n in SMALL:
        if n not in SMALL_SHARDED and n not in ("c_ctx", "b_ada"):
            w[n] = a[n]

    loss, (gw, gmod, gx) = jax.value_and_grad(local_loss, argnums=(0, 1, 2))(
        w, mod, a["x"][0], a["ctx"][0], a["loss_target"][0])
    loss = lax.psum(loss, ("x", "y", "c"))

    dmod_own = gmod[:, :, :6].reshape(2 * DEPTH, 6 * D)
    g_all = all_gather(jnp.pad(dmod_own, ((0, 8 - 2 * DEPTH), (0, 0))), "gather_dmod").reshape(N_DEV, 8, 6 * D)
    g_loc = lax.dynamic_slice_in_dim(g_all[:, :2 * DEPTH], me * cols, cols, axis=2).transpose(1, 0, 2)
    g_w_ada, g_c_ctx, g_b_ada = ada_backward(cond_in, g_loc, dmod_own, a["w_ada"])

    small_part = dict(gw, c_ctx=g_c_ctx, b_ada=g_b_ada)
    small_shapes = [a[n].shape if n not in SMALL_SHARDED else gw[n].shape for n in SMALL]
    rows = _pack_rows(small_shapes, 8, PACK_TILE)
    parts = all_gather(_pack([small_part[n] for n in SMALL], rows, F32, 8), "gather_small_grads")
    small_sum = _unpack(slab_sum(parts.reshape(N_DEV, rows, PACK_W), "sum_small_grads"), small_shapes, 8)
    g_small = {}
    for n, g in zip(SMALL, small_sum):
        if n in SMALL_SHARDED:
            ax = SMALL_SHARDED[n]
            g = lax.dynamic_slice_in_dim(g, me * a[n].shape[ax], a[n].shape[ax], axis=ax)
        g_small[n] = g

    big_rows = _pack_rows([a[n].shape for n in BIG], 16, PACK_TILE)
    slabs = _pack([_split_shards(gw[n], ax) for n, ax in BIG.items()], big_rows, BF16, 16, lead=1)
    by_core = slabs.reshape(N_CHIP, 2, big_rows, PACK_W)
    my_core = lax.axis_index("c")
    keep = lax.dynamic_index_in_dim(by_core, my_core, axis=1, keepdims=False)
    give = lax.dynamic_index_in_dim(by_core, 1 - my_core, axis=1, keepdims=False)
    pair_sum = pair_add(keep, pair_swap(give, "swap_grads"), "add_pair_grads")
    landed = chip_all_to_all(pair_sum, "scatter_grads")

    def update(names, g_slabs, rows, align, label):
        shapes = [a[n].shape for n in names]
        packs = [_pack([a[pre + n] for n in names], rows, F32, align) for pre in ("", "m_", "v_")]
        outs = adamw(g_slabs, *packs, label)
        return [dict(zip(names, _unpack(o, shapes, align))) for o in outs]

    res_big = update(list(BIG), landed, big_rows, 16, "adamw_big")
    ada_rows = _pack_rows([a["w_ada"].shape], 8, PACK_TILE)
    res_ada = update(["w_ada"], _pack([g_w_ada], ada_rows, F32, 8)[None], ada_rows, 8, "adamw_ada")
    small_rows = _pack_rows([a[n].shape for n in SMALL], 8, PACK_TILE)
    res_small = update(SMALL, _pack([g_small[n] for n in SMALL], small_rows, F32, 8)[None], small_rows, 8,
                       "adamw_small")

    outs = [loss, gx[None]]
    for k in range(4):
        merged = {**res_big[k], **res_ada[k], **res_small[k]}
        outs += [merged[n] for n in WEIGHTS]
    return tuple(outs)
```
